```python
import jax
import jax.numpy as jnp
from jax import lax
import numpy as np

D_MODEL = 1024
BATCH = 32
SEQ = 256
DEPTH = 2
DEC_BATCH = 4
DEC_SEQ = 2048
PAST_LEN = 512

GRID_W = 64
N_AB = (DEPTH + 1) // 2
N_C = DEPTH // 2
H_A = 8
KV_A = 2
G_A = H_A // KV_A
HD_A = 64
W_A = H_A * HD_A
WINDOW = 128
QBLK = 128
ROPE_BASE = 10000.0
ATTN_SCALE = HD_A ** -0.5
NEG_INF = -1e30
H_B = 8
HD_B = 64
W_B = H_B * HD_B
LORA_W = 64
LORA_A = 64
LORA_G = 128
GN_EPS = 64e-5
ATT_IN = W_A + 2 * KV_A * HD_A
RWKV_IN = 3 * W_B + LORA_W + LORA_A + LORA_G
IN_AB = ATT_IN + RWKV_IN
D_MIX_AB = W_A + W_B
H_C = 8
DK_C = 128
DV_C = 128
D_C = H_C * DV_C
CHUNK = 64
IN_C = 5 * D_C
N_EXPERTS = 64
TOP_K = 8
N_GROUPS = 8
TOPK_GROUPS = 4
D_EXPERT = 256
D_SHARED = 256
ROUTED_SCALE = 2.5
MOE_BLOCK = 128
EPS = 1e-6

kernel_name = 'hybrid_dit_attn_rwkv7_hgrn2_moe_step'


def _rms(x, g):
    xf = x.astype(jnp.float32)
    y = xf * lax.rsqrt(jnp.mean(xf * xf, axis=-1, keepdims=True) + EPS)
    return (y * g.astype(jnp.float32)).astype(x.dtype)


def _modulation(cvec, w, b):
    m = jax.nn.silu(cvec) @ w + b
    return jnp.split(m[:, None, :], 6, axis=-1)


def _centred_shift(x):
    xp = jnp.pad(x, ((0, 0), (1, 1), (0, 0)))
    return 0.5 * (xp[:, :-2] + xp[:, 2:])


def _axial_rope(x):
    T, hd = x.shape[1], x.shape[-1]
    rows = T // GRID_W
    row = jnp.repeat(jnp.arange(rows), GRID_W).astype(jnp.float32)
    col = (jnp.arange(rows * GRID_W) % GRID_W).astype(jnp.float32)
    d_axis = hd // 2
    inv = ROPE_BASE ** (-jnp.arange(0, d_axis, 2, dtype=jnp.float32) / d_axis)
    xf = x.astype(jnp.float32)

    def rot(seg, pos):
        ang = pos[:, None] * inv[None, :]
        cos = jnp.cos(ang)[:, None, :]
        sin = jnp.sin(ang)[:, None, :]
        a, b = jnp.split(seg, 2, axis=-1)
        return jnp.concatenate([a * cos - b * sin, a * sin + b * cos], axis=-1)

    out = jnp.concatenate([rot(xf[..., :d_axis], row), rot(xf[..., d_axis:], col)], axis=-1)
    return out.astype(x.dtype)


def _sink_attend(q, keys, vals, masks, sink):
    logits = []
    for k_, m_ in zip(keys, masks):
        s = jnp.einsum('bqkgd,bmkd->bkgqm', q, k_).astype(jnp.float32) * ATTN_SCALE
        if m_ is not None:
            s = jnp.where(m_, s, NEG_INF)
        logits.append(s)
    sink_col = jnp.broadcast_to(sink.astype(jnp.float32).reshape(1, KV_A, G_A, 1, 1), logits[0].shape[:-1] + (1,))
    p = jax.nn.softmax(jnp.concatenate([sink_col] + logits, axis=-1), axis=-1)[..., 1:]
    v = jnp.concatenate(vals, axis=1)
    return jnp.einsum('bkgqm,bmkd->bqkgd', p.astype(v.dtype), v)


def _context_attention(q, k, v, sink):
    B, T = q.shape[:2]
    nb = T // QBLK
    qb = jnp.moveaxis(q.reshape(B, nb, QBLK, KV_A, G_A, HD_A), 1, 0)
    ob = lax.map(lambda qq: _sink_attend(qq, [k], [v], [None], sink), qb)
    return jnp.moveaxis(ob, 0, 1).reshape(B, T, KV_A, G_A, HD_A)


def _latent_attention(q, k, v, k_ctx, v_ctx, sink):
    B, T = q.shape[:2]
    nb = T // QBLK
    qb = jnp.moveaxis(q.reshape(B, nb, QBLK, KV_A, G_A, HD_A), 1, 0)

    def band(a):
        ap = jnp.pad(a, ((0, 0), (QBLK, QBLK), (0, 0), (0, 0))).reshape(B, nb + 2, QBLK, KV_A, HD_A)
        return jnp.moveaxis(jnp.concatenate([ap[:, :-2], ap[:, 1:-1], ap[:, 2:]], axis=2), 1, 0)

    kb, vb = band(k), band(v)
    blk = jnp.arange(nb)[:, None, None]
    n = jnp.arange(QBLK)[None, :, None]
    m = jnp.arange(3 * QBLK)[None, None, :]
    j = (blk - 1) * QBLK + m
    i = blk * QBLK + n
    mask = (jnp.abs(j - i) <= WINDOW) & (j >= 0) & (j < T)
    ob = lax.map(lambda a: _sink_attend(a[0], [a[1], k_ctx], [a[2], v_ctx], [a[3], None], sink), (qb, kb, vb, mask))
    return jnp.moveaxis(ob, 0, 1).reshape(B, T, KV_A, G_A, HD_A)


def _rwkv7_scan(r, w, k, v, kk, a, s0):
    def step(S, inp):
        r_t, w_t, k_t, v_t, kk_t, a_t = inp
        sa = jnp.einsum('bhij,bhj->bhi', S, -kk_t)
        S = S * w_t[:, :, None, :] + sa[..., None] * (kk_t * a_t)[:, :, None, :] + v_t[..., None] * k_t[:, :, None, :]
        return S, jnp.einsum('bhij,bhj->bhi', S, r_t)

    xs = tuple(jnp.moveaxis(t, 1, 0) for t in (r, w, k, v, kk, a))
    s_fin, out = lax.scan(step, s0, xs)
    return jnp.moveaxis(out, 0, 1), s_fin


def _rwkv7_mix(pw, s0_f, s0_b, pr):
    B, T, _ = pw.shape
    pw = pw.astype(jnp.float32)
    r, k, v, wd, ad, gd = jnp.split(pw, [W_B, 2 * W_B, 3 * W_B, 3 * W_B + LORA_W, 3 * W_B + LORA_W + LORA_A], axis=-1)
    heads = lambda t: t.reshape(B, T, H_B, HD_B)
    kk = heads(k * pr['k_k'])
    kk = kk / jnp.maximum(jnp.sqrt(jnp.sum(kk * kk, axis=-1, keepdims=True)), 1e-12)
    g = jax.nn.sigmoid(gd) @ pr['g2']
    rh, vh = heads(r), heads(v)
    o_sum = 0.0
    bonus = 0.0
    finals = []
    for d, s0 in enumerate((s0_f, s0_b)):
        w_log = -jax.nn.softplus(-(pr['w0'][d] + jnp.tanh(wd) @ pr['w2'][d])) - 0.5
        decay = jnp.exp(-jnp.exp(w_log))
        a = jax.nn.sigmoid(pr['a0'][d] + ad @ pr['a2'][d])
        kd = heads(k * (1.0 + (a - 1.0) * pr['k_a']))
        seq = (rh, heads(decay), kd, vh, kk, heads(a))
        if d == 1:
            seq = tuple(t[:, ::-1] for t in seq)
        o, s_fin = _rwkv7_scan(*seq, s0.astype(jnp.float32))
        if d == 1:
            o = o[:, ::-1]
        o_sum = o_sum + o
        bonus = bonus + jnp.sum(rh * kd * pr['r_k'], axis=-1, keepdims=True) * vh
        finals.append(s_fin)
    mean = jnp.mean(o_sum, axis=-1, keepdims=True)
    var = jnp.mean(jnp.square(o_sum - mean), axis=-1, keepdims=True)
    gn = ((o_sum - mean) * lax.rsqrt(var + GN_EPS)).reshape(B, T, W_B) * pr['ln_w'] + pr['ln_b']
    out = (gn + bonus.reshape(B, T, W_B)) * g
    return out, finals[0], finals[1]


def _ab_mixer(h, pr, ctx):
    B, T, _ = h.shape
    p = h @ pr['w_in']
    q = _rms(p[..., :W_A].reshape(B, T, H_A, HD_A), pr['q_norm'])
    k = _rms(p[..., W_A:W_A + KV_A * HD_A].reshape(B, T, KV_A, HD_A), pr['k_norm'])
    v = p[..., W_A + KV_A * HD_A:ATT_IN].reshape(B, T, KV_A, HD_A)
    pw = p[..., ATT_IN:]
    pw = pw + (_centred_shift(pw) - pw) * pr['mu']
    if ctx is None:
        o_att = _context_attention(q.reshape(B, T, KV_A, G_A, HD_A), k, v, pr['sink'])
        zero = jnp.zeros((B, H_B, HD_B, HD_B), jnp.float32)
        o_rw, s_f, s_b = _rwkv7_mix(pw, zero, zero, pr)
    else:
        k_ctx, v_ctx, s0_f, s0_b = ctx
        q = _axial_rope(q)
        k = _axial_rope(k)
        o_att = _latent_attention(q.reshape(B, T, KV_A, G_A, HD_A), k, v, k_ctx, v_ctx, pr['sink'])
        o_rw, s_f, s_b = _rwkv7_mix(pw, s0_f, s0_b, pr)
    y = jnp.concatenate([o_att.reshape(B, T, W_A), o_rw.astype(h.dtype)], axis=-1) @ pr['w_out']
    return y, (k, v, s_f, s_b)


def _gla_chunked(q, k, v, logf, s0):
    B, T, H, DK = q.shape
    n = T // CHUNK
    c = lambda t: t.reshape(B, n, CHUNK, H, t.shape[-1])
    qc, kc, vc, gc = c(q), c(k), c(v), c(logf)
    b = jnp.cumsum(gc, axis=2)
    b_last = b[:, :, -1:]
    q_in = qc * jnp.exp(b)
    k_in = kc * jnp.exp(-b)
    k_out = kc * jnp.exp(b_last - b)
    causal = jnp.tril(jnp.ones((CHUNK, CHUNK), bool))
    att = jnp.where(causal, jnp.einsum('bnthd,bnshd->bnhts', q_in, k_in), 0.0)
    o_intra = jnp.einsum('bnhts,bnshv->bnthv', att, vc)
    kv = jnp.einsum('bnshd,bnshv->bnhdv', k_out, vc)
    dec = jnp.exp(b_last[:, :, 0])

    def step(S, inp):
        d_c, kv_c = inp
        return d_c[..., None] * S + kv_c, S

    s_fin, s_start = lax.scan(step, s0, (jnp.moveaxis(dec, 1, 0), jnp.moveaxis(kv, 1, 0)))
    o_inter = jnp.einsum('bnthd,bnhdv->bnthv', q_in, jnp.moveaxis(s_start, 0, 1))
    return (o_intra + o_inter).reshape(B, T, H, v.shape[-1]), s_fin


def _c_mixer(h, pr, ctx):
    B, T, _ = h.shape
    p = (h @ pr['w_in']).astype(jnp.float32)
    q, f_f, f_b, i, g = jnp.split(p, 5, axis=-1)
    qh = jax.nn.silu(q).reshape(B, T, H_C, DK_C)
    ih = i.reshape(B, T, H_C, DV_C)
    lb = pr['lb']
    if ctx is None:
        zero = jnp.zeros((B, H_C, DK_C, DV_C), jnp.float32)
        s0s = (zero, zero)
    else:
        s0s = ctx
    o_sum = 0.0
    finals = []
    for d, (fr, s0) in enumerate(((f_f, s0s[0]), (f_b, s0s[1]))):
        f = lb + (1.0 - lb) * jax.nn.sigmoid(fr)
        seq = (qh, (1.0 - f).reshape(B, T, H_C, DK_C), ih, jnp.log(f).reshape(B, T, H_C, DK_C))
        if d == 1:
            seq = tuple(t[:, ::-1] for t in seq)
        o, s_fin = _gla_chunked(*seq, s0.astype(jnp.float32))
        if d == 1:
            o = o[:, ::-1]
        o_sum = o_sum + o
        finals.append(s_fin)
    o = o_sum * lax.rsqrt(jnp.mean(o_sum * o_sum, axis=-1, keepdims=True) + EPS)
    o = o.reshape(B, T, D_C) * pr['norm_g'] * jax.nn.silu(g)
    y = o.astype(h.dtype) @ pr['w_out']
    return y, (finals[0], finals[1])


def _moe(h, pr):
    B, T, D = h.shape
    x = h.reshape(B * T, D)
    n = x.shape[0]
    scores = jax.nn.sigmoid((x @ pr['router']).astype(jnp.float32))
    biased = scores + pr['bias'].astype(jnp.float32)
    grp = biased.reshape(n, N_GROUPS, N_EXPERTS // N_GROUPS)
    grp_score = jnp.sum(lax.top_k(grp, 2)[0], axis=-1)
    _, gidx = lax.top_k(grp_score, TOPK_GROUPS)
    gmask = jnp.sum(jax.nn.one_hot(gidx, N_GROUPS, dtype=jnp.float32), axis=1) > 0
    emask = jnp.repeat(gmask, N_EXPERTS // N_GROUPS, axis=-1)
    _, eidx = lax.top_k(jnp.where(emask, biased, -jnp.inf), TOP_K)
    w = jnp.take_along_axis(scores, eidx, axis=-1)
    w = w / jnp.sum(w, axis=-1, keepdims=True) * ROUTED_SCALE
    gates = jnp.einsum('nk,nke->ne', w, jax.nn.one_hot(eidx, N_EXPERTS, dtype=jnp.float32))
    nb = n // MOE_BLOCK

    def block(args):
        xb, gb = args
        hg = jnp.einsum('nd,edf->nef', xb, pr['wg'])
        hu = jnp.einsum('nd,edf->nef', xb, pr['wu'])
        act = jax.nn.silu(hg) * hu * gb[..., None].astype(xb.dtype)
        return jnp.einsum('nef,efd->nd', act, pr['wd'])

    routed = lax.map(block, (x.reshape(nb, MOE_BLOCK, D), gates.reshape(nb, MOE_BLOCK, N_EXPERTS))).reshape(n, D)
    shared = (jax.nn.silu(x @ pr['sg']) * (x @ pr['su'])) @ pr['sd']
    return (routed + shared).reshape(B, T, D)


def setup_inputs(seed: int = 0) -> dict:
    key = jax.random.key(seed)
    keys = iter(jax.random.split(key, 64))
    f32 = jnp.float32

    def nrm(shape, scale=1.0):
        return scale * jax.random.normal(next(keys), shape, f32)

    def gain(shape):
        return 1.0 + 0.02 * jax.random.normal(next(keys), shape, f32)

    def unif(shape, lo, hi):
        return jax.random.uniform(next(keys), shape, f32, lo, hi)

    return {
        'x_prompt': nrm((BATCH, SEQ, D_MODEL)),
        'x_sample': nrm((DEC_BATCH, DEC_SEQ, D_MODEL)),
        'c': nrm((DEC_BATCH, D_MODEL)),
        'c_ctx': nrm((D_MODEL,)),
        'cache_attn_k': nrm((DEC_BATCH, N_AB, PAST_LEN, KV_A, HD_A)),
        'cache_attn_v': nrm((DEC_BATCH, N_AB, PAST_LEN, KV_A, HD_A)),
        'state_rwkv_fwd': nrm((DEC_BATCH, N_AB, H_B, HD_B, HD_B), 0.5),
        'state_rwkv_bwd': nrm((DEC_BATCH, N_AB, H_B, HD_B, HD_B), 0.5),
        'state_hgrn_fwd': nrm((DEC_BATCH, N_C, H_C, DK_C, DV_C), 0.5),
        'state_hgrn_bwd': nrm((DEC_BATCH, N_C, H_C, DK_C, DV_C), 0.5),
        'norm1_g': gain((DEPTH, D_MODEL)),
        'norm2_g': gain((DEPTH, D_MODEL)),
        'mod_w': nrm((DEPTH, D_MODEL, 6 * D_MODEL), 0.5 * D_MODEL ** -0.5),
        'mod_b': nrm((DEPTH, 6 * D_MODEL), 0.02),
        'ab_w_in': nrm((N_AB, D_MODEL, IN_AB), D_MODEL ** -0.5),
        'ab_w_out': nrm((N_AB, D_MIX_AB, D_MODEL), D_MIX_AB ** -0.5),
        'attn_q_norm': gain((N_AB, HD_A)),
        'attn_k_norm': gain((N_AB, HD_A)),
        'attn_sink': nrm((N_AB, H_A), 0.5),
        'rwkv_mu': unif((N_AB, RWKV_IN), 0.0, 1.0),
        'rwkv_w0': unif((N_AB, 2, W_B), -6.0, -1.0),
        'rwkv_w2': nrm((N_AB, 2, LORA_W, W_B), LORA_W ** -0.5),
        'rwkv_a0': nrm((N_AB, 2, W_B), 0.1),
        'rwkv_a2': nrm((N_AB, 2, LORA_A, W_B), LORA_A ** -0.5),
        'rwkv_g2': nrm((N_AB, LORA_G, W_B), LORA_G ** -0.5),
        'rwkv_k_k': 0.85 + nrm((N_AB, W_B), 0.05),
        'rwkv_k_a': 1.0 + nrm((N_AB, W_B), 0.05),
        'rwkv_r_k': nrm((N_AB, H_B, HD_B), 0.1),
        'rwkv_ln_w': gain((N_AB, W_B)),
        'rwkv_ln_b': nrm((N_AB, W_B), 0.02),
        'hgrn_w_in': nrm((N_C, D_MODEL, IN_C), D_MODEL ** -0.5),
        'hgrn_w_out': nrm((N_C, D_C, D_MODEL), D_C ** -0.5),
        'hgrn_lower_bounds': nrm((DEPTH, D_C), 0.1),
        'hgrn_norm_g': gain((N_C, D_C)),
        'moe_router': nrm((DEPTH, D_MODEL, N_EXPERTS), D_MODEL ** -0.5),
        'moe_bias': nrm((DEPTH, N_EXPERTS), 0.01),
        'moe_w_gate': nrm((DEPTH, N_EXPERTS, D_MODEL, D_EXPERT), D_MODEL ** -0.5),
        'moe_w_up': nrm((DEPTH, N_EXPERTS, D_MODEL, D_EXPERT), D_MODEL ** -0.5),
        'moe_w_down': nrm((DEPTH, N_EXPERTS, D_EXPERT, D_MODEL), D_EXPERT ** -0.5),
        'moe_shared_gate': nrm((DEPTH, D_MODEL, D_SHARED), D_MODEL ** -0.5),
        'moe_shared_up': nrm((DEPTH, D_MODEL, D_SHARED), D_MODEL ** -0.5),
        'moe_shared_down': nrm((DEPTH, D_SHARED, D_MODEL), D_SHARED ** -0.5),
    }


def reference(x_prompt, x_sample, c, c_ctx,
              cache_attn_k, cache_attn_v, state_rwkv_fwd, state_rwkv_bwd, state_hgrn_fwd, state_hgrn_bwd,
              norm1_g, norm2_g, mod_w, mod_b,
              ab_w_in, ab_w_out, attn_q_norm, attn_k_norm, attn_sink,
              rwkv_mu, rwkv_w0, rwkv_w2, rwkv_a0, rwkv_a2, rwkv_g2, rwkv_k_k, rwkv_k_a, rwkv_r_k, rwkv_ln_w, rwkv_ln_b,
              hgrn_w_in, hgrn_w_out, hgrn_lower_bounds, hgrn_norm_g,
              moe_router, moe_bias, moe_w_gate, moe_w_up, moe_w_down, moe_shared_gate, moe_shared_up, moe_shared_down):
    sm = jax.nn.softmax(hgrn_lower_bounds.astype(jnp.float32), axis=0)
    lower_bounds = jnp.cumsum(sm, axis=0) - sm[0]

    def ab_params(j):
        return {'w_in': ab_w_in[j], 'w_out': ab_w_out[j], 'q_norm': attn_q_norm[j], 'k_norm': attn_k_norm[j],
                'sink': attn_sink[j], 'mu': rwkv_mu[j], 'w0': rwkv_w0[j], 'w2': rwkv_w2[j], 'a0': rwkv_a0[j],
                'a2': rwkv_a2[j], 'g2': rwkv_g2[j], 'k_k': rwkv_k_k[j], 'k_a': rwkv_k_a[j], 'r_k': rwkv_r_k[j],
                'ln_w': rwkv_ln_w[j], 'ln_b': rwkv_ln_b[j]}

    def c_params(l):
        j = l // 2
        return {'w_in': hgrn_w_in[j], 'w_out': hgrn_w_out[j], 'lb': lower_bounds[l], 'norm_g': hgrn_norm_g[j]}

    def moe_params(l):
        return {'router': moe_router[l], 'bias': moe_bias[l], 'wg': moe_w_gate[l], 'wu': moe_w_up[l],
                'wd': moe_w_down[l], 'sg': moe_shared_gate[l], 'su': moe_shared_up[l], 'sd': moe_shared_down[l]}

    def run_layer(x, l, cvec, ctx):
        sh1, sc1, g1, sh2, sc2, g2 = _modulation(cvec, mod_w[l], mod_b[l])
        hmix = _rms(x, norm1_g[l]) * (1.0 + sc1) + sh1
        if l % 2 == 0:
            y, st = _ab_mixer(hmix, ab_params(l // 2), ctx)
        else:
            y, st = _c_mixer(hmix, c_params(l), ctx)
        x = x + g1 * y
        hffn = _rms(x, norm2_g[l]) * (1.0 + sc2) + sh2
        x = x + g2 * _moe(hffn, moe_params(l))
        return x, st

    xp = x_prompt
    cctx = c_ctx[None, :]
    att_k, att_v, rw_f, rw_b, hg_f, hg_b = [], [], [], [], [], []
    for l in range(DEPTH):
        xp, st = run_layer(xp, l, cctx, None)
        if l % 2 == 0:
            att_k.append(st[0])
            att_v.append(st[1])
            rw_f.append(st[2])
            rw_b.append(st[3])
        else:
            hg_f.append(st[0])
            hg_b.append(st[1])

    xs = x_sample
    for l in range(DEPTH):
        j = l // 2
        if l % 2 == 0:
            ctx = (cache_attn_k[:, j], cache_attn_v[:, j], state_rwkv_fwd[:, j], state_rwkv_bwd[:, j])
        else:
            ctx = (state_hgrn_fwd[:, j], state_hgrn_bwd[:, j])
        xs, _ = run_layer(xs, l, c, ctx)

    return (xp, xs, jnp.stack(att_k, axis=1), jnp.stack(att_v, axis=1), jnp.stack(rw_f, axis=1), jnp.stack(rw_b, axis=1), jnp.stack(hg_f, axis=1), jnp.stack(hg_b, axis=1))
```

```python
import functools

import numpy as np
import jax
import jax.numpy as jnp
from jax import lax
from jax.experimental import pallas as pl
from jax.experimental.pallas import tpu as pltpu

F32 = jnp.float32
BF16 = jnp.bfloat16

D_MODEL = 1024
GRID_W = 64
H_A = 8
KV_A = 2
G_A = H_A // KV_A
HD_A = 64
W_A = H_A * HD_A
WINDOW = 128
QBLK = 128
ROPE_BASE = 10000.0
ATTN_SCALE = HD_A ** -0.5
NEG_INF = -1e30
H_B = 8
HD_B = 64
W_B = H_B * HD_B
LORA_W = 64
LORA_A = 64
LORA_G = 128
GN_EPS = 64e-5
ATT_IN = W_A + 2 * KV_A * HD_A
RWKV_IN = 3 * W_B + LORA_W + LORA_A + LORA_G
IN_AB = ATT_IN + RWKV_IN
H_C = 8
DK_C = 128
DV_C = 128
D_C = H_C * DV_C
CHUNK = 64
IN_C = 5 * D_C
N_EXPERTS = 64
TOP_K = 8
N_GROUPS = 8
TOPK_GROUPS = 4
D_EXPERT = 256
ROUTED_SCALE = 2.5
EPS = 1e-6

LANES = 128
SUBLANES = 8
VMEM_LIMIT = 52 * 1024 * 1024

TM = 256
RW_TB = 128
RW_BB = 2
HG_TB = 256
MOE_TM = 1024


def _cparams(n_axes):
    return pltpu.CompilerParams(dimension_semantics=("arbitrary",) * n_axes,
                                vmem_limit_bytes=VMEM_LIMIT)


def _bf(x):
    return x.astype(BF16)


def _split2(x):
    hi = lax.bitcast_convert_type(
        lax.bitcast_convert_type(x, jnp.uint32) & jnp.uint32(0xFFFF0000), F32)
    return hi, x - hi


def _seg_sum(x, ones2):
    hi, lo = _split2(x)
    return jnp.dot(jnp.concatenate([_bf(hi), _bf(lo)], axis=1), ones2,
                   preferred_element_type=F32)


def _dot_nt(a, b):
    return lax.dot_general(a, b, (((1,), (1,)), ((), ())), preferred_element_type=F32)


def _dot_tn(a, b):
    return lax.dot_general(a, b, (((0,), (0,)), ((), ())), preferred_element_type=F32)


def _sigmoid(x):
    return 1.0 / (1.0 + jnp.exp(-x))


def _silu(x):
    return x * _sigmoid(x)


def _block_ones(width, seg):
    idx = np.arange(width) // seg
    bd = (idx[:, None] == idx[None, :]).astype(np.float32)
    return jnp.asarray(np.concatenate([bd, bd], axis=0), dtype=BF16)


def _mod_kernel(c_ref, w_ref, b_ref, o_ref):
    s = _silu(c_ref[...])
    o_ref[0] = jnp.dot(_bf(s), _bf(w_ref[0]), preferred_element_type=F32) + b_ref[0]


def _modulation(cvecs, mod_w, mod_b):
    depth = mod_w.shape[0]
    n_col = 6 * D_MODEL // D_MODEL
    return pl.pallas_call(
        _mod_kernel,
        grid=(depth, n_col),
        in_specs=[pl.BlockSpec((SUBLANES, D_MODEL), lambda l, j: (0, 0)),
                  pl.BlockSpec((1, D_MODEL, D_MODEL), lambda l, j: (l, 0, j)),
                  pl.BlockSpec((1, 1, D_MODEL), lambda l, j: (l, 0, j))],
        out_specs=pl.BlockSpec((1, SUBLANES, D_MODEL), lambda l, j: (l, 0, j)),
        out_shape=jax.ShapeDtypeStruct((depth, SUBLANES, 6 * D_MODEL), F32),
        compiler_params=_cparams(2),
        name="modulation",
    )(cvecs, mod_w, mod_b.reshape(depth, 1, 6 * D_MODEL))


class _Rows:
    def __init__(self, n_ctx, n_lat, lat_seq):
        self.n_ctx, self.n_lat, self.lat_seq = n_ctx, n_lat, lat_seq
        self.n = n_ctx + n_lat

    def mod_row(self, i, tm):
        nctx_blk = self.n_ctx // tm
        per_seq = self.lat_seq // tm
        return jnp.where(i < nctx_blk, 0, 1 + (i - nctx_blk) // per_seq)

    def mod_spec(self, layer, chunk, tm):
        return pl.BlockSpec((None, None, 1, D_MODEL),
                            lambda i, *_: (layer, self.mod_row(i, tm), 0, chunk))


def _rms_mod(x, g, sc, sh):
    ms = jnp.mean(x * x, axis=-1, keepdims=True)
    return x * lax.rsqrt(ms + EPS) * g * (1.0 + sc) + sh


def _inproj_kernel(splits, x_ref, g_ref, sh_ref, sc_ref, w_ref, *o_refs):
    h = _rms_mod(x_ref[...], g_ref[...], sc_ref[...], sh_ref[...])
    p = jnp.dot(_bf(h), w_ref[...], preferred_element_type=F32)
    lo = 0
    for o_ref, width in zip(o_refs, splits):
        o_ref[...] = p[:, lo:lo + width]
        lo += width


def _inproj(rows, x, g, mod4, layer, w_bf, splits):
    n_out = w_bf.shape[1]
    return pl.pallas_call(
        functools.partial(_inproj_kernel, splits),
        grid=(rows.n // TM,),
        in_specs=[pl.BlockSpec((TM, D_MODEL), lambda i: (i, 0)),
                  pl.BlockSpec((1, D_MODEL), lambda i: (0, 0)),
                  rows.mod_spec(layer, 0, TM),
                  rows.mod_spec(layer, 1, TM),
                  pl.BlockSpec((D_MODEL, n_out), lambda i: (0, 0))],
        out_specs=[pl.BlockSpec((TM, wd), lambda i: (i, 0)) for wd in splits],
        out_shape=[jax.ShapeDtypeStruct((rows.n, wd), F32) for wd in splits],
        compiler_params=_cparams(1),
        name=f"inproj{layer}",
    )(x, g.reshape(1, D_MODEL), mod4, mod4, w_bf)


def _head_rms(x, gain_t, ones2):
    ms = _seg_sum(x * x, ones2) * (1.0 / HD_A)
    return x * lax.rsqrt(ms + EPS) * gain_t


def _sink_softmax_pv(parts, sink):
    m = jnp.maximum(functools.reduce(jnp.maximum, [jnp.max(s, axis=-1, keepdims=True) for s, _ in parts]), sink)
    den = jnp.exp(sink - m)
    acc = None
    for s, v in parts:
        p = jnp.exp(s - m)
        den = den + jnp.sum(p, axis=-1, keepdims=True)
        pv = jnp.dot(_bf(p), v, preferred_element_type=F32)
        acc = pv if acc is None else acc + pv
    return acc / den


def _ctx_attn_kernel(p_ref, qg_ref, kg_ref, sink_ref, ones_q_ref, ones_k_ref, o_ref, k_ref, v_ref):
    p = p_ref[...]
    q = _head_rms(p[:, :W_A], qg_ref[...], ones_q_ref[...]) * ATTN_SCALE
    k = _head_rms(p[:, W_A:W_A + KV_A * HD_A], kg_ref[...], ones_k_ref[...])
    v = p[:, W_A + KV_A * HD_A:ATT_IN]
    k_ref[0] = k
    v_ref[0] = v
    qb, kb, vb = _bf(q), _bf(k), _bf(v)
    outs = []
    for h in range(H_A):
        j = h // G_A
        s = _dot_nt(qb[:, h * HD_A:(h + 1) * HD_A], kb[:, j * HD_A:(j + 1) * HD_A])
        outs.append(_sink_softmax_pv([(s, vb[:, j * HD_A:(j + 1) * HD_A])], sink_ref[h]))
    o_ref[...] = jnp.concatenate(outs, axis=1)


def _ctx_attention(p_att, n_seq, seq, qg_t, kg_t, sink, ones_q, ones_k):
    kv_w = KV_A * HD_A
    return pl.pallas_call(
        _ctx_attn_kernel,
        grid=(n_seq,),
        in_specs=[pl.BlockSpec((seq, ATT_IN), lambda b: (b, 0)),
                  pl.BlockSpec((1, W_A), lambda b: (0, 0)),
                  pl.BlockSpec((1, kv_w), lambda b: (0, 0)),
                  pl.BlockSpec(memory_space=pltpu.SMEM),
                  pl.BlockSpec(ones_q.shape, lambda b: (0, 0)),
                  pl.BlockSpec(ones_k.shape, lambda b: (0, 0))],
        out_specs=[pl.BlockSpec((seq, W_A), lambda b: (b, 0)),
                   pl.BlockSpec((1, seq, kv_w), lambda b: (b, 0, 0)),
                   pl.BlockSpec((1, seq, kv_w), lambda b: (b, 0, 0))],
        out_shape=[jax.ShapeDtypeStruct((n_seq * seq, W_A), F32),
                   jax.ShapeDtypeStruct((n_seq, seq, kv_w), F32),
                   jax.ShapeDtypeStruct((n_seq, seq, kv_w), F32)],
        compiler_params=_cparams(1),
        name="ctx_attention",
    )(p_att, qg_t, kg_t, sink, ones_q, ones_k)


def _rope(x, cos_t, sin_t):
    lane = lax.broadcasted_iota(jnp.int32, cos_t.shape, 1)
    low = (lane % 32) < 16
    outs = []
    for s in range(x.shape[1] // LANES):
        xs = x[:, s * LANES:(s + 1) * LANES]
        partner = jnp.where(low, pltpu.roll(xs, LANES - 16, 1), pltpu.roll(xs, 16, 1))
        outs.append(xs * cos_t + partner * sin_t)
    return outs[0] if len(outs) == 1 else jnp.concatenate(outs, axis=1)


def _lat_attn_kernel(seq, p_ref, qg_ref, kg_ref, sink_ref, ones_q_ref, ones_k_ref, cos_ref, sin_ref,
                     kc_ref, vc_ref, o_ref, q_scr, k_scr, v_scr):
    kv_w = KV_A * HD_A
    p = p_ref[...]
    q = _head_rms(p[:, :W_A], qg_ref[...], ones_q_ref[...])
    k = _head_rms(p[:, W_A:W_A + kv_w], kg_ref[...], ones_k_ref[...])
    q_scr[...] = _bf(_rope(q, cos_ref[...], sin_ref[...]) * ATTN_SCALE)
    k_scr[...] = _bf(_rope(k, cos_ref[...], sin_ref[...]))
    v_scr[...] = _bf(p[:, W_A + kv_w:ATT_IN])
    kc = _bf(kc_ref[0])
    vc = _bf(vc_ref[0])
    n_local = 3 * QBLK

    def block(i, carry):
        q0 = pl.multiple_of(i * QBLK, QBLK)
        start = pl.multiple_of(jnp.clip((i - 1) * QBLK, 0, seq - n_local), QBLK)
        qb = q_scr[pl.ds(q0, QBLK), :]
        kl = k_scr[pl.ds(start, n_local), :]
        vl = v_scr[pl.ds(start, n_local), :]
        ipos = q0 + lax.broadcasted_iota(jnp.int32, (QBLK, n_local), 0)
        jpos = start + lax.broadcasted_iota(jnp.int32, (QBLK, n_local), 1)
        band = jnp.abs(jpos - ipos) <= WINDOW
        outs = []
        for h in range(H_A):
            j = h // G_A
            qh = qb[:, h * HD_A:(h + 1) * HD_A]
            s_loc = jnp.where(band, _dot_nt(qh, kl[:, j * HD_A:(j + 1) * HD_A]), NEG_INF)
            s_ctx = _dot_nt(qh, kc[:, j * HD_A:(j + 1) * HD_A])
            outs.append(_sink_softmax_pv([(s_loc, vl[:, j * HD_A:(j + 1) * HD_A]),
                                          (s_ctx, vc[:, j * HD_A:(j + 1) * HD_A])], sink_ref[h]))
        o_ref[pl.ds(q0, QBLK), :] = jnp.concatenate(outs, axis=1)
        return carry

    lax.fori_loop(0, seq // QBLK, block, 0)


def _lat_attention(p_att, row_blk0, n_seq, seq, qg_t, kg_t, sink, ones_q, ones_k, cos_t, sin_t, kc, vc):
    kv_w = KV_A * HD_A
    past = kc.shape[1]
    return pl.pallas_call(
        functools.partial(_lat_attn_kernel, seq),
        grid=(n_seq,),
        in_specs=[pl.BlockSpec((seq, ATT_IN), lambda b: (row_blk0 + b, 0)),
                  pl.BlockSpec((1, W_A), lambda b: (0, 0)),
                  pl.BlockSpec((1, kv_w), lambda b: (0, 0)),
                  pl.BlockSpec(memory_space=pltpu.SMEM),
                  pl.BlockSpec(ones_q.shape, lambda b: (0, 0)),
                  pl.BlockSpec(ones_k.shape, lambda b: (0, 0)),
                  pl.BlockSpec((seq, LANES), lambda b: (0, 0)),
                  pl.BlockSpec((seq, LANES), lambda b: (0, 0)),
                  pl.BlockSpec((1, past, kv_w), lambda b: (b, 0, 0)),
                  pl.BlockSpec((1, past, kv_w), lambda b: (b, 0, 0))],
        out_specs=pl.BlockSpec((seq, W_A), lambda b: (b, 0)),
        out_shape=jax.ShapeDtypeStruct((n_seq * seq, W_A), F32),
        scratch_shapes=[pltpu.VMEM((seq, W_A), BF16), pltpu.VMEM((seq, kv_w), BF16),
                        pltpu.VMEM((seq, kv_w), BF16)],
        compiler_params=_cparams(1),
        name="lat_attention",
    )(p_att, qg_t, kg_t, sink, ones_q, ones_k, cos_t, sin_t, kc, vc)


def _rope_tables(seq):
    pos = np.arange(seq)
    row = (pos // GRID_W).astype(np.float32)
    col = (pos % GRID_W).astype(np.float32)
    d_axis = HD_A // 2
    inv = (ROPE_BASE ** (-np.arange(0, d_axis, 2, dtype=np.float32) / d_axis)).astype(np.float32)
    cos_h = np.zeros((seq, HD_A), np.float32)
    sin_h = np.zeros((seq, HD_A), np.float32)
    for seg, p_ in enumerate((row, col)):
        ang = (p_[:, None] * inv[None, :]).astype(np.float32)
        c, s = np.cos(ang), np.sin(ang)
        base = seg * d_axis
        cos_h[:, base:base + d_axis // 2] = c
        cos_h[:, base + d_axis // 2:base + d_axis] = c
        sin_h[:, base:base + d_axis // 2] = -s
        sin_h[:, base + d_axis // 2:base + d_axis] = s
    rep = LANES // HD_A
    return jnp.asarray(np.tile(cos_h, (1, rep))), jnp.asarray(np.tile(sin_h, (1, rep)))


def _rwkv_prep_kernel(rows, x_ref, prev_ref, next_ref, mu_ref, kk_ref, ka_ref, rk_ref, w0_ref, w2_ref,
                      a0_ref, a2_ref, g2_ref, ones_ref,
                      nkk_ref, r_ref, v_ref, g_ref, bonus_ref,
                      wf_ref, kaf_ref, kdf_ref, wb_ref, kab_ref, kdb_ref):
    i = pl.program_id(0)
    nctx_blk = rows.n_ctx // TM
    per_seq = rows.lat_seq // TM
    is_ctx = i < nctx_blk
    first = jnp.logical_or(is_ctx, (i - nctx_blk) % per_seq == 0)
    last = jnp.logical_or(is_ctx, (i - nctx_blk) % per_seq == per_seq - 1)
    x = x_ref[...]
    ridx = lax.broadcasted_iota(jnp.int32, x.shape, 0)
    prev_row = jnp.where(first, 0.0, prev_ref[SUBLANES - 1:SUBLANES, :])
    next_row = jnp.where(last, 0.0, next_ref[0:1, :])
    xm1 = jnp.where(ridx == 0, prev_row, pltpu.roll(x, 1, 0))
    xp1 = jnp.where(ridx == TM - 1, next_row, pltpu.roll(x, TM - 1, 0))
    pw = x + (0.5 * (xm1 + xp1) - x) * mu_ref[...]

    r = pw[:, 0:W_B]
    k = pw[:, W_B:2 * W_B]
    v = pw[:, 2 * W_B:3 * W_B]
    wd = pw[:, 3 * W_B:3 * W_B + LORA_W]
    ad = pw[:, 3 * W_B + LORA_W:3 * W_B + LORA_W + LORA_A]
    gd = pw[:, 3 * W_B + LORA_W + LORA_A:]
    ones2 = ones_ref[...]

    kk = k * kk_ref[...]
    kk = kk / jnp.maximum(jnp.sqrt(_seg_sum(kk * kk, ones2)), 1e-12)
    nkk_ref[...] = -kk
    r_ref[...] = r
    v_ref[...] = v
    g_ref[...] = jnp.dot(_bf(_sigmoid(gd)), g2_ref[...], preferred_element_type=F32)
    tw = _bf(jnp.tanh(wd))
    adb = _bf(ad)
    bonus = jnp.zeros_like(r)
    for d, (w_o, ka_o, kd_o) in enumerate(((wf_ref, kaf_ref, kdf_ref), (wb_ref, kab_ref, kdb_ref))):
        z = -(w0_ref[d:d + 1, :] + jnp.dot(tw, w2_ref[d], preferred_element_type=F32))
        softplus = jnp.maximum(z, 0.0) + jnp.log(1.0 + jnp.exp(-jnp.abs(z)))
        w_o[...] = jnp.exp(-jnp.exp(-softplus - 0.5))
        a = _sigmoid(a0_ref[d:d + 1, :] + jnp.dot(adb, a2_ref[d], preferred_element_type=F32))
        kd = k * (1.0 + (a - 1.0) * ka_ref[...])
        ka_o[...] = kk * a
        kd_o[...] = kd
        bonus = bonus + _seg_sum(r * kd * rk_ref[...], ones2) * v
    bonus_ref[...] = bonus


def _rwkv_prep(rows, p_rw, pr, ones_b):
    n = rows.n
    n_halo = n // SUBLANES
    blk_halo = TM // SUBLANES
    row = lambda a: a.reshape(1, -1)
    full = lambda a: pl.BlockSpec(a.shape, lambda i: (0,) * a.ndim)
    consts = [row(pr['mu']), row(pr['k_k']), row(pr['k_a']), row(pr['r_k']), pr['w0'], _bf(pr['w2']),
              pr['a0'], _bf(pr['a2']), _bf(pr['g2']), ones_b]
    outs = pl.pallas_call(
        functools.partial(_rwkv_prep_kernel, rows),
        grid=(n // TM,),
        in_specs=[pl.BlockSpec((TM, RWKV_IN), lambda i: (i, 0)),
                  pl.BlockSpec((SUBLANES, RWKV_IN), lambda i: (jnp.maximum(i * blk_halo - 1, 0), 0)),
                  pl.BlockSpec((SUBLANES, RWKV_IN), lambda i: (jnp.minimum((i + 1) * blk_halo, n_halo - 1), 0))]
                 + [full(a) for a in consts],
        out_specs=[pl.BlockSpec((TM, W_B), lambda i: (i, 0))] * 11,
        out_shape=[jax.ShapeDtypeStruct((n, W_B), F32)] * 11,
        compiler_params=_cparams(1),
        name="rwkv_prep",
    )(p_rw, p_rw, p_rw, *consts)
    names = ('nkk', 'r', 'v', 'g', 'bonus', 'w_f', 'ka_f', 'kd_f', 'w_b', 'ka_b', 'kd_b')
    return dict(zip(names, outs))


def _rwkv_scan_kernel(n_tb, nkkf_ref, rf_ref, vf_ref, wf_ref, kaf_ref, kdf_ref,
                      nkkb_ref, rb_ref, vb_ref, wb_ref, kab_ref, kdb_ref,
                      s0f_ref, s0b_ref, ones_ref,
                      of_ref, ob_ref, sff_ref, sfb_ref, s_scr, vt_scr):
    tb = pl.program_id(1)
    n_pair = H_B // 2
    half = RW_TB // 2
    dirs = ((nkkf_ref, rf_ref, vf_ref, wf_ref, kaf_ref, kdf_ref, of_ref, False),
            (nkkb_ref, rb_ref, vb_ref, wb_ref, kab_ref, kdb_ref, ob_ref, True))

    @pl.when(tb == 0)
    def _():
        s_scr[0] = s0f_ref[...]
        s_scr[1] = s0b_ref[...]

    lane = lax.broadcasted_iota(jnp.int32, (HD_B, LANES), 1)
    for d, refs in enumerate(dirs):
        v_ref = refs[2]
        for bb in range(RW_BB):
            for p in range(n_pair):
                vt = v_ref[bb, :, p * LANES:(p + 1) * LANES].T
                top, bot = vt[:HD_B], vt[HD_B:]
                for s in range(2):
                    if s == 0:
                        t2 = jnp.where(lane < HD_B, top, pltpu.roll(bot, HD_B, 1))
                    else:
                        t2 = jnp.where(lane < HD_B, pltpu.roll(top, HD_B, 1), bot)
                    hi, lo = _split2(t2)
                    vt_scr[d, bb, p, s, 0] = hi
                    vt_scr[d, bb, p, s, 1] = lo

    ones2 = ones_ref[...]
    row8 = lax.broadcasted_iota(jnp.int32, (SUBLANES, LANES), 0)
    lane8 = lax.broadcasted_iota(jnp.int32, (SUBLANES, LANES), 1)
    sel_r = jnp.logical_or(jnp.logical_and(row8 == 0, lane8 < HD_B),
                           jnp.logical_and(row8 == 1, lane8 >= HD_B))

    for s_idx in range(2):
        def step(tt, carry, s_idx=s_idx):
            lhs = []
            for d, (nkk_ref, r_ref, v_ref, w_ref, ka_ref, kd_ref, o_ref, rev) in enumerate(dirs):
                sub = 1 - s_idx if rev else s_idx
                lt = half - 1 - tt if rev else tt
                mask = jnp.logical_or(lane == lt, lane == lt + HD_B)
                tau = sub * half + lt
                for bb in range(RW_BB):
                    nkk = nkk_ref[bb, pl.ds(tau, 1), :]
                    for p in range(n_pair):
                        prod = s_scr[d, bb, p] * nkk[:, p * LANES:(p + 1) * LANES]
                        hi, lo = _split2(prod)
                        lhs.append(jnp.concatenate([_bf(hi), _bf(lo)], axis=1))
                        lhs.append(jnp.concatenate([_bf(jnp.where(mask, vt_scr[d, bb, p, sub, 0], 0.0)),
                                                    _bf(jnp.where(mask, vt_scr[d, bb, p, sub, 1], 0.0))], axis=1))
            red = jnp.dot(jnp.concatenate(lhs, axis=0), ones2, preferred_element_type=F32)
            c = 0
            for d, (nkk_ref, r_ref, v_ref, w_ref, ka_ref, kd_ref, o_ref, rev) in enumerate(dirs):
                sub = 1 - s_idx if rev else s_idx
                lt = half - 1 - tt if rev else tt
                tau = sub * half + lt
                for bb in range(RW_BB):
                    w = w_ref[bb, pl.ds(tau, 1), :]
                    ka = ka_ref[bb, pl.ds(tau, 1), :]
                    kd = kd_ref[bb, pl.ds(tau, 1), :]
                    r = r_ref[bb, pl.ds(tau, 1), :]
                    o_parts = []
                    for p in range(n_pair):
                        sl = slice(p * LANES, (p + 1) * LANES)
                        sa = red[c * HD_B:(c + 1) * HD_B]
                        vcol = red[(c + 1) * HD_B:(c + 2) * HD_B]
                        c += 2
                        s_new = s_scr[d, bb, p] * w[:, sl] + sa * ka[:, sl] + vcol * kd[:, sl]
                        s_scr[d, bb, p] = s_new
                        r2 = jnp.where(sel_r, jnp.broadcast_to(r[:, sl], (SUBLANES, LANES)), 0.0)
                        o2 = _dot_nt(_bf(r2), _bf(s_new))
                        o_parts.append(jnp.concatenate([o2[0:1], o2[1:2]], axis=1))
                    o_ref[bb, pl.ds(tau, 1), :] = jnp.concatenate(o_parts, axis=1)
            return carry

        lax.fori_loop(0, half, step, 0)

    @pl.when(tb == n_tb - 1)
    def _():
        sff_ref[...] = s_scr[0]
        sfb_ref[...] = s_scr[1]


def _rwkv_scan(pp, row0, n_seq, seq, s0_f, s0_b, ones_pair):
    n_tb = seq // RW_TB
    n_pair = H_B // 2
    blk0 = row0 // seq
    view = lambda a: a.reshape(a.shape[0] // seq, seq, W_B)
    fwd = pl.BlockSpec((RW_BB, RW_TB, W_B), lambda b, t: (blk0 // RW_BB + b, t, 0))
    bwd = pl.BlockSpec((RW_BB, RW_TB, W_B), lambda b, t: (blk0 // RW_BB + b, n_tb - 1 - t, 0))
    st = pl.BlockSpec((RW_BB, n_pair, HD_B, LANES), lambda b, t: (b, 0, 0, 0))
    o_f = pl.BlockSpec((RW_BB, RW_TB, W_B), lambda b, t: (b, t, 0))
    o_b = pl.BlockSpec((RW_BB, RW_TB, W_B), lambda b, t: (b, n_tb - 1 - t, 0))
    ins_f = [view(pp[k]) for k in ('nkk', 'r', 'v', 'w_f', 'ka_f', 'kd_f')]
    ins_b = [view(pp[k]) for k in ('nkk', 'r', 'v', 'w_b', 'ka_b', 'kd_b')]
    st_shape = jax.ShapeDtypeStruct((n_seq, n_pair, HD_B, LANES), F32)
    o_shape = jax.ShapeDtypeStruct((n_seq, seq, W_B), F32)
    return pl.pallas_call(
        functools.partial(_rwkv_scan_kernel, n_tb),
        grid=(n_seq // RW_BB, n_tb),
        in_specs=[fwd] * 6 + [bwd] * 6 + [st, st, pl.BlockSpec(ones_pair.shape, lambda b, t: (0, 0))],
        out_specs=[o_f, o_b, st, st],
        out_shape=[o_shape, o_shape, st_shape, st_shape],
        scratch_shapes=[pltpu.VMEM((2, RW_BB, n_pair, HD_B, LANES), F32),
                        pltpu.VMEM((2, RW_BB, n_pair, 2, 2, HD_B, LANES), F32)],
        compiler_params=_cparams(2),
        name="rwkv_scan",
    )(*ins_f, *ins_b, s0_f, s0_b, ones_pair)


def _state_to_pairs(s):
    b = s.shape[0]
    return s.reshape(b, H_B // 2, 2, HD_B, HD_B).transpose(0, 1, 3, 2, 4).reshape(b, H_B // 2, HD_B, 2 * HD_B)


def _pairs_to_state(s):
    b = s.shape[0]
    return s.reshape(b, H_B // 2, HD_B, 2, HD_B).transpose(0, 1, 3, 2, 4).reshape(b, H_B, HD_B, HD_B)


def _tail(x, y, g1, n2g, sc2, sh2, x1_ref, h_ref):
    x1 = x + g1 * y
    x1_ref[...] = x1
    h_ref[...] = _bf(_rms_mod(x1, n2g, sc2, sh2))


def _outproj0_kernel(x_ref, oa_ref, of_ref, ob_ref, bonus_ref, g_ref, lnw_ref, lnb_ref, ones_ref,
                     w_ref, g1_ref, n2g_ref, sc2_ref, sh2_ref, x1_ref, h_ref):
    o_sum = of_ref[...] + ob_ref[...]
    ones2 = ones_ref[...]
    mean = _seg_sum(o_sum, ones2) * (1.0 / HD_B)
    cen = o_sum - mean
    var = _seg_sum(cen * cen, ones2) * (1.0 / HD_B)
    gn = cen * lax.rsqrt(var + GN_EPS) * lnw_ref[...] + lnb_ref[...]
    o_rw = (gn + bonus_ref[...]) * g_ref[...]
    mix = jnp.concatenate([_bf(oa_ref[...]), _bf(o_rw)], axis=1)
    y = jnp.dot(mix, w_ref[...], preferred_element_type=F32)
    _tail(x_ref[...], y, g1_ref[...], n2g_ref[...], sc2_ref[...], sh2_ref[...], x1_ref, h_ref)


def _outproj0(rows, x, o_att, o_f, o_b, pp, pr, ones_b, w_out_bf, n2g, mod4, layer):
    n = rows.n
    tok = lambda w: pl.BlockSpec((TM, w), lambda i: (i, 0))
    const = lambda a: pl.BlockSpec(a.shape, lambda i: (0,) * a.ndim)
    lnw, lnb, n2 = pr['ln_w'].reshape(1, -1), pr['ln_b'].reshape(1, -1), n2g.reshape(1, -1)
    return pl.pallas_call(
        _outproj0_kernel,
        grid=(n // TM,),
        in_specs=[tok(D_MODEL), tok(W_A), tok(W_B), tok(W_B), tok(W_B), tok(W_B),
                  const(lnw), const(lnb), const(ones_b), const(w_out_bf),
                  rows.mod_spec(layer, 2, TM), const(n2), rows.mod_spec(layer, 4, TM), rows.mod_spec(layer, 3, TM)],
        out_specs=[tok(D_MODEL), tok(D_MODEL)],
        out_shape=[jax.ShapeDtypeStruct((n, D_MODEL), F32), jax.ShapeDtypeStruct((n, D_MODEL), BF16)],
        compiler_params=_cparams(1),
        name="outproj0",
    )(x, o_att, o_f, o_b, pp['bonus'], pp['g'], lnw, lnb, ones_b, w_out_bf, mod4, n2, mod4, mod4)


def _outproj1_kernel(x_ref, of_ref, ob_ref, gate_ref, ng_ref, w_ref, g1_ref, n2g_ref, sc2_ref, sh2_ref,
                     x1_ref, h_ref):
    o_sum = of_ref[...] + ob_ref[...]
    parts = []
    for h in range(H_C):
        oh = o_sum[:, h * DV_C:(h + 1) * DV_C]
        parts.append(oh * lax.rsqrt(jnp.mean(oh * oh, axis=-1, keepdims=True) + EPS))
    o = jnp.concatenate(parts, axis=1) * ng_ref[...] * _silu(gate_ref[...])
    y = jnp.dot(_bf(o), w_ref[...], preferred_element_type=F32)
    _tail(x_ref[...], y, g1_ref[...], n2g_ref[...], sc2_ref[...], sh2_ref[...], x1_ref, h_ref)


def _outproj1(rows, x, o_f, o_b, p1, norm_g, w_out_bf, n2g, mod4, layer):
    n = rows.n
    tok = lambda w: pl.BlockSpec((TM, w), lambda i: (i, 0))
    const = lambda a: pl.BlockSpec(a.shape, lambda i: (0,) * a.ndim)
    ng, n2 = norm_g.reshape(1, -1), n2g.reshape(1, -1)
    return pl.pallas_call(
        _outproj1_kernel,
        grid=(n // TM,),
        in_specs=[tok(D_MODEL), tok(D_C), tok(D_C), pl.BlockSpec((TM, D_C), lambda i: (i, 4)),
                  const(ng), const(w_out_bf),
                  rows.mod_spec(layer, 2, TM), const(n2), rows.mod_spec(layer, 4, TM), rows.mod_spec(layer, 3, TM)],
        out_specs=[tok(D_MODEL), tok(D_MODEL)],
        out_shape=[jax.ShapeDtypeStruct((n, D_MODEL), F32), jax.ShapeDtypeStruct((n, D_MODEL), BF16)],
        compiler_params=_cparams(1),
        name="outproj1",
    )(x, o_f, o_b, p1, ng, w_out_bf, mod4, n2, mod4, mod4)


def _hgrn_kernel(n_tb, qf_ref, ff_ref, if_ref, qb_ref, fb_ref, ib_ref, lbp_ref, s0f_ref, s0b_ref,
                 trif_ref, trib_ref, of_ref, ob_ref, sff_ref, sfb_ref, s_scr):
    tb = pl.program_id(1)

    @pl.when(tb == 0)
    def _():
        for h in range(H_C):
            s_scr[0, h] = s0f_ref[0, h].T
            s_scr[1, h] = s0b_ref[0, h].T

    lbp = lbp_ref[...]
    e = jnp.exp(lbp - jnp.max(lbp, axis=0, keepdims=True))
    sm = e / jnp.sum(e, axis=0, keepdims=True)
    lb = (sm[0:1] + sm[1:2]) - sm[0:1]

    n_chunk = HG_TB // CHUNK
    ti = lax.broadcasted_iota(jnp.int32, (CHUNK, CHUNK), 0)
    si = lax.broadcasted_iota(jnp.int32, (CHUNK, CHUNK), 1)
    dirs = ((qf_ref, ff_ref, if_ref, of_ref, trif_ref, ti >= si, CHUNK - 1, False),
            (qb_ref, fb_ref, ib_ref, ob_ref, trib_ref, ti <= si, 0, True))

    def chunk(c, carry):
        for d, (q_ref, f_ref, i_ref, o_ref, tri_ref, causal, last_row, rev) in enumerate(dirs):
            cc = n_chunk - 1 - c if rev else c
            r0 = pl.multiple_of(cc * CHUNK, CHUNK)
            tri3 = tri_ref[...]
            for h in range(H_C):
                sl = slice(h * DK_C, (h + 1) * DK_C)
                q = _silu(q_ref[pl.ds(r0, CHUNK), sl])
                f = lb[:, sl] + (1.0 - lb[:, sl]) * _sigmoid(f_ref[pl.ds(r0, CHUNK), sl])
                k = 1.0 - f
                v = _bf(i_ref[pl.ds(r0, CHUNK), sl])
                g = jnp.log(f)
                g1 = _bf(g)
                g2 = _bf(g - g1.astype(F32))
                g3 = _bf(g - g1.astype(F32) - g2.astype(F32))
                b = jnp.dot(tri3, jnp.concatenate([g1, g2, g3], axis=0), preferred_element_type=F32)
                b_last = b[last_row:last_row + 1]
                q_in = _bf(q * jnp.exp(b))
                k_in = _bf(k * jnp.exp(-b))
                k_out = _bf(k * jnp.exp(b_last - b))
                att = jnp.where(causal, _dot_nt(q_in, k_in), 0.0)
                s_t = s_scr[d, h]
                o = jnp.dot(_bf(att), v, preferred_element_type=F32) + _dot_nt(q_in, _bf(s_t))
                o_ref[pl.ds(r0, CHUNK), sl] = o
                s_scr[d, h] = jnp.exp(b_last) * s_t + _dot_tn(v, k_out)
        return carry

    lax.fori_loop(0, n_chunk, chunk, 0)

    @pl.when(tb == n_tb - 1)
    def _():
        for h in range(H_C):
            sff_ref[0, h] = s_scr[0, h].T
            sfb_ref[0, h] = s_scr[1, h].T


def _hgrn_scan(p1, row0, n_seq, seq, lb_params, s0_f, s0_b):
    n_tb = seq // HG_TB
    blk0 = row0 // HG_TB
    tri = np.tril(np.ones((CHUNK, CHUNK), np.float32))
    tri_f = jnp.asarray(np.concatenate([tri] * 3, axis=1), dtype=BF16)
    tri_b = jnp.asarray(np.concatenate([tri.T] * 3, axis=1), dtype=BF16)
    fwd = lambda col: pl.BlockSpec((HG_TB, D_C), lambda b, t: (blk0 + b * n_tb + t, col))
    bwd = lambda col: pl.BlockSpec((HG_TB, D_C), lambda b, t: (blk0 + b * n_tb + n_tb - 1 - t, col))
    st = pl.BlockSpec((1, H_C, DK_C, DV_C), lambda b, t: (b, 0, 0, 0))
    const = lambda a: pl.BlockSpec(a.shape, lambda b, t: (0,) * a.ndim)
    o_shape = jax.ShapeDtypeStruct((n_seq * seq, D_C), F32)
    st_shape = jax.ShapeDtypeStruct((n_seq, H_C, DK_C, DV_C), F32)
    return pl.pallas_call(
        functools.partial(_hgrn_kernel, n_tb),
        grid=(n_seq, n_tb),
        in_specs=[fwd(0), fwd(1), fwd(3), bwd(0), bwd(2), bwd(3), const(lb_params), st, st,
                  const(tri_f), const(tri_b)],
        out_specs=[pl.BlockSpec((HG_TB, D_C), lambda b, t: (b * n_tb + t, 0)),
                   pl.BlockSpec((HG_TB, D_C), lambda b, t: (b * n_tb + n_tb - 1 - t, 0)), st, st],
        out_shape=[o_shape, o_shape, st_shape, st_shape],
        scratch_shapes=[pltpu.VMEM((2, H_C, DV_C, DK_C), F32)],
        compiler_params=_cparams(2),
        name="hgrn_scan",
    )(p1, p1, p1, p1, p1, p1, lb_params, s0_f, s0_b, tri_f, tri_b)


def _router_kernel(h_ref, rhi_ref, rlo_ref, bias_ref, gates_ref):
    x = h_ref[...]
    tm = x.shape[0]
    logits = _dot_nt(rhi_ref[...], x) + _dot_nt(rlo_ref[...], x)
    scores = _sigmoid(logits)
    biased = scores + bias_ref[...]
    per = N_EXPERTS // N_GROUPS
    sub = lax.broadcasted_iota(jnp.int32, (per, tm), 0)
    gs_rows = []
    for g in range(N_GROUPS):
        blk = biased[g * per:(g + 1) * per]
        m1 = jnp.max(blk, axis=0, keepdims=True)
        first = jnp.min(jnp.where(blk == m1, sub, per), axis=0, keepdims=True)
        m2 = jnp.max(jnp.where(sub == first, -jnp.inf, blk), axis=0, keepdims=True)
        gs_rows.append(m1 + m2)
    gs = jnp.concatenate(gs_rows, axis=0)
    gi = lax.broadcasted_iota(jnp.int32, gs.shape, 0)
    rank = jnp.zeros(gs.shape, jnp.int32)
    for s in range(1, N_GROUPS):
        other = pltpu.roll(gs, s, 0)
        oi = pltpu.roll(gi, s, 0)
        beats = jnp.logical_or(other > gs, jnp.logical_and(other == gs, oi < gi))
        rank = rank + jnp.where(beats, 1, 0)
    keep = jnp.where(rank < TOPK_GROUPS, 1.0, 0.0)
    emask = jnp.concatenate([jnp.broadcast_to(keep[g:g + 1], (per, tm)) for g in range(N_GROUPS)], axis=0)
    cur = jnp.where(emask > 0.0, biased, -jnp.inf)
    ei = lax.broadcasted_iota(jnp.int32, cur.shape, 0)
    w = jnp.zeros(cur.shape, F32)
    for _ in range(TOP_K):
        m = jnp.max(cur, axis=0, keepdims=True)
        idx = jnp.min(jnp.where(cur == m, ei, N_EXPERTS), axis=0, keepdims=True)
        pick = ei == idx
        w = jnp.where(pick, scores, w)
        cur = jnp.where(pick, -jnp.inf, cur)
    gates_ref[...] = w / jnp.sum(w, axis=0, keepdims=True) * ROUTED_SCALE


def _router(hffn, router, bias):
    n = hffn.shape[0]
    r_t = router.T
    r_hi = _bf(r_t)
    r_lo = _bf(r_t - r_hi.astype(F32))
    const = lambda a: pl.BlockSpec(a.shape, lambda i: (0,) * a.ndim)
    b_col = bias.reshape(N_EXPERTS, 1)
    return pl.pallas_call(
        _router_kernel,
        grid=(n // TM,),
        in_specs=[pl.BlockSpec((TM, D_MODEL), lambda i: (i, 0)), const(r_hi), const(r_lo), const(b_col)],
        out_specs=pl.BlockSpec((N_EXPERTS, TM), lambda i: (0, i)),
        out_shape=jax.ShapeDtypeStruct((N_EXPERTS, n), F32),
        compiler_params=_cparams(1),
        name="router",
    )(hffn, r_hi, r_lo, b_col)


def _glu(x, wg, wu):
    hg = jnp.dot(x, wg, preferred_element_type=F32)
    hu = jnp.dot(x, wu, preferred_element_type=F32)
    return _silu(hg) * hu


def _moe_dense_kernel(h_ref, gates_ref, wg_ref, wu_ref, wd_ref, sg_ref, su_ref, sd_ref, x1_ref, g2_ref,
                      o_ref, acc_ref):
    e = pl.program_id(1)
    x = h_ref[...]

    @pl.when(e == 0)
    def _():
        act = _glu(x, sg_ref[...], su_ref[...])
        acc_ref[...] = jnp.dot(_bf(act), sd_ref[...], preferred_element_type=F32)

    act = _glu(x, wg_ref[0], wu_ref[0])
    wd = wd_ref[0]
    gates = gates_ref[...]
    ghi = _bf(gates)
    glo = _bf(gates - ghi.astype(F32))
    pick = (lax.broadcasted_iota(jnp.int32, (2 * N_EXPERTS, D_EXPERT), 0) % N_EXPERTS == e).astype(BF16)
    gcol = _dot_tn(jnp.concatenate([ghi, glo], axis=0), pick)
    acc_ref[...] += jnp.dot(_bf(act * gcol), wd, preferred_element_type=F32)

    @pl.when(e == N_EXPERTS - 1)
    def _():
        o_ref[...] = x1_ref[...] + g2_ref[...] * acc_ref[...]


def _moe_dense(rows, hffn, gates_t, mp, x1, mod4, layer):
    n = rows.n
    tm = MOE_TM
    const = lambda a: pl.BlockSpec(a.shape, lambda i, e: (0,) * a.ndim)
    tok = lambda w: pl.BlockSpec((tm, w), lambda i, e: (i, 0))
    return pl.pallas_call(
        _moe_dense_kernel,
        grid=(n // tm, N_EXPERTS),
        in_specs=[tok(D_MODEL), pl.BlockSpec((N_EXPERTS, tm), lambda i, e: (0, i)),
                  pl.BlockSpec((1, D_MODEL, D_EXPERT), lambda i, e: (e, 0, 0)),
                  pl.BlockSpec((1, D_MODEL, D_EXPERT), lambda i, e: (e, 0, 0)),
                  pl.BlockSpec((1, D_EXPERT, D_MODEL), lambda i, e: (e, 0, 0)),
                  const(mp['sg']), const(mp['su']), const(mp['sd']),
                  tok(D_MODEL), rows.mod_spec(layer, 5, tm)],
        out_specs=tok(D_MODEL),
        out_shape=jax.ShapeDtypeStruct((n, D_MODEL), F32),
        scratch_shapes=[pltpu.VMEM((tm, D_MODEL), F32)],
        compiler_params=_cparams(2),
        name=f"moe_dense{layer}",
    )(hffn, gates_t, mp['wg'], mp['wu'], mp['wd'], mp['sg'], mp['su'], mp['sd'], x1, mod4)


def kernel(x_prompt, x_sample, c, c_ctx, cache_attn_k, cache_attn_v, state_rwkv_fwd, state_rwkv_bwd,
           state_hgrn_fwd, state_hgrn_bwd, norm1_g, norm2_g, mod_w, mod_b, ab_w_in, ab_w_out, attn_q_norm,
           attn_k_norm, attn_sink, rwkv_mu, rwkv_w0, rwkv_w2, rwkv_a0, rwkv_a2, rwkv_g2, rwkv_k_k, rwkv_k_a,
           rwkv_r_k, rwkv_ln_w, rwkv_ln_b, hgrn_w_in, hgrn_w_out, hgrn_lower_bounds, hgrn_norm_g, moe_router,
           moe_bias, moe_w_gate, moe_w_up, moe_w_down, moe_shared_gate, moe_shared_up, moe_shared_down):
    n_cseq, cseq, _ = x_prompt.shape
    n_lseq, lseq, _ = x_sample.shape
    depth = mod_w.shape[0]
    assert depth == 2 and n_lseq + 1 <= SUBLANES
    assert cseq == TM and lseq % TM == 0 and lseq % HG_TB == 0 and cseq % HG_TB == 0
    assert n_cseq % RW_BB == 0 and n_lseq % RW_BB == 0 and (n_cseq * cseq) % (lseq * RW_BB) == 0
    rows = _Rows(n_cseq * cseq, n_lseq * lseq, lseq)
    assert rows.n % MOE_TM == 0 and lseq % MOE_TM == 0 and rows.n_ctx % MOE_TM == 0
    kv_w = KV_A * HD_A

    x = jnp.concatenate([x_prompt.reshape(rows.n_ctx, D_MODEL), x_sample.reshape(rows.n_lat, D_MODEL)], axis=0)
    cvecs = jnp.concatenate([c_ctx[None, :], c, jnp.zeros((SUBLANES - 1 - n_lseq, D_MODEL), F32)], axis=0)
    mod4 = _modulation(cvecs, mod_w, mod_b).reshape(depth, SUBLANES, 1, 6 * D_MODEL)

    ones_q = _block_ones(W_A, HD_A)
    ones_k = _block_ones(kv_w, HD_A)
    ones_b = _block_ones(W_B, HD_B)
    ones_pair = _block_ones(LANES, HD_B)
    cos_t, sin_t = _rope_tables(lseq)

    def moe(l, hffn, x1):
        mp = {'wg': _bf(moe_w_gate[l]), 'wu': _bf(moe_w_up[l]), 'wd': _bf(moe_w_down[l]),
              'sg': _bf(moe_shared_gate[l]), 'su': _bf(moe_shared_up[l]), 'sd': _bf(moe_shared_down[l])}
        gates_t = _router(hffn, moe_router[l], moe_bias[l])
        return _moe_dense(rows, hffn, gates_t, mp, x1, mod4, l)

    pr = {'mu': rwkv_mu[0], 'w0': rwkv_w0[0], 'w2': rwkv_w2[0], 'a0': rwkv_a0[0], 'a2': rwkv_a2[0],
          'g2': rwkv_g2[0], 'k_k': rwkv_k_k[0], 'k_a': rwkv_k_a[0], 'r_k': rwkv_r_k[0].reshape(-1),
          'ln_w': rwkv_ln_w[0], 'ln_b': rwkv_ln_b[0]}
    p_att, p_rw = _inproj(rows, x, norm1_g[0], mod4, 0, _bf(ab_w_in[0]), (ATT_IN, RWKV_IN))
    qg_t = jnp.tile(attn_q_norm[0], H_A).reshape(1, W_A)
    kg_t = jnp.tile(attn_k_norm[0], KV_A).reshape(1, kv_w)
    o_att_c, new_k, new_v = _ctx_attention(p_att, n_cseq, cseq, qg_t, kg_t, attn_sink[0], ones_q, ones_k)
    past = cache_attn_k.shape[2]
    o_att_l = _lat_attention(p_att, rows.n_ctx // lseq, n_lseq, lseq, qg_t, kg_t, attn_sink[0], ones_q, ones_k,
                             cos_t, sin_t, cache_attn_k[:, 0].reshape(n_lseq, past, kv_w),
                             cache_attn_v[:, 0].reshape(n_lseq, past, kv_w))
    o_att = jnp.concatenate([o_att_c, o_att_l], axis=0)

    pp = _rwkv_prep(rows, p_rw, pr, ones_b)
    zero_st = jnp.zeros((n_cseq, H_B // 2, HD_B, LANES), F32)
    of_c, ob_c, sf_c, sb_c = _rwkv_scan(pp, 0, n_cseq, cseq, zero_st, zero_st, ones_pair)
    of_l, ob_l, _, _ = _rwkv_scan(pp, rows.n_ctx, n_lseq, lseq, _state_to_pairs(state_rwkv_fwd[:, 0]),
                                  _state_to_pairs(state_rwkv_bwd[:, 0]), ones_pair)
    o_f = jnp.concatenate([of_c.reshape(rows.n_ctx, W_B), of_l.reshape(rows.n_lat, W_B)], axis=0)
    o_b = jnp.concatenate([ob_c.reshape(rows.n_ctx, W_B), ob_l.reshape(rows.n_lat, W_B)], axis=0)
    x1, hffn = _outproj0(rows, x, o_att, o_f, o_b, pp, pr, ones_b, _bf(ab_w_out[0]), norm2_g[0], mod4, 0)
    x = moe(0, hffn, x1)

    (p1,) = _inproj(rows, x, norm1_g[1], mod4, 1, _bf(hgrn_w_in[0]), (IN_C,))
    zero_h = jnp.zeros((n_cseq, H_C, DK_C, DV_C), F32)
    hf_c, hb_c, hsf_c, hsb_c = _hgrn_scan(p1, 0, n_cseq, cseq, hgrn_lower_bounds, zero_h, zero_h)
    hf_l, hb_l, _, _ = _hgrn_scan(p1, rows.n_ctx, n_lseq, lseq, hgrn_lower_bounds,
                                  state_hgrn_fwd[:, 0], state_hgrn_bwd[:, 0])
    h_f = jnp.concatenate([hf_c, hf_l], axis=0)
    h_b = jnp.concatenate([hb_c, hb_l], axis=0)
    x1, hffn = _outproj1(rows, x, h_f, h_b, p1, hgrn_norm_g[0], _bf(hgrn_w_out[0]), norm2_g[1], mod4, 1)
    x = moe(1, hffn, x1)

    y_prompt = x[:rows.n_ctx].reshape(n_cseq, cseq, D_MODEL)
    y_sample = x[rows.n_ctx:].reshape(n_lseq, lseq, D_MODEL)
    return (y_prompt, y_sample,
            new_k.reshape(n_cseq, 1, cseq, KV_A, HD_A), new_v.reshape(n_cseq, 1, cseq, KV_A, HD_A),
            _pairs_to_state(sf_c)[:, None], _pairs_to_state(sb_c)[:, None],
            hsf_c[:, None], hsb_c[:, None])
```

```python
import functools

import numpy as np
import jax
import jax.numpy as jnp
from jax import lax
from jax.experimental import pallas as pl
from jax.experimental.pallas import tpu as pltpu

F32 = jnp.float32
BF16 = jnp.bfloat16

D_MODEL = 1024
GRID_W = 64
H_A = 8
KV_A = 2
G_A = H_A // KV_A
HD_A = 64
W_A = H_A * HD_A
WINDOW = 128
QBLK = 128
ROPE_BASE = 10000.0
ATTN_SCALE = HD_A ** -0.5
NEG_INF = -1e30
H_B = 8
HD_B = 64
W_B = H_B * HD_B
LORA_W = 64
LORA_A = 64
LORA_G = 128
GN_EPS = 64e-5
ATT_IN = W_A + 2 * KV_A * HD_A
RWKV_IN = 3 * W_B + LORA_W + LORA_A + LORA_G
IN_AB = ATT_IN + RWKV_IN
H_C = 8
DK_C = 128
DV_C = 128
D_C = H_C * DV_C
CHUNK = 64
IN_C = 5 * D_C
N_EXPERTS = 64
TOP_K = 8
N_GROUPS = 8
TOPK_GROUPS = 4
D_EXPERT = 256
ROUTED_SCALE = 2.5
EPS = 1e-6

LANES = 128
SUBLANES = 8
VMEM_LIMIT = 52 * 1024 * 1024

TM = 256
RW_TB = 128
RW_BB = 4
RW_GROUP_BB = 2
HG_TB = 256
MOE_TM = 1024


def _cparams(n_axes):
    return pltpu.CompilerParams(dimension_semantics=("arbitrary",) * n_axes,
                                vmem_limit_bytes=VMEM_LIMIT)


def _bf(x):
    return x.astype(BF16)


def _split2(x):
    hi = lax.bitcast_convert_type(
        lax.bitcast_convert_type(x, jnp.uint32) & jnp.uint32(0xFFFF0000), F32)
    return hi, x - hi


def _seg_sum(x, ones2):
    hi, lo = _split2(x)
    return jnp.dot(jnp.concatenate([_bf(hi), _bf(lo)], axis=1), ones2,
                   preferred_element_type=F32)


def _dot_nt(a, b):
    return lax.dot_general(a, b, (((1,), (1,)), ((), ())), preferred_element_type=F32)


def _dot_tn(a, b):
    return lax.dot_general(a, b, (((0,), (0,)), ((), ())), preferred_element_type=F32)


def _sigmoid(x):
    return 1.0 / (1.0 + jnp.exp(-x))


def _silu(x):
    return x * _sigmoid(x)


def _chunks(seq, n):
    seq = list(seq)
    return [seq[i:i + n] for i in range(0, len(seq), n)]


def _block_ones(width, seg):
    idx = np.arange(width) // seg
    bd = (idx[:, None] == idx[None, :]).astype(np.float32)
    return jnp.asarray(np.concatenate([bd, bd], axis=0), dtype=BF16)


def _mod_kernel(c_ref, w_ref, b_ref, o_ref):
    s = _silu(c_ref[...])
    o_ref[0] = jnp.dot(_bf(s), _bf(w_ref[0]), preferred_element_type=F32) + b_ref[0]


def _modulation(cvecs, mod_w, mod_b):
    depth = mod_w.shape[0]
    n_col = 6 * D_MODEL // D_MODEL
    return pl.pallas_call(
        _mod_kernel,
        grid=(depth, n_col),
        in_specs=[pl.BlockSpec((SUBLANES, D_MODEL), lambda l, j: (0, 0)),
                  pl.BlockSpec((1, D_MODEL, D_MODEL), lambda l, j: (l, 0, j)),
                  pl.BlockSpec((1, 1, D_MODEL), lambda l, j: (l, 0, j))],
        out_specs=pl.BlockSpec((1, SUBLANES, D_MODEL), lambda l, j: (l, 0, j)),
        out_shape=jax.ShapeDtypeStruct((depth, SUBLANES, 6 * D_MODEL), F32),
        compiler_params=_cparams(2),
        name="modulation",
    )(cvecs, mod_w, mod_b.reshape(depth, 1, 6 * D_MODEL))


class _Rows:
    def __init__(self, n_ctx, n_lat, lat_seq):
        self.n_ctx, self.n_lat, self.lat_seq = n_ctx, n_lat, lat_seq
        self.n = n_ctx + n_lat

    def mod_row(self, i, tm):
        nctx_blk = self.n_ctx // tm
        per_seq = self.lat_seq // tm
        return jnp.where(i < nctx_blk, 0, 1 + (i - nctx_blk) // per_seq)

    def mod_spec(self, layer, chunk, tm):
        return pl.BlockSpec((None, None, 1, D_MODEL),
                            lambda i, *_: (layer, self.mod_row(i, tm), 0, chunk))


def _rms_mod(x, g, sc, sh):
    ms = jnp.mean(x * x, axis=-1, keepdims=True)
    return x * lax.rsqrt(ms + EPS) * g * (1.0 + sc) + sh


def _inproj_kernel(splits, x_ref, g_ref, sh_ref, sc_ref, w_ref, *o_refs):
    h = _rms_mod(x_ref[...], g_ref[...], sc_ref[...], sh_ref[...])
    p = jnp.dot(_bf(h), w_ref[...], preferred_element_type=F32)
    lo = 0
    for o_ref, width in zip(o_refs, splits):
        o_ref[...] = p[:, lo:lo + width]
        lo += width


def _inproj(rows, x, g, mod4, layer, w_bf, splits):
    n_out = w_bf.shape[1]
    return pl.pallas_call(
        functools.partial(_inproj_kernel, splits),
        grid=(rows.n // TM,),
        in_specs=[pl.BlockSpec((TM, D_MODEL), lambda i: (i, 0)),
                  pl.BlockSpec((1, D_MODEL), lambda i: (0, 0)),
                  rows.mod_spec(layer, 0, TM),
                  rows.mod_spec(layer, 1, TM),
                  pl.BlockSpec((D_MODEL, n_out), lambda i: (0, 0))],
        out_specs=[pl.BlockSpec((TM, wd), lambda i: (i, 0)) for wd in splits],
        out_shape=[jax.ShapeDtypeStruct((rows.n, wd), F32) for wd in splits],
        compiler_params=_cparams(1),
        name=f"inproj{layer}",
    )(x, g.reshape(1, D_MODEL), mod4, mod4, w_bf)


def _head_rms(x, gain_t, ones2):
    ms = _seg_sum(x * x, ones2) * (1.0 / HD_A)
    return x * lax.rsqrt(ms + EPS) * gain_t


def _sink_softmax_pv(parts, sink):
    m = jnp.maximum(functools.reduce(jnp.maximum, [jnp.max(s, axis=-1, keepdims=True) for s, _ in parts]), sink)
    den = jnp.exp(sink - m)
    acc = None
    for s, v in parts:
        p = jnp.exp(s - m)
        den = den + jnp.sum(p, axis=-1, keepdims=True)
        pv = jnp.dot(_bf(p), v, preferred_element_type=F32)
        acc = pv if acc is None else acc + pv
    return acc / den


def _ctx_attn_kernel(p_ref, qg_ref, kg_ref, sink_ref, ones_q_ref, ones_k_ref, o_ref, k_ref, v_ref):
    p = p_ref[...]
    q = _head_rms(p[:, :W_A], qg_ref[...], ones_q_ref[...]) * ATTN_SCALE
    k = _head_rms(p[:, W_A:W_A + KV_A * HD_A], kg_ref[...], ones_k_ref[...])
    v = p[:, W_A + KV_A * HD_A:ATT_IN]
    k_ref[0] = k
    v_ref[0] = v
    qb, kb, vb = _bf(q), _bf(k), _bf(v)
    outs = []
    for h in range(H_A):
        j = h // G_A
        s = _dot_nt(qb[:, h * HD_A:(h + 1) * HD_A], kb[:, j * HD_A:(j + 1) * HD_A])
        outs.append(_sink_softmax_pv([(s, vb[:, j * HD_A:(j + 1) * HD_A])], sink_ref[h]))
    o_ref[...] = jnp.concatenate(outs, axis=1)


def _ctx_attention(p_att, n_seq, seq, qg_t, kg_t, sink, ones_q, ones_k):
    kv_w = KV_A * HD_A
    return pl.pallas_call(
        _ctx_attn_kernel,
        grid=(n_seq,),
        in_specs=[pl.BlockSpec((seq, ATT_IN), lambda b: (b, 0)),
                  pl.BlockSpec((1, W_A), lambda b: (0, 0)),
                  pl.BlockSpec((1, kv_w), lambda b: (0, 0)),
                  pl.BlockSpec(memory_space=pltpu.SMEM),
                  pl.BlockSpec(ones_q.shape, lambda b: (0, 0)),
                  pl.BlockSpec(ones_k.shape, lambda b: (0, 0))],
        out_specs=[pl.BlockSpec((seq, W_A), lambda b: (b, 0)),
                   pl.BlockSpec((1, seq, kv_w), lambda b: (b, 0, 0)),
                   pl.BlockSpec((1, seq, kv_w), lambda b: (b, 0, 0))],
        out_shape=[jax.ShapeDtypeStruct((n_seq * seq, W_A), F32),
                   jax.ShapeDtypeStruct((n_seq, seq, kv_w), F32),
                   jax.ShapeDtypeStruct((n_seq, seq, kv_w), F32)],
        compiler_params=_cparams(1),
        name="ctx_attention",
    )(p_att, qg_t, kg_t, sink, ones_q, ones_k)


def _rope(x, cos_t, sin_t):
    lane = lax.broadcasted_iota(jnp.int32, cos_t.shape, 1)
    low = (lane % 32) < 16
    outs = []
    for s in range(x.shape[1] // LANES):
        xs = x[:, s * LANES:(s + 1) * LANES]
        partner = jnp.where(low, pltpu.roll(xs, LANES - 16, 1), pltpu.roll(xs, 16, 1))
        outs.append(xs * cos_t + partner * sin_t)
    return outs[0] if len(outs) == 1 else jnp.concatenate(outs, axis=1)


def _lat_attn_kernel(seq, p_ref, qg_ref, kg_ref, sink_ref, ones_q_ref, ones_k_ref, cos_ref, sin_ref,
                     kc_ref, vc_ref, o_ref, q_scr, k_scr, v_scr):
    kv_w = KV_A * HD_A
    p = p_ref[...]
    q = _head_rms(p[:, :W_A], qg_ref[...], ones_q_ref[...])
    k = _head_rms(p[:, W_A:W_A + kv_w], kg_ref[...], ones_k_ref[...])
    q_scr[...] = _bf(_rope(q, cos_ref[...], sin_ref[...]) * ATTN_SCALE)
    k_scr[...] = _bf(_rope(k, cos_ref[...], sin_ref[...]))
    v_scr[...] = _bf(p[:, W_A + kv_w:ATT_IN])
    kc = _bf(kc_ref[0])
    vc = _bf(vc_ref[0])
    n_local = 3 * QBLK

    def block(i, carry):
        q0 = pl.multiple_of(i * QBLK, QBLK)
        start = pl.multiple_of(jnp.clip((i - 1) * QBLK, 0, seq - n_local), QBLK)
        qb = q_scr[pl.ds(q0, QBLK), :]
        kl = k_scr[pl.ds(start, n_local), :]
        vl = v_scr[pl.ds(start, n_local), :]
        ipos = q0 + lax.broadcasted_iota(jnp.int32, (QBLK, n_local), 0)
        jpos = start + lax.broadcasted_iota(jnp.int32, (QBLK, n_local), 1)
        band = jnp.abs(jpos - ipos) <= WINDOW
        outs = []
        for h in range(H_A):
            j = h // G_A
            qh = qb[:, h * HD_A:(h + 1) * HD_A]
            s_loc = jnp.where(band, _dot_nt(qh, kl[:, j * HD_A:(j + 1) * HD_A]), NEG_INF)
            s_ctx = _dot_nt(qh, kc[:, j * HD_A:(j + 1) * HD_A])
            outs.append(_sink_softmax_pv([(s_loc, vl[:, j * HD_A:(j + 1) * HD_A]),
                                          (s_ctx, vc[:, j * HD_A:(j + 1) * HD_A])], sink_ref[h]))
        o_ref[pl.ds(q0, QBLK), :] = jnp.concatenate(outs, axis=1)
        return carry

    lax.fori_loop(0, seq // QBLK, block, 0)


def _lat_attention(p_att, row_blk0, n_seq, seq, qg_t, kg_t, sink, ones_q, ones_k, cos_t, sin_t, kc, vc):
    kv_w = KV_A * HD_A
    past = kc.shape[1]
    return pl.pallas_call(
        functools.partial(_lat_attn_kernel, seq),
        grid=(n_seq,),
        in_specs=[pl.BlockSpec((seq, ATT_IN), lambda b: (row_blk0 + b, 0)),
                  pl.BlockSpec((1, W_A), lambda b: (0, 0)),
                  pl.BlockSpec((1, kv_w), lambda b: (0, 0)),
                  pl.BlockSpec(memory_space=pltpu.SMEM),
                  pl.BlockSpec(ones_q.shape, lambda b: (0, 0)),
                  pl.BlockSpec(ones_k.shape, lambda b: (0, 0)),
                  pl.BlockSpec((seq, LANES), lambda b: (0, 0)),
                  pl.BlockSpec((seq, LANES), lambda b: (0, 0)),
                  pl.BlockSpec((1, past, kv_w), lambda b: (b, 0, 0)),
                  pl.BlockSpec((1, past, kv_w), lambda b: (b, 0, 0))],
        out_specs=pl.BlockSpec((seq, W_A), lambda b: (b, 0)),
        out_shape=jax.ShapeDtypeStruct((n_seq * seq, W_A), F32),
        scratch_shapes=[pltpu.VMEM((seq, W_A), BF16), pltpu.VMEM((seq, kv_w), BF16),
                        pltpu.VMEM((seq, kv_w), BF16)],
        compiler_params=_cparams(1),
        name="lat_attention",
    )(p_att, qg_t, kg_t, sink, ones_q, ones_k, cos_t, sin_t, kc, vc)


def _rope_tables(seq):
    pos = np.arange(seq)
    row = (pos // GRID_W).astype(np.float32)
    col = (pos % GRID_W).astype(np.float32)
    d_axis = HD_A // 2
    inv = (ROPE_BASE ** (-np.arange(0, d_axis, 2, dtype=np.float32) / d_axis)).astype(np.float32)
    cos_h = np.zeros((seq, HD_A), np.float32)
    sin_h = np.zeros((seq, HD_A), np.float32)
    for seg, p_ in enumerate((row, col)):
        ang = (p_[:, None] * inv[None, :]).astype(np.float32)
        c, s = np.cos(ang), np.sin(ang)
        base = seg * d_axis
        cos_h[:, base:base + d_axis // 2] = c
        cos_h[:, base + d_axis // 2:base + d_axis] = c
        sin_h[:, base:base + d_axis // 2] = -s
        sin_h[:, base + d_axis // 2:base + d_axis] = s
    rep = LANES // HD_A
    return jnp.asarray(np.tile(cos_h, (1, rep))), jnp.asarray(np.tile(sin_h, (1, rep)))


def _rwkv_prep_kernel(rows, x_ref, prev_ref, next_ref, mu_ref, kk_ref, ka_ref, rk_ref, w0_ref, w2_ref,
                      a0_ref, a2_ref, g2_ref, ones_ref,
                      nkk_ref, r_ref, v_ref, g_ref, bonus_ref,
                      wf_ref, kaf_ref, kdf_ref, wb_ref, kab_ref, kdb_ref):
    i = pl.program_id(0)
    nctx_blk = rows.n_ctx // TM
    per_seq = rows.lat_seq // TM
    is_ctx = i < nctx_blk
    first = jnp.logical_or(is_ctx, (i - nctx_blk) % per_seq == 0)
    last = jnp.logical_or(is_ctx, (i - nctx_blk) % per_seq == per_seq - 1)
    x = x_ref[...]
    ridx = lax.broadcasted_iota(jnp.int32, x.shape, 0)
    prev_row = jnp.where(first, 0.0, prev_ref[SUBLANES - 1:SUBLANES, :])
    next_row = jnp.where(last, 0.0, next_ref[0:1, :])
    xm1 = jnp.where(ridx == 0, prev_row, pltpu.roll(x, 1, 0))
    xp1 = jnp.where(ridx == TM - 1, next_row, pltpu.roll(x, TM - 1, 0))
    pw = x + (0.5 * (xm1 + xp1) - x) * mu_ref[...]

    r = pw[:, 0:W_B]
    k = pw[:, W_B:2 * W_B]
    v = pw[:, 2 * W_B:3 * W_B]
    wd = pw[:, 3 * W_B:3 * W_B + LORA_W]
    ad = pw[:, 3 * W_B + LORA_W:3 * W_B + LORA_W + LORA_A]
    gd = pw[:, 3 * W_B + LORA_W + LORA_A:]
    ones2 = ones_ref[...]

    kk = k * kk_ref[...]
    kk = kk / jnp.maximum(jnp.sqrt(_seg_sum(kk * kk, ones2)), 1e-12)
    nkk_ref[...] = -kk
    r_ref[...] = r
    v_ref[...] = v
    g_ref[...] = jnp.dot(_bf(_sigmoid(gd)), g2_ref[...], preferred_element_type=F32)
    tw = _bf(jnp.tanh(wd))
    adb = _bf(ad)
    bonus = jnp.zeros_like(r)
    for d, (w_o, ka_o, kd_o) in enumerate(((wf_ref, kaf_ref, kdf_ref), (wb_ref, kab_ref, kdb_ref))):
        z = -(w0_ref[d:d + 1, :] + jnp.dot(tw, w2_ref[d], preferred_element_type=F32))
        softplus = jnp.maximum(z, 0.0) + jnp.log(1.0 + jnp.exp(-jnp.abs(z)))
        w_o[...] = jnp.exp(-jnp.exp(-softplus - 0.5))
        a = _sigmoid(a0_ref[d:d + 1, :] + jnp.dot(adb, a2_ref[d], preferred_element_type=F32))
        kd = k * (1.0 + (a - 1.0) * ka_ref[...])
        ka_o[...] = kk * a
        kd_o[...] = kd
        bonus = bonus + _seg_sum(r * kd * rk_ref[...], ones2) * v
    bonus_ref[...] = bonus


def _rwkv_prep(rows, p_rw, pr, ones_b):
    n = rows.n
    n_halo = n // SUBLANES
    blk_halo = TM // SUBLANES
    row = lambda a: a.reshape(1, -1)
    full = lambda a: pl.BlockSpec(a.shape, lambda i: (0,) * a.ndim)
    consts = [row(pr['mu']), row(pr['k_k']), row(pr['k_a']), row(pr['r_k']), pr['w0'], _bf(pr['w2']),
              pr['a0'], _bf(pr['a2']), _bf(pr['g2']), ones_b]
    outs = pl.pallas_call(
        functools.partial(_rwkv_prep_kernel, rows),
        grid=(n // TM,),
        in_specs=[pl.BlockSpec((TM, RWKV_IN), lambda i: (i, 0)),
                  pl.BlockSpec((SUBLANES, RWKV_IN), lambda i: (jnp.maximum(i * blk_halo - 1, 0), 0)),
                  pl.BlockSpec((SUBLANES, RWKV_IN), lambda i: (jnp.minimum((i + 1) * blk_halo, n_halo - 1), 0))]
                 + [full(a) for a in consts],
        out_specs=[pl.BlockSpec((TM, W_B), lambda i: (i, 0))] * 11,
        out_shape=[jax.ShapeDtypeStruct((n, W_B), F32)] * 11,
        compiler_params=_cparams(1),
        name="rwkv_prep",
    )(p_rw, p_rw, p_rw, *consts)
    names = ('nkk', 'r', 'v', 'g', 'bonus', 'w_f', 'ka_f', 'kd_f', 'w_b', 'ka_b', 'kd_b')
    return dict(zip(names, outs))


def _rwkv_scan_kernel(n_tb, nkkf_ref, rf_ref, vf_ref, wf_ref, kaf_ref, kdf_ref,
                      nkkb_ref, rb_ref, vb_ref, wb_ref, kab_ref, kdb_ref,
                      s0f_ref, s0b_ref, ones_ref,
                      of_ref, ob_ref, sff_ref, sfb_ref, s_scr, vt_scr):
    tb = pl.program_id(1)
    n_pair = H_B // 2
    half = RW_TB // 2
    dirs = ((nkkf_ref, rf_ref, vf_ref, wf_ref, kaf_ref, kdf_ref, of_ref, False),
            (nkkb_ref, rb_ref, vb_ref, wb_ref, kab_ref, kdb_ref, ob_ref, True))

    @pl.when(tb == 0)
    def _():
        s_scr[0] = s0f_ref[...]
        s_scr[1] = s0b_ref[...]

    lane = lax.broadcasted_iota(jnp.int32, (HD_B, LANES), 1)
    for d, refs in enumerate(dirs):
        v_ref = refs[2]
        for bb in range(RW_BB):
            for p in range(n_pair):
                vt = v_ref[bb, :, p * LANES:(p + 1) * LANES].T
                top, bot = vt[:HD_B], vt[HD_B:]
                for s in range(2):
                    if s == 0:
                        t2 = jnp.where(lane < HD_B, top, pltpu.roll(bot, HD_B, 1))
                    else:
                        t2 = jnp.where(lane < HD_B, pltpu.roll(top, HD_B, 1), bot)
                    vt_scr[d, bb, p, s] = t2

    ones2 = ones_ref[...]
    row8 = lax.broadcasted_iota(jnp.int32, (SUBLANES, LANES), 0)
    lane8 = lax.broadcasted_iota(jnp.int32, (SUBLANES, LANES), 1)
    sel_r = jnp.logical_or(jnp.logical_and(row8 % 2 == 0, lane8 < HD_B),
                           jnp.logical_and(row8 % 2 == 1, lane8 >= HD_B))

    def row_of(rev, tt):
        return RW_TB - 1 - tt if rev else tt

    def emit_output(d, bb, tau):
        r_ref, o_ref = dirs[d][1], dirs[d][6]
        r = r_ref[bb, pl.ds(tau, 1), :]
        r8 = jnp.zeros((SUBLANES, LANES), F32)
        for p in range(n_pair):
            rp = jnp.broadcast_to(r[:, p * LANES:(p + 1) * LANES], (SUBLANES, LANES))
            r8 = jnp.where(jnp.logical_and(sel_r, row8 // 2 == p), rp, r8)
        s_all = jnp.concatenate([_bf(s_scr[d, bb, p]) for p in range(n_pair)], axis=0)
        o8 = _dot_nt(_bf(r8), s_all)
        o_parts = []
        for p in range(n_pair):
            for h in range(2):
                o_parts.append(o8[2 * p + h:2 * p + h + 1, p * HD_B:(p + 1) * HD_B])
        o_ref[bb, pl.ds(tau, 1), :] = jnp.concatenate(o_parts, axis=1)

    groups = [(d, bbs) for d in range(2) for bbs in _chunks(range(RW_BB), RW_GROUP_BB)]

    def step(tt, carry):
        tt_prev = jnp.maximum(tt - 1, 0)
        reds = []
        for d, bbs in groups:
            rev = dirs[d][7]
            tau = row_of(rev, tt)
            sub = tau // half
            lt = tau % half
            mask = jnp.logical_or(lane == lt, lane == lt + HD_B)
            lhs = []
            for bb in bbs:
                emit_output(d, bb, row_of(rev, tt_prev))
                nkk = dirs[d][0][bb, pl.ds(tau, 1), :]
                for p in range(n_pair):
                    prod = s_scr[d, bb, p] * nkk[:, p * LANES:(p + 1) * LANES]
                    lhs.append(jnp.concatenate([_bf(prod), _bf(jnp.where(mask, vt_scr[d, bb, p, sub], 0.0))],
                                               axis=1))
            reds.append(jnp.dot(jnp.concatenate(lhs, axis=0), ones2, preferred_element_type=F32))
        for (d, bbs), red in zip(groups, reds):
            _, _, _, w_ref, ka_ref, kd_ref, _, rev = dirs[d]
            tau = row_of(rev, tt)
            for k, bb in enumerate(bbs):
                w = w_ref[bb, pl.ds(tau, 1), :]
                ka = ka_ref[bb, pl.ds(tau, 1), :]
                kd = kd_ref[bb, pl.ds(tau, 1), :]
                for p in range(n_pair):
                    sl = slice(p * LANES, (p + 1) * LANES)
                    r0 = (k * n_pair + p) * HD_B
                    sa = red[r0:r0 + HD_B, :LANES]
                    vcol = red[r0:r0 + HD_B, LANES:]
                    s_scr[d, bb, p] = s_scr[d, bb, p] * w[:, sl] + sa * ka[:, sl] + vcol * kd[:, sl]
        return carry

    lax.fori_loop(0, RW_TB, step, 0)
    for d in range(2):
        for bb in range(RW_BB):
            emit_output(d, bb, row_of(dirs[d][7], RW_TB - 1))

    @pl.when(tb == n_tb - 1)
    def _():
        sff_ref[...] = s_scr[0]
        sfb_ref[...] = s_scr[1]


def _rwkv_scan(pp, row0, n_seq, seq, s0_f, s0_b, ones_pair):
    n_tb = seq // RW_TB
    n_pair = H_B // 2
    blk0 = row0 // seq
    view = lambda a: a.reshape(a.shape[0] // seq, seq, W_B)
    fwd = pl.BlockSpec((RW_BB, RW_TB, W_B), lambda b, t: (blk0 // RW_BB + b, t, 0))
    bwd = pl.BlockSpec((RW_BB, RW_TB, W_B), lambda b, t: (blk0 // RW_BB + b, n_tb - 1 - t, 0))
    st = pl.BlockSpec((RW_BB, n_pair, HD_B, LANES), lambda b, t: (b, 0, 0, 0))
    o_f = pl.BlockSpec((RW_BB, RW_TB, W_B), lambda b, t: (b, t, 0))
    o_b = pl.BlockSpec((RW_BB, RW_TB, W_B), lambda b, t: (b, n_tb - 1 - t, 0))
    ins_f = [view(pp[k]) for k in ('nkk', 'r', 'v', 'w_f', 'ka_f', 'kd_f')]
    ins_b = [view(pp[k]) for k in ('nkk', 'r', 'v', 'w_b', 'ka_b', 'kd_b')]
    st_shape = jax.ShapeDtypeStruct((n_seq, n_pair, HD_B, LANES), F32)
    o_shape = jax.ShapeDtypeStruct((n_seq, seq, W_B), F32)
    return pl.pallas_call(
        functools.partial(_rwkv_scan_kernel, n_tb),
        grid=(n_seq // RW_BB, n_tb),
        in_specs=[fwd] * 6 + [bwd] * 6 + [st, st, pl.BlockSpec(ones_pair.shape, lambda b, t: (0, 0))],
        out_specs=[o_f, o_b, st, st],
        out_shape=[o_shape, o_shape, st_shape, st_shape],
        scratch_shapes=[pltpu.VMEM((2, RW_BB, n_pair, HD_B, LANES), F32),
                        pltpu.VMEM((2, RW_BB, n_pair, 2, HD_B, LANES), F32)],
        compiler_params=_cparams(2),
        name="rwkv_scan",
    )(*ins_f, *ins_b, s0_f, s0_b, ones_pair)


def _state_to_pairs(s):
    b = s.shape[0]
    return s.reshape(b, H_B // 2, 2, HD_B, HD_B).transpose(0, 1, 3, 2, 4).reshape(b, H_B // 2, HD_B, 2 * HD_B)


def _pairs_to_state(s):
    b = s.shape[0]
    return s.reshape(b, H_B // 2, HD_B, 2, HD_B).transpose(0, 1, 3, 2, 4).reshape(b, H_B, HD_B, HD_B)


def _tail(x, y, g1, n2g, sc2, sh2, x1_ref, h_ref):
    x1 = x + g1 * y
    x1_ref[...] = x1
    h_ref[...] = _bf(_rms_mod(x1, n2g, sc2, sh2))


def _outproj0_kernel(x_ref, oa_ref, of_ref, ob_ref, bonus_ref, g_ref, lnw_ref, lnb_ref, ones_ref,
                     w_ref, g1_ref, n2g_ref, sc2_ref, sh2_ref, x1_ref, h_ref):
    o_sum = of_ref[...] + ob_ref[...]
    ones2 = ones_ref[...]
    mean = _seg_sum(o_sum, ones2) * (1.0 / HD_B)
    cen = o_sum - mean
    var = _seg_sum(cen * cen, ones2) * (1.0 / HD_B)
    gn = cen * lax.rsqrt(var + GN_EPS) * lnw_ref[...] + lnb_ref[...]
    o_rw = (gn + bonus_ref[...]) * g_ref[...]
    mix = jnp.concatenate([_bf(oa_ref[...]), _bf(o_rw)], axis=1)
    y = jnp.dot(mix, w_ref[...], preferred_element_type=F32)
    _tail(x_ref[...], y, g1_ref[...], n2g_ref[...], sc2_ref[...], sh2_ref[...], x1_ref, h_ref)


def _outproj0(rows, x, o_att, o_f, o_b, pp, pr, ones_b, w_out_bf, n2g, mod4, layer):
    n = rows.n
    tok = lambda w: pl.BlockSpec((TM, w), lambda i: (i, 0))
    const = lambda a: pl.BlockSpec(a.shape, lambda i: (0,) * a.ndim)
    lnw, lnb, n2 = pr['ln_w'].reshape(1, -1), pr['ln_b'].reshape(1, -1), n2g.reshape(1, -1)
    return pl.pallas_call(
        _outproj0_kernel,
        grid=(n // TM,),
        in_specs=[tok(D_MODEL), tok(W_A), tok(W_B), tok(W_B), tok(W_B), tok(W_B),
                  const(lnw), const(lnb), const(ones_b), const(w_out_bf),
                  rows.mod_spec(layer, 2, TM), const(n2), rows.mod_spec(layer, 4, TM), rows.mod_spec(layer, 3, TM)],
        out_specs=[tok(D_MODEL), tok(D_MODEL)],
        out_shape=[jax.ShapeDtypeStruct((n, D_MODEL), F32), jax.ShapeDtypeStruct((n, D_MODEL), BF16)],
        compiler_params=_cparams(1),
        name="outproj0",
    )(x, o_att, o_f, o_b, pp['bonus'], pp['g'], lnw, lnb, ones_b, w_out_bf, mod4, n2, mod4, mod4)


def _outproj1_kernel(x_ref, of_ref, ob_ref, gate_ref, ng_ref, w_ref, g1_ref, n2g_ref, sc2_ref, sh2_ref,
                     x1_ref, h_ref):
    o_sum = of_ref[...] + ob_ref[...]
    parts = []
    for h in range(H_C):
        oh = o_sum[:, h * DV_C:(h + 1) * DV_C]
        parts.append(oh * lax.rsqrt(jnp.mean(oh * oh, axis=-1, keepdims=True) + EPS))
    o = jnp.concatenate(parts, axis=1) * ng_ref[...] * _silu(gate_ref[...])
    y = jnp.dot(_bf(o), w_ref[...], preferred_element_type=F32)
    _tail(x_ref[...], y, g1_ref[...], n2g_ref[...], sc2_ref[...], sh2_ref[...], x1_ref, h_ref)


def _outproj1(rows, x, o_f, o_b, p1, norm_g, w_out_bf, n2g, mod4, layer):
    n = rows.n
    tok = lambda w: pl.BlockSpec((TM, w), lambda i: (i, 0))
    const = lambda a: pl.BlockSpec(a.shape, lambda i: (0,) * a.ndim)
    ng, n2 = norm_g.reshape(1, -1), n2g.reshape(1, -1)
    return pl.pallas_call(
        _outproj1_kernel,
        grid=(n // TM,),
        in_specs=[tok(D_MODEL), tok(D_C), tok(D_C), pl.BlockSpec((TM, D_C), lambda i: (i, 4)),
                  const(ng), const(w_out_bf),
                  rows.mod_spec(layer, 2, TM), const(n2), rows.mod_spec(layer, 4, TM), rows.mod_spec(layer, 3, TM)],
        out_specs=[tok(D_MODEL), tok(D_MODEL)],
        out_shape=[jax.ShapeDtypeStruct((n, D_MODEL), F32), jax.ShapeDtypeStruct((n, D_MODEL), BF16)],
        compiler_params=_cparams(1),
        name="outproj1",
    )(x, o_f, o_b, p1, ng, w_out_bf, mod4, n2, mod4, mod4)


def _hgrn_kernel(n_tb, qf_ref, ff_ref, if_ref, qb_ref, fb_ref, ib_ref, lbp_ref, s0f_ref, s0b_ref,
                 trif_ref, trib_ref, of_ref, ob_ref, sff_ref, sfb_ref, s_scr):
    tb = pl.program_id(1)

    @pl.when(tb == 0)
    def _():
        for h in range(H_C):
            s_scr[0, h] = s0f_ref[0, h].T
            s_scr[1, h] = s0b_ref[0, h].T

    lbp = lbp_ref[...]
    e = jnp.exp(lbp - jnp.max(lbp, axis=0, keepdims=True))
    sm = e / jnp.sum(e, axis=0, keepdims=True)
    lb = (sm[0:1] + sm[1:2]) - sm[0:1]

    n_chunk = HG_TB // CHUNK
    ti = lax.broadcasted_iota(jnp.int32, (CHUNK, CHUNK), 0)
    si = lax.broadcasted_iota(jnp.int32, (CHUNK, CHUNK), 1)
    dirs = ((qf_ref, ff_ref, if_ref, of_ref, trif_ref, ti >= si, CHUNK - 1, False),
            (qb_ref, fb_ref, ib_ref, ob_ref, trib_ref, ti <= si, 0, True))

    def chunk(c, carry):
        for d, (q_ref, f_ref, i_ref, o_ref, tri_ref, causal, last_row, rev) in enumerate(dirs):
            cc = n_chunk - 1 - c if rev else c
            r0 = pl.multiple_of(cc * CHUNK, CHUNK)
            tri3 = tri_ref[...]
            for h in range(H_C):
                sl = slice(h * DK_C, (h + 1) * DK_C)
                q = _silu(q_ref[pl.ds(r0, CHUNK), sl])
                f = lb[:, sl] + (1.0 - lb[:, sl]) * _sigmoid(f_ref[pl.ds(r0, CHUNK), sl])
                k = 1.0 - f
                v = _bf(i_ref[pl.ds(r0, CHUNK), sl])
                g = jnp.log(f)
                g1 = _bf(g)
                g2 = _bf(g - g1.astype(F32))
                g3 = _bf(g - g1.astype(F32) - g2.astype(F32))
                b = jnp.dot(tri3, jnp.concatenate([g1, g2, g3], axis=0), preferred_element_type=F32)
                b_last = b[last_row:last_row + 1]
                q_in = _bf(q * jnp.exp(b))
                k_in = _bf(k * jnp.exp(-b))
                k_out = _bf(k * jnp.exp(b_last - b))
                att = jnp.where(causal, _dot_nt(q_in, k_in), 0.0)
                s_t = s_scr[d, h]
                o = jnp.dot(_bf(att), v, preferred_element_type=F32) + _dot_nt(q_in, _bf(s_t))
                o_ref[pl.ds(r0, CHUNK), sl] = o
                s_scr[d, h] = jnp.exp(b_last) * s_t + _dot_tn(v, k_out)
        return carry

    lax.fori_loop(0, n_chunk, chunk, 0)

    @pl.when(tb == n_tb - 1)
    def _():
        for h in range(H_C):
            sff_ref[0, h] = s_scr[0, h].T
            sfb_ref[0, h] = s_scr[1, h].T


def _hgrn_scan(p1, row0, n_seq, seq, lb_params, s0_f, s0_b):
    n_tb = seq // HG_TB
    blk0 = row0 // HG_TB
    tri = np.tril(np.ones((CHUNK, CHUNK), np.float32))
    tri_f = jnp.asarray(np.concatenate([tri] * 3, axis=1), dtype=BF16)
    tri_b = jnp.asarray(np.concatenate([tri.T] * 3, axis=1), dtype=BF16)
    fwd = lambda col: pl.BlockSpec((HG_TB, D_C), lambda b, t: (blk0 + b * n_tb + t, col))
    bwd = lambda col: pl.BlockSpec((HG_TB, D_C), lambda b, t: (blk0 + b * n_tb + n_tb - 1 - t, col))
    st = pl.BlockSpec((1, H_C, DK_C, DV_C), lambda b, t: (b, 0, 0, 0))
    const = lambda a: pl.BlockSpec(a.shape, lambda b, t: (0,) * a.ndim)
    o_shape = jax.ShapeDtypeStruct((n_seq * seq, D_C), F32)
    st_shape = jax.ShapeDtypeStruct((n_seq, H_C, DK_C, DV_C), F32)
    return pl.pallas_call(
        functools.partial(_hgrn_kernel, n_tb),
        grid=(n_seq, n_tb),
        in_specs=[fwd(0), fwd(1), fwd(3), bwd(0), bwd(2), bwd(3), const(lb_params), st, st,
                  const(tri_f), const(tri_b)],
        out_specs=[pl.BlockSpec((HG_TB, D_C), lambda b, t: (b * n_tb + t, 0)),
                   pl.BlockSpec((HG_TB, D_C), lambda b, t: (b * n_tb + n_tb - 1 - t, 0)), st, st],
        out_shape=[o_shape, o_shape, st_shape, st_shape],
        scratch_shapes=[pltpu.VMEM((2, H_C, DV_C, DK_C), F32)],
        compiler_params=_cparams(2),
        name="hgrn_scan",
    )(p1, p1, p1, p1, p1, p1, lb_params, s0_f, s0_b, tri_f, tri_b)


def _router_kernel(h_ref, rhi_ref, rlo_ref, bias_ref, gates_ref):
    x = h_ref[...]
    tm = x.shape[0]
    logits = _dot_nt(rhi_ref[...], x) + _dot_nt(rlo_ref[...], x)
    scores = _sigmoid(logits)
    biased = scores + bias_ref[...]
    per = N_EXPERTS // N_GROUPS
    sub = lax.broadcasted_iota(jnp.int32, (per, tm), 0)
    gs_rows = []
    for g in range(N_GROUPS):
        blk = biased[g * per:(g + 1) * per]
        m1 = jnp.max(blk, axis=0, keepdims=True)
        first = jnp.min(jnp.where(blk == m1, sub, per), axis=0, keepdims=True)
        m2 = jnp.max(jnp.where(sub == first, -jnp.inf, blk), axis=0, keepdims=True)
        gs_rows.append(m1 + m2)
    gs = jnp.concatenate(gs_rows, axis=0)
    gi = lax.broadcasted_iota(jnp.int32, gs.shape, 0)
    rank = jnp.zeros(gs.shape, jnp.int32)
    for s in range(1, N_GROUPS):
        other = pltpu.roll(gs, s, 0)
        oi = pltpu.roll(gi, s, 0)
        beats = jnp.logical_or(other > gs, jnp.logical_and(other == gs, oi < gi))
        rank = rank + jnp.where(beats, 1, 0)
    keep = jnp.where(rank < TOPK_GROUPS, 1.0, 0.0)
    emask = jnp.concatenate([jnp.broadcast_to(keep[g:g + 1], (per, tm)) for g in range(N_GROUPS)], axis=0)
    cur = jnp.where(emask > 0.0, biased, -jnp.inf)
    ei = lax.broadcasted_iota(jnp.int32, cur.shape, 0)
    w = jnp.zeros(cur.shape, F32)
    for _ in range(TOP_K):
        m = jnp.max(cur, axis=0, keepdims=True)
        idx = jnp.min(jnp.where(cur == m, ei, N_EXPERTS), axis=0, keepdims=True)
        pick = ei == idx
        w = jnp.where(pick, scores, w)
        cur = jnp.where(pick, -jnp.inf, cur)
    gates_ref[...] = w / jnp.sum(w, axis=0, keepdims=True) * ROUTED_SCALE


def _router(hffn, router, bias):
    n = hffn.shape[0]
    r_t = router.T
    r_hi = _bf(r_t)
    r_lo = _bf(r_t - r_hi.astype(F32))
    const = lambda a: pl.BlockSpec(a.shape, lambda i: (0,) * a.ndim)
    b_col = bias.reshape(N_EXPERTS, 1)
    return pl.pallas_call(
        _router_kernel,
        grid=(n // TM,),
        in_specs=[pl.BlockSpec((TM, D_MODEL), lambda i: (i, 0)), const(r_hi), const(r_lo), const(b_col)],
        out_specs=pl.BlockSpec((N_EXPERTS, TM), lambda i: (0, i)),
        out_shape=jax.ShapeDtypeStruct((N_EXPERTS, n), F32),
        compiler_params=_cparams(1),
        name="router",
    )(hffn, r_hi, r_lo, b_col)


def _glu(x, wg, wu):
    hg = jnp.dot(x, wg, preferred_element_type=F32)
    hu = jnp.dot(x, wu, preferred_element_type=F32)
    return _silu(hg) * hu


def _moe_dense_kernel(h_ref, gates_ref, wg_ref, wu_ref, wd_ref, sg_ref, su_ref, sd_ref, x1_ref, g2_ref,
                      o_ref, acc_ref):
    e = pl.program_id(1)
    x = h_ref[...]

    @pl.when(e == 0)
    def _():
        act = _glu(x, sg_ref[...], su_ref[...])
        acc_ref[...] = jnp.dot(_bf(act), sd_ref[...], preferred_element_type=F32)

    act = _glu(x, wg_ref[0], wu_ref[0])
    wd = wd_ref[0]
    gates = gates_ref[...]
    ghi = _bf(gates)
    glo = _bf(gates - ghi.astype(F32))
    pick = (lax.broadcasted_iota(jnp.int32, (2 * N_EXPERTS, D_EXPERT), 0) % N_EXPERTS == e).astype(BF16)
    gcol = _dot_tn(jnp.concatenate([ghi, glo], axis=0), pick)
    acc_ref[...] += jnp.dot(_bf(act * gcol), wd, preferred_element_type=F32)

    @pl.when(e == N_EXPERTS - 1)
    def _():
        o_ref[...] = x1_ref[...] + g2_ref[...] * acc_ref[...]


def _moe_dense(rows, hffn, gates_t, mp, x1, mod4, layer):
    n = rows.n
    tm = MOE_TM
    const = lambda a: pl.BlockSpec(a.shape, lambda i, e: (0,) * a.ndim)
    tok = lambda w: pl.BlockSpec((tm, w), lambda i, e: (i, 0))
    return pl.pallas_call(
        _moe_dense_kernel,
        grid=(n // tm, N_EXPERTS),
        in_specs=[tok(D_MODEL), pl.BlockSpec((N_EXPERTS, tm), lambda i, e: (0, i)),
                  pl.BlockSpec((1, D_MODEL, D_EXPERT), lambda i, e: (e, 0, 0)),
                  pl.BlockSpec((1, D_MODEL, D_EXPERT), lambda i, e: (e, 0, 0)),
                  pl.BlockSpec((1, D_EXPERT, D_MODEL), lambda i, e: (e, 0, 0)),
                  const(mp['sg']), const(mp['su']), const(mp['sd']),
                  tok(D_MODEL), rows.mod_spec(layer, 5, tm)],
        out_specs=tok(D_MODEL),
        out_shape=jax.ShapeDtypeStruct((n, D_MODEL), F32),
        scratch_shapes=[pltpu.VMEM((tm, D_MODEL), F32)],
        compiler_params=_cparams(2),
        name=f"moe_dense{layer}",
    )(hffn, gates_t, mp['wg'], mp['wu'], mp['wd'], mp['sg'], mp['su'], mp['sd'], x1, mod4)


def kernel(x_prompt, x_sample, c, c_ctx, cache_attn_k, cache_attn_v, state_rwkv_fwd, state_rwkv_bwd,
           state_hgrn_fwd, state_hgrn_bwd, norm1_g, norm2_g, mod_w, mod_b, ab_w_in, ab_w_out, attn_q_norm,
           attn_k_norm, attn_sink, rwkv_mu, rwkv_w0, rwkv_w2, rwkv_a0, rwkv_a2, rwkv_g2, rwkv_k_k, rwkv_k_a,
           rwkv_r_k, rwkv_ln_w, rwkv_ln_b, hgrn_w_in, hgrn_w_out, hgrn_lower_bounds, hgrn_norm_g, moe_router,
           moe_bias, moe_w_gate, moe_w_up, moe_w_down, moe_shared_gate, moe_shared_up, moe_shared_down):
    n_cseq, cseq, _ = x_prompt.shape
    n_lseq, lseq, _ = x_sample.shape
    depth = mod_w.shape[0]
    assert depth == 2 and n_lseq + 1 <= SUBLANES
    assert cseq == TM and lseq % TM == 0 and lseq % HG_TB == 0 and cseq % HG_TB == 0
    assert n_cseq % RW_BB == 0 and n_lseq % RW_BB == 0 and (n_cseq * cseq) % (lseq * RW_BB) == 0
    rows = _Rows(n_cseq * cseq, n_lseq * lseq, lseq)
    assert rows.n % MOE_TM == 0 and lseq % MOE_TM == 0 and rows.n_ctx % MOE_TM == 0
    kv_w = KV_A * HD_A

    x = jnp.concatenate([x_prompt.reshape(rows.n_ctx, D_MODEL), x_sample.reshape(rows.n_lat, D_MODEL)], axis=0)
    cvecs = jnp.concatenate([c_ctx[None, :], c, jnp.zeros((SUBLANES - 1 - n_lseq, D_MODEL), F32)], axis=0)
    mod4 = _modulation(cvecs, mod_w, mod_b).reshape(depth, SUBLANES, 1, 6 * D_MODEL)

    ones_q = _block_ones(W_A, HD_A)
    ones_k = _block_ones(kv_w, HD_A)
    ones_b = _block_ones(W_B, HD_B)
    ones_pair = _block_ones(LANES, HD_B)[:LANES]
    ones_pair = jnp.kron(jnp.eye(2, dtype=BF16), ones_pair)
    cos_t, sin_t = _rope_tables(lseq)

    def moe(l, hffn, x1):
        mp = {'wg': _bf(moe_w_gate[l]), 'wu': _bf(moe_w_up[l]), 'wd': _bf(moe_w_down[l]),
              'sg': _bf(moe_shared_gate[l]), 'su': _bf(moe_shared_up[l]), 'sd': _bf(moe_shared_down[l])}
        gates_t = _router(hffn, moe_router[l], moe_bias[l])
        return _moe_dense(rows, hffn, gates_t, mp, x1, mod4, l)

    pr = {'mu': rwkv_mu[0], 'w0': rwkv_w0[0], 'w2': rwkv_w2[0], 'a0': rwkv_a0[0], 'a2': rwkv_a2[0],
          'g2': rwkv_g2[0], 'k_k': rwkv_k_k[0], 'k_a': rwkv_k_a[0], 'r_k': rwkv_r_k[0].reshape(-1),
          'ln_w': rwkv_ln_w[0], 'ln_b': rwkv_ln_b[0]}
    p_att, p_rw = _inproj(rows, x, norm1_g[0], mod4, 0, _bf(ab_w_in[0]), (ATT_IN, RWKV_IN))
    qg_t = jnp.tile(attn_q_norm[0], H_A).reshape(1, W_A)
    kg_t = jnp.tile(attn_k_norm[0], KV_A).reshape(1, kv_w)
    o_att_c, new_k, new_v = _ctx_attention(p_att, n_cseq, cseq, qg_t, kg_t, attn_sink[0], ones_q, ones_k)
    past = cache_attn_k.shape[2]
    o_att_l = _lat_attention(p_att, rows.n_ctx // lseq, n_lseq, lseq, qg_t, kg_t, attn_sink[0], ones_q, ones_k,
                             cos_t, sin_t, cache_attn_k[:, 0].reshape(n_lseq, past, kv_w),
                             cache_attn_v[:, 0].reshape(n_lseq, past, kv_w))
    o_att = jnp.concatenate([o_att_c, o_att_l], axis=0)

    pp = _rwkv_prep(rows, p_rw, pr, ones_b)
    zero_st = jnp.zeros((n_cseq, H_B // 2, HD_B, LANES), F32)
    of_c, ob_c, sf_c, sb_c = _rwkv_scan(pp, 0, n_cseq, cseq, zero_st, zero_st, ones_pair)
    of_l, ob_l, _, _ = _rwkv_scan(pp, rows.n_ctx, n_lseq, lseq, _state_to_pairs(state_rwkv_fwd[:, 0]),
                                  _state_to_pairs(state_rwkv_bwd[:, 0]), ones_pair)
    o_f = jnp.concatenate([of_c.reshape(rows.n_ctx, W_B), of_l.reshape(rows.n_lat, W_B)], axis=0)
    o_b = jnp.concatenate([ob_c.reshape(rows.n_ctx, W_B), ob_l.reshape(rows.n_lat, W_B)], axis=0)
    x1, hffn = _outproj0(rows, x, o_att, o_f, o_b, pp, pr, ones_b, _bf(ab_w_out[0]), norm2_g[0], mod4, 0)
    x = moe(0, hffn, x1)

    (p1,) = _inproj(rows, x, norm1_g[1], mod4, 1, _bf(hgrn_w_in[0]), (IN_C,))
    zero_h = jnp.zeros((n_cseq, H_C, DK_C, DV_C), F32)
    hf_c, hb_c, hsf_c, hsb_c = _hgrn_scan(p1, 0, n_cseq, cseq, hgrn_lower_bounds, zero_h, zero_h)
    hf_l, hb_l, _, _ = _hgrn_scan(p1, rows.n_ctx, n_lseq, lseq, hgrn_lower_bounds,
                                  state_hgrn_fwd[:, 0], state_hgrn_bwd[:, 0])
    h_f = jnp.concatenate([hf_c, hf_l], axis=0)
    h_b = jnp.concatenate([hb_c, hb_l], axis=0)
    x1, hffn = _outproj1(rows, x, h_f, h_b, p1, hgrn_norm_g[0], _bf(hgrn_w_out[0]), norm2_g[1], mod4, 1)
    x = moe(1, hffn, x1)

    y_prompt = x[:rows.n_ctx].reshape(n_cseq, cseq, D_MODEL)
    y_sample = x[rows.n_ctx:].reshape(n_lseq, lseq, D_MODEL)
    return (y_prompt, y_sample,
            new_k.reshape(n_cseq, 1, cseq, KV_A, HD_A), new_v.reshape(n_cseq, 1, cseq, KV_A, HD_A),
            _pairs_to_state(sf_c)[:, None], _pairs_to_state(sb_c)[:, None],
            hsf_c[:, None], hsb_c[:, None])
```

```python
import functools

import numpy as np
import jax
import jax.numpy as jnp
from jax import lax
from jax.experimental import pallas as pl
from jax.experimental.pallas import tpu as pltpu
from jax.experimental.pallas import tpu_sc as plsc

F32 = jnp.float32
BF16 = jnp.bfloat16

D_MODEL = 1024
GRID_W = 64
H_A = 8
KV_A = 2
G_A = H_A // KV_A
HD_A = 64
W_A = H_A * HD_A
WINDOW = 128
QBLK = 128
ROPE_BASE = 10000.0
ATTN_SCALE = HD_A ** -0.5
NEG_INF = -1e30
H_B = 8
HD_B = 64
W_B = H_B * HD_B
LORA_W = 64
LORA_A = 64
LORA_G = 128
GN_EPS = 64e-5
ATT_IN = W_A + 2 * KV_A * HD_A
RWKV_IN = 3 * W_B + LORA_W + LORA_A + LORA_G
IN_AB = ATT_IN + RWKV_IN
H_C = 8
DK_C = 128
DV_C = 128
D_C = H_C * DV_C
CHUNK = 64
IN_C = 5 * D_C
N_EXPERTS = 64
TOP_K = 8
N_GROUPS = 8
TOPK_GROUPS = 4
D_EXPERT = 256
ROUTED_SCALE = 2.5
EPS = 1e-6

LANES = 128
SUBLANES = 8
VMEM_LIMIT = 52 * 1024 * 1024

TM = 256
RW_TB = 128
RW_BB = 4
RW_GROUP_BB = 4
HG_TB = 256
MOE_TILE = 256
POS_TB = 512
SC_CORES = 2
SC_SUBCORES = 16
SC_CHUNK = 64


def _cparams(n_axes):
    return pltpu.CompilerParams(dimension_semantics=("arbitrary",) * n_axes,
                                vmem_limit_bytes=VMEM_LIMIT)


def _bf(x):
    return x.astype(BF16)


def _split2(x):
    hi = lax.bitcast_convert_type(
        lax.bitcast_convert_type(x, jnp.uint32) & jnp.uint32(0xFFFF0000), F32)
    return hi, x - hi


def _seg_sum(x, ones2):
    hi, lo = _split2(x)
    return jnp.dot(jnp.concatenate([_bf(hi), _bf(lo)], axis=1), ones2,
                   preferred_element_type=F32)


def _dot_nt(a, b):
    return lax.dot_general(a, b, (((1,), (1,)), ((), ())), preferred_element_type=F32)


def _dot_tn(a, b):
    return lax.dot_general(a, b, (((0,), (0,)), ((), ())), preferred_element_type=F32)


def _sigmoid(x):
    return 1.0 / (1.0 + jnp.exp(-x))


def _silu(x):
    return x * _sigmoid(x)


def _chunks(seq, n):
    seq = list(seq)
    return [seq[i:i + n] for i in range(0, len(seq), n)]


def _block_ones(width, seg):
    idx = np.arange(width) // seg
    bd = (idx[:, None] == idx[None, :]).astype(np.float32)
    return jnp.asarray(np.concatenate([bd, bd], axis=0), dtype=BF16)


def _mod_kernel(c_ref, w_ref, b_ref, o_ref):
    s = _silu(c_ref[...])
    o_ref[0] = jnp.dot(_bf(s), _bf(w_ref[0]), preferred_element_type=F32) + b_ref[0]


def _modulation(cvecs, mod_w, mod_b):
    depth = mod_w.shape[0]
    n_col = 6 * D_MODEL // D_MODEL
    return pl.pallas_call(
        _mod_kernel,
        grid=(depth, n_col),
        in_specs=[pl.BlockSpec((SUBLANES, D_MODEL), lambda l, j: (0, 0)),
                  pl.BlockSpec((1, D_MODEL, D_MODEL), lambda l, j: (l, 0, j)),
                  pl.BlockSpec((1, 1, D_MODEL), lambda l, j: (l, 0, j))],
        out_specs=pl.BlockSpec((1, SUBLANES, D_MODEL), lambda l, j: (l, 0, j)),
        out_shape=jax.ShapeDtypeStruct((depth, SUBLANES, 6 * D_MODEL), F32),
        compiler_params=_cparams(2),
        name="modulation",
    )(cvecs, mod_w, mod_b.reshape(depth, 1, 6 * D_MODEL))


class _Rows:
    def __init__(self, n_ctx, n_lat, lat_seq):
        self.n_ctx, self.n_lat, self.lat_seq = n_ctx, n_lat, lat_seq
        self.n = n_ctx + n_lat

    def mod_row(self, i, tm):
        nctx_blk = self.n_ctx // tm
        per_seq = self.lat_seq // tm
        return jnp.where(i < nctx_blk, 0, 1 + (i - nctx_blk) // per_seq)

    def mod_spec(self, layer, chunk, tm):
        return pl.BlockSpec((None, None, 1, D_MODEL),
                            lambda i, *_: (layer, self.mod_row(i, tm), 0, chunk))


def _rms_mod(x, g, sc, sh):
    ms = jnp.mean(x * x, axis=-1, keepdims=True)
    return x * lax.rsqrt(ms + EPS) * g * (1.0 + sc) + sh


def _inproj_kernel(splits, x_ref, g_ref, sh_ref, sc_ref, w_ref, *o_refs):
    h = _rms_mod(x_ref[...], g_ref[...], sc_ref[...], sh_ref[...])
    p = jnp.dot(_bf(h), w_ref[...], preferred_element_type=F32)
    lo = 0
    for o_ref, width in zip(o_refs, splits):
        o_ref[...] = p[:, lo:lo + width]
        lo += width


def _inproj(rows, x, g, mod4, layer, w_bf, splits):
    n_out = w_bf.shape[1]
    return pl.pallas_call(
        functools.partial(_inproj_kernel, splits),
        grid=(rows.n // TM,),
        in_specs=[pl.BlockSpec((TM, D_MODEL), lambda i: (i, 0)),
                  pl.BlockSpec((1, D_MODEL), lambda i: (0, 0)),
                  rows.mod_spec(layer, 0, TM),
                  rows.mod_spec(layer, 1, TM),
                  pl.BlockSpec((D_MODEL, n_out), lambda i: (0, 0))],
        out_specs=[pl.BlockSpec((TM, wd), lambda i: (i, 0)) for wd in splits],
        out_shape=[jax.ShapeDtypeStruct((rows.n, wd), F32) for wd in splits],
        compiler_params=_cparams(1),
        name=f"inproj{layer}",
    )(x, g.reshape(1, D_MODEL), mod4, mod4, w_bf)


def _head_rms(x, gain_t, ones2):
    ms = _seg_sum(x * x, ones2) * (1.0 / HD_A)
    return x * lax.rsqrt(ms + EPS) * gain_t


def _sink_softmax_pv(parts, sink):
    m = jnp.maximum(functools.reduce(jnp.maximum, [jnp.max(s, axis=-1, keepdims=True) for s, _ in parts]), sink)
    den = jnp.exp(sink - m)
    acc = None
    for s, v in parts:
        p = jnp.exp(s - m)
        den = den + jnp.sum(p, axis=-1, keepdims=True)
        pv = jnp.dot(_bf(p), v, preferred_element_type=F32)
        acc = pv if acc is None else acc + pv
    return acc / den


def _ctx_attn_kernel(p_ref, qg_ref, kg_ref, sink_ref, ones_q_ref, ones_k_ref, o_ref, k_ref, v_ref):
    p = p_ref[...]
    q = _head_rms(p[:, :W_A], qg_ref[...], ones_q_ref[...]) * ATTN_SCALE
    k = _head_rms(p[:, W_A:W_A + KV_A * HD_A], kg_ref[...], ones_k_ref[...])
    v = p[:, W_A + KV_A * HD_A:ATT_IN]
    k_ref[0] = k
    v_ref[0] = v
    qb, kb, vb = _bf(q), _bf(k), _bf(v)
    outs = []
    for h in range(H_A):
        j = h // G_A
        s = _dot_nt(qb[:, h * HD_A:(h + 1) * HD_A], kb[:, j * HD_A:(j + 1) * HD_A])
        outs.append(_sink_softmax_pv([(s, vb[:, j * HD_A:(j + 1) * HD_A])], sink_ref[h]))
    o_ref[...] = jnp.concatenate(outs, axis=1)


def _ctx_attention(p_att, n_seq, seq, qg_t, kg_t, sink, ones_q, ones_k):
    kv_w = KV_A * HD_A
    return pl.pallas_call(
        _ctx_attn_kernel,
        grid=(n_seq,),
        in_specs=[pl.BlockSpec((seq, ATT_IN), lambda b: (b, 0)),
                  pl.BlockSpec((1, W_A), lambda b: (0, 0)),
                  pl.BlockSpec((1, kv_w), lambda b: (0, 0)),
                  pl.BlockSpec(memory_space=pltpu.SMEM),
                  pl.BlockSpec(ones_q.shape, lambda b: (0, 0)),
                  pl.BlockSpec(ones_k.shape, lambda b: (0, 0))],
        out_specs=[pl.BlockSpec((seq, W_A), lambda b: (b, 0)),
                   pl.BlockSpec((1, seq, kv_w), lambda b: (b, 0, 0)),
                   pl.BlockSpec((1, seq, kv_w), lambda b: (b, 0, 0))],
        out_shape=[jax.ShapeDtypeStruct((n_seq * seq, W_A), F32),
                   jax.ShapeDtypeStruct((n_seq, seq, kv_w), F32),
                   jax.ShapeDtypeStruct((n_seq, seq, kv_w), F32)],
        compiler_params=_cparams(1),
        name="ctx_attention",
    )(p_att, qg_t, kg_t, sink, ones_q, ones_k)


def _rope(x, cos_t, sin_t):
    lane = lax.broadcasted_iota(jnp.int32, cos_t.shape, 1)
    low = (lane % 32) < 16
    outs = []
    for s in range(x.shape[1] // LANES):
        xs = x[:, s * LANES:(s + 1) * LANES]
        partner = jnp.where(low, pltpu.roll(xs, LANES - 16, 1), pltpu.roll(xs, 16, 1))
        outs.append(xs * cos_t + partner * sin_t)
    return outs[0] if len(outs) == 1 else jnp.concatenate(outs, axis=1)


def _lat_attn_kernel(seq, p_ref, qg_ref, kg_ref, sink_ref, ones_q_ref, ones_k_ref, cos_ref, sin_ref,
                     kc_ref, vc_ref, o_ref, q_scr, k_scr, v_scr):
    kv_w = KV_A * HD_A
    p = p_ref[...]
    q = _head_rms(p[:, :W_A], qg_ref[...], ones_q_ref[...])
    k = _head_rms(p[:, W_A:W_A + kv_w], kg_ref[...], ones_k_ref[...])
    q_scr[...] = _bf(_rope(q, cos_ref[...], sin_ref[...]) * ATTN_SCALE)
    k_scr[...] = _bf(_rope(k, cos_ref[...], sin_ref[...]))
    v_scr[...] = _bf(p[:, W_A + kv_w:ATT_IN])
    kc = _bf(kc_ref[0])
    vc = _bf(vc_ref[0])
    n_local = 3 * QBLK

    def block(i, carry):
        q0 = pl.multiple_of(i * QBLK, QBLK)
        start = pl.multiple_of(jnp.clip((i - 1) * QBLK, 0, seq - n_local), QBLK)
        qb = q_scr[pl.ds(q0, QBLK), :]
        kl = k_scr[pl.ds(start, n_local), :]
        vl = v_scr[pl.ds(start, n_local), :]
        ipos = q0 + lax.broadcasted_iota(jnp.int32, (QBLK, n_local), 0)
        jpos = start + lax.broadcasted_iota(jnp.int32, (QBLK, n_local), 1)
        band = jnp.abs(jpos - ipos) <= WINDOW
        outs = []
        for h in range(H_A):
            j = h // G_A
            qh = qb[:, h * HD_A:(h + 1) * HD_A]
            s_loc = jnp.where(band, _dot_nt(qh, kl[:, j * HD_A:(j + 1) * HD_A]), NEG_INF)
            s_ctx = _dot_nt(qh, kc[:, j * HD_A:(j + 1) * HD_A])
            outs.append(_sink_softmax_pv([(s_loc, vl[:, j * HD_A:(j + 1) * HD_A]),
                                          (s_ctx, vc[:, j * HD_A:(j + 1) * HD_A])], sink_ref[h]))
        o_ref[pl.ds(q0, QBLK), :] = jnp.concatenate(outs, axis=1)
        return carry

    lax.fori_loop(0, seq // QBLK, block, 0)


def _lat_attention(p_att, row_blk0, n_seq, seq, qg_t, kg_t, sink, ones_q, ones_k, cos_t, sin_t, kc, vc):
    kv_w = KV_A * HD_A
    past = kc.shape[1]
    return pl.pallas_call(
        functools.partial(_lat_attn_kernel, seq),
        grid=(n_seq,),
        in_specs=[pl.BlockSpec((seq, ATT_IN), lambda b: (row_blk0 + b, 0)),
                  pl.BlockSpec((1, W_A), lambda b: (0, 0)),
                  pl.BlockSpec((1, kv_w), lambda b: (0, 0)),
                  pl.BlockSpec(memory_space=pltpu.SMEM),
                  pl.BlockSpec(ones_q.shape, lambda b: (0, 0)),
                  pl.BlockSpec(ones_k.shape, lambda b: (0, 0)),
                  pl.BlockSpec((seq, LANES), lambda b: (0, 0)),
                  pl.BlockSpec((seq, LANES), lambda b: (0, 0)),
                  pl.BlockSpec((1, past, kv_w), lambda b: (b, 0, 0)),
                  pl.BlockSpec((1, past, kv_w), lambda b: (b, 0, 0))],
        out_specs=pl.BlockSpec((seq, W_A), lambda b: (b, 0)),
        out_shape=jax.ShapeDtypeStruct((n_seq * seq, W_A), F32),
        scratch_shapes=[pltpu.VMEM((seq, W_A), BF16), pltpu.VMEM((seq, kv_w), BF16),
                        pltpu.VMEM((seq, kv_w), BF16)],
        compiler_params=_cparams(1),
        name="lat_attention",
    )(p_att, qg_t, kg_t, sink, ones_q, ones_k, cos_t, sin_t, kc, vc)


def _rope_tables(seq):
    pos = np.arange(seq)
    row = (pos // GRID_W).astype(np.float32)
    col = (pos % GRID_W).astype(np.float32)
    d_axis = HD_A // 2
    inv = (ROPE_BASE ** (-np.arange(0, d_axis, 2, dtype=np.float32) / d_axis)).astype(np.float32)
    cos_h = np.zeros((seq, HD_A), np.float32)
    sin_h = np.zeros((seq, HD_A), np.float32)
    for seg, p_ in enumerate((row, col)):
        ang = (p_[:, None] * inv[None, :]).astype(np.float32)
        c, s = np.cos(ang), np.sin(ang)
        base = seg * d_axis
        cos_h[:, base:base + d_axis // 2] = c
        cos_h[:, base + d_axis // 2:base + d_axis] = c
        sin_h[:, base:base + d_axis // 2] = -s
        sin_h[:, base + d_axis // 2:base + d_axis] = s
    rep = LANES // HD_A
    return jnp.asarray(np.tile(cos_h, (1, rep))), jnp.asarray(np.tile(sin_h, (1, rep)))


def _rwkv_prep_kernel(rows, x_ref, prev_ref, next_ref, mu_ref, kk_ref, ka_ref, rk_ref, w0_ref, w2_ref,
                      a0_ref, a2_ref, g2_ref, ones_ref,
                      nkk_ref, r_ref, v_ref, g_ref, bonus_ref,
                      wf_ref, kaf_ref, kdf_ref, wb_ref, kab_ref, kdb_ref):
    i = pl.program_id(0)
    nctx_blk = rows.n_ctx // TM
    per_seq = rows.lat_seq // TM
    is_ctx = i < nctx_blk
    first = jnp.logical_or(is_ctx, (i - nctx_blk) % per_seq == 0)
    last = jnp.logical_or(is_ctx, (i - nctx_blk) % per_seq == per_seq - 1)
    x = x_ref[...]
    ridx = lax.broadcasted_iota(jnp.int32, x.shape, 0)
    prev_row = jnp.where(first, 0.0, prev_ref[SUBLANES - 1:SUBLANES, :])
    next_row = jnp.where(last, 0.0, next_ref[0:1, :])
    xm1 = jnp.where(ridx == 0, prev_row, pltpu.roll(x, 1, 0))
    xp1 = jnp.where(ridx == TM - 1, next_row, pltpu.roll(x, TM - 1, 0))
    pw = x + (0.5 * (xm1 + xp1) - x) * mu_ref[...]

    r = pw[:, 0:W_B]
    k = pw[:, W_B:2 * W_B]
    v = pw[:, 2 * W_B:3 * W_B]
    wd = pw[:, 3 * W_B:3 * W_B + LORA_W]
    ad = pw[:, 3 * W_B + LORA_W:3 * W_B + LORA_W + LORA_A]
    gd = pw[:, 3 * W_B + LORA_W + LORA_A:]
    ones2 = ones_ref[...]

    kk = k * kk_ref[...]
    kk = kk / jnp.maximum(jnp.sqrt(_seg_sum(kk * kk, ones2)), 1e-12)
    nkk_ref[...] = -kk
    r_ref[...] = r
    v_ref[...] = v
    g_ref[...] = jnp.dot(_bf(_sigmoid(gd)), g2_ref[...], preferred_element_type=F32)
    tw = _bf(jnp.tanh(wd))
    adb = _bf(ad)
    bonus = jnp.zeros_like(r)
    for d, (w_o, ka_o, kd_o) in enumerate(((wf_ref, kaf_ref, kdf_ref), (wb_ref, kab_ref, kdb_ref))):
        z = -(w0_ref[d:d + 1, :] + jnp.dot(tw, w2_ref[d], preferred_element_type=F32))
        softplus = jnp.maximum(z, 0.0) + jnp.log(1.0 + jnp.exp(-jnp.abs(z)))
        w_o[...] = jnp.exp(-jnp.exp(-softplus - 0.5))
        a = _sigmoid(a0_ref[d:d + 1, :] + jnp.dot(adb, a2_ref[d], preferred_element_type=F32))
        kd = k * (1.0 + (a - 1.0) * ka_ref[...])
        ka_o[...] = kk * a
        kd_o[...] = kd
        bonus = bonus + _seg_sum(r * kd * rk_ref[...], ones2) * v
    bonus_ref[...] = bonus


def _rwkv_prep(rows, p_rw, pr, ones_b):
    n = rows.n
    n_halo = n // SUBLANES
    blk_halo = TM // SUBLANES
    row = lambda a: a.reshape(1, -1)
    full = lambda a: pl.BlockSpec(a.shape, lambda i: (0,) * a.ndim)
    consts = [row(pr['mu']), row(pr['k_k']), row(pr['k_a']), row(pr['r_k']), pr['w0'], _bf(pr['w2']),
              pr['a0'], _bf(pr['a2']), _bf(pr['g2']), ones_b]
    outs = pl.pallas_call(
        functools.partial(_rwkv_prep_kernel, rows),
        grid=(n // TM,),
        in_specs=[pl.BlockSpec((TM, RWKV_IN), lambda i: (i, 0)),
                  pl.BlockSpec((SUBLANES, RWKV_IN), lambda i: (jnp.maximum(i * blk_halo - 1, 0), 0)),
                  pl.BlockSpec((SUBLANES, RWKV_IN), lambda i: (jnp.minimum((i + 1) * blk_halo, n_halo - 1), 0))]
                 + [full(a) for a in consts],
        out_specs=[pl.BlockSpec((TM, W_B), lambda i: (i, 0))] * 11,
        out_shape=[jax.ShapeDtypeStruct((n, W_B), F32)] * 11,
        compiler_params=_cparams(1),
        name="rwkv_prep",
    )(p_rw, p_rw, p_rw, *consts)
    names = ('nkk', 'r', 'v', 'g', 'bonus', 'w_f', 'ka_f', 'kd_f', 'w_b', 'ka_b', 'kd_b')
    return dict(zip(names, outs))


def _rwkv_scan_kernel(n_tb, nkkf_ref, rf_ref, vf_ref, wf_ref, kaf_ref, kdf_ref,
                      nkkb_ref, rb_ref, vb_ref, wb_ref, kab_ref, kdb_ref,
                      s0f_ref, s0b_ref, ones_ref,
                      of_ref, ob_ref, sff_ref, sfb_ref, s_scr, vt_scr):
    tb = pl.program_id(1)
    n_pair = H_B // 2
    half = RW_TB // 2
    dirs = ((nkkf_ref, rf_ref, vf_ref, wf_ref, kaf_ref, kdf_ref, of_ref, False),
            (nkkb_ref, rb_ref, vb_ref, wb_ref, kab_ref, kdb_ref, ob_ref, True))

    @pl.when(tb == 0)
    def _():
        s_scr[0] = s0f_ref[...]
        s_scr[1] = s0b_ref[...]

    lane = lax.broadcasted_iota(jnp.int32, (HD_B, LANES), 1)
    for d, refs in enumerate(dirs):
        v_ref = refs[2]
        for bb in range(RW_BB):
            for p in range(n_pair):
                vt = v_ref[bb, :, p * LANES:(p + 1) * LANES].T
                top, bot = vt[:HD_B], vt[HD_B:]
                for s in range(2):
                    if s == 0:
                        t2 = jnp.where(lane < HD_B, top, pltpu.roll(bot, HD_B, 1))
                    else:
                        t2 = jnp.where(lane < HD_B, pltpu.roll(top, HD_B, 1), bot)
                    vt_scr[d, bb, p, s] = t2

    ones2 = ones_ref[...]
    row8 = lax.broadcasted_iota(jnp.int32, (SUBLANES, LANES), 0)
    lane8 = lax.broadcasted_iota(jnp.int32, (SUBLANES, LANES), 1)
    sel_r = jnp.logical_or(jnp.logical_and(row8 % 2 == 0, lane8 < HD_B),
                           jnp.logical_and(row8 % 2 == 1, lane8 >= HD_B))

    def row_of(rev, tt):
        return RW_TB - 1 - tt if rev else tt

    def emit_output(d, bb, tau):
        r_ref, o_ref = dirs[d][1], dirs[d][6]
        r = r_ref[bb, pl.ds(tau, 1), :]
        r8 = jnp.zeros((SUBLANES, LANES), F32)
        for p in range(n_pair):
            rp = jnp.broadcast_to(r[:, p * LANES:(p + 1) * LANES], (SUBLANES, LANES))
            r8 = jnp.where(jnp.logical_and(sel_r, row8 // 2 == p), rp, r8)
        s_all = jnp.concatenate([_bf(s_scr[d, bb, p]) for p in range(n_pair)], axis=0)
        o8 = _dot_nt(_bf(r8), s_all)
        o_parts = []
        for p in range(n_pair):
            for h in range(2):
                o_parts.append(o8[2 * p + h:2 * p + h + 1, p * HD_B:(p + 1) * HD_B])
        o_ref[bb, pl.ds(tau, 1), :] = jnp.concatenate(o_parts, axis=1)

    groups = [(d, bbs) for d in range(2) for bbs in _chunks(range(RW_BB), RW_GROUP_BB)]

    def step(tt, carry):
        tt_prev = jnp.maximum(tt - 1, 0)
        reds = []
        for d, bbs in groups:
            rev = dirs[d][7]
            tau = row_of(rev, tt)
            sub = tau // half
            lt = tau % half
            mask = jnp.logical_or(lane == lt, lane == lt + HD_B)
            lhs = []
            for bb in bbs:
                emit_output(d, bb, row_of(rev, tt_prev))
                nkk = dirs[d][0][bb, pl.ds(tau, 1), :]
                for p in range(n_pair):
                    prod = s_scr[d, bb, p] * nkk[:, p * LANES:(p + 1) * LANES]
                    lhs.append(jnp.concatenate([_bf(prod), _bf(jnp.where(mask, vt_scr[d, bb, p, sub], 0.0))],
                                               axis=1))
            reds.append(jnp.dot(jnp.concatenate(lhs, axis=0), ones2, preferred_element_type=F32))
        for (d, bbs), red in zip(groups, reds):
            _, _, _, w_ref, ka_ref, kd_ref, _, rev = dirs[d]
            tau = row_of(rev, tt)
            for k, bb in enumerate(bbs):
                w = w_ref[bb, pl.ds(tau, 1), :]
                ka = ka_ref[bb, pl.ds(tau, 1), :]
                kd = kd_ref[bb, pl.ds(tau, 1), :]
                for p in range(n_pair):
                    sl = slice(p * LANES, (p + 1) * LANES)
                    r0 = (k * n_pair + p) * HD_B
                    sa = red[r0:r0 + HD_B, :LANES]
                    vcol = red[r0:r0 + HD_B, LANES:]
                    s_scr[d, bb, p] = s_scr[d, bb, p] * w[:, sl] + sa * ka[:, sl] + vcol * kd[:, sl]
        return carry

    lax.fori_loop(0, RW_TB, step, 0)
    for d in range(2):
        for bb in range(RW_BB):
            emit_output(d, bb, row_of(dirs[d][7], RW_TB - 1))

    @pl.when(tb == n_tb - 1)
    def _():
        sff_ref[...] = s_scr[0]
        sfb_ref[...] = s_scr[1]


def _rwkv_scan(pp, row0, n_seq, seq, s0_f, s0_b, ones_pair):
    n_tb = seq // RW_TB
    n_pair = H_B // 2
    blk0 = row0 // seq
    view = lambda a: a.reshape(a.shape[0] // seq, seq, W_B)
    fwd = pl.BlockSpec((RW_BB, RW_TB, W_B), lambda b, t: (blk0 // RW_BB + b, t, 0))
    bwd = pl.BlockSpec((RW_BB, RW_TB, W_B), lambda b, t: (blk0 // RW_BB + b, n_tb - 1 - t, 0))
    st = pl.BlockSpec((RW_BB, n_pair, HD_B, LANES), lambda b, t: (b, 0, 0, 0))
    o_f = pl.BlockSpec((RW_BB, RW_TB, W_B), lambda b, t: (b, t, 0))
    o_b = pl.BlockSpec((RW_BB, RW_TB, W_B), lambda b, t: (b, n_tb - 1 - t, 0))
    ins_f = [view(pp[k]) for k in ('nkk', 'r', 'v', 'w_f', 'ka_f', 'kd_f')]
    ins_b = [view(pp[k]) for k in ('nkk', 'r', 'v', 'w_b', 'ka_b', 'kd_b')]
    st_shape = jax.ShapeDtypeStruct((n_seq, n_pair, HD_B, LANES), F32)
    o_shape = jax.ShapeDtypeStruct((n_seq, seq, W_B), F32)
    return pl.pallas_call(
        functools.partial(_rwkv_scan_kernel, n_tb),
        grid=(n_seq // RW_BB, n_tb),
        in_specs=[fwd] * 6 + [bwd] * 6 + [st, st, pl.BlockSpec(ones_pair.shape, lambda b, t: (0, 0))],
        out_specs=[o_f, o_b, st, st],
        out_shape=[o_shape, o_shape, st_shape, st_shape],
        scratch_shapes=[pltpu.VMEM((2, RW_BB, n_pair, HD_B, LANES), F32),
                        pltpu.VMEM((2, RW_BB, n_pair, 2, HD_B, LANES), F32)],
        compiler_params=_cparams(2),
        name="rwkv_scan",
    )(*ins_f, *ins_b, s0_f, s0_b, ones_pair)


def _state_to_pairs(s):
    b = s.shape[0]
    return s.reshape(b, H_B // 2, 2, HD_B, HD_B).transpose(0, 1, 3, 2, 4).reshape(b, H_B // 2, HD_B, 2 * HD_B)


def _pairs_to_state(s):
    b = s.shape[0]
    return s.reshape(b, H_B // 2, HD_B, 2, HD_B).transpose(0, 1, 3, 2, 4).reshape(b, H_B, HD_B, HD_B)


def _tail(x, y, g1, n2g, sc2, sh2, x1_ref, h_ref, hp_ref):
    x1 = x + g1 * y
    x1_ref[...] = x1
    h = _rms_mod(x1, n2g, sc2, sh2)
    h_ref[...] = _bf(h)
    hp_ref[...] = _pack_pairs(h)


def _outproj0_kernel(x_ref, oa_ref, of_ref, ob_ref, bonus_ref, g_ref, lnw_ref, lnb_ref, ones_ref,
                     w_ref, g1_ref, n2g_ref, sc2_ref, sh2_ref, x1_ref, h_ref, hp_ref):
    o_sum = of_ref[...] + ob_ref[...]
    ones2 = ones_ref[...]
    mean = _seg_sum(o_sum, ones2) * (1.0 / HD_B)
    cen = o_sum - mean
    var = _seg_sum(cen * cen, ones2) * (1.0 / HD_B)
    gn = cen * lax.rsqrt(var + GN_EPS) * lnw_ref[...] + lnb_ref[...]
    o_rw = (gn + bonus_ref[...]) * g_ref[...]
    mix = jnp.concatenate([_bf(oa_ref[...]), _bf(o_rw)], axis=1)
    y = jnp.dot(mix, w_ref[...], preferred_element_type=F32)
    _tail(x_ref[...], y, g1_ref[...], n2g_ref[...], sc2_ref[...], sh2_ref[...], x1_ref, h_ref, hp_ref)


def _outproj0(rows, x, o_att, o_f, o_b, pp, pr, ones_b, w_out_bf, n2g, mod4, layer):
    n = rows.n
    tok = lambda w: pl.BlockSpec((TM, w), lambda i: (i, 0))
    const = lambda a: pl.BlockSpec(a.shape, lambda i: (0,) * a.ndim)
    lnw, lnb, n2 = pr['ln_w'].reshape(1, -1), pr['ln_b'].reshape(1, -1), n2g.reshape(1, -1)
    return pl.pallas_call(
        _outproj0_kernel,
        grid=(n // TM,),
        in_specs=[tok(D_MODEL), tok(W_A), tok(W_B), tok(W_B), tok(W_B), tok(W_B),
                  const(lnw), const(lnb), const(ones_b), const(w_out_bf),
                  rows.mod_spec(layer, 2, TM), const(n2), rows.mod_spec(layer, 4, TM), rows.mod_spec(layer, 3, TM)],
        out_specs=[tok(D_MODEL), tok(D_MODEL), tok(D_MODEL // 2)],
        out_shape=[jax.ShapeDtypeStruct((n, D_MODEL), F32), jax.ShapeDtypeStruct((n, D_MODEL), BF16),
                   jax.ShapeDtypeStruct((n, D_MODEL // 2), jnp.uint32)],
        compiler_params=_cparams(1),
        name="outproj0",
    )(x, o_att, o_f, o_b, pp['bonus'], pp['g'], lnw, lnb, ones_b, w_out_bf, mod4, n2, mod4, mod4)


def _outproj1_kernel(x_ref, of_ref, ob_ref, gate_ref, ng_ref, w_ref, g1_ref, n2g_ref, sc2_ref, sh2_ref,
                     x1_ref, h_ref, hp_ref):
    o_sum = of_ref[...] + ob_ref[...]
    parts = []
    for h in range(H_C):
        oh = o_sum[:, h * DV_C:(h + 1) * DV_C]
        parts.append(oh * lax.rsqrt(jnp.mean(oh * oh, axis=-1, keepdims=True) + EPS))
    o = jnp.concatenate(parts, axis=1) * ng_ref[...] * _silu(gate_ref[...])
    y = jnp.dot(_bf(o), w_ref[...], preferred_element_type=F32)
    _tail(x_ref[...], y, g1_ref[...], n2g_ref[...], sc2_ref[...], sh2_ref[...], x1_ref, h_ref, hp_ref)


def _outproj1(rows, x, o_f, o_b, p1, norm_g, w_out_bf, n2g, mod4, layer):
    n = rows.n
    tok = lambda w: pl.BlockSpec((TM, w), lambda i: (i, 0))
    const = lambda a: pl.BlockSpec(a.shape, lambda i: (0,) * a.ndim)
    ng, n2 = norm_g.reshape(1, -1), n2g.reshape(1, -1)
    return pl.pallas_call(
        _outproj1_kernel,
        grid=(n // TM,),
        in_specs=[tok(D_MODEL), tok(D_C), tok(D_C), pl.BlockSpec((TM, D_C), lambda i: (i, 4)),
                  const(ng), const(w_out_bf),
                  rows.mod_spec(layer, 2, TM), const(n2), rows.mod_spec(layer, 4, TM), rows.mod_spec(layer, 3, TM)],
        out_specs=[tok(D_MODEL), tok(D_MODEL), tok(D_MODEL // 2)],
        out_shape=[jax.ShapeDtypeStruct((n, D_MODEL), F32), jax.ShapeDtypeStruct((n, D_MODEL), BF16),
                   jax.ShapeDtypeStruct((n, D_MODEL // 2), jnp.uint32)],
        compiler_params=_cparams(1),
        name="outproj1",
    )(x, o_f, o_b, p1, ng, w_out_bf, mod4, n2, mod4, mod4)


def _hgrn_kernel(n_tb, qf_ref, ff_ref, if_ref, qb_ref, fb_ref, ib_ref, lbp_ref, s0f_ref, s0b_ref,
                 trif_ref, trib_ref, of_ref, ob_ref, sff_ref, sfb_ref, s_scr):
    tb = pl.program_id(1)

    @pl.when(tb == 0)
    def _():
        for h in range(H_C):
            s_scr[0, h] = s0f_ref[0, h].T
            s_scr[1, h] = s0b_ref[0, h].T

    lbp = lbp_ref[...]
    e = jnp.exp(lbp - jnp.max(lbp, axis=0, keepdims=True))
    sm = e / jnp.sum(e, axis=0, keepdims=True)
    lb = (sm[0:1] + sm[1:2]) - sm[0:1]

    n_chunk = HG_TB // CHUNK
    ti = lax.broadcasted_iota(jnp.int32, (CHUNK, CHUNK), 0)
    si = lax.broadcasted_iota(jnp.int32, (CHUNK, CHUNK), 1)
    dirs = ((qf_ref, ff_ref, if_ref, of_ref, trif_ref, ti >= si, CHUNK - 1, False),
            (qb_ref, fb_ref, ib_ref, ob_ref, trib_ref, ti <= si, 0, True))

    def chunk(c, carry):
        for d, (q_ref, f_ref, i_ref, o_ref, tri_ref, causal, last_row, rev) in enumerate(dirs):
            cc = n_chunk - 1 - c if rev else c
            r0 = pl.multiple_of(cc * CHUNK, CHUNK)
            tri3 = tri_ref[...]
            for h in range(H_C):
                sl = slice(h * DK_C, (h + 1) * DK_C)
                q = _silu(q_ref[pl.ds(r0, CHUNK), sl])
                f = lb[:, sl] + (1.0 - lb[:, sl]) * _sigmoid(f_ref[pl.ds(r0, CHUNK), sl])
                k = 1.0 - f
                v = _bf(i_ref[pl.ds(r0, CHUNK), sl])
                g = jnp.log(f)
                g1 = _bf(g)
                g2 = _bf(g - g1.astype(F32))
                g3 = _bf(g - g1.astype(F32) - g2.astype(F32))
                b = jnp.dot(tri3, jnp.concatenate([g1, g2, g3], axis=0), preferred_element_type=F32)
                b_last = b[last_row:last_row + 1]
                q_in = _bf(q * jnp.exp(b))
                k_in = _bf(k * jnp.exp(-b))
                k_out = _bf(k * jnp.exp(b_last - b))
                att = jnp.where(causal, _dot_nt(q_in, k_in), 0.0)
                s_t = s_scr[d, h]
                o = jnp.dot(_bf(att), v, preferred_element_type=F32) + _dot_nt(q_in, _bf(s_t))
                o_ref[pl.ds(r0, CHUNK), sl] = o
                s_scr[d, h] = jnp.exp(b_last) * s_t + _dot_tn(v, k_out)
        return carry

    lax.fori_loop(0, n_chunk, chunk, 0)

    @pl.when(tb == n_tb - 1)
    def _():
        for h in range(H_C):
            sff_ref[0, h] = s_scr[0, h].T
            sfb_ref[0, h] = s_scr[1, h].T


def _hgrn_scan(p1, row0, n_seq, seq, lb_params, s0_f, s0_b):
    n_tb = seq // HG_TB
    blk0 = row0 // HG_TB
    tri = np.tril(np.ones((CHUNK, CHUNK), np.float32))
    tri_f = jnp.asarray(np.concatenate([tri] * 3, axis=1), dtype=BF16)
    tri_b = jnp.asarray(np.concatenate([tri.T] * 3, axis=1), dtype=BF16)
    fwd = lambda col: pl.BlockSpec((HG_TB, D_C), lambda b, t: (blk0 + b * n_tb + t, col))
    bwd = lambda col: pl.BlockSpec((HG_TB, D_C), lambda b, t: (blk0 + b * n_tb + n_tb - 1 - t, col))
    st = pl.BlockSpec((1, H_C, DK_C, DV_C), lambda b, t: (b, 0, 0, 0))
    const = lambda a: pl.BlockSpec(a.shape, lambda b, t: (0,) * a.ndim)
    o_shape = jax.ShapeDtypeStruct((n_seq * seq, D_C), F32)
    st_shape = jax.ShapeDtypeStruct((n_seq, H_C, DK_C, DV_C), F32)
    return pl.pallas_call(
        functools.partial(_hgrn_kernel, n_tb),
        grid=(n_seq, n_tb),
        in_specs=[fwd(0), fwd(1), fwd(3), bwd(0), bwd(2), bwd(3), const(lb_params), st, st,
                  const(tri_f), const(tri_b)],
        out_specs=[pl.BlockSpec((HG_TB, D_C), lambda b, t: (b * n_tb + t, 0)),
                   pl.BlockSpec((HG_TB, D_C), lambda b, t: (b * n_tb + n_tb - 1 - t, 0)), st, st],
        out_shape=[o_shape, o_shape, st_shape, st_shape],
        scratch_shapes=[pltpu.VMEM((2, H_C, DV_C, DK_C), F32)],
        compiler_params=_cparams(2),
        name="hgrn_scan",
    )(p1, p1, p1, p1, p1, p1, lb_params, s0_f, s0_b, tri_f, tri_b)


def _router_kernel(h_ref, rhi_ref, rlo_ref, bias_ref, sel_ref, eidx_ref, ew_ref, cnt_ref):
    x = h_ref[...]
    tm = x.shape[0]
    logits = _dot_nt(rhi_ref[...], x) + _dot_nt(rlo_ref[...], x)
    scores = _sigmoid(logits)
    biased = scores + bias_ref[...]
    per = N_EXPERTS // N_GROUPS
    sub = lax.broadcasted_iota(jnp.int32, (per, tm), 0)
    gs_rows = []
    for g in range(N_GROUPS):
        blk = biased[g * per:(g + 1) * per]
        m1 = jnp.max(blk, axis=0, keepdims=True)
        first = jnp.min(jnp.where(blk == m1, sub, per), axis=0, keepdims=True)
        m2 = jnp.max(jnp.where(sub == first, -jnp.inf, blk), axis=0, keepdims=True)
        gs_rows.append(m1 + m2)
    gs = jnp.concatenate(gs_rows, axis=0)
    gi = lax.broadcasted_iota(jnp.int32, gs.shape, 0)
    rank = jnp.zeros(gs.shape, jnp.int32)
    for s in range(1, N_GROUPS):
        other = pltpu.roll(gs, s, 0)
        oi = pltpu.roll(gi, s, 0)
        beats = jnp.logical_or(other > gs, jnp.logical_and(other == gs, oi < gi))
        rank = rank + jnp.where(beats, 1, 0)
    keep = jnp.where(rank < TOPK_GROUPS, 1.0, 0.0)
    emask = jnp.concatenate([jnp.broadcast_to(keep[g:g + 1], (per, tm)) for g in range(N_GROUPS)], axis=0)
    cur = jnp.where(emask > 0.0, biased, -jnp.inf)
    ei = lax.broadcasted_iota(jnp.int32, cur.shape, 0)
    sel = jnp.zeros(cur.shape, F32)
    idxs, vals = [], []
    for _ in range(TOP_K):
        m = jnp.max(cur, axis=0, keepdims=True)
        idx = jnp.min(jnp.where(cur == m, ei, N_EXPERTS), axis=0, keepdims=True)
        pick = ei == idx
        idxs.append(idx)
        vals.append(jnp.sum(jnp.where(pick, scores, 0.0), axis=0, keepdims=True))
        sel = jnp.where(pick, 1.0, sel)
        cur = jnp.where(pick, -jnp.inf, cur)
    w = jnp.concatenate(vals, axis=0)
    eidx_ref[...] = jnp.concatenate(idxs, axis=0)
    ew_ref[...] = w / jnp.sum(w, axis=0, keepdims=True) * ROUTED_SCALE
    sel_ref[...] = _bf(sel)

    @pl.when(pl.program_id(0) == 0)
    def _():
        cnt_ref[...] = jnp.zeros_like(cnt_ref)

    cnt_ref[...] += jnp.sum(sel, axis=1, keepdims=True)


def _router(hffn, router, bias):
    n = hffn.shape[0]
    r_t = router.T
    r_hi = _bf(r_t)
    r_lo = _bf(r_t - r_hi.astype(F32))
    const = lambda a: pl.BlockSpec(a.shape, lambda i: (0,) * a.ndim)
    b_col = bias.reshape(N_EXPERTS, 1)
    return pl.pallas_call(
        _router_kernel,
        grid=(n // TM,),
        in_specs=[pl.BlockSpec((TM, D_MODEL), lambda i: (i, 0)), const(r_hi), const(r_lo), const(b_col)],
        out_specs=[pl.BlockSpec((N_EXPERTS, TM), lambda i: (0, i)),
                   pl.BlockSpec((TOP_K, TM), lambda i: (0, i)),
                   pl.BlockSpec((TOP_K, TM), lambda i: (0, i)),
                   pl.BlockSpec((N_EXPERTS, LANES), lambda i: (0, 0))],
        out_shape=[jax.ShapeDtypeStruct((N_EXPERTS, n), BF16),
                   jax.ShapeDtypeStruct((TOP_K, n), jnp.int32),
                   jax.ShapeDtypeStruct((TOP_K, n), F32),
                   jax.ShapeDtypeStruct((N_EXPERTS, LANES), F32)],
        compiler_params=_cparams(1),
        name="router",
    )(hffn, r_hi, r_lo, b_col)


def _positions_kernel(sel_ref, eidx_ref, base_ref, upper_ref, pos_ref, carry_ref):
    @pl.when(pl.program_id(0) == 0)
    def _():
        carry_ref[...] = jnp.zeros_like(carry_ref)

    sel = sel_ref[...]
    rank = jnp.dot(sel, upper_ref[...], preferred_element_type=F32)
    pos_e = base_ref[:, 0:1] + carry_ref[:, 0:1] + rank
    ei = lax.broadcasted_iota(jnp.int32, pos_e.shape, 0)
    eidx = eidx_ref[...]
    rows = [jnp.sum(jnp.where(ei == eidx[k:k + 1], pos_e, 0.0), axis=0, keepdims=True) for k in range(TOP_K)]
    pos_ref[...] = jnp.concatenate(rows, axis=0).astype(jnp.int32)
    carry_ref[...] += jnp.sum(sel.astype(F32), axis=1, keepdims=True)


def _positions(sel, eidx, base):
    n = sel.shape[1]
    pb = POS_TB
    upper = jnp.asarray(np.triu(np.ones((pb, pb), np.float32), 1), dtype=BF16)
    return pl.pallas_call(
        _positions_kernel,
        grid=(n // pb,),
        in_specs=[pl.BlockSpec((N_EXPERTS, pb), lambda i: (0, i)),
                  pl.BlockSpec((TOP_K, pb), lambda i: (0, i)),
                  pl.BlockSpec((N_EXPERTS, LANES), lambda i: (0, 0)),
                  pl.BlockSpec((pb, pb), lambda i: (0, 0))],
        out_specs=pl.BlockSpec((TOP_K, pb), lambda i: (0, i)),
        out_shape=jax.ShapeDtypeStruct((TOP_K, n), jnp.int32),
        scratch_shapes=[pltpu.VMEM((N_EXPERTS, LANES), F32)],
        compiler_params=_cparams(1),
        name="positions",
    )(sel, eidx, base, upper)


def _pack_pairs(x):
    half = x.shape[1] // 2
    bits = lax.bitcast_convert_type(_bf(x).astype(F32), jnp.uint32)
    return (bits[:, :half] >> 16) | (bits[:, half:] & jnp.uint32(0xFFFF0000))


def _unpack_pairs(w):
    lo = lax.bitcast_convert_type(w << 16, F32)
    hi = lax.bitcast_convert_type(w & jnp.uint32(0xFFFF0000), F32)
    return jnp.concatenate([_bf(lo), _bf(hi)], axis=1)


def _sc_gather(table, idx):
    b, w = idx.shape[0], table.shape[1]
    n_workers = SC_CORES * SC_SUBCORES
    per_w = b // n_workers
    assert b % (n_workers * SC_CHUNK) == 0
    mesh = plsc.VectorSubcoreMesh(core_axis_name="c", subcore_axis_name="s")

    @functools.partial(
        pl.kernel, mesh=mesh, out_type=jax.ShapeDtypeStruct((b, w), table.dtype),
        scratch_types=[pltpu.VMEM((SC_CHUNK,), jnp.int32), pltpu.VMEM((SC_CHUNK, w), table.dtype),
                       pltpu.SemaphoreType.DMA])
    def gather(table_hbm, idx_hbm, out_hbm, idx_v, rows_v, sem):
        wid = lax.axis_index("s") * SC_CORES + lax.axis_index("c")
        base = wid * per_w

        @pl.loop(0, per_w // SC_CHUNK)
        def _(c):
            off = pl.multiple_of(base + c * SC_CHUNK, SC_CHUNK)
            pltpu.sync_copy(idx_hbm.at[pl.ds(off, SC_CHUNK)], idx_v)
            pltpu.async_copy(table_hbm.at[idx_v], rows_v, sem).wait()
            pltpu.sync_copy(rows_v, out_hbm.at[pl.ds(off, SC_CHUNK)])

    return gather(table, idx)


def _experts_kernel(te_ref, nu_ref, xs_ref, wg_ref, wu_ref, wd_ref, ys_ref):
    @pl.when(pl.program_id(0) < nu_ref[0])
    def _():
        x = _unpack_pairs(xs_ref[...])
        act = _glu(x, wg_ref[0], wu_ref[0])
        ys_ref[...] = _pack_pairs(jnp.dot(_bf(act), wd_ref[0], preferred_element_type=F32))


def _experts(xs, tile_expert, n_used, mp):
    n_tiles = xs.shape[0] // MOE_TILE
    half = D_MODEL // 2
    wspec = lambda shape: pl.BlockSpec((1,) + shape, lambda i, te, nu: (te[i], 0, 0))
    return pl.pallas_call(
        _experts_kernel,
        grid_spec=pltpu.PrefetchScalarGridSpec(
            num_scalar_prefetch=2, grid=(n_tiles,),
            in_specs=[pl.BlockSpec((MOE_TILE, half), lambda i, te, nu: (i, 0)),
                      wspec((D_MODEL, D_EXPERT)), wspec((D_MODEL, D_EXPERT)), wspec((D_EXPERT, D_MODEL))],
            out_specs=pl.BlockSpec((MOE_TILE, half), lambda i, te, nu: (i, 0))),
        out_shape=jax.ShapeDtypeStruct(xs.shape, jnp.uint32),
        compiler_params=_cparams(1),
        name="experts",
    )(tile_expert, n_used, xs, mp['wg'], mp['wu'], mp['wd'])


def _combine_kernel(h_ref, yg_ref, ew_ref, eye_ref, sg_ref, su_ref, sd_ref, x1_ref, g2_ref, o_ref):
    act = _glu(h_ref[...], sg_ref[...], su_ref[...])
    acc = jnp.dot(_bf(act), sd_ref[...], preferred_element_type=F32)
    ew = ew_ref[...]
    hi = _bf(ew)
    lo = _bf(ew - hi.astype(F32))
    ew_t = _dot_tn(hi, eye_ref[...]) + _dot_tn(lo, eye_ref[...])
    half = D_MODEL // 2
    for k in range(TOP_K):
        yk = _unpack_pairs(yg_ref[:, k * half:(k + 1) * half]).astype(F32)
        acc = acc + ew_t[:, k:k + 1] * yk
    o_ref[...] = x1_ref[...] + g2_ref[...] * acc


def _combine(rows, hffn, yg, ew, mp, x1, mod4, layer):
    n = rows.n
    half = D_MODEL // 2
    const = lambda a: pl.BlockSpec(a.shape, lambda i: (0,) * a.ndim)
    tok = lambda w: pl.BlockSpec((TM, w), lambda i: (i, 0))
    eye = jnp.eye(TOP_K, dtype=BF16)
    return pl.pallas_call(
        _combine_kernel,
        grid=(n // TM,),
        in_specs=[tok(D_MODEL), tok(TOP_K * half), pl.BlockSpec((TOP_K, TM), lambda i: (0, i)), const(eye),
                  const(mp['sg']), const(mp['su']), const(mp['sd']), tok(D_MODEL), rows.mod_spec(layer, 5, TM)],
        out_specs=tok(D_MODEL),
        out_shape=jax.ShapeDtypeStruct((n, D_MODEL), F32),
        compiler_params=_cparams(1),
        name=f"combine{layer}",
    )(hffn, yg, ew, eye, mp['sg'], mp['su'], mp['sd'], x1, mod4)


def _moe(rows, hffn, hpack, x1, router, bias, mp, mod4, layer):
    n = rows.n
    sel, eidx, ew, cnt = _router(hffn, router, bias)
    counts = cnt[:, 0].astype(jnp.int32)
    padded = (counts + MOE_TILE - 1) // MOE_TILE * MOE_TILE
    ends = jnp.cumsum(padded)
    n_rows = n * TOP_K + N_EXPERTS * MOE_TILE
    n_tiles = n_rows // MOE_TILE
    base = jnp.broadcast_to((ends - padded).astype(F32)[:, None], (N_EXPERTS, LANES))
    tile_expert = jnp.minimum(jnp.searchsorted(ends, jnp.arange(n_tiles, dtype=jnp.int32) * MOE_TILE, side='right'),
                              N_EXPERTS - 1).astype(jnp.int32)
    n_used = (ends[-1:] // MOE_TILE).astype(jnp.int32)
    pos = _positions(sel, eidx, base)
    pos_flat = pos.T.reshape(-1)
    row_token = (jnp.arange(n_rows, dtype=jnp.int32) % n).at[pos_flat].set(
        jnp.repeat(jnp.arange(n, dtype=jnp.int32), TOP_K))
    xs = _sc_gather(hpack, row_token)
    ys = _experts(xs, tile_expert, n_used, mp)
    yg = _sc_gather(ys, pos_flat).reshape(n, TOP_K * (D_MODEL // 2))
    return _combine(rows, hffn, yg, ew, mp, x1, mod4, layer)


def _glu(x, wg, wu):
    hg = jnp.dot(x, wg, preferred_element_type=F32)
    hu = jnp.dot(x, wu, preferred_element_type=F32)
    return _silu(hg) * hu


def kernel(x_prompt, x_sample, c, c_ctx, cache_attn_k, cache_attn_v, state_rwkv_fwd, state_rwkv_bwd,
           state_hgrn_fwd, state_hgrn_bwd, norm1_g, norm2_g, mod_w, mod_b, ab_w_in, ab_w_out, attn_q_norm,
           attn_k_norm, attn_sink, rwkv_mu, rwkv_w0, rwkv_w2, rwkv_a0, rwkv_a2, rwkv_g2, rwkv_k_k, rwkv_k_a,
           rwkv_r_k, rwkv_ln_w, rwkv_ln_b, hgrn_w_in, hgrn_w_out, hgrn_lower_bounds, hgrn_norm_g, moe_router,
           moe_bias, moe_w_gate, moe_w_up, moe_w_down, moe_shared_gate, moe_shared_up, moe_shared_down):
    n_cseq, cseq, _ = x_prompt.shape
    n_lseq, lseq, _ = x_sample.shape
    depth = mod_w.shape[0]
    assert depth == 2 and n_lseq + 1 <= SUBLANES
    assert cseq == TM and lseq % TM == 0 and lseq % HG_TB == 0 and cseq % HG_TB == 0
    assert n_cseq % RW_BB == 0 and n_lseq % RW_BB == 0 and (n_cseq * cseq) % (lseq * RW_BB) == 0
    rows = _Rows(n_cseq * cseq, n_lseq * lseq, lseq)
    assert rows.n % MOE_TILE == 0 and lseq % MOE_TILE == 0 and rows.n_ctx % MOE_TILE == 0
    kv_w = KV_A * HD_A

    x = jnp.concatenate([x_prompt.reshape(rows.n_ctx, D_MODEL), x_sample.reshape(rows.n_lat, D_MODEL)], axis=0)
    cvecs = jnp.concatenate([c_ctx[None, :], c, jnp.zeros((SUBLANES - 1 - n_lseq, D_MODEL), F32)], axis=0)
    mod4 = _modulation(cvecs, mod_w, mod_b).reshape(depth, SUBLANES, 1, 6 * D_MODEL)

    ones_q = _block_ones(W_A, HD_A)
    ones_k = _block_ones(kv_w, HD_A)
    ones_b = _block_ones(W_B, HD_B)
    ones_pair = _block_ones(LANES, HD_B)[:LANES]
    ones_pair = jnp.kron(jnp.eye(2, dtype=BF16), ones_pair)
    cos_t, sin_t = _rope_tables(lseq)

    def moe(l, hffn, hpack, x1):
        mp = {'wg': _bf(moe_w_gate[l]), 'wu': _bf(moe_w_up[l]), 'wd': _bf(moe_w_down[l]),
              'sg': _bf(moe_shared_gate[l]), 'su': _bf(moe_shared_up[l]), 'sd': _bf(moe_shared_down[l])}
        return _moe(rows, hffn, hpack, x1, moe_router[l], moe_bias[l], mp, mod4, l)

    pr = {'mu': rwkv_mu[0], 'w0': rwkv_w0[0], 'w2': rwkv_w2[0], 'a0': rwkv_a0[0], 'a2': rwkv_a2[0],
          'g2': rwkv_g2[0], 'k_k': rwkv_k_k[0], 'k_a': rwkv_k_a[0], 'r_k': rwkv_r_k[0].reshape(-1),
          'ln_w': rwkv_ln_w[0], 'ln_b': rwkv_ln_b[0]}
    p_att, p_rw = _inproj(rows, x, norm1_g[0], mod4, 0, _bf(ab_w_in[0]), (ATT_IN, RWKV_IN))
    qg_t = jnp.tile(attn_q_norm[0], H_A).reshape(1, W_A)
    kg_t = jnp.tile(attn_k_norm[0], KV_A).reshape(1, kv_w)
    o_att_c, new_k, new_v = _ctx_attention(p_att, n_cseq, cseq, qg_t, kg_t, attn_sink[0], ones_q, ones_k)
    past = cache_attn_k.shape[2]
    o_att_l = _lat_attention(p_att, rows.n_ctx // lseq, n_lseq, lseq, qg_t, kg_t, attn_sink[0], ones_q, ones_k,
                             cos_t, sin_t, cache_attn_k[:, 0].reshape(n_lseq, past, kv_w),
                             cache_attn_v[:, 0].reshape(n_lseq, past, kv_w))
    o_att = jnp.concatenate([o_att_c, o_att_l], axis=0)

    pp = _rwkv_prep(rows, p_rw, pr, ones_b)
    zero_st = jnp.zeros((n_cseq, H_B // 2, HD_B, LANES), F32)
    of_c, ob_c, sf_c, sb_c = _rwkv_scan(pp, 0, n_cseq, cseq, zero_st, zero_st, ones_pair)
    of_l, ob_l, _, _ = _rwkv_scan(pp, rows.n_ctx, n_lseq, lseq, _state_to_pairs(state_rwkv_fwd[:, 0]),
                                  _state_to_pairs(state_rwkv_bwd[:, 0]), ones_pair)
    o_f = jnp.concatenate([of_c.reshape(rows.n_ctx, W_B), of_l.reshape(rows.n_lat, W_B)], axis=0)
    o_b = jnp.concatenate([ob_c.reshape(rows.n_ctx, W_B), ob_l.reshape(rows.n_lat, W_B)], axis=0)
    x1, hffn, hpack = _outproj0(rows, x, o_att, o_f, o_b, pp, pr, ones_b, _bf(ab_w_out[0]), norm2_g[0], mod4, 0)
    x = moe(0, hffn, hpack, x1)

    (p1,) = _inproj(rows, x, norm1_g[1], mod4, 1, _bf(hgrn_w_in[0]), (IN_C,))
    zero_h = jnp.zeros((n_cseq, H_C, DK_C, DV_C), F32)
    hf_c, hb_c, hsf_c, hsb_c = _hgrn_scan(p1, 0, n_cseq, cseq, hgrn_lower_bounds, zero_h, zero_h)
    hf_l, hb_l, _, _ = _hgrn_scan(p1, rows.n_ctx, n_lseq, lseq, hgrn_lower_bounds,
                                  state_hgrn_fwd[:, 0], state_hgrn_bwd[:, 0])
    h_f = jnp.concatenate([hf_c, hf_l], axis=0)
    h_b = jnp.concatenate([hb_c, hb_l], axis=0)
    x1, hffn, hpack = _outproj1(rows, x, h_f, h_b, p1, hgrn_norm_g[0], _bf(hgrn_w_out[0]), norm2_g[1], mod4, 1)
    x = moe(1, hffn, hpack, x1)

    y_prompt = x[:rows.n_ctx].reshape(n_cseq, cseq, D_MODEL)
    y_sample = x[rows.n_ctx:].reshape(n_lseq, lseq, D_MODEL)
    return (y_prompt, y_sample,
            new_k.reshape(n_cseq, 1, cseq, KV_A, HD_A), new_v.reshape(n_cseq, 1, cseq, KV_A, HD_A),
            _pairs_to_state(sf_c)[:, None], _pairs_to_state(sb_c)[:, None],
            hsf_c[:, None], hsb_c[:, None])
```

```python
import functools

import numpy as np
import jax
import jax.numpy as jnp
from jax import lax
from jax.experimental import pallas as pl
from jax.experimental.pallas import tpu as pltpu
from jax.experimental.pallas import tpu_sc as plsc

F32 = jnp.float32
BF16 = jnp.bfloat16

D_MODEL = 1024
GRID_W = 64
H_A = 8
KV_A = 2
G_A = H_A // KV_A
HD_A = 64
W_A = H_A * HD_A
WINDOW = 128
QBLK = 128
ROPE_BASE = 10000.0
ATTN_SCALE = HD_A ** -0.5
NEG_INF = -1e30
H_B = 8
HD_B = 64
W_B = H_B * HD_B
LORA_W = 64
LORA_A = 64
LORA_G = 128
GN_EPS = 64e-5
ATT_IN = W_A + 2 * KV_A * HD_A
RWKV_IN = 3 * W_B + LORA_W + LORA_A + LORA_G
IN_AB = ATT_IN + RWKV_IN
H_C = 8
DK_C = 128
DV_C = 128
D_C = H_C * DV_C
CHUNK = 64
IN_C = 5 * D_C
N_EXPERTS = 64
TOP_K = 8
N_GROUPS = 8
TOPK_GROUPS = 4
D_EXPERT = 256
ROUTED_SCALE = 2.5
EPS = 1e-6

LANES = 128
SUBLANES = 8
VMEM_LIMIT = 52 * 1024 * 1024

TM = 256
RW_TB = 128
RW_BB = 4
RW_GROUP_BB = 4
HG_TB = 256
MOE_TILE = 256
POS_TB = 512
SC_CORES = 2
SC_SUBCORES = 16
SC_CHUNK = 64


def _cparams(n_axes):
    return pltpu.CompilerParams(dimension_semantics=("arbitrary",) * n_axes,
                                vmem_limit_bytes=VMEM_LIMIT)


def _bf(x):
    return x.astype(BF16)


def _split2(x):
    hi = lax.bitcast_convert_type(
        lax.bitcast_convert_type(x, jnp.uint32) & jnp.uint32(0xFFFF0000), F32)
    return hi, x - hi


def _seg_sum(x, ones2):
    hi, lo = _split2(x)
    return jnp.dot(jnp.concatenate([_bf(hi), _bf(lo)], axis=1), ones2,
                   preferred_element_type=F32)


def _dot_nt(a, b):
    return lax.dot_general(a, b, (((1,), (1,)), ((), ())), preferred_element_type=F32)


def _dot_tn(a, b):
    return lax.dot_general(a, b, (((0,), (0,)), ((), ())), preferred_element_type=F32)


def _sigmoid(x):
    return 1.0 / (1.0 + jnp.exp(-x))


def _silu(x):
    return x * _sigmoid(x)


def _chunks(seq, n):
    seq = list(seq)
    return [seq[i:i + n] for i in range(0, len(seq), n)]


def _block_ones(width, seg):
    idx = np.arange(width) // seg
    bd = (idx[:, None] == idx[None, :]).astype(np.float32)
    return jnp.asarray(np.concatenate([bd, bd], axis=0), dtype=BF16)


def _mod_kernel(c_ref, w_ref, b_ref, o_ref):
    s = _silu(c_ref[...])
    o_ref[0] = jnp.dot(_bf(s), _bf(w_ref[0]), preferred_element_type=F32) + b_ref[0]


def _modulation(cvecs, mod_w, mod_b):
    depth = mod_w.shape[0]
    n_col = 6 * D_MODEL // D_MODEL
    return pl.pallas_call(
        _mod_kernel,
        grid=(depth, n_col),
        in_specs=[pl.BlockSpec((SUBLANES, D_MODEL), lambda l, j: (0, 0)),
                  pl.BlockSpec((1, D_MODEL, D_MODEL), lambda l, j: (l, 0, j)),
                  pl.BlockSpec((1, 1, D_MODEL), lambda l, j: (l, 0, j))],
        out_specs=pl.BlockSpec((1, SUBLANES, D_MODEL), lambda l, j: (l, 0, j)),
        out_shape=jax.ShapeDtypeStruct((depth, SUBLANES, 6 * D_MODEL), F32),
        compiler_params=_cparams(2),
        name="modulation",
    )(cvecs, mod_w, mod_b.reshape(depth, 1, 6 * D_MODEL))


class _Rows:
    def __init__(self, n_ctx, n_lat, lat_seq):
        self.n_ctx, self.n_lat, self.lat_seq = n_ctx, n_lat, lat_seq
        self.n = n_ctx + n_lat

    def mod_row(self, i, tm):
        nctx_blk = self.n_ctx // tm
        per_seq = self.lat_seq // tm
        return jnp.where(i < nctx_blk, 0, 1 + (i - nctx_blk) // per_seq)

    def mod_spec(self, layer, chunk, tm):
        return pl.BlockSpec((None, None, 1, D_MODEL),
                            lambda i, *_: (layer, self.mod_row(i, tm), 0, chunk))


def _rms_mod(x, g, sc, sh):
    ms = jnp.mean(x * x, axis=-1, keepdims=True)
    return x * lax.rsqrt(ms + EPS) * g * (1.0 + sc) + sh


def _inproj_kernel(splits, x_ref, g_ref, sh_ref, sc_ref, w_ref, *o_refs):
    h = _rms_mod(x_ref[...], g_ref[...], sc_ref[...], sh_ref[...])
    p = jnp.dot(_bf(h), w_ref[...], preferred_element_type=F32)
    lo = 0
    for o_ref, width in zip(o_refs, splits):
        o_ref[...] = p[:, lo:lo + width]
        lo += width


def _inproj(rows, x, g, mod4, layer, w_bf, splits):
    n_out = w_bf.shape[1]
    return pl.pallas_call(
        functools.partial(_inproj_kernel, splits),
        grid=(rows.n // TM,),
        in_specs=[pl.BlockSpec((TM, D_MODEL), lambda i: (i, 0)),
                  pl.BlockSpec((1, D_MODEL), lambda i: (0, 0)),
                  rows.mod_spec(layer, 0, TM),
                  rows.mod_spec(layer, 1, TM),
                  pl.BlockSpec((D_MODEL, n_out), lambda i: (0, 0))],
        out_specs=[pl.BlockSpec((TM, wd), lambda i: (i, 0)) for wd in splits],
        out_shape=[jax.ShapeDtypeStruct((rows.n, wd), F32) for wd in splits],
        compiler_params=_cparams(1),
        name=f"inproj{layer}",
    )(x, g.reshape(1, D_MODEL), mod4, mod4, w_bf)


def _head_rms(x, gain_t, ones2):
    ms = _seg_sum(x * x, ones2) * (1.0 / HD_A)
    return x * lax.rsqrt(ms + EPS) * gain_t


def _sink_softmax_pv(parts, sink):
    m = jnp.maximum(functools.reduce(jnp.maximum, [jnp.max(s, axis=-1, keepdims=True) for s, _ in parts]), sink)
    den = jnp.exp(sink - m)
    acc = None
    for s, v in parts:
        p = jnp.exp(s - m)
        den = den + jnp.sum(p, axis=-1, keepdims=True)
        pv = jnp.dot(_bf(p), v, preferred_element_type=F32)
        acc = pv if acc is None else acc + pv
    return acc / den


def _ctx_attn_kernel(p_ref, qg_ref, kg_ref, sink_ref, ones_q_ref, ones_k_ref, o_ref, k_ref, v_ref):
    p = p_ref[...]
    q = _head_rms(p[:, :W_A], qg_ref[...], ones_q_ref[...]) * ATTN_SCALE
    k = _head_rms(p[:, W_A:W_A + KV_A * HD_A], kg_ref[...], ones_k_ref[...])
    v = p[:, W_A + KV_A * HD_A:ATT_IN]
    k_ref[0] = k
    v_ref[0] = v
    qb, kb, vb = _bf(q), _bf(k), _bf(v)
    outs = []
    for h in range(H_A):
        j = h // G_A
        s = _dot_nt(qb[:, h * HD_A:(h + 1) * HD_A], kb[:, j * HD_A:(j + 1) * HD_A])
        outs.append(_sink_softmax_pv([(s, vb[:, j * HD_A:(j + 1) * HD_A])], sink_ref[h]))
    o_ref[...] = jnp.concatenate(outs, axis=1)


def _ctx_attention(p_att, n_seq, seq, qg_t, kg_t, sink, ones_q, ones_k):
    kv_w = KV_A * HD_A
    return pl.pallas_call(
        _ctx_attn_kernel,
        grid=(n_seq,),
        in_specs=[pl.BlockSpec((seq, ATT_IN), lambda b: (b, 0)),
                  pl.BlockSpec((1, W_A), lambda b: (0, 0)),
                  pl.BlockSpec((1, kv_w), lambda b: (0, 0)),
                  pl.BlockSpec(memory_space=pltpu.SMEM),
                  pl.BlockSpec(ones_q.shape, lambda b: (0, 0)),
                  pl.BlockSpec(ones_k.shape, lambda b: (0, 0))],
        out_specs=[pl.BlockSpec((seq, W_A), lambda b: (b, 0)),
                   pl.BlockSpec((1, seq, kv_w), lambda b: (b, 0, 0)),
                   pl.BlockSpec((1, seq, kv_w), lambda b: (b, 0, 0))],
        out_shape=[jax.ShapeDtypeStruct((n_seq * seq, W_A), F32),
                   jax.ShapeDtypeStruct((n_seq, seq, kv_w), F32),
                   jax.ShapeDtypeStruct((n_seq, seq, kv_w), F32)],
        compiler_params=_cparams(1),
        name="ctx_attention",
    )(p_att, qg_t, kg_t, sink, ones_q, ones_k)


def _rope(x, cos_t, sin_t):
    lane = lax.broadcasted_iota(jnp.int32, cos_t.shape, 1)
    low = (lane % 32) < 16
    outs = []
    for s in range(x.shape[1] // LANES):
        xs = x[:, s * LANES:(s + 1) * LANES]
        partner = jnp.where(low, pltpu.roll(xs, LANES - 16, 1), pltpu.roll(xs, 16, 1))
        outs.append(xs * cos_t + partner * sin_t)
    return outs[0] if len(outs) == 1 else jnp.concatenate(outs, axis=1)


def _lat_attn_kernel(seq, p_ref, qg_ref, kg_ref, sink_ref, ones_q_ref, ones_k_ref, cos_ref, sin_ref,
                     kc_ref, vc_ref, o_ref, q_scr, k_scr, v_scr):
    kv_w = KV_A * HD_A
    p = p_ref[...]
    q = _head_rms(p[:, :W_A], qg_ref[...], ones_q_ref[...])
    k = _head_rms(p[:, W_A:W_A + kv_w], kg_ref[...], ones_k_ref[...])
    q_scr[...] = _bf(_rope(q, cos_ref[...], sin_ref[...]) * ATTN_SCALE)
    k_scr[...] = _bf(_rope(k, cos_ref[...], sin_ref[...]))
    v_scr[...] = _bf(p[:, W_A + kv_w:ATT_IN])
    kc = _bf(kc_ref[0])
    vc = _bf(vc_ref[0])
    n_local = 3 * QBLK

    def block(i, carry):
        q0 = pl.multiple_of(i * QBLK, QBLK)
        start = pl.multiple_of(jnp.clip((i - 1) * QBLK, 0, seq - n_local), QBLK)
        qb = q_scr[pl.ds(q0, QBLK), :]
        kl = k_scr[pl.ds(start, n_local), :]
        vl = v_scr[pl.ds(start, n_local), :]
        ipos = q0 + lax.broadcasted_iota(jnp.int32, (QBLK, n_local), 0)
        jpos = start + lax.broadcasted_iota(jnp.int32, (QBLK, n_local), 1)
        band = jnp.abs(jpos - ipos) <= WINDOW
        outs = []
        for h in range(H_A):
            j = h // G_A
            qh = qb[:, h * HD_A:(h + 1) * HD_A]
            s_loc = jnp.where(band, _dot_nt(qh, kl[:, j * HD_A:(j + 1) * HD_A]), NEG_INF)
            s_ctx = _dot_nt(qh, kc[:, j * HD_A:(j + 1) * HD_A])
            outs.append(_sink_softmax_pv([(s_loc, vl[:, j * HD_A:(j + 1) * HD_A]),
                                          (s_ctx, vc[:, j * HD_A:(j + 1) * HD_A])], sink_ref[h]))
        o_ref[pl.ds(q0, QBLK), :] = jnp.concatenate(outs, axis=1)
        return carry

    lax.fori_loop(0, seq // QBLK, block, 0)


def _lat_attention(p_att, row_blk0, n_seq, seq, qg_t, kg_t, sink, ones_q, ones_k, cos_t, sin_t, kc, vc):
    kv_w = KV_A * HD_A
    past = kc.shape[1]
    return pl.pallas_call(
        functools.partial(_lat_attn_kernel, seq),
        grid=(n_seq,),
        in_specs=[pl.BlockSpec((seq, ATT_IN), lambda b: (row_blk0 + b, 0)),
                  pl.BlockSpec((1, W_A), lambda b: (0, 0)),
                  pl.BlockSpec((1, kv_w), lambda b: (0, 0)),
                  pl.BlockSpec(memory_space=pltpu.SMEM),
                  pl.BlockSpec(ones_q.shape, lambda b: (0, 0)),
                  pl.BlockSpec(ones_k.shape, lambda b: (0, 0)),
                  pl.BlockSpec((seq, LANES), lambda b: (0, 0)),
                  pl.BlockSpec((seq, LANES), lambda b: (0, 0)),
                  pl.BlockSpec((1, past, kv_w), lambda b: (b, 0, 0)),
                  pl.BlockSpec((1, past, kv_w), lambda b: (b, 0, 0))],
        out_specs=pl.BlockSpec((seq, W_A), lambda b: (b, 0)),
        out_shape=jax.ShapeDtypeStruct((n_seq * seq, W_A), F32),
        scratch_shapes=[pltpu.VMEM((seq, W_A), BF16), pltpu.VMEM((seq, kv_w), BF16),
                        pltpu.VMEM((seq, kv_w), BF16)],
        compiler_params=_cparams(1),
        name="lat_attention",
    )(p_att, qg_t, kg_t, sink, ones_q, ones_k, cos_t, sin_t, kc, vc)


def _rope_tables(seq):
    pos = np.arange(seq)
    row = (pos // GRID_W).astype(np.float32)
    col = (pos % GRID_W).astype(np.float32)
    d_axis = HD_A // 2
    inv = (ROPE_BASE ** (-np.arange(0, d_axis, 2, dtype=np.float32) / d_axis)).astype(np.float32)
    cos_h = np.zeros((seq, HD_A), np.float32)
    sin_h = np.zeros((seq, HD_A), np.float32)
    for seg, p_ in enumerate((row, col)):
        ang = (p_[:, None] * inv[None, :]).astype(np.float32)
        c, s = np.cos(ang), np.sin(ang)
        base = seg * d_axis
        cos_h[:, base:base + d_axis // 2] = c
        cos_h[:, base + d_axis // 2:base + d_axis] = c
        sin_h[:, base:base + d_axis // 2] = -s
        sin_h[:, base + d_axis // 2:base + d_axis] = s
    rep = LANES // HD_A
    return jnp.asarray(np.tile(cos_h, (1, rep))), jnp.asarray(np.tile(sin_h, (1, rep)))


def _rwkv_prep_kernel(rows, x_ref, prev_ref, next_ref, mu_ref, kk_ref, ka_ref, rk_ref, w0_ref, w2_ref,
                      a0_ref, a2_ref, g2_ref, ones_ref,
                      nkk_ref, r_ref, v_ref, g_ref, bonus_ref,
                      wf_ref, kaf_ref, kdf_ref, wb_ref, kab_ref, kdb_ref):
    i = pl.program_id(0)
    nctx_blk = rows.n_ctx // TM
    per_seq = rows.lat_seq // TM
    is_ctx = i < nctx_blk
    first = jnp.logical_or(is_ctx, (i - nctx_blk) % per_seq == 0)
    last = jnp.logical_or(is_ctx, (i - nctx_blk) % per_seq == per_seq - 1)
    x = x_ref[...]
    ridx = lax.broadcasted_iota(jnp.int32, x.shape, 0)
    prev_row = jnp.where(first, 0.0, prev_ref[SUBLANES - 1:SUBLANES, :])
    next_row = jnp.where(last, 0.0, next_ref[0:1, :])
    xm1 = jnp.where(ridx == 0, prev_row, pltpu.roll(x, 1, 0))
    xp1 = jnp.where(ridx == TM - 1, next_row, pltpu.roll(x, TM - 1, 0))
    pw = x + (0.5 * (xm1 + xp1) - x) * mu_ref[...]

    r = pw[:, 0:W_B]
    k = pw[:, W_B:2 * W_B]
    v = pw[:, 2 * W_B:3 * W_B]
    wd = pw[:, 3 * W_B:3 * W_B + LORA_W]
    ad = pw[:, 3 * W_B + LORA_W:3 * W_B + LORA_W + LORA_A]
    gd = pw[:, 3 * W_B + LORA_W + LORA_A:]
    ones2 = ones_ref[...]

    kk = k * kk_ref[...]
    kk = kk / jnp.maximum(jnp.sqrt(_seg_sum(kk * kk, ones2)), 1e-12)
    nkk_ref[...] = -kk
    r_ref[...] = r
    v_ref[...] = v
    g_ref[...] = jnp.dot(_bf(_sigmoid(gd)), g2_ref[...], preferred_element_type=F32)
    tw = _bf(jnp.tanh(wd))
    adb = _bf(ad)
    bonus = jnp.zeros_like(r)
    for d, (w_o, ka_o, kd_o) in enumerate(((wf_ref, kaf_ref, kdf_ref), (wb_ref, kab_ref, kdb_ref))):
        z = -(w0_ref[d:d + 1, :] + jnp.dot(tw, w2_ref[d], preferred_element_type=F32))
        softplus = jnp.maximum(z, 0.0) + jnp.log(1.0 + jnp.exp(-jnp.abs(z)))
        w_o[...] = jnp.exp(-jnp.exp(-softplus - 0.5))
        a = _sigmoid(a0_ref[d:d + 1, :] + jnp.dot(adb, a2_ref[d], preferred_element_type=F32))
        kd = k * (1.0 + (a - 1.0) * ka_ref[...])
        ka_o[...] = kk * a
        kd_o[...] = kd
        bonus = bonus + _seg_sum(r * kd * rk_ref[...], ones2) * v
    bonus_ref[...] = bonus


def _rwkv_prep(rows, p_rw, pr, ones_b):
    n = rows.n
    n_halo = n // SUBLANES
    blk_halo = TM // SUBLANES
    row = lambda a: a.reshape(1, -1)
    full = lambda a: pl.BlockSpec(a.shape, lambda i: (0,) * a.ndim)
    consts = [row(pr['mu']), row(pr['k_k']), row(pr['k_a']), row(pr['r_k']), pr['w0'], _bf(pr['w2']),
              pr['a0'], _bf(pr['a2']), _bf(pr['g2']), ones_b]
    outs = pl.pallas_call(
        functools.partial(_rwkv_prep_kernel, rows),
        grid=(n // TM,),
        in_specs=[pl.BlockSpec((TM, RWKV_IN), lambda i: (i, 0)),
                  pl.BlockSpec((SUBLANES, RWKV_IN), lambda i: (jnp.maximum(i * blk_halo - 1, 0), 0)),
                  pl.BlockSpec((SUBLANES, RWKV_IN), lambda i: (jnp.minimum((i + 1) * blk_halo, n_halo - 1), 0))]
                 + [full(a) for a in consts],
        out_specs=[pl.BlockSpec((TM, W_B), lambda i: (i, 0))] * 11,
        out_shape=[jax.ShapeDtypeStruct((n, W_B), F32)] * 11,
        compiler_params=_cparams(1),
        name="rwkv_prep",
    )(p_rw, p_rw, p_rw, *consts)
    names = ('nkk', 'r', 'v', 'g', 'bonus', 'w_f', 'ka_f', 'kd_f', 'w_b', 'ka_b', 'kd_b')
    return dict(zip(names, outs))


def _rwkv_scan_kernel(n_tb, nkkf_ref, rf_ref, vf_ref, wf_ref, kaf_ref, kdf_ref,
                      nkkb_ref, rb_ref, vb_ref, wb_ref, kab_ref, kdb_ref,
                      s0f_ref, s0b_ref, ones_ref,
                      of_ref, ob_ref, sff_ref, sfb_ref, s_scr, vt_scr):
    tb = pl.program_id(1)
    n_pair = H_B // 2
    half = RW_TB // 2
    dirs = ((nkkf_ref, rf_ref, vf_ref, wf_ref, kaf_ref, kdf_ref, of_ref, False),
            (nkkb_ref, rb_ref, vb_ref, wb_ref, kab_ref, kdb_ref, ob_ref, True))

    @pl.when(tb == 0)
    def _():
        s_scr[0] = s0f_ref[...]
        s_scr[1] = s0b_ref[...]

    lane = lax.broadcasted_iota(jnp.int32, (HD_B, LANES), 1)
    for d, refs in enumerate(dirs):
        v_ref = refs[2]
        for bb in range(RW_BB):
            for p in range(n_pair):
                vt = v_ref[bb, :, p * LANES:(p + 1) * LANES].T
                top, bot = vt[:HD_B], vt[HD_B:]
                for s in range(2):
                    if s == 0:
                        t2 = jnp.where(lane < HD_B, top, pltpu.roll(bot, HD_B, 1))
                    else:
                        t2 = jnp.where(lane < HD_B, pltpu.roll(top, HD_B, 1), bot)
                    vt_scr[d, bb, p, s] = t2

    ones2 = ones_ref[...]
    row8 = lax.broadcasted_iota(jnp.int32, (SUBLANES, LANES), 0)
    lane8 = lax.broadcasted_iota(jnp.int32, (SUBLANES, LANES), 1)
    sel_r = jnp.logical_or(jnp.logical_and(row8 % 2 == 0, lane8 < HD_B),
                           jnp.logical_and(row8 % 2 == 1, lane8 >= HD_B))

    def row_of(rev, tt):
        return RW_TB - 1 - tt if rev else tt

    def emit_output(d, bb, tau):
        r_ref, o_ref = dirs[d][1], dirs[d][6]
        r = r_ref[bb, pl.ds(tau, 1), :]
        r8 = jnp.zeros((SUBLANES, LANES), F32)
        for p in range(n_pair):
            rp = jnp.broadcast_to(r[:, p * LANES:(p + 1) * LANES], (SUBLANES, LANES))
            r8 = jnp.where(jnp.logical_and(sel_r, row8 // 2 == p), rp, r8)
        s_all = jnp.concatenate([_bf(s_scr[d, bb, p]) for p in range(n_pair)], axis=0)
        o8 = _dot_nt(_bf(r8), s_all)
        o_parts = []
        for p in range(n_pair):
            for h in range(2):
                o_parts.append(o8[2 * p + h:2 * p + h + 1, p * HD_B:(p + 1) * HD_B])
        o_ref[bb, pl.ds(tau, 1), :] = jnp.concatenate(o_parts, axis=1)

    groups = [(d, bbs) for d in range(2) for bbs in _chunks(range(RW_BB), RW_GROUP_BB)]

    def step(tt, carry):
        tt_prev = jnp.maximum(tt - 1, 0)
        reds = []
        for d, bbs in groups:
            rev = dirs[d][7]
            tau = row_of(rev, tt)
            sub = tau // half
            lt = tau % half
            mask = jnp.logical_or(lane == lt, lane == lt + HD_B)
            lhs = []
            for bb in bbs:
                emit_output(d, bb, row_of(rev, tt_prev))
                nkk = dirs[d][0][bb, pl.ds(tau, 1), :]
                for p in range(n_pair):
                    prod = s_scr[d, bb, p] * nkk[:, p * LANES:(p + 1) * LANES]
                    lhs.append(jnp.concatenate([_bf(prod), _bf(jnp.where(mask, vt_scr[d, bb, p, sub], 0.0))],
                                               axis=1))
            reds.append(jnp.dot(jnp.concatenate(lhs, axis=0), ones2, preferred_element_type=F32))
        for (d, bbs), red in zip(groups, reds):
            _, _, _, w_ref, ka_ref, kd_ref, _, rev = dirs[d]
            tau = row_of(rev, tt)
            for k, bb in enumerate(bbs):
                w = w_ref[bb, pl.ds(tau, 1), :]
                ka = ka_ref[bb, pl.ds(tau, 1), :]
                kd = kd_ref[bb, pl.ds(tau, 1), :]
                for p in range(n_pair):
                    sl = slice(p * LANES, (p + 1) * LANES)
                    r0 = (k * n_pair + p) * HD_B
                    sa = red[r0:r0 + HD_B, :LANES]
                    vcol = red[r0:r0 + HD_B, LANES:]
                    s_scr[d, bb, p] = s_scr[d, bb, p] * w[:, sl] + sa * ka[:, sl] + vcol * kd[:, sl]
        return carry

    lax.fori_loop(0, RW_TB, step, 0)
    for d in range(2):
        for bb in range(RW_BB):
            emit_output(d, bb, row_of(dirs[d][7], RW_TB - 1))

    @pl.when(tb == n_tb - 1)
    def _():
        sff_ref[...] = s_scr[0]
        sfb_ref[...] = s_scr[1]


def _rwkv_scan(pp, row0, n_seq, seq, s0_f, s0_b, ones_pair):
    n_tb = seq // RW_TB
    n_pair = H_B // 2
    blk0 = row0 // seq
    view = lambda a: a.reshape(a.shape[0] // seq, seq, W_B)
    fwd = pl.BlockSpec((RW_BB, RW_TB, W_B), lambda b, t: (blk0 // RW_BB + b, t, 0))
    bwd = pl.BlockSpec((RW_BB, RW_TB, W_B), lambda b, t: (blk0 // RW_BB + b, n_tb - 1 - t, 0))
    st = pl.BlockSpec((RW_BB, n_pair, HD_B, LANES), lambda b, t: (b, 0, 0, 0))
    o_f = pl.BlockSpec((RW_BB, RW_TB, W_B), lambda b, t: (b, t, 0))
    o_b = pl.BlockSpec((RW_BB, RW_TB, W_B), lambda b, t: (b, n_tb - 1 - t, 0))
    ins_f = [view(pp[k]) for k in ('nkk', 'r', 'v', 'w_f', 'ka_f', 'kd_f')]
    ins_b = [view(pp[k]) for k in ('nkk', 'r', 'v', 'w_b', 'ka_b', 'kd_b')]
    st_shape = jax.ShapeDtypeStruct((n_seq, n_pair, HD_B, LANES), F32)
    o_shape = jax.ShapeDtypeStruct((n_seq, seq, W_B), F32)
    return pl.pallas_call(
        functools.partial(_rwkv_scan_kernel, n_tb),
        grid=(n_seq // RW_BB, n_tb),
        in_specs=[fwd] * 6 + [bwd] * 6 + [st, st, pl.BlockSpec(ones_pair.shape, lambda b, t: (0, 0))],
        out_specs=[o_f, o_b, st, st],
        out_shape=[o_shape, o_shape, st_shape, st_shape],
        scratch_shapes=[pltpu.VMEM((2, RW_BB, n_pair, HD_B, LANES), F32),
                        pltpu.VMEM((2, RW_BB, n_pair, 2, HD_B, LANES), F32)],
        compiler_params=_cparams(2),
        name="rwkv_scan",
    )(*ins_f, *ins_b, s0_f, s0_b, ones_pair)


def _state_to_pairs(s):
    b = s.shape[0]
    return s.reshape(b, H_B // 2, 2, HD_B, HD_B).transpose(0, 1, 3, 2, 4).reshape(b, H_B // 2, HD_B, 2 * HD_B)


def _pairs_to_state(s):
    b = s.shape[0]
    return s.reshape(b, H_B // 2, HD_B, 2, HD_B).transpose(0, 1, 3, 2, 4).reshape(b, H_B, HD_B, HD_B)


def _tail(x, y, g1, n2g, sc2, sh2, x1_ref, h_ref, hp_ref):
    x1 = x + g1 * y
    x1_ref[...] = x1
    h = _rms_mod(x1, n2g, sc2, sh2)
    h_ref[...] = _bf(h)
    hp_ref[...] = _pack_pairs(h)


def _outproj0_kernel(x_ref, oa_ref, of_ref, ob_ref, bonus_ref, g_ref, lnw_ref, lnb_ref, ones_ref,
                     w_ref, g1_ref, n2g_ref, sc2_ref, sh2_ref, x1_ref, h_ref, hp_ref):
    o_sum = of_ref[...] + ob_ref[...]
    ones2 = ones_ref[...]
    mean = _seg_sum(o_sum, ones2) * (1.0 / HD_B)
    cen = o_sum - mean
    var = _seg_sum(cen * cen, ones2) * (1.0 / HD_B)
    gn = cen * lax.rsqrt(var + GN_EPS) * lnw_ref[...] + lnb_ref[...]
    o_rw = (gn + bonus_ref[...]) * g_ref[...]
    mix = jnp.concatenate([_bf(oa_ref[...]), _bf(o_rw)], axis=1)
    y = jnp.dot(mix, w_ref[...], preferred_element_type=F32)
    _tail(x_ref[...], y, g1_ref[...], n2g_ref[...], sc2_ref[...], sh2_ref[...], x1_ref, h_ref, hp_ref)


def _outproj0(rows, x, o_att, o_f, o_b, pp, pr, ones_b, w_out_bf, n2g, mod4, layer):
    n = rows.n
    tok = lambda w: pl.BlockSpec((TM, w), lambda i: (i, 0))
    const = lambda a: pl.BlockSpec(a.shape, lambda i: (0,) * a.ndim)
    lnw, lnb, n2 = pr['ln_w'].reshape(1, -1), pr['ln_b'].reshape(1, -1), n2g.reshape(1, -1)
    return pl.pallas_call(
        _outproj0_kernel,
        grid=(n // TM,),
        in_specs=[tok(D_MODEL), tok(W_A), tok(W_B), tok(W_B), tok(W_B), tok(W_B),
                  const(lnw), const(lnb), const(ones_b), const(w_out_bf),
                  rows.mod_spec(layer, 2, TM), const(n2), rows.mod_spec(layer, 4, TM), rows.mod_spec(layer, 3, TM)],
        out_specs=[tok(D_MODEL), tok(D_MODEL), tok(D_MODEL // 2)],
        out_shape=[jax.ShapeDtypeStruct((n, D_MODEL), F32), jax.ShapeDtypeStruct((n, D_MODEL), BF16),
                   jax.ShapeDtypeStruct((n, D_MODEL // 2), jnp.uint32)],
        compiler_params=_cparams(1),
        name="outproj0",
    )(x, o_att, o_f, o_b, pp['bonus'], pp['g'], lnw, lnb, ones_b, w_out_bf, mod4, n2, mod4, mod4)


def _outproj1_kernel(x_ref, of_ref, ob_ref, gate_ref, ng_ref, w_ref, g1_ref, n2g_ref, sc2_ref, sh2_ref,
                     x1_ref, h_ref, hp_ref):
    o_sum = of_ref[...] + ob_ref[...]
    parts = []
    for h in range(H_C):
        oh = o_sum[:, h * DV_C:(h + 1) * DV_C]
        parts.append(oh * lax.rsqrt(jnp.mean(oh * oh, axis=-1, keepdims=True) + EPS))
    o = jnp.concatenate(parts, axis=1) * ng_ref[...] * _silu(gate_ref[...])
    y = jnp.dot(_bf(o), w_ref[...], preferred_element_type=F32)
    _tail(x_ref[...], y, g1_ref[...], n2g_ref[...], sc2_ref[...], sh2_ref[...], x1_ref, h_ref, hp_ref)


def _outproj1(rows, x, o_f, o_b, p1, norm_g, w_out_bf, n2g, mod4, layer):
    n = rows.n
    tok = lambda w: pl.BlockSpec((TM, w), lambda i: (i, 0))
    const = lambda a: pl.BlockSpec(a.shape, lambda i: (0,) * a.ndim)
    ng, n2 = norm_g.reshape(1, -1), n2g.reshape(1, -1)
    return pl.pallas_call(
        _outproj1_kernel,
        grid=(n // TM,),
        in_specs=[tok(D_MODEL), tok(D_C), tok(D_C), pl.BlockSpec((TM, D_C), lambda i: (i, 4)),
                  const(ng), const(w_out_bf),
                  rows.mod_spec(layer, 2, TM), const(n2), rows.mod_spec(layer, 4, TM), rows.mod_spec(layer, 3, TM)],
        out_specs=[tok(D_MODEL), tok(D_MODEL), tok(D_MODEL // 2)],
        out_shape=[jax.ShapeDtypeStruct((n, D_MODEL), F32), jax.ShapeDtypeStruct((n, D_MODEL), BF16),
                   jax.ShapeDtypeStruct((n, D_MODEL // 2), jnp.uint32)],
        compiler_params=_cparams(1),
        name="outproj1",
    )(x, o_f, o_b, p1, ng, w_out_bf, mod4, n2, mod4, mod4)


def _hgrn_kernel(n_tb, qf_ref, ff_ref, if_ref, qb_ref, fb_ref, ib_ref, lbp_ref, s0f_ref, s0b_ref,
                 trif_ref, trib_ref, of_ref, ob_ref, sff_ref, sfb_ref, s_scr):
    tb = pl.program_id(1)

    @pl.when(tb == 0)
    def _():
        for h in range(H_C):
            s_scr[0, h] = s0f_ref[0, h].T
            s_scr[1, h] = s0b_ref[0, h].T

    lbp = lbp_ref[...]
    e = jnp.exp(lbp - jnp.max(lbp, axis=0, keepdims=True))
    sm = e / jnp.sum(e, axis=0, keepdims=True)
    lb = (sm[0:1] + sm[1:2]) - sm[0:1]

    n_chunk = HG_TB // CHUNK
    ti = lax.broadcasted_iota(jnp.int32, (CHUNK, CHUNK), 0)
    si = lax.broadcasted_iota(jnp.int32, (CHUNK, CHUNK), 1)
    dirs = ((qf_ref, ff_ref, if_ref, of_ref, trif_ref, ti >= si, CHUNK - 1, False),
            (qb_ref, fb_ref, ib_ref, ob_ref, trib_ref, ti <= si, 0, True))

    def chunk(c, carry):
        for d, (q_ref, f_ref, i_ref, o_ref, tri_ref, causal, last_row, rev) in enumerate(dirs):
            cc = n_chunk - 1 - c if rev else c
            r0 = pl.multiple_of(cc * CHUNK, CHUNK)
            tri3 = tri_ref[...]
            for h in range(H_C):
                sl = slice(h * DK_C, (h + 1) * DK_C)
                q = _silu(q_ref[pl.ds(r0, CHUNK), sl])
                f = lb[:, sl] + (1.0 - lb[:, sl]) * _sigmoid(f_ref[pl.ds(r0, CHUNK), sl])
                k = 1.0 - f
                v = _bf(i_ref[pl.ds(r0, CHUNK), sl])
                g = jnp.log(f)
                g1 = _bf(g)
                g2 = _bf(g - g1.astype(F32))
                g3 = _bf(g - g1.astype(F32) - g2.astype(F32))
                b = jnp.dot(tri3, jnp.concatenate([g1, g2, g3], axis=0), preferred_element_type=F32)
                b_last = b[last_row:last_row + 1]
                q_in = _bf(q * jnp.exp(b))
                k_in = _bf(k * jnp.exp(-b))
                k_out = _bf(k * jnp.exp(b_last - b))
                att = jnp.where(causal, _dot_nt(q_in, k_in), 0.0)
                s_t = s_scr[d, h]
                o = jnp.dot(_bf(att), v, preferred_element_type=F32) + _dot_nt(q_in, _bf(s_t))
                o_ref[pl.ds(r0, CHUNK), sl] = o
                s_scr[d, h] = jnp.exp(b_last) * s_t + _dot_tn(v, k_out)
        return carry

    lax.fori_loop(0, n_chunk, chunk, 0)

    @pl.when(tb == n_tb - 1)
    def _():
        for h in range(H_C):
            sff_ref[0, h] = s_scr[0, h].T
            sfb_ref[0, h] = s_scr[1, h].T


def _hgrn_scan(p1, row0, n_seq, seq, lb_params, s0_f, s0_b):
    n_tb = seq // HG_TB
    blk0 = row0 // HG_TB
    tri = np.tril(np.ones((CHUNK, CHUNK), np.float32))
    tri_f = jnp.asarray(np.concatenate([tri] * 3, axis=1), dtype=BF16)
    tri_b = jnp.asarray(np.concatenate([tri.T] * 3, axis=1), dtype=BF16)
    fwd = lambda col: pl.BlockSpec((HG_TB, D_C), lambda b, t: (blk0 + b * n_tb + t, col))
    bwd = lambda col: pl.BlockSpec((HG_TB, D_C), lambda b, t: (blk0 + b * n_tb + n_tb - 1 - t, col))
    st = pl.BlockSpec((1, H_C, DK_C, DV_C), lambda b, t: (b, 0, 0, 0))
    const = lambda a: pl.BlockSpec(a.shape, lambda b, t: (0,) * a.ndim)
    o_shape = jax.ShapeDtypeStruct((n_seq * seq, D_C), F32)
    st_shape = jax.ShapeDtypeStruct((n_seq, H_C, DK_C, DV_C), F32)
    return pl.pallas_call(
        functools.partial(_hgrn_kernel, n_tb),
        grid=(n_seq, n_tb),
        in_specs=[fwd(0), fwd(1), fwd(3), bwd(0), bwd(2), bwd(3), const(lb_params), st, st,
                  const(tri_f), const(tri_b)],
        out_specs=[pl.BlockSpec((HG_TB, D_C), lambda b, t: (b * n_tb + t, 0)),
                   pl.BlockSpec((HG_TB, D_C), lambda b, t: (b * n_tb + n_tb - 1 - t, 0)), st, st],
        out_shape=[o_shape, o_shape, st_shape, st_shape],
        scratch_shapes=[pltpu.VMEM((2, H_C, DV_C, DK_C), F32)],
        compiler_params=_cparams(2),
        name="hgrn_scan",
    )(p1, p1, p1, p1, p1, p1, lb_params, s0_f, s0_b, tri_f, tri_b)


def _router_kernel(h_ref, rhi_ref, rlo_ref, bias_ref, sel_ref, eidx_ref, ew_ref, cnt_ref):
    x = h_ref[...]
    tm = x.shape[0]
    logits = _dot_nt(rhi_ref[...], x) + _dot_nt(rlo_ref[...], x)
    scores = _sigmoid(logits)
    biased = scores + bias_ref[...]
    per = N_EXPERTS // N_GROUPS
    sub = lax.broadcasted_iota(jnp.int32, (per, tm), 0)
    gs_rows = []
    for g in range(N_GROUPS):
        blk = biased[g * per:(g + 1) * per]
        m1 = jnp.max(blk, axis=0, keepdims=True)
        first = jnp.min(jnp.where(blk == m1, sub, per), axis=0, keepdims=True)
        m2 = jnp.max(jnp.where(sub == first, -jnp.inf, blk), axis=0, keepdims=True)
        gs_rows.append(m1 + m2)
    gs = jnp.concatenate(gs_rows, axis=0)
    gi = lax.broadcasted_iota(jnp.int32, gs.shape, 0)
    rank = jnp.zeros(gs.shape, jnp.int32)
    for s in range(1, N_GROUPS):
        other = pltpu.roll(gs, s, 0)
        oi = pltpu.roll(gi, s, 0)
        beats = jnp.logical_or(other > gs, jnp.logical_and(other == gs, oi < gi))
        rank = rank + jnp.where(beats, 1, 0)
    keep = jnp.where(rank < TOPK_GROUPS, 1.0, 0.0)
    emask = jnp.concatenate([jnp.broadcast_to(keep[g:g + 1], (per, tm)) for g in range(N_GROUPS)], axis=0)
    cur = jnp.where(emask > 0.0, biased, -jnp.inf)
    ei = lax.broadcasted_iota(jnp.int32, cur.shape, 0)
    sel = jnp.zeros(cur.shape, F32)
    idxs, vals = [], []
    for _ in range(TOP_K):
        m = jnp.max(cur, axis=0, keepdims=True)
        idx = jnp.min(jnp.where(cur == m, ei, N_EXPERTS), axis=0, keepdims=True)
        pick = ei == idx
        idxs.append(idx)
        vals.append(jnp.sum(jnp.where(pick, scores, 0.0), axis=0, keepdims=True))
        sel = jnp.where(pick, 1.0, sel)
        cur = jnp.where(pick, -jnp.inf, cur)
    w = jnp.concatenate(vals, axis=0)
    eidx_ref[...] = jnp.concatenate(idxs, axis=0)
    ew_ref[...] = w / jnp.sum(w, axis=0, keepdims=True) * ROUTED_SCALE
    sel_ref[...] = _bf(sel)

    @pl.when(pl.program_id(0) == 0)
    def _():
        cnt_ref[...] = jnp.zeros_like(cnt_ref)

    cnt_ref[...] += jnp.sum(sel, axis=1, keepdims=True)


def _router(hffn, router, bias):
    n = hffn.shape[0]
    r_t = router.T
    r_hi = _bf(r_t)
    r_lo = _bf(r_t - r_hi.astype(F32))
    const = lambda a: pl.BlockSpec(a.shape, lambda i: (0,) * a.ndim)
    b_col = bias.reshape(N_EXPERTS, 1)
    return pl.pallas_call(
        _router_kernel,
        grid=(n // TM,),
        in_specs=[pl.BlockSpec((TM, D_MODEL), lambda i: (i, 0)), const(r_hi), const(r_lo), const(b_col)],
        out_specs=[pl.BlockSpec((N_EXPERTS, TM), lambda i: (0, i)),
                   pl.BlockSpec((TOP_K, TM), lambda i: (0, i)),
                   pl.BlockSpec((TOP_K, TM), lambda i: (0, i)),
                   pl.BlockSpec((N_EXPERTS, LANES), lambda i: (0, 0))],
        out_shape=[jax.ShapeDtypeStruct((N_EXPERTS, n), BF16),
                   jax.ShapeDtypeStruct((TOP_K, n), jnp.int32),
                   jax.ShapeDtypeStruct((TOP_K, n), F32),
                   jax.ShapeDtypeStruct((N_EXPERTS, LANES), F32)],
        compiler_params=_cparams(1),
        name="router",
    )(hffn, r_hi, r_lo, b_col)


def _positions_kernel(sel_ref, eidx_ref, base_ref, upper_ref, pos_ref, carry_ref):
    @pl.when(pl.program_id(0) == 0)
    def _():
        carry_ref[...] = jnp.zeros_like(carry_ref)

    sel = sel_ref[...]
    rank = jnp.dot(sel, upper_ref[...], preferred_element_type=F32)
    pos_e = base_ref[:, 0:1] + carry_ref[:, 0:1] + rank
    ei = lax.broadcasted_iota(jnp.int32, pos_e.shape, 0)
    eidx = eidx_ref[...]
    rows = [jnp.sum(jnp.where(ei == eidx[k:k + 1], pos_e, 0.0), axis=0, keepdims=True) for k in range(TOP_K)]
    pos_ref[...] = jnp.concatenate(rows, axis=0).astype(jnp.int32)
    carry_ref[...] += jnp.sum(sel.astype(F32), axis=1, keepdims=True)


def _positions(sel, eidx, base):
    n = sel.shape[1]
    pb = POS_TB
    upper = jnp.asarray(np.triu(np.ones((pb, pb), np.float32), 1), dtype=BF16)
    return pl.pallas_call(
        _positions_kernel,
        grid=(n // pb,),
        in_specs=[pl.BlockSpec((N_EXPERTS, pb), lambda i: (0, i)),
                  pl.BlockSpec((TOP_K, pb), lambda i: (0, i)),
                  pl.BlockSpec((N_EXPERTS, LANES), lambda i: (0, 0)),
                  pl.BlockSpec((pb, pb), lambda i: (0, 0))],
        out_specs=pl.BlockSpec((TOP_K, pb), lambda i: (0, i)),
        out_shape=jax.ShapeDtypeStruct((TOP_K, n), jnp.int32),
        scratch_shapes=[pltpu.VMEM((N_EXPERTS, LANES), F32)],
        compiler_params=_cparams(1),
        name="positions",
    )(sel, eidx, base, upper)


def _pack_pairs(x):
    half = x.shape[1] // 2
    bits = lax.bitcast_convert_type(_bf(x).astype(F32), jnp.uint32)
    return (bits[:, :half] >> 16) | (bits[:, half:] & jnp.uint32(0xFFFF0000))


def _unpack_pairs(w):
    lo = lax.bitcast_convert_type(w << 16, F32)
    hi = lax.bitcast_convert_type(w & jnp.uint32(0xFFFF0000), F32)
    return jnp.concatenate([_bf(lo), _bf(hi)], axis=1)


def _sc_gather(table, idx):
    b, w = idx.shape[0], table.shape[1]
    n_workers = SC_CORES * SC_SUBCORES
    per_w = b // n_workers
    assert b % (n_workers * SC_CHUNK) == 0
    mesh = plsc.VectorSubcoreMesh(core_axis_name="c", subcore_axis_name="s")

    @functools.partial(
        pl.kernel, mesh=mesh, out_type=jax.ShapeDtypeStruct((b, w), table.dtype),
        scratch_types=[pltpu.VMEM((SC_CHUNK,), jnp.int32), pltpu.VMEM((SC_CHUNK, w), table.dtype),
                       pltpu.SemaphoreType.DMA])
    def gather(table_hbm, idx_hbm, out_hbm, idx_v, rows_v, sem):
        wid = lax.axis_index("s") * SC_CORES + lax.axis_index("c")
        base = wid * per_w

        @pl.loop(0, per_w // SC_CHUNK)
        def _(c):
            off = pl.multiple_of(base + c * SC_CHUNK, SC_CHUNK)
            pltpu.sync_copy(idx_hbm.at[pl.ds(off, SC_CHUNK)], idx_v)
            pltpu.async_copy(table_hbm.at[idx_v], rows_v, sem).wait()
            pltpu.sync_copy(rows_v, out_hbm.at[pl.ds(off, SC_CHUNK)])

    return gather(table, idx)


def _sc_scatter(src, pos3, n_rows):
    n, w = src.shape
    n_workers = SC_CORES * SC_SUBCORES
    per_w = n // n_workers
    assert n % (n_workers * SC_CHUNK) == 0
    mesh = plsc.VectorSubcoreMesh(core_axis_name="c", subcore_axis_name="s")

    @functools.partial(
        pl.kernel, mesh=mesh, out_type=jax.ShapeDtypeStruct((n_rows, w), src.dtype),
        scratch_types=[pltpu.VMEM((TOP_K, SC_CHUNK), jnp.int32), pltpu.VMEM((SC_CHUNK, w), src.dtype),
                       pltpu.SemaphoreType.DMA])
    def scatter(src_hbm, pos_hbm, out_hbm, idx_v, rows_v, sem):
        wid = lax.axis_index("s") * SC_CORES + lax.axis_index("c")
        base = wid * per_w

        @pl.loop(0, per_w // SC_CHUNK)
        def _(c):
            off = pl.multiple_of(base + c * SC_CHUNK, SC_CHUNK)
            pltpu.sync_copy(src_hbm.at[pl.ds(off, SC_CHUNK)], rows_v)
            pltpu.sync_copy(pos_hbm.at[off // SC_CHUNK], idx_v)
            copies = [pltpu.async_copy(rows_v, out_hbm.at[idx_v.at[k]], sem) for k in range(TOP_K)]
            for cp in copies:
                cp.wait()

    return scatter(src, pos3)


def _experts_kernel(te_ref, nu_ref, xs_ref, wg_ref, wu_ref, wd_ref, ys_ref):
    @pl.when(pl.program_id(0) < nu_ref[0])
    def _():
        x = _unpack_pairs(xs_ref[...])
        act = _glu(x, wg_ref[0], wu_ref[0])
        ys_ref[...] = _pack_pairs(jnp.dot(_bf(act), wd_ref[0], preferred_element_type=F32))


def _experts(xs, tile_expert, n_used, mp):
    n_tiles = xs.shape[0] // MOE_TILE
    half = D_MODEL // 2
    wspec = lambda shape: pl.BlockSpec((1,) + shape, lambda i, te, nu: (te[i], 0, 0))
    return pl.pallas_call(
        _experts_kernel,
        grid_spec=pltpu.PrefetchScalarGridSpec(
            num_scalar_prefetch=2, grid=(n_tiles,),
            in_specs=[pl.BlockSpec((MOE_TILE, half), lambda i, te, nu: (i, 0)),
                      wspec((D_MODEL, D_EXPERT)), wspec((D_MODEL, D_EXPERT)), wspec((D_EXPERT, D_MODEL))],
            out_specs=pl.BlockSpec((MOE_TILE, half), lambda i, te, nu: (i, 0))),
        out_shape=jax.ShapeDtypeStruct(xs.shape, jnp.uint32),
        compiler_params=_cparams(1),
        name="experts",
    )(tile_expert, n_used, xs, mp['wg'], mp['wu'], mp['wd'])


def _combine_kernel(h_ref, *refs):
    yg_refs = refs[:TOP_K]
    ew_ref, eye_ref, sg_ref, su_ref, sd_ref, x1_ref, g2_ref, o_ref = refs[TOP_K:]
    act = _glu(h_ref[...], sg_ref[...], su_ref[...])
    acc = jnp.dot(_bf(act), sd_ref[...], preferred_element_type=F32)
    ew = ew_ref[...]
    hi = _bf(ew)
    lo = _bf(ew - hi.astype(F32))
    ew_t = _dot_tn(hi, eye_ref[...]) + _dot_tn(lo, eye_ref[...])
    for k in range(TOP_K):
        acc = acc + ew_t[:, k:k + 1] * _unpack_pairs(yg_refs[k][...]).astype(F32)
    o_ref[...] = x1_ref[...] + g2_ref[...] * acc


def _combine(rows, hffn, yg, ew, mp, x1, mod4, layer):
    n = rows.n
    half = D_MODEL // 2
    n_blk = n // TM
    const = lambda a: pl.BlockSpec(a.shape, lambda i: (0,) * a.ndim)
    tok = lambda w: pl.BlockSpec((TM, w), lambda i: (i, 0))
    slot = lambda k: pl.BlockSpec((TM, half), lambda i: (k * n_blk + i, 0))
    eye = jnp.eye(TOP_K, dtype=BF16)
    return pl.pallas_call(
        _combine_kernel,
        grid=(n_blk,),
        in_specs=[tok(D_MODEL)] + [slot(k) for k in range(TOP_K)]
                 + [pl.BlockSpec((TOP_K, TM), lambda i: (0, i)), const(eye),
                    const(mp['sg']), const(mp['su']), const(mp['sd']), tok(D_MODEL), rows.mod_spec(layer, 5, TM)],
        out_specs=tok(D_MODEL),
        out_shape=jax.ShapeDtypeStruct((n, D_MODEL), F32),
        compiler_params=_cparams(1),
        name=f"combine{layer}",
    )(hffn, *([yg] * TOP_K), ew, eye, mp['sg'], mp['su'], mp['sd'], x1, mod4)


def _moe(rows, hffn, hpack, x1, router, bias, mp, mod4, layer):
    n = rows.n
    sel, eidx, ew, cnt = _router(hffn, router, bias)
    counts = cnt[:, 0].astype(jnp.int32)
    padded = (counts + MOE_TILE - 1) // MOE_TILE * MOE_TILE
    ends = jnp.cumsum(padded)
    n_rows = n * TOP_K + N_EXPERTS * MOE_TILE
    n_tiles = n_rows // MOE_TILE
    base = jnp.broadcast_to((ends - padded).astype(F32)[:, None], (N_EXPERTS, LANES))
    tile_start = jnp.arange(n_tiles, dtype=jnp.int32) * MOE_TILE
    tile_expert = jnp.minimum(jnp.sum((ends[None, :] <= tile_start[:, None]).astype(jnp.int32), axis=1),
                              N_EXPERTS - 1)
    n_used = (ends[-1:] // MOE_TILE).astype(jnp.int32)
    pos = _positions(sel, eidx, base)
    pos3 = pos.reshape(TOP_K, n // SC_CHUNK, SC_CHUNK).transpose(1, 0, 2)
    xs = _sc_scatter(hpack, pos3, n_rows)
    ys = _experts(xs, tile_expert, n_used, mp)
    yg = _sc_gather(ys, pos.reshape(-1))
    return _combine(rows, hffn, yg, ew, mp, x1, mod4, layer)


def _glu(x, wg, wu):
    hg = jnp.dot(x, wg, preferred_element_type=F32)
    hu = jnp.dot(x, wu, preferred_element_type=F32)
    return _silu(hg) * hu


def kernel(x_prompt, x_sample, c, c_ctx, cache_attn_k, cache_attn_v, state_rwkv_fwd, state_rwkv_bwd,
           state_hgrn_fwd, state_hgrn_bwd, norm1_g, norm2_g, mod_w, mod_b, ab_w_in, ab_w_out, attn_q_norm,
           attn_k_norm, attn_sink, rwkv_mu, rwkv_w0, rwkv_w2, rwkv_a0, rwkv_a2, rwkv_g2, rwkv_k_k, rwkv_k_a,
           rwkv_r_k, rwkv_ln_w, rwkv_ln_b, hgrn_w_in, hgrn_w_out, hgrn_lower_bounds, hgrn_norm_g, moe_router,
           moe_bias, moe_w_gate, moe_w_up, moe_w_down, moe_shared_gate, moe_shared_up, moe_shared_down):
    n_cseq, cseq, _ = x_prompt.shape
    n_lseq, lseq, _ = x_sample.shape
    depth = mod_w.shape[0]
    assert depth == 2 and n_lseq + 1 <= SUBLANES
    assert cseq == TM and lseq % TM == 0 and lseq % HG_TB == 0 and cseq % HG_TB == 0
    assert n_cseq % RW_BB == 0 and n_lseq % RW_BB == 0 and (n_cseq * cseq) % (lseq * RW_BB) == 0
    rows = _Rows(n_cseq * cseq, n_lseq * lseq, lseq)
    assert rows.n % MOE_TILE == 0 and lseq % MOE_TILE == 0 and rows.n_ctx % MOE_TILE == 0
    kv_w = KV_A * HD_A

    x = jnp.concatenate([x_prompt.reshape(rows.n_ctx, D_MODEL), x_sample.reshape(rows.n_lat, D_MODEL)], axis=0)
    cvecs = jnp.concatenate([c_ctx[None, :], c, jnp.zeros((SUBLANES - 1 - n_lseq, D_MODEL), F32)], axis=0)
    mod4 = _modulation(cvecs, mod_w, mod_b).reshape(depth, SUBLANES, 1, 6 * D_MODEL)

    ones_q = _block_ones(W_A, HD_A)
    ones_k = _block_ones(kv_w, HD_A)
    ones_b = _block_ones(W_B, HD_B)
    ones_pair = _block_ones(LANES, HD_B)[:LANES]
    ones_pair = jnp.kron(jnp.eye(2, dtype=BF16), ones_pair)
    cos_t, sin_t = _rope_tables(lseq)

    def moe(l, hffn, hpack, x1):
        mp = {'wg': _bf(moe_w_gate[l]), 'wu': _bf(moe_w_up[l]), 'wd': _bf(moe_w_down[l]),
              'sg': _bf(moe_shared_gate[l]), 'su': _bf(moe_shared_up[l]), 'sd': _bf(moe_shared_down[l])}
        return _moe(rows, hffn, hpack, x1, moe_router[l], moe_bias[l], mp, mod4, l)

    pr = {'mu': rwkv_mu[0], 'w0': rwkv_w0[0], 'w2': rwkv_w2[0], 'a0': rwkv_a0[0], 'a2': rwkv_a2[0],
          'g2': rwkv_g2[0], 'k_k': rwkv_k_k[0], 'k_a': rwkv_k_a[0], 'r_k': rwkv_r_k[0].reshape(-1),
          'ln_w': rwkv_ln_w[0], 'ln_b': rwkv_ln_b[0]}
    p_att, p_rw = _inproj(rows, x, norm1_g[0], mod4, 0, _bf(ab_w_in[0]), (ATT_IN, RWKV_IN))
    qg_t = jnp.tile(attn_q_norm[0], H_A).reshape(1, W_A)
    kg_t = jnp.tile(attn_k_norm[0], KV_A).reshape(1, kv_w)
    o_att_c, new_k, new_v = _ctx_attention(p_att, n_cseq, cseq, qg_t, kg_t, attn_sink[0], ones_q, ones_k)
    past = cache_attn_k.shape[2]
    o_att_l = _lat_attention(p_att, rows.n_ctx // lseq, n_lseq, lseq, qg_t, kg_t, attn_sink[0], ones_q, ones_k,
                             cos_t, sin_t, cache_attn_k[:, 0].reshape(n_lseq, past, kv_w),
                             cache_attn_v[:, 0].reshape(n_lseq, past, kv_w))
    o_att = jnp.concatenate([o_att_c, o_att_l], axis=0)

    pp = _rwkv_prep(rows, p_rw, pr, ones_b)
    zero_st = jnp.zeros((n_cseq, H_B // 2, HD_B, LANES), F32)
    of_c, ob_c, sf_c, sb_c = _rwkv_scan(pp, 0, n_cseq, cseq, zero_st, zero_st, ones_pair)
    of_l, ob_l, _, _ = _rwkv_scan(pp, rows.n_ctx, n_lseq, lseq, _state_to_pairs(state_rwkv_fwd[:, 0]),
                                  _state_to_pairs(state_rwkv_bwd[:, 0]), ones_pair)
    o_f = jnp.concatenate([of_c.reshape(rows.n_ctx, W_B), of_l.reshape(rows.n_lat, W_B)], axis=0)
    o_b = jnp.concatenate([ob_c.reshape(rows.n_ctx, W_B), ob_l.reshape(rows.n_lat, W_B)], axis=0)
    x1, hffn, hpack = _outproj0(rows, x, o_att, o_f, o_b, pp, pr, ones_b, _bf(ab_w_out[0]), norm2_g[0], mod4, 0)
    x = moe(0, hffn, hpack, x1)

    (p1,) = _inproj(rows, x, norm1_g[1], mod4, 1, _bf(hgrn_w_in[0]), (IN_C,))
    zero_h = jnp.zeros((n_cseq, H_C, DK_C, DV_C), F32)
    hf_c, hb_c, hsf_c, hsb_c = _hgrn_scan(p1, 0, n_cseq, cseq, hgrn_lower_bounds, zero_h, zero_h)
    hf_l, hb_l, _, _ = _hgrn_scan(p1, rows.n_ctx, n_lseq, lseq, hgrn_lower_bounds,
                                  state_hgrn_fwd[:, 0], state_hgrn_bwd[:, 0])
    h_f = jnp.concatenate([hf_c, hf_l], axis=0)
    h_b = jnp.concatenate([hb_c, hb_l], axis=0)
    x1, hffn, hpack = _outproj1(rows, x, h_f, h_b, p1, hgrn_norm_g[0], _bf(hgrn_w_out[0]), norm2_g[1], mod4, 1)
    x = moe(1, hffn, hpack, x1)

    y_prompt = x[:rows.n_ctx].reshape(n_cseq, cseq, D_MODEL)
    y_sample = x[rows.n_ctx:].reshape(n_lseq, lseq, D_MODEL)
    return (y_prompt, y_sample,
            new_k.reshape(n_cseq, 1, cseq, KV_A, HD_A), new_v.reshape(n_cseq, 1, cseq, KV_A, HD_A),
            _pairs_to_state(sf_c)[:, None], _pairs_to_state(sb_c)[:, None],
            hsf_c[:, None], hsb_c[:, None])
```

```python
import functools

import numpy as np
import jax
import jax.numpy as jnp
from jax import lax
from jax.experimental import pallas as pl
from jax.experimental.pallas import tpu as pltpu
from jax.experimental.pallas import tpu_sc as plsc

F32 = jnp.float32
BF16 = jnp.bfloat16

D_MODEL = 1024
GRID_W = 64
H_A = 8
KV_A = 2
G_A = H_A // KV_A
HD_A = 64
W_A = H_A * HD_A
WINDOW = 128
QBLK = 128
ROPE_BASE = 10000.0
ATTN_SCALE = HD_A ** -0.5
NEG_INF = -1e30
H_B = 8
HD_B = 64
W_B = H_B * HD_B
LORA_W = 64
LORA_A = 64
LORA_G = 128
GN_EPS = 64e-5
ATT_IN = W_A + 2 * KV_A * HD_A
RWKV_IN = 3 * W_B + LORA_W + LORA_A + LORA_G
IN_AB = ATT_IN + RWKV_IN
H_C = 8
DK_C = 128
DV_C = 128
D_C = H_C * DV_C
CHUNK = 64
IN_C = 5 * D_C
N_EXPERTS = 64
TOP_K = 8
N_GROUPS = 8
TOPK_GROUPS = 4
D_EXPERT = 256
ROUTED_SCALE = 2.5
EPS = 1e-6

LANES = 128
SUBLANES = 8
VMEM_LIMIT = 52 * 1024 * 1024

TM = 256
RW_TB = 128
RW_BB = 4
RW_GROUP_BB = 4
HG_TB = 256
MOE_TILE = 512
POS_TB = 512
SC_CORES = 2
SC_SUBCORES = 16
SC_CHUNK = 64


def _cparams(n_axes):
    return pltpu.CompilerParams(dimension_semantics=("arbitrary",) * n_axes,
                                vmem_limit_bytes=VMEM_LIMIT)


def _bf(x):
    return x.astype(BF16)


def _split2(x):
    hi = lax.bitcast_convert_type(
        lax.bitcast_convert_type(x, jnp.uint32) & jnp.uint32(0xFFFF0000), F32)
    return hi, x - hi


def _seg_sum(x, ones2):
    hi, lo = _split2(x)
    return jnp.dot(jnp.concatenate([_bf(hi), _bf(lo)], axis=1), ones2,
                   preferred_element_type=F32)


def _dot_nt(a, b):
    return lax.dot_general(a, b, (((1,), (1,)), ((), ())), preferred_element_type=F32)


def _dot_tn(a, b):
    return lax.dot_general(a, b, (((0,), (0,)), ((), ())), preferred_element_type=F32)


def _sigmoid(x):
    return 1.0 / (1.0 + jnp.exp(-x))


def _silu(x):
    return x * _sigmoid(x)


def _chunks(seq, n):
    seq = list(seq)
    return [seq[i:i + n] for i in range(0, len(seq), n)]


def _block_ones(width, seg):
    idx = np.arange(width) // seg
    bd = (idx[:, None] == idx[None, :]).astype(np.float32)
    return jnp.asarray(np.concatenate([bd, bd], axis=0), dtype=BF16)


def _mod_kernel(c_ref, w_ref, b_ref, o_ref):
    s = _silu(c_ref[...])
    o_ref[0] = jnp.dot(_bf(s), _bf(w_ref[0]), preferred_element_type=F32) + b_ref[0]


def _modulation(cvecs, mod_w, mod_b):
    depth = mod_w.shape[0]
    n_col = 6 * D_MODEL // D_MODEL
    return pl.pallas_call(
        _mod_kernel,
        grid=(depth, n_col),
        in_specs=[pl.BlockSpec((SUBLANES, D_MODEL), lambda l, j: (0, 0)),
                  pl.BlockSpec((1, D_MODEL, D_MODEL), lambda l, j: (l, 0, j)),
                  pl.BlockSpec((1, 1, D_MODEL), lambda l, j: (l, 0, j))],
        out_specs=pl.BlockSpec((1, SUBLANES, D_MODEL), lambda l, j: (l, 0, j)),
        out_shape=jax.ShapeDtypeStruct((depth, SUBLANES, 6 * D_MODEL), F32),
        compiler_params=_cparams(2),
        name="modulation",
    )(cvecs, mod_w, mod_b.reshape(depth, 1, 6 * D_MODEL))


class _Rows:
    def __init__(self, n_ctx, n_lat, lat_seq):
        self.n_ctx, self.n_lat, self.lat_seq = n_ctx, n_lat, lat_seq
        self.n = n_ctx + n_lat

    def mod_row(self, i, tm):
        nctx_blk = self.n_ctx // tm
        per_seq = self.lat_seq // tm
        return jnp.where(i < nctx_blk, 0, 1 + (i - nctx_blk) // per_seq)

    def mod_spec(self, layer, chunk, tm):
        return pl.BlockSpec((None, None, 1, D_MODEL),
                            lambda i, *_: (layer, self.mod_row(i, tm), 0, chunk))


def _rms_mod(x, g, sc, sh):
    ms = jnp.mean(x * x, axis=-1, keepdims=True)
    return x * lax.rsqrt(ms + EPS) * g * (1.0 + sc) + sh


def _inproj_kernel(splits, x_ref, g_ref, sh_ref, sc_ref, w_ref, *o_refs):
    h = _rms_mod(x_ref[...], g_ref[...], sc_ref[...], sh_ref[...])
    p = jnp.dot(_bf(h), w_ref[...], preferred_element_type=F32)
    lo = 0
    for o_ref, width in zip(o_refs, splits):
        o_ref[...] = p[:, lo:lo + width]
        lo += width


def _inproj(rows, x, g, mod4, layer, w_bf, splits):
    n_out = w_bf.shape[1]
    return pl.pallas_call(
        functools.partial(_inproj_kernel, splits),
        grid=(rows.n // TM,),
        in_specs=[pl.BlockSpec((TM, D_MODEL), lambda i: (i, 0)),
                  pl.BlockSpec((1, D_MODEL), lambda i: (0, 0)),
                  rows.mod_spec(layer, 0, TM),
                  rows.mod_spec(layer, 1, TM),
                  pl.BlockSpec((D_MODEL, n_out), lambda i: (0, 0))],
        out_specs=[pl.BlockSpec((TM, wd), lambda i: (i, 0)) for wd in splits],
        out_shape=[jax.ShapeDtypeStruct((rows.n, wd), F32) for wd in splits],
        compiler_params=_cparams(1),
        name=f"inproj{layer}",
    )(x, g.reshape(1, D_MODEL), mod4, mod4, w_bf)


def _head_rms(x, gain_t, ones2):
    ms = _seg_sum(x * x, ones2) * (1.0 / HD_A)
    return x * lax.rsqrt(ms + EPS) * gain_t


def _sink_softmax_pv(parts, sink):
    m = jnp.maximum(functools.reduce(jnp.maximum, [jnp.max(s, axis=-1, keepdims=True) for s, _ in parts]), sink)
    den = jnp.exp(sink - m)
    acc = None
    for s, v in parts:
        p = jnp.exp(s - m)
        den = den + jnp.sum(p, axis=-1, keepdims=True)
        pv = jnp.dot(_bf(p), v, preferred_element_type=F32)
        acc = pv if acc is None else acc + pv
    return acc / den


def _ctx_attn_kernel(p_ref, qg_ref, kg_ref, sink_ref, ones_q_ref, ones_k_ref, o_ref, k_ref, v_ref):
    p = p_ref[...]
    q = _head_rms(p[:, :W_A], qg_ref[...], ones_q_ref[...]) * ATTN_SCALE
    k = _head_rms(p[:, W_A:W_A + KV_A * HD_A], kg_ref[...], ones_k_ref[...])
    v = p[:, W_A + KV_A * HD_A:ATT_IN]
    k_ref[0] = k
    v_ref[0] = v
    qb, kb, vb = _bf(q), _bf(k), _bf(v)
    outs = []
    for h in range(H_A):
        j = h // G_A
        s = _dot_nt(qb[:, h * HD_A:(h + 1) * HD_A], kb[:, j * HD_A:(j + 1) * HD_A])
        outs.append(_sink_softmax_pv([(s, vb[:, j * HD_A:(j + 1) * HD_A])], sink_ref[h]))
    o_ref[...] = jnp.concatenate(outs, axis=1)


def _ctx_attention(p_att, n_seq, seq, qg_t, kg_t, sink, ones_q, ones_k):
    kv_w = KV_A * HD_A
    return pl.pallas_call(
        _ctx_attn_kernel,
        grid=(n_seq,),
        in_specs=[pl.BlockSpec((seq, ATT_IN), lambda b: (b, 0)),
                  pl.BlockSpec((1, W_A), lambda b: (0, 0)),
                  pl.BlockSpec((1, kv_w), lambda b: (0, 0)),
                  pl.BlockSpec(memory_space=pltpu.SMEM),
                  pl.BlockSpec(ones_q.shape, lambda b: (0, 0)),
                  pl.BlockSpec(ones_k.shape, lambda b: (0, 0))],
        out_specs=[pl.BlockSpec((seq, W_A), lambda b: (b, 0)),
                   pl.BlockSpec((1, seq, kv_w), lambda b: (b, 0, 0)),
                   pl.BlockSpec((1, seq, kv_w), lambda b: (b, 0, 0))],
        out_shape=[jax.ShapeDtypeStruct((n_seq * seq, W_A), F32),
                   jax.ShapeDtypeStruct((n_seq, seq, kv_w), F32),
                   jax.ShapeDtypeStruct((n_seq, seq, kv_w), F32)],
        compiler_params=_cparams(1),
        name="ctx_attention",
    )(p_att, qg_t, kg_t, sink, ones_q, ones_k)


def _rope(x, cos_t, sin_t):
    lane = lax.broadcasted_iota(jnp.int32, cos_t.shape, 1)
    low = (lane % 32) < 16
    outs = []
    for s in range(x.shape[1] // LANES):
        xs = x[:, s * LANES:(s + 1) * LANES]
        partner = jnp.where(low, pltpu.roll(xs, LANES - 16, 1), pltpu.roll(xs, 16, 1))
        outs.append(xs * cos_t + partner * sin_t)
    return outs[0] if len(outs) == 1 else jnp.concatenate(outs, axis=1)


def _lat_attn_kernel(seq, p_ref, qg_ref, kg_ref, sink_ref, ones_q_ref, ones_k_ref, cos_ref, sin_ref,
                     kc_ref, vc_ref, o_ref, q_scr, k_scr, v_scr):
    kv_w = KV_A * HD_A
    p = p_ref[...]
    q = _head_rms(p[:, :W_A], qg_ref[...], ones_q_ref[...])
    k = _head_rms(p[:, W_A:W_A + kv_w], kg_ref[...], ones_k_ref[...])
    q_scr[...] = _bf(_rope(q, cos_ref[...], sin_ref[...]) * ATTN_SCALE)
    k_scr[...] = _bf(_rope(k, cos_ref[...], sin_ref[...]))
    v_scr[...] = _bf(p[:, W_A + kv_w:ATT_IN])
    kc = _bf(kc_ref[0])
    vc = _bf(vc_ref[0])
    n_local = 3 * QBLK

    def block(i, carry):
        q0 = pl.multiple_of(i * QBLK, QBLK)
        start = pl.multiple_of(jnp.clip((i - 1) * QBLK, 0, seq - n_local), QBLK)
        qb = q_scr[pl.ds(q0, QBLK), :]
        kl = k_scr[pl.ds(start, n_local), :]
        vl = v_scr[pl.ds(start, n_local), :]
        ipos = q0 + lax.broadcasted_iota(jnp.int32, (QBLK, n_local), 0)
        jpos = start + lax.broadcasted_iota(jnp.int32, (QBLK, n_local), 1)
        band = jnp.abs(jpos - ipos) <= WINDOW
        outs = []
        for h in range(H_A):
            j = h // G_A
            qh = qb[:, h * HD_A:(h + 1) * HD_A]
            s_loc = jnp.where(band, _dot_nt(qh, kl[:, j * HD_A:(j + 1) * HD_A]), NEG_INF)
            s_ctx = _dot_nt(qh, kc[:, j * HD_A:(j + 1) * HD_A])
            outs.append(_sink_softmax_pv([(s_loc, vl[:, j * HD_A:(j + 1) * HD_A]),
                                          (s_ctx, vc[:, j * HD_A:(j + 1) * HD_A])], sink_ref[h]))
        o_ref[pl.ds(q0, QBLK), :] = jnp.concatenate(outs, axis=1)
        return carry

    lax.fori_loop(0, seq // QBLK, block, 0)


def _lat_attention(p_att, row_blk0, n_seq, seq, qg_t, kg_t, sink, ones_q, ones_k, cos_t, sin_t, kc, vc):
    kv_w = KV_A * HD_A
    past = kc.shape[1]
    return pl.pallas_call(
        functools.partial(_lat_attn_kernel, seq),
        grid=(n_seq,),
        in_specs=[pl.BlockSpec((seq, ATT_IN), lambda b: (row_blk0 + b, 0)),
                  pl.BlockSpec((1, W_A), lambda b: (0, 0)),
                  pl.BlockSpec((1, kv_w), lambda b: (0, 0)),
                  pl.BlockSpec(memory_space=pltpu.SMEM),
                  pl.BlockSpec(ones_q.shape, lambda b: (0, 0)),
                  pl.BlockSpec(ones_k.shape, lambda b: (0, 0)),
                  pl.BlockSpec((seq, LANES), lambda b: (0, 0)),
                  pl.BlockSpec((seq, LANES), lambda b: (0, 0)),
                  pl.BlockSpec((1, past, kv_w), lambda b: (b, 0, 0)),
                  pl.BlockSpec((1, past, kv_w), lambda b: (b, 0, 0))],
        out_specs=pl.BlockSpec((seq, W_A), lambda b: (b, 0)),
        out_shape=jax.ShapeDtypeStruct((n_seq * seq, W_A), F32),
        scratch_shapes=[pltpu.VMEM((seq, W_A), BF16), pltpu.VMEM((seq, kv_w), BF16),
                        pltpu.VMEM((seq, kv_w), BF16)],
        compiler_params=_cparams(1),
        name="lat_attention",
    )(p_att, qg_t, kg_t, sink, ones_q, ones_k, cos_t, sin_t, kc, vc)


def _rope_tables(seq):
    pos = np.arange(seq)
    row = (pos // GRID_W).astype(np.float32)
    col = (pos % GRID_W).astype(np.float32)
    d_axis = HD_A // 2
    inv = (ROPE_BASE ** (-np.arange(0, d_axis, 2, dtype=np.float32) / d_axis)).astype(np.float32)
    cos_h = np.zeros((seq, HD_A), np.float32)
    sin_h = np.zeros((seq, HD_A), np.float32)
    for seg, p_ in enumerate((row, col)):
        ang = (p_[:, None] * inv[None, :]).astype(np.float32)
        c, s = np.cos(ang), np.sin(ang)
        base = seg * d_axis
        cos_h[:, base:base + d_axis // 2] = c
        cos_h[:, base + d_axis // 2:base + d_axis] = c
        sin_h[:, base:base + d_axis // 2] = -s
        sin_h[:, base + d_axis // 2:base + d_axis] = s
    rep = LANES // HD_A
    return jnp.asarray(np.tile(cos_h, (1, rep))), jnp.asarray(np.tile(sin_h, (1, rep)))


def _rwkv_prep_kernel(rows, x_ref, prev_ref, next_ref, mu_ref, kk_ref, ka_ref, rk_ref, w0_ref, w2_ref,
                      a0_ref, a2_ref, g2_ref, ones_ref,
                      nkk_ref, r_ref, v_ref, g_ref, bonus_ref,
                      wf_ref, kaf_ref, kdf_ref, wb_ref, kab_ref, kdb_ref):
    i = pl.program_id(0)
    nctx_blk = rows.n_ctx // TM
    per_seq = rows.lat_seq // TM
    is_ctx = i < nctx_blk
    first = jnp.logical_or(is_ctx, (i - nctx_blk) % per_seq == 0)
    last = jnp.logical_or(is_ctx, (i - nctx_blk) % per_seq == per_seq - 1)
    x = x_ref[...]
    ridx = lax.broadcasted_iota(jnp.int32, x.shape, 0)
    prev_row = jnp.where(first, 0.0, prev_ref[SUBLANES - 1:SUBLANES, :])
    next_row = jnp.where(last, 0.0, next_ref[0:1, :])
    xm1 = jnp.where(ridx == 0, prev_row, pltpu.roll(x, 1, 0))
    xp1 = jnp.where(ridx == TM - 1, next_row, pltpu.roll(x, TM - 1, 0))
    pw = x + (0.5 * (xm1 + xp1) - x) * mu_ref[...]

    r = pw[:, 0:W_B]
    k = pw[:, W_B:2 * W_B]
    v = pw[:, 2 * W_B:3 * W_B]
    wd = pw[:, 3 * W_B:3 * W_B + LORA_W]
    ad = pw[:, 3 * W_B + LORA_W:3 * W_B + LORA_W + LORA_A]
    gd = pw[:, 3 * W_B + LORA_W + LORA_A:]
    ones2 = ones_ref[...]

    kk = k * kk_ref[...]
    kk = kk / jnp.maximum(jnp.sqrt(_seg_sum(kk * kk, ones2)), 1e-12)
    nkk_ref[...] = -kk
    r_ref[...] = r
    v_ref[...] = v
    g_ref[...] = jnp.dot(_bf(_sigmoid(gd)), g2_ref[...], preferred_element_type=F32)
    tw = _bf(jnp.tanh(wd))
    adb = _bf(ad)
    bonus = jnp.zeros_like(r)
    for d, (w_o, ka_o, kd_o) in enumerate(((wf_ref, kaf_ref, kdf_ref), (wb_ref, kab_ref, kdb_ref))):
        z = -(w0_ref[d:d + 1, :] + jnp.dot(tw, w2_ref[d], preferred_element_type=F32))
        softplus = jnp.maximum(z, 0.0) + jnp.log(1.0 + jnp.exp(-jnp.abs(z)))
        w_o[...] = jnp.exp(-jnp.exp(-softplus - 0.5))
        a = _sigmoid(a0_ref[d:d + 1, :] + jnp.dot(adb, a2_ref[d], preferred_element_type=F32))
        kd = k * (1.0 + (a - 1.0) * ka_ref[...])
        ka_o[...] = kk * a
        kd_o[...] = kd
        bonus = bonus + _seg_sum(r * kd * rk_ref[...], ones2) * v
    bonus_ref[...] = bonus


def _rwkv_prep(rows, p_rw, pr, ones_b):
    n = rows.n
    n_halo = n // SUBLANES
    blk_halo = TM // SUBLANES
    row = lambda a: a.reshape(1, -1)
    full = lambda a: pl.BlockSpec(a.shape, lambda i: (0,) * a.ndim)
    consts = [row(pr['mu']), row(pr['k_k']), row(pr['k_a']), row(pr['r_k']), pr['w0'], _bf(pr['w2']),
              pr['a0'], _bf(pr['a2']), _bf(pr['g2']), ones_b]
    outs = pl.pallas_call(
        functools.partial(_rwkv_prep_kernel, rows),
        grid=(n // TM,),
        in_specs=[pl.BlockSpec((TM, RWKV_IN), lambda i: (i, 0)),
                  pl.BlockSpec((SUBLANES, RWKV_IN), lambda i: (jnp.maximum(i * blk_halo - 1, 0), 0)),
                  pl.BlockSpec((SUBLANES, RWKV_IN), lambda i: (jnp.minimum((i + 1) * blk_halo, n_halo - 1), 0))]
                 + [full(a) for a in consts],
        out_specs=[pl.BlockSpec((TM, W_B), lambda i: (i, 0))] * 11,
        out_shape=[jax.ShapeDtypeStruct((n, W_B), F32)] * 11,
        compiler_params=_cparams(1),
        name="rwkv_prep",
    )(p_rw, p_rw, p_rw, *consts)
    names = ('nkk', 'r', 'v', 'g', 'bonus', 'w_f', 'ka_f', 'kd_f', 'w_b', 'ka_b', 'kd_b')
    return dict(zip(names, outs))


def _rwkv_scan_kernel(n_tb, nkkf_ref, rf_ref, vf_ref, wf_ref, kaf_ref, kdf_ref,
                      nkkb_ref, rb_ref, vb_ref, wb_ref, kab_ref, kdb_ref,
                      s0f_ref, s0b_ref, ones_ref,
                      of_ref, ob_ref, sff_ref, sfb_ref, s_scr, vt_scr):
    tb = pl.program_id(1)
    n_pair = H_B // 2
    half = RW_TB // 2
    dirs = ((nkkf_ref, rf_ref, vf_ref, wf_ref, kaf_ref, kdf_ref, of_ref, False),
            (nkkb_ref, rb_ref, vb_ref, wb_ref, kab_ref, kdb_ref, ob_ref, True))

    @pl.when(tb == 0)
    def _():
        s_scr[0] = s0f_ref[...]
        s_scr[1] = s0b_ref[...]

    lane = lax.broadcasted_iota(jnp.int32, (HD_B, LANES), 1)
    for d, refs in enumerate(dirs):
        v_ref = refs[2]
        for bb in range(RW_BB):
            for p in range(n_pair):
                vt = v_ref[bb, :, p * LANES:(p + 1) * LANES].T
                top, bot = vt[:HD_B], vt[HD_B:]
                for s in range(2):
                    if s == 0:
                        t2 = jnp.where(lane < HD_B, top, pltpu.roll(bot, HD_B, 1))
                    else:
                        t2 = jnp.where(lane < HD_B, pltpu.roll(top, HD_B, 1), bot)
                    vt_scr[d, bb, p, s] = t2

    ones2 = ones_ref[...]
    row8 = lax.broadcasted_iota(jnp.int32, (SUBLANES, LANES), 0)
    lane8 = lax.broadcasted_iota(jnp.int32, (SUBLANES, LANES), 1)
    sel_r = jnp.logical_or(jnp.logical_and(row8 % 2 == 0, lane8 < HD_B),
                           jnp.logical_and(row8 % 2 == 1, lane8 >= HD_B))

    def row_of(rev, tt):
        return RW_TB - 1 - tt if rev else tt

    def emit_output(d, bb, tau):
        r_ref, o_ref = dirs[d][1], dirs[d][6]
        r = r_ref[bb, pl.ds(tau, 1), :]
        r8 = jnp.zeros((SUBLANES, LANES), F32)
        for p in range(n_pair):
            rp = jnp.broadcast_to(r[:, p * LANES:(p + 1) * LANES], (SUBLANES, LANES))
            r8 = jnp.where(jnp.logical_and(sel_r, row8 // 2 == p), rp, r8)
        s_all = jnp.concatenate([_bf(s_scr[d, bb, p]) for p in range(n_pair)], axis=0)
        o8 = _dot_nt(_bf(r8), s_all)
        o_parts = []
        for p in range(n_pair):
            for h in range(2):
                o_parts.append(o8[2 * p + h:2 * p + h + 1, p * HD_B:(p + 1) * HD_B])
        o_ref[bb, pl.ds(tau, 1), :] = jnp.concatenate(o_parts, axis=1)

    groups = [(d, bbs) for d in range(2) for bbs in _chunks(range(RW_BB), RW_GROUP_BB)]

    def step(tt, carry):
        tt_prev = jnp.maximum(tt - 1, 0)
        reds = []
        for d, bbs in groups:
            rev = dirs[d][7]
            tau = row_of(rev, tt)
            sub = tau // half
            lt = tau % half
            mask = jnp.logical_or(lane == lt, lane == lt + HD_B)
            lhs = []
            for bb in bbs:
                emit_output(d, bb, row_of(rev, tt_prev))
                nkk = dirs[d][0][bb, pl.ds(tau, 1), :]
                for p in range(n_pair):
                    prod = s_scr[d, bb, p] * nkk[:, p * LANES:(p + 1) * LANES]
                    lhs.append(jnp.concatenate([_bf(prod), _bf(jnp.where(mask, vt_scr[d, bb, p, sub], 0.0))],
                                               axis=1))
            reds.append(jnp.dot(jnp.concatenate(lhs, axis=0), ones2, preferred_element_type=F32))
        for (d, bbs), red in zip(groups, reds):
            _, _, _, w_ref, ka_ref, kd_ref, _, rev = dirs[d]
            tau = row_of(rev, tt)
            for k, bb in enumerate(bbs):
                w = w_ref[bb, pl.ds(tau, 1), :]
                ka = ka_ref[bb, pl.ds(tau, 1), :]
                kd = kd_ref[bb, pl.ds(tau, 1), :]
                for p in range(n_pair):
                    sl = slice(p * LANES, (p + 1) * LANES)
                    r0 = (k * n_pair + p) * HD_B
                    sa = red[r0:r0 + HD_B, :LANES]
                    vcol = red[r0:r0 + HD_B, LANES:]
                    s_scr[d, bb, p] = s_scr[d, bb, p] * w[:, sl] + sa * ka[:, sl] + vcol * kd[:, sl]
        return carry

    lax.fori_loop(0, RW_TB, step, 0)
    for d in range(2):
        for bb in range(RW_BB):
            emit_output(d, bb, row_of(dirs[d][7], RW_TB - 1))

    @pl.when(tb == n_tb - 1)
    def _():
        sff_ref[...] = s_scr[0]
        sfb_ref[...] = s_scr[1]


def _rwkv_scan(pp, row0, n_seq, seq, s0_f, s0_b, ones_pair):
    n_tb = seq // RW_TB
    n_pair = H_B // 2
    blk0 = row0 // seq
    view = lambda a: a.reshape(a.shape[0] // seq, seq, W_B)
    fwd = pl.BlockSpec((RW_BB, RW_TB, W_B), lambda b, t: (blk0 // RW_BB + b, t, 0))
    bwd = pl.BlockSpec((RW_BB, RW_TB, W_B), lambda b, t: (blk0 // RW_BB + b, n_tb - 1 - t, 0))
    st = pl.BlockSpec((RW_BB, n_pair, HD_B, LANES), lambda b, t: (b, 0, 0, 0))
    o_f = pl.BlockSpec((RW_BB, RW_TB, W_B), lambda b, t: (b, t, 0))
    o_b = pl.BlockSpec((RW_BB, RW_TB, W_B), lambda b, t: (b, n_tb - 1 - t, 0))
    ins_f = [view(pp[k]) for k in ('nkk', 'r', 'v', 'w_f', 'ka_f', 'kd_f')]
    ins_b = [view(pp[k]) for k in ('nkk', 'r', 'v', 'w_b', 'ka_b', 'kd_b')]
    st_shape = jax.ShapeDtypeStruct((n_seq, n_pair, HD_B, LANES), F32)
    o_shape = jax.ShapeDtypeStruct((n_seq, seq, W_B), F32)
    return pl.pallas_call(
        functools.partial(_rwkv_scan_kernel, n_tb),
        grid=(n_seq // RW_BB, n_tb),
        in_specs=[fwd] * 6 + [bwd] * 6 + [st, st, pl.BlockSpec(ones_pair.shape, lambda b, t: (0, 0))],
        out_specs=[o_f, o_b, st, st],
        out_shape=[o_shape, o_shape, st_shape, st_shape],
        scratch_shapes=[pltpu.VMEM((2, RW_BB, n_pair, HD_B, LANES), F32),
                        pltpu.VMEM((2, RW_BB, n_pair, 2, HD_B, LANES), F32)],
        compiler_params=_cparams(2),
        name="rwkv_scan",
    )(*ins_f, *ins_b, s0_f, s0_b, ones_pair)


def _state_to_pairs(s):
    b = s.shape[0]
    return s.reshape(b, H_B // 2, 2, HD_B, HD_B).transpose(0, 1, 3, 2, 4).reshape(b, H_B // 2, HD_B, 2 * HD_B)


def _pairs_to_state(s):
    b = s.shape[0]
    return s.reshape(b, H_B // 2, HD_B, 2, HD_B).transpose(0, 1, 3, 2, 4).reshape(b, H_B, HD_B, HD_B)


def _tail(x, y, g1, n2g, sc2, sh2, x1_ref, h_ref, hp_ref):
    x1 = x + g1 * y
    x1_ref[...] = x1
    h = _rms_mod(x1, n2g, sc2, sh2)
    h_ref[...] = _bf(h)
    hp_ref[...] = _pack_pairs(h)


def _outproj0_kernel(x_ref, oa_ref, of_ref, ob_ref, bonus_ref, g_ref, lnw_ref, lnb_ref, ones_ref,
                     w_ref, g1_ref, n2g_ref, sc2_ref, sh2_ref, x1_ref, h_ref, hp_ref):
    o_sum = of_ref[...] + ob_ref[...]
    ones2 = ones_ref[...]
    mean = _seg_sum(o_sum, ones2) * (1.0 / HD_B)
    cen = o_sum - mean
    var = _seg_sum(cen * cen, ones2) * (1.0 / HD_B)
    gn = cen * lax.rsqrt(var + GN_EPS) * lnw_ref[...] + lnb_ref[...]
    o_rw = (gn + bonus_ref[...]) * g_ref[...]
    mix = jnp.concatenate([_bf(oa_ref[...]), _bf(o_rw)], axis=1)
    y = jnp.dot(mix, w_ref[...], preferred_element_type=F32)
    _tail(x_ref[...], y, g1_ref[...], n2g_ref[...], sc2_ref[...], sh2_ref[...], x1_ref, h_ref, hp_ref)


def _outproj0(rows, x, o_att, o_f, o_b, pp, pr, ones_b, w_out_bf, n2g, mod4, layer):
    n = rows.n
    tok = lambda w: pl.BlockSpec((TM, w), lambda i: (i, 0))
    const = lambda a: pl.BlockSpec(a.shape, lambda i: (0,) * a.ndim)
    lnw, lnb, n2 = pr['ln_w'].reshape(1, -1), pr['ln_b'].reshape(1, -1), n2g.reshape(1, -1)
    return pl.pallas_call(
        _outproj0_kernel,
        grid=(n // TM,),
        in_specs=[tok(D_MODEL), tok(W_A), tok(W_B), tok(W_B), tok(W_B), tok(W_B),
                  const(lnw), const(lnb), const(ones_b), const(w_out_bf),
                  rows.mod_spec(layer, 2, TM), const(n2), rows.mod_spec(layer, 4, TM), rows.mod_spec(layer, 3, TM)],
        out_specs=[tok(D_MODEL), tok(D_MODEL), tok(D_MODEL // 2)],
        out_shape=[jax.ShapeDtypeStruct((n, D_MODEL), F32), jax.ShapeDtypeStruct((n, D_MODEL), BF16),
                   jax.ShapeDtypeStruct((n, D_MODEL // 2), jnp.uint32)],
        compiler_params=_cparams(1),
        name="outproj0",
    )(x, o_att, o_f, o_b, pp['bonus'], pp['g'], lnw, lnb, ones_b, w_out_bf, mod4, n2, mod4, mod4)


def _outproj1_kernel(x_ref, of_ref, ob_ref, gate_ref, ng_ref, w_ref, g1_ref, n2g_ref, sc2_ref, sh2_ref,
                     x1_ref, h_ref, hp_ref):
    o_sum = of_ref[...] + ob_ref[...]
    parts = []
    for h in range(H_C):
        oh = o_sum[:, h * DV_C:(h + 1) * DV_C]
        parts.append(oh * lax.rsqrt(jnp.mean(oh * oh, axis=-1, keepdims=True) + EPS))
    o = jnp.concatenate(parts, axis=1) * ng_ref[...] * _silu(gate_ref[...])
    y = jnp.dot(_bf(o), w_ref[...], preferred_element_type=F32)
    _tail(x_ref[...], y, g1_ref[...], n2g_ref[...], sc2_ref[...], sh2_ref[...], x1_ref, h_ref, hp_ref)


def _outproj1(rows, x, o_f, o_b, p1, norm_g, w_out_bf, n2g, mod4, layer):
    n = rows.n
    tok = lambda w: pl.BlockSpec((TM, w), lambda i: (i, 0))
    const = lambda a: pl.BlockSpec(a.shape, lambda i: (0,) * a.ndim)
    ng, n2 = norm_g.reshape(1, -1), n2g.reshape(1, -1)
    return pl.pallas_call(
        _outproj1_kernel,
        grid=(n // TM,),
        in_specs=[tok(D_MODEL), tok(D_C), tok(D_C), pl.BlockSpec((TM, D_C), lambda i: (i, 4)),
                  const(ng), const(w_out_bf),
                  rows.mod_spec(layer, 2, TM), const(n2), rows.mod_spec(layer, 4, TM), rows.mod_spec(layer, 3, TM)],
        out_specs=[tok(D_MODEL), tok(D_MODEL), tok(D_MODEL // 2)],
        out_shape=[jax.ShapeDtypeStruct((n, D_MODEL), F32), jax.ShapeDtypeStruct((n, D_MODEL), BF16),
                   jax.ShapeDtypeStruct((n, D_MODEL // 2), jnp.uint32)],
        compiler_params=_cparams(1),
        name="outproj1",
    )(x, o_f, o_b, p1, ng, w_out_bf, mod4, n2, mod4, mod4)


def _hgrn_kernel(n_tb, qf_ref, ff_ref, if_ref, qb_ref, fb_ref, ib_ref, lbp_ref, s0f_ref, s0b_ref,
                 trif_ref, trib_ref, of_ref, ob_ref, sff_ref, sfb_ref, s_scr):
    tb = pl.program_id(1)

    @pl.when(tb == 0)
    def _():
        for h in range(H_C):
            s_scr[0, h] = s0f_ref[0, h].T
            s_scr[1, h] = s0b_ref[0, h].T

    lbp = lbp_ref[...]
    e = jnp.exp(lbp - jnp.max(lbp, axis=0, keepdims=True))
    sm = e / jnp.sum(e, axis=0, keepdims=True)
    lb = (sm[0:1] + sm[1:2]) - sm[0:1]

    n_chunk = HG_TB // CHUNK
    ti = lax.broadcasted_iota(jnp.int32, (HG_TB, HG_TB), 0)
    si = lax.broadcasted_iota(jnp.int32, (HG_TB, HG_TB), 1)
    same = (ti // CHUNK) == (si // CHUNK)
    dirs = ((qf_ref, ff_ref, if_ref, of_ref, trif_ref, jnp.logical_and(same, ti >= si), CHUNK - 1, False),
            (qb_ref, fb_ref, ib_ref, ob_ref, trib_ref, jnp.logical_and(same, ti <= si), 0, True))

    staged = []
    for d, (q_ref, f_ref, i_ref, o_ref, tri_ref, causal, last_row, rev) in enumerate(dirs):
        q = _silu(q_ref[...])
        f = lb + (1.0 - lb) * _sigmoid(f_ref[...])
        k = 1.0 - f
        v = _bf(i_ref[...])
        g = jnp.log(f)
        g1 = _bf(g)
        g2 = _bf(g - g1.astype(F32))
        tri2 = tri_ref[...]
        b_parts, last_parts, dec = [], [], []
        for c in range(n_chunk):
            rc = slice(c * CHUNK, (c + 1) * CHUNK)
            bc = jnp.dot(tri2, jnp.concatenate([g1[rc], g2[rc]], axis=0), preferred_element_type=F32)
            b_parts.append(bc)
            last = bc[last_row:last_row + 1]
            last_parts.append(jnp.broadcast_to(last, bc.shape))
            dec.append(jnp.exp(last))
        b = jnp.concatenate(b_parts, axis=0)
        b_last = jnp.concatenate(last_parts, axis=0)
        staged.append((_bf(q * jnp.exp(b)), _bf(k * jnp.exp(-b)), _bf(k * jnp.exp(b_last - b)), v, dec))

    for h in range(H_C):
        sl = slice(h * DK_C, (h + 1) * DK_C)
        for d, (q_ref, f_ref, i_ref, o_ref, tri_ref, causal, last_row, rev) in enumerate(dirs):
            q_in, k_in, k_out, v, dec = staged[d]
            qh, vh = q_in[:, sl], v[:, sl]
            att = jnp.where(causal, _dot_nt(qh, k_in[:, sl]), 0.0)
            o_intra = jnp.dot(_bf(att), vh, preferred_element_type=F32)
            s_t = s_scr[d, h]
            for c in (range(n_chunk - 1, -1, -1) if rev else range(n_chunk)):
                rc = slice(c * CHUNK, (c + 1) * CHUNK)
                o_ref[rc, sl] = o_intra[rc] + _dot_nt(qh[rc], _bf(s_t))
                s_t = dec[c][:, sl] * s_t + _dot_tn(vh[rc], k_out[rc, sl])
            s_scr[d, h] = s_t

    @pl.when(tb == n_tb - 1)
    def _():
        for h in range(H_C):
            sff_ref[0, h] = s_scr[0, h].T
            sfb_ref[0, h] = s_scr[1, h].T


def _hgrn_scan(p1, row0, n_seq, seq, lb_params, s0_f, s0_b):
    n_tb = seq // HG_TB
    blk0 = row0 // HG_TB
    tri = np.tril(np.ones((CHUNK, CHUNK), np.float32))
    tri_f = jnp.asarray(np.concatenate([tri] * 2, axis=1), dtype=BF16)
    tri_b = jnp.asarray(np.concatenate([tri.T] * 2, axis=1), dtype=BF16)
    fwd = lambda col: pl.BlockSpec((HG_TB, D_C), lambda b, t: (blk0 + b * n_tb + t, col))
    bwd = lambda col: pl.BlockSpec((HG_TB, D_C), lambda b, t: (blk0 + b * n_tb + n_tb - 1 - t, col))
    st = pl.BlockSpec((1, H_C, DK_C, DV_C), lambda b, t: (b, 0, 0, 0))
    const = lambda a: pl.BlockSpec(a.shape, lambda b, t: (0,) * a.ndim)
    o_shape = jax.ShapeDtypeStruct((n_seq * seq, D_C), F32)
    st_shape = jax.ShapeDtypeStruct((n_seq, H_C, DK_C, DV_C), F32)
    return pl.pallas_call(
        functools.partial(_hgrn_kernel, n_tb),
        grid=(n_seq, n_tb),
        in_specs=[fwd(0), fwd(1), fwd(3), bwd(0), bwd(2), bwd(3), const(lb_params), st, st,
                  const(tri_f), const(tri_b)],
        out_specs=[pl.BlockSpec((HG_TB, D_C), lambda b, t: (b * n_tb + t, 0)),
                   pl.BlockSpec((HG_TB, D_C), lambda b, t: (b * n_tb + n_tb - 1 - t, 0)), st, st],
        out_shape=[o_shape, o_shape, st_shape, st_shape],
        scratch_shapes=[pltpu.VMEM((2, H_C, DV_C, DK_C), F32)],
        compiler_params=_cparams(2),
        name="hgrn_scan",
    )(p1, p1, p1, p1, p1, p1, lb_params, s0_f, s0_b, tri_f, tri_b)


def _router_kernel(h_ref, rhi_ref, rlo_ref, bias_ref, sel_ref, eidx_ref, ew_ref, cnt_ref):
    x = h_ref[...]
    tm = x.shape[0]
    logits = _dot_nt(rhi_ref[...], x) + _dot_nt(rlo_ref[...], x)
    scores = _sigmoid(logits)
    biased = scores + bias_ref[...]
    per = N_EXPERTS // N_GROUPS
    sub = lax.broadcasted_iota(jnp.int32, (per, tm), 0)
    gs_rows = []
    for g in range(N_GROUPS):
        blk = biased[g * per:(g + 1) * per]
        m1 = jnp.max(blk, axis=0, keepdims=True)
        first = jnp.min(jnp.where(blk == m1, sub, per), axis=0, keepdims=True)
        m2 = jnp.max(jnp.where(sub == first, -jnp.inf, blk), axis=0, keepdims=True)
        gs_rows.append(m1 + m2)
    gs = jnp.concatenate(gs_rows, axis=0)
    gi = lax.broadcasted_iota(jnp.int32, gs.shape, 0)
    rank = jnp.zeros(gs.shape, jnp.int32)
    for s in range(1, N_GROUPS):
        other = pltpu.roll(gs, s, 0)
        oi = pltpu.roll(gi, s, 0)
        beats = jnp.logical_or(other > gs, jnp.logical_and(other == gs, oi < gi))
        rank = rank + jnp.where(beats, 1, 0)
    keep = jnp.where(rank < TOPK_GROUPS, 1.0, 0.0)
    emask = jnp.concatenate([jnp.broadcast_to(keep[g:g + 1], (per, tm)) for g in range(N_GROUPS)], axis=0)
    cur = jnp.where(emask > 0.0, biased, -jnp.inf)
    ei = lax.broadcasted_iota(jnp.int32, cur.shape, 0)
    sel = jnp.zeros(cur.shape, F32)
    idxs, vals = [], []
    for _ in range(TOP_K):
        m = jnp.max(cur, axis=0, keepdims=True)
        idx = jnp.min(jnp.where(cur == m, ei, N_EXPERTS), axis=0, keepdims=True)
        pick = ei == idx
        idxs.append(idx)
        vals.append(jnp.sum(jnp.where(pick, scores, 0.0), axis=0, keepdims=True))
        sel = jnp.where(pick, 1.0, sel)
        cur = jnp.where(pick, -jnp.inf, cur)
    w = jnp.concatenate(vals, axis=0)
    eidx_ref[...] = jnp.concatenate(idxs, axis=0)
    ew_ref[...] = w / jnp.sum(w, axis=0, keepdims=True) * ROUTED_SCALE
    sel_ref[...] = _bf(sel)

    @pl.when(pl.program_id(0) == 0)
    def _():
        cnt_ref[...] = jnp.zeros_like(cnt_ref)

    cnt_ref[...] += jnp.sum(sel, axis=1, keepdims=True)


def _router(hffn, router, bias):
    n = hffn.shape[0]
    r_t = router.T
    r_hi = _bf(r_t)
    r_lo = _bf(r_t - r_hi.astype(F32))
    const = lambda a: pl.BlockSpec(a.shape, lambda i: (0,) * a.ndim)
    b_col = bias.reshape(N_EXPERTS, 1)
    return pl.pallas_call(
        _router_kernel,
        grid=(n // TM,),
        in_specs=[pl.BlockSpec((TM, D_MODEL), lambda i: (i, 0)), const(r_hi), const(r_lo), const(b_col)],
        out_specs=[pl.BlockSpec((N_EXPERTS, TM), lambda i: (0, i)),
                   pl.BlockSpec((TOP_K, TM), lambda i: (0, i)),
                   pl.BlockSpec((TOP_K, TM), lambda i: (0, i)),
                   pl.BlockSpec((N_EXPERTS, LANES), lambda i: (0, 0))],
        out_shape=[jax.ShapeDtypeStruct((N_EXPERTS, n), BF16),
                   jax.ShapeDtypeStruct((TOP_K, n), jnp.int32),
                   jax.ShapeDtypeStruct((TOP_K, n), F32),
                   jax.ShapeDtypeStruct((N_EXPERTS, LANES), F32)],
        compiler_params=_cparams(1),
        name="router",
    )(hffn, r_hi, r_lo, b_col)


def _positions_kernel(sel_ref, eidx_ref, base_ref, upper_ref, pos_ref, carry_ref):
    @pl.when(pl.program_id(0) == 0)
    def _():
        carry_ref[...] = jnp.zeros_like(carry_ref)

    sel = sel_ref[...]
    rank = jnp.dot(sel, upper_ref[...], preferred_element_type=F32)
    pos_e = base_ref[:, 0:1] + carry_ref[:, 0:1] + rank
    ei = lax.broadcasted_iota(jnp.int32, pos_e.shape, 0)
    eidx = eidx_ref[...]
    rows = [jnp.sum(jnp.where(ei == eidx[k:k + 1], pos_e, 0.0), axis=0, keepdims=True) for k in range(TOP_K)]
    pos_ref[...] = jnp.concatenate(rows, axis=0).astype(jnp.int32)
    carry_ref[...] += jnp.sum(sel.astype(F32), axis=1, keepdims=True)


def _positions(sel, eidx, base):
    n = sel.shape[1]
    pb = POS_TB
    upper = jnp.asarray(np.triu(np.ones((pb, pb), np.float32), 1), dtype=BF16)
    return pl.pallas_call(
        _positions_kernel,
        grid=(n // pb,),
        in_specs=[pl.BlockSpec((N_EXPERTS, pb), lambda i: (0, i)),
                  pl.BlockSpec((TOP_K, pb), lambda i: (0, i)),
                  pl.BlockSpec((N_EXPERTS, LANES), lambda i: (0, 0)),
                  pl.BlockSpec((pb, pb), lambda i: (0, 0))],
        out_specs=pl.BlockSpec((TOP_K, pb), lambda i: (0, i)),
        out_shape=jax.ShapeDtypeStruct((TOP_K, n), jnp.int32),
        scratch_shapes=[pltpu.VMEM((N_EXPERTS, LANES), F32)],
        compiler_params=_cparams(1),
        name="positions",
    )(sel, eidx, base, upper)


def _pack_pairs(x):
    half = x.shape[1] // 2
    bits = lax.bitcast_convert_type(_bf(x).astype(F32), jnp.uint32)
    return (bits[:, :half] >> 16) | (bits[:, half:] & jnp.uint32(0xFFFF0000))


def _unpack_pairs(w):
    lo = lax.bitcast_convert_type(w << 16, F32)
    hi = lax.bitcast_convert_type(w & jnp.uint32(0xFFFF0000), F32)
    return jnp.concatenate([_bf(lo), _bf(hi)], axis=1)


def _sc_gather(table, idx):
    b, w = idx.shape[0], table.shape[1]
    n_workers = SC_CORES * SC_SUBCORES
    per_w = b // n_workers
    assert b % (n_workers * SC_CHUNK) == 0
    mesh = plsc.VectorSubcoreMesh(core_axis_name="c", subcore_axis_name="s")

    @functools.partial(
        pl.kernel, mesh=mesh, out_type=jax.ShapeDtypeStruct((b, w), table.dtype),
        scratch_types=[pltpu.VMEM((SC_CHUNK,), jnp.int32), pltpu.VMEM((SC_CHUNK, w), table.dtype),
                       pltpu.SemaphoreType.DMA])
    def gather(table_hbm, idx_hbm, out_hbm, idx_v, rows_v, sem):
        wid = lax.axis_index("s") * SC_CORES + lax.axis_index("c")
        base = wid * per_w

        @pl.loop(0, per_w // SC_CHUNK)
        def _(c):
            off = pl.multiple_of(base + c * SC_CHUNK, SC_CHUNK)
            pltpu.sync_copy(idx_hbm.at[pl.ds(off, SC_CHUNK)], idx_v)
            pltpu.async_copy(table_hbm.at[idx_v], rows_v, sem).wait()
            pltpu.sync_copy(rows_v, out_hbm.at[pl.ds(off, SC_CHUNK)])

    return gather(table, idx)


def _sc_scatter(src, pos3, n_rows):
    n, w = src.shape
    n_workers = SC_CORES * SC_SUBCORES
    per_w = n // n_workers
    assert n % (n_workers * SC_CHUNK) == 0
    mesh = plsc.VectorSubcoreMesh(core_axis_name="c", subcore_axis_name="s")

    @functools.partial(
        pl.kernel, mesh=mesh, out_type=jax.ShapeDtypeStruct((n_rows, w), src.dtype),
        scratch_types=[pltpu.VMEM((TOP_K, SC_CHUNK), jnp.int32), pltpu.VMEM((SC_CHUNK, w), src.dtype),
                       pltpu.SemaphoreType.DMA])
    def scatter(src_hbm, pos_hbm, out_hbm, idx_v, rows_v, sem):
        wid = lax.axis_index("s") * SC_CORES + lax.axis_index("c")
        base = wid * per_w

        @pl.loop(0, per_w // SC_CHUNK)
        def _(c):
            off = pl.multiple_of(base + c * SC_CHUNK, SC_CHUNK)
            pltpu.sync_copy(src_hbm.at[pl.ds(off, SC_CHUNK)], rows_v)
            pltpu.sync_copy(pos_hbm.at[off // SC_CHUNK], idx_v)
            copies = [pltpu.async_copy(rows_v, out_hbm.at[idx_v.at[k]], sem) for k in range(TOP_K)]
            for cp in copies:
                cp.wait()

    return scatter(src, pos3)


def _experts_kernel(te_ref, nu_ref, xs_ref, wg_ref, wu_ref, wd_ref, ys_ref):
    @pl.when(pl.program_id(0) < nu_ref[0])
    def _():
        x = _unpack_pairs(xs_ref[...])
        act = _glu(x, wg_ref[0], wu_ref[0])
        ys_ref[...] = _pack_pairs(jnp.dot(_bf(act), wd_ref[0], preferred_element_type=F32))


def _experts(xs, tile_expert, n_used, mp):
    n_tiles = xs.shape[0] // MOE_TILE
    half = D_MODEL // 2
    wspec = lambda shape: pl.BlockSpec((1,) + shape, lambda i, te, nu: (te[i], 0, 0))
    return pl.pallas_call(
        _experts_kernel,
        grid_spec=pltpu.PrefetchScalarGridSpec(
            num_scalar_prefetch=2, grid=(n_tiles,),
            in_specs=[pl.BlockSpec((MOE_TILE, half), lambda i, te, nu: (i, 0)),
                      wspec((D_MODEL, D_EXPERT)), wspec((D_MODEL, D_EXPERT)), wspec((D_EXPERT, D_MODEL))],
            out_specs=pl.BlockSpec((MOE_TILE, half), lambda i, te, nu: (i, 0))),
        out_shape=jax.ShapeDtypeStruct(xs.shape, jnp.uint32),
        compiler_params=_cparams(1),
        name="experts",
    )(tile_expert, n_used, xs, mp['wg'], mp['wu'], mp['wd'])


def _combine_kernel(h_ref, *refs):
    yg_refs = refs[:TOP_K]
    ew_ref, eye_ref, sg_ref, su_ref, sd_ref, x1_ref, g2_ref, o_ref = refs[TOP_K:]
    act = _glu(h_ref[...], sg_ref[...], su_ref[...])
    acc = jnp.dot(_bf(act), sd_ref[...], preferred_element_type=F32)
    ew = ew_ref[...]
    hi = _bf(ew)
    lo = _bf(ew - hi.astype(F32))
    ew_t = _dot_tn(hi, eye_ref[...]) + _dot_tn(lo, eye_ref[...])
    for k in range(TOP_K):
        acc = acc + ew_t[:, k:k + 1] * _unpack_pairs(yg_refs[k][...]).astype(F32)
    o_ref[...] = x1_ref[...] + g2_ref[...] * acc


def _combine(rows, hffn, yg, ew, mp, x1, mod4, layer):
    n = rows.n
    half = D_MODEL // 2
    n_blk = n // TM
    const = lambda a: pl.BlockSpec(a.shape, lambda i: (0,) * a.ndim)
    tok = lambda w: pl.BlockSpec((TM, w), lambda i: (i, 0))
    slot = lambda k: pl.BlockSpec((TM, half), lambda i: (k * n_blk + i, 0))
    eye = jnp.eye(TOP_K, dtype=BF16)
    return pl.pallas_call(
        _combine_kernel,
        grid=(n_blk,),
        in_specs=[tok(D_MODEL)] + [slot(k) for k in range(TOP_K)]
                 + [pl.BlockSpec((TOP_K, TM), lambda i: (0, i)), const(eye),
                    const(mp['sg']), const(mp['su']), const(mp['sd']), tok(D_MODEL), rows.mod_spec(layer, 5, TM)],
        out_specs=tok(D_MODEL),
        out_shape=jax.ShapeDtypeStruct((n, D_MODEL), F32),
        compiler_params=_cparams(1),
        name=f"combine{layer}",
    )(hffn, *([yg] * TOP_K), ew, eye, mp['sg'], mp['su'], mp['sd'], x1, mod4)


def _moe(rows, hffn, hpack, x1, router, bias, mp, mod4, layer):
    n = rows.n
    sel, eidx, ew, cnt = _router(hffn, router, bias)
    counts = cnt[:, 0].astype(jnp.int32)
    padded = (counts + MOE_TILE - 1) // MOE_TILE * MOE_TILE
    ends = jnp.cumsum(padded)
    n_rows = n * TOP_K + N_EXPERTS * MOE_TILE
    n_tiles = n_rows // MOE_TILE
    base = jnp.broadcast_to((ends - padded).astype(F32)[:, None], (N_EXPERTS, LANES))
    tile_start = jnp.arange(n_tiles, dtype=jnp.int32) * MOE_TILE
    tile_expert = jnp.minimum(jnp.sum((ends[None, :] <= tile_start[:, None]).astype(jnp.int32), axis=1),
                              N_EXPERTS - 1)
    n_used = (ends[-1:] // MOE_TILE).astype(jnp.int32)
    pos = _positions(sel, eidx, base)
    pos3 = pos.reshape(TOP_K, n // SC_CHUNK, SC_CHUNK).transpose(1, 0, 2)
    xs = _sc_scatter(hpack, pos3, n_rows)
    ys = _experts(xs, tile_expert, n_used, mp)
    yg = _sc_gather(ys, pos.reshape(-1))
    return _combine(rows, hffn, yg, ew, mp, x1, mod4, layer)


def _glu(x, wg, wu):
    hg = jnp.dot(x, wg, preferred_element_type=F32)
    hu = jnp.dot(x, wu, preferred_element_type=F32)
    return _silu(hg) * hu


def kernel(x_prompt, x_sample, c, c_ctx, cache_attn_k, cache_attn_v, state_rwkv_fwd, state_rwkv_bwd,
           state_hgrn_fwd, state_hgrn_bwd, norm1_g, norm2_g, mod_w, mod_b, ab_w_in, ab_w_out, attn_q_norm,
           attn_k_norm, attn_sink, rwkv_mu, rwkv_w0, rwkv_w2, rwkv_a0, rwkv_a2, rwkv_g2, rwkv_k_k, rwkv_k_a,
           rwkv_r_k, rwkv_ln_w, rwkv_ln_b, hgrn_w_in, hgrn_w_out, hgrn_lower_bounds, hgrn_norm_g, moe_router,
           moe_bias, moe_w_gate, moe_w_up, moe_w_down, moe_shared_gate, moe_shared_up, moe_shared_down):
    n_cseq, cseq, _ = x_prompt.shape
    n_lseq, lseq, _ = x_sample.shape
    depth = mod_w.shape[0]
    assert depth == 2 and n_lseq + 1 <= SUBLANES
    assert cseq == TM and lseq % TM == 0 and lseq % HG_TB == 0 and cseq % HG_TB == 0
    assert n_cseq % RW_BB == 0 and n_lseq % RW_BB == 0 and (n_cseq * cseq) % (lseq * RW_BB) == 0
    rows = _Rows(n_cseq * cseq, n_lseq * lseq, lseq)
    assert rows.n % MOE_TILE == 0 and lseq % MOE_TILE == 0 and rows.n_ctx % MOE_TILE == 0
    kv_w = KV_A * HD_A

    x = jnp.concatenate([x_prompt.reshape(rows.n_ctx, D_MODEL), x_sample.reshape(rows.n_lat, D_MODEL)], axis=0)
    cvecs = jnp.concatenate([c_ctx[None, :], c, jnp.zeros((SUBLANES - 1 - n_lseq, D_MODEL), F32)], axis=0)
    mod4 = _modulation(cvecs, mod_w, mod_b).reshape(depth, SUBLANES, 1, 6 * D_MODEL)

    ones_q = _block_ones(W_A, HD_A)
    ones_k = _block_ones(kv_w, HD_A)
    ones_b = _block_ones(W_B, HD_B)
    ones_pair = _block_ones(LANES, HD_B)[:LANES]
    ones_pair = jnp.kron(jnp.eye(2, dtype=BF16), ones_pair)
    cos_t, sin_t = _rope_tables(lseq)

    def moe(l, hffn, hpack, x1):
        mp = {'wg': _bf(moe_w_gate[l]), 'wu': _bf(moe_w_up[l]), 'wd': _bf(moe_w_down[l]),
              'sg': _bf(moe_shared_gate[l]), 'su': _bf(moe_shared_up[l]), 'sd': _bf(moe_shared_down[l])}
        return _moe(rows, hffn, hpack, x1, moe_router[l], moe_bias[l], mp, mod4, l)

    pr = {'mu': rwkv_mu[0], 'w0': rwkv_w0[0], 'w2': rwkv_w2[0], 'a0': rwkv_a0[0], 'a2': rwkv_a2[0],
          'g2': rwkv_g2[0], 'k_k': rwkv_k_k[0], 'k_a': rwkv_k_a[0], 'r_k': rwkv_r_k[0].reshape(-1),
          'ln_w': rwkv_ln_w[0], 'ln_b': rwkv_ln_b[0]}
    p_att, p_rw = _inproj(rows, x, norm1_g[0], mod4, 0, _bf(ab_w_in[0]), (ATT_IN, RWKV_IN))
    qg_t = jnp.tile(attn_q_norm[0], H_A).reshape(1, W_A)
    kg_t = jnp.tile(attn_k_norm[0], KV_A).reshape(1, kv_w)
    o_att_c, new_k, new_v = _ctx_attention(p_att, n_cseq, cseq, qg_t, kg_t, attn_sink[0], ones_q, ones_k)
    past = cache_attn_k.shape[2]
    o_att_l = _lat_attention(p_att, rows.n_ctx // lseq, n_lseq, lseq, qg_t, kg_t, attn_sink[0], ones_q, ones_k,
                             cos_t, sin_t, cache_attn_k[:, 0].reshape(n_lseq, past, kv_w),
                             cache_attn_v[:, 0].reshape(n_lseq, past, kv_w))
    o_att = jnp.concatenate([o_att_c, o_att_l], axis=0)

    pp = _rwkv_prep(rows, p_rw, pr, ones_b)
    zero_st = jnp.zeros((n_cseq, H_B // 2, HD_B, LANES), F32)
    of_c, ob_c, sf_c, sb_c = _rwkv_scan(pp, 0, n_cseq, cseq, zero_st, zero_st, ones_pair)
    of_l, ob_l, _, _ = _rwkv_scan(pp, rows.n_ctx, n_lseq, lseq, _state_to_pairs(state_rwkv_fwd[:, 0]),
                                  _state_to_pairs(state_rwkv_bwd[:, 0]), ones_pair)
    o_f = jnp.concatenate([of_c.reshape(rows.n_ctx, W_B), of_l.reshape(rows.n_lat, W_B)], axis=0)
    o_b = jnp.concatenate([ob_c.reshape(rows.n_ctx, W_B), ob_l.reshape(rows.n_lat, W_B)], axis=0)
    x1, hffn, hpack = _outproj0(rows, x, o_att, o_f, o_b, pp, pr, ones_b, _bf(ab_w_out[0]), norm2_g[0], mod4, 0)
    x = moe(0, hffn, hpack, x1)

    (p1,) = _inproj(rows, x, norm1_g[1], mod4, 1, _bf(hgrn_w_in[0]), (IN_C,))
    zero_h = jnp.zeros((n_cseq, H_C, DK_C, DV_C), F32)
    hf_c, hb_c, hsf_c, hsb_c = _hgrn_scan(p1, 0, n_cseq, cseq, hgrn_lower_bounds, zero_h, zero_h)
    hf_l, hb_l, _, _ = _hgrn_scan(p1, rows.n_ctx, n_lseq, lseq, hgrn_lower_bounds,
                                  state_hgrn_fwd[:, 0], state_hgrn_bwd[:, 0])
    h_f = jnp.concatenate([hf_c, hf_l], axis=0)
    h_b = jnp.concatenate([hb_c, hb_l], axis=0)
    x1, hffn, hpack = _outproj1(rows, x, h_f, h_b, p1, hgrn_norm_g[0], _bf(hgrn_w_out[0]), norm2_g[1], mod4, 1)
    x = moe(1, hffn, hpack, x1)

    y_prompt = x[:rows.n_ctx].reshape(n_cseq, cseq, D_MODEL)
    y_sample = x[rows.n_ctx:].reshape(n_lseq, lseq, D_MODEL)
    return (y_prompt, y_sample,
            new_k.reshape(n_cseq, 1, cseq, KV_A, HD_A), new_v.reshape(n_cseq, 1, cseq, KV_A, HD_A),
            _pairs_to_state(sf_c)[:, None], _pairs_to_state(sb_c)[:, None],
            hsf_c[:, None], hsb_c[:, None])
```

```python
import functools

import numpy as np
import jax
import jax.numpy as jnp
from jax import lax
from jax.experimental import pallas as pl
from jax.experimental.pallas import tpu as pltpu
from jax.experimental.pallas import tpu_sc as plsc

F32 = jnp.float32
BF16 = jnp.bfloat16

D_MODEL = 1024
GRID_W = 64
H_A = 8
KV_A = 2
G_A = H_A // KV_A
HD_A = 64
W_A = H_A * HD_A
WINDOW = 128
QBLK = 128
ROPE_BASE = 10000.0
ATTN_SCALE = HD_A ** -0.5
NEG_INF = -1e30
H_B = 8
HD_B = 64
W_B = H_B * HD_B
LORA_W = 64
LORA_A = 64
LORA_G = 128
GN_EPS = 64e-5
ATT_IN = W_A + 2 * KV_A * HD_A
RWKV_IN = 3 * W_B + LORA_W + LORA_A + LORA_G
IN_AB = ATT_IN + RWKV_IN
H_C = 8
DK_C = 128
DV_C = 128
D_C = H_C * DV_C
CHUNK = 64
IN_C = 5 * D_C
N_EXPERTS = 64
TOP_K = 8
N_GROUPS = 8
TOPK_GROUPS = 4
D_EXPERT = 256
ROUTED_SCALE = 2.5
EPS = 1e-6

LANES = 128
SUBLANES = 8
VMEM_LIMIT = 52 * 1024 * 1024

TM = 256
RW_TB = 128
RW_BB = 4
RW_GROUP_BB = 4
HG_TB = 256
MOE_TILE = 512
POS_TB = 512
SC_CORES = 2
SC_SUBCORES = 16
SC_CHUNK = 64


def _cparams(n_axes):
    return pltpu.CompilerParams(dimension_semantics=("arbitrary",) * n_axes,
                                vmem_limit_bytes=VMEM_LIMIT)


def _bf(x):
    return x.astype(BF16)


def _split2(x):
    hi = lax.bitcast_convert_type(
        lax.bitcast_convert_type(x, jnp.uint32) & jnp.uint32(0xFFFF0000), F32)
    return hi, x - hi


def _seg_sum(x, ones2):
    hi, lo = _split2(x)
    return jnp.dot(jnp.concatenate([_bf(hi), _bf(lo)], axis=1), ones2,
                   preferred_element_type=F32)


def _dot_nt(a, b):
    return lax.dot_general(a, b, (((1,), (1,)), ((), ())), preferred_element_type=F32)


def _dot_tn(a, b):
    return lax.dot_general(a, b, (((0,), (0,)), ((), ())), preferred_element_type=F32)


def _sigmoid(x):
    return 1.0 / (1.0 + jnp.exp(-x))


def _silu(x):
    return x * _sigmoid(x)


def _chunks(seq, n):
    seq = list(seq)
    return [seq[i:i + n] for i in range(0, len(seq), n)]


def _block_ones(width, seg):
    idx = np.arange(width) // seg
    bd = (idx[:, None] == idx[None, :]).astype(np.float32)
    return jnp.asarray(np.concatenate([bd, bd], axis=0), dtype=BF16)


def _mod_kernel(c_ref, w_ref, b_ref, o_ref):
    s = _silu(c_ref[...])
    o_ref[0] = jnp.dot(_bf(s), _bf(w_ref[0]), preferred_element_type=F32) + b_ref[0]


def _modulation(cvecs, mod_w, mod_b):
    depth = mod_w.shape[0]
    n_col = 6 * D_MODEL // D_MODEL
    return pl.pallas_call(
        _mod_kernel,
        grid=(depth, n_col),
        in_specs=[pl.BlockSpec((SUBLANES, D_MODEL), lambda l, j: (0, 0)),
                  pl.BlockSpec((1, D_MODEL, D_MODEL), lambda l, j: (l, 0, j)),
                  pl.BlockSpec((1, 1, D_MODEL), lambda l, j: (l, 0, j))],
        out_specs=pl.BlockSpec((1, SUBLANES, D_MODEL), lambda l, j: (l, 0, j)),
        out_shape=jax.ShapeDtypeStruct((depth, SUBLANES, 6 * D_MODEL), F32),
        compiler_params=_cparams(2),
        name="modulation",
    )(cvecs, mod_w, mod_b.reshape(depth, 1, 6 * D_MODEL))


class _Rows:
    def __init__(self, n_ctx, n_lat, lat_seq):
        self.n_ctx, self.n_lat, self.lat_seq = n_ctx, n_lat, lat_seq
        self.n = n_ctx + n_lat

    def mod_row(self, i, tm):
        nctx_blk = self.n_ctx // tm
        per_seq = self.lat_seq // tm
        return jnp.where(i < nctx_blk, 0, 1 + (i - nctx_blk) // per_seq)

    def mod_spec(self, layer, chunk, tm, blk0=0):
        return pl.BlockSpec((None, None, 1, D_MODEL),
                            lambda i, *_: (layer, self.mod_row(i + blk0, tm), 0, chunk))


def _rms_mod(x, g, sc, sh):
    ms = jnp.mean(x * x, axis=-1, keepdims=True)
    return x * lax.rsqrt(ms + EPS) * g * (1.0 + sc) + sh


def _x_specs(rows, xs):
    xa, xb = xs
    nctx_blk = rows.n_ctx // TM
    lat0 = nctx_blk if xb.shape[0] == rows.n else 0
    return [pl.BlockSpec((TM, D_MODEL), lambda i: (jnp.minimum(i, nctx_blk - 1), 0)),
            pl.BlockSpec((TM, D_MODEL), lambda i: (jnp.maximum(i - nctx_blk, 0) + lat0, 0))]


def _pick_x(rows, xa_ref, xb_ref):
    return jnp.where(pl.program_id(0) < rows.n_ctx // TM, xa_ref[...], xb_ref[...])


def _inproj_kernel(rows, splits, xa_ref, xb_ref, g_ref, sh_ref, sc_ref, w_ref, *o_refs):
    h = _rms_mod(_pick_x(rows, xa_ref, xb_ref), g_ref[...], sc_ref[...], sh_ref[...])
    p = jnp.dot(_bf(h), w_ref[...], preferred_element_type=F32)
    lo = 0
    for o_ref, width in zip(o_refs, splits):
        o_ref[...] = p[:, lo:lo + width]
        lo += width


def _inproj(rows, xs, g, mod4, layer, w_bf, splits):
    n_out = w_bf.shape[1]
    return pl.pallas_call(
        functools.partial(_inproj_kernel, rows, splits),
        grid=(rows.n // TM,),
        in_specs=_x_specs(rows, xs) + [
            pl.BlockSpec((1, D_MODEL), lambda i: (0, 0)),
            rows.mod_spec(layer, 0, TM),
            rows.mod_spec(layer, 1, TM),
            pl.BlockSpec((D_MODEL, n_out), lambda i: (0, 0))],
        out_specs=[pl.BlockSpec((TM, wd), lambda i: (i, 0)) for wd in splits],
        out_shape=[jax.ShapeDtypeStruct((rows.n, wd), F32) for wd in splits],
        compiler_params=_cparams(1),
        name=f"inproj{layer}",
    )(*xs, g.reshape(1, D_MODEL), mod4, mod4, w_bf)


def _head_rms(x, gain_t, ones2):
    ms = _seg_sum(x * x, ones2) * (1.0 / HD_A)
    return x * lax.rsqrt(ms + EPS) * gain_t


def _sink_softmax_pv(parts, sink):
    m = jnp.maximum(functools.reduce(jnp.maximum, [jnp.max(s, axis=-1, keepdims=True) for s, _ in parts]), sink)
    den = jnp.exp(sink - m)
    acc = None
    for s, v in parts:
        p = jnp.exp(s - m)
        den = den + jnp.sum(p, axis=-1, keepdims=True)
        pv = jnp.dot(_bf(p), v, preferred_element_type=F32)
        acc = pv if acc is None else acc + pv
    return acc / den


def _ctx_attn_kernel(p_ref, qg_ref, kg_ref, sink_ref, ones_q_ref, ones_k_ref, prev_ref, o_ref, k_ref, v_ref):
    del prev_ref
    p = p_ref[...]
    q = _head_rms(p[:, :W_A], qg_ref[...], ones_q_ref[...]) * ATTN_SCALE
    k = _head_rms(p[:, W_A:W_A + KV_A * HD_A], kg_ref[...], ones_k_ref[...])
    v = p[:, W_A + KV_A * HD_A:ATT_IN]
    k_ref[0] = k
    v_ref[0] = v
    qb, kb, vb = _bf(q), _bf(k), _bf(v)
    outs = []
    for h in range(H_A):
        j = h // G_A
        s = _dot_nt(qb[:, h * HD_A:(h + 1) * HD_A], kb[:, j * HD_A:(j + 1) * HD_A])
        outs.append(_sink_softmax_pv([(s, vb[:, j * HD_A:(j + 1) * HD_A])], sink_ref[h]))
    o_ref[...] = jnp.concatenate(outs, axis=1)


def _ctx_attention(p_att, n_seq, seq, qg_t, kg_t, sink, ones_q, ones_k, prev):
    kv_w = KV_A * HD_A
    return pl.pallas_call(
        _ctx_attn_kernel,
        grid=(n_seq,),
        in_specs=[pl.BlockSpec((seq, ATT_IN), lambda b: (b, 0)),
                  pl.BlockSpec((1, W_A), lambda b: (0, 0)),
                  pl.BlockSpec((1, kv_w), lambda b: (0, 0)),
                  pl.BlockSpec(memory_space=pltpu.SMEM),
                  pl.BlockSpec(ones_q.shape, lambda b: (0, 0)),
                  pl.BlockSpec(ones_k.shape, lambda b: (0, 0)),
                  pl.BlockSpec(memory_space=pl.ANY)],
        out_specs=[pl.BlockSpec((seq, W_A), lambda b: (b, 0)),
                   pl.BlockSpec((1, seq, kv_w), lambda b: (b, 0, 0)),
                   pl.BlockSpec((1, seq, kv_w), lambda b: (b, 0, 0))],
        input_output_aliases={6: 0},
        out_shape=[jax.ShapeDtypeStruct(prev.shape, F32),
                   jax.ShapeDtypeStruct((n_seq, seq, kv_w), F32),
                   jax.ShapeDtypeStruct((n_seq, seq, kv_w), F32)],
        compiler_params=_cparams(1),
        name="ctx_attention",
    )(p_att, qg_t, kg_t, sink, ones_q, ones_k, prev)


def _rope(x, cos_t, sin_t):
    lane = lax.broadcasted_iota(jnp.int32, cos_t.shape, 1)
    low = (lane % 32) < 16
    outs = []
    for s in range(x.shape[1] // LANES):
        xs = x[:, s * LANES:(s + 1) * LANES]
        partner = jnp.where(low, pltpu.roll(xs, LANES - 16, 1), pltpu.roll(xs, 16, 1))
        outs.append(xs * cos_t + partner * sin_t)
    return outs[0] if len(outs) == 1 else jnp.concatenate(outs, axis=1)


def _lat_attn_kernel(seq, p_ref, qg_ref, kg_ref, sink_ref, ones_q_ref, ones_k_ref, cos_ref, sin_ref,
                     kc_ref, vc_ref, prev_ref, o_ref, q_scr, k_scr, v_scr):
    del prev_ref
    kv_w = KV_A * HD_A
    p = p_ref[...]
    q = _head_rms(p[:, :W_A], qg_ref[...], ones_q_ref[...])
    k = _head_rms(p[:, W_A:W_A + kv_w], kg_ref[...], ones_k_ref[...])
    q_scr[...] = _bf(_rope(q, cos_ref[...], sin_ref[...]) * ATTN_SCALE)
    k_scr[...] = _bf(_rope(k, cos_ref[...], sin_ref[...]))
    v_scr[...] = _bf(p[:, W_A + kv_w:ATT_IN])
    kc = _bf(kc_ref[0])
    vc = _bf(vc_ref[0])
    n_local = 3 * QBLK

    def block(i, carry):
        q0 = pl.multiple_of(i * QBLK, QBLK)
        start = pl.multiple_of(jnp.clip((i - 1) * QBLK, 0, seq - n_local), QBLK)
        qb = q_scr[pl.ds(q0, QBLK), :]
        kl = k_scr[pl.ds(start, n_local), :]
        vl = v_scr[pl.ds(start, n_local), :]
        ipos = q0 + lax.broadcasted_iota(jnp.int32, (QBLK, n_local), 0)
        jpos = start + lax.broadcasted_iota(jnp.int32, (QBLK, n_local), 1)
        band = jnp.abs(jpos - ipos) <= WINDOW
        outs = []
        for h in range(H_A):
            j = h // G_A
            qh = qb[:, h * HD_A:(h + 1) * HD_A]
            s_loc = jnp.where(band, _dot_nt(qh, kl[:, j * HD_A:(j + 1) * HD_A]), NEG_INF)
            s_ctx = _dot_nt(qh, kc[:, j * HD_A:(j + 1) * HD_A])
            outs.append(_sink_softmax_pv([(s_loc, vl[:, j * HD_A:(j + 1) * HD_A]),
                                          (s_ctx, vc[:, j * HD_A:(j + 1) * HD_A])], sink_ref[h]))
        o_ref[pl.ds(q0, QBLK), :] = jnp.concatenate(outs, axis=1)
        return carry

    lax.fori_loop(0, seq // QBLK, block, 0)


def _lat_attention(p_att, row_blk0, n_seq, seq, qg_t, kg_t, sink, ones_q, ones_k, cos_t, sin_t, kc, vc, prev):
    kv_w = KV_A * HD_A
    past = kc.shape[1]
    return pl.pallas_call(
        functools.partial(_lat_attn_kernel, seq),
        grid=(n_seq,),
        in_specs=[pl.BlockSpec((seq, ATT_IN), lambda b: (row_blk0 + b, 0)),
                  pl.BlockSpec((1, W_A), lambda b: (0, 0)),
                  pl.BlockSpec((1, kv_w), lambda b: (0, 0)),
                  pl.BlockSpec(memory_space=pltpu.SMEM),
                  pl.BlockSpec(ones_q.shape, lambda b: (0, 0)),
                  pl.BlockSpec(ones_k.shape, lambda b: (0, 0)),
                  pl.BlockSpec((seq, LANES), lambda b: (0, 0)),
                  pl.BlockSpec((seq, LANES), lambda b: (0, 0)),
                  pl.BlockSpec((1, past, kv_w), lambda b: (b, 0, 0)),
                  pl.BlockSpec((1, past, kv_w), lambda b: (b, 0, 0)),
                  pl.BlockSpec(memory_space=pl.ANY)],
        out_specs=pl.BlockSpec((seq, W_A), lambda b: (row_blk0 + b, 0)),
        out_shape=jax.ShapeDtypeStruct(prev.shape, F32),
        input_output_aliases={10: 0},
        scratch_shapes=[pltpu.VMEM((seq, W_A), BF16), pltpu.VMEM((seq, kv_w), BF16),
                        pltpu.VMEM((seq, kv_w), BF16)],
        compiler_params=_cparams(1),
        name="lat_attention",
    )(p_att, qg_t, kg_t, sink, ones_q, ones_k, cos_t, sin_t, kc, vc, prev)


def _rope_tables(seq):
    pos = np.arange(seq)
    row = (pos // GRID_W).astype(np.float32)
    col = (pos % GRID_W).astype(np.float32)
    d_axis = HD_A // 2
    inv = (ROPE_BASE ** (-np.arange(0, d_axis, 2, dtype=np.float32) / d_axis)).astype(np.float32)
    cos_h = np.zeros((seq, HD_A), np.float32)
    sin_h = np.zeros((seq, HD_A), np.float32)
    for seg, p_ in enumerate((row, col)):
        ang = (p_[:, None] * inv[None, :]).astype(np.float32)
        c, s = np.cos(ang), np.sin(ang)
        base = seg * d_axis
        cos_h[:, base:base + d_axis // 2] = c
        cos_h[:, base + d_axis // 2:base + d_axis] = c
        sin_h[:, base:base + d_axis // 2] = -s
        sin_h[:, base + d_axis // 2:base + d_axis] = s
    rep = LANES // HD_A
    return jnp.asarray(np.tile(cos_h, (1, rep))), jnp.asarray(np.tile(sin_h, (1, rep)))


def _rwkv_prep_kernel(rows, x_ref, prev_ref, next_ref, mu_ref, kk_ref, ka_ref, rk_ref, w0_ref, w2_ref,
                      a0_ref, a2_ref, g2_ref, ones_ref,
                      nkk_ref, r_ref, v_ref, g_ref, bonus_ref,
                      wf_ref, kaf_ref, kdf_ref, wb_ref, kab_ref, kdb_ref):
    i = pl.program_id(0)
    nctx_blk = rows.n_ctx // TM
    per_seq = rows.lat_seq // TM
    is_ctx = i < nctx_blk
    first = jnp.logical_or(is_ctx, (i - nctx_blk) % per_seq == 0)
    last = jnp.logical_or(is_ctx, (i - nctx_blk) % per_seq == per_seq - 1)
    x = x_ref[...]
    ridx = lax.broadcasted_iota(jnp.int32, x.shape, 0)
    prev_row = jnp.where(first, 0.0, prev_ref[SUBLANES - 1:SUBLANES, :])
    next_row = jnp.where(last, 0.0, next_ref[0:1, :])
    xm1 = jnp.where(ridx == 0, prev_row, pltpu.roll(x, 1, 0))
    xp1 = jnp.where(ridx == TM - 1, next_row, pltpu.roll(x, TM - 1, 0))
    pw = x + (0.5 * (xm1 + xp1) - x) * mu_ref[...]

    r = pw[:, 0:W_B]
    k = pw[:, W_B:2 * W_B]
    v = pw[:, 2 * W_B:3 * W_B]
    wd = pw[:, 3 * W_B:3 * W_B + LORA_W]
    ad = pw[:, 3 * W_B + LORA_W:3 * W_B + LORA_W + LORA_A]
    gd = pw[:, 3 * W_B + LORA_W + LORA_A:]
    ones2 = ones_ref[...]

    kk = k * kk_ref[...]
    kk = kk / jnp.maximum(jnp.sqrt(_seg_sum(kk * kk, ones2)), 1e-12)
    nkk_ref[...] = -kk
    r_ref[...] = r
    v_ref[...] = v
    g_ref[...] = jnp.dot(_bf(_sigmoid(gd)), g2_ref[...], preferred_element_type=F32)
    tw = _bf(jnp.tanh(wd))
    adb = _bf(ad)
    bonus = jnp.zeros_like(r)
    for d, (w_o, ka_o, kd_o) in enumerate(((wf_ref, kaf_ref, kdf_ref), (wb_ref, kab_ref, kdb_ref))):
        z = -(w0_ref[d:d + 1, :] + jnp.dot(tw, w2_ref[d], preferred_element_type=F32))
        softplus = jnp.maximum(z, 0.0) + jnp.log(1.0 + jnp.exp(-jnp.abs(z)))
        w_o[...] = jnp.exp(-jnp.exp(-softplus - 0.5))
        a = _sigmoid(a0_ref[d:d + 1, :] + jnp.dot(adb, a2_ref[d], preferred_element_type=F32))
        kd = k * (1.0 + (a - 1.0) * ka_ref[...])
        ka_o[...] = kk * a
        kd_o[...] = kd
        bonus = bonus + _seg_sum(r * kd * rk_ref[...], ones2) * v
    bonus_ref[...] = bonus


def _rwkv_prep(rows, p_rw, pr, ones_b):
    n = rows.n
    n_halo = n // SUBLANES
    blk_halo = TM // SUBLANES
    row = lambda a: a.reshape(1, -1)
    full = lambda a: pl.BlockSpec(a.shape, lambda i: (0,) * a.ndim)
    consts = [row(pr['mu']), row(pr['k_k']), row(pr['k_a']), row(pr['r_k']), pr['w0'], _bf(pr['w2']),
              pr['a0'], _bf(pr['a2']), _bf(pr['g2']), ones_b]
    outs = pl.pallas_call(
        functools.partial(_rwkv_prep_kernel, rows),
        grid=(n // TM,),
        in_specs=[pl.BlockSpec((TM, RWKV_IN), lambda i: (i, 0)),
                  pl.BlockSpec((SUBLANES, RWKV_IN), lambda i: (jnp.maximum(i * blk_halo - 1, 0), 0)),
                  pl.BlockSpec((SUBLANES, RWKV_IN), lambda i: (jnp.minimum((i + 1) * blk_halo, n_halo - 1), 0))]
                 + [full(a) for a in consts],
        out_specs=[pl.BlockSpec((TM, W_B), lambda i: (i, 0))] * 11,
        out_shape=[jax.ShapeDtypeStruct((n, W_B), F32)] * 11,
        compiler_params=_cparams(1),
        name="rwkv_prep",
    )(p_rw, p_rw, p_rw, *consts)
    names = ('nkk', 'r', 'v', 'g', 'bonus', 'w_f', 'ka_f', 'kd_f', 'w_b', 'ka_b', 'kd_b')
    return dict(zip(names, outs))


def _rwkv_scan_kernel(n_tb, nkkf_ref, rf_ref, vf_ref, wf_ref, kaf_ref, kdf_ref,
                      nkkb_ref, rb_ref, vb_ref, wb_ref, kab_ref, kdb_ref,
                      s0f_ref, s0b_ref, ones_ref, prevf_ref, prevb_ref,
                      of_ref, ob_ref, sff_ref, sfb_ref, s_scr, vt_scr):
    del prevf_ref, prevb_ref
    tb = pl.program_id(1)
    n_pair = H_B // 2
    half = RW_TB // 2
    dirs = ((nkkf_ref, rf_ref, vf_ref, wf_ref, kaf_ref, kdf_ref, of_ref, False),
            (nkkb_ref, rb_ref, vb_ref, wb_ref, kab_ref, kdb_ref, ob_ref, True))

    @pl.when(tb == 0)
    def _():
        s_scr[0] = s0f_ref[...]
        s_scr[1] = s0b_ref[...]

    lane = lax.broadcasted_iota(jnp.int32, (HD_B, LANES), 1)
    for d, refs in enumerate(dirs):
        v_ref = refs[2]
        for bb in range(RW_BB):
            for p in range(n_pair):
                vt = v_ref[bb, :, p * LANES:(p + 1) * LANES].T
                top, bot = vt[:HD_B], vt[HD_B:]
                for s in range(2):
                    if s == 0:
                        t2 = jnp.where(lane < HD_B, top, pltpu.roll(bot, HD_B, 1))
                    else:
                        t2 = jnp.where(lane < HD_B, pltpu.roll(top, HD_B, 1), bot)
                    vt_scr[d, bb, p, s] = t2

    ones2 = ones_ref[...]
    row8 = lax.broadcasted_iota(jnp.int32, (SUBLANES, LANES), 0)
    lane8 = lax.broadcasted_iota(jnp.int32, (SUBLANES, LANES), 1)
    sel_r = jnp.logical_or(jnp.logical_and(row8 % 2 == 0, lane8 < HD_B),
                           jnp.logical_and(row8 % 2 == 1, lane8 >= HD_B))

    def row_of(rev, tt):
        return RW_TB - 1 - tt if rev else tt

    def emit_output(d, bb, tau):
        r_ref, o_ref = dirs[d][1], dirs[d][6]
        r = r_ref[bb, pl.ds(tau, 1), :]
        r8 = jnp.zeros((SUBLANES, LANES), F32)
        for p in range(n_pair):
            rp = jnp.broadcast_to(r[:, p * LANES:(p + 1) * LANES], (SUBLANES, LANES))
            r8 = jnp.where(jnp.logical_and(sel_r, row8 // 2 == p), rp, r8)
        s_all = jnp.concatenate([_bf(s_scr[d, bb, p]) for p in range(n_pair)], axis=0)
        o8 = _dot_nt(_bf(r8), s_all)
        o_parts = []
        for p in range(n_pair):
            for h in range(2):
                o_parts.append(o8[2 * p + h:2 * p + h + 1, p * HD_B:(p + 1) * HD_B])
        o_ref[bb, pl.ds(tau, 1), :] = jnp.concatenate(o_parts, axis=1)

    groups = [(d, bbs) for d in range(2) for bbs in _chunks(range(RW_BB), RW_GROUP_BB)]

    def step(tt, carry):
        tt_prev = jnp.maximum(tt - 1, 0)
        reds = []
        for d, bbs in groups:
            rev = dirs[d][7]
            tau = row_of(rev, tt)
            sub = tau // half
            lt = tau % half
            mask = jnp.logical_or(lane == lt, lane == lt + HD_B)
            lhs = []
            for bb in bbs:
                emit_output(d, bb, row_of(rev, tt_prev))
                nkk = dirs[d][0][bb, pl.ds(tau, 1), :]
                for p in range(n_pair):
                    prod = s_scr[d, bb, p] * nkk[:, p * LANES:(p + 1) * LANES]
                    lhs.append(jnp.concatenate([_bf(prod), _bf(jnp.where(mask, vt_scr[d, bb, p, sub], 0.0))],
                                               axis=1))
            reds.append(jnp.dot(jnp.concatenate(lhs, axis=0), ones2, preferred_element_type=F32))
        for (d, bbs), red in zip(groups, reds):
            _, _, _, w_ref, ka_ref, kd_ref, _, rev = dirs[d]
            tau = row_of(rev, tt)
            for k, bb in enumerate(bbs):
                w = w_ref[bb, pl.ds(tau, 1), :]
                ka = ka_ref[bb, pl.ds(tau, 1), :]
                kd = kd_ref[bb, pl.ds(tau, 1), :]
                for p in range(n_pair):
                    sl = slice(p * LANES, (p + 1) * LANES)
                    r0 = (k * n_pair + p) * HD_B
                    sa = red[r0:r0 + HD_B, :LANES]
                    vcol = red[r0:r0 + HD_B, LANES:]
                    s_scr[d, bb, p] = s_scr[d, bb, p] * w[:, sl] + sa * ka[:, sl] + vcol * kd[:, sl]
        return carry

    lax.fori_loop(0, RW_TB, step, 0)
    for d in range(2):
        for bb in range(RW_BB):
            emit_output(d, bb, row_of(dirs[d][7], RW_TB - 1))

    @pl.when(tb == n_tb - 1)
    def _():
        sff_ref[...] = s_scr[0]
        sfb_ref[...] = s_scr[1]


def _rwkv_scan(pp, row0, n_seq, seq, s0_f, s0_b, ones_pair, prev_f, prev_b):
    n_tb = seq // RW_TB
    n_pair = H_B // 2
    blk0 = row0 // seq
    view = lambda a: a.reshape(a.shape[0] // seq, seq, W_B)
    fwd = pl.BlockSpec((RW_BB, RW_TB, W_B), lambda b, t: (blk0 // RW_BB + b, t, 0))
    bwd = pl.BlockSpec((RW_BB, RW_TB, W_B), lambda b, t: (blk0 // RW_BB + b, n_tb - 1 - t, 0))
    st = pl.BlockSpec((RW_BB, n_pair, HD_B, LANES), lambda b, t: (b, 0, 0, 0))
    ins_f = [view(pp[k]) for k in ('nkk', 'r', 'v', 'w_f', 'ka_f', 'kd_f')]
    ins_b = [view(pp[k]) for k in ('nkk', 'r', 'v', 'w_b', 'ka_b', 'kd_b')]
    st_shape = jax.ShapeDtypeStruct((n_seq, n_pair, HD_B, LANES), F32)
    o_shape = jax.ShapeDtypeStruct(view(prev_f).shape, F32)
    any_spec = pl.BlockSpec(memory_space=pl.ANY)
    o_f, o_b, sf, sb = pl.pallas_call(
        functools.partial(_rwkv_scan_kernel, n_tb),
        grid=(n_seq // RW_BB, n_tb),
        in_specs=[fwd] * 6 + [bwd] * 6 + [st, st, pl.BlockSpec(ones_pair.shape, lambda b, t: (0, 0)),
                                           any_spec, any_spec],
        out_specs=[fwd, bwd, st, st],
        out_shape=[o_shape, o_shape, st_shape, st_shape],
        input_output_aliases={15: 0, 16: 1},
        scratch_shapes=[pltpu.VMEM((2, RW_BB, n_pair, HD_B, LANES), F32),
                        pltpu.VMEM((2, RW_BB, n_pair, 2, HD_B, LANES), F32)],
        compiler_params=_cparams(2),
        name="rwkv_scan",
    )(*ins_f, *ins_b, s0_f, s0_b, ones_pair, view(prev_f), view(prev_b))
    return o_f.reshape(prev_f.shape), o_b.reshape(prev_b.shape), sf, sb


def _state_to_pairs(s):
    b = s.shape[0]
    return s.reshape(b, H_B // 2, 2, HD_B, HD_B).transpose(0, 1, 3, 2, 4).reshape(b, H_B // 2, HD_B, 2 * HD_B)


def _pairs_to_state(s):
    b = s.shape[0]
    return s.reshape(b, H_B // 2, HD_B, 2, HD_B).transpose(0, 1, 3, 2, 4).reshape(b, H_B, HD_B, HD_B)


def _tail(x, y, g1, n2g, sc2, sh2, x1_ref, h_ref, hp_ref):
    x1 = x + g1 * y
    x1_ref[...] = x1
    h = _rms_mod(x1, n2g, sc2, sh2)
    h_ref[...] = _bf(h)
    hp_ref[...] = _pack_pairs(h)


def _outproj0_kernel(rows, xa_ref, xb_ref, oa_ref, of_ref, ob_ref, bonus_ref, g_ref, lnw_ref, lnb_ref, ones_ref,
                     w_ref, g1_ref, n2g_ref, sc2_ref, sh2_ref, x1_ref, h_ref, hp_ref):
    o_sum = of_ref[...] + ob_ref[...]
    ones2 = ones_ref[...]
    mean = _seg_sum(o_sum, ones2) * (1.0 / HD_B)
    cen = o_sum - mean
    var = _seg_sum(cen * cen, ones2) * (1.0 / HD_B)
    gn = cen * lax.rsqrt(var + GN_EPS) * lnw_ref[...] + lnb_ref[...]
    o_rw = (gn + bonus_ref[...]) * g_ref[...]
    mix = jnp.concatenate([_bf(oa_ref[...]), _bf(o_rw)], axis=1)
    y = jnp.dot(mix, w_ref[...], preferred_element_type=F32)
    _tail(_pick_x(rows, xa_ref, xb_ref), y, g1_ref[...], n2g_ref[...], sc2_ref[...], sh2_ref[...], x1_ref, h_ref, hp_ref)


def _outproj0(rows, xs, o_att, o_f, o_b, pp, pr, ones_b, w_out_bf, n2g, mod4, layer):
    n = rows.n
    tok = lambda w: pl.BlockSpec((TM, w), lambda i: (i, 0))
    const = lambda a: pl.BlockSpec(a.shape, lambda i: (0,) * a.ndim)
    lnw, lnb, n2 = pr['ln_w'].reshape(1, -1), pr['ln_b'].reshape(1, -1), n2g.reshape(1, -1)
    return pl.pallas_call(
        functools.partial(_outproj0_kernel, rows),
        grid=(n // TM,),
        in_specs=_x_specs(rows, xs) + [tok(W_A), tok(W_B), tok(W_B), tok(W_B), tok(W_B),
                  const(lnw), const(lnb), const(ones_b), const(w_out_bf),
                  rows.mod_spec(layer, 2, TM), const(n2), rows.mod_spec(layer, 4, TM), rows.mod_spec(layer, 3, TM)],
        out_specs=[tok(D_MODEL), tok(D_MODEL), tok(D_MODEL // 2)],
        out_shape=[jax.ShapeDtypeStruct((n, D_MODEL), F32), jax.ShapeDtypeStruct((n, D_MODEL), BF16),
                   jax.ShapeDtypeStruct((n, D_MODEL // 2), jnp.uint32)],
        compiler_params=_cparams(1),
        name="outproj0",
    )(*xs, o_att, o_f, o_b, pp['bonus'], pp['g'], lnw, lnb, ones_b, w_out_bf, mod4, n2, mod4, mod4)


def _outproj1_kernel(rows, xa_ref, xb_ref, of_ref, ob_ref, gate_ref, ng_ref, w_ref, g1_ref, n2g_ref, sc2_ref, sh2_ref,
                     x1_ref, h_ref, hp_ref):
    o_sum = of_ref[...] + ob_ref[...]
    parts = []
    for h in range(H_C):
        oh = o_sum[:, h * DV_C:(h + 1) * DV_C]
        parts.append(oh * lax.rsqrt(jnp.mean(oh * oh, axis=-1, keepdims=True) + EPS))
    o = jnp.concatenate(parts, axis=1) * ng_ref[...] * _silu(gate_ref[...])
    y = jnp.dot(_bf(o), w_ref[...], preferred_element_type=F32)
    _tail(_pick_x(rows, xa_ref, xb_ref), y, g1_ref[...], n2g_ref[...], sc2_ref[...], sh2_ref[...], x1_ref, h_ref, hp_ref)


def _outproj1(rows, xs, o_f, o_b, p1, norm_g, w_out_bf, n2g, mod4, layer):
    n = rows.n
    tok = lambda w: pl.BlockSpec((TM, w), lambda i: (i, 0))
    const = lambda a: pl.BlockSpec(a.shape, lambda i: (0,) * a.ndim)
    ng, n2 = norm_g.reshape(1, -1), n2g.reshape(1, -1)
    return pl.pallas_call(
        functools.partial(_outproj1_kernel, rows),
        grid=(n // TM,),
        in_specs=_x_specs(rows, xs) + [tok(D_C), tok(D_C), pl.BlockSpec((TM, D_C), lambda i: (i, 4)),
                  const(ng), const(w_out_bf),
                  rows.mod_spec(layer, 2, TM), const(n2), rows.mod_spec(layer, 4, TM), rows.mod_spec(layer, 3, TM)],
        out_specs=[tok(D_MODEL), tok(D_MODEL), tok(D_MODEL // 2)],
        out_shape=[jax.ShapeDtypeStruct((n, D_MODEL), F32), jax.ShapeDtypeStruct((n, D_MODEL), BF16),
                   jax.ShapeDtypeStruct((n, D_MODEL // 2), jnp.uint32)],
        compiler_params=_cparams(1),
        name="outproj1",
    )(*xs, o_f, o_b, p1, ng, w_out_bf, mod4, n2, mod4, mod4)


def _hgrn_kernel(n_tb, qf_ref, ff_ref, if_ref, qb_ref, fb_ref, ib_ref, lbp_ref, s0f_ref, s0b_ref,
                 trif_ref, trib_ref, prevf_ref, prevb_ref, of_ref, ob_ref, sff_ref, sfb_ref, s_scr):
    del prevf_ref, prevb_ref
    tb = pl.program_id(1)

    @pl.when(tb == 0)
    def _():
        for h in range(H_C):
            s_scr[0, h] = s0f_ref[0, h].T
            s_scr[1, h] = s0b_ref[0, h].T

    lbp = lbp_ref[...]
    e = jnp.exp(lbp - jnp.max(lbp, axis=0, keepdims=True))
    sm = e / jnp.sum(e, axis=0, keepdims=True)
    lb = (sm[0:1] + sm[1:2]) - sm[0:1]

    n_chunk = HG_TB // CHUNK
    ti = lax.broadcasted_iota(jnp.int32, (HG_TB, HG_TB), 0)
    si = lax.broadcasted_iota(jnp.int32, (HG_TB, HG_TB), 1)
    same = (ti // CHUNK) == (si // CHUNK)
    dirs = ((qf_ref, ff_ref, if_ref, of_ref, trif_ref, jnp.logical_and(same, ti >= si), CHUNK - 1, False),
            (qb_ref, fb_ref, ib_ref, ob_ref, trib_ref, jnp.logical_and(same, ti <= si), 0, True))

    staged = []
    for d, (q_ref, f_ref, i_ref, o_ref, tri_ref, causal, last_row, rev) in enumerate(dirs):
        q = _silu(q_ref[...])
        f = lb + (1.0 - lb) * _sigmoid(f_ref[...])
        k = 1.0 - f
        v = _bf(i_ref[...])
        g = jnp.log(f)
        g1 = _bf(g)
        g2 = _bf(g - g1.astype(F32))
        tri2 = tri_ref[...]
        b_parts, last_parts, dec = [], [], []
        for c in range(n_chunk):
            rc = slice(c * CHUNK, (c + 1) * CHUNK)
            bc = jnp.dot(tri2, jnp.concatenate([g1[rc], g2[rc]], axis=0), preferred_element_type=F32)
            b_parts.append(bc)
            last = bc[last_row:last_row + 1]
            last_parts.append(jnp.broadcast_to(last, bc.shape))
            dec.append(jnp.exp(last))
        b = jnp.concatenate(b_parts, axis=0)
        b_last = jnp.concatenate(last_parts, axis=0)
        staged.append((_bf(q * jnp.exp(b)), _bf(k * jnp.exp(-b)), _bf(k * jnp.exp(b_last - b)), v, dec))

    for h in range(H_C):
        sl = slice(h * DK_C, (h + 1) * DK_C)
        for d, (q_ref, f_ref, i_ref, o_ref, tri_ref, causal, last_row, rev) in enumerate(dirs):
            q_in, k_in, k_out, v, dec = staged[d]
            qh, vh = q_in[:, sl], v[:, sl]
            att = jnp.where(causal, _dot_nt(qh, k_in[:, sl]), 0.0)
            o_intra = jnp.dot(_bf(att), vh, preferred_element_type=F32)
            s_t = s_scr[d, h]
            for c in (range(n_chunk - 1, -1, -1) if rev else range(n_chunk)):
                rc = slice(c * CHUNK, (c + 1) * CHUNK)
                o_ref[rc, sl] = o_intra[rc] + _dot_nt(qh[rc], _bf(s_t))
                s_t = dec[c][:, sl] * s_t + _dot_tn(vh[rc], k_out[rc, sl])
            s_scr[d, h] = s_t

    @pl.when(tb == n_tb - 1)
    def _():
        for h in range(H_C):
            sff_ref[0, h] = s_scr[0, h].T
            sfb_ref[0, h] = s_scr[1, h].T


def _hgrn_scan(p1, row0, n_seq, seq, lb_params, s0_f, s0_b, prev_f, prev_b):
    n_tb = seq // HG_TB
    blk0 = row0 // HG_TB
    tri = np.tril(np.ones((CHUNK, CHUNK), np.float32))
    tri_f = jnp.asarray(np.concatenate([tri] * 2, axis=1), dtype=BF16)
    tri_b = jnp.asarray(np.concatenate([tri.T] * 2, axis=1), dtype=BF16)
    fwd = lambda col: pl.BlockSpec((HG_TB, D_C), lambda b, t: (blk0 + b * n_tb + t, col))
    bwd = lambda col: pl.BlockSpec((HG_TB, D_C), lambda b, t: (blk0 + b * n_tb + n_tb - 1 - t, col))
    st = pl.BlockSpec((1, H_C, DK_C, DV_C), lambda b, t: (b, 0, 0, 0))
    const = lambda a: pl.BlockSpec(a.shape, lambda b, t: (0,) * a.ndim)
    o_shape = jax.ShapeDtypeStruct(prev_f.shape, F32)
    st_shape = jax.ShapeDtypeStruct((n_seq, H_C, DK_C, DV_C), F32)
    any_spec = pl.BlockSpec(memory_space=pl.ANY)
    return pl.pallas_call(
        functools.partial(_hgrn_kernel, n_tb),
        grid=(n_seq, n_tb),
        in_specs=[fwd(0), fwd(1), fwd(3), bwd(0), bwd(2), bwd(3), const(lb_params), st, st,
                  const(tri_f), const(tri_b), any_spec, any_spec],
        out_specs=[fwd(0), bwd(0), st, st],
        out_shape=[o_shape, o_shape, st_shape, st_shape],
        input_output_aliases={11: 0, 12: 1},
        scratch_shapes=[pltpu.VMEM((2, H_C, DV_C, DK_C), F32)],
        compiler_params=_cparams(2),
        name="hgrn_scan",
    )(p1, p1, p1, p1, p1, p1, lb_params, s0_f, s0_b, tri_f, tri_b, prev_f, prev_b)


def _router_kernel(h_ref, rhi_ref, rlo_ref, bias_ref, sel_ref, eidx_ref, ew_ref, cnt_ref):
    x = h_ref[...]
    tm = x.shape[0]
    logits = _dot_nt(rhi_ref[...], x) + _dot_nt(rlo_ref[...], x)
    scores = _sigmoid(logits)
    biased = scores + bias_ref[...]
    per = N_EXPERTS // N_GROUPS
    sub = lax.broadcasted_iota(jnp.int32, (per, tm), 0)
    gs_rows = []
    for g in range(N_GROUPS):
        blk = biased[g * per:(g + 1) * per]
        m1 = jnp.max(blk, axis=0, keepdims=True)
        first = jnp.min(jnp.where(blk == m1, sub, per), axis=0, keepdims=True)
        m2 = jnp.max(jnp.where(sub == first, -jnp.inf, blk), axis=0, keepdims=True)
        gs_rows.append(m1 + m2)
    gs = jnp.concatenate(gs_rows, axis=0)
    gi = lax.broadcasted_iota(jnp.int32, gs.shape, 0)
    rank = jnp.zeros(gs.shape, jnp.int32)
    for s in range(1, N_GROUPS):
        other = pltpu.roll(gs, s, 0)
        oi = pltpu.roll(gi, s, 0)
        beats = jnp.logical_or(other > gs, jnp.logical_and(other == gs, oi < gi))
        rank = rank + jnp.where(beats, 1, 0)
    keep = jnp.where(rank < TOPK_GROUPS, 1.0, 0.0)
    emask = jnp.concatenate([jnp.broadcast_to(keep[g:g + 1], (per, tm)) for g in range(N_GROUPS)], axis=0)
    cur = jnp.where(emask > 0.0, biased, -jnp.inf)
    ei = lax.broadcasted_iota(jnp.int32, cur.shape, 0)
    sel = jnp.zeros(cur.shape, F32)
    idxs, vals = [], []
    for _ in range(TOP_K):
        m = jnp.max(cur, axis=0, keepdims=True)
        idx = jnp.min(jnp.where(cur == m, ei, N_EXPERTS), axis=0, keepdims=True)
        pick = ei == idx
        idxs.append(idx)
        vals.append(jnp.sum(jnp.where(pick, scores, 0.0), axis=0, keepdims=True))
        sel = jnp.where(pick, 1.0, sel)
        cur = jnp.where(pick, -jnp.inf, cur)
    w = jnp.concatenate(vals, axis=0)
    eidx_ref[...] = jnp.concatenate(idxs, axis=0)
    ew_ref[...] = w / jnp.sum(w, axis=0, keepdims=True) * ROUTED_SCALE
    sel_ref[...] = _bf(sel)

    @pl.when(pl.program_id(0) == 0)
    def _():
        cnt_ref[...] = jnp.zeros_like(cnt_ref)

    cnt_ref[...] += jnp.sum(sel, axis=1, keepdims=True)


def _router(hffn, router, bias):
    n = hffn.shape[0]
    r_t = router.T
    r_hi = _bf(r_t)
    r_lo = _bf(r_t - r_hi.astype(F32))
    const = lambda a: pl.BlockSpec(a.shape, lambda i: (0,) * a.ndim)
    b_col = bias.reshape(N_EXPERTS, 1)
    return pl.pallas_call(
        _router_kernel,
        grid=(n // TM,),
        in_specs=[pl.BlockSpec((TM, D_MODEL), lambda i: (i, 0)), const(r_hi), const(r_lo), const(b_col)],
        out_specs=[pl.BlockSpec((N_EXPERTS, TM), lambda i: (0, i)),
                   pl.BlockSpec((TOP_K, TM), lambda i: (0, i)),
                   pl.BlockSpec((TOP_K, TM), lambda i: (0, i)),
                   pl.BlockSpec((N_EXPERTS, LANES), lambda i: (0, 0))],
        out_shape=[jax.ShapeDtypeStruct((N_EXPERTS, n), BF16),
                   jax.ShapeDtypeStruct((TOP_K, n), jnp.int32),
                   jax.ShapeDtypeStruct((TOP_K, n), F32),
                   jax.ShapeDtypeStruct((N_EXPERTS, LANES), F32)],
        compiler_params=_cparams(1),
        name="router",
    )(hffn, r_hi, r_lo, b_col)


def _positions_kernel(sel_ref, eidx_ref, base_ref, upper_ref, pos_ref, carry_ref):
    @pl.when(pl.program_id(0) == 0)
    def _():
        carry_ref[...] = jnp.zeros_like(carry_ref)

    sel = sel_ref[...]
    rank = jnp.dot(sel, upper_ref[...], preferred_element_type=F32)
    pos_e = base_ref[:, 0:1] + carry_ref[:, 0:1] + rank
    ei = lax.broadcasted_iota(jnp.int32, pos_e.shape, 0)
    eidx = eidx_ref[...]
    rows = [jnp.sum(jnp.where(ei == eidx[k:k + 1], pos_e, 0.0), axis=0, keepdims=True) for k in range(TOP_K)]
    pos_ref[...] = jnp.concatenate(rows, axis=0).astype(jnp.int32)
    carry_ref[...] += jnp.sum(sel.astype(F32), axis=1, keepdims=True)


def _positions(sel, eidx, base):
    n = sel.shape[1]
    pb = POS_TB
    upper = jnp.asarray(np.triu(np.ones((pb, pb), np.float32), 1), dtype=BF16)
    return pl.pallas_call(
        _positions_kernel,
        grid=(n // pb,),
        in_specs=[pl.BlockSpec((N_EXPERTS, pb), lambda i: (0, i)),
                  pl.BlockSpec((TOP_K, pb), lambda i: (0, i)),
                  pl.BlockSpec((N_EXPERTS, LANES), lambda i: (0, 0)),
                  pl.BlockSpec((pb, pb), lambda i: (0, 0))],
        out_specs=pl.BlockSpec((TOP_K, pb), lambda i: (0, i)),
        out_shape=jax.ShapeDtypeStruct((TOP_K, n), jnp.int32),
        scratch_shapes=[pltpu.VMEM((N_EXPERTS, LANES), F32)],
        compiler_params=_cparams(1),
        name="positions",
    )(sel, eidx, base, upper)


def _pack_pairs(x):
    half = x.shape[1] // 2
    bits = lax.bitcast_convert_type(_bf(x).astype(F32), jnp.uint32)
    return (bits[:, :half] >> 16) | (bits[:, half:] & jnp.uint32(0xFFFF0000))


def _unpack_pairs(w):
    lo = lax.bitcast_convert_type(w << 16, F32)
    hi = lax.bitcast_convert_type(w & jnp.uint32(0xFFFF0000), F32)
    return jnp.concatenate([_bf(lo), _bf(hi)], axis=1)


def _sc_gather(table, idx):
    b, w = idx.shape[0], table.shape[1]
    n_workers = SC_CORES * SC_SUBCORES
    per_w = b // n_workers
    assert b % (n_workers * SC_CHUNK) == 0
    mesh = plsc.VectorSubcoreMesh(core_axis_name="c", subcore_axis_name="s")

    @functools.partial(
        pl.kernel, mesh=mesh, out_type=jax.ShapeDtypeStruct((b, w), table.dtype),
        scratch_types=[pltpu.VMEM((SC_CHUNK,), jnp.int32), pltpu.VMEM((SC_CHUNK, w), table.dtype),
                       pltpu.SemaphoreType.DMA])
    def gather(table_hbm, idx_hbm, out_hbm, idx_v, rows_v, sem):
        wid = lax.axis_index("s") * SC_CORES + lax.axis_index("c")
        base = wid * per_w

        @pl.loop(0, per_w // SC_CHUNK)
        def _(c):
            off = pl.multiple_of(base + c * SC_CHUNK, SC_CHUNK)
            pltpu.sync_copy(idx_hbm.at[pl.ds(off, SC_CHUNK)], idx_v)
            pltpu.async_copy(table_hbm.at[idx_v], rows_v, sem).wait()
            pltpu.sync_copy(rows_v, out_hbm.at[pl.ds(off, SC_CHUNK)])

    return gather(table, idx)


def _sc_scatter(src, pos3, n_rows):
    n, w = src.shape
    n_workers = SC_CORES * SC_SUBCORES
    per_w = n // n_workers
    assert n % (n_workers * SC_CHUNK) == 0
    mesh = plsc.VectorSubcoreMesh(core_axis_name="c", subcore_axis_name="s")

    @functools.partial(
        pl.kernel, mesh=mesh, out_type=jax.ShapeDtypeStruct((n_rows, w), src.dtype),
        scratch_types=[pltpu.VMEM((TOP_K, SC_CHUNK), jnp.int32), pltpu.VMEM((SC_CHUNK, w), src.dtype),
                       pltpu.SemaphoreType.DMA])
    def scatter(src_hbm, pos_hbm, out_hbm, idx_v, rows_v, sem):
        wid = lax.axis_index("s") * SC_CORES + lax.axis_index("c")
        base = wid * per_w

        @pl.loop(0, per_w // SC_CHUNK)
        def _(c):
            off = pl.multiple_of(base + c * SC_CHUNK, SC_CHUNK)
            pltpu.sync_copy(src_hbm.at[pl.ds(off, SC_CHUNK)], rows_v)
            pltpu.sync_copy(pos_hbm.at[off // SC_CHUNK], idx_v)
            copies = [pltpu.async_copy(rows_v, out_hbm.at[idx_v.at[k]], sem) for k in range(TOP_K)]
            for cp in copies:
                cp.wait()

    return scatter(src, pos3)


def _experts_kernel(te_ref, nu_ref, xs_ref, wg_ref, wu_ref, wd_ref, ys_ref):
    @pl.when(pl.program_id(0) < nu_ref[0])
    def _():
        x = _unpack_pairs(xs_ref[...])
        act = _glu(x, _bf(wg_ref[0]), _bf(wu_ref[0]))
        ys_ref[...] = _pack_pairs(jnp.dot(_bf(act), _bf(wd_ref[0]), preferred_element_type=F32))


def _experts(xs, tile_expert, n_used, mp):
    n_tiles = xs.shape[0] // MOE_TILE
    half = D_MODEL // 2
    wspec = lambda shape: pl.BlockSpec((1,) + shape, lambda i, te, nu: (te[i], 0, 0))
    return pl.pallas_call(
        _experts_kernel,
        grid_spec=pltpu.PrefetchScalarGridSpec(
            num_scalar_prefetch=2, grid=(n_tiles,),
            in_specs=[pl.BlockSpec((MOE_TILE, half), lambda i, te, nu: (i, 0)),
                      wspec((D_MODEL, D_EXPERT)), wspec((D_MODEL, D_EXPERT)), wspec((D_EXPERT, D_MODEL))],
            out_specs=pl.BlockSpec((MOE_TILE, half), lambda i, te, nu: (i, 0))),
        out_shape=jax.ShapeDtypeStruct(xs.shape, jnp.uint32),
        compiler_params=_cparams(1),
        name="experts",
    )(tile_expert, n_used, xs, mp['wg'], mp['wu'], mp['wd'])


def _combine_kernel(h_ref, *refs):
    yg_refs = refs[:TOP_K]
    ew_ref, eye_ref, sg_ref, su_ref, sd_ref, x1_ref, g2_ref, o_ref = refs[TOP_K:]
    act = _glu(h_ref[...], sg_ref[...], su_ref[...])
    acc = jnp.dot(_bf(act), sd_ref[...], preferred_element_type=F32)
    ew = ew_ref[...]
    hi = _bf(ew)
    lo = _bf(ew - hi.astype(F32))
    ew_t = _dot_tn(hi, eye_ref[...]) + _dot_tn(lo, eye_ref[...])
    for k in range(TOP_K):
        acc = acc + ew_t[:, k:k + 1] * _unpack_pairs(yg_refs[k][...]).astype(F32)
    o_ref[...] = x1_ref[...] + g2_ref[...] * acc


def _combine(rows, hffn, yg, ew, mp, x1, mod4, layer, row0, n_out):
    half = D_MODEL // 2
    n_blk = rows.n // TM
    blk0 = row0 // TM
    const = lambda a: pl.BlockSpec(a.shape, lambda i: (0,) * a.ndim)
    tok = lambda w: pl.BlockSpec((TM, w), lambda i: (blk0 + i, 0))
    slot = lambda k: pl.BlockSpec((TM, half), lambda i: (k * n_blk + blk0 + i, 0))
    eye = jnp.eye(TOP_K, dtype=BF16)
    return pl.pallas_call(
        _combine_kernel,
        grid=(n_out // TM,),
        in_specs=[tok(D_MODEL)] + [slot(k) for k in range(TOP_K)]
                 + [pl.BlockSpec((TOP_K, TM), lambda i: (0, blk0 + i)), const(eye),
                    const(mp['sg']), const(mp['su']), const(mp['sd']), tok(D_MODEL),
                    rows.mod_spec(layer, 5, TM, blk0)],
        out_specs=pl.BlockSpec((TM, D_MODEL), lambda i: (i, 0)),
        out_shape=jax.ShapeDtypeStruct((n_out, D_MODEL), F32),
        compiler_params=_cparams(1),
        name=f"combine{layer}",
    )(hffn, *([yg] * TOP_K), ew, eye, mp['sg'], mp['su'], mp['sd'], x1, mod4)


def _moe(rows, hffn, hpack, x1, router, bias, mp, mod4, layer, out_ranges):
    n = rows.n
    sel, eidx, ew, cnt = _router(hffn, router, bias)
    counts = cnt[:, 0].astype(jnp.int32)
    padded = (counts + MOE_TILE - 1) // MOE_TILE * MOE_TILE
    ends = jnp.cumsum(padded)
    n_rows = n * TOP_K + N_EXPERTS * MOE_TILE
    n_tiles = n_rows // MOE_TILE
    base = jnp.broadcast_to((ends - padded).astype(F32)[:, None], (N_EXPERTS, LANES))
    tile_start = jnp.arange(n_tiles, dtype=jnp.int32) * MOE_TILE
    tile_expert = jnp.minimum(jnp.sum((ends[None, :] <= tile_start[:, None]).astype(jnp.int32), axis=1),
                              N_EXPERTS - 1)
    n_used = (ends[-1:] // MOE_TILE).astype(jnp.int32)
    pos = _positions(sel, eidx, base)
    pos3 = pos.reshape(TOP_K, n // SC_CHUNK, SC_CHUNK).transpose(1, 0, 2)
    xs = _sc_scatter(hpack, pos3, n_rows)
    ys = _experts(xs, tile_expert, n_used, mp)
    yg = _sc_gather(ys, pos.reshape(-1))
    return [_combine(rows, hffn, yg, ew, mp, x1, mod4, layer, row0, n_out) for row0, n_out in out_ranges]


def _glu(x, wg, wu):
    hg = jnp.dot(x, wg, preferred_element_type=F32)
    hu = jnp.dot(x, wu, preferred_element_type=F32)
    return _silu(hg) * hu


def kernel(x_prompt, x_sample, c, c_ctx, cache_attn_k, cache_attn_v, state_rwkv_fwd, state_rwkv_bwd,
           state_hgrn_fwd, state_hgrn_bwd, norm1_g, norm2_g, mod_w, mod_b, ab_w_in, ab_w_out, attn_q_norm,
           attn_k_norm, attn_sink, rwkv_mu, rwkv_w0, rwkv_w2, rwkv_a0, rwkv_a2, rwkv_g2, rwkv_k_k, rwkv_k_a,
           rwkv_r_k, rwkv_ln_w, rwkv_ln_b, hgrn_w_in, hgrn_w_out, hgrn_lower_bounds, hgrn_norm_g, moe_router,
           moe_bias, moe_w_gate, moe_w_up, moe_w_down, moe_shared_gate, moe_shared_up, moe_shared_down):
    n_cseq, cseq, _ = x_prompt.shape
    n_lseq, lseq, _ = x_sample.shape
    depth = mod_w.shape[0]
    assert depth == 2 and n_lseq + 1 <= SUBLANES
    assert cseq == TM and lseq % TM == 0 and lseq % HG_TB == 0 and cseq % HG_TB == 0
    assert n_cseq % RW_BB == 0 and n_lseq % RW_BB == 0 and (n_cseq * cseq) % (lseq * RW_BB) == 0
    rows = _Rows(n_cseq * cseq, n_lseq * lseq, lseq)
    assert rows.n % MOE_TILE == 0 and lseq % MOE_TILE == 0 and rows.n_ctx % MOE_TILE == 0
    kv_w = KV_A * HD_A

    xs = (x_prompt.reshape(rows.n_ctx, D_MODEL), x_sample.reshape(rows.n_lat, D_MODEL))
    cvecs = jnp.concatenate([c_ctx[None, :], c, jnp.zeros((SUBLANES - 1 - n_lseq, D_MODEL), F32)], axis=0)
    mod4 = _modulation(cvecs, mod_w, mod_b).reshape(depth, SUBLANES, 1, 6 * D_MODEL)

    ones_q = _block_ones(W_A, HD_A)
    ones_k = _block_ones(kv_w, HD_A)
    ones_b = _block_ones(W_B, HD_B)
    ones_pair = _block_ones(LANES, HD_B)[:LANES]
    ones_pair = jnp.kron(jnp.eye(2, dtype=BF16), ones_pair)
    cos_t, sin_t = _rope_tables(lseq)

    def moe(l, hffn, hpack, x1, out_ranges):
        mp = {'wg': moe_w_gate[l], 'wu': moe_w_up[l], 'wd': moe_w_down[l],
              'sg': _bf(moe_shared_gate[l]), 'su': _bf(moe_shared_up[l]), 'sd': _bf(moe_shared_down[l])}
        return _moe(rows, hffn, hpack, x1, moe_router[l], moe_bias[l], mp, mod4, l, out_ranges)

    assert W_A == W_B
    all_rows = jnp.zeros((rows.n, W_B), F32)

    pr = {'mu': rwkv_mu[0], 'w0': rwkv_w0[0], 'w2': rwkv_w2[0], 'a0': rwkv_a0[0], 'a2': rwkv_a2[0],
          'g2': rwkv_g2[0], 'k_k': rwkv_k_k[0], 'k_a': rwkv_k_a[0], 'r_k': rwkv_r_k[0].reshape(-1),
          'ln_w': rwkv_ln_w[0], 'ln_b': rwkv_ln_b[0]}
    p_att, p_rw = _inproj(rows, xs, norm1_g[0], mod4, 0, _bf(ab_w_in[0]), (ATT_IN, RWKV_IN))
    qg_t = jnp.tile(attn_q_norm[0], H_A).reshape(1, W_A)
    kg_t = jnp.tile(attn_k_norm[0], KV_A).reshape(1, kv_w)
    o_att, new_k, new_v = _ctx_attention(p_att, n_cseq, cseq, qg_t, kg_t, attn_sink[0], ones_q, ones_k, all_rows)
    past = cache_attn_k.shape[2]
    o_att = _lat_attention(p_att, rows.n_ctx // lseq, n_lseq, lseq, qg_t, kg_t, attn_sink[0], ones_q, ones_k,
                           cos_t, sin_t, cache_attn_k[:, 0].reshape(n_lseq, past, kv_w),
                           cache_attn_v[:, 0].reshape(n_lseq, past, kv_w), o_att)

    pp = _rwkv_prep(rows, p_rw, pr, ones_b)
    zero_st = jnp.zeros((n_cseq, H_B // 2, HD_B, LANES), F32)
    o_f, o_b, sf_c, sb_c = _rwkv_scan(pp, 0, n_cseq, cseq, zero_st, zero_st, ones_pair, all_rows, all_rows)
    o_f, o_b, _, _ = _rwkv_scan(pp, rows.n_ctx, n_lseq, lseq, _state_to_pairs(state_rwkv_fwd[:, 0]),
                                _state_to_pairs(state_rwkv_bwd[:, 0]), ones_pair, o_f, o_b)
    x1, hffn, hpack = _outproj0(rows, xs, o_att, o_f, o_b, pp, pr, ones_b, _bf(ab_w_out[0]), norm2_g[0], mod4, 0)
    (x,) = moe(0, hffn, hpack, x1, [(0, rows.n)])

    (p1,) = _inproj(rows, (x, x), norm1_g[1], mod4, 1, _bf(hgrn_w_in[0]), (IN_C,))
    zero_h = jnp.zeros((n_cseq, H_C, DK_C, DV_C), F32)
    all_rows_c = jnp.zeros((rows.n, D_C), F32)
    h_f, h_b, hsf_c, hsb_c = _hgrn_scan(p1, 0, n_cseq, cseq, hgrn_lower_bounds, zero_h, zero_h,
                                        all_rows_c, all_rows_c)
    h_f, h_b, _, _ = _hgrn_scan(p1, rows.n_ctx, n_lseq, lseq, hgrn_lower_bounds,
                                state_hgrn_fwd[:, 0], state_hgrn_bwd[:, 0], h_f, h_b)
    x1, hffn, hpack = _outproj1(rows, (x, x), h_f, h_b, p1, hgrn_norm_g[0], _bf(hgrn_w_out[0]), norm2_g[1],
                                mod4, 1)
    y_c, y_l = moe(1, hffn, hpack, x1, [(0, rows.n_ctx), (rows.n_ctx, rows.n_lat)])

    y_prompt = y_c.reshape(n_cseq, cseq, D_MODEL)
    y_sample = y_l.reshape(n_lseq, lseq, D_MODEL)
    return (y_prompt, y_sample,
            new_k.reshape(n_cseq, 1, cseq, KV_A, HD_A), new_v.reshape(n_cseq, 1, cseq, KV_A, HD_A),
            _pairs_to_state(sf_c)[:, None], _pairs_to_state(sb_c)[:, None],
            hsf_c[:, None], hsb_c[:, None])
```

```python
import functools

import numpy as np
import jax
import jax.numpy as jnp
from jax import lax
from jax.experimental import pallas as pl
from jax.experimental.pallas import tpu as pltpu
from jax.experimental.pallas import tpu_sc as plsc

F32 = jnp.float32
BF16 = jnp.bfloat16

D_MODEL = 1024
GRID_W = 64
H_A = 8
KV_A = 2
G_A = H_A // KV_A
HD_A = 64
W_A = H_A * HD_A
WINDOW = 128
QBLK = 128
ROPE_BASE = 10000.0
ATTN_SCALE = HD_A ** -0.5
NEG_INF = -1e30
H_B = 8
HD_B = 64
W_B = H_B * HD_B
LORA_W = 64
LORA_A = 64
LORA_G = 128
GN_EPS = 64e-5
ATT_IN = W_A + 2 * KV_A * HD_A
RWKV_IN = 3 * W_B + LORA_W + LORA_A + LORA_G
IN_AB = ATT_IN + RWKV_IN
H_C = 8
DK_C = 128
DV_C = 128
D_C = H_C * DV_C
CHUNK = 64
IN_C = 5 * D_C
N_EXPERTS = 64
TOP_K = 8
N_GROUPS = 8
TOPK_GROUPS = 4
D_EXPERT = 256
ROUTED_SCALE = 2.5
EPS = 1e-6

LANES = 128
SUBLANES = 8
VMEM_LIMIT = 52 * 1024 * 1024

TM = 256
RW_TB = 128
RW_BB = 4
RW_GROUP_BB = 4
HG_TB = 256
MOE_TILE = 512
POS_TB = 512
SC_CORES = 2
SC_SUBCORES = 16
SC_CHUNK = 64


def _cparams(n_axes):
    return pltpu.CompilerParams(dimension_semantics=("arbitrary",) * n_axes,
                                vmem_limit_bytes=VMEM_LIMIT)


def _bf(x):
    return x.astype(BF16)


def _split2(x):
    hi = lax.bitcast_convert_type(
        lax.bitcast_convert_type(x, jnp.uint32) & jnp.uint32(0xFFFF0000), F32)
    return hi, x - hi


def _seg_sum(x, ones2):
    hi, lo = _split2(x)
    return jnp.dot(jnp.concatenate([_bf(hi), _bf(lo)], axis=1), ones2,
                   preferred_element_type=F32)


def _dot_nt(a, b):
    return lax.dot_general(a, b, (((1,), (1,)), ((), ())), preferred_element_type=F32)


def _dot_tn(a, b):
    return lax.dot_general(a, b, (((0,), (0,)), ((), ())), preferred_element_type=F32)


def _sigmoid(x):
    return 1.0 / (1.0 + jnp.exp(-x))


def _silu(x):
    return x * _sigmoid(x)


def _chunks(seq, n):
    seq = list(seq)
    return [seq[i:i + n] for i in range(0, len(seq), n)]


def _block_ones(width, seg):
    idx = np.arange(width) // seg
    bd = (idx[:, None] == idx[None, :]).astype(np.float32)
    return jnp.asarray(np.concatenate([bd, bd], axis=0), dtype=BF16)


def _mod_kernel(c_ref, w_ref, b_ref, o_ref):
    s = _silu(c_ref[...])
    o_ref[0] = jnp.dot(_bf(s), _bf(w_ref[0]), preferred_element_type=F32) + b_ref[0]


def _modulation(cvecs, mod_w, mod_b):
    depth = mod_w.shape[0]
    n_col = 6 * D_MODEL // D_MODEL
    return pl.pallas_call(
        _mod_kernel,
        grid=(depth, n_col),
        in_specs=[pl.BlockSpec((SUBLANES, D_MODEL), lambda l, j: (0, 0)),
                  pl.BlockSpec((1, D_MODEL, D_MODEL), lambda l, j: (l, 0, j)),
                  pl.BlockSpec((1, 1, D_MODEL), lambda l, j: (l, 0, j))],
        out_specs=pl.BlockSpec((1, SUBLANES, D_MODEL), lambda l, j: (l, 0, j)),
        out_shape=jax.ShapeDtypeStruct((depth, SUBLANES, 6 * D_MODEL), F32),
        compiler_params=_cparams(2),
        name="modulation",
    )(cvecs, mod_w, mod_b.reshape(depth, 1, 6 * D_MODEL))


class _Rows:
    def __init__(self, n_ctx, n_lat, lat_seq):
        self.n_ctx, self.n_lat, self.lat_seq = n_ctx, n_lat, lat_seq
        self.n = n_ctx + n_lat

    def mod_row(self, i, tm):
        nctx_blk = self.n_ctx // tm
        per_seq = self.lat_seq // tm
        return jnp.where(i < nctx_blk, 0, 1 + (i - nctx_blk) // per_seq)

    def mod_spec(self, layer, chunk, tm, blk0=0):
        return pl.BlockSpec((None, None, 1, D_MODEL),
                            lambda i, *_: (layer, self.mod_row(i + blk0, tm), 0, chunk))


def _rms_mod(x, g, sc, sh):
    ms = jnp.mean(x * x, axis=-1, keepdims=True)
    return x * lax.rsqrt(ms + EPS) * g * (1.0 + sc) + sh


def _x_specs(rows, xs):
    xa, xb = xs
    nctx_blk = rows.n_ctx // TM
    lat0 = nctx_blk if xb.shape[0] == rows.n else 0
    return [pl.BlockSpec((TM, D_MODEL), lambda i: (jnp.minimum(i, nctx_blk - 1), 0)),
            pl.BlockSpec((TM, D_MODEL), lambda i: (jnp.maximum(i - nctx_blk, 0) + lat0, 0))]


def _pick_x(rows, xa_ref, xb_ref):
    return jnp.where(pl.program_id(0) < rows.n_ctx // TM, xa_ref[...], xb_ref[...])


def _inproj_kernel(rows, splits, xa_ref, xb_ref, g_ref, sh_ref, sc_ref, w_ref, *o_refs):
    h = _rms_mod(_pick_x(rows, xa_ref, xb_ref), g_ref[...], sc_ref[...], sh_ref[...])
    p = jnp.dot(_bf(h), w_ref[...], preferred_element_type=F32)
    lo = 0
    for o_ref, width in zip(o_refs, splits):
        o_ref[...] = p[:, lo:lo + width]
        lo += width


def _inproj(rows, xs, g, mod4, layer, w_bf, splits):
    n_out = w_bf.shape[1]
    return pl.pallas_call(
        functools.partial(_inproj_kernel, rows, splits),
        grid=(rows.n // TM,),
        in_specs=_x_specs(rows, xs) + [
            pl.BlockSpec((1, D_MODEL), lambda i: (0, 0)),
            rows.mod_spec(layer, 0, TM),
            rows.mod_spec(layer, 1, TM),
            pl.BlockSpec((D_MODEL, n_out), lambda i: (0, 0))],
        out_specs=[pl.BlockSpec((TM, wd), lambda i: (i, 0)) for wd in splits],
        out_shape=[jax.ShapeDtypeStruct((rows.n, wd), F32) for wd in splits],
        compiler_params=_cparams(1),
        name=f"inproj{layer}",
    )(*xs, g.reshape(1, D_MODEL), mod4, mod4, w_bf)


def _head_rms(x, gain_t, ones2):
    ms = _seg_sum(x * x, ones2) * (1.0 / HD_A)
    return x * lax.rsqrt(ms + EPS) * gain_t


def _sink_softmax_pv(parts, sink):
    m = jnp.maximum(functools.reduce(jnp.maximum, [jnp.max(s, axis=-1, keepdims=True) for s, _ in parts]), sink)
    den = jnp.exp(sink - m)
    acc = None
    for s, v in parts:
        p = jnp.exp(s - m)
        den = den + jnp.sum(p, axis=-1, keepdims=True)
        pv = jnp.dot(_bf(p), v, preferred_element_type=F32)
        acc = pv if acc is None else acc + pv
    return acc / den


def _ctx_attn_kernel(p_ref, qg_ref, kg_ref, sink_ref, ones_q_ref, ones_k_ref, prev_ref, o_ref, k_ref, v_ref):
    del prev_ref
    p = p_ref[...]
    q = _head_rms(p[:, :W_A], qg_ref[...], ones_q_ref[...]) * ATTN_SCALE
    k = _head_rms(p[:, W_A:W_A + KV_A * HD_A], kg_ref[...], ones_k_ref[...])
    v = p[:, W_A + KV_A * HD_A:ATT_IN]
    k_ref[0] = k
    v_ref[0] = v
    qb, kb, vb = _bf(q), _bf(k), _bf(v)
    outs = []
    for h in range(H_A):
        j = h // G_A
        s = _dot_nt(qb[:, h * HD_A:(h + 1) * HD_A], kb[:, j * HD_A:(j + 1) * HD_A])
        outs.append(_sink_softmax_pv([(s, vb[:, j * HD_A:(j + 1) * HD_A])], sink_ref[h]))
    o_ref[...] = jnp.concatenate(outs, axis=1)


def _ctx_attention(p_att, n_seq, seq, qg_t, kg_t, sink, ones_q, ones_k, prev):
    kv_w = KV_A * HD_A
    return pl.pallas_call(
        _ctx_attn_kernel,
        grid=(n_seq,),
        in_specs=[pl.BlockSpec((seq, ATT_IN), lambda b: (b, 0)),
                  pl.BlockSpec((1, W_A), lambda b: (0, 0)),
                  pl.BlockSpec((1, kv_w), lambda b: (0, 0)),
                  pl.BlockSpec(memory_space=pltpu.SMEM),
                  pl.BlockSpec(ones_q.shape, lambda b: (0, 0)),
                  pl.BlockSpec(ones_k.shape, lambda b: (0, 0)),
                  pl.BlockSpec(memory_space=pl.ANY)],
        out_specs=[pl.BlockSpec((seq, W_A), lambda b: (b, 0)),
                   pl.BlockSpec((1, seq, kv_w), lambda b: (b, 0, 0)),
                   pl.BlockSpec((1, seq, kv_w), lambda b: (b, 0, 0))],
        input_output_aliases={6: 0},
        out_shape=[jax.ShapeDtypeStruct(prev.shape, F32),
                   jax.ShapeDtypeStruct((n_seq, seq, kv_w), F32),
                   jax.ShapeDtypeStruct((n_seq, seq, kv_w), F32)],
        compiler_params=_cparams(1),
        name="ctx_attention",
    )(p_att, qg_t, kg_t, sink, ones_q, ones_k, prev)


def _rope(x, cos_t, sin_t):
    lane = lax.broadcasted_iota(jnp.int32, cos_t.shape, 1)
    low = (lane % 32) < 16
    outs = []
    for s in range(x.shape[1] // LANES):
        xs = x[:, s * LANES:(s + 1) * LANES]
        partner = jnp.where(low, pltpu.roll(xs, LANES - 16, 1), pltpu.roll(xs, 16, 1))
        outs.append(xs * cos_t + partner * sin_t)
    return outs[0] if len(outs) == 1 else jnp.concatenate(outs, axis=1)


def _lat_attn_kernel(seq, p_ref, qg_ref, kg_ref, sink_ref, ones_q_ref, ones_k_ref, cos_ref, sin_ref,
                     kc_ref, vc_ref, prev_ref, o_ref, q_scr, k_scr, v_scr):
    del prev_ref
    kv_w = KV_A * HD_A
    p = p_ref[...]
    q = _head_rms(p[:, :W_A], qg_ref[...], ones_q_ref[...])
    k = _head_rms(p[:, W_A:W_A + kv_w], kg_ref[...], ones_k_ref[...])
    q_scr[...] = _bf(_rope(q, cos_ref[...], sin_ref[...]) * ATTN_SCALE)
    k_scr[...] = _bf(_rope(k, cos_ref[...], sin_ref[...]))
    v_scr[...] = _bf(p[:, W_A + kv_w:ATT_IN])
    kc = _bf(kc_ref[0])
    vc = _bf(vc_ref[0])
    n_local = 3 * QBLK

    def block(i, carry):
        q0 = pl.multiple_of(i * QBLK, QBLK)
        start = pl.multiple_of(jnp.clip((i - 1) * QBLK, 0, seq - n_local), QBLK)
        qb = q_scr[pl.ds(q0, QBLK), :]
        kl = k_scr[pl.ds(start, n_local), :]
        vl = v_scr[pl.ds(start, n_local), :]
        ipos = q0 + lax.broadcasted_iota(jnp.int32, (QBLK, n_local), 0)
        jpos = start + lax.broadcasted_iota(jnp.int32, (QBLK, n_local), 1)
        band = jnp.abs(jpos - ipos) <= WINDOW
        outs = []
        for h in range(H_A):
            j = h // G_A
            qh = qb[:, h * HD_A:(h + 1) * HD_A]
            s_loc = jnp.where(band, _dot_nt(qh, kl[:, j * HD_A:(j + 1) * HD_A]), NEG_INF)
            s_ctx = _dot_nt(qh, kc[:, j * HD_A:(j + 1) * HD_A])
            outs.append(_sink_softmax_pv([(s_loc, vl[:, j * HD_A:(j + 1) * HD_A]),
                                          (s_ctx, vc[:, j * HD_A:(j + 1) * HD_A])], sink_ref[h]))
        o_ref[pl.ds(q0, QBLK), :] = jnp.concatenate(outs, axis=1)
        return carry

    lax.fori_loop(0, seq // QBLK, block, 0)


def _lat_attention(p_att, row_blk0, n_seq, seq, qg_t, kg_t, sink, ones_q, ones_k, cos_t, sin_t, kc, vc, prev):
    kv_w = KV_A * HD_A
    past = kc.shape[1]
    return pl.pallas_call(
        functools.partial(_lat_attn_kernel, seq),
        grid=(n_seq,),
        in_specs=[pl.BlockSpec((seq, ATT_IN), lambda b: (row_blk0 + b, 0)),
                  pl.BlockSpec((1, W_A), lambda b: (0, 0)),
                  pl.BlockSpec((1, kv_w), lambda b: (0, 0)),
                  pl.BlockSpec(memory_space=pltpu.SMEM),
                  pl.BlockSpec(ones_q.shape, lambda b: (0, 0)),
                  pl.BlockSpec(ones_k.shape, lambda b: (0, 0)),
                  pl.BlockSpec((seq, LANES), lambda b: (0, 0)),
                  pl.BlockSpec((seq, LANES), lambda b: (0, 0)),
                  pl.BlockSpec((1, past, kv_w), lambda b: (b, 0, 0)),
                  pl.BlockSpec((1, past, kv_w), lambda b: (b, 0, 0)),
                  pl.BlockSpec(memory_space=pl.ANY)],
        out_specs=pl.BlockSpec((seq, W_A), lambda b: (row_blk0 + b, 0)),
        out_shape=jax.ShapeDtypeStruct(prev.shape, F32),
        input_output_aliases={10: 0},
        scratch_shapes=[pltpu.VMEM((seq, W_A), BF16), pltpu.VMEM((seq, kv_w), BF16),
                        pltpu.VMEM((seq, kv_w), BF16)],
        compiler_params=_cparams(1),
        name="lat_attention",
    )(p_att, qg_t, kg_t, sink, ones_q, ones_k, cos_t, sin_t, kc, vc, prev)


def _rope_tables(seq):
    pos = np.arange(seq)
    row = (pos // GRID_W).astype(np.float32)
    col = (pos % GRID_W).astype(np.float32)
    d_axis = HD_A // 2
    inv = (ROPE_BASE ** (-np.arange(0, d_axis, 2, dtype=np.float32) / d_axis)).astype(np.float32)
    cos_h = np.zeros((seq, HD_A), np.float32)
    sin_h = np.zeros((seq, HD_A), np.float32)
    for seg, p_ in enumerate((row, col)):
        ang = (p_[:, None] * inv[None, :]).astype(np.float32)
        c, s = np.cos(ang), np.sin(ang)
        base = seg * d_axis
        cos_h[:, base:base + d_axis // 2] = c
        cos_h[:, base + d_axis // 2:base + d_axis] = c
        sin_h[:, base:base + d_axis // 2] = -s
        sin_h[:, base + d_axis // 2:base + d_axis] = s
    rep = LANES // HD_A
    return jnp.asarray(np.tile(cos_h, (1, rep))), jnp.asarray(np.tile(sin_h, (1, rep)))


def _rwkv_prep_kernel(rows, x_ref, prev_ref, next_ref, mu_ref, kk_ref, ka_ref, rk_ref, w0_ref, w2_ref,
                      a0_ref, a2_ref, g2_ref, ones_ref,
                      nkk_ref, r_ref, v_ref, g_ref, bonus_ref,
                      wf_ref, kaf_ref, kdf_ref, wb_ref, kab_ref, kdb_ref):
    i = pl.program_id(0)
    nctx_blk = rows.n_ctx // TM
    per_seq = rows.lat_seq // TM
    is_ctx = i < nctx_blk
    first = jnp.logical_or(is_ctx, (i - nctx_blk) % per_seq == 0)
    last = jnp.logical_or(is_ctx, (i - nctx_blk) % per_seq == per_seq - 1)
    x = x_ref[...]
    ridx = lax.broadcasted_iota(jnp.int32, x.shape, 0)
    prev_row = jnp.where(first, 0.0, prev_ref[SUBLANES - 1:SUBLANES, :])
    next_row = jnp.where(last, 0.0, next_ref[0:1, :])
    xm1 = jnp.where(ridx == 0, prev_row, pltpu.roll(x, 1, 0))
    xp1 = jnp.where(ridx == TM - 1, next_row, pltpu.roll(x, TM - 1, 0))
    pw = x + (0.5 * (xm1 + xp1) - x) * mu_ref[...]

    r = pw[:, 0:W_B]
    k = pw[:, W_B:2 * W_B]
    v = pw[:, 2 * W_B:3 * W_B]
    wd = pw[:, 3 * W_B:3 * W_B + LORA_W]
    ad = pw[:, 3 * W_B + LORA_W:3 * W_B + LORA_W + LORA_A]
    gd = pw[:, 3 * W_B + LORA_W + LORA_A:]
    ones2 = ones_ref[...]

    kk = k * kk_ref[...]
    kk = kk / jnp.maximum(jnp.sqrt(_seg_sum(kk * kk, ones2)), 1e-12)
    nkk_ref[...] = -kk
    r_ref[...] = r
    v_ref[...] = v
    g_ref[...] = jnp.dot(_bf(_sigmoid(gd)), g2_ref[...], preferred_element_type=F32)
    tw = _bf(jnp.tanh(wd))
    adb = _bf(ad)
    bonus = jnp.zeros_like(r)
    for d, (w_o, ka_o, kd_o) in enumerate(((wf_ref, kaf_ref, kdf_ref), (wb_ref, kab_ref, kdb_ref))):
        z = -(w0_ref[d:d + 1, :] + jnp.dot(tw, w2_ref[d], preferred_element_type=F32))
        softplus = jnp.maximum(z, 0.0) + jnp.log(1.0 + jnp.exp(-jnp.abs(z)))
        w_o[...] = jnp.exp(-jnp.exp(-softplus - 0.5))
        a = _sigmoid(a0_ref[d:d + 1, :] + jnp.dot(adb, a2_ref[d], preferred_element_type=F32))
        kd = k * (1.0 + (a - 1.0) * ka_ref[...])
        ka_o[...] = kk * a
        kd_o[...] = kd
        bonus = bonus + _seg_sum(r * kd * rk_ref[...], ones2) * v
    bonus_ref[...] = bonus


def _rwkv_prep(rows, p_rw, pr, ones_b):
    n = rows.n
    n_halo = n // SUBLANES
    blk_halo = TM // SUBLANES
    row = lambda a: a.reshape(1, -1)
    full = lambda a: pl.BlockSpec(a.shape, lambda i: (0,) * a.ndim)
    consts = [row(pr['mu']), row(pr['k_k']), row(pr['k_a']), row(pr['r_k']), pr['w0'], _bf(pr['w2']),
              pr['a0'], _bf(pr['a2']), _bf(pr['g2']), ones_b]
    outs = pl.pallas_call(
        functools.partial(_rwkv_prep_kernel, rows),
        grid=(n // TM,),
        in_specs=[pl.BlockSpec((TM, RWKV_IN), lambda i: (i, 0)),
                  pl.BlockSpec((SUBLANES, RWKV_IN), lambda i: (jnp.maximum(i * blk_halo - 1, 0), 0)),
                  pl.BlockSpec((SUBLANES, RWKV_IN), lambda i: (jnp.minimum((i + 1) * blk_halo, n_halo - 1), 0))]
                 + [full(a) for a in consts],
        out_specs=[pl.BlockSpec((TM, W_B), lambda i: (i, 0))] * 11,
        out_shape=[jax.ShapeDtypeStruct((n, W_B), F32)] * 11,
        compiler_params=_cparams(1),
        name="rwkv_prep",
    )(p_rw, p_rw, p_rw, *consts)
    names = ('nkk', 'r', 'v', 'g', 'bonus', 'w_f', 'ka_f', 'kd_f', 'w_b', 'ka_b', 'kd_b')
    return dict(zip(names, outs))


def _rwkv_scan_kernel(n_tb, nkkf_ref, rf_ref, vf_ref, wf_ref, kaf_ref, kdf_ref,
                      nkkb_ref, rb_ref, vb_ref, wb_ref, kab_ref, kdb_ref,
                      s0f_ref, s0b_ref, ones_ref, prevf_ref, prevb_ref,
                      of_ref, ob_ref, sff_ref, sfb_ref, s_scr, vt_scr):
    del prevf_ref, prevb_ref
    tb = pl.program_id(1)
    n_pair = H_B // 2
    half = RW_TB // 2
    dirs = ((nkkf_ref, rf_ref, vf_ref, wf_ref, kaf_ref, kdf_ref, of_ref, False),
            (nkkb_ref, rb_ref, vb_ref, wb_ref, kab_ref, kdb_ref, ob_ref, True))

    @pl.when(tb == 0)
    def _():
        s_scr[0] = s0f_ref[...]
        s_scr[1] = s0b_ref[...]

    lane = lax.broadcasted_iota(jnp.int32, (HD_B, LANES), 1)
    for d, refs in enumerate(dirs):
        v_ref = refs[2]
        for bb in range(RW_BB):
            for p in range(n_pair):
                vt = v_ref[bb, :, p * LANES:(p + 1) * LANES].T
                top, bot = vt[:HD_B], vt[HD_B:]
                for s in range(2):
                    if s == 0:
                        t2 = jnp.where(lane < HD_B, top, pltpu.roll(bot, HD_B, 1))
                    else:
                        t2 = jnp.where(lane < HD_B, pltpu.roll(top, HD_B, 1), bot)
                    vt_scr[d, bb, p, s] = t2

    ones2 = ones_ref[...]
    row8 = lax.broadcasted_iota(jnp.int32, (SUBLANES, LANES), 0)
    lane8 = lax.broadcasted_iota(jnp.int32, (SUBLANES, LANES), 1)
    sel_r = jnp.logical_or(jnp.logical_and(row8 % 2 == 0, lane8 < HD_B),
                           jnp.logical_and(row8 % 2 == 1, lane8 >= HD_B))

    def row_of(rev, tt):
        return RW_TB - 1 - tt if rev else tt

    def emit_output(d, bb, tau):
        r_ref, o_ref = dirs[d][1], dirs[d][6]
        r = r_ref[bb, pl.ds(tau, 1), :]
        r8 = jnp.zeros((SUBLANES, LANES), F32)
        for p in range(n_pair):
            rp = jnp.broadcast_to(r[:, p * LANES:(p + 1) * LANES], (SUBLANES, LANES))
            r8 = jnp.where(jnp.logical_and(sel_r, row8 // 2 == p), rp, r8)
        s_all = jnp.concatenate([_bf(s_scr[d, bb, p]) for p in range(n_pair)], axis=0)
        o8 = _dot_nt(_bf(r8), s_all)
        o_parts = []
        for p in range(n_pair):
            for h in range(2):
                o_parts.append(o8[2 * p + h:2 * p + h + 1, p * HD_B:(p + 1) * HD_B])
        o_ref[bb, pl.ds(tau, 1), :] = jnp.concatenate(o_parts, axis=1)

    groups = [(d, bbs) for d in range(2) for bbs in _chunks(range(RW_BB), RW_GROUP_BB)]

    def step(tt, carry):
        tt_prev = jnp.maximum(tt - 1, 0)
        reds = []
        for d, bbs in groups:
            rev = dirs[d][7]
            tau = row_of(rev, tt)
            sub = tau // half
            lt = tau % half
            mask = jnp.logical_or(lane == lt, lane == lt + HD_B)
            lhs = []
            for bb in bbs:
                emit_output(d, bb, row_of(rev, tt_prev))
                nkk = dirs[d][0][bb, pl.ds(tau, 1), :]
                for p in range(n_pair):
                    prod = s_scr[d, bb, p] * nkk[:, p * LANES:(p + 1) * LANES]
                    lhs.append(jnp.concatenate([_bf(prod), _bf(jnp.where(mask, vt_scr[d, bb, p, sub], 0.0))],
                                               axis=1))
            reds.append(jnp.dot(jnp.concatenate(lhs, axis=0), ones2, preferred_element_type=F32))
        for (d, bbs), red in zip(groups, reds):
            _, _, _, w_ref, ka_ref, kd_ref, _, rev = dirs[d]
            tau = row_of(rev, tt)
            for k, bb in enumerate(bbs):
                w = w_ref[bb, pl.ds(tau, 1), :]
                ka = ka_ref[bb, pl.ds(tau, 1), :]
                kd = kd_ref[bb, pl.ds(tau, 1), :]
                for p in range(n_pair):
                    sl = slice(p * LANES, (p + 1) * LANES)
                    r0 = (k * n_pair + p) * HD_B
                    sa = red[r0:r0 + HD_B, :LANES]
                    vcol = red[r0:r0 + HD_B, LANES:]
                    s_scr[d, bb, p] = s_scr[d, bb, p] * w[:, sl] + sa * ka[:, sl] + vcol * kd[:, sl]
        return carry

    lax.fori_loop(0, RW_TB, step, 0)
    for d in range(2):
        for bb in range(RW_BB):
            emit_output(d, bb, row_of(dirs[d][7], RW_TB - 1))

    @pl.when(tb == n_tb - 1)
    def _():
        sff_ref[...] = s_scr[0]
        sfb_ref[...] = s_scr[1]


def _rwkv_scan(pp, row0, n_seq, seq, s0_f, s0_b, ones_pair, prev_f, prev_b):
    n_tb = seq // RW_TB
    n_pair = H_B // 2
    blk0 = row0 // seq
    view = lambda a: a.reshape(a.shape[0] // seq, seq, W_B)
    fwd = pl.BlockSpec((RW_BB, RW_TB, W_B), lambda b, t: (blk0 // RW_BB + b, t, 0))
    bwd = pl.BlockSpec((RW_BB, RW_TB, W_B), lambda b, t: (blk0 // RW_BB + b, n_tb - 1 - t, 0))
    st = pl.BlockSpec((RW_BB, n_pair, HD_B, LANES), lambda b, t: (b, 0, 0, 0))
    ins_f = [view(pp[k]) for k in ('nkk', 'r', 'v', 'w_f', 'ka_f', 'kd_f')]
    ins_b = [view(pp[k]) for k in ('nkk', 'r', 'v', 'w_b', 'ka_b', 'kd_b')]
    st_shape = jax.ShapeDtypeStruct((n_seq, n_pair, HD_B, LANES), F32)
    o_shape = jax.ShapeDtypeStruct(view(prev_f).shape, F32)
    any_spec = pl.BlockSpec(memory_space=pl.ANY)
    o_f, o_b, sf, sb = pl.pallas_call(
        functools.partial(_rwkv_scan_kernel, n_tb),
        grid=(n_seq // RW_BB, n_tb),
        in_specs=[fwd] * 6 + [bwd] * 6 + [st, st, pl.BlockSpec(ones_pair.shape, lambda b, t: (0, 0)),
                                           any_spec, any_spec],
        out_specs=[fwd, bwd, st, st],
        out_shape=[o_shape, o_shape, st_shape, st_shape],
        input_output_aliases={15: 0, 16: 1},
        scratch_shapes=[pltpu.VMEM((2, RW_BB, n_pair, HD_B, LANES), F32),
                        pltpu.VMEM((2, RW_BB, n_pair, 2, HD_B, LANES), F32)],
        compiler_params=_cparams(2),
        name="rwkv_scan",
    )(*ins_f, *ins_b, s0_f, s0_b, ones_pair, view(prev_f), view(prev_b))
    return o_f.reshape(prev_f.shape), o_b.reshape(prev_b.shape), sf, sb


def _state_to_pairs(s):
    b = s.shape[0]
    return s.reshape(b, H_B // 2, 2, HD_B, HD_B).transpose(0, 1, 3, 2, 4).reshape(b, H_B // 2, HD_B, 2 * HD_B)


def _pairs_to_state(s):
    b = s.shape[0]
    return s.reshape(b, H_B // 2, HD_B, 2, HD_B).transpose(0, 1, 3, 2, 4).reshape(b, H_B, HD_B, HD_B)


def _tail(x, y, g1, n2g, sc2, sh2, x1_ref, h_ref, hp_ref):
    x1 = x + g1 * y
    x1_ref[...] = x1
    h = _rms_mod(x1, n2g, sc2, sh2)
    h_ref[...] = _bf(h)
    hp_ref[...] = _pack_pairs(h)


def _outproj0_kernel(rows, xa_ref, xb_ref, oa_ref, of_ref, ob_ref, bonus_ref, g_ref, lnw_ref, lnb_ref, ones_ref,
                     w_ref, g1_ref, n2g_ref, sc2_ref, sh2_ref, x1_ref, h_ref, hp_ref):
    o_sum = of_ref[...] + ob_ref[...]
    ones2 = ones_ref[...]
    mean = _seg_sum(o_sum, ones2) * (1.0 / HD_B)
    cen = o_sum - mean
    var = _seg_sum(cen * cen, ones2) * (1.0 / HD_B)
    gn = cen * lax.rsqrt(var + GN_EPS) * lnw_ref[...] + lnb_ref[...]
    o_rw = (gn + bonus_ref[...]) * g_ref[...]
    mix = jnp.concatenate([_bf(oa_ref[...]), _bf(o_rw)], axis=1)
    y = jnp.dot(mix, w_ref[...], preferred_element_type=F32)
    _tail(_pick_x(rows, xa_ref, xb_ref), y, g1_ref[...], n2g_ref[...], sc2_ref[...], sh2_ref[...], x1_ref, h_ref, hp_ref)


def _outproj0(rows, xs, o_att, o_f, o_b, pp, pr, ones_b, w_out_bf, n2g, mod4, layer):
    n = rows.n
    tok = lambda w: pl.BlockSpec((TM, w), lambda i: (i, 0))
    const = lambda a: pl.BlockSpec(a.shape, lambda i: (0,) * a.ndim)
    lnw, lnb, n2 = pr['ln_w'].reshape(1, -1), pr['ln_b'].reshape(1, -1), n2g.reshape(1, -1)
    return pl.pallas_call(
        functools.partial(_outproj0_kernel, rows),
        grid=(n // TM,),
        in_specs=_x_specs(rows, xs) + [tok(W_A), tok(W_B), tok(W_B), tok(W_B), tok(W_B),
                  const(lnw), const(lnb), const(ones_b), const(w_out_bf),
                  rows.mod_spec(layer, 2, TM), const(n2), rows.mod_spec(layer, 4, TM), rows.mod_spec(layer, 3, TM)],
        out_specs=[tok(D_MODEL), tok(D_MODEL), tok(D_MODEL // 2)],
        out_shape=[jax.ShapeDtypeStruct((n, D_MODEL), F32), jax.ShapeDtypeStruct((n, D_MODEL), BF16),
                   jax.ShapeDtypeStruct((n, D_MODEL // 2), jnp.uint32)],
        compiler_params=_cparams(1),
        name="outproj0",
    )(*xs, o_att, o_f, o_b, pp['bonus'], pp['g'], lnw, lnb, ones_b, w_out_bf, mod4, n2, mod4, mod4)


def _outproj1_kernel(rows, xa_ref, xb_ref, of_ref, ob_ref, gate_ref, ng_ref, w_ref, g1_ref, n2g_ref, sc2_ref, sh2_ref,
                     x1_ref, h_ref, hp_ref):
    o_sum = of_ref[...] + ob_ref[...]
    parts = []
    for h in range(H_C):
        oh = o_sum[:, h * DV_C:(h + 1) * DV_C]
        parts.append(oh * lax.rsqrt(jnp.mean(oh * oh, axis=-1, keepdims=True) + EPS))
    o = jnp.concatenate(parts, axis=1) * ng_ref[...] * _silu(gate_ref[...])
    y = jnp.dot(_bf(o), w_ref[...], preferred_element_type=F32)
    _tail(_pick_x(rows, xa_ref, xb_ref), y, g1_ref[...], n2g_ref[...], sc2_ref[...], sh2_ref[...], x1_ref, h_ref, hp_ref)


def _outproj1(rows, xs, o_f, o_b, p1, norm_g, w_out_bf, n2g, mod4, layer):
    n = rows.n
    tok = lambda w: pl.BlockSpec((TM, w), lambda i: (i, 0))
    const = lambda a: pl.BlockSpec(a.shape, lambda i: (0,) * a.ndim)
    ng, n2 = norm_g.reshape(1, -1), n2g.reshape(1, -1)
    return pl.pallas_call(
        functools.partial(_outproj1_kernel, rows),
        grid=(n // TM,),
        in_specs=_x_specs(rows, xs) + [tok(D_C), tok(D_C), pl.BlockSpec((TM, D_C), lambda i: (i, 4)),
                  const(ng), const(w_out_bf),
                  rows.mod_spec(layer, 2, TM), const(n2), rows.mod_spec(layer, 4, TM), rows.mod_spec(layer, 3, TM)],
        out_specs=[tok(D_MODEL), tok(D_MODEL), tok(D_MODEL // 2)],
        out_shape=[jax.ShapeDtypeStruct((n, D_MODEL), F32), jax.ShapeDtypeStruct((n, D_MODEL), BF16),
                   jax.ShapeDtypeStruct((n, D_MODEL // 2), jnp.uint32)],
        compiler_params=_cparams(1),
        name="outproj1",
    )(*xs, o_f, o_b, p1, ng, w_out_bf, mod4, n2, mod4, mod4)


def _hgrn_kernel(n_tb, qf_ref, ff_ref, if_ref, qb_ref, fb_ref, ib_ref, lbp_ref, s0f_ref, s0b_ref,
                 trif_ref, trib_ref, prevf_ref, prevb_ref, of_ref, ob_ref, sff_ref, sfb_ref, s_scr):
    del prevf_ref, prevb_ref
    tb = pl.program_id(1)

    @pl.when(tb == 0)
    def _():
        for h in range(H_C):
            s_scr[0, h] = s0f_ref[0, h].T
            s_scr[1, h] = s0b_ref[0, h].T

    lbp = lbp_ref[...]
    e = jnp.exp(lbp - jnp.max(lbp, axis=0, keepdims=True))
    sm = e / jnp.sum(e, axis=0, keepdims=True)
    lb = (sm[0:1] + sm[1:2]) - sm[0:1]

    n_chunk = HG_TB // CHUNK
    ti = lax.broadcasted_iota(jnp.int32, (HG_TB, HG_TB), 0)
    si = lax.broadcasted_iota(jnp.int32, (HG_TB, HG_TB), 1)
    same = (ti // CHUNK) == (si // CHUNK)
    dirs = ((qf_ref, ff_ref, if_ref, of_ref, trif_ref, jnp.logical_and(same, ti >= si), CHUNK - 1, False),
            (qb_ref, fb_ref, ib_ref, ob_ref, trib_ref, jnp.logical_and(same, ti <= si), 0, True))

    staged = []
    for d, (q_ref, f_ref, i_ref, o_ref, tri_ref, causal, last_row, rev) in enumerate(dirs):
        q = _silu(q_ref[...])
        f = lb + (1.0 - lb) * _sigmoid(f_ref[...])
        k = 1.0 - f
        v = _bf(i_ref[...])
        g = jnp.log(f)
        g1 = _bf(g)
        g2 = _bf(g - g1.astype(F32))
        tri2 = tri_ref[...]
        b_parts, last_parts, dec = [], [], []
        for c in range(n_chunk):
            rc = slice(c * CHUNK, (c + 1) * CHUNK)
            bc = jnp.dot(tri2, jnp.concatenate([g1[rc], g2[rc]], axis=0), preferred_element_type=F32)
            b_parts.append(bc)
            last = bc[last_row:last_row + 1]
            last_parts.append(jnp.broadcast_to(last, bc.shape))
            dec.append(jnp.exp(last))
        b = jnp.concatenate(b_parts, axis=0)
        b_last = jnp.concatenate(last_parts, axis=0)
        staged.append((_bf(q * jnp.exp(b)), _bf(k * jnp.exp(-b)), _bf(k * jnp.exp(b_last - b)), v, dec))

    for h in range(H_C):
        sl = slice(h * DK_C, (h + 1) * DK_C)
        for d, (q_ref, f_ref, i_ref, o_ref, tri_ref, causal, last_row, rev) in enumerate(dirs):
            q_in, k_in, k_out, v, dec = staged[d]
            qh, vh = q_in[:, sl], v[:, sl]
            att = jnp.where(causal, _dot_nt(qh, k_in[:, sl]), 0.0)
            o_intra = jnp.dot(_bf(att), vh, preferred_element_type=F32)
            s_t = s_scr[d, h]
            for c in (range(n_chunk - 1, -1, -1) if rev else range(n_chunk)):
                rc = slice(c * CHUNK, (c + 1) * CHUNK)
                o_ref[rc, sl] = o_intra[rc] + _dot_nt(qh[rc], _bf(s_t))
                s_t = dec[c][:, sl] * s_t + _dot_tn(vh[rc], k_out[rc, sl])
            s_scr[d, h] = s_t

    @pl.when(tb == n_tb - 1)
    def _():
        for h in range(H_C):
            sff_ref[0, h] = s_scr[0, h].T
            sfb_ref[0, h] = s_scr[1, h].T


def _hgrn_scan(p1, row0, n_seq, seq, lb_params, s0_f, s0_b, prev_f, prev_b):
    n_tb = seq // HG_TB
    blk0 = row0 // HG_TB
    tri = np.tril(np.ones((CHUNK, CHUNK), np.float32))
    tri_f = jnp.asarray(np.concatenate([tri] * 2, axis=1), dtype=BF16)
    tri_b = jnp.asarray(np.concatenate([tri.T] * 2, axis=1), dtype=BF16)
    fwd = lambda col: pl.BlockSpec((HG_TB, D_C), lambda b, t: (blk0 + b * n_tb + t, col))
    bwd = lambda col: pl.BlockSpec((HG_TB, D_C), lambda b, t: (blk0 + b * n_tb + n_tb - 1 - t, col))
    st = pl.BlockSpec((1, H_C, DK_C, DV_C), lambda b, t: (b, 0, 0, 0))
    const = lambda a: pl.BlockSpec(a.shape, lambda b, t: (0,) * a.ndim)
    o_shape = jax.ShapeDtypeStruct(prev_f.shape, F32)
    st_shape = jax.ShapeDtypeStruct((n_seq, H_C, DK_C, DV_C), F32)
    any_spec = pl.BlockSpec(memory_space=pl.ANY)
    return pl.pallas_call(
        functools.partial(_hgrn_kernel, n_tb),
        grid=(n_seq, n_tb),
        in_specs=[fwd(0), fwd(1), fwd(3), bwd(0), bwd(2), bwd(3), const(lb_params), st, st,
                  const(tri_f), const(tri_b), any_spec, any_spec],
        out_specs=[fwd(0), bwd(0), st, st],
        out_shape=[o_shape, o_shape, st_shape, st_shape],
        input_output_aliases={11: 0, 12: 1},
        scratch_shapes=[pltpu.VMEM((2, H_C, DV_C, DK_C), F32)],
        compiler_params=_cparams(2),
        name="hgrn_scan",
    )(p1, p1, p1, p1, p1, p1, lb_params, s0_f, s0_b, tri_f, tri_b, prev_f, prev_b)


def _router_kernel(h_ref, rhi_ref, rlo_ref, bias_ref, sel_ref, eidx_ref, ew_ref, cnt_ref):
    x = h_ref[...]
    tm = x.shape[0]
    logits = _dot_nt(rhi_ref[...], x) + _dot_nt(rlo_ref[...], x)
    scores = _sigmoid(logits)
    biased = scores + bias_ref[...]
    per = N_EXPERTS // N_GROUPS
    sub = lax.broadcasted_iota(jnp.int32, (per, tm), 0)
    gs_rows = []
    for g in range(N_GROUPS):
        blk = biased[g * per:(g + 1) * per]
        m1 = jnp.max(blk, axis=0, keepdims=True)
        first = jnp.min(jnp.where(blk == m1, sub, per), axis=0, keepdims=True)
        m2 = jnp.max(jnp.where(sub == first, -jnp.inf, blk), axis=0, keepdims=True)
        gs_rows.append(m1 + m2)
    gs = jnp.concatenate(gs_rows, axis=0)
    gi = lax.broadcasted_iota(jnp.int32, gs.shape, 0)
    rank = jnp.zeros(gs.shape, jnp.int32)
    for s in range(1, N_GROUPS):
        other = pltpu.roll(gs, s, 0)
        oi = pltpu.roll(gi, s, 0)
        beats = jnp.logical_or(other > gs, jnp.logical_and(other == gs, oi < gi))
        rank = rank + jnp.where(beats, 1, 0)
    keep = jnp.where(rank < TOPK_GROUPS, 1.0, 0.0)
    emask = jnp.concatenate([jnp.broadcast_to(keep[g:g + 1], (per, tm)) for g in range(N_GROUPS)], axis=0)
    cur = jnp.where(emask > 0.0, biased, -jnp.inf)
    ei = lax.broadcasted_iota(jnp.int32, cur.shape, 0)
    sel = jnp.zeros(cur.shape, F32)
    idxs, vals = [], []
    for _ in range(TOP_K):
        m = jnp.max(cur, axis=0, keepdims=True)
        idx = jnp.min(jnp.where(cur == m, ei, N_EXPERTS), axis=0, keepdims=True)
        pick = ei == idx
        idxs.append(idx)
        vals.append(jnp.sum(jnp.where(pick, scores, 0.0), axis=0, keepdims=True))
        sel = jnp.where(pick, 1.0, sel)
        cur = jnp.where(pick, -jnp.inf, cur)
    w = jnp.concatenate(vals, axis=0)
    eidx_ref[...] = jnp.concatenate(idxs, axis=0)
    ew_ref[...] = w / jnp.sum(w, axis=0, keepdims=True) * ROUTED_SCALE
    sel_ref[...] = _bf(sel)

    @pl.when(pl.program_id(0) == 0)
    def _():
        cnt_ref[...] = jnp.zeros_like(cnt_ref)

    cnt_ref[...] += jnp.sum(sel, axis=1, keepdims=True)


def _router(hffn, router, bias):
    n = hffn.shape[0]
    r_t = router.T
    r_hi = _bf(r_t)
    r_lo = _bf(r_t - r_hi.astype(F32))
    const = lambda a: pl.BlockSpec(a.shape, lambda i: (0,) * a.ndim)
    b_col = bias.reshape(N_EXPERTS, 1)
    return pl.pallas_call(
        _router_kernel,
        grid=(n // TM,),
        in_specs=[pl.BlockSpec((TM, D_MODEL), lambda i: (i, 0)), const(r_hi), const(r_lo), const(b_col)],
        out_specs=[pl.BlockSpec((N_EXPERTS, TM), lambda i: (0, i)),
                   pl.BlockSpec((TOP_K, TM), lambda i: (0, i)),
                   pl.BlockSpec((TOP_K, TM), lambda i: (0, i)),
                   pl.BlockSpec((N_EXPERTS, LANES), lambda i: (0, 0))],
        out_shape=[jax.ShapeDtypeStruct((N_EXPERTS, n), BF16),
                   jax.ShapeDtypeStruct((TOP_K, n), jnp.int32),
                   jax.ShapeDtypeStruct((TOP_K, n), F32),
                   jax.ShapeDtypeStruct((N_EXPERTS, LANES), F32)],
        compiler_params=_cparams(1),
        name="router",
    )(hffn, r_hi, r_lo, b_col)


def _positions_kernel(sel_ref, eidx_ref, base_ref, upper_ref, pos_ref, carry_ref):
    @pl.when(pl.program_id(0) == 0)
    def _():
        carry_ref[...] = jnp.zeros_like(carry_ref)

    sel = sel_ref[...]
    rank = jnp.dot(sel, upper_ref[...], preferred_element_type=F32)
    pos_e = base_ref[:, 0:1] + carry_ref[:, 0:1] + rank
    ei = lax.broadcasted_iota(jnp.int32, pos_e.shape, 0)
    eidx = eidx_ref[...]
    rows = [jnp.sum(jnp.where(ei == eidx[k:k + 1], pos_e, 0.0), axis=0, keepdims=True) for k in range(TOP_K)]
    pos_ref[...] = jnp.concatenate(rows, axis=0).astype(jnp.int32)
    carry_ref[...] += jnp.sum(sel.astype(F32), axis=1, keepdims=True)


def _positions(sel, eidx, base):
    n = sel.shape[1]
    pb = POS_TB
    upper = jnp.asarray(np.triu(np.ones((pb, pb), np.float32), 1), dtype=BF16)
    return pl.pallas_call(
        _positions_kernel,
        grid=(n // pb,),
        in_specs=[pl.BlockSpec((N_EXPERTS, pb), lambda i: (0, i)),
                  pl.BlockSpec((TOP_K, pb), lambda i: (0, i)),
                  pl.BlockSpec((N_EXPERTS, LANES), lambda i: (0, 0)),
                  pl.BlockSpec((pb, pb), lambda i: (0, 0))],
        out_specs=pl.BlockSpec((TOP_K, pb), lambda i: (0, i)),
        out_shape=jax.ShapeDtypeStruct((TOP_K, n), jnp.int32),
        scratch_shapes=[pltpu.VMEM((N_EXPERTS, LANES), F32)],
        compiler_params=_cparams(1),
        name="positions",
    )(sel, eidx, base, upper)


def _pack_pairs(x):
    half = x.shape[1] // 2
    bits = lax.bitcast_convert_type(_bf(x).astype(F32), jnp.uint32)
    return (bits[:, :half] >> 16) | (bits[:, half:] & jnp.uint32(0xFFFF0000))


def _unpack_pairs(w):
    lo = lax.bitcast_convert_type(w << 16, F32)
    hi = lax.bitcast_convert_type(w & jnp.uint32(0xFFFF0000), F32)
    return jnp.concatenate([_bf(lo), _bf(hi)], axis=1)


def _sc_gather(table, idx):
    b, w = idx.shape[0], table.shape[1]
    n_workers = SC_CORES * SC_SUBCORES
    per_w = b // n_workers
    assert b % (n_workers * SC_CHUNK) == 0
    mesh = plsc.VectorSubcoreMesh(core_axis_name="c", subcore_axis_name="s")

    n_chunk = per_w // SC_CHUNK
    assert n_chunk % 2 == 0

    @functools.partial(
        pl.kernel, mesh=mesh, out_type=jax.ShapeDtypeStruct((b, w), table.dtype),
        scratch_types=[pltpu.VMEM((2, SC_CHUNK), jnp.int32), pltpu.VMEM((2, SC_CHUNK, w), table.dtype),
                       pltpu.SemaphoreType.DMA((2,))])
    def gather(table_hbm, idx_hbm, out_hbm, idx_v, rows_v, sems):
        wid = lax.axis_index("s") * SC_CORES + lax.axis_index("c")
        base = wid * per_w

        def start(c, slot):
            off = pl.multiple_of(base + c * SC_CHUNK, SC_CHUNK)
            pltpu.sync_copy(idx_hbm.at[pl.ds(off, SC_CHUNK)], idx_v.at[slot])
            pltpu.async_copy(table_hbm.at[idx_v.at[slot]], rows_v.at[slot], sems.at[slot])

        def finish(c, slot):
            off = pl.multiple_of(base + c * SC_CHUNK, SC_CHUNK)
            pltpu.make_async_copy(table_hbm.at[idx_v.at[slot]], rows_v.at[slot], sems.at[slot]).wait()
            pltpu.sync_copy(rows_v.at[slot], out_hbm.at[pl.ds(off, SC_CHUNK)])

        start(0, 0)

        @pl.loop(0, n_chunk, step=2)
        def _(c):
            start(c + 1, 1)
            finish(c, 0)

            @pl.when(c + 2 < n_chunk)
            def _():
                start(c + 2, 0)

            finish(c + 1, 1)

    return gather(table, idx)


def _sc_scatter(src, pos3, n_rows):
    n, w = src.shape
    n_workers = SC_CORES * SC_SUBCORES
    per_w = n // n_workers
    assert n % (n_workers * SC_CHUNK) == 0
    mesh = plsc.VectorSubcoreMesh(core_axis_name="c", subcore_axis_name="s")

    n_chunk = per_w // SC_CHUNK
    assert n_chunk % 2 == 0

    @functools.partial(
        pl.kernel, mesh=mesh, out_type=jax.ShapeDtypeStruct((n_rows, w), src.dtype),
        scratch_types=[pltpu.VMEM((2, TOP_K, SC_CHUNK), jnp.int32), pltpu.VMEM((2, SC_CHUNK, w), src.dtype),
                       pltpu.SemaphoreType.DMA((2,)), pltpu.SemaphoreType.DMA((2,))])
    def scatter(src_hbm, pos_hbm, out_hbm, idx_v, rows_v, ld_sems, sc_sems):
        wid = lax.axis_index("s") * SC_CORES + lax.axis_index("c")
        base = wid * per_w

        def loads(c, slot):
            off = pl.multiple_of(base + c * SC_CHUNK, SC_CHUNK)
            return (pltpu.make_async_copy(src_hbm.at[pl.ds(off, SC_CHUNK)], rows_v.at[slot], ld_sems.at[slot]),
                    pltpu.make_async_copy(pos_hbm.at[off // SC_CHUNK], idx_v.at[slot], ld_sems.at[slot]))

        def scatter_chunk(slot):
            copies = [pltpu.async_copy(rows_v.at[slot], out_hbm.at[idx_v.at[slot, k]], sc_sems.at[slot])
                      for k in range(TOP_K)]
            for cp in copies:
                cp.wait()

        def half_step(c, slot):
            for cp in loads(c, slot):
                cp.wait()

            @pl.when(c + 1 < n_chunk)
            def _():
                for cp in loads(c + 1, 1 - slot):
                    cp.start()

            scatter_chunk(slot)

        for cp in loads(0, 0):
            cp.start()

        @pl.loop(0, n_chunk, step=2)
        def _(c):
            half_step(c, 0)
            half_step(c + 1, 1)

    return scatter(src, pos3)


def _experts_kernel(te_ref, nu_ref, xs_ref, wg_ref, wu_ref, wd_ref, ys_ref):
    @pl.when(pl.program_id(0) < nu_ref[0])
    def _():
        x = _unpack_pairs(xs_ref[...])
        act = _glu(x, _bf(wg_ref[0]), _bf(wu_ref[0]))
        ys_ref[...] = _pack_pairs(jnp.dot(_bf(act), _bf(wd_ref[0]), preferred_element_type=F32))


def _experts(xs, tile_expert, n_used, mp, layer):
    n_tiles = xs.shape[0] // MOE_TILE
    half = D_MODEL // 2
    wspec = lambda shape: pl.BlockSpec((None, 1) + shape, lambda i, te, nu: (layer, te[i], 0, 0))
    return pl.pallas_call(
        _experts_kernel,
        grid_spec=pltpu.PrefetchScalarGridSpec(
            num_scalar_prefetch=2, grid=(n_tiles,),
            in_specs=[pl.BlockSpec((MOE_TILE, half), lambda i, te, nu: (i, 0)),
                      wspec((D_MODEL, D_EXPERT)), wspec((D_MODEL, D_EXPERT)), wspec((D_EXPERT, D_MODEL))],
            out_specs=pl.BlockSpec((MOE_TILE, half), lambda i, te, nu: (i, 0))),
        out_shape=jax.ShapeDtypeStruct(xs.shape, jnp.uint32),
        compiler_params=_cparams(1),
        name="experts",
    )(tile_expert, n_used, xs, mp['wg'], mp['wu'], mp['wd'])


def _combine_kernel(h_ref, *refs):
    yg_refs = refs[:TOP_K]
    ew_ref, eye_ref, sg_ref, su_ref, sd_ref, x1_ref, g2_ref, o_ref = refs[TOP_K:]
    act = _glu(h_ref[...], sg_ref[...], su_ref[...])
    acc = jnp.dot(_bf(act), sd_ref[...], preferred_element_type=F32)
    ew = ew_ref[...]
    hi = _bf(ew)
    lo = _bf(ew - hi.astype(F32))
    ew_t = _dot_tn(hi, eye_ref[...]) + _dot_tn(lo, eye_ref[...])
    for k in range(TOP_K):
        acc = acc + ew_t[:, k:k + 1] * _unpack_pairs(yg_refs[k][...]).astype(F32)
    o_ref[...] = x1_ref[...] + g2_ref[...] * acc


def _combine(rows, hffn, yg, ew, mp, x1, mod4, layer, row0, n_out):
    half = D_MODEL // 2
    n_blk = rows.n // TM
    blk0 = row0 // TM
    const = lambda a: pl.BlockSpec(a.shape, lambda i: (0,) * a.ndim)
    tok = lambda w: pl.BlockSpec((TM, w), lambda i: (blk0 + i, 0))
    slot = lambda k: pl.BlockSpec((TM, half), lambda i: (k * n_blk + blk0 + i, 0))
    eye = jnp.eye(TOP_K, dtype=BF16)
    return pl.pallas_call(
        _combine_kernel,
        grid=(n_out // TM,),
        in_specs=[tok(D_MODEL)] + [slot(k) for k in range(TOP_K)]
                 + [pl.BlockSpec((TOP_K, TM), lambda i: (0, blk0 + i)), const(eye),
                    const(mp['sg']), const(mp['su']), const(mp['sd']), tok(D_MODEL),
                    rows.mod_spec(layer, 5, TM, blk0)],
        out_specs=pl.BlockSpec((TM, D_MODEL), lambda i: (i, 0)),
        out_shape=jax.ShapeDtypeStruct((n_out, D_MODEL), F32),
        compiler_params=_cparams(1),
        name=f"combine{layer}",
    )(hffn, *([yg] * TOP_K), ew, eye, mp['sg'], mp['su'], mp['sd'], x1, mod4)


def _moe(rows, hffn, hpack, x1, router, bias, mp, mod4, layer, out_ranges):
    n = rows.n
    sel, eidx, ew, cnt = _router(hffn, router, bias)
    counts = cnt[:, 0].astype(jnp.int32)
    padded = (counts + MOE_TILE - 1) // MOE_TILE * MOE_TILE
    ends = jnp.cumsum(padded)
    n_rows = n * TOP_K + N_EXPERTS * MOE_TILE
    n_tiles = n_rows // MOE_TILE
    base = jnp.broadcast_to((ends - padded).astype(F32)[:, None], (N_EXPERTS, LANES))
    tile_start = jnp.arange(n_tiles, dtype=jnp.int32) * MOE_TILE
    tile_expert = jnp.minimum(jnp.sum((ends[None, :] <= tile_start[:, None]).astype(jnp.int32), axis=1),
                              N_EXPERTS - 1)
    n_used = (ends[-1:] // MOE_TILE).astype(jnp.int32)
    pos = _positions(sel, eidx, base)
    pos3 = pos.reshape(TOP_K, n // SC_CHUNK, SC_CHUNK).transpose(1, 0, 2)
    xs = _sc_scatter(hpack, pos3, n_rows)
    ys = _experts(xs, tile_expert, n_used, mp, layer)
    yg = _sc_gather(ys, pos.reshape(-1))
    return [_combine(rows, hffn, yg, ew, mp, x1, mod4, layer, row0, n_out) for row0, n_out in out_ranges]


def _glu(x, wg, wu):
    hg = jnp.dot(x, wg, preferred_element_type=F32)
    hu = jnp.dot(x, wu, preferred_element_type=F32)
    return _silu(hg) * hu


def kernel(x_prompt, x_sample, c, c_ctx, cache_attn_k, cache_attn_v, state_rwkv_fwd, state_rwkv_bwd,
           state_hgrn_fwd, state_hgrn_bwd, norm1_g, norm2_g, mod_w, mod_b, ab_w_in, ab_w_out, attn_q_norm,
           attn_k_norm, attn_sink, rwkv_mu, rwkv_w0, rwkv_w2, rwkv_a0, rwkv_a2, rwkv_g2, rwkv_k_k, rwkv_k_a,
           rwkv_r_k, rwkv_ln_w, rwkv_ln_b, hgrn_w_in, hgrn_w_out, hgrn_lower_bounds, hgrn_norm_g, moe_router,
           moe_bias, moe_w_gate, moe_w_up, moe_w_down, moe_shared_gate, moe_shared_up, moe_shared_down):
    n_cseq, cseq, _ = x_prompt.shape
    n_lseq, lseq, _ = x_sample.shape
    depth = mod_w.shape[0]
    assert depth == 2 and n_lseq + 1 <= SUBLANES
    assert cseq == TM and lseq % TM == 0 and lseq % HG_TB == 0 and cseq % HG_TB == 0
    assert n_cseq % RW_BB == 0 and n_lseq % RW_BB == 0 and (n_cseq * cseq) % (lseq * RW_BB) == 0
    rows = _Rows(n_cseq * cseq, n_lseq * lseq, lseq)
    assert rows.n % MOE_TILE == 0 and lseq % MOE_TILE == 0 and rows.n_ctx % MOE_TILE == 0
    kv_w = KV_A * HD_A

    xs = (x_prompt.reshape(rows.n_ctx, D_MODEL), x_sample.reshape(rows.n_lat, D_MODEL))
    cvecs = jnp.concatenate([c_ctx[None, :], c, jnp.zeros((SUBLANES - 1 - n_lseq, D_MODEL), F32)], axis=0)
    mod4 = _modulation(cvecs, mod_w, mod_b).reshape(depth, SUBLANES, 1, 6 * D_MODEL)

    ones_q = _block_ones(W_A, HD_A)
    ones_k = _block_ones(kv_w, HD_A)
    ones_b = _block_ones(W_B, HD_B)
    ones_pair = _block_ones(LANES, HD_B)[:LANES]
    ones_pair = jnp.kron(jnp.eye(2, dtype=BF16), ones_pair)
    cos_t, sin_t = _rope_tables(lseq)

    def moe(l, hffn, hpack, x1, out_ranges):
        mp = {'wg': moe_w_gate, 'wu': moe_w_up, 'wd': moe_w_down,
              'sg': _bf(moe_shared_gate[l]), 'su': _bf(moe_shared_up[l]), 'sd': _bf(moe_shared_down[l])}
        return _moe(rows, hffn, hpack, x1, moe_router[l], moe_bias[l], mp, mod4, l, out_ranges)

    assert W_A == W_B
    all_rows = jnp.zeros((rows.n, W_B), F32)

    pr = {'mu': rwkv_mu[0], 'w0': rwkv_w0[0], 'w2': rwkv_w2[0], 'a0': rwkv_a0[0], 'a2': rwkv_a2[0],
          'g2': rwkv_g2[0], 'k_k': rwkv_k_k[0], 'k_a': rwkv_k_a[0], 'r_k': rwkv_r_k[0].reshape(-1),
          'ln_w': rwkv_ln_w[0], 'ln_b': rwkv_ln_b[0]}
    p_att, p_rw = _inproj(rows, xs, norm1_g[0], mod4, 0, _bf(ab_w_in[0]), (ATT_IN, RWKV_IN))
    qg_t = jnp.tile(attn_q_norm[0], H_A).reshape(1, W_A)
    kg_t = jnp.tile(attn_k_norm[0], KV_A).reshape(1, kv_w)
    o_att, new_k, new_v = _ctx_attention(p_att, n_cseq, cseq, qg_t, kg_t, attn_sink[0], ones_q, ones_k, all_rows)
    past = cache_attn_k.shape[2]
    o_att = _lat_attention(p_att, rows.n_ctx // lseq, n_lseq, lseq, qg_t, kg_t, attn_sink[0], ones_q, ones_k,
                           cos_t, sin_t, cache_attn_k[:, 0].reshape(n_lseq, past, kv_w),
                           cache_attn_v[:, 0].reshape(n_lseq, past, kv_w), o_att)

    pp = _rwkv_prep(rows, p_rw, pr, ones_b)
    zero_st = jnp.zeros((n_cseq, H_B // 2, HD_B, LANES), F32)
    o_f, o_b, sf_c, sb_c = _rwkv_scan(pp, 0, n_cseq, cseq, zero_st, zero_st, ones_pair, all_rows, all_rows)
    o_f, o_b, _, _ = _rwkv_scan(pp, rows.n_ctx, n_lseq, lseq, _state_to_pairs(state_rwkv_fwd[:, 0]),
                                _state_to_pairs(state_rwkv_bwd[:, 0]), ones_pair, o_f, o_b)
    x1, hffn, hpack = _outproj0(rows, xs, o_att, o_f, o_b, pp, pr, ones_b, _bf(ab_w_out[0]), norm2_g[0], mod4, 0)
    (x,) = moe(0, hffn, hpack, x1, [(0, rows.n)])

    (p1,) = _inproj(rows, (x, x), norm1_g[1], mod4, 1, _bf(hgrn_w_in[0]), (IN_C,))
    zero_h = jnp.zeros((n_cseq, H_C, DK_C, DV_C), F32)
    all_rows_c = jnp.zeros((rows.n, D_C), F32)
    h_f, h_b, hsf_c, hsb_c = _hgrn_scan(p1, 0, n_cseq, cseq, hgrn_lower_bounds, zero_h, zero_h,
                                        all_rows_c, all_rows_c)
    h_f, h_b, _, _ = _hgrn_scan(p1, rows.n_ctx, n_lseq, lseq, hgrn_lower_bounds,
                                state_hgrn_fwd[:, 0], state_hgrn_bwd[:, 0], h_f, h_b)
    x1, hffn, hpack = _outproj1(rows, (x, x), h_f, h_b, p1, hgrn_norm_g[0], _bf(hgrn_w_out[0]), norm2_g[1],
                                mod4, 1)
    y_c, y_l = moe(1, hffn, hpack, x1, [(0, rows.n_ctx), (rows.n_ctx, rows.n_lat)])

    y_prompt = y_c.reshape(n_cseq, cseq, D_MODEL)
    y_sample = y_l.reshape(n_lseq, lseq, D_MODEL)
    return (y_prompt, y_sample,
            new_k.reshape(n_cseq, 1, cseq, KV_A, HD_A), new_v.reshape(n_cseq, 1, cseq, KV_A, HD_A),
            _pairs_to_state(sf_c)[:, None], _pairs_to_state(sb_c)[:, None],
            hsf_c[:, None], hsb_c[:, None])
```

```python
import functools

import numpy as np
import jax
import jax.numpy as jnp
from jax import lax
from jax.experimental import pallas as pl
from jax.experimental.pallas import tpu as pltpu
from jax.experimental.pallas import tpu_sc as plsc

F32 = jnp.float32
BF16 = jnp.bfloat16

D_MODEL = 1024
GRID_W = 64
H_A = 8
KV_A = 2
G_A = H_A // KV_A
HD_A = 64
W_A = H_A * HD_A
WINDOW = 128
QBLK = 128
ROPE_BASE = 10000.0
ATTN_SCALE = HD_A ** -0.5
NEG_INF = -1e30
H_B = 8
HD_B = 64
W_B = H_B * HD_B
LORA_W = 64
LORA_A = 64
LORA_G = 128
GN_EPS = 64e-5
ATT_IN = W_A + 2 * KV_A * HD_A
RWKV_IN = 3 * W_B + LORA_W + LORA_A + LORA_G
IN_AB = ATT_IN + RWKV_IN
H_C = 8
DK_C = 128
DV_C = 128
D_C = H_C * DV_C
CHUNK = 64
IN_C = 5 * D_C
N_EXPERTS = 64
TOP_K = 8
N_GROUPS = 8
TOPK_GROUPS = 4
D_EXPERT = 256
ROUTED_SCALE = 2.5
EPS = 1e-6

LANES = 128
SUBLANES = 8
VMEM_LIMIT = 52 * 1024 * 1024

TM = 256
RW_TB = 128
RW_BB = 4
RW_GROUP_BB = 4
HG_TB = 256
MOE_TILE = 512
POS_TB = 512
SC_CORES = 2
SC_SUBCORES = 16
SC_CHUNK = 64


def _cparams(n_axes):
    return pltpu.CompilerParams(dimension_semantics=("arbitrary",) * n_axes,
                                vmem_limit_bytes=VMEM_LIMIT)


def _bf(x):
    return x.astype(BF16)


def _split2(x):
    hi = lax.bitcast_convert_type(
        lax.bitcast_convert_type(x, jnp.uint32) & jnp.uint32(0xFFFF0000), F32)
    return hi, x - hi


def _seg_sum(x, ones2):
    hi, lo = _split2(x)
    return jnp.dot(jnp.concatenate([_bf(hi), _bf(lo)], axis=1), ones2,
                   preferred_element_type=F32)


def _dot_nt(a, b):
    return lax.dot_general(a, b, (((1,), (1,)), ((), ())), preferred_element_type=F32)


def _dot_tn(a, b):
    return lax.dot_general(a, b, (((0,), (0,)), ((), ())), preferred_element_type=F32)


def _sigmoid(x):
    return 1.0 / (1.0 + jnp.exp(-x))


def _silu(x):
    return x * _sigmoid(x)


def _chunks(seq, n):
    seq = list(seq)
    return [seq[i:i + n] for i in range(0, len(seq), n)]


def _block_ones(width, seg):
    idx = np.arange(width) // seg
    bd = (idx[:, None] == idx[None, :]).astype(np.float32)
    return jnp.asarray(np.concatenate([bd, bd], axis=0), dtype=BF16)


def _mod_kernel(c_ref, w_ref, b_ref, o_ref):
    s = _silu(c_ref[...])
    o_ref[0] = jnp.dot(_bf(s), _bf(w_ref[0]), preferred_element_type=F32) + b_ref[0]


def _modulation(cvecs, mod_w, mod_b):
    depth = mod_w.shape[0]
    n_col = 6 * D_MODEL // D_MODEL
    return pl.pallas_call(
        _mod_kernel,
        grid=(depth, n_col),
        in_specs=[pl.BlockSpec((SUBLANES, D_MODEL), lambda l, j: (0, 0)),
                  pl.BlockSpec((1, D_MODEL, D_MODEL), lambda l, j: (l, 0, j)),
                  pl.BlockSpec((1, 1, D_MODEL), lambda l, j: (l, 0, j))],
        out_specs=pl.BlockSpec((1, SUBLANES, D_MODEL), lambda l, j: (l, 0, j)),
        out_shape=jax.ShapeDtypeStruct((depth, SUBLANES, 6 * D_MODEL), F32),
        compiler_params=_cparams(2),
        name="modulation",
    )(cvecs, mod_w, mod_b.reshape(depth, 1, 6 * D_MODEL))


class _Rows:
    def __init__(self, n_ctx, n_lat, lat_seq):
        self.n_ctx, self.n_lat, self.lat_seq = n_ctx, n_lat, lat_seq
        self.n = n_ctx + n_lat

    def mod_row(self, i, tm):
        nctx_blk = self.n_ctx // tm
        per_seq = self.lat_seq // tm
        return jnp.where(i < nctx_blk, 0, 1 + (i - nctx_blk) // per_seq)

    def mod_spec(self, layer, chunk, tm, blk0=0):
        return pl.BlockSpec((None, None, 1, D_MODEL),
                            lambda i, *_: (layer, self.mod_row(i + blk0, tm), 0, chunk))


def _rms_mod(x, g, sc, sh):
    ms = jnp.mean(x * x, axis=-1, keepdims=True)
    return x * lax.rsqrt(ms + EPS) * g * (1.0 + sc) + sh


def _x_specs(rows, xs):
    xa, xb = xs
    nctx_blk = rows.n_ctx // TM
    lat0 = nctx_blk if xb.shape[0] == rows.n else 0
    return [pl.BlockSpec((TM, D_MODEL), lambda i: (jnp.minimum(i, nctx_blk - 1), 0)),
            pl.BlockSpec((TM, D_MODEL), lambda i: (jnp.maximum(i - nctx_blk, 0) + lat0, 0))]


def _pick_x(rows, xa_ref, xb_ref):
    return jnp.where(pl.program_id(0) < rows.n_ctx // TM, xa_ref[...], xb_ref[...])


def _inproj_kernel(rows, splits, xa_ref, xb_ref, g_ref, sh_ref, sc_ref, w_ref, *o_refs):
    h = _rms_mod(_pick_x(rows, xa_ref, xb_ref), g_ref[...], sc_ref[...], sh_ref[...])
    p = jnp.dot(_bf(h), w_ref[...], preferred_element_type=F32)
    lo = 0
    for o_ref, width in zip(o_refs, splits):
        o_ref[...] = p[:, lo:lo + width]
        lo += width


def _inproj(rows, xs, g, mod4, layer, w_bf, splits):
    n_out = w_bf.shape[1]
    return pl.pallas_call(
        functools.partial(_inproj_kernel, rows, splits),
        grid=(rows.n // TM,),
        in_specs=_x_specs(rows, xs) + [
            pl.BlockSpec((1, D_MODEL), lambda i: (0, 0)),
            rows.mod_spec(layer, 0, TM),
            rows.mod_spec(layer, 1, TM),
            pl.BlockSpec((D_MODEL, n_out), lambda i: (0, 0))],
        out_specs=[pl.BlockSpec((TM, wd), lambda i: (i, 0)) for wd in splits],
        out_shape=[jax.ShapeDtypeStruct((rows.n, wd), F32) for wd in splits],
        compiler_params=_cparams(1),
        name=f"inproj{layer}",
    )(*xs, g.reshape(1, D_MODEL), mod4, mod4, w_bf)


def _head_rms(x, gain_t, ones2):
    ms = _seg_sum(x * x, ones2) * (1.0 / HD_A)
    return x * lax.rsqrt(ms + EPS) * gain_t


def _sink_softmax_pv(parts, sink):
    m = jnp.maximum(functools.reduce(jnp.maximum, [jnp.max(s, axis=-1, keepdims=True) for s, _ in parts]), sink)
    den = jnp.exp(sink - m)
    acc = None
    for s, v in parts:
        p = jnp.exp(s - m)
        den = den + jnp.sum(p, axis=-1, keepdims=True)
        pv = jnp.dot(_bf(p), v, preferred_element_type=F32)
        acc = pv if acc is None else acc + pv
    return acc / den


def _ctx_attn_kernel(p_ref, qg_ref, kg_ref, sink_ref, ones_q_ref, ones_k_ref, prev_ref, o_ref, k_ref, v_ref):
    del prev_ref
    p = p_ref[...]
    q = _head_rms(p[:, :W_A], qg_ref[...], ones_q_ref[...]) * ATTN_SCALE
    k = _head_rms(p[:, W_A:W_A + KV_A * HD_A], kg_ref[...], ones_k_ref[...])
    v = p[:, W_A + KV_A * HD_A:ATT_IN]
    k_ref[0] = k
    v_ref[0] = v
    qb, kb, vb = _bf(q), _bf(k), _bf(v)
    outs = []
    for h in range(H_A):
        j = h // G_A
        s = _dot_nt(qb[:, h * HD_A:(h + 1) * HD_A], kb[:, j * HD_A:(j + 1) * HD_A])
        outs.append(_sink_softmax_pv([(s, vb[:, j * HD_A:(j + 1) * HD_A])], sink_ref[h]))
    o_ref[...] = jnp.concatenate(outs, axis=1)


def _ctx_attention(p_att, n_seq, seq, qg_t, kg_t, sink, ones_q, ones_k, prev):
    kv_w = KV_A * HD_A
    return pl.pallas_call(
        _ctx_attn_kernel,
        grid=(n_seq,),
        in_specs=[pl.BlockSpec((seq, ATT_IN), lambda b: (b, 0)),
                  pl.BlockSpec((1, W_A), lambda b: (0, 0)),
                  pl.BlockSpec((1, kv_w), lambda b: (0, 0)),
                  pl.BlockSpec(memory_space=pltpu.SMEM),
                  pl.BlockSpec(ones_q.shape, lambda b: (0, 0)),
                  pl.BlockSpec(ones_k.shape, lambda b: (0, 0)),
                  pl.BlockSpec(memory_space=pl.ANY)],
        out_specs=[pl.BlockSpec((seq, W_A), lambda b: (b, 0)),
                   pl.BlockSpec((1, seq, kv_w), lambda b: (b, 0, 0)),
                   pl.BlockSpec((1, seq, kv_w), lambda b: (b, 0, 0))],
        input_output_aliases={6: 0},
        out_shape=[jax.ShapeDtypeStruct(prev.shape, F32),
                   jax.ShapeDtypeStruct((n_seq, seq, kv_w), F32),
                   jax.ShapeDtypeStruct((n_seq, seq, kv_w), F32)],
        compiler_params=_cparams(1),
        name="ctx_attention",
    )(p_att, qg_t, kg_t, sink, ones_q, ones_k, prev)


def _rope(x, cos_t, sin_t):
    lane = lax.broadcasted_iota(jnp.int32, cos_t.shape, 1)
    low = (lane % 32) < 16
    outs = []
    for s in range(x.shape[1] // LANES):
        xs = x[:, s * LANES:(s + 1) * LANES]
        partner = jnp.where(low, pltpu.roll(xs, LANES - 16, 1), pltpu.roll(xs, 16, 1))
        outs.append(xs * cos_t + partner * sin_t)
    return outs[0] if len(outs) == 1 else jnp.concatenate(outs, axis=1)


def _lat_attn_kernel(seq, p_ref, qg_ref, kg_ref, sink_ref, ones_q_ref, ones_k_ref, cos_ref, sin_ref,
                     kc_ref, vc_ref, prev_ref, o_ref, q_scr, k_scr, v_scr):
    del prev_ref
    kv_w = KV_A * HD_A
    p = p_ref[...]
    q = _head_rms(p[:, :W_A], qg_ref[...], ones_q_ref[...])
    k = _head_rms(p[:, W_A:W_A + kv_w], kg_ref[...], ones_k_ref[...])
    qr = _bf(_rope(q, cos_ref[...], sin_ref[...]) * ATTN_SCALE)
    kr = _bf(_rope(k, cos_ref[...], sin_ref[...]))
    vb = _bf(p[:, W_A + kv_w:ATT_IN])
    for h in range(H_A):
        q_scr[h] = qr[:, h * HD_A:(h + 1) * HD_A]
    for j in range(KV_A):
        k_scr[j] = kr[:, j * HD_A:(j + 1) * HD_A]
        v_scr[j] = vb[:, j * HD_A:(j + 1) * HD_A]
    kc = _bf(kc_ref[0])
    vc = _bf(vc_ref[0])
    n_local = 3 * QBLK
    grp = lax.broadcasted_iota(jnp.int32, (G_A * QBLK, 1), 0) // QBLK

    def block(i, carry):
        q0 = pl.multiple_of(i * QBLK, QBLK)
        start = pl.multiple_of(jnp.clip((i - 1) * QBLK, 0, seq - n_local), QBLK)
        ipos = q0 + lax.broadcasted_iota(jnp.int32, (G_A * QBLK, n_local), 0) % QBLK
        jpos = start + lax.broadcasted_iota(jnp.int32, (G_A * QBLK, n_local), 1)
        band = jnp.abs(jpos - ipos) <= WINDOW
        outs = []
        for j in range(KV_A):
            qs = jnp.concatenate([q_scr[j * G_A + g, pl.ds(q0, QBLK), :] for g in range(G_A)], axis=0)
            kl = k_scr[j, pl.ds(start, n_local), :]
            vl = v_scr[j, pl.ds(start, n_local), :]
            sink = jnp.zeros((G_A * QBLK, 1), F32)
            for g in range(G_A):
                sink = jnp.where(grp == g, sink_ref[j * G_A + g], sink)
            s_loc = jnp.where(band, _dot_nt(qs, kl), NEG_INF)
            s_ctx = _dot_nt(qs, kc[:, j * HD_A:(j + 1) * HD_A])
            o = _sink_softmax_pv([(s_loc, vl), (s_ctx, vc[:, j * HD_A:(j + 1) * HD_A])], sink)
            outs.extend(o[g * QBLK:(g + 1) * QBLK] for g in range(G_A))
        o_ref[pl.ds(q0, QBLK), :] = jnp.concatenate(outs, axis=1)
        return carry

    lax.fori_loop(0, seq // QBLK, block, 0)


def _lat_attention(p_att, row_blk0, n_seq, seq, qg_t, kg_t, sink, ones_q, ones_k, cos_t, sin_t, kc, vc, prev):
    kv_w = KV_A * HD_A
    past = kc.shape[1]
    return pl.pallas_call(
        functools.partial(_lat_attn_kernel, seq),
        grid=(n_seq,),
        in_specs=[pl.BlockSpec((seq, ATT_IN), lambda b: (row_blk0 + b, 0)),
                  pl.BlockSpec((1, W_A), lambda b: (0, 0)),
                  pl.BlockSpec((1, kv_w), lambda b: (0, 0)),
                  pl.BlockSpec(memory_space=pltpu.SMEM),
                  pl.BlockSpec(ones_q.shape, lambda b: (0, 0)),
                  pl.BlockSpec(ones_k.shape, lambda b: (0, 0)),
                  pl.BlockSpec((seq, LANES), lambda b: (0, 0)),
                  pl.BlockSpec((seq, LANES), lambda b: (0, 0)),
                  pl.BlockSpec((1, past, kv_w), lambda b: (b, 0, 0)),
                  pl.BlockSpec((1, past, kv_w), lambda b: (b, 0, 0)),
                  pl.BlockSpec(memory_space=pl.ANY)],
        out_specs=pl.BlockSpec((seq, W_A), lambda b: (row_blk0 + b, 0)),
        out_shape=jax.ShapeDtypeStruct(prev.shape, F32),
        input_output_aliases={10: 0},
        scratch_shapes=[pltpu.VMEM((H_A, seq, HD_A), BF16), pltpu.VMEM((KV_A, seq, HD_A), BF16),
                        pltpu.VMEM((KV_A, seq, HD_A), BF16)],
        compiler_params=_cparams(1),
        name="lat_attention",
    )(p_att, qg_t, kg_t, sink, ones_q, ones_k, cos_t, sin_t, kc, vc, prev)


def _rope_tables(seq):
    pos = np.arange(seq)
    row = (pos // GRID_W).astype(np.float32)
    col = (pos % GRID_W).astype(np.float32)
    d_axis = HD_A // 2
    inv = (ROPE_BASE ** (-np.arange(0, d_axis, 2, dtype=np.float32) / d_axis)).astype(np.float32)
    cos_h = np.zeros((seq, HD_A), np.float32)
    sin_h = np.zeros((seq, HD_A), np.float32)
    for seg, p_ in enumerate((row, col)):
        ang = (p_[:, None] * inv[None, :]).astype(np.float32)
        c, s = np.cos(ang), np.sin(ang)
        base = seg * d_axis
        cos_h[:, base:base + d_axis // 2] = c
        cos_h[:, base + d_axis // 2:base + d_axis] = c
        sin_h[:, base:base + d_axis // 2] = -s
        sin_h[:, base + d_axis // 2:base + d_axis] = s
    rep = LANES // HD_A
    return jnp.asarray(np.tile(cos_h, (1, rep))), jnp.asarray(np.tile(sin_h, (1, rep)))


def _rwkv_prep_kernel(rows, x_ref, prev_ref, next_ref, mu_ref, kk_ref, ka_ref, rk_ref, w0_ref, w2_ref,
                      a0_ref, a2_ref, g2_ref, ones_ref,
                      nkk_ref, r_ref, v_ref, g_ref, bonus_ref,
                      wf_ref, kaf_ref, kdf_ref, wb_ref, kab_ref, kdb_ref):
    i = pl.program_id(0)
    nctx_blk = rows.n_ctx // TM
    per_seq = rows.lat_seq // TM
    is_ctx = i < nctx_blk
    first = jnp.logical_or(is_ctx, (i - nctx_blk) % per_seq == 0)
    last = jnp.logical_or(is_ctx, (i - nctx_blk) % per_seq == per_seq - 1)
    x = x_ref[...]
    ridx = lax.broadcasted_iota(jnp.int32, x.shape, 0)
    prev_row = jnp.where(first, 0.0, prev_ref[SUBLANES - 1:SUBLANES, :])
    next_row = jnp.where(last, 0.0, next_ref[0:1, :])
    xm1 = jnp.where(ridx == 0, prev_row, pltpu.roll(x, 1, 0))
    xp1 = jnp.where(ridx == TM - 1, next_row, pltpu.roll(x, TM - 1, 0))
    pw = x + (0.5 * (xm1 + xp1) - x) * mu_ref[...]

    r = pw[:, 0:W_B]
    k = pw[:, W_B:2 * W_B]
    v = pw[:, 2 * W_B:3 * W_B]
    wd = pw[:, 3 * W_B:3 * W_B + LORA_W]
    ad = pw[:, 3 * W_B + LORA_W:3 * W_B + LORA_W + LORA_A]
    gd = pw[:, 3 * W_B + LORA_W + LORA_A:]
    ones2 = ones_ref[...]

    kk = k * kk_ref[...]
    kk = kk / jnp.maximum(jnp.sqrt(_seg_sum(kk * kk, ones2)), 1e-12)
    nkk_ref[...] = -kk
    r_ref[...] = r
    v_ref[...] = v
    g_ref[...] = jnp.dot(_bf(_sigmoid(gd)), g2_ref[...], preferred_element_type=F32)
    tw = _bf(jnp.tanh(wd))
    adb = _bf(ad)
    bonus = jnp.zeros_like(r)
    for d, (w_o, ka_o, kd_o) in enumerate(((wf_ref, kaf_ref, kdf_ref), (wb_ref, kab_ref, kdb_ref))):
        z = -(w0_ref[d:d + 1, :] + jnp.dot(tw, w2_ref[d], preferred_element_type=F32))
        softplus = jnp.maximum(z, 0.0) + jnp.log(1.0 + jnp.exp(-jnp.abs(z)))
        w_o[...] = jnp.exp(-jnp.exp(-softplus - 0.5))
        a = _sigmoid(a0_ref[d:d + 1, :] + jnp.dot(adb, a2_ref[d], preferred_element_type=F32))
        kd = k * (1.0 + (a - 1.0) * ka_ref[...])
        ka_o[...] = kk * a
        kd_o[...] = kd
        bonus = bonus + _seg_sum(r * kd * rk_ref[...], ones2) * v
    bonus_ref[...] = bonus


def _rwkv_prep(rows, p_rw, pr, ones_b):
    n = rows.n
    n_halo = n // SUBLANES
    blk_halo = TM // SUBLANES
    row = lambda a: a.reshape(1, -1)
    full = lambda a: pl.BlockSpec(a.shape, lambda i: (0,) * a.ndim)
    consts = [row(pr['mu']), row(pr['k_k']), row(pr['k_a']), row(pr['r_k']), pr['w0'], _bf(pr['w2']),
              pr['a0'], _bf(pr['a2']), _bf(pr['g2']), ones_b]
    outs = pl.pallas_call(
        functools.partial(_rwkv_prep_kernel, rows),
        grid=(n // TM,),
        in_specs=[pl.BlockSpec((TM, RWKV_IN), lambda i: (i, 0)),
                  pl.BlockSpec((SUBLANES, RWKV_IN), lambda i: (jnp.maximum(i * blk_halo - 1, 0), 0)),
                  pl.BlockSpec((SUBLANES, RWKV_IN), lambda i: (jnp.minimum((i + 1) * blk_halo, n_halo - 1), 0))]
                 + [full(a) for a in consts],
        out_specs=[pl.BlockSpec((TM, W_B), lambda i: (i, 0))] * 11,
        out_shape=[jax.ShapeDtypeStruct((n, W_B), F32)] * 11,
        compiler_params=_cparams(1),
        name="rwkv_prep",
    )(p_rw, p_rw, p_rw, *consts)
    names = ('nkk', 'r', 'v', 'g', 'bonus', 'w_f', 'ka_f', 'kd_f', 'w_b', 'ka_b', 'kd_b')
    return dict(zip(names, outs))


def _rwkv_scan_kernel(n_tb, nkkf_ref, rf_ref, vf_ref, wf_ref, kaf_ref, kdf_ref,
                      nkkb_ref, rb_ref, vb_ref, wb_ref, kab_ref, kdb_ref,
                      s0f_ref, s0b_ref, ones_ref, prevf_ref, prevb_ref,
                      of_ref, ob_ref, sff_ref, sfb_ref, s_scr, vt_scr):
    del prevf_ref, prevb_ref
    tb = pl.program_id(1)
    n_pair = H_B // 2
    half = RW_TB // 2
    dirs = ((nkkf_ref, rf_ref, vf_ref, wf_ref, kaf_ref, kdf_ref, of_ref, False),
            (nkkb_ref, rb_ref, vb_ref, wb_ref, kab_ref, kdb_ref, ob_ref, True))

    @pl.when(tb == 0)
    def _():
        s_scr[0] = s0f_ref[...]
        s_scr[1] = s0b_ref[...]

    lane = lax.broadcasted_iota(jnp.int32, (HD_B, LANES), 1)
    for d, refs in enumerate(dirs):
        v_ref = refs[2]
        for bb in range(RW_BB):
            for p in range(n_pair):
                vt = v_ref[bb, :, p * LANES:(p + 1) * LANES].T
                top, bot = vt[:HD_B], vt[HD_B:]
                for s in range(2):
                    if s == 0:
                        t2 = jnp.where(lane < HD_B, top, pltpu.roll(bot, HD_B, 1))
                    else:
                        t2 = jnp.where(lane < HD_B, pltpu.roll(top, HD_B, 1), bot)
                    vt_scr[d, bb, p, s] = t2

    ones2 = ones_ref[...]
    row8 = lax.broadcasted_iota(jnp.int32, (SUBLANES, LANES), 0)
    lane8 = lax.broadcasted_iota(jnp.int32, (SUBLANES, LANES), 1)
    sel_r = jnp.logical_or(jnp.logical_and(row8 % 2 == 0, lane8 < HD_B),
                           jnp.logical_and(row8 % 2 == 1, lane8 >= HD_B))

    def row_of(rev, tt):
        return RW_TB - 1 - tt if rev else tt

    def emit_output(d, bb, tau):
        r_ref, o_ref = dirs[d][1], dirs[d][6]
        r = r_ref[bb, pl.ds(tau, 1), :]
        r8 = jnp.zeros((SUBLANES, LANES), F32)
        for p in range(n_pair):
            rp = jnp.broadcast_to(r[:, p * LANES:(p + 1) * LANES], (SUBLANES, LANES))
            r8 = jnp.where(jnp.logical_and(sel_r, row8 // 2 == p), rp, r8)
        s_all = jnp.concatenate([_bf(s_scr[d, bb, p]) for p in range(n_pair)], axis=0)
        o8 = _dot_nt(_bf(r8), s_all)
        o_parts = []
        for p in range(n_pair):
            for h in range(2):
                o_parts.append(o8[2 * p + h:2 * p + h + 1, p * HD_B:(p + 1) * HD_B])
        o_ref[bb, pl.ds(tau, 1), :] = jnp.concatenate(o_parts, axis=1)

    groups = [(d, bbs) for d in range(2) for bbs in _chunks(range(RW_BB), RW_GROUP_BB)]

    def step(tt, carry):
        tt_prev = jnp.maximum(tt - 1, 0)
        reds = []
        for d, bbs in groups:
            rev = dirs[d][7]
            tau = row_of(rev, tt)
            sub = tau // half
            lt = tau % half
            mask = jnp.logical_or(lane == lt, lane == lt + HD_B)
            lhs = []
            for bb in bbs:
                emit_output(d, bb, row_of(rev, tt_prev))
                nkk = dirs[d][0][bb, pl.ds(tau, 1), :]
                for p in range(n_pair):
                    prod = s_scr[d, bb, p] * nkk[:, p * LANES:(p + 1) * LANES]
                    lhs.append(jnp.concatenate([_bf(prod), _bf(jnp.where(mask, vt_scr[d, bb, p, sub], 0.0))],
                                               axis=1))
            reds.append(jnp.dot(jnp.concatenate(lhs, axis=0), ones2, preferred_element_type=F32))
        for (d, bbs), red in zip(groups, reds):
            _, _, _, w_ref, ka_ref, kd_ref, _, rev = dirs[d]
            tau = row_of(rev, tt)
            for k, bb in enumerate(bbs):
                w = w_ref[bb, pl.ds(tau, 1), :]
                ka = ka_ref[bb, pl.ds(tau, 1), :]
                kd = kd_ref[bb, pl.ds(tau, 1), :]
                for p in range(n_pair):
                    sl = slice(p * LANES, (p + 1) * LANES)
                    r0 = (k * n_pair + p) * HD_B
                    sa = red[r0:r0 + HD_B, :LANES]
                    vcol = red[r0:r0 + HD_B, LANES:]
                    s_scr[d, bb, p] = s_scr[d, bb, p] * w[:, sl] + sa * ka[:, sl] + vcol * kd[:, sl]
        return carry

    lax.fori_loop(0, RW_TB, step, 0)
    for d in range(2):
        for bb in range(RW_BB):
            emit_output(d, bb, row_of(dirs[d][7], RW_TB - 1))

    @pl.when(tb == n_tb - 1)
    def _():
        sff_ref[...] = s_scr[0]
        sfb_ref[...] = s_scr[1]


def _rwkv_scan(pp, row0, n_seq, seq, s0_f, s0_b, ones_pair, prev_f, prev_b):
    n_tb = seq // RW_TB
    n_pair = H_B // 2
    blk0 = row0 // seq
    view = lambda a: a.reshape(a.shape[0] // seq, seq, W_B)
    fwd = pl.BlockSpec((RW_BB, RW_TB, W_B), lambda b, t: (blk0 // RW_BB + b, t, 0))
    bwd = pl.BlockSpec((RW_BB, RW_TB, W_B), lambda b, t: (blk0 // RW_BB + b, n_tb - 1 - t, 0))
    st = pl.BlockSpec((RW_BB, n_pair, HD_B, LANES), lambda b, t: (b, 0, 0, 0))
    ins_f = [view(pp[k]) for k in ('nkk', 'r', 'v', 'w_f', 'ka_f', 'kd_f')]
    ins_b = [view(pp[k]) for k in ('nkk', 'r', 'v', 'w_b', 'ka_b', 'kd_b')]
    st_shape = jax.ShapeDtypeStruct((n_seq, n_pair, HD_B, LANES), F32)
    o_shape = jax.ShapeDtypeStruct(view(prev_f).shape, F32)
    any_spec = pl.BlockSpec(memory_space=pl.ANY)
    o_f, o_b, sf, sb = pl.pallas_call(
        functools.partial(_rwkv_scan_kernel, n_tb),
        grid=(n_seq // RW_BB, n_tb),
        in_specs=[fwd] * 6 + [bwd] * 6 + [st, st, pl.BlockSpec(ones_pair.shape, lambda b, t: (0, 0)),
                                           any_spec, any_spec],
        out_specs=[fwd, bwd, st, st],
        out_shape=[o_shape, o_shape, st_shape, st_shape],
        input_output_aliases={15: 0, 16: 1},
        scratch_shapes=[pltpu.VMEM((2, RW_BB, n_pair, HD_B, LANES), F32),
                        pltpu.VMEM((2, RW_BB, n_pair, 2, HD_B, LANES), F32)],
        compiler_params=_cparams(2),
        name="rwkv_scan",
    )(*ins_f, *ins_b, s0_f, s0_b, ones_pair, view(prev_f), view(prev_b))
    return o_f.reshape(prev_f.shape), o_b.reshape(prev_b.shape), sf, sb


def _state_to_pairs(s):
    b = s.shape[0]
    return s.reshape(b, H_B // 2, 2, HD_B, HD_B).transpose(0, 1, 3, 2, 4).reshape(b, H_B // 2, HD_B, 2 * HD_B)


def _pairs_to_state(s):
    b = s.shape[0]
    return s.reshape(b, H_B // 2, HD_B, 2, HD_B).transpose(0, 1, 3, 2, 4).reshape(b, H_B, HD_B, HD_B)


def _tail(x, y, g1, n2g, sc2, sh2, x1_ref, h_ref, hp_ref):
    x1 = x + g1 * y
    x1_ref[...] = x1
    h = _rms_mod(x1, n2g, sc2, sh2)
    h_ref[...] = _bf(h)
    hp_ref[...] = _pack_pairs(h)


def _outproj0_kernel(rows, xa_ref, xb_ref, oa_ref, of_ref, ob_ref, bonus_ref, g_ref, lnw_ref, lnb_ref, ones_ref,
                     w_ref, g1_ref, n2g_ref, sc2_ref, sh2_ref, x1_ref, h_ref, hp_ref):
    o_sum = of_ref[...] + ob_ref[...]
    ones2 = ones_ref[...]
    mean = _seg_sum(o_sum, ones2) * (1.0 / HD_B)
    cen = o_sum - mean
    var = _seg_sum(cen * cen, ones2) * (1.0 / HD_B)
    gn = cen * lax.rsqrt(var + GN_EPS) * lnw_ref[...] + lnb_ref[...]
    o_rw = (gn + bonus_ref[...]) * g_ref[...]
    mix = jnp.concatenate([_bf(oa_ref[...]), _bf(o_rw)], axis=1)
    y = jnp.dot(mix, w_ref[...], preferred_element_type=F32)
    _tail(_pick_x(rows, xa_ref, xb_ref), y, g1_ref[...], n2g_ref[...], sc2_ref[...], sh2_ref[...], x1_ref, h_ref, hp_ref)


def _outproj0(rows, xs, o_att, o_f, o_b, pp, pr, ones_b, w_out_bf, n2g, mod4, layer):
    n = rows.n
    tok = lambda w: pl.BlockSpec((TM, w), lambda i: (i, 0))
    const = lambda a: pl.BlockSpec(a.shape, lambda i: (0,) * a.ndim)
    lnw, lnb, n2 = pr['ln_w'].reshape(1, -1), pr['ln_b'].reshape(1, -1), n2g.reshape(1, -1)
    return pl.pallas_call(
        functools.partial(_outproj0_kernel, rows),
        grid=(n // TM,),
        in_specs=_x_specs(rows, xs) + [tok(W_A), tok(W_B), tok(W_B), tok(W_B), tok(W_B),
                  const(lnw), const(lnb), const(ones_b), const(w_out_bf),
                  rows.mod_spec(layer, 2, TM), const(n2), rows.mod_spec(layer, 4, TM), rows.mod_spec(layer, 3, TM)],
        out_specs=[tok(D_MODEL), tok(D_MODEL), tok(D_MODEL // 2)],
        out_shape=[jax.ShapeDtypeStruct((n, D_MODEL), F32), jax.ShapeDtypeStruct((n, D_MODEL), BF16),
                   jax.ShapeDtypeStruct((n, D_MODEL // 2), jnp.uint32)],
        compiler_params=_cparams(1),
        name="outproj0",
    )(*xs, o_att, o_f, o_b, pp['bonus'], pp['g'], lnw, lnb, ones_b, w_out_bf, mod4, n2, mod4, mod4)


def _outproj1_kernel(rows, xa_ref, xb_ref, of_ref, ob_ref, gate_ref, ng_ref, w_ref, g1_ref, n2g_ref, sc2_ref, sh2_ref,
                     x1_ref, h_ref, hp_ref):
    o_sum = of_ref[...] + ob_ref[...]
    parts = []
    for h in range(H_C):
        oh = o_sum[:, h * DV_C:(h + 1) * DV_C]
        parts.append(oh * lax.rsqrt(jnp.mean(oh * oh, axis=-1, keepdims=True) + EPS))
    o = jnp.concatenate(parts, axis=1) * ng_ref[...] * _silu(gate_ref[...])
    y = jnp.dot(_bf(o), w_ref[...], preferred_element_type=F32)
    _tail(_pick_x(rows, xa_ref, xb_ref), y, g1_ref[...], n2g_ref[...], sc2_ref[...], sh2_ref[...], x1_ref, h_ref, hp_ref)


def _outproj1(rows, xs, o_f, o_b, p1, norm_g, w_out_bf, n2g, mod4, layer):
    n = rows.n
    tok = lambda w: pl.BlockSpec((TM, w), lambda i: (i, 0))
    const = lambda a: pl.BlockSpec(a.shape, lambda i: (0,) * a.ndim)
    ng, n2 = norm_g.reshape(1, -1), n2g.reshape(1, -1)
    return pl.pallas_call(
        functools.partial(_outproj1_kernel, rows),
        grid=(n // TM,),
        in_specs=_x_specs(rows, xs) + [tok(D_C), tok(D_C), pl.BlockSpec((TM, D_C), lambda i: (i, 4)),
                  const(ng), const(w_out_bf),
                  rows.mod_spec(layer, 2, TM), const(n2), rows.mod_spec(layer, 4, TM), rows.mod_spec(layer, 3, TM)],
        out_specs=[tok(D_MODEL), tok(D_MODEL), tok(D_MODEL // 2)],
        out_shape=[jax.ShapeDtypeStruct((n, D_MODEL), F32), jax.ShapeDtypeStruct((n, D_MODEL), BF16),
                   jax.ShapeDtypeStruct((n, D_MODEL // 2), jnp.uint32)],
        compiler_params=_cparams(1),
        name="outproj1",
    )(*xs, o_f, o_b, p1, ng, w_out_bf, mod4, n2, mod4, mod4)


def _hgrn_kernel(n_tb, qf_ref, ff_ref, if_ref, qb_ref, fb_ref, ib_ref, lbp_ref, s0f_ref, s0b_ref,
                 trif_ref, trib_ref, prevf_ref, prevb_ref, of_ref, ob_ref, sff_ref, sfb_ref, s_scr):
    del prevf_ref, prevb_ref
    tb = pl.program_id(1)

    @pl.when(tb == 0)
    def _():
        for h in range(H_C):
            s_scr[0, h] = s0f_ref[0, h].T
            s_scr[1, h] = s0b_ref[0, h].T

    lbp = lbp_ref[...]
    e = jnp.exp(lbp - jnp.max(lbp, axis=0, keepdims=True))
    sm = e / jnp.sum(e, axis=0, keepdims=True)
    lb = (sm[0:1] + sm[1:2]) - sm[0:1]

    n_chunk = HG_TB // CHUNK
    ti = lax.broadcasted_iota(jnp.int32, (HG_TB, HG_TB), 0)
    si = lax.broadcasted_iota(jnp.int32, (HG_TB, HG_TB), 1)
    same = (ti // CHUNK) == (si // CHUNK)
    dirs = ((qf_ref, ff_ref, if_ref, of_ref, trif_ref, jnp.logical_and(same, ti >= si), CHUNK - 1, False),
            (qb_ref, fb_ref, ib_ref, ob_ref, trib_ref, jnp.logical_and(same, ti <= si), 0, True))

    staged = []
    for d, (q_ref, f_ref, i_ref, o_ref, tri_ref, causal, last_row, rev) in enumerate(dirs):
        q = _silu(q_ref[...])
        f = lb + (1.0 - lb) * _sigmoid(f_ref[...])
        k = 1.0 - f
        v = _bf(i_ref[...])
        g = jnp.log(f)
        g1 = _bf(g)
        g2 = _bf(g - g1.astype(F32))
        tri2 = tri_ref[...]
        b_parts, last_parts, dec = [], [], []
        for c in range(n_chunk):
            rc = slice(c * CHUNK, (c + 1) * CHUNK)
            bc = jnp.dot(tri2, jnp.concatenate([g1[rc], g2[rc]], axis=0), preferred_element_type=F32)
            b_parts.append(bc)
            last = bc[last_row:last_row + 1]
            last_parts.append(jnp.broadcast_to(last, bc.shape))
            dec.append(jnp.exp(last))
        b = jnp.concatenate(b_parts, axis=0)
        b_last = jnp.concatenate(last_parts, axis=0)
        staged.append((_bf(q * jnp.exp(b)), _bf(k * jnp.exp(-b)), _bf(k * jnp.exp(b_last - b)), v, dec))

    for h in range(H_C):
        sl = slice(h * DK_C, (h + 1) * DK_C)
        for d, (q_ref, f_ref, i_ref, o_ref, tri_ref, causal, last_row, rev) in enumerate(dirs):
            q_in, k_in, k_out, v, dec = staged[d]
            qh, vh = q_in[:, sl], v[:, sl]
            att = jnp.where(causal, _dot_nt(qh, k_in[:, sl]), 0.0)
            o_intra = jnp.dot(_bf(att), vh, preferred_element_type=F32)
            s_t = s_scr[d, h]
            for c in (range(n_chunk - 1, -1, -1) if rev else range(n_chunk)):
                rc = slice(c * CHUNK, (c + 1) * CHUNK)
                o_ref[rc, sl] = o_intra[rc] + _dot_nt(qh[rc], _bf(s_t))
                s_t = dec[c][:, sl] * s_t + _dot_tn(vh[rc], k_out[rc, sl])
            s_scr[d, h] = s_t

    @pl.when(tb == n_tb - 1)
    def _():
        for h in range(H_C):
            sff_ref[0, h] = s_scr[0, h].T
            sfb_ref[0, h] = s_scr[1, h].T


def _hgrn_scan(p1, row0, n_seq, seq, lb_params, s0_f, s0_b, prev_f, prev_b):
    n_tb = seq // HG_TB
    blk0 = row0 // HG_TB
    tri = np.tril(np.ones((CHUNK, CHUNK), np.float32))
    tri_f = jnp.asarray(np.concatenate([tri] * 2, axis=1), dtype=BF16)
    tri_b = jnp.asarray(np.concatenate([tri.T] * 2, axis=1), dtype=BF16)
    fwd = lambda col: pl.BlockSpec((HG_TB, D_C), lambda b, t: (blk0 + b * n_tb + t, col))
    bwd = lambda col: pl.BlockSpec((HG_TB, D_C), lambda b, t: (blk0 + b * n_tb + n_tb - 1 - t, col))
    st = pl.BlockSpec((1, H_C, DK_C, DV_C), lambda b, t: (b, 0, 0, 0))
    const = lambda a: pl.BlockSpec(a.shape, lambda b, t: (0,) * a.ndim)
    o_shape = jax.ShapeDtypeStruct(prev_f.shape, F32)
    st_shape = jax.ShapeDtypeStruct((n_seq, H_C, DK_C, DV_C), F32)
    any_spec = pl.BlockSpec(memory_space=pl.ANY)
    return pl.pallas_call(
        functools.partial(_hgrn_kernel, n_tb),
        grid=(n_seq, n_tb),
        in_specs=[fwd(0), fwd(1), fwd(3), bwd(0), bwd(2), bwd(3), const(lb_params), st, st,
                  const(tri_f), const(tri_b), any_spec, any_spec],
        out_specs=[fwd(0), bwd(0), st, st],
        out_shape=[o_shape, o_shape, st_shape, st_shape],
        input_output_aliases={11: 0, 12: 1},
        scratch_shapes=[pltpu.VMEM((2, H_C, DV_C, DK_C), F32)],
        compiler_params=_cparams(2),
        name="hgrn_scan",
    )(p1, p1, p1, p1, p1, p1, lb_params, s0_f, s0_b, tri_f, tri_b, prev_f, prev_b)


def _router_kernel(h_ref, rhi_ref, rlo_ref, bias_ref, sel_ref, eidx_ref, ew_ref, cnt_ref):
    x = h_ref[...]
    tm = x.shape[0]
    logits = _dot_nt(rhi_ref[...], x) + _dot_nt(rlo_ref[...], x)
    scores = _sigmoid(logits)
    biased = scores + bias_ref[...]
    per = N_EXPERTS // N_GROUPS
    sub = lax.broadcasted_iota(jnp.int32, (per, tm), 0)
    gs_rows = []
    for g in range(N_GROUPS):
        blk = biased[g * per:(g + 1) * per]
        m1 = jnp.max(blk, axis=0, keepdims=True)
        first = jnp.min(jnp.where(blk == m1, sub, per), axis=0, keepdims=True)
        m2 = jnp.max(jnp.where(sub == first, -jnp.inf, blk), axis=0, keepdims=True)
        gs_rows.append(m1 + m2)
    gs = jnp.concatenate(gs_rows, axis=0)
    gi = lax.broadcasted_iota(jnp.int32, gs.shape, 0)
    rank = jnp.zeros(gs.shape, jnp.int32)
    for s in range(1, N_GROUPS):
        other = pltpu.roll(gs, s, 0)
        oi = pltpu.roll(gi, s, 0)
        beats = jnp.logical_or(other > gs, jnp.logical_and(other == gs, oi < gi))
        rank = rank + jnp.where(beats, 1, 0)
    keep = jnp.where(rank < TOPK_GROUPS, 1.0, 0.0)
    emask = jnp.concatenate([jnp.broadcast_to(keep[g:g + 1], (per, tm)) for g in range(N_GROUPS)], axis=0)
    cur = jnp.where(emask > 0.0, biased, -jnp.inf)
    ei = lax.broadcasted_iota(jnp.int32, cur.shape, 0)
    sel = jnp.zeros(cur.shape, F32)
    idxs, vals = [], []
    for _ in range(TOP_K):
        m = jnp.max(cur, axis=0, keepdims=True)
        idx = jnp.min(jnp.where(cur == m, ei, N_EXPERTS), axis=0, keepdims=True)
        pick = ei == idx
        idxs.append(idx)
        vals.append(jnp.sum(jnp.where(pick, scores, 0.0), axis=0, keepdims=True))
        sel = jnp.where(pick, 1.0, sel)
        cur = jnp.where(pick, -jnp.inf, cur)
    w = jnp.concatenate(vals, axis=0)
    eidx_ref[...] = jnp.concatenate(idxs, axis=0)
    ew_ref[...] = w / jnp.sum(w, axis=0, keepdims=True) * ROUTED_SCALE
    sel_ref[...] = _bf(sel)

    @pl.when(pl.program_id(0) == 0)
    def _():
        cnt_ref[...] = jnp.zeros_like(cnt_ref)

    cnt_ref[...] += jnp.sum(sel, axis=1, keepdims=True)


def _router(hffn, router, bias):
    n = hffn.shape[0]
    r_t = router.T
    r_hi = _bf(r_t)
    r_lo = _bf(r_t - r_hi.astype(F32))
    const = lambda a: pl.BlockSpec(a.shape, lambda i: (0,) * a.ndim)
    b_col = bias.reshape(N_EXPERTS, 1)
    return pl.pallas_call(
        _router_kernel,
        grid=(n // TM,),
        in_specs=[pl.BlockSpec((TM, D_MODEL), lambda i: (i, 0)), const(r_hi), const(r_lo), const(b_col)],
        out_specs=[pl.BlockSpec((N_EXPERTS, TM), lambda i: (0, i)),
                   pl.BlockSpec((TOP_K, TM), lambda i: (0, i)),
                   pl.BlockSpec((TOP_K, TM), lambda i: (0, i)),
                   pl.BlockSpec((N_EXPERTS, LANES), lambda i: (0, 0))],
        out_shape=[jax.ShapeDtypeStruct((N_EXPERTS, n), BF16),
                   jax.ShapeDtypeStruct((TOP_K, n), jnp.int32),
                   jax.ShapeDtypeStruct((TOP_K, n), F32),
                   jax.ShapeDtypeStruct((N_EXPERTS, LANES), F32)],
        compiler_params=_cparams(1),
        name="router",
    )(hffn, r_hi, r_lo, b_col)


def _positions_kernel(sel_ref, eidx_ref, base_ref, upper_ref, pos_ref, carry_ref):
    @pl.when(pl.program_id(0) == 0)
    def _():
        carry_ref[...] = jnp.zeros_like(carry_ref)

    sel = sel_ref[...]
    rank = jnp.dot(sel, upper_ref[...], preferred_element_type=F32)
    pos_e = base_ref[:, 0:1] + carry_ref[:, 0:1] + rank
    ei = lax.broadcasted_iota(jnp.int32, pos_e.shape, 0)
    eidx = eidx_ref[...]
    rows = [jnp.sum(jnp.where(ei == eidx[k:k + 1], pos_e, 0.0), axis=0, keepdims=True) for k in range(TOP_K)]
    pos_ref[...] = jnp.concatenate(rows, axis=0).astype(jnp.int32)
    carry_ref[...] += jnp.sum(sel.astype(F32), axis=1, keepdims=True)


def _positions(sel, eidx, base):
    n = sel.shape[1]
    pb = POS_TB
    upper = jnp.asarray(np.triu(np.ones((pb, pb), np.float32), 1), dtype=BF16)
    return pl.pallas_call(
        _positions_kernel,
        grid=(n // pb,),
        in_specs=[pl.BlockSpec((N_EXPERTS, pb), lambda i: (0, i)),
                  pl.BlockSpec((TOP_K, pb), lambda i: (0, i)),
                  pl.BlockSpec((N_EXPERTS, LANES), lambda i: (0, 0)),
                  pl.BlockSpec((pb, pb), lambda i: (0, 0))],
        out_specs=pl.BlockSpec((TOP_K, pb), lambda i: (0, i)),
        out_shape=jax.ShapeDtypeStruct((TOP_K, n), jnp.int32),
        scratch_shapes=[pltpu.VMEM((N_EXPERTS, LANES), F32)],
        compiler_params=_cparams(1),
        name="positions",
    )(sel, eidx, base, upper)


def _pack_pairs(x):
    half = x.shape[1] // 2
    bits = lax.bitcast_convert_type(_bf(x).astype(F32), jnp.uint32)
    return (bits[:, :half] >> 16) | (bits[:, half:] & jnp.uint32(0xFFFF0000))


def _unpack_pairs(w):
    lo = lax.bitcast_convert_type(w << 16, F32)
    hi = lax.bitcast_convert_type(w & jnp.uint32(0xFFFF0000), F32)
    return jnp.concatenate([_bf(lo), _bf(hi)], axis=1)


def _sc_gather(table, idx):
    b, w = idx.shape[0], table.shape[1]
    n_workers = SC_CORES * SC_SUBCORES
    per_w = b // n_workers
    assert b % (n_workers * SC_CHUNK) == 0
    mesh = plsc.VectorSubcoreMesh(core_axis_name="c", subcore_axis_name="s")

    n_chunk = per_w // SC_CHUNK
    assert n_chunk % 2 == 0

    @functools.partial(
        pl.kernel, mesh=mesh, out_type=jax.ShapeDtypeStruct((b, w), table.dtype),
        scratch_types=[pltpu.VMEM((2, SC_CHUNK), jnp.int32), pltpu.VMEM((2, SC_CHUNK, w), table.dtype),
                       pltpu.SemaphoreType.DMA((2,))])
    def gather(table_hbm, idx_hbm, out_hbm, idx_v, rows_v, sems):
        wid = lax.axis_index("s") * SC_CORES + lax.axis_index("c")
        base = wid * per_w

        def start(c, slot):
            off = pl.multiple_of(base + c * SC_CHUNK, SC_CHUNK)
            pltpu.sync_copy(idx_hbm.at[pl.ds(off, SC_CHUNK)], idx_v.at[slot])
            pltpu.async_copy(table_hbm.at[idx_v.at[slot]], rows_v.at[slot], sems.at[slot])

        def finish(c, slot):
            off = pl.multiple_of(base + c * SC_CHUNK, SC_CHUNK)
            pltpu.make_async_copy(table_hbm.at[idx_v.at[slot]], rows_v.at[slot], sems.at[slot]).wait()
            pltpu.sync_copy(rows_v.at[slot], out_hbm.at[pl.ds(off, SC_CHUNK)])

        start(0, 0)

        @pl.loop(0, n_chunk, step=2)
        def _(c):
            start(c + 1, 1)
            finish(c, 0)

            @pl.when(c + 2 < n_chunk)
            def _():
                start(c + 2, 0)

            finish(c + 1, 1)

    return gather(table, idx)


def _sc_scatter(src, pos3, n_rows):
    n, w = src.shape
    n_workers = SC_CORES * SC_SUBCORES
    per_w = n // n_workers
    assert n % (n_workers * SC_CHUNK) == 0
    mesh = plsc.VectorSubcoreMesh(core_axis_name="c", subcore_axis_name="s")

    n_chunk = per_w // SC_CHUNK
    assert n_chunk % 2 == 0

    @functools.partial(
        pl.kernel, mesh=mesh, out_type=jax.ShapeDtypeStruct((n_rows, w), src.dtype),
        scratch_types=[pltpu.VMEM((2, TOP_K, SC_CHUNK), jnp.int32), pltpu.VMEM((2, SC_CHUNK, w), src.dtype),
                       pltpu.SemaphoreType.DMA((2,)), pltpu.SemaphoreType.DMA((2,))])
    def scatter(src_hbm, pos_hbm, out_hbm, idx_v, rows_v, ld_sems, sc_sems):
        wid = lax.axis_index("s") * SC_CORES + lax.axis_index("c")
        base = wid * per_w

        def loads(c, slot):
            off = pl.multiple_of(base + c * SC_CHUNK, SC_CHUNK)
            return (pltpu.make_async_copy(src_hbm.at[pl.ds(off, SC_CHUNK)], rows_v.at[slot], ld_sems.at[slot]),
                    pltpu.make_async_copy(pos_hbm.at[off // SC_CHUNK], idx_v.at[slot], ld_sems.at[slot]))

        def scatter_chunk(slot):
            copies = [pltpu.async_copy(rows_v.at[slot], out_hbm.at[idx_v.at[slot, k]], sc_sems.at[slot])
                      for k in range(TOP_K)]
            for cp in copies:
                cp.wait()

        def half_step(c, slot):
            for cp in loads(c, slot):
                cp.wait()

            @pl.when(c + 1 < n_chunk)
            def _():
                for cp in loads(c + 1, 1 - slot):
                    cp.start()

            scatter_chunk(slot)

        for cp in loads(0, 0):
            cp.start()

        @pl.loop(0, n_chunk, step=2)
        def _(c):
            half_step(c, 0)
            half_step(c + 1, 1)

    return scatter(src, pos3)


def _experts_kernel(te_ref, nu_ref, xs_ref, wg_ref, wu_ref, wd_ref, ys_ref, wg_bf, wu_bf, wd_bf):
    i = pl.program_id(0)
    active = i < nu_ref[0]

    @pl.when(jnp.logical_and(active, jnp.logical_or(i == 0, te_ref[i] != te_ref[jnp.maximum(i - 1, 0)])))
    def _():
        wg_bf[...] = _bf(wg_ref[0])
        wu_bf[...] = _bf(wu_ref[0])
        wd_bf[...] = _bf(wd_ref[0])

    @pl.when(active)
    def _():
        x = _unpack_pairs(xs_ref[...])
        act = _glu(x, wg_bf[...], wu_bf[...])
        ys_ref[...] = _pack_pairs(jnp.dot(_bf(act), wd_bf[...], preferred_element_type=F32))


def _experts(xs, tile_expert, n_used, mp, layer):
    n_tiles = xs.shape[0] // MOE_TILE
    half = D_MODEL // 2
    wspec = lambda shape: pl.BlockSpec((None, 1) + shape, lambda i, te, nu: (layer, te[i], 0, 0))
    return pl.pallas_call(
        _experts_kernel,
        grid_spec=pltpu.PrefetchScalarGridSpec(
            num_scalar_prefetch=2, grid=(n_tiles,),
            in_specs=[pl.BlockSpec((MOE_TILE, half), lambda i, te, nu: (i, 0)),
                      wspec((D_MODEL, D_EXPERT)), wspec((D_MODEL, D_EXPERT)), wspec((D_EXPERT, D_MODEL))],
            out_specs=pl.BlockSpec((MOE_TILE, half), lambda i, te, nu: (i, 0)),
            scratch_shapes=[pltpu.VMEM((D_MODEL, D_EXPERT), BF16), pltpu.VMEM((D_MODEL, D_EXPERT), BF16),
                            pltpu.VMEM((D_EXPERT, D_MODEL), BF16)]),
        out_shape=jax.ShapeDtypeStruct(xs.shape, jnp.uint32),
        compiler_params=_cparams(1),
        name="experts",
    )(tile_expert, n_used, xs, mp['wg'], mp['wu'], mp['wd'])


def _combine_kernel(h_ref, *refs):
    yg_refs = refs[:TOP_K]
    ew_ref, eye_ref, sg_ref, su_ref, sd_ref, x1_ref, g2_ref, o_ref = refs[TOP_K:]
    act = _glu(h_ref[...], sg_ref[...], su_ref[...])
    acc = jnp.dot(_bf(act), sd_ref[...], preferred_element_type=F32)
    ew = ew_ref[...]
    hi = _bf(ew)
    lo = _bf(ew - hi.astype(F32))
    ew_t = _dot_tn(hi, eye_ref[...]) + _dot_tn(lo, eye_ref[...])
    for k in range(TOP_K):
        acc = acc + ew_t[:, k:k + 1] * _unpack_pairs(yg_refs[k][...]).astype(F32)
    o_ref[...] = x1_ref[...] + g2_ref[...] * acc


def _combine(rows, hffn, yg, ew, mp, x1, mod4, layer, row0, n_out):
    half = D_MODEL // 2
    n_blk = rows.n // TM
    blk0 = row0 // TM
    const = lambda a: pl.BlockSpec(a.shape, lambda i: (0,) * a.ndim)
    tok = lambda w: pl.BlockSpec((TM, w), lambda i: (blk0 + i, 0))
    slot = lambda k: pl.BlockSpec((TM, half), lambda i: (k * n_blk + blk0 + i, 0))
    eye = jnp.eye(TOP_K, dtype=BF16)
    return pl.pallas_call(
        _combine_kernel,
        grid=(n_out // TM,),
        in_specs=[tok(D_MODEL)] + [slot(k) for k in range(TOP_K)]
                 + [pl.BlockSpec((TOP_K, TM), lambda i: (0, blk0 + i)), const(eye),
                    const(mp['sg']), const(mp['su']), const(mp['sd']), tok(D_MODEL),
                    rows.mod_spec(layer, 5, TM, blk0)],
        out_specs=pl.BlockSpec((TM, D_MODEL), lambda i: (i, 0)),
        out_shape=jax.ShapeDtypeStruct((n_out, D_MODEL), F32),
        compiler_params=_cparams(1),
        name=f"combine{layer}",
    )(hffn, *([yg] * TOP_K), ew, eye, mp['sg'], mp['su'], mp['sd'], x1, mod4)


def _moe(rows, hffn, hpack, x1, router, bias, mp, mod4, layer, out_ranges):
    n = rows.n
    sel, eidx, ew, cnt = _router(hffn, router, bias)
    counts = cnt[:, 0].astype(jnp.int32)
    padded = (counts + MOE_TILE - 1) // MOE_TILE * MOE_TILE
    ends = jnp.cumsum(padded)
    n_rows = n * TOP_K + N_EXPERTS * MOE_TILE
    n_tiles = n_rows // MOE_TILE
    base = jnp.broadcast_to((ends - padded).astype(F32)[:, None], (N_EXPERTS, LANES))
    tile_start = jnp.arange(n_tiles, dtype=jnp.int32) * MOE_TILE
    tile_expert = jnp.minimum(jnp.sum((ends[None, :] <= tile_start[:, None]).astype(jnp.int32), axis=1),
                              N_EXPERTS - 1)
    n_used = (ends[-1:] // MOE_TILE).astype(jnp.int32)
    pos = _positions(sel, eidx, base)
    pos3 = pos.reshape(TOP_K, n // SC_CHUNK, SC_CHUNK).transpose(1, 0, 2)
    xs = _sc_scatter(hpack, pos3, n_rows)
    ys = _experts(xs, tile_expert, n_used, mp, layer)
    yg = _sc_gather(ys, pos.reshape(-1))
    return [_combine(rows, hffn, yg, ew, mp, x1, mod4, layer, row0, n_out) for row0, n_out in out_ranges]


def _glu(x, wg, wu):
    hg = jnp.dot(x, wg, preferred_element_type=F32)
    hu = jnp.dot(x, wu, preferred_element_type=F32)
    return _silu(hg) * hu


def kernel(x_prompt, x_sample, c, c_ctx, cache_attn_k, cache_attn_v, state_rwkv_fwd, state_rwkv_bwd,
           state_hgrn_fwd, state_hgrn_bwd, norm1_g, norm2_g, mod_w, mod_b, ab_w_in, ab_w_out, attn_q_norm,
           attn_k_norm, attn_sink, rwkv_mu, rwkv_w0, rwkv_w2, rwkv_a0, rwkv_a2, rwkv_g2, rwkv_k_k, rwkv_k_a,
           rwkv_r_k, rwkv_ln_w, rwkv_ln_b, hgrn_w_in, hgrn_w_out, hgrn_lower_bounds, hgrn_norm_g, moe_router,
           moe_bias, moe_w_gate, moe_w_up, moe_w_down, moe_shared_gate, moe_shared_up, moe_shared_down):
    n_cseq, cseq, _ = x_prompt.shape
    n_lseq, lseq, _ = x_sample.shape
    depth = mod_w.shape[0]
    assert depth == 2 and n_lseq + 1 <= SUBLANES
    assert cseq == TM and lseq % TM == 0 and lseq % HG_TB == 0 and cseq % HG_TB == 0
    assert n_cseq % RW_BB == 0 and n_lseq % RW_BB == 0 and (n_cseq * cseq) % (lseq * RW_BB) == 0
    rows = _Rows(n_cseq * cseq, n_lseq * lseq, lseq)
    assert rows.n % MOE_TILE == 0 and lseq % MOE_TILE == 0 and rows.n_ctx % MOE_TILE == 0
    kv_w = KV_A * HD_A

    xs = (x_prompt.reshape(rows.n_ctx, D_MODEL), x_sample.reshape(rows.n_lat, D_MODEL))
    cvecs = jnp.concatenate([c_ctx[None, :], c, jnp.zeros((SUBLANES - 1 - n_lseq, D_MODEL), F32)], axis=0)
    mod4 = _modulation(cvecs, mod_w, mod_b).reshape(depth, SUBLANES, 1, 6 * D_MODEL)

    ones_q = _block_ones(W_A, HD_A)
    ones_k = _block_ones(kv_w, HD_A)
    ones_b = _block_ones(W_B, HD_B)
    ones_pair = _block_ones(LANES, HD_B)[:LANES]
    ones_pair = jnp.kron(jnp.eye(2, dtype=BF16), ones_pair)
    cos_t, sin_t = _rope_tables(lseq)

    def moe(l, hffn, hpack, x1, out_ranges):
        mp = {'wg': moe_w_gate, 'wu': moe_w_up, 'wd': moe_w_down,
              'sg': _bf(moe_shared_gate[l]), 'su': _bf(moe_shared_up[l]), 'sd': _bf(moe_shared_down[l])}
        return _moe(rows, hffn, hpack, x1, moe_router[l], moe_bias[l], mp, mod4, l, out_ranges)

    assert W_A == W_B
    all_rows = jnp.zeros((rows.n, W_B), F32)

    pr = {'mu': rwkv_mu[0], 'w0': rwkv_w0[0], 'w2': rwkv_w2[0], 'a0': rwkv_a0[0], 'a2': rwkv_a2[0],
          'g2': rwkv_g2[0], 'k_k': rwkv_k_k[0], 'k_a': rwkv_k_a[0], 'r_k': rwkv_r_k[0].reshape(-1),
          'ln_w': rwkv_ln_w[0], 'ln_b': rwkv_ln_b[0]}
    p_att, p_rw = _inproj(rows, xs, norm1_g[0], mod4, 0, _bf(ab_w_in[0]), (ATT_IN, RWKV_IN))
    qg_t = jnp.tile(attn_q_norm[0], H_A).reshape(1, W_A)
    kg_t = jnp.tile(attn_k_norm[0], KV_A).reshape(1, kv_w)
    o_att, new_k, new_v = _ctx_attention(p_att, n_cseq, cseq, qg_t, kg_t, attn_sink[0], ones_q, ones_k, all_rows)
    past = cache_attn_k.shape[2]
    o_att = _lat_attention(p_att, rows.n_ctx // lseq, n_lseq, lseq, qg_t, kg_t, attn_sink[0], ones_q, ones_k,
                           cos_t, sin_t, cache_attn_k[:, 0].reshape(n_lseq, past, kv_w),
                           cache_attn_v[:, 0].reshape(n_lseq, past, kv_w), o_att)

    pp = _rwkv_prep(rows, p_rw, pr, ones_b)
    zero_st = jnp.zeros((n_cseq, H_B // 2, HD_B, LANES), F32)
    o_f, o_b, sf_c, sb_c = _rwkv_scan(pp, 0, n_cseq, cseq, zero_st, zero_st, ones_pair, all_rows, all_rows)
    o_f, o_b, _, _ = _rwkv_scan(pp, rows.n_ctx, n_lseq, lseq, _state_to_pairs(state_rwkv_fwd[:, 0]),
                                _state_to_pairs(state_rwkv_bwd[:, 0]), ones_pair, o_f, o_b)
    x1, hffn, hpack = _outproj0(rows, xs, o_att, o_f, o_b, pp, pr, ones_b, _bf(ab_w_out[0]), norm2_g[0], mod4, 0)
    (x,) = moe(0, hffn, hpack, x1, [(0, rows.n)])

    (p1,) = _inproj(rows, (x, x), norm1_g[1], mod4, 1, _bf(hgrn_w_in[0]), (IN_C,))
    zero_h = jnp.zeros((n_cseq, H_C, DK_C, DV_C), F32)
    all_rows_c = jnp.zeros((rows.n, D_C), F32)
    h_f, h_b, hsf_c, hsb_c = _hgrn_scan(p1, 0, n_cseq, cseq, hgrn_lower_bounds, zero_h, zero_h,
                                        all_rows_c, all_rows_c)
    h_f, h_b, _, _ = _hgrn_scan(p1, rows.n_ctx, n_lseq, lseq, hgrn_lower_bounds,
                                state_hgrn_fwd[:, 0], state_hgrn_bwd[:, 0], h_f, h_b)
    x1, hffn, hpack = _outproj1(rows, (x, x), h_f, h_b, p1, hgrn_norm_g[0], _bf(hgrn_w_out[0]), norm2_g[1],
                                mod4, 1)
    y_c, y_l = moe(1, hffn, hpack, x1, [(0, rows.n_ctx), (rows.n_ctx, rows.n_lat)])

    y_prompt = y_c.reshape(n_cseq, cseq, D_MODEL)
    y_sample = y_l.reshape(n_lseq, lseq, D_MODEL)
    return (y_prompt, y_sample,
            new_k.reshape(n_cseq, 1, cseq, KV_A, HD_A), new_v.reshape(n_cseq, 1, cseq, KV_A, HD_A),
            _pairs_to_state(sf_c)[:, None], _pairs_to_state(sb_c)[:, None],
            hsf_c[:, None], hsb_c[:, None])
```

```python
import functools

import numpy as np
import jax
import jax.numpy as jnp
from jax import lax
from jax.experimental import pallas as pl
from jax.experimental.pallas import tpu as pltpu
from jax.experimental.pallas import tpu_sc as plsc

F32 = jnp.float32
BF16 = jnp.bfloat16

D_MODEL = 1024
GRID_W = 64
H_A = 8
KV_A = 2
G_A = H_A // KV_A
HD_A = 64
W_A = H_A * HD_A
WINDOW = 128
QBLK = 128
ROPE_BASE = 10000.0
ATTN_SCALE = HD_A ** -0.5
NEG_INF = -1e30
H_B = 8
HD_B = 64
W_B = H_B * HD_B
LORA_W = 64
LORA_A = 64
LORA_G = 128
GN_EPS = 64e-5
ATT_IN = W_A + 2 * KV_A * HD_A
RWKV_IN = 3 * W_B + LORA_W + LORA_A + LORA_G
IN_AB = ATT_IN + RWKV_IN
H_C = 8
DK_C = 128
DV_C = 128
D_C = H_C * DV_C
CHUNK = 64
IN_C = 5 * D_C
N_EXPERTS = 64
TOP_K = 8
N_GROUPS = 8
TOPK_GROUPS = 4
D_EXPERT = 256
ROUTED_SCALE = 2.5
EPS = 1e-6

LANES = 128
SUBLANES = 8
VMEM_LIMIT = 52 * 1024 * 1024

TM = 512
PREP_TM = 256
IN1_TM = 256
RW_TB = 128
RW_BB = 4
RW_GROUP_BB = 4
HG_TB = 256
MOE_TILE = 512
POS_TB = 512
SC_CORES = 2
SC_SUBCORES = 16
SC_CHUNK = 64


def _cparams(n_axes):
    return pltpu.CompilerParams(dimension_semantics=("arbitrary",) * n_axes,
                                vmem_limit_bytes=VMEM_LIMIT)


def _bf(x):
    return x.astype(BF16)


def _split2(x):
    hi = lax.bitcast_convert_type(
        lax.bitcast_convert_type(x, jnp.uint32) & jnp.uint32(0xFFFF0000), F32)
    return hi, x - hi


def _seg_sum(x, ones2):
    hi, lo = _split2(x)
    return jnp.dot(jnp.concatenate([_bf(hi), _bf(lo)], axis=1), ones2,
                   preferred_element_type=F32)


def _dot_nt(a, b):
    return lax.dot_general(a, b, (((1,), (1,)), ((), ())), preferred_element_type=F32)


def _dot_tn(a, b):
    return lax.dot_general(a, b, (((0,), (0,)), ((), ())), preferred_element_type=F32)


def _sigmoid(x):
    return 1.0 / (1.0 + jnp.exp(-x))


def _silu(x):
    return x * _sigmoid(x)


def _chunks(seq, n):
    seq = list(seq)
    return [seq[i:i + n] for i in range(0, len(seq), n)]


def _block_ones(width, seg):
    idx = np.arange(width) // seg
    bd = (idx[:, None] == idx[None, :]).astype(np.float32)
    return jnp.asarray(np.concatenate([bd, bd], axis=0), dtype=BF16)


def _mod_kernel(c_ref, w_ref, b_ref, o_ref):
    s = _silu(c_ref[...])
    o_ref[0] = jnp.dot(_bf(s), _bf(w_ref[0]), preferred_element_type=F32) + b_ref[0]


def _modulation(cvecs, mod_w, mod_b):
    depth = mod_w.shape[0]
    n_col = 6 * D_MODEL // D_MODEL
    return pl.pallas_call(
        _mod_kernel,
        grid=(depth, n_col),
        in_specs=[pl.BlockSpec((SUBLANES, D_MODEL), lambda l, j: (0, 0)),
                  pl.BlockSpec((1, D_MODEL, D_MODEL), lambda l, j: (l, 0, j)),
                  pl.BlockSpec((1, 1, D_MODEL), lambda l, j: (l, 0, j))],
        out_specs=pl.BlockSpec((1, SUBLANES, D_MODEL), lambda l, j: (l, 0, j)),
        out_shape=jax.ShapeDtypeStruct((depth, SUBLANES, 6 * D_MODEL), F32),
        compiler_params=_cparams(2),
        name="modulation",
    )(cvecs, mod_w, mod_b.reshape(depth, 1, 6 * D_MODEL))


class _Rows:
    def __init__(self, n_ctx, n_lat, lat_seq):
        self.n_ctx, self.n_lat, self.lat_seq = n_ctx, n_lat, lat_seq
        self.n = n_ctx + n_lat

    def mod_row(self, i, tm):
        nctx_blk = self.n_ctx // tm
        per_seq = self.lat_seq // tm
        return jnp.where(i < nctx_blk, 0, 1 + (i - nctx_blk) // per_seq)

    def mod_spec(self, layer, chunk, tm, blk0=0):
        return pl.BlockSpec((None, None, 1, D_MODEL),
                            lambda i, *_: (layer, self.mod_row(i + blk0, tm), 0, chunk))


def _rms_mod(x, g, sc, sh):
    ms = jnp.mean(x * x, axis=-1, keepdims=True)
    return x * lax.rsqrt(ms + EPS) * g * (1.0 + sc) + sh


def _x_specs(rows, xs, tm=TM):
    xa, xb = xs
    nctx_blk = rows.n_ctx // tm
    lat0 = nctx_blk if xb.shape[0] == rows.n else 0
    return [pl.BlockSpec((tm, D_MODEL), lambda i: (jnp.minimum(i, nctx_blk - 1), 0)),
            pl.BlockSpec((tm, D_MODEL), lambda i: (jnp.maximum(i - nctx_blk, 0) + lat0, 0))]


def _pick_x(rows, xa_ref, xb_ref):
    return jnp.where(pl.program_id(0) < rows.n_ctx // xa_ref.shape[0], xa_ref[...], xb_ref[...])


def _inproj_kernel(rows, splits, xa_ref, xb_ref, g_ref, sh_ref, sc_ref, w_ref, *o_refs):
    h = _rms_mod(_pick_x(rows, xa_ref, xb_ref), g_ref[...], sc_ref[...], sh_ref[...])
    p = jnp.dot(_bf(h), w_ref[...], preferred_element_type=F32)
    lo = 0
    for o_ref, width in zip(o_refs, splits):
        o_ref[...] = p[:, lo:lo + width]
        lo += width


def _inproj(rows, xs, g, mod4, layer, w_bf, splits, tm):
    n_out = w_bf.shape[1]
    return pl.pallas_call(
        functools.partial(_inproj_kernel, rows, splits),
        grid=(rows.n // tm,),
        in_specs=_x_specs(rows, xs, tm) + [
            pl.BlockSpec((1, D_MODEL), lambda i: (0, 0)),
            rows.mod_spec(layer, 0, tm),
            rows.mod_spec(layer, 1, tm),
            pl.BlockSpec((D_MODEL, n_out), lambda i: (0, 0))],
        out_specs=[pl.BlockSpec((tm, wd), lambda i: (i, 0)) for wd in splits],
        out_shape=[jax.ShapeDtypeStruct((rows.n, wd), F32) for wd in splits],
        compiler_params=_cparams(1),
        name=f"inproj{layer}",
    )(*xs, g.reshape(1, D_MODEL), mod4, mod4, w_bf)


def _head_rms(x, gain_t, ones2):
    ms = _seg_sum(x * x, ones2) * (1.0 / HD_A)
    return x * lax.rsqrt(ms + EPS) * gain_t


def _sink_softmax_pv(parts, sink):
    m = jnp.maximum(functools.reduce(jnp.maximum, [jnp.max(s, axis=-1, keepdims=True) for s, _ in parts]), sink)
    den = jnp.exp(sink - m)
    acc = None
    for s, v in parts:
        p = jnp.exp(s - m)
        den = den + jnp.sum(p, axis=-1, keepdims=True)
        pv = jnp.dot(_bf(p), v, preferred_element_type=F32)
        acc = pv if acc is None else acc + pv
    return acc / den


def _ctx_attn_kernel(p_ref, qg_ref, kg_ref, sink_ref, ones_q_ref, ones_k_ref, prev_ref, o_ref, k_ref, v_ref):
    del prev_ref
    p = p_ref[...]
    q = _head_rms(p[:, :W_A], qg_ref[...], ones_q_ref[...]) * ATTN_SCALE
    k = _head_rms(p[:, W_A:W_A + KV_A * HD_A], kg_ref[...], ones_k_ref[...])
    v = p[:, W_A + KV_A * HD_A:ATT_IN]
    k_ref[0] = k
    v_ref[0] = v
    qb, kb, vb = _bf(q), _bf(k), _bf(v)
    outs = []
    for h in range(H_A):
        j = h // G_A
        s = _dot_nt(qb[:, h * HD_A:(h + 1) * HD_A], kb[:, j * HD_A:(j + 1) * HD_A])
        outs.append(_sink_softmax_pv([(s, vb[:, j * HD_A:(j + 1) * HD_A])], sink_ref[h]))
    o_ref[...] = jnp.concatenate(outs, axis=1)


def _ctx_attention(p_att, n_seq, seq, qg_t, kg_t, sink, ones_q, ones_k, prev):
    kv_w = KV_A * HD_A
    return pl.pallas_call(
        _ctx_attn_kernel,
        grid=(n_seq,),
        in_specs=[pl.BlockSpec((seq, ATT_IN), lambda b: (b, 0)),
                  pl.BlockSpec((1, W_A), lambda b: (0, 0)),
                  pl.BlockSpec((1, kv_w), lambda b: (0, 0)),
                  pl.BlockSpec(memory_space=pltpu.SMEM),
                  pl.BlockSpec(ones_q.shape, lambda b: (0, 0)),
                  pl.BlockSpec(ones_k.shape, lambda b: (0, 0)),
                  pl.BlockSpec(memory_space=pl.ANY)],
        out_specs=[pl.BlockSpec((seq, W_A), lambda b: (b, 0)),
                   pl.BlockSpec((1, seq, kv_w), lambda b: (b, 0, 0)),
                   pl.BlockSpec((1, seq, kv_w), lambda b: (b, 0, 0))],
        input_output_aliases={6: 0},
        out_shape=[jax.ShapeDtypeStruct(prev.shape, F32),
                   jax.ShapeDtypeStruct((n_seq, seq, kv_w), F32),
                   jax.ShapeDtypeStruct((n_seq, seq, kv_w), F32)],
        compiler_params=_cparams(1),
        name="ctx_attention",
    )(p_att, qg_t, kg_t, sink, ones_q, ones_k, prev)


def _rope(x, cos_t, sin_t):
    lane = lax.broadcasted_iota(jnp.int32, cos_t.shape, 1)
    low = (lane % 32) < 16
    outs = []
    for s in range(x.shape[1] // LANES):
        xs = x[:, s * LANES:(s + 1) * LANES]
        partner = jnp.where(low, pltpu.roll(xs, LANES - 16, 1), pltpu.roll(xs, 16, 1))
        outs.append(xs * cos_t + partner * sin_t)
    return outs[0] if len(outs) == 1 else jnp.concatenate(outs, axis=1)


def _lat_attn_kernel(seq, p_ref, qg_ref, kg_ref, sink_ref, ones_q_ref, ones_k_ref, cos_ref, sin_ref,
                     kc_ref, vc_ref, prev_ref, o_ref, q_scr, k_scr, v_scr):
    del prev_ref
    kv_w = KV_A * HD_A
    p = p_ref[...]
    q = _head_rms(p[:, :W_A], qg_ref[...], ones_q_ref[...])
    k = _head_rms(p[:, W_A:W_A + kv_w], kg_ref[...], ones_k_ref[...])
    qr = _bf(_rope(q, cos_ref[...], sin_ref[...]) * ATTN_SCALE)
    kr = _bf(_rope(k, cos_ref[...], sin_ref[...]))
    vb = _bf(p[:, W_A + kv_w:ATT_IN])
    for h in range(H_A):
        q_scr[h] = qr[:, h * HD_A:(h + 1) * HD_A]
    for j in range(KV_A):
        k_scr[j] = kr[:, j * HD_A:(j + 1) * HD_A]
        v_scr[j] = vb[:, j * HD_A:(j + 1) * HD_A]
    kc = _bf(kc_ref[0])
    vc = _bf(vc_ref[0])
    n_local = 3 * QBLK
    grp = lax.broadcasted_iota(jnp.int32, (G_A * QBLK, 1), 0) // QBLK

    def block(i, carry):
        q0 = pl.multiple_of(i * QBLK, QBLK)
        start = pl.multiple_of(jnp.clip((i - 1) * QBLK, 0, seq - n_local), QBLK)
        ipos = q0 + lax.broadcasted_iota(jnp.int32, (G_A * QBLK, n_local), 0) % QBLK
        jpos = start + lax.broadcasted_iota(jnp.int32, (G_A * QBLK, n_local), 1)
        band = jnp.abs(jpos - ipos) <= WINDOW
        outs = []
        for j in range(KV_A):
            qs = jnp.concatenate([q_scr[j * G_A + g, pl.ds(q0, QBLK), :] for g in range(G_A)], axis=0)
            kl = k_scr[j, pl.ds(start, n_local), :]
            vl = v_scr[j, pl.ds(start, n_local), :]
            sink = jnp.zeros((G_A * QBLK, 1), F32)
            for g in range(G_A):
                sink = jnp.where(grp == g, sink_ref[j * G_A + g], sink)
            s_loc = jnp.where(band, _dot_nt(qs, kl), NEG_INF)
            s_ctx = _dot_nt(qs, kc[:, j * HD_A:(j + 1) * HD_A])
            o = _sink_softmax_pv([(s_loc, vl), (s_ctx, vc[:, j * HD_A:(j + 1) * HD_A])], sink)
            outs.extend(o[g * QBLK:(g + 1) * QBLK] for g in range(G_A))
        o_ref[pl.ds(q0, QBLK), :] = jnp.concatenate(outs, axis=1)
        return carry

    lax.fori_loop(0, seq // QBLK, block, 0)


def _lat_attention(p_att, row_blk0, n_seq, seq, qg_t, kg_t, sink, ones_q, ones_k, cos_t, sin_t, kc, vc, prev):
    kv_w = KV_A * HD_A
    past = kc.shape[1]
    return pl.pallas_call(
        functools.partial(_lat_attn_kernel, seq),
        grid=(n_seq,),
        in_specs=[pl.BlockSpec((seq, ATT_IN), lambda b: (row_blk0 + b, 0)),
                  pl.BlockSpec((1, W_A), lambda b: (0, 0)),
                  pl.BlockSpec((1, kv_w), lambda b: (0, 0)),
                  pl.BlockSpec(memory_space=pltpu.SMEM),
                  pl.BlockSpec(ones_q.shape, lambda b: (0, 0)),
                  pl.BlockSpec(ones_k.shape, lambda b: (0, 0)),
                  pl.BlockSpec((seq, LANES), lambda b: (0, 0)),
                  pl.BlockSpec((seq, LANES), lambda b: (0, 0)),
                  pl.BlockSpec((1, past, kv_w), lambda b: (b, 0, 0)),
                  pl.BlockSpec((1, past, kv_w), lambda b: (b, 0, 0)),
                  pl.BlockSpec(memory_space=pl.ANY)],
        out_specs=pl.BlockSpec((seq, W_A), lambda b: (row_blk0 + b, 0)),
        out_shape=jax.ShapeDtypeStruct(prev.shape, F32),
        input_output_aliases={10: 0},
        scratch_shapes=[pltpu.VMEM((H_A, seq, HD_A), BF16), pltpu.VMEM((KV_A, seq, HD_A), BF16),
                        pltpu.VMEM((KV_A, seq, HD_A), BF16)],
        compiler_params=_cparams(1),
        name="lat_attention",
    )(p_att, qg_t, kg_t, sink, ones_q, ones_k, cos_t, sin_t, kc, vc, prev)


def _rope_tables(seq):
    pos = np.arange(seq)
    row = (pos // GRID_W).astype(np.float32)
    col = (pos % GRID_W).astype(np.float32)
    d_axis = HD_A // 2
    inv = (ROPE_BASE ** (-np.arange(0, d_axis, 2, dtype=np.float32) / d_axis)).astype(np.float32)
    cos_h = np.zeros((seq, HD_A), np.float32)
    sin_h = np.zeros((seq, HD_A), np.float32)
    for seg, p_ in enumerate((row, col)):
        ang = (p_[:, None] * inv[None, :]).astype(np.float32)
        c, s = np.cos(ang), np.sin(ang)
        base = seg * d_axis
        cos_h[:, base:base + d_axis // 2] = c
        cos_h[:, base + d_axis // 2:base + d_axis] = c
        sin_h[:, base:base + d_axis // 2] = -s
        sin_h[:, base + d_axis // 2:base + d_axis] = s
    rep = LANES // HD_A
    return jnp.asarray(np.tile(cos_h, (1, rep))), jnp.asarray(np.tile(sin_h, (1, rep)))


def _rwkv_prep_kernel(rows, x_ref, prev_ref, next_ref, mu_ref, kk_ref, ka_ref, rk_ref, w0_ref, w2_ref,
                      a0_ref, a2_ref, g2_ref, ones_ref,
                      nkk_ref, r_ref, v_ref, g_ref, bonus_ref,
                      wf_ref, kaf_ref, kdf_ref, wb_ref, kab_ref, kdb_ref):
    i = pl.program_id(0)
    nctx_blk = rows.n_ctx // PREP_TM
    per_seq = rows.lat_seq // PREP_TM
    is_ctx = i < nctx_blk
    first = jnp.logical_or(is_ctx, (i - nctx_blk) % per_seq == 0)
    last = jnp.logical_or(is_ctx, (i - nctx_blk) % per_seq == per_seq - 1)
    x = x_ref[...]
    ridx = lax.broadcasted_iota(jnp.int32, x.shape, 0)
    prev_row = jnp.where(first, 0.0, prev_ref[SUBLANES - 1:SUBLANES, :])
    next_row = jnp.where(last, 0.0, next_ref[0:1, :])
    xm1 = jnp.where(ridx == 0, prev_row, pltpu.roll(x, 1, 0))
    xp1 = jnp.where(ridx == PREP_TM - 1, next_row, pltpu.roll(x, PREP_TM - 1, 0))
    pw = x + (0.5 * (xm1 + xp1) - x) * mu_ref[...]

    r = pw[:, 0:W_B]
    k = pw[:, W_B:2 * W_B]
    v = pw[:, 2 * W_B:3 * W_B]
    wd = pw[:, 3 * W_B:3 * W_B + LORA_W]
    ad = pw[:, 3 * W_B + LORA_W:3 * W_B + LORA_W + LORA_A]
    gd = pw[:, 3 * W_B + LORA_W + LORA_A:]
    ones2 = ones_ref[...]

    kk = k * kk_ref[...]
    kk = kk / jnp.maximum(jnp.sqrt(_seg_sum(kk * kk, ones2)), 1e-12)
    nkk_ref[...] = -kk
    r_ref[...] = r
    v_ref[...] = v
    g_ref[...] = jnp.dot(_bf(_sigmoid(gd)), g2_ref[...], preferred_element_type=F32)
    tw = _bf(jnp.tanh(wd))
    adb = _bf(ad)
    bonus = jnp.zeros_like(r)
    for d, (w_o, ka_o, kd_o) in enumerate(((wf_ref, kaf_ref, kdf_ref), (wb_ref, kab_ref, kdb_ref))):
        z = -(w0_ref[d:d + 1, :] + jnp.dot(tw, w2_ref[d], preferred_element_type=F32))
        softplus = jnp.maximum(z, 0.0) + jnp.log(1.0 + jnp.exp(-jnp.abs(z)))
        w_o[...] = jnp.exp(-jnp.exp(-softplus - 0.5))
        a = _sigmoid(a0_ref[d:d + 1, :] + jnp.dot(adb, a2_ref[d], preferred_element_type=F32))
        kd = k * (1.0 + (a - 1.0) * ka_ref[...])
        ka_o[...] = kk * a
        kd_o[...] = kd
        bonus = bonus + _seg_sum(r * kd * rk_ref[...], ones2) * v
    bonus_ref[...] = bonus


def _rwkv_prep(rows, p_rw, pr, ones_b):
    n = rows.n
    n_halo = n // SUBLANES
    blk_halo = PREP_TM // SUBLANES
    row = lambda a: a.reshape(1, -1)
    full = lambda a: pl.BlockSpec(a.shape, lambda i: (0,) * a.ndim)
    consts = [row(pr['mu']), row(pr['k_k']), row(pr['k_a']), row(pr['r_k']), pr['w0'], _bf(pr['w2']),
              pr['a0'], _bf(pr['a2']), _bf(pr['g2']), ones_b]
    outs = pl.pallas_call(
        functools.partial(_rwkv_prep_kernel, rows),
        grid=(n // PREP_TM,),
        in_specs=[pl.BlockSpec((PREP_TM, RWKV_IN), lambda i: (i, 0)),
                  pl.BlockSpec((SUBLANES, RWKV_IN), lambda i: (jnp.maximum(i * blk_halo - 1, 0), 0)),
                  pl.BlockSpec((SUBLANES, RWKV_IN), lambda i: (jnp.minimum((i + 1) * blk_halo, n_halo - 1), 0))]
                 + [full(a) for a in consts],
        out_specs=[pl.BlockSpec((PREP_TM, W_B), lambda i: (i, 0))] * 11,
        out_shape=[jax.ShapeDtypeStruct((n, W_B), F32)] * 11,
        compiler_params=_cparams(1),
        name="rwkv_prep",
    )(p_rw, p_rw, p_rw, *consts)
    names = ('nkk', 'r', 'v', 'g', 'bonus', 'w_f', 'ka_f', 'kd_f', 'w_b', 'ka_b', 'kd_b')
    return dict(zip(names, outs))


def _rwkv_scan_kernel(n_tb, nkkf_ref, rf_ref, vf_ref, wf_ref, kaf_ref, kdf_ref,
                      nkkb_ref, rb_ref, vb_ref, wb_ref, kab_ref, kdb_ref,
                      s0f_ref, s0b_ref, ones_ref, prevf_ref, prevb_ref,
                      of_ref, ob_ref, sff_ref, sfb_ref, s_scr, vt_scr):
    del prevf_ref, prevb_ref
    tb = pl.program_id(1)
    n_pair = H_B // 2
    half = RW_TB // 2
    dirs = ((nkkf_ref, rf_ref, vf_ref, wf_ref, kaf_ref, kdf_ref, of_ref, False),
            (nkkb_ref, rb_ref, vb_ref, wb_ref, kab_ref, kdb_ref, ob_ref, True))

    @pl.when(tb == 0)
    def _():
        s_scr[0] = s0f_ref[...]
        s_scr[1] = s0b_ref[...]

    lane = lax.broadcasted_iota(jnp.int32, (HD_B, LANES), 1)
    for d, refs in enumerate(dirs):
        v_ref = refs[2]
        for bb in range(RW_BB):
            for p in range(n_pair):
                vt = v_ref[bb, :, p * LANES:(p + 1) * LANES].T
                top, bot = vt[:HD_B], vt[HD_B:]
                for s in range(2):
                    if s == 0:
                        t2 = jnp.where(lane < HD_B, top, pltpu.roll(bot, HD_B, 1))
                    else:
                        t2 = jnp.where(lane < HD_B, pltpu.roll(top, HD_B, 1), bot)
                    vt_scr[d, bb, p, s] = t2

    ones2 = ones_ref[...]
    row8 = lax.broadcasted_iota(jnp.int32, (SUBLANES, LANES), 0)
    lane8 = lax.broadcasted_iota(jnp.int32, (SUBLANES, LANES), 1)
    sel_r = jnp.logical_or(jnp.logical_and(row8 % 2 == 0, lane8 < HD_B),
                           jnp.logical_and(row8 % 2 == 1, lane8 >= HD_B))

    def row_of(rev, tt):
        return RW_TB - 1 - tt if rev else tt

    def emit_output(d, bb, tau):
        r_ref, o_ref = dirs[d][1], dirs[d][6]
        r = r_ref[bb, pl.ds(tau, 1), :]
        r8 = jnp.zeros((SUBLANES, LANES), F32)
        for p in range(n_pair):
            rp = jnp.broadcast_to(r[:, p * LANES:(p + 1) * LANES], (SUBLANES, LANES))
            r8 = jnp.where(jnp.logical_and(sel_r, row8 // 2 == p), rp, r8)
        s_all = jnp.concatenate([_bf(s_scr[d, bb, p]) for p in range(n_pair)], axis=0)
        o8 = _dot_nt(_bf(r8), s_all)
        o_parts = []
        for p in range(n_pair):
            for h in range(2):
                o_parts.append(o8[2 * p + h:2 * p + h + 1, p * HD_B:(p + 1) * HD_B])
        o_ref[bb, pl.ds(tau, 1), :] = jnp.concatenate(o_parts, axis=1)

    groups = [(d, bbs) for d in range(2) for bbs in _chunks(range(RW_BB), RW_GROUP_BB)]

    def step(tt, carry):
        tt_prev = jnp.maximum(tt - 1, 0)
        reds = []
        for d, bbs in groups:
            rev = dirs[d][7]
            tau = row_of(rev, tt)
            sub = tau // half
            lt = tau % half
            mask = jnp.logical_or(lane == lt, lane == lt + HD_B)
            lhs = []
            for bb in bbs:
                emit_output(d, bb, row_of(rev, tt_prev))
                nkk = dirs[d][0][bb, pl.ds(tau, 1), :]
                for p in range(n_pair):
                    prod = s_scr[d, bb, p] * nkk[:, p * LANES:(p + 1) * LANES]
                    lhs.append(jnp.concatenate([_bf(prod), _bf(jnp.where(mask, vt_scr[d, bb, p, sub], 0.0))],
                                               axis=1))
            reds.append(jnp.dot(jnp.concatenate(lhs, axis=0), ones2, preferred_element_type=F32))
        for (d, bbs), red in zip(groups, reds):
            _, _, _, w_ref, ka_ref, kd_ref, _, rev = dirs[d]
            tau = row_of(rev, tt)
            for k, bb in enumerate(bbs):
                w = w_ref[bb, pl.ds(tau, 1), :]
                ka = ka_ref[bb, pl.ds(tau, 1), :]
                kd = kd_ref[bb, pl.ds(tau, 1), :]
                for p in range(n_pair):
                    sl = slice(p * LANES, (p + 1) * LANES)
                    r0 = (k * n_pair + p) * HD_B
                    sa = red[r0:r0 + HD_B, :LANES]
                    vcol = red[r0:r0 + HD_B, LANES:]
                    s_scr[d, bb, p] = s_scr[d, bb, p] * w[:, sl] + sa * ka[:, sl] + vcol * kd[:, sl]
        return carry

    lax.fori_loop(0, RW_TB, step, 0)
    for d in range(2):
        for bb in range(RW_BB):
            emit_output(d, bb, row_of(dirs[d][7], RW_TB - 1))

    @pl.when(tb == n_tb - 1)
    def _():
        sff_ref[...] = s_scr[0]
        sfb_ref[...] = s_scr[1]


def _rwkv_scan(pp, row0, n_seq, seq, s0_f, s0_b, ones_pair, prev_f, prev_b):
    n_tb = seq // RW_TB
    n_pair = H_B // 2
    blk0 = row0 // seq
    view = lambda a: a.reshape(a.shape[0] // seq, seq, W_B)
    fwd = pl.BlockSpec((RW_BB, RW_TB, W_B), lambda b, t: (blk0 // RW_BB + b, t, 0))
    bwd = pl.BlockSpec((RW_BB, RW_TB, W_B), lambda b, t: (blk0 // RW_BB + b, n_tb - 1 - t, 0))
    st = pl.BlockSpec((RW_BB, n_pair, HD_B, LANES), lambda b, t: (b, 0, 0, 0))
    ins_f = [view(pp[k]) for k in ('nkk', 'r', 'v', 'w_f', 'ka_f', 'kd_f')]
    ins_b = [view(pp[k]) for k in ('nkk', 'r', 'v', 'w_b', 'ka_b', 'kd_b')]
    st_shape = jax.ShapeDtypeStruct((n_seq, n_pair, HD_B, LANES), F32)
    o_shape = jax.ShapeDtypeStruct(view(prev_f).shape, F32)
    any_spec = pl.BlockSpec(memory_space=pl.ANY)
    o_f, o_b, sf, sb = pl.pallas_call(
        functools.partial(_rwkv_scan_kernel, n_tb),
        grid=(n_seq // RW_BB, n_tb),
        in_specs=[fwd] * 6 + [bwd] * 6 + [st, st, pl.BlockSpec(ones_pair.shape, lambda b, t: (0, 0)),
                                           any_spec, any_spec],
        out_specs=[fwd, bwd, st, st],
        out_shape=[o_shape, o_shape, st_shape, st_shape],
        input_output_aliases={15: 0, 16: 1},
        scratch_shapes=[pltpu.VMEM((2, RW_BB, n_pair, HD_B, LANES), F32),
                        pltpu.VMEM((2, RW_BB, n_pair, 2, HD_B, LANES), F32)],
        compiler_params=_cparams(2),
        name="rwkv_scan",
    )(*ins_f, *ins_b, s0_f, s0_b, ones_pair, view(prev_f), view(prev_b))
    return o_f.reshape(prev_f.shape), o_b.reshape(prev_b.shape), sf, sb


def _state_to_pairs(s):
    b = s.shape[0]
    return s.reshape(b, H_B // 2, 2, HD_B, HD_B).transpose(0, 1, 3, 2, 4).reshape(b, H_B // 2, HD_B, 2 * HD_B)


def _pairs_to_state(s):
    b = s.shape[0]
    return s.reshape(b, H_B // 2, HD_B, 2, HD_B).transpose(0, 1, 3, 2, 4).reshape(b, H_B, HD_B, HD_B)


def _tail(x, y, g1, n2g, sc2, sh2, x1_ref, h_ref, hp_ref):
    x1 = x + g1 * y
    x1_ref[...] = x1
    h = _rms_mod(x1, n2g, sc2, sh2)
    h_ref[...] = _bf(h)
    hp_ref[...] = _pack_pairs(h)


def _outproj0_kernel(rows, xa_ref, xb_ref, oa_ref, of_ref, ob_ref, bonus_ref, g_ref, lnw_ref, lnb_ref, ones_ref,
                     w_ref, g1_ref, n2g_ref, sc2_ref, sh2_ref, x1_ref, h_ref, hp_ref):
    o_sum = of_ref[...] + ob_ref[...]
    ones2 = ones_ref[...]
    mean = _seg_sum(o_sum, ones2) * (1.0 / HD_B)
    cen = o_sum - mean
    var = _seg_sum(cen * cen, ones2) * (1.0 / HD_B)
    gn = cen * lax.rsqrt(var + GN_EPS) * lnw_ref[...] + lnb_ref[...]
    o_rw = (gn + bonus_ref[...]) * g_ref[...]
    mix = jnp.concatenate([_bf(oa_ref[...]), _bf(o_rw)], axis=1)
    y = jnp.dot(mix, w_ref[...], preferred_element_type=F32)
    _tail(_pick_x(rows, xa_ref, xb_ref), y, g1_ref[...], n2g_ref[...], sc2_ref[...], sh2_ref[...], x1_ref, h_ref, hp_ref)


def _outproj0(rows, xs, o_att, o_f, o_b, pp, pr, ones_b, w_out_bf, n2g, mod4, layer):
    n = rows.n
    tok = lambda w: pl.BlockSpec((TM, w), lambda i: (i, 0))
    const = lambda a: pl.BlockSpec(a.shape, lambda i: (0,) * a.ndim)
    lnw, lnb, n2 = pr['ln_w'].reshape(1, -1), pr['ln_b'].reshape(1, -1), n2g.reshape(1, -1)
    return pl.pallas_call(
        functools.partial(_outproj0_kernel, rows),
        grid=(n // TM,),
        in_specs=_x_specs(rows, xs) + [tok(W_A), tok(W_B), tok(W_B), tok(W_B), tok(W_B),
                  const(lnw), const(lnb), const(ones_b), const(w_out_bf),
                  rows.mod_spec(layer, 2, TM), const(n2), rows.mod_spec(layer, 4, TM), rows.mod_spec(layer, 3, TM)],
        out_specs=[tok(D_MODEL), tok(D_MODEL), tok(D_MODEL // 2)],
        out_shape=[jax.ShapeDtypeStruct((n, D_MODEL), F32), jax.ShapeDtypeStruct((n, D_MODEL), BF16),
                   jax.ShapeDtypeStruct((n, D_MODEL // 2), jnp.uint32)],
        compiler_params=_cparams(1),
        name="outproj0",
    )(*xs, o_att, o_f, o_b, pp['bonus'], pp['g'], lnw, lnb, ones_b, w_out_bf, mod4, n2, mod4, mod4)


def _outproj1_kernel(rows, xa_ref, xb_ref, of_ref, ob_ref, gate_ref, ng_ref, w_ref, g1_ref, n2g_ref, sc2_ref, sh2_ref,
                     x1_ref, h_ref, hp_ref):
    o_sum = of_ref[...] + ob_ref[...]
    parts = []
    for h in range(H_C):
        oh = o_sum[:, h * DV_C:(h + 1) * DV_C]
        parts.append(oh * lax.rsqrt(jnp.mean(oh * oh, axis=-1, keepdims=True) + EPS))
    o = jnp.concatenate(parts, axis=1) * ng_ref[...] * _silu(gate_ref[...])
    y = jnp.dot(_bf(o), w_ref[...], preferred_element_type=F32)
    _tail(_pick_x(rows, xa_ref, xb_ref), y, g1_ref[...], n2g_ref[...], sc2_ref[...], sh2_ref[...], x1_ref, h_ref, hp_ref)


def _outproj1(rows, xs, o_f, o_b, p1, norm_g, w_out_bf, n2g, mod4, layer):
    n = rows.n
    tok = lambda w: pl.BlockSpec((TM, w), lambda i: (i, 0))
    const = lambda a: pl.BlockSpec(a.shape, lambda i: (0,) * a.ndim)
    ng, n2 = norm_g.reshape(1, -1), n2g.reshape(1, -1)
    return pl.pallas_call(
        functools.partial(_outproj1_kernel, rows),
        grid=(n // TM,),
        in_specs=_x_specs(rows, xs) + [tok(D_C), tok(D_C), pl.BlockSpec((TM, D_C), lambda i: (i, 4)),
                  const(ng), const(w_out_bf),
                  rows.mod_spec(layer, 2, TM), const(n2), rows.mod_spec(layer, 4, TM), rows.mod_spec(layer, 3, TM)],
        out_specs=[tok(D_MODEL), tok(D_MODEL), tok(D_MODEL // 2)],
        out_shape=[jax.ShapeDtypeStruct((n, D_MODEL), F32), jax.ShapeDtypeStruct((n, D_MODEL), BF16),
                   jax.ShapeDtypeStruct((n, D_MODEL // 2), jnp.uint32)],
        compiler_params=_cparams(1),
        name="outproj1",
    )(*xs, o_f, o_b, p1, ng, w_out_bf, mod4, n2, mod4, mod4)


def _hgrn_kernel(n_tb, qf_ref, ff_ref, if_ref, qb_ref, fb_ref, ib_ref, lbp_ref, s0f_ref, s0b_ref,
                 trif_ref, trib_ref, prevf_ref, prevb_ref, of_ref, ob_ref, sff_ref, sfb_ref, s_scr):
    del prevf_ref, prevb_ref
    tb = pl.program_id(1)

    @pl.when(tb == 0)
    def _():
        for h in range(H_C):
            s_scr[0, h] = s0f_ref[0, h].T
            s_scr[1, h] = s0b_ref[0, h].T

    lbp = lbp_ref[...]
    e = jnp.exp(lbp - jnp.max(lbp, axis=0, keepdims=True))
    sm = e / jnp.sum(e, axis=0, keepdims=True)
    lb = (sm[0:1] + sm[1:2]) - sm[0:1]

    n_chunk = HG_TB // CHUNK
    ti = lax.broadcasted_iota(jnp.int32, (HG_TB, HG_TB), 0)
    si = lax.broadcasted_iota(jnp.int32, (HG_TB, HG_TB), 1)
    same = (ti // CHUNK) == (si // CHUNK)
    dirs = ((qf_ref, ff_ref, if_ref, of_ref, trif_ref, jnp.logical_and(same, ti >= si), CHUNK - 1, False),
            (qb_ref, fb_ref, ib_ref, ob_ref, trib_ref, jnp.logical_and(same, ti <= si), 0, True))

    staged = []
    for d, (q_ref, f_ref, i_ref, o_ref, tri_ref, causal, last_row, rev) in enumerate(dirs):
        q = _silu(q_ref[...])
        f = lb + (1.0 - lb) * _sigmoid(f_ref[...])
        k = 1.0 - f
        v = _bf(i_ref[...])
        g = jnp.log(f)
        g1 = _bf(g)
        g2 = _bf(g - g1.astype(F32))
        tri2 = tri_ref[...]
        b_parts, last_parts, dec = [], [], []
        for c in range(n_chunk):
            rc = slice(c * CHUNK, (c + 1) * CHUNK)
            bc = jnp.dot(tri2, jnp.concatenate([g1[rc], g2[rc]], axis=0), preferred_element_type=F32)
            b_parts.append(bc)
            last = bc[last_row:last_row + 1]
            last_parts.append(jnp.broadcast_to(last, bc.shape))
            dec.append(jnp.exp(last))
        b = jnp.concatenate(b_parts, axis=0)
        b_last = jnp.concatenate(last_parts, axis=0)
        staged.append((_bf(q * jnp.exp(b)), _bf(k * jnp.exp(-b)), _bf(k * jnp.exp(b_last - b)), v, dec))

    for h in range(H_C):
        sl = slice(h * DK_C, (h + 1) * DK_C)
        for d, (q_ref, f_ref, i_ref, o_ref, tri_ref, causal, last_row, rev) in enumerate(dirs):
            q_in, k_in, k_out, v, dec = staged[d]
            qh, vh = q_in[:, sl], v[:, sl]
            att = jnp.where(causal, _dot_nt(qh, k_in[:, sl]), 0.0)
            o_intra = jnp.dot(_bf(att), vh, preferred_element_type=F32)
            s_t = s_scr[d, h]
            for c in (range(n_chunk - 1, -1, -1) if rev else range(n_chunk)):
                rc = slice(c * CHUNK, (c + 1) * CHUNK)
                o_ref[rc, sl] = o_intra[rc] + _dot_nt(qh[rc], _bf(s_t))
                s_t = dec[c][:, sl] * s_t + _dot_tn(vh[rc], k_out[rc, sl])
            s_scr[d, h] = s_t

    @pl.when(tb == n_tb - 1)
    def _():
        for h in range(H_C):
            sff_ref[0, h] = s_scr[0, h].T
            sfb_ref[0, h] = s_scr[1, h].T


def _hgrn_scan(p1, row0, n_seq, seq, lb_params, s0_f, s0_b, prev_f, prev_b):
    n_tb = seq // HG_TB
    blk0 = row0 // HG_TB
    tri = np.tril(np.ones((CHUNK, CHUNK), np.float32))
    tri_f = jnp.asarray(np.concatenate([tri] * 2, axis=1), dtype=BF16)
    tri_b = jnp.asarray(np.concatenate([tri.T] * 2, axis=1), dtype=BF16)
    fwd = lambda col: pl.BlockSpec((HG_TB, D_C), lambda b, t: (blk0 + b * n_tb + t, col))
    bwd = lambda col: pl.BlockSpec((HG_TB, D_C), lambda b, t: (blk0 + b * n_tb + n_tb - 1 - t, col))
    st = pl.BlockSpec((1, H_C, DK_C, DV_C), lambda b, t: (b, 0, 0, 0))
    const = lambda a: pl.BlockSpec(a.shape, lambda b, t: (0,) * a.ndim)
    o_shape = jax.ShapeDtypeStruct(prev_f.shape, F32)
    st_shape = jax.ShapeDtypeStruct((n_seq, H_C, DK_C, DV_C), F32)
    any_spec = pl.BlockSpec(memory_space=pl.ANY)
    return pl.pallas_call(
        functools.partial(_hgrn_kernel, n_tb),
        grid=(n_seq, n_tb),
        in_specs=[fwd(0), fwd(1), fwd(3), bwd(0), bwd(2), bwd(3), const(lb_params), st, st,
                  const(tri_f), const(tri_b), any_spec, any_spec],
        out_specs=[fwd(0), bwd(0), st, st],
        out_shape=[o_shape, o_shape, st_shape, st_shape],
        input_output_aliases={11: 0, 12: 1},
        scratch_shapes=[pltpu.VMEM((2, H_C, DV_C, DK_C), F32)],
        compiler_params=_cparams(2),
        name="hgrn_scan",
    )(p1, p1, p1, p1, p1, p1, lb_params, s0_f, s0_b, tri_f, tri_b, prev_f, prev_b)


def _router_kernel(h_ref, rhi_ref, rlo_ref, bias_ref, sel_ref, eidx_ref, ew_ref, cnt_ref):
    x = h_ref[...]
    tm = x.shape[0]
    logits = _dot_nt(rhi_ref[...], x) + _dot_nt(rlo_ref[...], x)
    scores = _sigmoid(logits)
    biased = scores + bias_ref[...]
    per = N_EXPERTS // N_GROUPS
    sub = lax.broadcasted_iota(jnp.int32, (per, tm), 0)
    gs_rows = []
    for g in range(N_GROUPS):
        blk = biased[g * per:(g + 1) * per]
        m1 = jnp.max(blk, axis=0, keepdims=True)
        first = jnp.min(jnp.where(blk == m1, sub, per), axis=0, keepdims=True)
        m2 = jnp.max(jnp.where(sub == first, -jnp.inf, blk), axis=0, keepdims=True)
        gs_rows.append(m1 + m2)
    gs = jnp.concatenate(gs_rows, axis=0)
    gi = lax.broadcasted_iota(jnp.int32, gs.shape, 0)
    rank = jnp.zeros(gs.shape, jnp.int32)
    for s in range(1, N_GROUPS):
        other = pltpu.roll(gs, s, 0)
        oi = pltpu.roll(gi, s, 0)
        beats = jnp.logical_or(other > gs, jnp.logical_and(other == gs, oi < gi))
        rank = rank + jnp.where(beats, 1, 0)
    keep = jnp.where(rank < TOPK_GROUPS, 1.0, 0.0)
    emask = jnp.concatenate([jnp.broadcast_to(keep[g:g + 1], (per, tm)) for g in range(N_GROUPS)], axis=0)
    cur = jnp.where(emask > 0.0, biased, -jnp.inf)
    ei = lax.broadcasted_iota(jnp.int32, cur.shape, 0)
    sel = jnp.zeros(cur.shape, F32)
    idxs, vals = [], []
    for _ in range(TOP_K):
        m = jnp.max(cur, axis=0, keepdims=True)
        idx = jnp.min(jnp.where(cur == m, ei, N_EXPERTS), axis=0, keepdims=True)
        pick = ei == idx
        idxs.append(idx)
        vals.append(jnp.sum(jnp.where(pick, scores, 0.0), axis=0, keepdims=True))
        sel = jnp.where(pick, 1.0, sel)
        cur = jnp.where(pick, -jnp.inf, cur)
    w = jnp.concatenate(vals, axis=0)
    eidx_ref[...] = jnp.concatenate(idxs, axis=0)
    ew_ref[...] = w / jnp.sum(w, axis=0, keepdims=True) * ROUTED_SCALE
    sel_ref[...] = _bf(sel)

    @pl.when(pl.program_id(0) == 0)
    def _():
        cnt_ref[...] = jnp.zeros_like(cnt_ref)

    cnt_ref[...] += jnp.sum(sel, axis=1, keepdims=True)


def _router(hffn, router, bias):
    n = hffn.shape[0]
    r_t = router.T
    r_hi = _bf(r_t)
    r_lo = _bf(r_t - r_hi.astype(F32))
    const = lambda a: pl.BlockSpec(a.shape, lambda i: (0,) * a.ndim)
    b_col = bias.reshape(N_EXPERTS, 1)
    return pl.pallas_call(
        _router_kernel,
        grid=(n // TM,),
        in_specs=[pl.BlockSpec((TM, D_MODEL), lambda i: (i, 0)), const(r_hi), const(r_lo), const(b_col)],
        out_specs=[pl.BlockSpec((N_EXPERTS, TM), lambda i: (0, i)),
                   pl.BlockSpec((TOP_K, TM), lambda i: (0, i)),
                   pl.BlockSpec((TOP_K, TM), lambda i: (0, i)),
                   pl.BlockSpec((N_EXPERTS, LANES), lambda i: (0, 0))],
        out_shape=[jax.ShapeDtypeStruct((N_EXPERTS, n), BF16),
                   jax.ShapeDtypeStruct((TOP_K, n), jnp.int32),
                   jax.ShapeDtypeStruct((TOP_K, n), F32),
                   jax.ShapeDtypeStruct((N_EXPERTS, LANES), F32)],
        compiler_params=_cparams(1),
        name="router",
    )(hffn, r_hi, r_lo, b_col)


def _positions_kernel(sel_ref, eidx_ref, base_ref, upper_ref, pos_ref, carry_ref):
    @pl.when(pl.program_id(0) == 0)
    def _():
        carry_ref[...] = jnp.zeros_like(carry_ref)

    sel = sel_ref[...]
    rank = jnp.dot(sel, upper_ref[...], preferred_element_type=F32)
    pos_e = base_ref[:, 0:1] + carry_ref[:, 0:1] + rank
    ei = lax.broadcasted_iota(jnp.int32, pos_e.shape, 0)
    eidx = eidx_ref[...]
    rows = [jnp.sum(jnp.where(ei == eidx[k:k + 1], pos_e, 0.0), axis=0, keepdims=True) for k in range(TOP_K)]
    pos_ref[...] = jnp.concatenate(rows, axis=0).astype(jnp.int32)
    carry_ref[...] += jnp.sum(sel.astype(F32), axis=1, keepdims=True)


def _positions(sel, eidx, base):
    n = sel.shape[1]
    pb = POS_TB
    upper = jnp.asarray(np.triu(np.ones((pb, pb), np.float32), 1), dtype=BF16)
    return pl.pallas_call(
        _positions_kernel,
        grid=(n // pb,),
        in_specs=[pl.BlockSpec((N_EXPERTS, pb), lambda i: (0, i)),
                  pl.BlockSpec((TOP_K, pb), lambda i: (0, i)),
                  pl.BlockSpec((N_EXPERTS, LANES), lambda i: (0, 0)),
                  pl.BlockSpec((pb, pb), lambda i: (0, 0))],
        out_specs=pl.BlockSpec((TOP_K, pb), lambda i: (0, i)),
        out_shape=jax.ShapeDtypeStruct((TOP_K, n), jnp.int32),
        scratch_shapes=[pltpu.VMEM((N_EXPERTS, LANES), F32)],
        compiler_params=_cparams(1),
        name="positions",
    )(sel, eidx, base, upper)


def _pack_pairs(x):
    half = x.shape[1] // 2
    bits = lax.bitcast_convert_type(_bf(x).astype(F32), jnp.uint32)
    return (bits[:, :half] >> 16) | (bits[:, half:] & jnp.uint32(0xFFFF0000))


def _unpack_pairs(w):
    lo = lax.bitcast_convert_type(w << 16, F32)
    hi = lax.bitcast_convert_type(w & jnp.uint32(0xFFFF0000), F32)
    return jnp.concatenate([_bf(lo), _bf(hi)], axis=1)


def _sc_gather(table, idx):
    b, w = idx.shape[0], table.shape[1]
    n_workers = SC_CORES * SC_SUBCORES
    per_w = b // n_workers
    assert b % (n_workers * SC_CHUNK) == 0
    mesh = plsc.VectorSubcoreMesh(core_axis_name="c", subcore_axis_name="s")

    n_chunk = per_w // SC_CHUNK
    assert n_chunk % 2 == 0

    @functools.partial(
        pl.kernel, mesh=mesh, out_type=jax.ShapeDtypeStruct((b, w), table.dtype),
        scratch_types=[pltpu.VMEM((2, SC_CHUNK), jnp.int32), pltpu.VMEM((2, SC_CHUNK, w), table.dtype),
                       pltpu.SemaphoreType.DMA((2,))])
    def gather(table_hbm, idx_hbm, out_hbm, idx_v, rows_v, sems):
        wid = lax.axis_index("s") * SC_CORES + lax.axis_index("c")
        base = wid * per_w

        def start(c, slot):
            off = pl.multiple_of(base + c * SC_CHUNK, SC_CHUNK)
            pltpu.sync_copy(idx_hbm.at[pl.ds(off, SC_CHUNK)], idx_v.at[slot])
            pltpu.async_copy(table_hbm.at[idx_v.at[slot]], rows_v.at[slot], sems.at[slot])

        def finish(c, slot):
            off = pl.multiple_of(base + c * SC_CHUNK, SC_CHUNK)
            pltpu.make_async_copy(table_hbm.at[idx_v.at[slot]], rows_v.at[slot], sems.at[slot]).wait()
            pltpu.sync_copy(rows_v.at[slot], out_hbm.at[pl.ds(off, SC_CHUNK)])

        start(0, 0)

        @pl.loop(0, n_chunk, step=2)
        def _(c):
            start(c + 1, 1)
            finish(c, 0)

            @pl.when(c + 2 < n_chunk)
            def _():
                start(c + 2, 0)

            finish(c + 1, 1)

    return gather(table, idx)


def _sc_scatter(src, pos3, n_rows):
    n, w = src.shape
    n_workers = SC_CORES * SC_SUBCORES
    per_w = n // n_workers
    assert n % (n_workers * SC_CHUNK) == 0
    mesh = plsc.VectorSubcoreMesh(core_axis_name="c", subcore_axis_name="s")

    n_chunk = per_w // SC_CHUNK
    assert n_chunk % 2 == 0

    @functools.partial(
        pl.kernel, mesh=mesh, out_type=jax.ShapeDtypeStruct((n_rows, w), src.dtype),
        scratch_types=[pltpu.VMEM((2, TOP_K, SC_CHUNK), jnp.int32), pltpu.VMEM((2, SC_CHUNK, w), src.dtype),
                       pltpu.SemaphoreType.DMA((2,)), pltpu.SemaphoreType.DMA((2,))])
    def scatter(src_hbm, pos_hbm, out_hbm, idx_v, rows_v, ld_sems, sc_sems):
        wid = lax.axis_index("s") * SC_CORES + lax.axis_index("c")
        base = wid * per_w

        def loads(c, slot):
            off = pl.multiple_of(base + c * SC_CHUNK, SC_CHUNK)
            return (pltpu.make_async_copy(src_hbm.at[pl.ds(off, SC_CHUNK)], rows_v.at[slot], ld_sems.at[slot]),
                    pltpu.make_async_copy(pos_hbm.at[off // SC_CHUNK], idx_v.at[slot], ld_sems.at[slot]))

        def scatter_chunk(slot):
            copies = [pltpu.async_copy(rows_v.at[slot], out_hbm.at[idx_v.at[slot, k]], sc_sems.at[slot])
                      for k in range(TOP_K)]
            for cp in copies:
                cp.wait()

        def half_step(c, slot):
            for cp in loads(c, slot):
                cp.wait()

            @pl.when(c + 1 < n_chunk)
            def _():
                for cp in loads(c + 1, 1 - slot):
                    cp.start()

            scatter_chunk(slot)

        for cp in loads(0, 0):
            cp.start()

        @pl.loop(0, n_chunk, step=2)
        def _(c):
            half_step(c, 0)
            half_step(c + 1, 1)

    return scatter(src, pos3)


def _experts_kernel(te_ref, nu_ref, xs_ref, wg_ref, wu_ref, wd_ref, ys_ref, wg_bf, wu_bf, wd_bf):
    i = pl.program_id(0)
    active = i < nu_ref[0]

    @pl.when(jnp.logical_and(active, jnp.logical_or(i == 0, te_ref[i] != te_ref[jnp.maximum(i - 1, 0)])))
    def _():
        wg_bf[...] = _bf(wg_ref[0])
        wu_bf[...] = _bf(wu_ref[0])
        wd_bf[...] = _bf(wd_ref[0])

    @pl.when(active)
    def _():
        x = _unpack_pairs(xs_ref[...])
        act = _glu(x, wg_bf[...], wu_bf[...])
        ys_ref[...] = _pack_pairs(jnp.dot(_bf(act), wd_bf[...], preferred_element_type=F32))


def _experts(xs, tile_expert, n_used, mp, layer):
    n_tiles = xs.shape[0] // MOE_TILE
    half = D_MODEL // 2
    wspec = lambda shape: pl.BlockSpec((None, 1) + shape, lambda i, te, nu: (layer, te[i], 0, 0))
    return pl.pallas_call(
        _experts_kernel,
        grid_spec=pltpu.PrefetchScalarGridSpec(
            num_scalar_prefetch=2, grid=(n_tiles,),
            in_specs=[pl.BlockSpec((MOE_TILE, half), lambda i, te, nu: (i, 0)),
                      wspec((D_MODEL, D_EXPERT)), wspec((D_MODEL, D_EXPERT)), wspec((D_EXPERT, D_MODEL))],
            out_specs=pl.BlockSpec((MOE_TILE, half), lambda i, te, nu: (i, 0)),
            scratch_shapes=[pltpu.VMEM((D_MODEL, D_EXPERT), BF16), pltpu.VMEM((D_MODEL, D_EXPERT), BF16),
                            pltpu.VMEM((D_EXPERT, D_MODEL), BF16)]),
        out_shape=jax.ShapeDtypeStruct(xs.shape, jnp.uint32),
        compiler_params=_cparams(1),
        name="experts",
    )(tile_expert, n_used, xs, mp['wg'], mp['wu'], mp['wd'])


def _combine_kernel(h_ref, *refs):
    yg_refs = refs[:TOP_K]
    ew_ref, eye_ref, sg_ref, su_ref, sd_ref, x1_ref, g2_ref, o_ref = refs[TOP_K:]
    act = _glu(h_ref[...], sg_ref[...], su_ref[...])
    acc = jnp.dot(_bf(act), sd_ref[...], preferred_element_type=F32)
    ew = ew_ref[...]
    hi = _bf(ew)
    lo = _bf(ew - hi.astype(F32))
    ew_t = _dot_tn(hi, eye_ref[...]) + _dot_tn(lo, eye_ref[...])
    for k in range(TOP_K):
        acc = acc + ew_t[:, k:k + 1] * _unpack_pairs(yg_refs[k][...]).astype(F32)
    o_ref[...] = x1_ref[...] + g2_ref[...] * acc


def _combine(rows, hffn, yg, ew, mp, x1, mod4, layer, row0, n_out):
    half = D_MODEL // 2
    n_blk = rows.n // TM
    blk0 = row0 // TM
    const = lambda a: pl.BlockSpec(a.shape, lambda i: (0,) * a.ndim)
    tok = lambda w: pl.BlockSpec((TM, w), lambda i: (blk0 + i, 0))
    slot = lambda k: pl.BlockSpec((TM, half), lambda i: (k * n_blk + blk0 + i, 0))
    eye = jnp.eye(TOP_K, dtype=BF16)
    return pl.pallas_call(
        _combine_kernel,
        grid=(n_out // TM,),
        in_specs=[tok(D_MODEL)] + [slot(k) for k in range(TOP_K)]
                 + [pl.BlockSpec((TOP_K, TM), lambda i: (0, blk0 + i)), const(eye),
                    const(mp['sg']), const(mp['su']), const(mp['sd']), tok(D_MODEL),
                    rows.mod_spec(layer, 5, TM, blk0)],
        out_specs=pl.BlockSpec((TM, D_MODEL), lambda i: (i, 0)),
        out_shape=jax.ShapeDtypeStruct((n_out, D_MODEL), F32),
        compiler_params=_cparams(1),
        name=f"combine{layer}",
    )(hffn, *([yg] * TOP_K), ew, eye, mp['sg'], mp['su'], mp['sd'], x1, mod4)


def _moe(rows, hffn, hpack, x1, router, bias, mp, mod4, layer, out_ranges):
    n = rows.n
    sel, eidx, ew, cnt = _router(hffn, router, bias)
    counts = cnt[:, 0].astype(jnp.int32)
    padded = (counts + MOE_TILE - 1) // MOE_TILE * MOE_TILE
    ends = jnp.cumsum(padded)
    n_rows = n * TOP_K + N_EXPERTS * MOE_TILE
    n_tiles = n_rows // MOE_TILE
    base = jnp.broadcast_to((ends - padded).astype(F32)[:, None], (N_EXPERTS, LANES))
    tile_start = jnp.arange(n_tiles, dtype=jnp.int32) * MOE_TILE
    tile_expert = jnp.minimum(jnp.sum((ends[None, :] <= tile_start[:, None]).astype(jnp.int32), axis=1),
                              N_EXPERTS - 1)
    n_used = (ends[-1:] // MOE_TILE).astype(jnp.int32)
    pos = _positions(sel, eidx, base)
    pos3 = pos.reshape(TOP_K, n // SC_CHUNK, SC_CHUNK).transpose(1, 0, 2)
    xs = _sc_scatter(hpack, pos3, n_rows)
    ys = _experts(xs, tile_expert, n_used, mp, layer)
    yg = _sc_gather(ys, pos.reshape(-1))
    return [_combine(rows, hffn, yg, ew, mp, x1, mod4, layer, row0, n_out) for row0, n_out in out_ranges]


def _glu(x, wg, wu):
    hg = jnp.dot(x, wg, preferred_element_type=F32)
    hu = jnp.dot(x, wu, preferred_element_type=F32)
    return _silu(hg) * hu


def kernel(x_prompt, x_sample, c, c_ctx, cache_attn_k, cache_attn_v, state_rwkv_fwd, state_rwkv_bwd,
           state_hgrn_fwd, state_hgrn_bwd, norm1_g, norm2_g, mod_w, mod_b, ab_w_in, ab_w_out, attn_q_norm,
           attn_k_norm, attn_sink, rwkv_mu, rwkv_w0, rwkv_w2, rwkv_a0, rwkv_a2, rwkv_g2, rwkv_k_k, rwkv_k_a,
           rwkv_r_k, rwkv_ln_w, rwkv_ln_b, hgrn_w_in, hgrn_w_out, hgrn_lower_bounds, hgrn_norm_g, moe_router,
           moe_bias, moe_w_gate, moe_w_up, moe_w_down, moe_shared_gate, moe_shared_up, moe_shared_down):
    n_cseq, cseq, _ = x_prompt.shape
    n_lseq, lseq, _ = x_sample.shape
    depth = mod_w.shape[0]
    assert depth == 2 and n_lseq + 1 <= SUBLANES
    assert cseq == PREP_TM and lseq % TM == 0 and lseq % HG_TB == 0 and cseq % HG_TB == 0
    assert (n_cseq * cseq) % TM == 0
    assert n_cseq % RW_BB == 0 and n_lseq % RW_BB == 0 and (n_cseq * cseq) % (lseq * RW_BB) == 0
    rows = _Rows(n_cseq * cseq, n_lseq * lseq, lseq)
    assert rows.n % MOE_TILE == 0 and lseq % MOE_TILE == 0 and rows.n_ctx % MOE_TILE == 0
    kv_w = KV_A * HD_A

    xs = (x_prompt.reshape(rows.n_ctx, D_MODEL), x_sample.reshape(rows.n_lat, D_MODEL))
    cvecs = jnp.concatenate([c_ctx[None, :], c, jnp.zeros((SUBLANES - 1 - n_lseq, D_MODEL), F32)], axis=0)
    mod4 = _modulation(cvecs, mod_w, mod_b).reshape(depth, SUBLANES, 1, 6 * D_MODEL)

    ones_q = _block_ones(W_A, HD_A)
    ones_k = _block_ones(kv_w, HD_A)
    ones_b = _block_ones(W_B, HD_B)
    ones_pair = _block_ones(LANES, HD_B)[:LANES]
    ones_pair = jnp.kron(jnp.eye(2, dtype=BF16), ones_pair)
    cos_t, sin_t = _rope_tables(lseq)

    def moe(l, hffn, hpack, x1, out_ranges):
        mp = {'wg': moe_w_gate, 'wu': moe_w_up, 'wd': moe_w_down,
              'sg': _bf(moe_shared_gate[l]), 'su': _bf(moe_shared_up[l]), 'sd': _bf(moe_shared_down[l])}
        return _moe(rows, hffn, hpack, x1, moe_router[l], moe_bias[l], mp, mod4, l, out_ranges)

    assert W_A == W_B
    all_rows = jnp.zeros((rows.n, W_B), F32)

    pr = {'mu': rwkv_mu[0], 'w0': rwkv_w0[0], 'w2': rwkv_w2[0], 'a0': rwkv_a0[0], 'a2': rwkv_a2[0],
          'g2': rwkv_g2[0], 'k_k': rwkv_k_k[0], 'k_a': rwkv_k_a[0], 'r_k': rwkv_r_k[0].reshape(-1),
          'ln_w': rwkv_ln_w[0], 'ln_b': rwkv_ln_b[0]}
    p_att, p_rw = _inproj(rows, xs, norm1_g[0], mod4, 0, _bf(ab_w_in[0]), (ATT_IN, RWKV_IN), TM)
    qg_t = jnp.tile(attn_q_norm[0], H_A).reshape(1, W_A)
    kg_t = jnp.tile(attn_k_norm[0], KV_A).reshape(1, kv_w)
    o_att, new_k, new_v = _ctx_attention(p_att, n_cseq, cseq, qg_t, kg_t, attn_sink[0], ones_q, ones_k, all_rows)
    past = cache_attn_k.shape[2]
    o_att = _lat_attention(p_att, rows.n_ctx // lseq, n_lseq, lseq, qg_t, kg_t, attn_sink[0], ones_q, ones_k,
                           cos_t, sin_t, cache_attn_k[:, 0].reshape(n_lseq, past, kv_w),
                           cache_attn_v[:, 0].reshape(n_lseq, past, kv_w), o_att)

    pp = _rwkv_prep(rows, p_rw, pr, ones_b)
    zero_st = jnp.zeros((n_cseq, H_B // 2, HD_B, LANES), F32)
    o_f, o_b, sf_c, sb_c = _rwkv_scan(pp, 0, n_cseq, cseq, zero_st, zero_st, ones_pair, all_rows, all_rows)
    o_f, o_b, _, _ = _rwkv_scan(pp, rows.n_ctx, n_lseq, lseq, _state_to_pairs(state_rwkv_fwd[:, 0]),
                                _state_to_pairs(state_rwkv_bwd[:, 0]), ones_pair, o_f, o_b)
    x1, hffn, hpack = _outproj0(rows, xs, o_att, o_f, o_b, pp, pr, ones_b, _bf(ab_w_out[0]), norm2_g[0], mod4, 0)
    (x,) = moe(0, hffn, hpack, x1, [(0, rows.n)])

    (p1,) = _inproj(rows, (x, x), norm1_g[1], mod4, 1, _bf(hgrn_w_in[0]), (IN_C,), IN1_TM)
    zero_h = jnp.zeros((n_cseq, H_C, DK_C, DV_C), F32)
    all_rows_c = jnp.zeros((rows.n, D_C), F32)
    h_f, h_b, hsf_c, hsb_c = _hgrn_scan(p1, 0, n_cseq, cseq, hgrn_lower_bounds, zero_h, zero_h,
                                        all_rows_c, all_rows_c)
    h_f, h_b, _, _ = _hgrn_scan(p1, rows.n_ctx, n_lseq, lseq, hgrn_lower_bounds,
                                state_hgrn_fwd[:, 0], state_hgrn_bwd[:, 0], h_f, h_b)
    x1, hffn, hpack = _outproj1(rows, (x, x), h_f, h_b, p1, hgrn_norm_g[0], _bf(hgrn_w_out[0]), norm2_g[1],
                                mod4, 1)
    y_c, y_l = moe(1, hffn, hpack, x1, [(0, rows.n_ctx), (rows.n_ctx, rows.n_lat)])

    y_prompt = y_c.reshape(n_cseq, cseq, D_MODEL)
    y_sample = y_l.reshape(n_lseq, lseq, D_MODEL)
    return (y_prompt, y_sample,
            new_k.reshape(n_cseq, 1, cseq, KV_A, HD_A), new_v.reshape(n_cseq, 1, cseq, KV_A, HD_A),
            _pairs_to_state(sf_c)[:, None], _pairs_to_state(sb_c)[:, None],
            hsf_c[:, None], hsb_c[:, None])
```

```python
import functools

import numpy as np
import jax
import jax.numpy as jnp
from jax import lax
from jax.experimental import pallas as pl
from jax.experimental.pallas import tpu as pltpu
from jax.experimental.pallas import tpu_sc as plsc

F32 = jnp.float32
BF16 = jnp.bfloat16

D_MODEL = 1024
GRID_W = 64
H_A = 8
KV_A = 2
G_A = H_A // KV_A
HD_A = 64
W_A = H_A * HD_A
WINDOW = 128
QBLK = 128
ROPE_BASE = 10000.0
ATTN_SCALE = HD_A ** -0.5
NEG_INF = -1e30
H_B = 8
HD_B = 64
W_B = H_B * HD_B
LORA_W = 64
LORA_A = 64
LORA_G = 128
GN_EPS = 64e-5
ATT_IN = W_A + 2 * KV_A * HD_A
RWKV_IN = 3 * W_B + LORA_W + LORA_A + LORA_G
IN_AB = ATT_IN + RWKV_IN
H_C = 8
DK_C = 128
DV_C = 128
D_C = H_C * DV_C
CHUNK = 64
IN_C = 5 * D_C
N_EXPERTS = 64
TOP_K = 8
N_GROUPS = 8
TOPK_GROUPS = 4
D_EXPERT = 256
ROUTED_SCALE = 2.5
EPS = 1e-6

LANES = 128
SUBLANES = 8
VMEM_LIMIT = 52 * 1024 * 1024

TM = 512
PREP_TM = 256
IN1_TM = 256
RW_TB = 128
RW_BB = 4
RW_GROUP_BB = 4
HG_TB = 256
MOE_TILE = 512
POS_TB = 512
SC_CORES = 2
SC_SUBCORES = 16
SC_CHUNK = 64


def _cparams(n_axes):
    return pltpu.CompilerParams(dimension_semantics=("arbitrary",) * n_axes,
                                vmem_limit_bytes=VMEM_LIMIT)


def _bf(x):
    return x.astype(BF16)


def _split2(x):
    hi = lax.bitcast_convert_type(
        lax.bitcast_convert_type(x, jnp.uint32) & jnp.uint32(0xFFFF0000), F32)
    return hi, x - hi


def _seg_sum(x, ones2):
    hi, lo = _split2(x)
    return jnp.dot(jnp.concatenate([_bf(hi), _bf(lo)], axis=1), ones2,
                   preferred_element_type=F32)


def _dot_nt(a, b):
    return lax.dot_general(a, b, (((1,), (1,)), ((), ())), preferred_element_type=F32)


def _dot_tn(a, b):
    return lax.dot_general(a, b, (((0,), (0,)), ((), ())), preferred_element_type=F32)


def _sigmoid(x):
    return 1.0 / (1.0 + jnp.exp(-x))


def _silu(x):
    return x * _sigmoid(x)


def _chunks(seq, n):
    seq = list(seq)
    return [seq[i:i + n] for i in range(0, len(seq), n)]


def _block_ones(width, seg):
    idx = np.arange(width) // seg
    bd = (idx[:, None] == idx[None, :]).astype(np.float32)
    return jnp.asarray(np.concatenate([bd, bd], axis=0), dtype=BF16)


def _mod_kernel(c_ref, w_ref, b_ref, o_ref):
    s = _silu(c_ref[...])
    o_ref[0] = jnp.dot(_bf(s), _bf(w_ref[0]), preferred_element_type=F32) + b_ref[0]


def _modulation(cvecs, mod_w, mod_b):
    depth = mod_w.shape[0]
    n_col = 6 * D_MODEL // D_MODEL
    return pl.pallas_call(
        _mod_kernel,
        grid=(depth, n_col),
        in_specs=[pl.BlockSpec((SUBLANES, D_MODEL), lambda l, j: (0, 0)),
                  pl.BlockSpec((1, D_MODEL, D_MODEL), lambda l, j: (l, 0, j)),
                  pl.BlockSpec((1, 1, D_MODEL), lambda l, j: (l, 0, j))],
        out_specs=pl.BlockSpec((1, SUBLANES, D_MODEL), lambda l, j: (l, 0, j)),
        out_shape=jax.ShapeDtypeStruct((depth, SUBLANES, 6 * D_MODEL), F32),
        compiler_params=_cparams(2),
        name="modulation",
    )(cvecs, mod_w, mod_b.reshape(depth, 1, 6 * D_MODEL))


class _Rows:
    def __init__(self, n_ctx, n_lat, lat_seq):
        self.n_ctx, self.n_lat, self.lat_seq = n_ctx, n_lat, lat_seq
        self.n = n_ctx + n_lat

    def mod_row(self, i, tm):
        nctx_blk = self.n_ctx // tm
        per_seq = self.lat_seq // tm
        return jnp.where(i < nctx_blk, 0, 1 + (i - nctx_blk) // per_seq)

    def mod_spec(self, layer, chunk, tm, blk0=0):
        return pl.BlockSpec((None, None, 1, D_MODEL),
                            lambda i, *_: (layer, self.mod_row(i + blk0, tm), 0, chunk))


def _rms_mod(x, g, sc, sh):
    ms = jnp.mean(x * x, axis=-1, keepdims=True)
    return x * lax.rsqrt(ms + EPS) * g * (1.0 + sc) + sh


def _x_specs(rows, xs, tm=TM):
    xa, xb = xs
    nctx_blk = rows.n_ctx // tm
    lat0 = nctx_blk if xb.shape[0] == rows.n else 0
    return [pl.BlockSpec((tm, D_MODEL), lambda i: (jnp.minimum(i, nctx_blk - 1), 0)),
            pl.BlockSpec((tm, D_MODEL), lambda i: (jnp.maximum(i - nctx_blk, 0) + lat0, 0))]


def _pick_x(rows, xa_ref, xb_ref):
    return jnp.where(pl.program_id(0) < rows.n_ctx // xa_ref.shape[0], xa_ref[...], xb_ref[...])


def _inproj_kernel(rows, splits, xa_ref, xb_ref, g_ref, sh_ref, sc_ref, w_ref, *o_refs):
    h = _rms_mod(_pick_x(rows, xa_ref, xb_ref), g_ref[...], sc_ref[...], sh_ref[...])
    p = jnp.dot(_bf(h), w_ref[...], preferred_element_type=F32)
    lo = 0
    for o_ref, width in zip(o_refs, splits):
        o_ref[...] = p[:, lo:lo + width]
        lo += width


def _inproj(rows, xs, g, mod4, layer, w_bf, splits, tm):
    n_out = w_bf.shape[1]
    return pl.pallas_call(
        functools.partial(_inproj_kernel, rows, splits),
        grid=(rows.n // tm,),
        in_specs=_x_specs(rows, xs, tm) + [
            pl.BlockSpec((1, D_MODEL), lambda i: (0, 0)),
            rows.mod_spec(layer, 0, tm),
            rows.mod_spec(layer, 1, tm),
            pl.BlockSpec((D_MODEL, n_out), lambda i: (0, 0))],
        out_specs=[pl.BlockSpec((tm, wd), lambda i: (i, 0)) for wd in splits],
        out_shape=[jax.ShapeDtypeStruct((rows.n, wd), F32) for wd in splits],
        compiler_params=_cparams(1),
        name=f"inproj{layer}",
    )(*xs, g.reshape(1, D_MODEL), mod4, mod4, w_bf)


def _head_rms(x, gain_t, ones2):
    ms = _seg_sum(x * x, ones2) * (1.0 / HD_A)
    return x * lax.rsqrt(ms + EPS) * gain_t


def _sink_softmax_pv(parts, sink):
    m = jnp.maximum(functools.reduce(jnp.maximum, [jnp.max(s, axis=-1, keepdims=True) for s, _ in parts]), sink)
    den = jnp.exp(sink - m)
    acc = None
    for s, v in parts:
        p = jnp.exp(s - m)
        den = den + jnp.sum(p, axis=-1, keepdims=True)
        pv = jnp.dot(_bf(p), v, preferred_element_type=F32)
        acc = pv if acc is None else acc + pv
    return acc / den


def _ctx_attn_kernel(p_ref, qg_ref, kg_ref, sink_ref, ones_q_ref, ones_k_ref, prev_ref, o_ref, k_ref, v_ref):
    del prev_ref
    p = p_ref[...]
    q = _head_rms(p[:, :W_A], qg_ref[...], ones_q_ref[...]) * ATTN_SCALE
    k = _head_rms(p[:, W_A:W_A + KV_A * HD_A], kg_ref[...], ones_k_ref[...])
    v = p[:, W_A + KV_A * HD_A:ATT_IN]
    k_ref[0] = k
    v_ref[0] = v
    qb, kb, vb = _bf(q), _bf(k), _bf(v)
    outs = []
    for h in range(H_A):
        j = h // G_A
        s = _dot_nt(qb[:, h * HD_A:(h + 1) * HD_A], kb[:, j * HD_A:(j + 1) * HD_A])
        outs.append(_sink_softmax_pv([(s, vb[:, j * HD_A:(j + 1) * HD_A])], sink_ref[h]))
    o_ref[...] = jnp.concatenate(outs, axis=1)


def _ctx_attention(p_att, n_seq, seq, qg_t, kg_t, sink, ones_q, ones_k, prev):
    kv_w = KV_A * HD_A
    return pl.pallas_call(
        _ctx_attn_kernel,
        grid=(n_seq,),
        in_specs=[pl.BlockSpec((seq, ATT_IN), lambda b: (b, 0)),
                  pl.BlockSpec((1, W_A), lambda b: (0, 0)),
                  pl.BlockSpec((1, kv_w), lambda b: (0, 0)),
                  pl.BlockSpec(memory_space=pltpu.SMEM),
                  pl.BlockSpec(ones_q.shape, lambda b: (0, 0)),
                  pl.BlockSpec(ones_k.shape, lambda b: (0, 0)),
                  pl.BlockSpec(memory_space=pl.ANY)],
        out_specs=[pl.BlockSpec((seq, W_A), lambda b: (b, 0)),
                   pl.BlockSpec((1, seq, kv_w), lambda b: (b, 0, 0)),
                   pl.BlockSpec((1, seq, kv_w), lambda b: (b, 0, 0))],
        input_output_aliases={6: 0},
        out_shape=[jax.ShapeDtypeStruct(prev.shape, F32),
                   jax.ShapeDtypeStruct((n_seq, seq, kv_w), F32),
                   jax.ShapeDtypeStruct((n_seq, seq, kv_w), F32)],
        compiler_params=_cparams(1),
        name="ctx_attention",
    )(p_att, qg_t, kg_t, sink, ones_q, ones_k, prev)


def _rope(x, cos_t, sin_t):
    lane = lax.broadcasted_iota(jnp.int32, cos_t.shape, 1)
    low = (lane % 32) < 16
    outs = []
    for s in range(x.shape[1] // LANES):
        xs = x[:, s * LANES:(s + 1) * LANES]
        partner = jnp.where(low, pltpu.roll(xs, LANES - 16, 1), pltpu.roll(xs, 16, 1))
        outs.append(xs * cos_t + partner * sin_t)
    return outs[0] if len(outs) == 1 else jnp.concatenate(outs, axis=1)


def _lat_attn_kernel(seq, p_ref, qg_ref, kg_ref, sink_ref, ones_q_ref, ones_k_ref, cos_ref, sin_ref,
                     kc_ref, vc_ref, prev_ref, o_ref, q_scr, k_scr, v_scr):
    del prev_ref
    kv_w = KV_A * HD_A
    p = p_ref[...]
    q = _head_rms(p[:, :W_A], qg_ref[...], ones_q_ref[...])
    k = _head_rms(p[:, W_A:W_A + kv_w], kg_ref[...], ones_k_ref[...])
    qr = _bf(_rope(q, cos_ref[...], sin_ref[...]) * ATTN_SCALE)
    kr = _bf(_rope(k, cos_ref[...], sin_ref[...]))
    vb = _bf(p[:, W_A + kv_w:ATT_IN])
    for h in range(H_A):
        q_scr[h] = qr[:, h * HD_A:(h + 1) * HD_A]
    for j in range(KV_A):
        k_scr[j] = kr[:, j * HD_A:(j + 1) * HD_A]
        v_scr[j] = vb[:, j * HD_A:(j + 1) * HD_A]
    kc = _bf(kc_ref[0])
    vc = _bf(vc_ref[0])
    n_local = 3 * QBLK
    grp = lax.broadcasted_iota(jnp.int32, (G_A * QBLK, 1), 0) // QBLK

    def block(i, carry):
        q0 = pl.multiple_of(i * QBLK, QBLK)
        start = pl.multiple_of(jnp.clip((i - 1) * QBLK, 0, seq - n_local), QBLK)
        ipos = q0 + lax.broadcasted_iota(jnp.int32, (G_A * QBLK, n_local), 0) % QBLK
        jpos = start + lax.broadcasted_iota(jnp.int32, (G_A * QBLK, n_local), 1)
        band = jnp.abs(jpos - ipos) <= WINDOW
        outs = []
        for j in range(KV_A):
            qs = jnp.concatenate([q_scr[j * G_A + g, pl.ds(q0, QBLK), :] for g in range(G_A)], axis=0)
            kl = k_scr[j, pl.ds(start, n_local), :]
            vl = v_scr[j, pl.ds(start, n_local), :]
            sink = jnp.zeros((G_A * QBLK, 1), F32)
            for g in range(G_A):
                sink = jnp.where(grp == g, sink_ref[j * G_A + g], sink)
            s_loc = jnp.where(band, _dot_nt(qs, kl), NEG_INF)
            s_ctx = _dot_nt(qs, kc[:, j * HD_A:(j + 1) * HD_A])
            o = _sink_softmax_pv([(s_loc, vl), (s_ctx, vc[:, j * HD_A:(j + 1) * HD_A])], sink)
            outs.extend(o[g * QBLK:(g + 1) * QBLK] for g in range(G_A))
        o_ref[pl.ds(q0, QBLK), :] = jnp.concatenate(outs, axis=1)
        return carry

    lax.fori_loop(0, seq // QBLK, block, 0)


def _lat_attention(p_att, row_blk0, n_seq, seq, qg_t, kg_t, sink, ones_q, ones_k, cos_t, sin_t, kc, vc, prev):
    kv_w = KV_A * HD_A
    past = kc.shape[1]
    return pl.pallas_call(
        functools.partial(_lat_attn_kernel, seq),
        grid=(n_seq,),
        in_specs=[pl.BlockSpec((seq, ATT_IN), lambda b: (row_blk0 + b, 0)),
                  pl.BlockSpec((1, W_A), lambda b: (0, 0)),
                  pl.BlockSpec((1, kv_w), lambda b: (0, 0)),
                  pl.BlockSpec(memory_space=pltpu.SMEM),
                  pl.BlockSpec(ones_q.shape, lambda b: (0, 0)),
                  pl.BlockSpec(ones_k.shape, lambda b: (0, 0)),
                  pl.BlockSpec((seq, LANES), lambda b: (0, 0)),
                  pl.BlockSpec((seq, LANES), lambda b: (0, 0)),
                  pl.BlockSpec((1, past, kv_w), lambda b: (b, 0, 0)),
                  pl.BlockSpec((1, past, kv_w), lambda b: (b, 0, 0)),
                  pl.BlockSpec(memory_space=pl.ANY)],
        out_specs=pl.BlockSpec((seq, W_A), lambda b: (row_blk0 + b, 0)),
        out_shape=jax.ShapeDtypeStruct(prev.shape, F32),
        input_output_aliases={10: 0},
        scratch_shapes=[pltpu.VMEM((H_A, seq, HD_A), BF16), pltpu.VMEM((KV_A, seq, HD_A), BF16),
                        pltpu.VMEM((KV_A, seq, HD_A), BF16)],
        compiler_params=_cparams(1),
        name="lat_attention",
    )(p_att, qg_t, kg_t, sink, ones_q, ones_k, cos_t, sin_t, kc, vc, prev)


def _rope_tables(seq):
    pos = np.arange(seq)
    row = (pos // GRID_W).astype(np.float32)
    col = (pos % GRID_W).astype(np.float32)
    d_axis = HD_A // 2
    inv = (ROPE_BASE ** (-np.arange(0, d_axis, 2, dtype=np.float32) / d_axis)).astype(np.float32)
    cos_h = np.zeros((seq, HD_A), np.float32)
    sin_h = np.zeros((seq, HD_A), np.float32)
    for seg, p_ in enumerate((row, col)):
        ang = (p_[:, None] * inv[None, :]).astype(np.float32)
        c, s = np.cos(ang), np.sin(ang)
        base = seg * d_axis
        cos_h[:, base:base + d_axis // 2] = c
        cos_h[:, base + d_axis // 2:base + d_axis] = c
        sin_h[:, base:base + d_axis // 2] = -s
        sin_h[:, base + d_axis // 2:base + d_axis] = s
    rep = LANES // HD_A
    return jnp.asarray(np.tile(cos_h, (1, rep))), jnp.asarray(np.tile(sin_h, (1, rep)))


def _rwkv_prep_kernel(rows, x_ref, prev_ref, next_ref, mu_ref, kk_ref, ka_ref, rk_ref, w0_ref, w2_ref,
                      a0_ref, a2_ref, g2_ref, ones_ref,
                      nkk_ref, r_ref, v_ref, g_ref, bonus_ref,
                      wf_ref, kaf_ref, kdf_ref, wb_ref, kab_ref, kdb_ref):
    i = pl.program_id(0)
    nctx_blk = rows.n_ctx // PREP_TM
    per_seq = rows.lat_seq // PREP_TM
    is_ctx = i < nctx_blk
    first = jnp.logical_or(is_ctx, (i - nctx_blk) % per_seq == 0)
    last = jnp.logical_or(is_ctx, (i - nctx_blk) % per_seq == per_seq - 1)
    x = x_ref[...]
    ridx = lax.broadcasted_iota(jnp.int32, x.shape, 0)
    prev_row = jnp.where(first, 0.0, prev_ref[SUBLANES - 1:SUBLANES, :])
    next_row = jnp.where(last, 0.0, next_ref[0:1, :])
    xm1 = jnp.where(ridx == 0, prev_row, pltpu.roll(x, 1, 0))
    xp1 = jnp.where(ridx == PREP_TM - 1, next_row, pltpu.roll(x, PREP_TM - 1, 0))
    pw = x + (0.5 * (xm1 + xp1) - x) * mu_ref[...]

    r = pw[:, 0:W_B]
    k = pw[:, W_B:2 * W_B]
    v = pw[:, 2 * W_B:3 * W_B]
    wd = pw[:, 3 * W_B:3 * W_B + LORA_W]
    ad = pw[:, 3 * W_B + LORA_W:3 * W_B + LORA_W + LORA_A]
    gd = pw[:, 3 * W_B + LORA_W + LORA_A:]
    ones2 = ones_ref[...]

    kk = k * kk_ref[...]
    kk = kk / jnp.maximum(jnp.sqrt(_seg_sum(kk * kk, ones2)), 1e-12)
    nkk_ref[...] = -kk
    r_ref[...] = r
    v_ref[...] = v
    g_ref[...] = jnp.dot(_bf(_sigmoid(gd)), g2_ref[...], preferred_element_type=F32)
    tw = _bf(jnp.tanh(wd))
    adb = _bf(ad)
    bonus = jnp.zeros_like(r)
    for d, (w_o, ka_o, kd_o) in enumerate(((wf_ref, kaf_ref, kdf_ref), (wb_ref, kab_ref, kdb_ref))):
        z = -(w0_ref[d:d + 1, :] + jnp.dot(tw, w2_ref[d], preferred_element_type=F32))
        softplus = jnp.maximum(z, 0.0) + jnp.log(1.0 + jnp.exp(-jnp.abs(z)))
        w_o[...] = jnp.exp(-jnp.exp(-softplus - 0.5))
        a = _sigmoid(a0_ref[d:d + 1, :] + jnp.dot(adb, a2_ref[d], preferred_element_type=F32))
        kd = k * (1.0 + (a - 1.0) * ka_ref[...])
        ka_o[...] = kk * a
        kd_o[...] = kd
        bonus = bonus + _seg_sum(r * kd * rk_ref[...], ones2) * v
    bonus_ref[...] = bonus


def _rwkv_prep(rows, p_rw, pr, ones_b):
    n = rows.n
    n_halo = n // SUBLANES
    blk_halo = PREP_TM // SUBLANES
    row = lambda a: a.reshape(1, -1)
    full = lambda a: pl.BlockSpec(a.shape, lambda i: (0,) * a.ndim)
    consts = [row(pr['mu']), row(pr['k_k']), row(pr['k_a']), row(pr['r_k']), pr['w0'], _bf(pr['w2']),
              pr['a0'], _bf(pr['a2']), _bf(pr['g2']), ones_b]
    outs = pl.pallas_call(
        functools.partial(_rwkv_prep_kernel, rows),
        grid=(n // PREP_TM,),
        in_specs=[pl.BlockSpec((PREP_TM, RWKV_IN), lambda i: (i, 0)),
                  pl.BlockSpec((SUBLANES, RWKV_IN), lambda i: (jnp.maximum(i * blk_halo - 1, 0), 0)),
                  pl.BlockSpec((SUBLANES, RWKV_IN), lambda i: (jnp.minimum((i + 1) * blk_halo, n_halo - 1), 0))]
                 + [full(a) for a in consts],
        out_specs=[pl.BlockSpec((PREP_TM, W_B), lambda i: (i, 0))] * 11,
        out_shape=[jax.ShapeDtypeStruct((n, W_B), F32)] * 11,
        compiler_params=_cparams(1),
        name="rwkv_prep",
    )(p_rw, p_rw, p_rw, *consts)
    names = ('nkk', 'r', 'v', 'g', 'bonus', 'w_f', 'ka_f', 'kd_f', 'w_b', 'ka_b', 'kd_b')
    return dict(zip(names, outs))


def _rwkv_scan_kernel(n_tb, nkkf_ref, rf_ref, vf_ref, wf_ref, kaf_ref, kdf_ref,
                      nkkb_ref, rb_ref, vb_ref, wb_ref, kab_ref, kdb_ref,
                      s0f_ref, s0b_ref, ones_ref, prevf_ref, prevb_ref,
                      of_ref, ob_ref, sff_ref, sfb_ref, sf_scr, sb_scr, vt_scr):
    del prevf_ref, prevb_ref
    s_scr = (sf_scr, sb_scr)
    tb = pl.program_id(1)
    n_pair = H_B // 2
    half = RW_TB // 2
    dirs = ((nkkf_ref, rf_ref, vf_ref, wf_ref, kaf_ref, kdf_ref, of_ref, False),
            (nkkb_ref, rb_ref, vb_ref, wb_ref, kab_ref, kdb_ref, ob_ref, True))

    @pl.when(tb == 0)
    def _():
        sf_scr[...] = s0f_ref[...]
        sb_scr[...] = s0b_ref[...]

    lane = lax.broadcasted_iota(jnp.int32, (HD_B, LANES), 1)
    for d, refs in enumerate(dirs):
        v_ref = refs[2]
        for bb in range(RW_BB):
            for p in range(n_pair):
                vt = v_ref[bb, :, p * LANES:(p + 1) * LANES].T
                top, bot = vt[:HD_B], vt[HD_B:]
                for s in range(2):
                    if s == 0:
                        t2 = jnp.where(lane < HD_B, top, pltpu.roll(bot, HD_B, 1))
                    else:
                        t2 = jnp.where(lane < HD_B, pltpu.roll(top, HD_B, 1), bot)
                    vt_scr[d, bb, p, s] = t2

    ones2 = ones_ref[...]
    row8 = lax.broadcasted_iota(jnp.int32, (SUBLANES, LANES), 0)
    lane8 = lax.broadcasted_iota(jnp.int32, (SUBLANES, LANES), 1)
    sel_r = jnp.logical_or(jnp.logical_and(row8 % 2 == 0, lane8 < HD_B),
                           jnp.logical_and(row8 % 2 == 1, lane8 >= HD_B))

    def row_of(rev, tt):
        return RW_TB - 1 - tt if rev else tt

    def emit_output(d, bb, tau):
        r_ref, o_ref = dirs[d][1], dirs[d][6]
        r = r_ref[bb, pl.ds(tau, 1), :]
        r8 = jnp.zeros((SUBLANES, LANES), F32)
        for p in range(n_pair):
            rp = jnp.broadcast_to(r[:, p * LANES:(p + 1) * LANES], (SUBLANES, LANES))
            r8 = jnp.where(jnp.logical_and(sel_r, row8 // 2 == p), rp, r8)
        s_all = jnp.concatenate([_bf(s_scr[d][bb, p]) for p in range(n_pair)], axis=0)
        o8 = _dot_nt(_bf(r8), s_all)
        o_parts = []
        for p in range(n_pair):
            for h in range(2):
                o_parts.append(o8[2 * p + h:2 * p + h + 1, p * HD_B:(p + 1) * HD_B])
        o_ref[bb, pl.ds(tau, 1), :] = jnp.concatenate(o_parts, axis=1)

    groups = [(d, bbs) for d in range(2) for bbs in _chunks(range(RW_BB), RW_GROUP_BB)]

    def reduce_phase(grp, tt):
        d, bbs = grp
        rev = dirs[d][7]
        tau = row_of(rev, tt)
        sub = tau // half
        lt = tau % half
        mask = jnp.logical_or(lane == lt, lane == lt + HD_B)
        lhs = []
        for bb in bbs:
            emit_output(d, bb, row_of(rev, jnp.maximum(tt - 1, 0)))
            nkk = dirs[d][0][bb, pl.ds(tau, 1), :]
            for p in range(n_pair):
                prod = s_scr[d][bb, p] * nkk[:, p * LANES:(p + 1) * LANES]
                lhs.append(jnp.concatenate([_bf(prod), _bf(jnp.where(mask, vt_scr[d, bb, p, sub], 0.0))],
                                           axis=1))
        return jnp.dot(jnp.concatenate(lhs, axis=0), ones2, preferred_element_type=F32)

    def update_phase(grp, tt, red):
        d, bbs = grp
        _, _, _, w_ref, ka_ref, kd_ref, _, rev = dirs[d]
        tau = row_of(rev, tt)
        for k, bb in enumerate(bbs):
            w = w_ref[bb, pl.ds(tau, 1), :]
            ka = ka_ref[bb, pl.ds(tau, 1), :]
            kd = kd_ref[bb, pl.ds(tau, 1), :]
            for p in range(n_pair):
                sl = slice(p * LANES, (p + 1) * LANES)
                r0 = (k * n_pair + p) * HD_B
                sa = red[r0:r0 + HD_B, :LANES]
                vcol = red[r0:r0 + HD_B, LANES:]
                s_scr[d][bb, p] = s_scr[d][bb, p] * w[:, sl] + sa * ka[:, sl] + vcol * kd[:, sl]

    def step(tt, carry):
        reds = [reduce_phase(g, tt) for g in groups]
        for g, red in zip(groups, reds):
            update_phase(g, tt, red)
        return carry

    lax.fori_loop(0, RW_TB, step, 0)
    for d in range(2):
        for bb in range(RW_BB):
            emit_output(d, bb, row_of(dirs[d][7], RW_TB - 1))

    @pl.when(tb == n_tb - 1)
    def _():
        sff_ref[...] = sf_scr[...]
        sfb_ref[...] = sb_scr[...]


def _rwkv_scan(pp, row0, n_seq, seq, s0_f, s0_b, ones_pair, prev_f, prev_b):
    n_tb = seq // RW_TB
    n_pair = H_B // 2
    blk0 = row0 // seq
    view = lambda a: a.reshape(a.shape[0] // seq, seq, W_B)
    fwd = pl.BlockSpec((RW_BB, RW_TB, W_B), lambda b, t: (blk0 // RW_BB + b, t, 0))
    bwd = pl.BlockSpec((RW_BB, RW_TB, W_B), lambda b, t: (blk0 // RW_BB + b, n_tb - 1 - t, 0))
    st = pl.BlockSpec((RW_BB, n_pair, HD_B, LANES), lambda b, t: (b, 0, 0, 0))
    ins_f = [view(pp[k]) for k in ('nkk', 'r', 'v', 'w_f', 'ka_f', 'kd_f')]
    ins_b = [view(pp[k]) for k in ('nkk', 'r', 'v', 'w_b', 'ka_b', 'kd_b')]
    st_shape = jax.ShapeDtypeStruct((n_seq, n_pair, HD_B, LANES), F32)
    o_shape = jax.ShapeDtypeStruct(view(prev_f).shape, F32)
    any_spec = pl.BlockSpec(memory_space=pl.ANY)
    o_f, o_b, sf, sb = pl.pallas_call(
        functools.partial(_rwkv_scan_kernel, n_tb),
        grid=(n_seq // RW_BB, n_tb),
        in_specs=[fwd] * 6 + [bwd] * 6 + [st, st, pl.BlockSpec(ones_pair.shape, lambda b, t: (0, 0)),
                                           any_spec, any_spec],
        out_specs=[fwd, bwd, st, st],
        out_shape=[o_shape, o_shape, st_shape, st_shape],
        input_output_aliases={15: 0, 16: 1},
        scratch_shapes=[pltpu.VMEM((RW_BB, n_pair, HD_B, LANES), F32),
                        pltpu.VMEM((RW_BB, n_pair, HD_B, LANES), F32),
                        pltpu.VMEM((2, RW_BB, n_pair, 2, HD_B, LANES), F32)],
        compiler_params=_cparams(2),
        name="rwkv_scan",
    )(*ins_f, *ins_b, s0_f, s0_b, ones_pair, view(prev_f), view(prev_b))
    return o_f.reshape(prev_f.shape), o_b.reshape(prev_b.shape), sf, sb


def _state_to_pairs(s):
    b = s.shape[0]
    return s.reshape(b, H_B // 2, 2, HD_B, HD_B).transpose(0, 1, 3, 2, 4).reshape(b, H_B // 2, HD_B, 2 * HD_B)


def _pairs_to_state(s):
    b = s.shape[0]
    return s.reshape(b, H_B // 2, HD_B, 2, HD_B).transpose(0, 1, 3, 2, 4).reshape(b, H_B, HD_B, HD_B)


def _tail(x, y, g1, n2g, sc2, sh2, x1_ref, h_ref, hp_ref):
    x1 = x + g1 * y
    x1_ref[...] = x1
    h = _rms_mod(x1, n2g, sc2, sh2)
    h_ref[...] = _bf(h)
    hp_ref[...] = _pack_pairs(h)


def _outproj0_kernel(rows, xa_ref, xb_ref, oa_ref, of_ref, ob_ref, bonus_ref, g_ref, lnw_ref, lnb_ref, ones_ref,
                     w_ref, g1_ref, n2g_ref, sc2_ref, sh2_ref, x1_ref, h_ref, hp_ref):
    o_sum = of_ref[...] + ob_ref[...]
    ones2 = ones_ref[...]
    mean = _seg_sum(o_sum, ones2) * (1.0 / HD_B)
    cen = o_sum - mean
    var = _seg_sum(cen * cen, ones2) * (1.0 / HD_B)
    gn = cen * lax.rsqrt(var + GN_EPS) * lnw_ref[...] + lnb_ref[...]
    o_rw = (gn + bonus_ref[...]) * g_ref[...]
    mix = jnp.concatenate([_bf(oa_ref[...]), _bf(o_rw)], axis=1)
    y = jnp.dot(mix, w_ref[...], preferred_element_type=F32)
    _tail(_pick_x(rows, xa_ref, xb_ref), y, g1_ref[...], n2g_ref[...], sc2_ref[...], sh2_ref[...], x1_ref, h_ref, hp_ref)


def _outproj0(rows, xs, o_att, o_f, o_b, pp, pr, ones_b, w_out_bf, n2g, mod4, layer):
    n = rows.n
    tok = lambda w: pl.BlockSpec((TM, w), lambda i: (i, 0))
    const = lambda a: pl.BlockSpec(a.shape, lambda i: (0,) * a.ndim)
    lnw, lnb, n2 = pr['ln_w'].reshape(1, -1), pr['ln_b'].reshape(1, -1), n2g.reshape(1, -1)
    return pl.pallas_call(
        functools.partial(_outproj0_kernel, rows),
        grid=(n // TM,),
        in_specs=_x_specs(rows, xs) + [tok(W_A), tok(W_B), tok(W_B), tok(W_B), tok(W_B),
                  const(lnw), const(lnb), const(ones_b), const(w_out_bf),
                  rows.mod_spec(layer, 2, TM), const(n2), rows.mod_spec(layer, 4, TM), rows.mod_spec(layer, 3, TM)],
        out_specs=[tok(D_MODEL), tok(D_MODEL), tok(D_MODEL // 2)],
        out_shape=[jax.ShapeDtypeStruct((n, D_MODEL), F32), jax.ShapeDtypeStruct((n, D_MODEL), BF16),
                   jax.ShapeDtypeStruct((n, D_MODEL // 2), jnp.uint32)],
        compiler_params=_cparams(1),
        name="outproj0",
    )(*xs, o_att, o_f, o_b, pp['bonus'], pp['g'], lnw, lnb, ones_b, w_out_bf, mod4, n2, mod4, mod4)


def _outproj1_kernel(rows, xa_ref, xb_ref, of_ref, ob_ref, gate_ref, ng_ref, w_ref, g1_ref, n2g_ref, sc2_ref, sh2_ref,
                     x1_ref, h_ref, hp_ref):
    o_sum = of_ref[...] + ob_ref[...]
    parts = []
    for h in range(H_C):
        oh = o_sum[:, h * DV_C:(h + 1) * DV_C]
        parts.append(oh * lax.rsqrt(jnp.mean(oh * oh, axis=-1, keepdims=True) + EPS))
    o = jnp.concatenate(parts, axis=1) * ng_ref[...] * _silu(gate_ref[...])
    y = jnp.dot(_bf(o), w_ref[...], preferred_element_type=F32)
    _tail(_pick_x(rows, xa_ref, xb_ref), y, g1_ref[...], n2g_ref[...], sc2_ref[...], sh2_ref[...], x1_ref, h_ref, hp_ref)


def _outproj1(rows, xs, o_f, o_b, p1, norm_g, w_out_bf, n2g, mod4, layer):
    n = rows.n
    tok = lambda w: pl.BlockSpec((TM, w), lambda i: (i, 0))
    const = lambda a: pl.BlockSpec(a.shape, lambda i: (0,) * a.ndim)
    ng, n2 = norm_g.reshape(1, -1), n2g.reshape(1, -1)
    return pl.pallas_call(
        functools.partial(_outproj1_kernel, rows),
        grid=(n // TM,),
        in_specs=_x_specs(rows, xs) + [tok(D_C), tok(D_C), pl.BlockSpec((TM, D_C), lambda i: (i, 4)),
                  const(ng), const(w_out_bf),
                  rows.mod_spec(layer, 2, TM), const(n2), rows.mod_spec(layer, 4, TM), rows.mod_spec(layer, 3, TM)],
        out_specs=[tok(D_MODEL), tok(D_MODEL), tok(D_MODEL // 2)],
        out_shape=[jax.ShapeDtypeStruct((n, D_MODEL), F32), jax.ShapeDtypeStruct((n, D_MODEL), BF16),
                   jax.ShapeDtypeStruct((n, D_MODEL // 2), jnp.uint32)],
        compiler_params=_cparams(1),
        name="outproj1",
    )(*xs, o_f, o_b, p1, ng, w_out_bf, mod4, n2, mod4, mod4)


def _hgrn_kernel(n_tb, qf_ref, ff_ref, if_ref, qb_ref, fb_ref, ib_ref, lbp_ref, s0f_ref, s0b_ref,
                 trif_ref, trib_ref, prevf_ref, prevb_ref, of_ref, ob_ref, sff_ref, sfb_ref, s_scr):
    del prevf_ref, prevb_ref
    tb = pl.program_id(1)

    @pl.when(tb == 0)
    def _():
        for h in range(H_C):
            s_scr[0, h] = s0f_ref[0, h].T
            s_scr[1, h] = s0b_ref[0, h].T

    lbp = lbp_ref[...]
    e = jnp.exp(lbp - jnp.max(lbp, axis=0, keepdims=True))
    sm = e / jnp.sum(e, axis=0, keepdims=True)
    lb = (sm[0:1] + sm[1:2]) - sm[0:1]

    n_chunk = HG_TB // CHUNK
    ti = lax.broadcasted_iota(jnp.int32, (HG_TB, HG_TB), 0)
    si = lax.broadcasted_iota(jnp.int32, (HG_TB, HG_TB), 1)
    same = (ti // CHUNK) == (si // CHUNK)
    dirs = ((qf_ref, ff_ref, if_ref, of_ref, trif_ref, jnp.logical_and(same, ti >= si), CHUNK - 1, False),
            (qb_ref, fb_ref, ib_ref, ob_ref, trib_ref, jnp.logical_and(same, ti <= si), 0, True))

    staged = []
    for d, (q_ref, f_ref, i_ref, o_ref, tri_ref, causal, last_row, rev) in enumerate(dirs):
        q = _silu(q_ref[...])
        f = lb + (1.0 - lb) * _sigmoid(f_ref[...])
        k = 1.0 - f
        v = _bf(i_ref[...])
        g = jnp.log(f)
        g1 = _bf(g)
        g2 = _bf(g - g1.astype(F32))
        tri2 = tri_ref[...]
        b_parts, last_parts, dec = [], [], []
        for c in range(n_chunk):
            rc = slice(c * CHUNK, (c + 1) * CHUNK)
            bc = jnp.dot(tri2, jnp.concatenate([g1[rc], g2[rc]], axis=0), preferred_element_type=F32)
            b_parts.append(bc)
            last = bc[last_row:last_row + 1]
            last_parts.append(jnp.broadcast_to(last, bc.shape))
            dec.append(jnp.exp(last))
        b = jnp.concatenate(b_parts, axis=0)
        b_last = jnp.concatenate(last_parts, axis=0)
        staged.append((_bf(q * jnp.exp(b)), _bf(k * jnp.exp(-b)), _bf(k * jnp.exp(b_last - b)), v, dec))

    for h in range(H_C):
        sl = slice(h * DK_C, (h + 1) * DK_C)
        for d, (q_ref, f_ref, i_ref, o_ref, tri_ref, causal, last_row, rev) in enumerate(dirs):
            q_in, k_in, k_out, v, dec = staged[d]
            qh, vh = q_in[:, sl], v[:, sl]
            att = jnp.where(causal, _dot_nt(qh, k_in[:, sl]), 0.0)
            o_intra = jnp.dot(_bf(att), vh, preferred_element_type=F32)
            s_t = s_scr[d, h]
            for c in (range(n_chunk - 1, -1, -1) if rev else range(n_chunk)):
                rc = slice(c * CHUNK, (c + 1) * CHUNK)
                o_ref[rc, sl] = o_intra[rc] + _dot_nt(qh[rc], _bf(s_t))
                s_t = dec[c][:, sl] * s_t + _dot_tn(vh[rc], k_out[rc, sl])
            s_scr[d, h] = s_t

    @pl.when(tb == n_tb - 1)
    def _():
        for h in range(H_C):
            sff_ref[0, h] = s_scr[0, h].T
            sfb_ref[0, h] = s_scr[1, h].T


def _hgrn_scan(p1, row0, n_seq, seq, lb_params, s0_f, s0_b, prev_f, prev_b):
    n_tb = seq // HG_TB
    blk0 = row0 // HG_TB
    tri = np.tril(np.ones((CHUNK, CHUNK), np.float32))
    tri_f = jnp.asarray(np.concatenate([tri] * 2, axis=1), dtype=BF16)
    tri_b = jnp.asarray(np.concatenate([tri.T] * 2, axis=1), dtype=BF16)
    fwd = lambda col: pl.BlockSpec((HG_TB, D_C), lambda b, t: (blk0 + b * n_tb + t, col))
    bwd = lambda col: pl.BlockSpec((HG_TB, D_C), lambda b, t: (blk0 + b * n_tb + n_tb - 1 - t, col))
    st = pl.BlockSpec((1, H_C, DK_C, DV_C), lambda b, t: (b, 0, 0, 0))
    const = lambda a: pl.BlockSpec(a.shape, lambda b, t: (0,) * a.ndim)
    o_shape = jax.ShapeDtypeStruct(prev_f.shape, F32)
    st_shape = jax.ShapeDtypeStruct((n_seq, H_C, DK_C, DV_C), F32)
    any_spec = pl.BlockSpec(memory_space=pl.ANY)
    return pl.pallas_call(
        functools.partial(_hgrn_kernel, n_tb),
        grid=(n_seq, n_tb),
        in_specs=[fwd(0), fwd(1), fwd(3), bwd(0), bwd(2), bwd(3), const(lb_params), st, st,
                  const(tri_f), const(tri_b), any_spec, any_spec],
        out_specs=[fwd(0), bwd(0), st, st],
        out_shape=[o_shape, o_shape, st_shape, st_shape],
        input_output_aliases={11: 0, 12: 1},
        scratch_shapes=[pltpu.VMEM((2, H_C, DV_C, DK_C), F32)],
        compiler_params=_cparams(2),
        name="hgrn_scan",
    )(p1, p1, p1, p1, p1, p1, lb_params, s0_f, s0_b, tri_f, tri_b, prev_f, prev_b)


def _router_kernel(h_ref, rhi_ref, rlo_ref, bias_ref, sel_ref, eidx_ref, ew_ref, cnt_ref):
    x = h_ref[...]
    tm = x.shape[0]
    logits = _dot_nt(rhi_ref[...], x) + _dot_nt(rlo_ref[...], x)
    scores = _sigmoid(logits)
    biased = scores + bias_ref[...]
    per = N_EXPERTS // N_GROUPS
    sub = lax.broadcasted_iota(jnp.int32, (per, tm), 0)
    gs_rows = []
    for g in range(N_GROUPS):
        blk = biased[g * per:(g + 1) * per]
        m1 = jnp.max(blk, axis=0, keepdims=True)
        first = jnp.min(jnp.where(blk == m1, sub, per), axis=0, keepdims=True)
        m2 = jnp.max(jnp.where(sub == first, -jnp.inf, blk), axis=0, keepdims=True)
        gs_rows.append(m1 + m2)
    gs = jnp.concatenate(gs_rows, axis=0)
    gi = lax.broadcasted_iota(jnp.int32, gs.shape, 0)
    rank = jnp.zeros(gs.shape, jnp.int32)
    for s in range(1, N_GROUPS):
        other = pltpu.roll(gs, s, 0)
        oi = pltpu.roll(gi, s, 0)
        beats = jnp.logical_or(other > gs, jnp.logical_and(other == gs, oi < gi))
        rank = rank + jnp.where(beats, 1, 0)
    keep = jnp.where(rank < TOPK_GROUPS, 1.0, 0.0)
    emask = jnp.concatenate([jnp.broadcast_to(keep[g:g + 1], (per, tm)) for g in range(N_GROUPS)], axis=0)
    cur = jnp.where(emask > 0.0, biased, -jnp.inf)
    ei = lax.broadcasted_iota(jnp.int32, cur.shape, 0)
    sel = jnp.zeros(cur.shape, F32)
    idxs, vals = [], []
    for _ in range(TOP_K):
        m = jnp.max(cur, axis=0, keepdims=True)
        idx = jnp.min(jnp.where(cur == m, ei, N_EXPERTS), axis=0, keepdims=True)
        pick = ei == idx
        idxs.append(idx)
        vals.append(jnp.sum(jnp.where(pick, scores, 0.0), axis=0, keepdims=True))
        sel = jnp.where(pick, 1.0, sel)
        cur = jnp.where(pick, -jnp.inf, cur)
    w = jnp.concatenate(vals, axis=0)
    eidx_ref[...] = jnp.concatenate(idxs, axis=0)
    ew_ref[...] = w / jnp.sum(w, axis=0, keepdims=True) * ROUTED_SCALE
    sel_ref[...] = _bf(sel)

    @pl.when(pl.program_id(0) == 0)
    def _():
        cnt_ref[...] = jnp.zeros_like(cnt_ref)

    cnt_ref[...] += jnp.sum(sel, axis=1, keepdims=True)


def _router(hffn, router, bias):
    n = hffn.shape[0]
    r_t = router.T
    r_hi = _bf(r_t)
    r_lo = _bf(r_t - r_hi.astype(F32))
    const = lambda a: pl.BlockSpec(a.shape, lambda i: (0,) * a.ndim)
    b_col = bias.reshape(N_EXPERTS, 1)
    return pl.pallas_call(
        _router_kernel,
        grid=(n // TM,),
        in_specs=[pl.BlockSpec((TM, D_MODEL), lambda i: (i, 0)), const(r_hi), const(r_lo), const(b_col)],
        out_specs=[pl.BlockSpec((N_EXPERTS, TM), lambda i: (0, i)),
                   pl.BlockSpec((TOP_K, TM), lambda i: (0, i)),
                   pl.BlockSpec((TOP_K, TM), lambda i: (0, i)),
                   pl.BlockSpec((N_EXPERTS, LANES), lambda i: (0, 0))],
        out_shape=[jax.ShapeDtypeStruct((N_EXPERTS, n), BF16),
                   jax.ShapeDtypeStruct((TOP_K, n), jnp.int32),
                   jax.ShapeDtypeStruct((TOP_K, n), F32),
                   jax.ShapeDtypeStruct((N_EXPERTS, LANES), F32)],
        compiler_params=_cparams(1),
        name="router",
    )(hffn, r_hi, r_lo, b_col)


def _positions_kernel(sel_ref, eidx_ref, base_ref, upper_ref, pos_ref, carry_ref):
    @pl.when(pl.program_id(0) == 0)
    def _():
        carry_ref[...] = jnp.zeros_like(carry_ref)

    sel = sel_ref[...]
    rank = jnp.dot(sel, upper_ref[...], preferred_element_type=F32)
    pos_e = base_ref[:, 0:1] + carry_ref[:, 0:1] + rank
    ei = lax.broadcasted_iota(jnp.int32, pos_e.shape, 0)
    eidx = eidx_ref[...]
    rows = [jnp.sum(jnp.where(ei == eidx[k:k + 1], pos_e, 0.0), axis=0, keepdims=True) for k in range(TOP_K)]
    pos_ref[...] = jnp.concatenate(rows, axis=0).astype(jnp.int32)
    carry_ref[...] += jnp.sum(sel.astype(F32), axis=1, keepdims=True)


def _positions(sel, eidx, base):
    n = sel.shape[1]
    pb = POS_TB
    upper = jnp.asarray(np.triu(np.ones((pb, pb), np.float32), 1), dtype=BF16)
    return pl.pallas_call(
        _positions_kernel,
        grid=(n // pb,),
        in_specs=[pl.BlockSpec((N_EXPERTS, pb), lambda i: (0, i)),
                  pl.BlockSpec((TOP_K, pb), lambda i: (0, i)),
                  pl.BlockSpec((N_EXPERTS, LANES), lambda i: (0, 0)),
                  pl.BlockSpec((pb, pb), lambda i: (0, 0))],
        out_specs=pl.BlockSpec((TOP_K, pb), lambda i: (0, i)),
        out_shape=jax.ShapeDtypeStruct((TOP_K, n), jnp.int32),
        scratch_shapes=[pltpu.VMEM((N_EXPERTS, LANES), F32)],
        compiler_params=_cparams(1),
        name="positions",
    )(sel, eidx, base, upper)


def _pack_pairs(x):
    half = x.shape[1] // 2
    bits = lax.bitcast_convert_type(_bf(x).astype(F32), jnp.uint32)
    return (bits[:, :half] >> 16) | (bits[:, half:] & jnp.uint32(0xFFFF0000))


def _unpack_pairs(w):
    lo = lax.bitcast_convert_type(w << 16, F32)
    hi = lax.bitcast_convert_type(w & jnp.uint32(0xFFFF0000), F32)
    return jnp.concatenate([_bf(lo), _bf(hi)], axis=1)


def _sc_gather(table, idx):
    b, w = idx.shape[0], table.shape[1]
    n_workers = SC_CORES * SC_SUBCORES
    per_w = b // n_workers
    assert b % (n_workers * SC_CHUNK) == 0
    mesh = plsc.VectorSubcoreMesh(core_axis_name="c", subcore_axis_name="s")

    n_chunk = per_w // SC_CHUNK
    assert n_chunk % 2 == 0

    @functools.partial(
        pl.kernel, mesh=mesh, out_type=jax.ShapeDtypeStruct((b, w), table.dtype),
        scratch_types=[pltpu.VMEM((2, SC_CHUNK), jnp.int32), pltpu.VMEM((2, SC_CHUNK, w), table.dtype),
                       pltpu.SemaphoreType.DMA((2,))])
    def gather(table_hbm, idx_hbm, out_hbm, idx_v, rows_v, sems):
        wid = lax.axis_index("s") * SC_CORES + lax.axis_index("c")
        base = wid * per_w

        def start(c, slot):
            off = pl.multiple_of(base + c * SC_CHUNK, SC_CHUNK)
            pltpu.sync_copy(idx_hbm.at[pl.ds(off, SC_CHUNK)], idx_v.at[slot])
            pltpu.async_copy(table_hbm.at[idx_v.at[slot]], rows_v.at[slot], sems.at[slot])

        def finish(c, slot):
            off = pl.multiple_of(base + c * SC_CHUNK, SC_CHUNK)
            pltpu.make_async_copy(table_hbm.at[idx_v.at[slot]], rows_v.at[slot], sems.at[slot]).wait()
            pltpu.sync_copy(rows_v.at[slot], out_hbm.at[pl.ds(off, SC_CHUNK)])

        start(0, 0)

        @pl.loop(0, n_chunk, step=2)
        def _(c):
            start(c + 1, 1)
            finish(c, 0)

            @pl.when(c + 2 < n_chunk)
            def _():
                start(c + 2, 0)

            finish(c + 1, 1)

    return gather(table, idx)


def _sc_scatter(src, pos3, n_rows):
    n, w = src.shape
    n_workers = SC_CORES * SC_SUBCORES
    per_w = n // n_workers
    assert n % (n_workers * SC_CHUNK) == 0
    mesh = plsc.VectorSubcoreMesh(core_axis_name="c", subcore_axis_name="s")

    n_chunk = per_w // SC_CHUNK
    assert n_chunk % 2 == 0

    @functools.partial(
        pl.kernel, mesh=mesh, out_type=jax.ShapeDtypeStruct((n_rows, w), src.dtype),
        scratch_types=[pltpu.VMEM((2, TOP_K, SC_CHUNK), jnp.int32), pltpu.VMEM((2, SC_CHUNK, w), src.dtype),
                       pltpu.SemaphoreType.DMA((2,)), pltpu.SemaphoreType.DMA((2,))])
    def scatter(src_hbm, pos_hbm, out_hbm, idx_v, rows_v, ld_sems, sc_sems):
        wid = lax.axis_index("s") * SC_CORES + lax.axis_index("c")
        base = wid * per_w

        def loads(c, slot):
            off = pl.multiple_of(base + c * SC_CHUNK, SC_CHUNK)
            return (pltpu.make_async_copy(src_hbm.at[pl.ds(off, SC_CHUNK)], rows_v.at[slot], ld_sems.at[slot]),
                    pltpu.make_async_copy(pos_hbm.at[off // SC_CHUNK], idx_v.at[slot], ld_sems.at[slot]))

        def scatter_chunk(slot):
            copies = [pltpu.async_copy(rows_v.at[slot], out_hbm.at[idx_v.at[slot, k]], sc_sems.at[slot])
                      for k in range(TOP_K)]
            for cp in copies:
                cp.wait()

        def half_step(c, slot):
            for cp in loads(c, slot):
                cp.wait()

            @pl.when(c + 1 < n_chunk)
            def _():
                for cp in loads(c + 1, 1 - slot):
                    cp.start()

            scatter_chunk(slot)

        for cp in loads(0, 0):
            cp.start()

        @pl.loop(0, n_chunk, step=2)
        def _(c):
            half_step(c, 0)
            half_step(c + 1, 1)

    return scatter(src, pos3)


def _experts_kernel(te_ref, nu_ref, xs_ref, wg_ref, wu_ref, wd_ref, ys_ref, wg_bf, wu_bf, wd_bf):
    i = pl.program_id(0)
    active = i < nu_ref[0]

    @pl.when(jnp.logical_and(active, jnp.logical_or(i == 0, te_ref[i] != te_ref[jnp.maximum(i - 1, 0)])))
    def _():
        wg_bf[...] = _bf(wg_ref[0])
        wu_bf[...] = _bf(wu_ref[0])
        wd_bf[...] = _bf(wd_ref[0])

    @pl.when(active)
    def _():
        x = _unpack_pairs(xs_ref[...])
        act = _glu(x, wg_bf[...], wu_bf[...])
        ys_ref[...] = _pack_pairs(jnp.dot(_bf(act), wd_bf[...], preferred_element_type=F32))


def _experts(xs, tile_expert, n_used, mp, layer):
    n_tiles = xs.shape[0] // MOE_TILE
    half = D_MODEL // 2
    wspec = lambda shape: pl.BlockSpec((None, 1) + shape, lambda i, te, nu: (layer, te[i], 0, 0))
    return pl.pallas_call(
        _experts_kernel,
        grid_spec=pltpu.PrefetchScalarGridSpec(
            num_scalar_prefetch=2, grid=(n_tiles,),
            in_specs=[pl.BlockSpec((MOE_TILE, half), lambda i, te, nu: (jnp.minimum(i, nu[0] - 1), 0)),
                      wspec((D_MODEL, D_EXPERT)), wspec((D_MODEL, D_EXPERT)), wspec((D_EXPERT, D_MODEL))],
            out_specs=pl.BlockSpec((MOE_TILE, half), lambda i, te, nu: (jnp.minimum(i, nu[0] - 1), 0)),
            scratch_shapes=[pltpu.VMEM((D_MODEL, D_EXPERT), BF16), pltpu.VMEM((D_MODEL, D_EXPERT), BF16),
                            pltpu.VMEM((D_EXPERT, D_MODEL), BF16)]),
        out_shape=jax.ShapeDtypeStruct(xs.shape, jnp.uint32),
        compiler_params=_cparams(1),
        name="experts",
    )(tile_expert, n_used, xs, mp['wg'], mp['wu'], mp['wd'])


def _combine_kernel(h_ref, *refs):
    yg_refs = refs[:TOP_K]
    ew_ref, eye_ref, sg_ref, su_ref, sd_ref, x1_ref, g2_ref, o_ref = refs[TOP_K:]
    act = _glu(h_ref[...], sg_ref[...], su_ref[...])
    acc = jnp.dot(_bf(act), sd_ref[...], preferred_element_type=F32)
    ew = ew_ref[...]
    hi = _bf(ew)
    lo = _bf(ew - hi.astype(F32))
    ew_t = _dot_tn(hi, eye_ref[...]) + _dot_tn(lo, eye_ref[...])
    for k in range(TOP_K):
        acc = acc + ew_t[:, k:k + 1] * _unpack_pairs(yg_refs[k][...]).astype(F32)
    o_ref[...] = x1_ref[...] + g2_ref[...] * acc


def _combine(rows, hffn, yg, ew, mp, x1, mod4, layer, row0, n_out):
    half = D_MODEL // 2
    n_blk = rows.n // TM
    blk0 = row0 // TM
    const = lambda a: pl.BlockSpec(a.shape, lambda i: (0,) * a.ndim)
    tok = lambda w: pl.BlockSpec((TM, w), lambda i: (blk0 + i, 0))
    slot = lambda k: pl.BlockSpec((TM, half), lambda i: (k * n_blk + blk0 + i, 0))
    eye = jnp.eye(TOP_K, dtype=BF16)
    return pl.pallas_call(
        _combine_kernel,
        grid=(n_out // TM,),
        in_specs=[tok(D_MODEL)] + [slot(k) for k in range(TOP_K)]
                 + [pl.BlockSpec((TOP_K, TM), lambda i: (0, blk0 + i)), const(eye),
                    const(mp['sg']), const(mp['su']), const(mp['sd']), tok(D_MODEL),
                    rows.mod_spec(layer, 5, TM, blk0)],
        out_specs=pl.BlockSpec((TM, D_MODEL), lambda i: (i, 0)),
        out_shape=jax.ShapeDtypeStruct((n_out, D_MODEL), F32),
        compiler_params=_cparams(1),
        name=f"combine{layer}",
    )(hffn, *([yg] * TOP_K), ew, eye, mp['sg'], mp['su'], mp['sd'], x1, mod4)


def _moe(rows, hffn, hpack, x1, router, bias, mp, mod4, layer, out_ranges):
    n = rows.n
    sel, eidx, ew, cnt = _router(hffn, router, bias)
    counts = cnt[:, 0].astype(jnp.int32)
    padded = (counts + MOE_TILE - 1) // MOE_TILE * MOE_TILE
    ends = jnp.cumsum(padded)
    n_rows = n * TOP_K + N_EXPERTS * MOE_TILE
    n_tiles = n_rows // MOE_TILE
    base = jnp.broadcast_to((ends - padded).astype(F32)[:, None], (N_EXPERTS, LANES))
    tile_start = jnp.arange(n_tiles, dtype=jnp.int32) * MOE_TILE
    tile_expert = jnp.minimum(jnp.sum((ends[None, :] <= tile_start[:, None]).astype(jnp.int32), axis=1),
                              N_EXPERTS - 1)
    n_used = (ends[-1:] // MOE_TILE).astype(jnp.int32)
    pos = _positions(sel, eidx, base)
    pos3 = pos.reshape(TOP_K, n // SC_CHUNK, SC_CHUNK).transpose(1, 0, 2)
    xs = _sc_scatter(hpack, pos3, n_rows)
    ys = _experts(xs, tile_expert, n_used, mp, layer)
    yg = _sc_gather(ys, pos.reshape(-1))
    return [_combine(rows, hffn, yg, ew, mp, x1, mod4, layer, row0, n_out) for row0, n_out in out_ranges]


def _glu(x, wg, wu):
    hg = jnp.dot(x, wg, preferred_element_type=F32)
    hu = jnp.dot(x, wu, preferred_element_type=F32)
    return _silu(hg) * hu


def kernel(x_prompt, x_sample, c, c_ctx, cache_attn_k, cache_attn_v, state_rwkv_fwd, state_rwkv_bwd,
           state_hgrn_fwd, state_hgrn_bwd, norm1_g, norm2_g, mod_w, mod_b, ab_w_in, ab_w_out, attn_q_norm,
           attn_k_norm, attn_sink, rwkv_mu, rwkv_w0, rwkv_w2, rwkv_a0, rwkv_a2, rwkv_g2, rwkv_k_k, rwkv_k_a,
           rwkv_r_k, rwkv_ln_w, rwkv_ln_b, hgrn_w_in, hgrn_w_out, hgrn_lower_bounds, hgrn_norm_g, moe_router,
           moe_bias, moe_w_gate, moe_w_up, moe_w_down, moe_shared_gate, moe_shared_up, moe_shared_down):
    n_cseq, cseq, _ = x_prompt.shape
    n_lseq, lseq, _ = x_sample.shape
    depth = mod_w.shape[0]
    assert depth == 2 and n_lseq + 1 <= SUBLANES
    assert cseq == PREP_TM and lseq % TM == 0 and lseq % HG_TB == 0 and cseq % HG_TB == 0
    assert (n_cseq * cseq) % TM == 0
    assert n_cseq % RW_BB == 0 and n_lseq % RW_BB == 0 and (n_cseq * cseq) % (lseq * RW_BB) == 0
    rows = _Rows(n_cseq * cseq, n_lseq * lseq, lseq)
    assert rows.n % MOE_TILE == 0 and lseq % MOE_TILE == 0 and rows.n_ctx % MOE_TILE == 0
    kv_w = KV_A * HD_A

    xs = (x_prompt.reshape(rows.n_ctx, D_MODEL), x_sample.reshape(rows.n_lat, D_MODEL))
    cvecs = jnp.concatenate([c_ctx[None, :], c, jnp.zeros((SUBLANES - 1 - n_lseq, D_MODEL), F32)], axis=0)
    mod4 = _modulation(cvecs, mod_w, mod_b).reshape(depth, SUBLANES, 1, 6 * D_MODEL)

    ones_q = _block_ones(W_A, HD_A)
    ones_k = _block_ones(kv_w, HD_A)
    ones_b = _block_ones(W_B, HD_B)
    ones_pair = _block_ones(LANES, HD_B)[:LANES]
    ones_pair = jnp.kron(jnp.eye(2, dtype=BF16), ones_pair)
    cos_t, sin_t = _rope_tables(lseq)

    def moe(l, hffn, hpack, x1, out_ranges):
        mp = {'wg': moe_w_gate, 'wu': moe_w_up, 'wd': moe_w_down,
              'sg': _bf(moe_shared_gate[l]), 'su': _bf(moe_shared_up[l]), 'sd': _bf(moe_shared_down[l])}
        return _moe(rows, hffn, hpack, x1, moe_router[l], moe_bias[l], mp, mod4, l, out_ranges)

    assert W_A == W_B
    all_rows = jnp.zeros((rows.n, W_B), F32)

    pr = {'mu': rwkv_mu[0], 'w0': rwkv_w0[0], 'w2': rwkv_w2[0], 'a0': rwkv_a0[0], 'a2': rwkv_a2[0],
          'g2': rwkv_g2[0], 'k_k': rwkv_k_k[0], 'k_a': rwkv_k_a[0], 'r_k': rwkv_r_k[0].reshape(-1),
          'ln_w': rwkv_ln_w[0], 'ln_b': rwkv_ln_b[0]}
    p_att, p_rw = _inproj(rows, xs, norm1_g[0], mod4, 0, _bf(ab_w_in[0]), (ATT_IN, RWKV_IN), TM)
    qg_t = jnp.tile(attn_q_norm[0], H_A).reshape(1, W_A)
    kg_t = jnp.tile(attn_k_norm[0], KV_A).reshape(1, kv_w)
    o_att, new_k, new_v = _ctx_attention(p_att, n_cseq, cseq, qg_t, kg_t, attn_sink[0], ones_q, ones_k, all_rows)
    past = cache_attn_k.shape[2]
    o_att = _lat_attention(p_att, rows.n_ctx // lseq, n_lseq, lseq, qg_t, kg_t, attn_sink[0], ones_q, ones_k,
                           cos_t, sin_t, cache_attn_k[:, 0].reshape(n_lseq, past, kv_w),
                           cache_attn_v[:, 0].reshape(n_lseq, past, kv_w), o_att)

    pp = _rwkv_prep(rows, p_rw, pr, ones_b)
    zero_st = jnp.zeros((n_cseq, H_B // 2, HD_B, LANES), F32)
    o_f, o_b, sf_c, sb_c = _rwkv_scan(pp, 0, n_cseq, cseq, zero_st, zero_st, ones_pair, all_rows, all_rows)
    o_f, o_b, _, _ = _rwkv_scan(pp, rows.n_ctx, n_lseq, lseq, _state_to_pairs(state_rwkv_fwd[:, 0]),
                                _state_to_pairs(state_rwkv_bwd[:, 0]), ones_pair, o_f, o_b)
    x1, hffn, hpack = _outproj0(rows, xs, o_att, o_f, o_b, pp, pr, ones_b, _bf(ab_w_out[0]), norm2_g[0], mod4, 0)
    (x,) = moe(0, hffn, hpack, x1, [(0, rows.n)])

    (p1,) = _inproj(rows, (x, x), norm1_g[1], mod4, 1, _bf(hgrn_w_in[0]), (IN_C,), IN1_TM)
    zero_h = jnp.zeros((n_cseq, H_C, DK_C, DV_C), F32)
    all_rows_c = jnp.zeros((rows.n, D_C), F32)
    h_f, h_b, hsf_c, hsb_c = _hgrn_scan(p1, 0, n_cseq, cseq, hgrn_lower_bounds, zero_h, zero_h,
                                        all_rows_c, all_rows_c)
    h_f, h_b, _, _ = _hgrn_scan(p1, rows.n_ctx, n_lseq, lseq, hgrn_lower_bounds,
                                state_hgrn_fwd[:, 0], state_hgrn_bwd[:, 0], h_f, h_b)
    x1, hffn, hpack = _outproj1(rows, (x, x), h_f, h_b, p1, hgrn_norm_g[0], _bf(hgrn_w_out[0]), norm2_g[1],
                                mod4, 1)
    y_c, y_l = moe(1, hffn, hpack, x1, [(0, rows.n_ctx), (rows.n_ctx, rows.n_lat)])

    y_prompt = y_c.reshape(n_cseq, cseq, D_MODEL)
    y_sample = y_l.reshape(n_lseq, lseq, D_MODEL)
    return (y_prompt, y_sample,
            new_k.reshape(n_cseq, 1, cseq, KV_A, HD_A), new_v.reshape(n_cseq, 1, cseq, KV_A, HD_A),
            _pairs_to_state(sf_c)[:, None], _pairs_to_state(sb_c)[:, None],
            hsf_c[:, None], hsb_c[:, None])
```

```python
import functools

import numpy as np
import jax
import jax.numpy as jnp
from jax import lax
from jax.experimental import pallas as pl
from jax.experimental.pallas import tpu as pltpu
from jax.experimental.pallas import tpu_sc as plsc

F32 = jnp.float32
BF16 = jnp.bfloat16

D_MODEL = 1024
GRID_W = 64
H_A = 8
KV_A = 2
G_A = H_A // KV_A
HD_A = 64
W_A = H_A * HD_A
WINDOW = 128
QBLK = 128
ROPE_BASE = 10000.0
ROPE_PAIR = HD_A // 4
ATTN_SCALE = HD_A ** -0.5
NEG_INF = -1e30
H_B = 8
HD_B = 64
W_B = H_B * HD_B
LORA_W = 64
LORA_A = 64
LORA_G = 128
GN_EPS = 64e-5
ATT_IN = W_A + 2 * KV_A * HD_A
RWKV_IN = 3 * W_B + LORA_W + LORA_A + LORA_G
IN_AB = ATT_IN + RWKV_IN
H_C = 8
DK_C = 128
DV_C = 128
D_C = H_C * DV_C
CHUNK = 64
IN_C = 5 * D_C
N_EXPERTS = 64
TOP_K = 8
N_GROUPS = 8
TOPK_GROUPS = 4
D_EXPERT = 256
ROUTED_SCALE = 2.5
EPS = 1e-6

LANES = 128
SUBLANES = 8
VMEM_LIMIT = 52 * 1024 * 1024

TM = 512
PREP_TM = 256
IN1_TM = 256
RW_TB = 128
RW_BB = 4
RW_GROUP_BB = 4
HG_TB = 256
MOE_TILE = 512
POS_TB = 512
SC_CORES = 2
SC_SUBCORES = 16
SC_CHUNK = 64


def _cparams(n_axes):
    return pltpu.CompilerParams(dimension_semantics=("arbitrary",) * n_axes,
                                vmem_limit_bytes=VMEM_LIMIT)


def _bf(x):
    return x.astype(BF16)


def _split2(x):
    hi = lax.bitcast_convert_type(
        lax.bitcast_convert_type(x, jnp.uint32) & jnp.uint32(0xFFFF0000), F32)
    return hi, x - hi


def _seg_sum(x, ones2):
    hi, lo = _split2(x)
    return jnp.dot(jnp.concatenate([_bf(hi), _bf(lo)], axis=1), ones2,
                   preferred_element_type=F32)


def _dot_nt(a, b):
    return lax.dot_general(a, b, (((1,), (1,)), ((), ())), preferred_element_type=F32)


def _dot_tn(a, b):
    return lax.dot_general(a, b, (((0,), (0,)), ((), ())), preferred_element_type=F32)


def _sigmoid(x):
    return 1.0 / (1.0 + jnp.exp(-x))


def _silu(x):
    return x * _sigmoid(x)


def _chunks(seq, n):
    seq = list(seq)
    return [seq[i:i + n] for i in range(0, len(seq), n)]


def _block_ones(width, seg):
    idx = np.arange(width) // seg
    bd = (idx[:, None] == idx[None, :]).astype(np.float32)
    return jnp.asarray(np.concatenate([bd, bd], axis=0), dtype=BF16)


def _mod_kernel(c_ref, w_ref, b_ref, o_ref):
    s = _silu(c_ref[...])
    o_ref[0] = jnp.dot(_bf(s), _bf(w_ref[0]), preferred_element_type=F32) + b_ref[0]


def _modulation(cvecs, mod_w, mod_b):
    depth = mod_w.shape[0]
    n_col = 6 * D_MODEL // D_MODEL
    return pl.pallas_call(
        _mod_kernel,
        grid=(depth, n_col),
        in_specs=[pl.BlockSpec((SUBLANES, D_MODEL), lambda l, j: (0, 0)),
                  pl.BlockSpec((1, D_MODEL, D_MODEL), lambda l, j: (l, 0, j)),
                  pl.BlockSpec((1, 1, D_MODEL), lambda l, j: (l, 0, j))],
        out_specs=pl.BlockSpec((1, SUBLANES, D_MODEL), lambda l, j: (l, 0, j)),
        out_shape=jax.ShapeDtypeStruct((depth, SUBLANES, 6 * D_MODEL), F32),
        compiler_params=_cparams(2),
        name="modulation",
    )(cvecs, mod_w, mod_b.reshape(depth, 1, 6 * D_MODEL))


class _Rows:
    def __init__(self, n_ctx, n_lat, lat_seq):
        self.n_ctx, self.n_lat, self.lat_seq = n_ctx, n_lat, lat_seq
        self.n = n_ctx + n_lat

    def mod_row(self, i, tm):
        nctx_blk = self.n_ctx // tm
        per_seq = self.lat_seq // tm
        return jnp.where(i < nctx_blk, 0, 1 + (i - nctx_blk) // per_seq)

    def mod_spec(self, layer, chunk, tm, blk0=0):
        return pl.BlockSpec((None, None, 1, D_MODEL),
                            lambda i, *_: (layer, self.mod_row(i + blk0, tm), 0, chunk))


def _rms_mod(x, g, sc, sh):
    ms = jnp.mean(x * x, axis=-1, keepdims=True)
    return x * lax.rsqrt(ms + EPS) * g * (1.0 + sc) + sh


def _x_specs(rows, xs, tm=TM):
    xa, xb = xs
    nctx_blk = rows.n_ctx // tm
    lat0 = nctx_blk if xb.shape[0] == rows.n else 0
    return [pl.BlockSpec((tm, D_MODEL), lambda i: (jnp.minimum(i, nctx_blk - 1), 0)),
            pl.BlockSpec((tm, D_MODEL), lambda i: (jnp.maximum(i - nctx_blk, 0) + lat0, 0))]


def _pick_x(rows, xa_ref, xb_ref):
    return jnp.where(pl.program_id(0) < rows.n_ctx // xa_ref.shape[0], xa_ref[...], xb_ref[...])


def _inproj_kernel(rows, splits, xa_ref, xb_ref, g_ref, sh_ref, sc_ref, w_ref, *o_refs):
    h = _rms_mod(_pick_x(rows, xa_ref, xb_ref), g_ref[...], sc_ref[...], sh_ref[...])
    p = jnp.dot(_bf(h), w_ref[...], preferred_element_type=F32)
    lo = 0
    for o_ref, width in zip(o_refs, splits):
        o_ref[...] = p[:, lo:lo + width]
        lo += width


def _inproj(rows, xs, g, mod4, layer, w_bf, splits, tm):
    n_out = w_bf.shape[1]
    return pl.pallas_call(
        functools.partial(_inproj_kernel, rows, splits),
        grid=(rows.n // tm,),
        in_specs=_x_specs(rows, xs, tm) + [
            pl.BlockSpec((1, D_MODEL), lambda i: (0, 0)),
            rows.mod_spec(layer, 0, tm),
            rows.mod_spec(layer, 1, tm),
            pl.BlockSpec((D_MODEL, n_out), lambda i: (0, 0))],
        out_specs=[pl.BlockSpec((tm, wd), lambda i: (i, 0)) for wd in splits],
        out_shape=[jax.ShapeDtypeStruct((rows.n, wd), F32) for wd in splits],
        compiler_params=_cparams(1),
        name=f"inproj{layer}",
    )(*xs, g.reshape(1, D_MODEL), mod4, mod4, w_bf)


def _head_rms(x, gain_t, ones2):
    ms = _seg_sum(x * x, ones2) * (1.0 / HD_A)
    return x * lax.rsqrt(ms + EPS) * gain_t


def _sink_softmax_pv(parts, sink):
    m = jnp.maximum(functools.reduce(jnp.maximum, [jnp.max(s, axis=-1, keepdims=True) for s, _ in parts]), sink)
    den = jnp.exp(sink - m)
    acc = None
    for s, v in parts:
        p = jnp.exp(s - m)
        den = den + jnp.sum(p, axis=-1, keepdims=True)
        pv = jnp.dot(_bf(p), v, preferred_element_type=F32)
        acc = pv if acc is None else acc + pv
    return acc / den


def _ctx_attn_kernel(p_ref, qg_ref, kg_ref, sink_ref, ones_q_ref, ones_k_ref, prev_ref, o_ref, k_ref, v_ref):
    del prev_ref
    p = p_ref[...]
    q = _head_rms(p[:, :W_A], qg_ref[...], ones_q_ref[...]) * ATTN_SCALE
    k = _head_rms(p[:, W_A:W_A + KV_A * HD_A], kg_ref[...], ones_k_ref[...])
    v = p[:, W_A + KV_A * HD_A:ATT_IN]
    k_ref[0] = k
    v_ref[0] = v
    qb, kb, vb = _bf(q), _bf(k), _bf(v)
    seq = p.shape[0]
    grp = lax.broadcasted_iota(jnp.int32, (G_A * seq, 1), 0) // seq
    outs = []
    for j in range(KV_A):
        qs = jnp.concatenate([qb[:, (j * G_A + g) * HD_A:(j * G_A + g + 1) * HD_A] for g in range(G_A)], axis=0)
        sink = jnp.zeros((G_A * seq, 1), F32)
        for g in range(G_A):
            sink = jnp.where(grp == g, sink_ref[j * G_A + g], sink)
        s = _dot_nt(qs, kb[:, j * HD_A:(j + 1) * HD_A])
        o = _sink_softmax_pv([(s, vb[:, j * HD_A:(j + 1) * HD_A])], sink)
        outs.extend(o[g * seq:(g + 1) * seq] for g in range(G_A))
    o_ref[...] = jnp.concatenate(outs, axis=1)


def _ctx_attention(p_att, n_seq, seq, qg_t, kg_t, sink, ones_q, ones_k, prev):
    kv_w = KV_A * HD_A
    return pl.pallas_call(
        _ctx_attn_kernel,
        grid=(n_seq,),
        in_specs=[pl.BlockSpec((seq, ATT_IN), lambda b: (b, 0)),
                  pl.BlockSpec((1, W_A), lambda b: (0, 0)),
                  pl.BlockSpec((1, kv_w), lambda b: (0, 0)),
                  pl.BlockSpec(memory_space=pltpu.SMEM),
                  pl.BlockSpec(ones_q.shape, lambda b: (0, 0)),
                  pl.BlockSpec(ones_k.shape, lambda b: (0, 0)),
                  pl.BlockSpec(memory_space=pl.ANY)],
        out_specs=[pl.BlockSpec((seq, W_A), lambda b: (b, 0)),
                   pl.BlockSpec((1, seq, kv_w), lambda b: (b, 0, 0)),
                   pl.BlockSpec((1, seq, kv_w), lambda b: (b, 0, 0))],
        input_output_aliases={6: 0},
        out_shape=[jax.ShapeDtypeStruct(prev.shape, F32),
                   jax.ShapeDtypeStruct((n_seq, seq, kv_w), F32),
                   jax.ShapeDtypeStruct((n_seq, seq, kv_w), F32)],
        compiler_params=_cparams(1),
        name="ctx_attention",
    )(p_att, qg_t, kg_t, sink, ones_q, ones_k, prev)


def _rope(x, cos_t, sin_t):
    lane = lax.broadcasted_iota(jnp.int32, cos_t.shape, 1)
    low = (lane % (2 * ROPE_PAIR)) < ROPE_PAIR
    outs = []
    for s in range(x.shape[1] // LANES):
        xs = x[:, s * LANES:(s + 1) * LANES]
        partner = jnp.where(low, pltpu.roll(xs, LANES - ROPE_PAIR, 1), pltpu.roll(xs, ROPE_PAIR, 1))
        outs.append(xs * cos_t + partner * sin_t)
    return outs[0] if len(outs) == 1 else jnp.concatenate(outs, axis=1)


def _lat_attn_kernel(seq, p_ref, qg_ref, kg_ref, sink_ref, ones_q_ref, ones_k_ref, cos_ref, sin_ref,
                     kc_ref, vc_ref, prev_ref, o_ref, q_scr, k_scr, v_scr):
    del prev_ref
    kv_w = KV_A * HD_A
    p = p_ref[...]
    q = _head_rms(p[:, :W_A], qg_ref[...], ones_q_ref[...])
    k = _head_rms(p[:, W_A:W_A + kv_w], kg_ref[...], ones_k_ref[...])
    qr = _bf(_rope(q, cos_ref[...], sin_ref[...]) * ATTN_SCALE)
    kr = _bf(_rope(k, cos_ref[...], sin_ref[...]))
    vb = _bf(p[:, W_A + kv_w:ATT_IN])
    for h in range(H_A):
        q_scr[h] = qr[:, h * HD_A:(h + 1) * HD_A]
    for j in range(KV_A):
        k_scr[j] = kr[:, j * HD_A:(j + 1) * HD_A]
        v_scr[j] = vb[:, j * HD_A:(j + 1) * HD_A]
    kc = _bf(kc_ref[0])
    vc = _bf(vc_ref[0])
    n_local = 3 * QBLK
    grp = lax.broadcasted_iota(jnp.int32, (G_A * QBLK, 1), 0) // QBLK

    def block(i, carry):
        q0 = pl.multiple_of(i * QBLK, QBLK)
        start = pl.multiple_of(jnp.clip((i - 1) * QBLK, 0, seq - n_local), QBLK)
        ipos = q0 + lax.broadcasted_iota(jnp.int32, (G_A * QBLK, n_local), 0) % QBLK
        jpos = start + lax.broadcasted_iota(jnp.int32, (G_A * QBLK, n_local), 1)
        band = jnp.abs(jpos - ipos) <= WINDOW
        outs = []
        for j in range(KV_A):
            qs = jnp.concatenate([q_scr[j * G_A + g, pl.ds(q0, QBLK), :] for g in range(G_A)], axis=0)
            kl = k_scr[j, pl.ds(start, n_local), :]
            vl = v_scr[j, pl.ds(start, n_local), :]
            sink = jnp.zeros((G_A * QBLK, 1), F32)
            for g in range(G_A):
                sink = jnp.where(grp == g, sink_ref[j * G_A + g], sink)
            s_loc = jnp.where(band, _dot_nt(qs, kl), NEG_INF)
            s_ctx = _dot_nt(qs, kc[:, j * HD_A:(j + 1) * HD_A])
            o = _sink_softmax_pv([(s_loc, vl), (s_ctx, vc[:, j * HD_A:(j + 1) * HD_A])], sink)
            outs.extend(o[g * QBLK:(g + 1) * QBLK] for g in range(G_A))
        o_ref[pl.ds(q0, QBLK), :] = jnp.concatenate(outs, axis=1)
        return carry

    lax.fori_loop(0, seq // QBLK, block, 0)


def _lat_attention(p_att, row_blk0, n_seq, seq, qg_t, kg_t, sink, ones_q, ones_k, cos_t, sin_t, kc, vc, prev):
    kv_w = KV_A * HD_A
    past = kc.shape[1]
    return pl.pallas_call(
        functools.partial(_lat_attn_kernel, seq),
        grid=(n_seq,),
        in_specs=[pl.BlockSpec((seq, ATT_IN), lambda b: (row_blk0 + b, 0)),
                  pl.BlockSpec((1, W_A), lambda b: (0, 0)),
                  pl.BlockSpec((1, kv_w), lambda b: (0, 0)),
                  pl.BlockSpec(memory_space=pltpu.SMEM),
                  pl.BlockSpec(ones_q.shape, lambda b: (0, 0)),
                  pl.BlockSpec(ones_k.shape, lambda b: (0, 0)),
                  pl.BlockSpec((seq, LANES), lambda b: (0, 0)),
                  pl.BlockSpec((seq, LANES), lambda b: (0, 0)),
                  pl.BlockSpec((1, past, kv_w), lambda b: (b, 0, 0)),
                  pl.BlockSpec((1, past, kv_w), lambda b: (b, 0, 0)),
                  pl.BlockSpec(memory_space=pl.ANY)],
        out_specs=pl.BlockSpec((seq, W_A), lambda b: (row_blk0 + b, 0)),
        out_shape=jax.ShapeDtypeStruct(prev.shape, F32),
        input_output_aliases={10: 0},
        scratch_shapes=[pltpu.VMEM((H_A, seq, HD_A), BF16), pltpu.VMEM((KV_A, seq, HD_A), BF16),
                        pltpu.VMEM((KV_A, seq, HD_A), BF16)],
        compiler_params=_cparams(1),
        name="lat_attention",
    )(p_att, qg_t, kg_t, sink, ones_q, ones_k, cos_t, sin_t, kc, vc, prev)


def _rope_tables(seq):
    pos = np.arange(seq)
    row = (pos // GRID_W).astype(np.float32)
    col = (pos % GRID_W).astype(np.float32)
    d_axis = HD_A // 2
    inv = (ROPE_BASE ** (-np.arange(0, d_axis, 2, dtype=np.float32) / d_axis)).astype(np.float32)
    cos_h = np.zeros((seq, HD_A), np.float32)
    sin_h = np.zeros((seq, HD_A), np.float32)
    for seg, p_ in enumerate((row, col)):
        ang = (p_[:, None] * inv[None, :]).astype(np.float32)
        c, s = np.cos(ang), np.sin(ang)
        base = seg * d_axis
        cos_h[:, base:base + d_axis // 2] = c
        cos_h[:, base + d_axis // 2:base + d_axis] = c
        sin_h[:, base:base + d_axis // 2] = -s
        sin_h[:, base + d_axis // 2:base + d_axis] = s
    rep = LANES // HD_A
    return jnp.asarray(np.tile(cos_h, (1, rep))), jnp.asarray(np.tile(sin_h, (1, rep)))


def _rwkv_prep_kernel(rows, x_ref, prev_ref, next_ref, mu_ref, kk_ref, ka_ref, rk_ref, w0_ref, w2_ref,
                      a0_ref, a2_ref, g2_ref, ones_ref,
                      nkk_ref, r_ref, v_ref, g_ref, bonus_ref,
                      wf_ref, kaf_ref, kdf_ref, wb_ref, kab_ref, kdb_ref):
    i = pl.program_id(0)
    nctx_blk = rows.n_ctx // PREP_TM
    per_seq = rows.lat_seq // PREP_TM
    is_ctx = i < nctx_blk
    first = jnp.logical_or(is_ctx, (i - nctx_blk) % per_seq == 0)
    last = jnp.logical_or(is_ctx, (i - nctx_blk) % per_seq == per_seq - 1)
    x = x_ref[...]
    ridx = lax.broadcasted_iota(jnp.int32, x.shape, 0)
    prev_row = jnp.where(first, 0.0, prev_ref[SUBLANES - 1:SUBLANES, :])
    next_row = jnp.where(last, 0.0, next_ref[0:1, :])
    xm1 = jnp.where(ridx == 0, prev_row, pltpu.roll(x, 1, 0))
    xp1 = jnp.where(ridx == PREP_TM - 1, next_row, pltpu.roll(x, PREP_TM - 1, 0))
    pw = x + (0.5 * (xm1 + xp1) - x) * mu_ref[...]

    r = pw[:, 0:W_B]
    k = pw[:, W_B:2 * W_B]
    v = pw[:, 2 * W_B:3 * W_B]
    wd = pw[:, 3 * W_B:3 * W_B + LORA_W]
    ad = pw[:, 3 * W_B + LORA_W:3 * W_B + LORA_W + LORA_A]
    gd = pw[:, 3 * W_B + LORA_W + LORA_A:]
    ones2 = ones_ref[...]

    kk = k * kk_ref[...]
    kk = kk / jnp.maximum(jnp.sqrt(_seg_sum(kk * kk, ones2)), 1e-12)
    nkk_ref[...] = -kk
    r_ref[...] = r
    v_ref[...] = v
    g_ref[...] = jnp.dot(_bf(_sigmoid(gd)), g2_ref[...], preferred_element_type=F32)
    tw = _bf(jnp.tanh(wd))
    adb = _bf(ad)
    bonus = jnp.zeros_like(r)
    for d, (w_o, ka_o, kd_o) in enumerate(((wf_ref, kaf_ref, kdf_ref), (wb_ref, kab_ref, kdb_ref))):
        z = -(w0_ref[d:d + 1, :] + jnp.dot(tw, w2_ref[d], preferred_element_type=F32))
        softplus = jnp.maximum(z, 0.0) + jnp.log(1.0 + jnp.exp(-jnp.abs(z)))
        w_o[...] = jnp.exp(-jnp.exp(-softplus - 0.5))
        a = _sigmoid(a0_ref[d:d + 1, :] + jnp.dot(adb, a2_ref[d], preferred_element_type=F32))
        kd = k * (1.0 + (a - 1.0) * ka_ref[...])
        ka_o[...] = kk * a
        kd_o[...] = kd
        bonus = bonus + _seg_sum(r * kd * rk_ref[...], ones2) * v
    bonus_ref[...] = bonus


def _rwkv_prep(rows, p_rw, pr, ones_b):
    n = rows.n
    n_halo = n // SUBLANES
    blk_halo = PREP_TM // SUBLANES
    row = lambda a: a.reshape(1, -1)
    full = lambda a: pl.BlockSpec(a.shape, lambda i: (0,) * a.ndim)
    consts = [row(pr['mu']), row(pr['k_k']), row(pr['k_a']), row(pr['r_k']), pr['w0'], _bf(pr['w2']),
              pr['a0'], _bf(pr['a2']), _bf(pr['g2']), ones_b]
    outs = pl.pallas_call(
        functools.partial(_rwkv_prep_kernel, rows),
        grid=(n // PREP_TM,),
        in_specs=[pl.BlockSpec((PREP_TM, RWKV_IN), lambda i: (i, 0)),
                  pl.BlockSpec((SUBLANES, RWKV_IN), lambda i: (jnp.maximum(i * blk_halo - 1, 0), 0)),
                  pl.BlockSpec((SUBLANES, RWKV_IN), lambda i: (jnp.minimum((i + 1) * blk_halo, n_halo - 1), 0))]
                 + [full(a) for a in consts],
        out_specs=[pl.BlockSpec((PREP_TM, W_B), lambda i: (i, 0))] * 11,
        out_shape=[jax.ShapeDtypeStruct((n, W_B), F32)] * 11,
        compiler_params=_cparams(1),
        name="rwkv_prep",
    )(p_rw, p_rw, p_rw, *consts)
    names = ('nkk', 'r', 'v', 'g', 'bonus', 'w_f', 'ka_f', 'kd_f', 'w_b', 'ka_b', 'kd_b')
    return dict(zip(names, outs))


def _rwkv_scan_kernel(n_tb, nkkf_ref, rf_ref, vf_ref, wf_ref, kaf_ref, kdf_ref,
                      nkkb_ref, rb_ref, vb_ref, wb_ref, kab_ref, kdb_ref,
                      s0f_ref, s0b_ref, ones_ref, prevf_ref, prevb_ref,
                      of_ref, ob_ref, sff_ref, sfb_ref, sf_scr, sb_scr, vt_scr):
    del prevf_ref, prevb_ref
    s_scr = (sf_scr, sb_scr)
    tb = pl.program_id(1)
    n_pair = H_B // 2
    half = RW_TB // 2
    dirs = ((nkkf_ref, rf_ref, vf_ref, wf_ref, kaf_ref, kdf_ref, of_ref, False),
            (nkkb_ref, rb_ref, vb_ref, wb_ref, kab_ref, kdb_ref, ob_ref, True))

    @pl.when(tb == 0)
    def _():
        sf_scr[...] = s0f_ref[...]
        sb_scr[...] = s0b_ref[...]

    lane = lax.broadcasted_iota(jnp.int32, (HD_B, LANES), 1)
    for d, refs in enumerate(dirs):
        v_ref = refs[2]
        for bb in range(RW_BB):
            for p in range(n_pair):
                vt = v_ref[bb, :, p * LANES:(p + 1) * LANES].T
                top, bot = vt[:HD_B], vt[HD_B:]
                for s in range(2):
                    if s == 0:
                        t2 = jnp.where(lane < HD_B, top, pltpu.roll(bot, HD_B, 1))
                    else:
                        t2 = jnp.where(lane < HD_B, pltpu.roll(top, HD_B, 1), bot)
                    vt_scr[d, bb, p, s] = t2

    ones2 = ones_ref[...]
    row8 = lax.broadcasted_iota(jnp.int32, (SUBLANES, LANES), 0)
    lane8 = lax.broadcasted_iota(jnp.int32, (SUBLANES, LANES), 1)
    sel_r = jnp.logical_or(jnp.logical_and(row8 % 2 == 0, lane8 < HD_B),
                           jnp.logical_and(row8 % 2 == 1, lane8 >= HD_B))

    def row_of(rev, tt):
        return RW_TB - 1 - tt if rev else tt

    def emit_output(d, bb, tau):
        r_ref, o_ref = dirs[d][1], dirs[d][6]
        r = r_ref[bb, pl.ds(tau, 1), :]
        r8 = jnp.zeros((SUBLANES, LANES), F32)
        for p in range(n_pair):
            rp = jnp.broadcast_to(r[:, p * LANES:(p + 1) * LANES], (SUBLANES, LANES))
            r8 = jnp.where(jnp.logical_and(sel_r, row8 // 2 == p), rp, r8)
        s_all = jnp.concatenate([_bf(s_scr[d][bb, p]) for p in range(n_pair)], axis=0)
        o8 = _dot_nt(_bf(r8), s_all)
        o_parts = []
        for p in range(n_pair):
            for h in range(2):
                o_parts.append(o8[2 * p + h:2 * p + h + 1, p * HD_B:(p + 1) * HD_B])
        o_ref[bb, pl.ds(tau, 1), :] = jnp.concatenate(o_parts, axis=1)

    groups = [(d, bbs) for d in range(2) for bbs in _chunks(range(RW_BB), RW_GROUP_BB)]

    def reduce_phase(grp, tt):
        d, bbs = grp
        rev = dirs[d][7]
        tau = row_of(rev, tt)
        sub = tau // half
        lt = tau % half
        mask = jnp.logical_or(lane == lt, lane == lt + HD_B)
        lhs = []
        for bb in bbs:
            emit_output(d, bb, row_of(rev, jnp.maximum(tt - 1, 0)))
            nkk = dirs[d][0][bb, pl.ds(tau, 1), :]
            for p in range(n_pair):
                prod = s_scr[d][bb, p] * nkk[:, p * LANES:(p + 1) * LANES]
                lhs.append(jnp.concatenate([_bf(prod), _bf(jnp.where(mask, vt_scr[d, bb, p, sub], 0.0))],
                                           axis=1))
        return jnp.dot(jnp.concatenate(lhs, axis=0), ones2, preferred_element_type=F32)

    def update_phase(grp, tt, red):
        d, bbs = grp
        _, _, _, w_ref, ka_ref, kd_ref, _, rev = dirs[d]
        tau = row_of(rev, tt)
        for k, bb in enumerate(bbs):
            w = w_ref[bb, pl.ds(tau, 1), :]
            ka = ka_ref[bb, pl.ds(tau, 1), :]
            kd = kd_ref[bb, pl.ds(tau, 1), :]
            for p in range(n_pair):
                sl = slice(p * LANES, (p + 1) * LANES)
                r0 = (k * n_pair + p) * HD_B
                sa = red[r0:r0 + HD_B, :LANES]
                vcol = red[r0:r0 + HD_B, LANES:]
                s_scr[d][bb, p] = s_scr[d][bb, p] * w[:, sl] + sa * ka[:, sl] + vcol * kd[:, sl]

    def step(tt, carry):
        reds = [reduce_phase(g, tt) for g in groups]
        for g, red in zip(groups, reds):
            update_phase(g, tt, red)
        return carry

    lax.fori_loop(0, RW_TB, step, 0)
    for d in range(2):
        for bb in range(RW_BB):
            emit_output(d, bb, row_of(dirs[d][7], RW_TB - 1))

    @pl.when(tb == n_tb - 1)
    def _():
        sff_ref[...] = sf_scr[...]
        sfb_ref[...] = sb_scr[...]


def _rwkv_scan(pp, row0, n_seq, seq, s0_f, s0_b, ones_pair, prev_f, prev_b):
    n_tb = seq // RW_TB
    n_pair = H_B // 2
    blk0 = row0 // seq
    view = lambda a: a.reshape(a.shape[0] // seq, seq, W_B)
    fwd = pl.BlockSpec((RW_BB, RW_TB, W_B), lambda b, t: (blk0 // RW_BB + b, t, 0))
    bwd = pl.BlockSpec((RW_BB, RW_TB, W_B), lambda b, t: (blk0 // RW_BB + b, n_tb - 1 - t, 0))
    st = pl.BlockSpec((RW_BB, n_pair, HD_B, LANES), lambda b, t: (b, 0, 0, 0))
    ins_f = [view(pp[k]) for k in ('nkk', 'r', 'v', 'w_f', 'ka_f', 'kd_f')]
    ins_b = [view(pp[k]) for k in ('nkk', 'r', 'v', 'w_b', 'ka_b', 'kd_b')]
    st_shape = jax.ShapeDtypeStruct((n_seq, n_pair, HD_B, LANES), F32)
    o_shape = jax.ShapeDtypeStruct(view(prev_f).shape, F32)
    any_spec = pl.BlockSpec(memory_space=pl.ANY)
    o_f, o_b, sf, sb = pl.pallas_call(
        functools.partial(_rwkv_scan_kernel, n_tb),
        grid=(n_seq // RW_BB, n_tb),
        in_specs=[fwd] * 6 + [bwd] * 6 + [st, st, pl.BlockSpec(ones_pair.shape, lambda b, t: (0, 0)),
                                           any_spec, any_spec],
        out_specs=[fwd, bwd, st, st],
        out_shape=[o_shape, o_shape, st_shape, st_shape],
        input_output_aliases={15: 0, 16: 1},
        scratch_shapes=[pltpu.VMEM((RW_BB, n_pair, HD_B, LANES), F32),
                        pltpu.VMEM((RW_BB, n_pair, HD_B, LANES), F32),
                        pltpu.VMEM((2, RW_BB, n_pair, 2, HD_B, LANES), F32)],
        compiler_params=_cparams(2),
        name="rwkv_scan",
    )(*ins_f, *ins_b, s0_f, s0_b, ones_pair, view(prev_f), view(prev_b))
    return o_f.reshape(prev_f.shape), o_b.reshape(prev_b.shape), sf, sb


def _state_to_pairs(s):
    b = s.shape[0]
    return s.reshape(b, H_B // 2, 2, HD_B, HD_B).transpose(0, 1, 3, 2, 4).reshape(b, H_B // 2, HD_B, 2 * HD_B)


def _pairs_to_state(s):
    b = s.shape[0]
    return s.reshape(b, H_B // 2, HD_B, 2, HD_B).transpose(0, 1, 3, 2, 4).reshape(b, H_B, HD_B, HD_B)


def _tail(x, y, g1, n2g, sc2, sh2, x1_ref, h_ref, hp_ref):
    x1 = x + g1 * y
    x1_ref[...] = x1
    h = _rms_mod(x1, n2g, sc2, sh2)
    h_ref[...] = _bf(h)
    hp_ref[...] = _pack_pairs(h)


def _outproj0_kernel(rows, xa_ref, xb_ref, oa_ref, of_ref, ob_ref, bonus_ref, g_ref, lnw_ref, lnb_ref, ones_ref,
                     w_ref, g1_ref, n2g_ref, sc2_ref, sh2_ref, x1_ref, h_ref, hp_ref):
    o_sum = of_ref[...] + ob_ref[...]
    ones2 = ones_ref[...]
    mean = _seg_sum(o_sum, ones2) * (1.0 / HD_B)
    cen = o_sum - mean
    var = _seg_sum(cen * cen, ones2) * (1.0 / HD_B)
    gn = cen * lax.rsqrt(var + GN_EPS) * lnw_ref[...] + lnb_ref[...]
    o_rw = (gn + bonus_ref[...]) * g_ref[...]
    mix = jnp.concatenate([_bf(oa_ref[...]), _bf(o_rw)], axis=1)
    y = jnp.dot(mix, w_ref[...], preferred_element_type=F32)
    _tail(_pick_x(rows, xa_ref, xb_ref), y, g1_ref[...], n2g_ref[...], sc2_ref[...], sh2_ref[...], x1_ref, h_ref, hp_ref)


def _outproj0(rows, xs, o_att, o_f, o_b, pp, pr, ones_b, w_out_bf, n2g, mod4, layer):
    n = rows.n
    tok = lambda w: pl.BlockSpec((TM, w), lambda i: (i, 0))
    const = lambda a: pl.BlockSpec(a.shape, lambda i: (0,) * a.ndim)
    lnw, lnb, n2 = pr['ln_w'].reshape(1, -1), pr['ln_b'].reshape(1, -1), n2g.reshape(1, -1)
    return pl.pallas_call(
        functools.partial(_outproj0_kernel, rows),
        grid=(n // TM,),
        in_specs=_x_specs(rows, xs) + [tok(W_A), tok(W_B), tok(W_B), tok(W_B), tok(W_B),
                  const(lnw), const(lnb), const(ones_b), const(w_out_bf),
                  rows.mod_spec(layer, 2, TM), const(n2), rows.mod_spec(layer, 4, TM), rows.mod_spec(layer, 3, TM)],
        out_specs=[tok(D_MODEL), tok(D_MODEL), tok(D_MODEL // 2)],
        out_shape=[jax.ShapeDtypeStruct((n, D_MODEL), F32), jax.ShapeDtypeStruct((n, D_MODEL), BF16),
                   jax.ShapeDtypeStruct((n, D_MODEL // 2), jnp.uint32)],
        compiler_params=_cparams(1),
        name="outproj0",
    )(*xs, o_att, o_f, o_b, pp['bonus'], pp['g'], lnw, lnb, ones_b, w_out_bf, mod4, n2, mod4, mod4)


def _outproj1_kernel(rows, xa_ref, xb_ref, of_ref, ob_ref, gate_ref, ng_ref, w_ref, g1_ref, n2g_ref, sc2_ref, sh2_ref,
                     x1_ref, h_ref, hp_ref):
    o_sum = of_ref[...] + ob_ref[...]
    parts = []
    for h in range(H_C):
        oh = o_sum[:, h * DV_C:(h + 1) * DV_C]
        parts.append(oh * lax.rsqrt(jnp.mean(oh * oh, axis=-1, keepdims=True) + EPS))
    o = jnp.concatenate(parts, axis=1) * ng_ref[...] * _silu(gate_ref[...])
    y = jnp.dot(_bf(o), w_ref[...], preferred_element_type=F32)
    _tail(_pick_x(rows, xa_ref, xb_ref), y, g1_ref[...], n2g_ref[...], sc2_ref[...], sh2_ref[...], x1_ref, h_ref, hp_ref)


def _outproj1(rows, xs, o_f, o_b, p1, norm_g, w_out_bf, n2g, mod4, layer):
    n = rows.n
    tok = lambda w: pl.BlockSpec((TM, w), lambda i: (i, 0))
    const = lambda a: pl.BlockSpec(a.shape, lambda i: (0,) * a.ndim)
    ng, n2 = norm_g.reshape(1, -1), n2g.reshape(1, -1)
    return pl.pallas_call(
        functools.partial(_outproj1_kernel, rows),
        grid=(n // TM,),
        in_specs=_x_specs(rows, xs) + [tok(D_C), tok(D_C), pl.BlockSpec((TM, D_C), lambda i: (i, 4)),
                  const(ng), const(w_out_bf),
                  rows.mod_spec(layer, 2, TM), const(n2), rows.mod_spec(layer, 4, TM), rows.mod_spec(layer, 3, TM)],
        out_specs=[tok(D_MODEL), tok(D_MODEL), tok(D_MODEL // 2)],
        out_shape=[jax.ShapeDtypeStruct((n, D_MODEL), F32), jax.ShapeDtypeStruct((n, D_MODEL), BF16),
                   jax.ShapeDtypeStruct((n, D_MODEL // 2), jnp.uint32)],
        compiler_params=_cparams(1),
        name="outproj1",
    )(*xs, o_f, o_b, p1, ng, w_out_bf, mod4, n2, mod4, mod4)


def _hgrn_kernel(n_tb, qf_ref, ff_ref, if_ref, qb_ref, fb_ref, ib_ref, lbp_ref, s0f_ref, s0b_ref,
                 trif_ref, trib_ref, prevf_ref, prevb_ref, of_ref, ob_ref, sff_ref, sfb_ref, s_scr):
    del prevf_ref, prevb_ref
    tb = pl.program_id(1)

    @pl.when(tb == 0)
    def _():
        for h in range(H_C):
            s_scr[0, h] = s0f_ref[0, h].T
            s_scr[1, h] = s0b_ref[0, h].T

    lbp = lbp_ref[...]
    e = jnp.exp(lbp - jnp.max(lbp, axis=0, keepdims=True))
    sm = e / jnp.sum(e, axis=0, keepdims=True)
    lb = (sm[0:1] + sm[1:2]) - sm[0:1]

    n_chunk = HG_TB // CHUNK
    ti = lax.broadcasted_iota(jnp.int32, (HG_TB, HG_TB), 0)
    si = lax.broadcasted_iota(jnp.int32, (HG_TB, HG_TB), 1)
    same = (ti // CHUNK) == (si // CHUNK)
    dirs = ((qf_ref, ff_ref, if_ref, of_ref, trif_ref, jnp.logical_and(same, ti >= si), CHUNK - 1, False),
            (qb_ref, fb_ref, ib_ref, ob_ref, trib_ref, jnp.logical_and(same, ti <= si), 0, True))

    staged = []
    for d, (q_ref, f_ref, i_ref, o_ref, tri_ref, causal, last_row, rev) in enumerate(dirs):
        q = _silu(q_ref[...])
        f = lb + (1.0 - lb) * _sigmoid(f_ref[...])
        k = 1.0 - f
        v = _bf(i_ref[...])
        g = jnp.log(f)
        g1 = _bf(g)
        g2 = _bf(g - g1.astype(F32))
        tri2 = tri_ref[...]
        b_parts, last_parts, dec = [], [], []
        for c in range(n_chunk):
            rc = slice(c * CHUNK, (c + 1) * CHUNK)
            bc = jnp.dot(tri2, jnp.concatenate([g1[rc], g2[rc]], axis=0), preferred_element_type=F32)
            b_parts.append(bc)
            last = bc[last_row:last_row + 1]
            last_parts.append(jnp.broadcast_to(last, bc.shape))
            dec.append(jnp.exp(last))
        b = jnp.concatenate(b_parts, axis=0)
        b_last = jnp.concatenate(last_parts, axis=0)
        staged.append((_bf(q * jnp.exp(b)), _bf(k * jnp.exp(-b)), _bf(k * jnp.exp(b_last - b)), v, dec))

    for h in range(H_C):
        sl = slice(h * DK_C, (h + 1) * DK_C)
        for d, (q_ref, f_ref, i_ref, o_ref, tri_ref, causal, last_row, rev) in enumerate(dirs):
            q_in, k_in, k_out, v, dec = staged[d]
            qh, vh = q_in[:, sl], v[:, sl]
            att = jnp.where(causal, _dot_nt(qh, k_in[:, sl]), 0.0)
            o_intra = jnp.dot(_bf(att), vh, preferred_element_type=F32)
            s_t = s_scr[d, h]
            for c in (range(n_chunk - 1, -1, -1) if rev else range(n_chunk)):
                rc = slice(c * CHUNK, (c + 1) * CHUNK)
                o_ref[rc, sl] = o_intra[rc] + _dot_nt(qh[rc], _bf(s_t))
                s_t = dec[c][:, sl] * s_t + _dot_tn(vh[rc], k_out[rc, sl])
            s_scr[d, h] = s_t

    @pl.when(tb == n_tb - 1)
    def _():
        for h in range(H_C):
            sff_ref[0, h] = s_scr[0, h].T
            sfb_ref[0, h] = s_scr[1, h].T


def _hgrn_scan(p1, row0, n_seq, seq, lb_params, s0_f, s0_b, prev_f, prev_b):
    n_tb = seq // HG_TB
    blk0 = row0 // HG_TB
    tri = np.tril(np.ones((CHUNK, CHUNK), np.float32))
    tri_f = jnp.asarray(np.concatenate([tri] * 2, axis=1), dtype=BF16)
    tri_b = jnp.asarray(np.concatenate([tri.T] * 2, axis=1), dtype=BF16)
    fwd = lambda col: pl.BlockSpec((HG_TB, D_C), lambda b, t: (blk0 + b * n_tb + t, col))
    bwd = lambda col: pl.BlockSpec((HG_TB, D_C), lambda b, t: (blk0 + b * n_tb + n_tb - 1 - t, col))
    st = pl.BlockSpec((1, H_C, DK_C, DV_C), lambda b, t: (b, 0, 0, 0))
    const = lambda a: pl.BlockSpec(a.shape, lambda b, t: (0,) * a.ndim)
    o_shape = jax.ShapeDtypeStruct(prev_f.shape, F32)
    st_shape = jax.ShapeDtypeStruct((n_seq, H_C, DK_C, DV_C), F32)
    any_spec = pl.BlockSpec(memory_space=pl.ANY)
    return pl.pallas_call(
        functools.partial(_hgrn_kernel, n_tb),
        grid=(n_seq, n_tb),
        in_specs=[fwd(0), fwd(1), fwd(3), bwd(0), bwd(2), bwd(3), const(lb_params), st, st,
                  const(tri_f), const(tri_b), any_spec, any_spec],
        out_specs=[fwd(0), bwd(0), st, st],
        out_shape=[o_shape, o_shape, st_shape, st_shape],
        input_output_aliases={11: 0, 12: 1},
        scratch_shapes=[pltpu.VMEM((2, H_C, DV_C, DK_C), F32)],
        compiler_params=_cparams(2),
        name="hgrn_scan",
    )(p1, p1, p1, p1, p1, p1, lb_params, s0_f, s0_b, tri_f, tri_b, prev_f, prev_b)


def _router_kernel(h_ref, rhi_ref, rlo_ref, bias_ref, sel_ref, eidx_ref, ew_ref, cnt_ref):
    x = h_ref[...]
    tm = x.shape[0]
    logits = _dot_nt(rhi_ref[...], x) + _dot_nt(rlo_ref[...], x)
    scores = _sigmoid(logits)
    biased = scores + bias_ref[...]
    per = N_EXPERTS // N_GROUPS
    sub = lax.broadcasted_iota(jnp.int32, (per, tm), 0)
    gs_rows = []
    for g in range(N_GROUPS):
        blk = biased[g * per:(g + 1) * per]
        m1 = jnp.max(blk, axis=0, keepdims=True)
        first = jnp.min(jnp.where(blk == m1, sub, per), axis=0, keepdims=True)
        m2 = jnp.max(jnp.where(sub == first, -jnp.inf, blk), axis=0, keepdims=True)
        gs_rows.append(m1 + m2)
    gs = jnp.concatenate(gs_rows, axis=0)
    gi = lax.broadcasted_iota(jnp.int32, gs.shape, 0)
    rank = jnp.zeros(gs.shape, jnp.int32)
    for s in range(1, N_GROUPS):
        other = pltpu.roll(gs, s, 0)
        oi = pltpu.roll(gi, s, 0)
        beats = jnp.logical_or(other > gs, jnp.logical_and(other == gs, oi < gi))
        rank = rank + jnp.where(beats, 1, 0)
    keep = jnp.where(rank < TOPK_GROUPS, 1.0, 0.0)
    emask = jnp.concatenate([jnp.broadcast_to(keep[g:g + 1], (per, tm)) for g in range(N_GROUPS)], axis=0)
    cur = jnp.where(emask > 0.0, biased, -jnp.inf)
    ei = lax.broadcasted_iota(jnp.int32, cur.shape, 0)
    sel = jnp.zeros(cur.shape, F32)
    idxs, vals = [], []
    for _ in range(TOP_K):
        m = jnp.max(cur, axis=0, keepdims=True)
        idx = jnp.min(jnp.where(cur == m, ei, N_EXPERTS), axis=0, keepdims=True)
        pick = ei == idx
        idxs.append(idx)
        vals.append(jnp.sum(jnp.where(pick, scores, 0.0), axis=0, keepdims=True))
        sel = jnp.where(pick, 1.0, sel)
        cur = jnp.where(pick, -jnp.inf, cur)
    w = jnp.concatenate(vals, axis=0)
    eidx_ref[...] = jnp.concatenate(idxs, axis=0)
    ew_ref[...] = w / jnp.sum(w, axis=0, keepdims=True) * ROUTED_SCALE
    sel_ref[...] = _bf(sel)

    @pl.when(pl.program_id(0) == 0)
    def _():
        cnt_ref[...] = jnp.zeros_like(cnt_ref)

    cnt_ref[...] += jnp.sum(sel, axis=1, keepdims=True)


def _router(hffn, router, bias):
    n = hffn.shape[0]
    r_t = router.T
    r_hi = _bf(r_t)
    r_lo = _bf(r_t - r_hi.astype(F32))
    const = lambda a: pl.BlockSpec(a.shape, lambda i: (0,) * a.ndim)
    b_col = bias.reshape(N_EXPERTS, 1)
    return pl.pallas_call(
        _router_kernel,
        grid=(n // TM,),
        in_specs=[pl.BlockSpec((TM, D_MODEL), lambda i: (i, 0)), const(r_hi), const(r_lo), const(b_col)],
        out_specs=[pl.BlockSpec((N_EXPERTS, TM), lambda i: (0, i)),
                   pl.BlockSpec((TOP_K, TM), lambda i: (0, i)),
                   pl.BlockSpec((TOP_K, TM), lambda i: (0, i)),
                   pl.BlockSpec((N_EXPERTS, LANES), lambda i: (0, 0))],
        out_shape=[jax.ShapeDtypeStruct((N_EXPERTS, n), BF16),
                   jax.ShapeDtypeStruct((TOP_K, n), jnp.int32),
                   jax.ShapeDtypeStruct((TOP_K, n), F32),
                   jax.ShapeDtypeStruct((N_EXPERTS, LANES), F32)],
        compiler_params=_cparams(1),
        name="router",
    )(hffn, r_hi, r_lo, b_col)


def _positions_kernel(sel_ref, eidx_ref, base_ref, upper_ref, pos_ref, carry_ref):
    @pl.when(pl.program_id(0) == 0)
    def _():
        carry_ref[...] = jnp.zeros_like(carry_ref)

    sel = sel_ref[...]
    rank = jnp.dot(sel, upper_ref[...], preferred_element_type=F32)
    pos_e = base_ref[:, 0:1] + carry_ref[:, 0:1] + rank
    ei = lax.broadcasted_iota(jnp.int32, pos_e.shape, 0)
    eidx = eidx_ref[...]
    rows = [jnp.sum(jnp.where(ei == eidx[k:k + 1], pos_e, 0.0), axis=0, keepdims=True) for k in range(TOP_K)]
    pos_ref[...] = jnp.concatenate(rows, axis=0).astype(jnp.int32)
    carry_ref[...] += jnp.sum(sel.astype(F32), axis=1, keepdims=True)


def _positions(sel, eidx, base):
    n = sel.shape[1]
    pb = POS_TB
    upper = jnp.asarray(np.triu(np.ones((pb, pb), np.float32), 1), dtype=BF16)
    return pl.pallas_call(
        _positions_kernel,
        grid=(n // pb,),
        in_specs=[pl.BlockSpec((N_EXPERTS, pb), lambda i: (0, i)),
                  pl.BlockSpec((TOP_K, pb), lambda i: (0, i)),
                  pl.BlockSpec((N_EXPERTS, LANES), lambda i: (0, 0)),
                  pl.BlockSpec((pb, pb), lambda i: (0, 0))],
        out_specs=pl.BlockSpec((TOP_K, pb), lambda i: (0, i)),
        out_shape=jax.ShapeDtypeStruct((TOP_K, n), jnp.int32),
        scratch_shapes=[pltpu.VMEM((N_EXPERTS, LANES), F32)],
        compiler_params=_cparams(1),
        name="positions",
    )(sel, eidx, base, upper)


def _pack_pairs(x):
    half = x.shape[1] // 2
    bits = lax.bitcast_convert_type(_bf(x).astype(F32), jnp.uint32)
    return (bits[:, :half] >> 16) | (bits[:, half:] & jnp.uint32(0xFFFF0000))


def _unpack_pairs(w):
    lo = lax.bitcast_convert_type(w << 16, F32)
    hi = lax.bitcast_convert_type(w & jnp.uint32(0xFFFF0000), F32)
    return jnp.concatenate([_bf(lo), _bf(hi)], axis=1)


def _sc_gather(table, idx):
    b, w = idx.shape[0], table.shape[1]
    n_workers = SC_CORES * SC_SUBCORES
    per_w = b // n_workers
    assert b % (n_workers * SC_CHUNK) == 0
    mesh = plsc.VectorSubcoreMesh(core_axis_name="c", subcore_axis_name="s")

    n_chunk = per_w // SC_CHUNK
    assert n_chunk % 2 == 0

    @functools.partial(
        pl.kernel, mesh=mesh, out_type=jax.ShapeDtypeStruct((b, w), table.dtype),
        scratch_types=[pltpu.VMEM((2, SC_CHUNK), jnp.int32), pltpu.VMEM((2, SC_CHUNK, w), table.dtype),
                       pltpu.SemaphoreType.DMA((2,))])
    def gather(table_hbm, idx_hbm, out_hbm, idx_v, rows_v, sems):
        wid = lax.axis_index("s") * SC_CORES + lax.axis_index("c")
        base = wid * per_w

        def start(c, slot):
            off = pl.multiple_of(base + c * SC_CHUNK, SC_CHUNK)
            pltpu.sync_copy(idx_hbm.at[pl.ds(off, SC_CHUNK)], idx_v.at[slot])
            pltpu.async_copy(table_hbm.at[idx_v.at[slot]], rows_v.at[slot], sems.at[slot])

        def finish(c, slot):
            off = pl.multiple_of(base + c * SC_CHUNK, SC_CHUNK)
            pltpu.make_async_copy(table_hbm.at[idx_v.at[slot]], rows_v.at[slot], sems.at[slot]).wait()
            pltpu.sync_copy(rows_v.at[slot], out_hbm.at[pl.ds(off, SC_CHUNK)])

        start(0, 0)

        @pl.loop(0, n_chunk, step=2)
        def _(c):
            start(c + 1, 1)
            finish(c, 0)

            @pl.when(c + 2 < n_chunk)
            def _():
                start(c + 2, 0)

            finish(c + 1, 1)

    return gather(table, idx)


def _sc_scatter(src, pos3, n_rows):
    n, w = src.shape
    n_workers = SC_CORES * SC_SUBCORES
    per_w = n // n_workers
    assert n % (n_workers * SC_CHUNK) == 0
    mesh = plsc.VectorSubcoreMesh(core_axis_name="c", subcore_axis_name="s")

    n_chunk = per_w // SC_CHUNK
    assert n_chunk % 2 == 0

    @functools.partial(
        pl.kernel, mesh=mesh, out_type=jax.ShapeDtypeStruct((n_rows, w), src.dtype),
        scratch_types=[pltpu.VMEM((2, TOP_K, SC_CHUNK), jnp.int32), pltpu.VMEM((2, SC_CHUNK, w), src.dtype),
                       pltpu.SemaphoreType.DMA((2,)), pltpu.SemaphoreType.DMA((2,))])
    def scatter(src_hbm, pos_hbm, out_hbm, idx_v, rows_v, ld_sems, sc_sems):
        wid = lax.axis_index("s") * SC_CORES + lax.axis_index("c")
        base = wid * per_w

        def loads(c, slot):
            off = pl.multiple_of(base + c * SC_CHUNK, SC_CHUNK)
            return (pltpu.make_async_copy(src_hbm.at[pl.ds(off, SC_CHUNK)], rows_v.at[slot], ld_sems.at[slot]),
                    pltpu.make_async_copy(pos_hbm.at[off // SC_CHUNK], idx_v.at[slot], ld_sems.at[slot]))

        def scatter_chunk(slot):
            copies = [pltpu.async_copy(rows_v.at[slot], out_hbm.at[idx_v.at[slot, k]], sc_sems.at[slot])
                      for k in range(TOP_K)]
            for cp in copies:
                cp.wait()

        def half_step(c, slot):
            for cp in loads(c, slot):
                cp.wait()

            @pl.when(c + 1 < n_chunk)
            def _():
                for cp in loads(c + 1, 1 - slot):
                    cp.start()

            scatter_chunk(slot)

        for cp in loads(0, 0):
            cp.start()

        @pl.loop(0, n_chunk, step=2)
        def _(c):
            half_step(c, 0)
            half_step(c + 1, 1)

    return scatter(src, pos3)


def _experts_kernel(te_ref, nu_ref, xs_ref, wg_ref, wu_ref, wd_ref, ys_ref, wg_bf, wu_bf, wd_bf):
    i = pl.program_id(0)
    active = i < nu_ref[0]

    @pl.when(jnp.logical_and(active, jnp.logical_or(i == 0, te_ref[i] != te_ref[jnp.maximum(i - 1, 0)])))
    def _():
        wg_bf[...] = _bf(wg_ref[0])
        wu_bf[...] = _bf(wu_ref[0])
        wd_bf[...] = _bf(wd_ref[0])

    @pl.when(active)
    def _():
        x = _unpack_pairs(xs_ref[...])
        act = _glu(x, wg_bf[...], wu_bf[...])
        ys_ref[...] = _pack_pairs(jnp.dot(_bf(act), wd_bf[...], preferred_element_type=F32))


def _experts(xs, tile_expert, n_used, mp, layer):
    n_tiles = xs.shape[0] // MOE_TILE
    half = D_MODEL // 2
    wspec = lambda shape: pl.BlockSpec((None, 1) + shape, lambda i, te, nu: (layer, te[i], 0, 0))
    return pl.pallas_call(
        _experts_kernel,
        grid_spec=pltpu.PrefetchScalarGridSpec(
            num_scalar_prefetch=2, grid=(n_tiles,),
            in_specs=[pl.BlockSpec((MOE_TILE, half), lambda i, te, nu: (jnp.minimum(i, nu[0] - 1), 0)),
                      wspec((D_MODEL, D_EXPERT)), wspec((D_MODEL, D_EXPERT)), wspec((D_EXPERT, D_MODEL))],
            out_specs=pl.BlockSpec((MOE_TILE, half), lambda i, te, nu: (jnp.minimum(i, nu[0] - 1), 0)),
            scratch_shapes=[pltpu.VMEM((D_MODEL, D_EXPERT), BF16), pltpu.VMEM((D_MODEL, D_EXPERT), BF16),
                            pltpu.VMEM((D_EXPERT, D_MODEL), BF16)]),
        out_shape=jax.ShapeDtypeStruct(xs.shape, jnp.uint32),
        compiler_params=_cparams(1),
        name="experts",
    )(tile_expert, n_used, xs, mp['wg'], mp['wu'], mp['wd'])


def _combine_kernel(h_ref, *refs):
    yg_refs = refs[:TOP_K]
    ew_ref, eye_ref, sg_ref, su_ref, sd_ref, x1_ref, g2_ref, o_ref = refs[TOP_K:]
    act = _glu(h_ref[...], sg_ref[...], su_ref[...])
    acc = jnp.dot(_bf(act), sd_ref[...], preferred_element_type=F32)
    ew = ew_ref[...]
    hi = _bf(ew)
    lo = _bf(ew - hi.astype(F32))
    ew_t = _dot_tn(hi, eye_ref[...]) + _dot_tn(lo, eye_ref[...])
    for k in range(TOP_K):
        acc = acc + ew_t[:, k:k + 1] * _unpack_pairs(yg_refs[k][...]).astype(F32)
    o_ref[...] = x1_ref[...] + g2_ref[...] * acc


def _combine(rows, hffn, yg, ew, mp, x1, mod4, layer, row0, n_out):
    half = D_MODEL // 2
    n_blk = rows.n // TM
    blk0 = row0 // TM
    const = lambda a: pl.BlockSpec(a.shape, lambda i: (0,) * a.ndim)
    tok = lambda w: pl.BlockSpec((TM, w), lambda i: (blk0 + i, 0))
    slot = lambda k: pl.BlockSpec((TM, half), lambda i: (k * n_blk + blk0 + i, 0))
    eye = jnp.eye(TOP_K, dtype=BF16)
    return pl.pallas_call(
        _combine_kernel,
        grid=(n_out // TM,),
        in_specs=[tok(D_MODEL)] + [slot(k) for k in range(TOP_K)]
                 + [pl.BlockSpec((TOP_K, TM), lambda i: (0, blk0 + i)), const(eye),
                    const(mp['sg']), const(mp['su']), const(mp['sd']), tok(D_MODEL),
                    rows.mod_spec(layer, 5, TM, blk0)],
        out_specs=pl.BlockSpec((TM, D_MODEL), lambda i: (i, 0)),
        out_shape=jax.ShapeDtypeStruct((n_out, D_MODEL), F32),
        compiler_params=_cparams(1),
        name=f"combine{layer}",
    )(hffn, *([yg] * TOP_K), ew, eye, mp['sg'], mp['su'], mp['sd'], x1, mod4)


def _moe(rows, hffn, hpack, x1, router, bias, mp, mod4, layer, out_ranges):
    n = rows.n
    sel, eidx, ew, cnt = _router(hffn, router, bias)
    counts = cnt[:, 0].astype(jnp.int32)
    padded = (counts + MOE_TILE - 1) // MOE_TILE * MOE_TILE
    ends = jnp.cumsum(padded)
    n_rows = n * TOP_K + N_EXPERTS * MOE_TILE
    n_tiles = n_rows // MOE_TILE
    base = jnp.broadcast_to((ends - padded).astype(F32)[:, None], (N_EXPERTS, LANES))
    tile_start = jnp.arange(n_tiles, dtype=jnp.int32) * MOE_TILE
    tile_expert = jnp.minimum(jnp.sum((ends[None, :] <= tile_start[:, None]).astype(jnp.int32), axis=1),
                              N_EXPERTS - 1)
    n_used = (ends[-1:] // MOE_TILE).astype(jnp.int32)
    pos = _positions(sel, eidx, base)
    pos3 = pos.reshape(TOP_K, n // SC_CHUNK, SC_CHUNK).transpose(1, 0, 2)
    xs = _sc_scatter(hpack, pos3, n_rows)
    ys = _experts(xs, tile_expert, n_used, mp, layer)
    yg = _sc_gather(ys, pos.reshape(-1))
    return [_combine(rows, hffn, yg, ew, mp, x1, mod4, layer, row0, n_out) for row0, n_out in out_ranges]


def _glu(x, wg, wu):
    hg = jnp.dot(x, wg, preferred_element_type=F32)
    hu = jnp.dot(x, wu, preferred_element_type=F32)
    return _silu(hg) * hu


def kernel(x_prompt, x_sample, c, c_ctx, cache_attn_k, cache_attn_v, state_rwkv_fwd, state_rwkv_bwd,
           state_hgrn_fwd, state_hgrn_bwd, norm1_g, norm2_g, mod_w, mod_b, ab_w_in, ab_w_out, attn_q_norm,
           attn_k_norm, attn_sink, rwkv_mu, rwkv_w0, rwkv_w2, rwkv_a0, rwkv_a2, rwkv_g2, rwkv_k_k, rwkv_k_a,
           rwkv_r_k, rwkv_ln_w, rwkv_ln_b, hgrn_w_in, hgrn_w_out, hgrn_lower_bounds, hgrn_norm_g, moe_router,
           moe_bias, moe_w_gate, moe_w_up, moe_w_down, moe_shared_gate, moe_shared_up, moe_shared_down):
    n_cseq, cseq, _ = x_prompt.shape
    n_lseq, lseq, _ = x_sample.shape
    depth = mod_w.shape[0]
    assert depth == 2 and n_lseq + 1 <= SUBLANES
    assert cseq == PREP_TM and lseq % TM == 0 and lseq % HG_TB == 0 and cseq % HG_TB == 0
    assert (n_cseq * cseq) % TM == 0
    assert n_cseq % RW_BB == 0 and n_lseq % RW_BB == 0 and (n_cseq * cseq) % (lseq * RW_BB) == 0
    rows = _Rows(n_cseq * cseq, n_lseq * lseq, lseq)
    assert rows.n % MOE_TILE == 0 and lseq % MOE_TILE == 0 and rows.n_ctx % MOE_TILE == 0
    kv_w = KV_A * HD_A

    xs = (x_prompt.reshape(rows.n_ctx, D_MODEL), x_sample.reshape(rows.n_lat, D_MODEL))
    cvecs = jnp.concatenate([c_ctx[None, :], c, jnp.zeros((SUBLANES - 1 - n_lseq, D_MODEL), F32)], axis=0)
    mod4 = _modulation(cvecs, mod_w, mod_b).reshape(depth, SUBLANES, 1, 6 * D_MODEL)

    ones_q = _block_ones(W_A, HD_A)
    ones_k = _block_ones(kv_w, HD_A)
    ones_b = _block_ones(W_B, HD_B)
    ones_pair = _block_ones(LANES, HD_B)[:LANES]
    ones_pair = jnp.kron(jnp.eye(2, dtype=BF16), ones_pair)
    cos_t, sin_t = _rope_tables(lseq)

    def moe(l, hffn, hpack, x1, out_ranges):
        mp = {'wg': moe_w_gate, 'wu': moe_w_up, 'wd': moe_w_down,
              'sg': _bf(moe_shared_gate[l]), 'su': _bf(moe_shared_up[l]), 'sd': _bf(moe_shared_down[l])}
        return _moe(rows, hffn, hpack, x1, moe_router[l], moe_bias[l], mp, mod4, l, out_ranges)

    assert W_A == W_B
    all_rows = jnp.zeros((rows.n, W_B), F32)

    pr = {'mu': rwkv_mu[0], 'w0': rwkv_w0[0], 'w2': rwkv_w2[0], 'a0': rwkv_a0[0], 'a2': rwkv_a2[0],
          'g2': rwkv_g2[0], 'k_k': rwkv_k_k[0], 'k_a': rwkv_k_a[0], 'r_k': rwkv_r_k[0].reshape(-1),
          'ln_w': rwkv_ln_w[0], 'ln_b': rwkv_ln_b[0]}
    p_att, p_rw = _inproj(rows, xs, norm1_g[0], mod4, 0, _bf(ab_w_in[0]), (ATT_IN, RWKV_IN), TM)
    qg_t = jnp.tile(attn_q_norm[0], H_A).reshape(1, W_A)
    kg_t = jnp.tile(attn_k_norm[0], KV_A).reshape(1, kv_w)
    o_att, new_k, new_v = _ctx_attention(p_att, n_cseq, cseq, qg_t, kg_t, attn_sink[0], ones_q, ones_k, all_rows)
    past = cache_attn_k.shape[2]
    o_att = _lat_attention(p_att, rows.n_ctx // lseq, n_lseq, lseq, qg_t, kg_t, attn_sink[0], ones_q, ones_k,
                           cos_t, sin_t, cache_attn_k[:, 0].reshape(n_lseq, past, kv_w),
                           cache_attn_v[:, 0].reshape(n_lseq, past, kv_w), o_att)

    pp = _rwkv_prep(rows, p_rw, pr, ones_b)
    zero_st = jnp.zeros((n_cseq, H_B // 2, HD_B, LANES), F32)
    o_f, o_b, sf_c, sb_c = _rwkv_scan(pp, 0, n_cseq, cseq, zero_st, zero_st, ones_pair, all_rows, all_rows)
    o_f, o_b, _, _ = _rwkv_scan(pp, rows.n_ctx, n_lseq, lseq, _state_to_pairs(state_rwkv_fwd[:, 0]),
                                _state_to_pairs(state_rwkv_bwd[:, 0]), ones_pair, o_f, o_b)
    x1, hffn, hpack = _outproj0(rows, xs, o_att, o_f, o_b, pp, pr, ones_b, _bf(ab_w_out[0]), norm2_g[0], mod4, 0)
    (x,) = moe(0, hffn, hpack, x1, [(0, rows.n)])

    (p1,) = _inproj(rows, (x, x), norm1_g[1], mod4, 1, _bf(hgrn_w_in[0]), (IN_C,), IN1_TM)
    zero_h = jnp.zeros((n_cseq, H_C, DK_C, DV_C), F32)
    all_rows_c = jnp.zeros((rows.n, D_C), F32)
    h_f, h_b, hsf_c, hsb_c = _hgrn_scan(p1, 0, n_cseq, cseq, hgrn_lower_bounds, zero_h, zero_h,
                                        all_rows_c, all_rows_c)
    h_f, h_b, _, _ = _hgrn_scan(p1, rows.n_ctx, n_lseq, lseq, hgrn_lower_bounds,
                                state_hgrn_fwd[:, 0], state_hgrn_bwd[:, 0], h_f, h_b)
    x1, hffn, hpack = _outproj1(rows, (x, x), h_f, h_b, p1, hgrn_norm_g[0], _bf(hgrn_w_out[0]), norm2_g[1],
                                mod4, 1)
    y_c, y_l = moe(1, hffn, hpack, x1, [(0, rows.n_ctx), (rows.n_ctx, rows.n_lat)])

    y_prompt = y_c.reshape(n_cseq, cseq, D_MODEL)
    y_sample = y_l.reshape(n_lseq, lseq, D_MODEL)
    return (y_prompt, y_sample,
            new_k.reshape(n_cseq, 1, cseq, KV_A, HD_A), new_v.reshape(n_cseq, 1, cseq, KV_A, HD_A),
            _pairs_to_state(sf_c)[:, None], _pairs_to_state(sb_c)[:, None],
            hsf_c[:, None], hsb_c[:, None])
```

```python
import functools

import numpy as np
import jax
import jax.numpy as jnp
from jax import lax
from jax.experimental import pallas as pl
from jax.experimental.pallas import tpu as pltpu
from jax.experimental.pallas import tpu_sc as plsc

F32 = jnp.float32
BF16 = jnp.bfloat16

D_MODEL = 1024
GRID_W = 64
H_A = 8
KV_A = 2
G_A = H_A // KV_A
HD_A = 64
W_A = H_A * HD_A
WINDOW = 128
QBLK = 128
ROPE_BASE = 10000.0
ROPE_PAIR = HD_A // 4
ATTN_SCALE = HD_A ** -0.5
NEG_INF = -1e30
H_B = 8
HD_B = 64
W_B = H_B * HD_B
LORA_W = 64
LORA_A = 64
LORA_G = 128
GN_EPS = 64e-5
ATT_IN = W_A + 2 * KV_A * HD_A
RWKV_IN = 3 * W_B + LORA_W + LORA_A + LORA_G
IN_AB = ATT_IN + RWKV_IN
H_C = 8
DK_C = 128
DV_C = 128
D_C = H_C * DV_C
CHUNK = 64
IN_C = 5 * D_C
N_EXPERTS = 64
TOP_K = 8
N_GROUPS = 8
TOPK_GROUPS = 4
D_EXPERT = 256
ROUTED_SCALE = 2.5
EPS = 1e-6

LANES = 128
SUBLANES = 8
VMEM_LIMIT = 52 * 1024 * 1024

TM = 512
PREP_TM = 256
IN1_TM = 256
RW_TB = 128
RW_BB = 4
RW_GROUP_BB = 4
HG_TB = 256
MOE_TILE = 512
POS_TB = 512
SC_CORES = 2
SC_SUBCORES = 16
SC_CHUNK = 64


def _cparams(n_axes):
    return pltpu.CompilerParams(dimension_semantics=("arbitrary",) * n_axes,
                                vmem_limit_bytes=VMEM_LIMIT)


def _bf(x):
    return x.astype(BF16)


def _split2(x):
    hi = lax.bitcast_convert_type(
        lax.bitcast_convert_type(x, jnp.uint32) & jnp.uint32(0xFFFF0000), F32)
    return hi, x - hi


def _seg_sum(x, ones2):
    hi, lo = _split2(x)
    return jnp.dot(jnp.concatenate([_bf(hi), _bf(lo)], axis=1), ones2,
                   preferred_element_type=F32)


def _dot_nt(a, b):
    return lax.dot_general(a, b, (((1,), (1,)), ((), ())), preferred_element_type=F32)


def _dot_tn(a, b):
    return lax.dot_general(a, b, (((0,), (0,)), ((), ())), preferred_element_type=F32)


def _sigmoid(x):
    return 1.0 / (1.0 + jnp.exp(-x))


def _silu(x):
    return x * _sigmoid(x)


def _chunks(seq, n):
    seq = list(seq)
    return [seq[i:i + n] for i in range(0, len(seq), n)]


def _block_ones(width, seg):
    idx = np.arange(width) // seg
    bd = (idx[:, None] == idx[None, :]).astype(np.float32)
    return jnp.asarray(np.concatenate([bd, bd], axis=0), dtype=BF16)


def _mod_kernel(c_ref, w_ref, b_ref, o_ref):
    s = _silu(c_ref[...])
    o_ref[0] = jnp.dot(_bf(s), _bf(w_ref[0]), preferred_element_type=F32) + b_ref[0]


def _modulation(cvecs, mod_w, mod_b):
    depth = mod_w.shape[0]
    n_col = 6 * D_MODEL // D_MODEL
    return pl.pallas_call(
        _mod_kernel,
        grid=(depth, n_col),
        in_specs=[pl.BlockSpec((SUBLANES, D_MODEL), lambda l, j: (0, 0)),
                  pl.BlockSpec((1, D_MODEL, D_MODEL), lambda l, j: (l, 0, j)),
                  pl.BlockSpec((1, 1, D_MODEL), lambda l, j: (l, 0, j))],
        out_specs=pl.BlockSpec((1, SUBLANES, D_MODEL), lambda l, j: (l, 0, j)),
        out_shape=jax.ShapeDtypeStruct((depth, SUBLANES, 6 * D_MODEL), F32),
        compiler_params=_cparams(2),
        name="modulation",
    )(cvecs, mod_w, mod_b.reshape(depth, 1, 6 * D_MODEL))


class _Rows:
    def __init__(self, n_ctx, n_lat, lat_seq):
        self.n_ctx, self.n_lat, self.lat_seq = n_ctx, n_lat, lat_seq
        self.n = n_ctx + n_lat

    def mod_row(self, i, tm):
        nctx_blk = self.n_ctx // tm
        per_seq = self.lat_seq // tm
        return jnp.where(i < nctx_blk, 0, 1 + (i - nctx_blk) // per_seq)

    def mod_spec(self, layer, chunk, tm, blk0=0):
        return pl.BlockSpec((None, None, 1, D_MODEL),
                            lambda i, *_: (layer, self.mod_row(i + blk0, tm), 0, chunk))


def _rms_mod(x, g, sc, sh):
    ms = jnp.mean(x * x, axis=-1, keepdims=True)
    return x * lax.rsqrt(ms + EPS) * g * (1.0 + sc) + sh


def _x_specs(rows, xs, tm=TM):
    xa, xb = xs
    nctx_blk = rows.n_ctx // tm
    lat0 = nctx_blk if xb.shape[0] == rows.n else 0
    return [pl.BlockSpec((tm, D_MODEL), lambda i: (jnp.minimum(i, nctx_blk - 1), 0)),
            pl.BlockSpec((tm, D_MODEL), lambda i: (jnp.maximum(i - nctx_blk, 0) + lat0, 0))]


def _pick_x(rows, xa_ref, xb_ref):
    return jnp.where(pl.program_id(0) < rows.n_ctx // xa_ref.shape[0], xa_ref[...], xb_ref[...])


def _inproj_kernel(rows, splits, xa_ref, xb_ref, g_ref, sh_ref, sc_ref, w_ref, *o_refs):
    h = _rms_mod(_pick_x(rows, xa_ref, xb_ref), g_ref[...], sc_ref[...], sh_ref[...])
    p = jnp.dot(_bf(h), w_ref[...], preferred_element_type=F32)
    lo = 0
    for o_ref, width in zip(o_refs, splits):
        o_ref[...] = p[:, lo:lo + width]
        lo += width


def _inproj(rows, xs, g, mod4, layer, w_bf, splits, tm):
    n_out = w_bf.shape[1]
    return pl.pallas_call(
        functools.partial(_inproj_kernel, rows, splits),
        grid=(rows.n // tm,),
        in_specs=_x_specs(rows, xs, tm) + [
            pl.BlockSpec((1, D_MODEL), lambda i: (0, 0)),
            rows.mod_spec(layer, 0, tm),
            rows.mod_spec(layer, 1, tm),
            pl.BlockSpec((D_MODEL, n_out), lambda i: (0, 0))],
        out_specs=[pl.BlockSpec((tm, wd), lambda i: (i, 0)) for wd in splits],
        out_shape=[jax.ShapeDtypeStruct((rows.n, wd), F32) for wd in splits],
        compiler_params=_cparams(1),
        name=f"inproj{layer}",
    )(*xs, g.reshape(1, D_MODEL), mod4, mod4, w_bf)


def _head_rms(x, gain_t, ones2):
    ms = _seg_sum(x * x, ones2) * (1.0 / HD_A)
    return x * lax.rsqrt(ms + EPS) * gain_t


def _sink_softmax_pv(parts, sink):
    m = jnp.maximum(functools.reduce(jnp.maximum, [jnp.max(s, axis=-1, keepdims=True) for s, _ in parts]), sink)
    den = jnp.exp(sink - m)
    acc = None
    for s, v in parts:
        p = jnp.exp(s - m)
        den = den + jnp.sum(p, axis=-1, keepdims=True)
        pv = jnp.dot(_bf(p), v, preferred_element_type=F32)
        acc = pv if acc is None else acc + pv
    return acc / den


def _ctx_attn_kernel(p_ref, qg_ref, kg_ref, sink_ref, ones_q_ref, ones_k_ref, prev_ref, o_ref, k_ref, v_ref):
    del prev_ref
    p = p_ref[...]
    q = _head_rms(p[:, :W_A], qg_ref[...], ones_q_ref[...]) * ATTN_SCALE
    k = _head_rms(p[:, W_A:W_A + KV_A * HD_A], kg_ref[...], ones_k_ref[...])
    v = p[:, W_A + KV_A * HD_A:ATT_IN]
    k_ref[0] = k
    v_ref[0] = v
    qb, kb, vb = _bf(q), _bf(k), _bf(v)
    outs = []
    for h in range(H_A):
        j = h // G_A
        s = _dot_nt(qb[:, h * HD_A:(h + 1) * HD_A], kb[:, j * HD_A:(j + 1) * HD_A])
        outs.append(_sink_softmax_pv([(s, vb[:, j * HD_A:(j + 1) * HD_A])], sink_ref[h]))
    o_ref[...] = jnp.concatenate(outs, axis=1)


def _ctx_attention(p_att, n_seq, seq, qg_t, kg_t, sink, ones_q, ones_k, prev):
    kv_w = KV_A * HD_A
    return pl.pallas_call(
        _ctx_attn_kernel,
        grid=(n_seq,),
        in_specs=[pl.BlockSpec((seq, ATT_IN), lambda b: (b, 0)),
                  pl.BlockSpec((1, W_A), lambda b: (0, 0)),
                  pl.BlockSpec((1, kv_w), lambda b: (0, 0)),
                  pl.BlockSpec(memory_space=pltpu.SMEM),
                  pl.BlockSpec(ones_q.shape, lambda b: (0, 0)),
                  pl.BlockSpec(ones_k.shape, lambda b: (0, 0)),
                  pl.BlockSpec(memory_space=pl.ANY)],
        out_specs=[pl.BlockSpec((seq, W_A), lambda b: (b, 0)),
                   pl.BlockSpec((1, seq, kv_w), lambda b: (b, 0, 0)),
                   pl.BlockSpec((1, seq, kv_w), lambda b: (b, 0, 0))],
        input_output_aliases={6: 0},
        out_shape=[jax.ShapeDtypeStruct(prev.shape, F32),
                   jax.ShapeDtypeStruct((n_seq, seq, kv_w), F32),
                   jax.ShapeDtypeStruct((n_seq, seq, kv_w), F32)],
        compiler_params=_cparams(1),
        name="ctx_attention",
    )(p_att, qg_t, kg_t, sink, ones_q, ones_k, prev)


def _rope(x, cos_t, sin_t):
    lane = lax.broadcasted_iota(jnp.int32, cos_t.shape, 1)
    low = (lane % (2 * ROPE_PAIR)) < ROPE_PAIR
    outs = []
    for s in range(x.shape[1] // LANES):
        xs = x[:, s * LANES:(s + 1) * LANES]
        partner = jnp.where(low, pltpu.roll(xs, LANES - ROPE_PAIR, 1), pltpu.roll(xs, ROPE_PAIR, 1))
        outs.append(xs * cos_t + partner * sin_t)
    return outs[0] if len(outs) == 1 else jnp.concatenate(outs, axis=1)


def _lat_attn_kernel(seq, p_ref, qg_ref, kg_ref, sink_ref, ones_q_ref, ones_k_ref, cos_ref, sin_ref,
                     kc_ref, vc_ref, prev_ref, o_ref, q_scr, k_scr, v_scr):
    del prev_ref
    kv_w = KV_A * HD_A
    p = p_ref[...]
    q = _head_rms(p[:, :W_A], qg_ref[...], ones_q_ref[...])
    k = _head_rms(p[:, W_A:W_A + kv_w], kg_ref[...], ones_k_ref[...])
    qr = _bf(_rope(q, cos_ref[...], sin_ref[...]) * ATTN_SCALE)
    kr = _bf(_rope(k, cos_ref[...], sin_ref[...]))
    vb = _bf(p[:, W_A + kv_w:ATT_IN])
    for h in range(H_A):
        q_scr[h] = qr[:, h * HD_A:(h + 1) * HD_A]
    for j in range(KV_A):
        k_scr[j] = kr[:, j * HD_A:(j + 1) * HD_A]
        v_scr[j] = vb[:, j * HD_A:(j + 1) * HD_A]
    kc = _bf(kc_ref[0])
    vc = _bf(vc_ref[0])
    n_local = 3 * QBLK
    grp = lax.broadcasted_iota(jnp.int32, (G_A * QBLK, 1), 0) // QBLK

    def block(i, carry):
        q0 = pl.multiple_of(i * QBLK, QBLK)
        start = pl.multiple_of(jnp.clip((i - 1) * QBLK, 0, seq - n_local), QBLK)
        ipos = q0 + lax.broadcasted_iota(jnp.int32, (G_A * QBLK, n_local), 0) % QBLK
        jpos = start + lax.broadcasted_iota(jnp.int32, (G_A * QBLK, n_local), 1)
        band = jnp.abs(jpos - ipos) <= WINDOW
        outs = []
        for j in range(KV_A):
            qs = jnp.concatenate([q_scr[j * G_A + g, pl.ds(q0, QBLK), :] for g in range(G_A)], axis=0)
            kl = k_scr[j, pl.ds(start, n_local), :]
            vl = v_scr[j, pl.ds(start, n_local), :]
            sink = jnp.zeros((G_A * QBLK, 1), F32)
            for g in range(G_A):
                sink = jnp.where(grp == g, sink_ref[j * G_A + g], sink)
            s_loc = jnp.where(band, _dot_nt(qs, kl), NEG_INF)
            s_ctx = _dot_nt(qs, kc[:, j * HD_A:(j + 1) * HD_A])
            o = _sink_softmax_pv([(s_loc, vl), (s_ctx, vc[:, j * HD_A:(j + 1) * HD_A])], sink)
            outs.extend(o[g * QBLK:(g + 1) * QBLK] for g in range(G_A))
        o_ref[pl.ds(q0, QBLK), :] = jnp.concatenate(outs, axis=1)
        return carry

    lax.fori_loop(0, seq // QBLK, block, 0)


def _lat_attention(p_att, row_blk0, n_seq, seq, qg_t, kg_t, sink, ones_q, ones_k, cos_t, sin_t, kc, vc, prev):
    kv_w = KV_A * HD_A
    past = kc.shape[1]
    return pl.pallas_call(
        functools.partial(_lat_attn_kernel, seq),
        grid=(n_seq,),
        in_specs=[pl.BlockSpec((seq, ATT_IN), lambda b: (row_blk0 + b, 0)),
                  pl.BlockSpec((1, W_A), lambda b: (0, 0)),
                  pl.BlockSpec((1, kv_w), lambda b: (0, 0)),
                  pl.BlockSpec(memory_space=pltpu.SMEM),
                  pl.BlockSpec(ones_q.shape, lambda b: (0, 0)),
                  pl.BlockSpec(ones_k.shape, lambda b: (0, 0)),
                  pl.BlockSpec((seq, LANES), lambda b: (0, 0)),
                  pl.BlockSpec((seq, LANES), lambda b: (0, 0)),
                  pl.BlockSpec((1, past, kv_w), lambda b: (b, 0, 0)),
                  pl.BlockSpec((1, past, kv_w), lambda b: (b, 0, 0)),
                  pl.BlockSpec(memory_space=pl.ANY)],
        out_specs=pl.BlockSpec((seq, W_A), lambda b: (row_blk0 + b, 0)),
        out_shape=jax.ShapeDtypeStruct(prev.shape, F32),
        input_output_aliases={10: 0},
        scratch_shapes=[pltpu.VMEM((H_A, seq, HD_A), BF16), pltpu.VMEM((KV_A, seq, HD_A), BF16),
                        pltpu.VMEM((KV_A, seq, HD_A), BF16)],
        compiler_params=_cparams(1),
        name="lat_attention",
    )(p_att, qg_t, kg_t, sink, ones_q, ones_k, cos_t, sin_t, kc, vc, prev)


def _rope_tables(seq):
    pos = np.arange(seq)
    row = (pos // GRID_W).astype(np.float32)
    col = (pos % GRID_W).astype(np.float32)
    d_axis = HD_A // 2
    inv = (ROPE_BASE ** (-np.arange(0, d_axis, 2, dtype=np.float32) / d_axis)).astype(np.float32)
    cos_h = np.zeros((seq, HD_A), np.float32)
    sin_h = np.zeros((seq, HD_A), np.float32)
    for seg, p_ in enumerate((row, col)):
        ang = (p_[:, None] * inv[None, :]).astype(np.float32)
        c, s = np.cos(ang), np.sin(ang)
        base = seg * d_axis
        cos_h[:, base:base + d_axis // 2] = c
        cos_h[:, base + d_axis // 2:base + d_axis] = c
        sin_h[:, base:base + d_axis // 2] = -s
        sin_h[:, base + d_axis // 2:base + d_axis] = s
    rep = LANES // HD_A
    return jnp.asarray(np.tile(cos_h, (1, rep))), jnp.asarray(np.tile(sin_h, (1, rep)))


def _rwkv_prep_kernel(rows, x_ref, prev_ref, next_ref, mu_ref, kk_ref, ka_ref, rk_ref, w0_ref, w2_ref,
                      a0_ref, a2_ref, g2_ref, ones_ref,
                      nkk_ref, r_ref, v_ref, g_ref, bonus_ref,
                      wf_ref, kaf_ref, kdf_ref, wb_ref, kab_ref, kdb_ref):
    i = pl.program_id(0)
    nctx_blk = rows.n_ctx // PREP_TM
    per_seq = rows.lat_seq // PREP_TM
    is_ctx = i < nctx_blk
    first = jnp.logical_or(is_ctx, (i - nctx_blk) % per_seq == 0)
    last = jnp.logical_or(is_ctx, (i - nctx_blk) % per_seq == per_seq - 1)
    x = x_ref[...]
    ridx = lax.broadcasted_iota(jnp.int32, x.shape, 0)
    prev_row = jnp.where(first, 0.0, prev_ref[SUBLANES - 1:SUBLANES, :])
    next_row = jnp.where(last, 0.0, next_ref[0:1, :])
    xm1 = jnp.where(ridx == 0, prev_row, pltpu.roll(x, 1, 0))
    xp1 = jnp.where(ridx == PREP_TM - 1, next_row, pltpu.roll(x, PREP_TM - 1, 0))
    pw = x + (0.5 * (xm1 + xp1) - x) * mu_ref[...]

    r = pw[:, 0:W_B]
    k = pw[:, W_B:2 * W_B]
    v = pw[:, 2 * W_B:3 * W_B]
    wd = pw[:, 3 * W_B:3 * W_B + LORA_W]
    ad = pw[:, 3 * W_B + LORA_W:3 * W_B + LORA_W + LORA_A]
    gd = pw[:, 3 * W_B + LORA_W + LORA_A:]
    ones2 = ones_ref[...]

    kk = k * kk_ref[...]
    kk = kk / jnp.maximum(jnp.sqrt(_seg_sum(kk * kk, ones2)), 1e-12)
    nkk_ref[...] = -kk
    r_ref[...] = r
    v_ref[...] = v
    g_ref[...] = jnp.dot(_bf(_sigmoid(gd)), g2_ref[...], preferred_element_type=F32)
    tw = _bf(jnp.tanh(wd))
    adb = _bf(ad)
    bonus = jnp.zeros_like(r)
    for d, (w_o, ka_o, kd_o) in enumerate(((wf_ref, kaf_ref, kdf_ref), (wb_ref, kab_ref, kdb_ref))):
        z = -(w0_ref[d:d + 1, :] + jnp.dot(tw, w2_ref[d], preferred_element_type=F32))
        softplus = jnp.maximum(z, 0.0) + jnp.log(1.0 + jnp.exp(-jnp.abs(z)))
        w_o[...] = jnp.exp(-jnp.exp(-softplus - 0.5))
        a = _sigmoid(a0_ref[d:d + 1, :] + jnp.dot(adb, a2_ref[d], preferred_element_type=F32))
        kd = k * (1.0 + (a - 1.0) * ka_ref[...])
        ka_o[...] = kk * a
        kd_o[...] = kd
        bonus = bonus + _seg_sum(r * kd * rk_ref[...], ones2) * v
    bonus_ref[...] = bonus


def _rwkv_prep(rows, p_rw, pr, ones_b):
    n = rows.n
    n_halo = n // SUBLANES
    blk_halo = PREP_TM // SUBLANES
    row = lambda a: a.reshape(1, -1)
    full = lambda a: pl.BlockSpec(a.shape, lambda i: (0,) * a.ndim)
    consts = [row(pr['mu']), row(pr['k_k']), row(pr['k_a']), row(pr['r_k']), pr['w0'], _bf(pr['w2']),
              pr['a0'], _bf(pr['a2']), _bf(pr['g2']), ones_b]
    outs = pl.pallas_call(
        functools.partial(_rwkv_prep_kernel, rows),
        grid=(n // PREP_TM,),
        in_specs=[pl.BlockSpec((PREP_TM, RWKV_IN), lambda i: (i, 0)),
                  pl.BlockSpec((SUBLANES, RWKV_IN), lambda i: (jnp.maximum(i * blk_halo - 1, 0), 0)),
                  pl.BlockSpec((SUBLANES, RWKV_IN), lambda i: (jnp.minimum((i + 1) * blk_halo, n_halo - 1), 0))]
                 + [full(a) for a in consts],
        out_specs=[pl.BlockSpec((PREP_TM, W_B), lambda i: (i, 0))] * 11,
        out_shape=[jax.ShapeDtypeStruct((n, W_B), F32)] * 11,
        compiler_params=_cparams(1),
        name="rwkv_prep",
    )(p_rw, p_rw, p_rw, *consts)
    names = ('nkk', 'r', 'v', 'g', 'bonus', 'w_f', 'ka_f', 'kd_f', 'w_b', 'ka_b', 'kd_b')
    return dict(zip(names, outs))


def _rwkv_scan_kernel(n_tb, nkkf_ref, rf_ref, vf_ref, wf_ref, kaf_ref, kdf_ref,
                      nkkb_ref, rb_ref, vb_ref, wb_ref, kab_ref, kdb_ref,
                      s0f_ref, s0b_ref, ones_ref, prevf_ref, prevb_ref,
                      of_ref, ob_ref, sff_ref, sfb_ref, sf_scr, sb_scr, vt_scr):
    del prevf_ref, prevb_ref
    s_scr = (sf_scr, sb_scr)
    tb = pl.program_id(1)
    n_pair = H_B // 2
    half = RW_TB // 2
    dirs = ((nkkf_ref, rf_ref, vf_ref, wf_ref, kaf_ref, kdf_ref, of_ref, False),
            (nkkb_ref, rb_ref, vb_ref, wb_ref, kab_ref, kdb_ref, ob_ref, True))

    @pl.when(tb == 0)
    def _():
        sf_scr[...] = s0f_ref[...]
        sb_scr[...] = s0b_ref[...]

    lane = lax.broadcasted_iota(jnp.int32, (HD_B, LANES), 1)
    for d, refs in enumerate(dirs):
        v_ref = refs[2]
        for bb in range(RW_BB):
            for p in range(n_pair):
                vt = v_ref[bb, :, p * LANES:(p + 1) * LANES].T
                top, bot = vt[:HD_B], vt[HD_B:]
                for s in range(2):
                    if s == 0:
                        t2 = jnp.where(lane < HD_B, top, pltpu.roll(bot, HD_B, 1))
                    else:
                        t2 = jnp.where(lane < HD_B, pltpu.roll(top, HD_B, 1), bot)
                    vt_scr[d, bb, p, s] = t2

    ones2 = ones_ref[...]
    row8 = lax.broadcasted_iota(jnp.int32, (SUBLANES, LANES), 0)
    lane8 = lax.broadcasted_iota(jnp.int32, (SUBLANES, LANES), 1)
    sel_r = jnp.logical_or(jnp.logical_and(row8 % 2 == 0, lane8 < HD_B),
                           jnp.logical_and(row8 % 2 == 1, lane8 >= HD_B))

    def row_of(rev, tt):
        return RW_TB - 1 - tt if rev else tt

    def emit_output(d, bb, tau):
        r_ref, o_ref = dirs[d][1], dirs[d][6]
        r = r_ref[bb, pl.ds(tau, 1), :]
        r8 = jnp.zeros((SUBLANES, LANES), F32)
        for p in range(n_pair):
            rp = jnp.broadcast_to(r[:, p * LANES:(p + 1) * LANES], (SUBLANES, LANES))
            r8 = jnp.where(jnp.logical_and(sel_r, row8 // 2 == p), rp, r8)
        s_all = jnp.concatenate([_bf(s_scr[d][bb, p]) for p in range(n_pair)], axis=0)
        o8 = _dot_nt(_bf(r8), s_all)
        o_parts = []
        for p in range(n_pair):
            for h in range(2):
                o_parts.append(o8[2 * p + h:2 * p + h + 1, p * HD_B:(p + 1) * HD_B])
        o_ref[bb, pl.ds(tau, 1), :] = jnp.concatenate(o_parts, axis=1)

    groups = [(d, bbs) for d in range(2) for bbs in _chunks(range(RW_BB), RW_GROUP_BB)]

    def reduce_phase(grp, tt):
        d, bbs = grp
        rev = dirs[d][7]
        tau = row_of(rev, tt)
        sub = tau // half
        lt = tau % half
        mask = jnp.logical_or(lane == lt, lane == lt + HD_B)
        lhs = []
        for bb in bbs:
            emit_output(d, bb, row_of(rev, jnp.maximum(tt - 1, 0)))
            nkk = dirs[d][0][bb, pl.ds(tau, 1), :]
            for p in range(n_pair):
                prod = s_scr[d][bb, p] * nkk[:, p * LANES:(p + 1) * LANES]
                lhs.append(jnp.concatenate([_bf(prod), _bf(jnp.where(mask, vt_scr[d, bb, p, sub], 0.0))],
                                           axis=1))
        return jnp.dot(jnp.concatenate(lhs, axis=0), ones2, preferred_element_type=F32)

    def update_phase(grp, tt, red):
        d, bbs = grp
        _, _, _, w_ref, ka_ref, kd_ref, _, rev = dirs[d]
        tau = row_of(rev, tt)
        for k, bb in enumerate(bbs):
            w = w_ref[bb, pl.ds(tau, 1), :]
            ka = ka_ref[bb, pl.ds(tau, 1), :]
            kd = kd_ref[bb, pl.ds(tau, 1), :]
            for p in range(n_pair):
                sl = slice(p * LANES, (p + 1) * LANES)
                r0 = (k * n_pair + p) * HD_B
                sa = red[r0:r0 + HD_B, :LANES]
                vcol = red[r0:r0 + HD_B, LANES:]
                s_scr[d][bb, p] = s_scr[d][bb, p] * w[:, sl] + sa * ka[:, sl] + vcol * kd[:, sl]

    def step(tt, carry):
        reds = [reduce_phase(g, tt) for g in groups]
        for g, red in zip(groups, reds):
            update_phase(g, tt, red)
        return carry

    lax.fori_loop(0, RW_TB, step, 0)
    for d in range(2):
        for bb in range(RW_BB):
            emit_output(d, bb, row_of(dirs[d][7], RW_TB - 1))

    @pl.when(tb == n_tb - 1)
    def _():
        sff_ref[...] = sf_scr[...]
        sfb_ref[...] = sb_scr[...]


def _rwkv_scan(pp, row0, n_seq, seq, s0_f, s0_b, ones_pair, prev_f, prev_b):
    n_tb = seq // RW_TB
    n_pair = H_B // 2
    blk0 = row0 // seq
    view = lambda a: a.reshape(a.shape[0] // seq, seq, W_B)
    fwd = pl.BlockSpec((RW_BB, RW_TB, W_B), lambda b, t: (blk0 // RW_BB + b, t, 0))
    bwd = pl.BlockSpec((RW_BB, RW_TB, W_B), lambda b, t: (blk0 // RW_BB + b, n_tb - 1 - t, 0))
    st = pl.BlockSpec((RW_BB, n_pair, HD_B, LANES), lambda b, t: (b, 0, 0, 0))
    ins_f = [view(pp[k]) for k in ('nkk', 'r', 'v', 'w_f', 'ka_f', 'kd_f')]
    ins_b = [view(pp[k]) for k in ('nkk', 'r', 'v', 'w_b', 'ka_b', 'kd_b')]
    st_shape = jax.ShapeDtypeStruct((n_seq, n_pair, HD_B, LANES), F32)
    o_shape = jax.ShapeDtypeStruct(view(prev_f).shape, F32)
    any_spec = pl.BlockSpec(memory_space=pl.ANY)
    o_f, o_b, sf, sb = pl.pallas_call(
        functools.partial(_rwkv_scan_kernel, n_tb),
        grid=(n_seq // RW_BB, n_tb),
        in_specs=[fwd] * 6 + [bwd] * 6 + [st, st, pl.BlockSpec(ones_pair.shape, lambda b, t: (0, 0)),
                                           any_spec, any_spec],
        out_specs=[fwd, bwd, st, st],
        out_shape=[o_shape, o_shape, st_shape, st_shape],
        input_output_aliases={15: 0, 16: 1},
        scratch_shapes=[pltpu.VMEM((RW_BB, n_pair, HD_B, LANES), F32),
                        pltpu.VMEM((RW_BB, n_pair, HD_B, LANES), F32),
                        pltpu.VMEM((2, RW_BB, n_pair, 2, HD_B, LANES), F32)],
        compiler_params=_cparams(2),
        name="rwkv_scan",
    )(*ins_f, *ins_b, s0_f, s0_b, ones_pair, view(prev_f), view(prev_b))
    return o_f.reshape(prev_f.shape), o_b.reshape(prev_b.shape), sf, sb


def _state_to_pairs(s):
    b = s.shape[0]
    return s.reshape(b, H_B // 2, 2, HD_B, HD_B).transpose(0, 1, 3, 2, 4).reshape(b, H_B // 2, HD_B, 2 * HD_B)


def _pairs_to_state(s):
    b = s.shape[0]
    return s.reshape(b, H_B // 2, HD_B, 2, HD_B).transpose(0, 1, 3, 2, 4).reshape(b, H_B, HD_B, HD_B)


def _tail(x, y, g1, n2g, sc2, sh2, x1_ref, h_ref, hp_ref):
    x1 = x + g1 * y
    x1_ref[...] = x1
    h = _rms_mod(x1, n2g, sc2, sh2)
    h_ref[...] = _bf(h)
    hp_ref[...] = _pack_pairs(h)


def _outproj0_kernel(rows, xa_ref, xb_ref, oa_ref, of_ref, ob_ref, bonus_ref, g_ref, lnw_ref, lnb_ref, ones_ref,
                     w_ref, g1_ref, n2g_ref, sc2_ref, sh2_ref, x1_ref, h_ref, hp_ref):
    o_sum = of_ref[...] + ob_ref[...]
    ones2 = ones_ref[...]
    mean = _seg_sum(o_sum, ones2) * (1.0 / HD_B)
    cen = o_sum - mean
    var = _seg_sum(cen * cen, ones2) * (1.0 / HD_B)
    gn = cen * lax.rsqrt(var + GN_EPS) * lnw_ref[...] + lnb_ref[...]
    o_rw = (gn + bonus_ref[...]) * g_ref[...]
    mix = jnp.concatenate([_bf(oa_ref[...]), _bf(o_rw)], axis=1)
    y = jnp.dot(mix, w_ref[...], preferred_element_type=F32)
    _tail(_pick_x(rows, xa_ref, xb_ref), y, g1_ref[...], n2g_ref[...], sc2_ref[...], sh2_ref[...], x1_ref, h_ref, hp_ref)


def _outproj0(rows, xs, o_att, o_f, o_b, pp, pr, ones_b, w_out_bf, n2g, mod4, layer):
    n = rows.n
    tok = lambda w: pl.BlockSpec((TM, w), lambda i: (i, 0))
    const = lambda a: pl.BlockSpec(a.shape, lambda i: (0,) * a.ndim)
    lnw, lnb, n2 = pr['ln_w'].reshape(1, -1), pr['ln_b'].reshape(1, -1), n2g.reshape(1, -1)
    return pl.pallas_call(
        functools.partial(_outproj0_kernel, rows),
        grid=(n // TM,),
        in_specs=_x_specs(rows, xs) + [tok(W_A), tok(W_B), tok(W_B), tok(W_B), tok(W_B),
                  const(lnw), const(lnb), const(ones_b), const(w_out_bf),
                  rows.mod_spec(layer, 2, TM), const(n2), rows.mod_spec(layer, 4, TM), rows.mod_spec(layer, 3, TM)],
        out_specs=[tok(D_MODEL), tok(D_MODEL), tok(D_MODEL // 2)],
        out_shape=[jax.ShapeDtypeStruct((n, D_MODEL), F32), jax.ShapeDtypeStruct((n, D_MODEL), BF16),
                   jax.ShapeDtypeStruct((n, D_MODEL // 2), jnp.uint32)],
        compiler_params=_cparams(1),
        name="outproj0",
    )(*xs, o_att, o_f, o_b, pp['bonus'], pp['g'], lnw, lnb, ones_b, w_out_bf, mod4, n2, mod4, mod4)


def _outproj1_kernel(rows, xa_ref, xb_ref, of_ref, ob_ref, gate_ref, ng_ref, w_ref, g1_ref, n2g_ref, sc2_ref, sh2_ref,
                     x1_ref, h_ref, hp_ref):
    o_sum = of_ref[...] + ob_ref[...]
    parts = []
    for h in range(H_C):
        oh = o_sum[:, h * DV_C:(h + 1) * DV_C]
        parts.append(oh * lax.rsqrt(jnp.mean(oh * oh, axis=-1, keepdims=True) + EPS))
    o = jnp.concatenate(parts, axis=1) * ng_ref[...] * _silu(gate_ref[...])
    y = jnp.dot(_bf(o), w_ref[...], preferred_element_type=F32)
    _tail(_pick_x(rows, xa_ref, xb_ref), y, g1_ref[...], n2g_ref[...], sc2_ref[...], sh2_ref[...], x1_ref, h_ref, hp_ref)


def _outproj1(rows, xs, o_f, o_b, p1, norm_g, w_out_bf, n2g, mod4, layer):
    n = rows.n
    tok = lambda w: pl.BlockSpec((TM, w), lambda i: (i, 0))
    const = lambda a: pl.BlockSpec(a.shape, lambda i: (0,) * a.ndim)
    ng, n2 = norm_g.reshape(1, -1), n2g.reshape(1, -1)
    return pl.pallas_call(
        functools.partial(_outproj1_kernel, rows),
        grid=(n // TM,),
        in_specs=_x_specs(rows, xs) + [tok(D_C), tok(D_C), pl.BlockSpec((TM, D_C), lambda i: (i, 4)),
                  const(ng), const(w_out_bf),
                  rows.mod_spec(layer, 2, TM), const(n2), rows.mod_spec(layer, 4, TM), rows.mod_spec(layer, 3, TM)],
        out_specs=[tok(D_MODEL), tok(D_MODEL), tok(D_MODEL // 2)],
        out_shape=[jax.ShapeDtypeStruct((n, D_MODEL), F32), jax.ShapeDtypeStruct((n, D_MODEL), BF16),
                   jax.ShapeDtypeStruct((n, D_MODEL // 2), jnp.uint32)],
        compiler_params=_cparams(1),
        name="outproj1",
    )(*xs, o_f, o_b, p1, ng, w_out_bf, mod4, n2, mod4, mod4)


def _hgrn_kernel(n_tb, qf_ref, ff_ref, if_ref, qb_ref, fb_ref, ib_ref, lbp_ref, s0f_ref, s0b_ref,
                 trif_ref, trib_ref, prevf_ref, prevb_ref, of_ref, ob_ref, sff_ref, sfb_ref, s_scr):
    del prevf_ref, prevb_ref
    tb = pl.program_id(1)

    @pl.when(tb == 0)
    def _():
        for h in range(H_C):
            s_scr[0, h] = s0f_ref[0, h].T
            s_scr[1, h] = s0b_ref[0, h].T

    lbp = lbp_ref[...]
    e = jnp.exp(lbp - jnp.max(lbp, axis=0, keepdims=True))
    sm = e / jnp.sum(e, axis=0, keepdims=True)
    lb = (sm[0:1] + sm[1:2]) - sm[0:1]

    n_chunk = HG_TB // CHUNK
    ti = lax.broadcasted_iota(jnp.int32, (HG_TB, HG_TB), 0)
    si = lax.broadcasted_iota(jnp.int32, (HG_TB, HG_TB), 1)
    same = (ti // CHUNK) == (si // CHUNK)
    dirs = ((qf_ref, ff_ref, if_ref, of_ref, trif_ref, jnp.logical_and(same, ti >= si), CHUNK - 1, False),
            (qb_ref, fb_ref, ib_ref, ob_ref, trib_ref, jnp.logical_and(same, ti <= si), 0, True))

    staged = []
    for d, (q_ref, f_ref, i_ref, o_ref, tri_ref, causal, last_row, rev) in enumerate(dirs):
        q = _silu(q_ref[...])
        f = lb + (1.0 - lb) * _sigmoid(f_ref[...])
        k = 1.0 - f
        v = _bf(i_ref[...])
        g = jnp.log(f)
        g1 = _bf(g)
        g2 = _bf(g - g1.astype(F32))
        tri2 = tri_ref[...]
        b_parts, last_parts, dec = [], [], []
        for c in range(n_chunk):
            rc = slice(c * CHUNK, (c + 1) * CHUNK)
            bc = jnp.dot(tri2, jnp.concatenate([g1[rc], g2[rc]], axis=0), preferred_element_type=F32)
            b_parts.append(bc)
            last = bc[last_row:last_row + 1]
            last_parts.append(jnp.broadcast_to(last, bc.shape))
            dec.append(jnp.exp(last))
        b = jnp.concatenate(b_parts, axis=0)
        b_last = jnp.concatenate(last_parts, axis=0)
        staged.append((_bf(q * jnp.exp(b)), _bf(k * jnp.exp(-b)), _bf(k * jnp.exp(b_last - b)), v, dec))

    for h in range(H_C):
        sl = slice(h * DK_C, (h + 1) * DK_C)
        for d, (q_ref, f_ref, i_ref, o_ref, tri_ref, causal, last_row, rev) in enumerate(dirs):
            q_in, k_in, k_out, v, dec = staged[d]
            qh, vh = q_in[:, sl], v[:, sl]
            att = jnp.where(causal, _dot_nt(qh, k_in[:, sl]), 0.0)
            o_intra = jnp.dot(_bf(att), vh, preferred_element_type=F32)
            s_t = s_scr[d, h]
            for c in (range(n_chunk - 1, -1, -1) if rev else range(n_chunk)):
                rc = slice(c * CHUNK, (c + 1) * CHUNK)
                o_ref[rc, sl] = o_intra[rc] + _dot_nt(qh[rc], _bf(s_t))
                s_t = dec[c][:, sl] * s_t + _dot_tn(vh[rc], k_out[rc, sl])
            s_scr[d, h] = s_t

    @pl.when(tb == n_tb - 1)
    def _():
        for h in range(H_C):
            sff_ref[0, h] = s_scr[0, h].T
            sfb_ref[0, h] = s_scr[1, h].T


def _hgrn_scan(p1, row0, n_seq, seq, lb_params, s0_f, s0_b, prev_f, prev_b):
    n_tb = seq // HG_TB
    blk0 = row0 // HG_TB
    tri = np.tril(np.ones((CHUNK, CHUNK), np.float32))
    tri_f = jnp.asarray(np.concatenate([tri] * 2, axis=1), dtype=BF16)
    tri_b = jnp.asarray(np.concatenate([tri.T] * 2, axis=1), dtype=BF16)
    fwd = lambda col: pl.BlockSpec((HG_TB, D_C), lambda b, t: (blk0 + b * n_tb + t, col))
    bwd = lambda col: pl.BlockSpec((HG_TB, D_C), lambda b, t: (blk0 + b * n_tb + n_tb - 1 - t, col))
    st = pl.BlockSpec((1, H_C, DK_C, DV_C), lambda b, t: (b, 0, 0, 0))
    const = lambda a: pl.BlockSpec(a.shape, lambda b, t: (0,) * a.ndim)
    o_shape = jax.ShapeDtypeStruct(prev_f.shape, F32)
    st_shape = jax.ShapeDtypeStruct((n_seq, H_C, DK_C, DV_C), F32)
    any_spec = pl.BlockSpec(memory_space=pl.ANY)
    return pl.pallas_call(
        functools.partial(_hgrn_kernel, n_tb),
        grid=(n_seq, n_tb),
        in_specs=[fwd(0), fwd(1), fwd(3), bwd(0), bwd(2), bwd(3), const(lb_params), st, st,
                  const(tri_f), const(tri_b), any_spec, any_spec],
        out_specs=[fwd(0), bwd(0), st, st],
        out_shape=[o_shape, o_shape, st_shape, st_shape],
        input_output_aliases={11: 0, 12: 1},
        scratch_shapes=[pltpu.VMEM((2, H_C, DV_C, DK_C), F32)],
        compiler_params=_cparams(2),
        name="hgrn_scan",
    )(p1, p1, p1, p1, p1, p1, lb_params, s0_f, s0_b, tri_f, tri_b, prev_f, prev_b)


def _router_kernel(h_ref, rhi_ref, rlo_ref, bias_ref, sel_ref, eidx_ref, ew_ref, cnt_ref):
    x = h_ref[...]
    tm = x.shape[0]
    logits = _dot_nt(rhi_ref[...], x) + _dot_nt(rlo_ref[...], x)
    scores = _sigmoid(logits)
    biased = scores + bias_ref[...]
    per = N_EXPERTS // N_GROUPS
    sub = lax.broadcasted_iota(jnp.int32, (per, tm), 0)
    gs_rows = []
    for g in range(N_GROUPS):
        blk = biased[g * per:(g + 1) * per]
        m1 = jnp.max(blk, axis=0, keepdims=True)
        first = jnp.min(jnp.where(blk == m1, sub, per), axis=0, keepdims=True)
        m2 = jnp.max(jnp.where(sub == first, -jnp.inf, blk), axis=0, keepdims=True)
        gs_rows.append(m1 + m2)
    gs = jnp.concatenate(gs_rows, axis=0)
    gi = lax.broadcasted_iota(jnp.int32, gs.shape, 0)
    rank = jnp.zeros(gs.shape, jnp.int32)
    for s in range(1, N_GROUPS):
        other = pltpu.roll(gs, s, 0)
        oi = pltpu.roll(gi, s, 0)
        beats = jnp.logical_or(other > gs, jnp.logical_and(other == gs, oi < gi))
        rank = rank + jnp.where(beats, 1, 0)
    keep = jnp.where(rank < TOPK_GROUPS, 1.0, 0.0)
    emask = jnp.concatenate([jnp.broadcast_to(keep[g:g + 1], (per, tm)) for g in range(N_GROUPS)], axis=0)
    cur = jnp.where(emask > 0.0, biased, -jnp.inf)
    ei = lax.broadcasted_iota(jnp.int32, cur.shape, 0)
    sel = jnp.zeros(cur.shape, F32)
    idxs, vals = [], []
    for _ in range(TOP_K):
        m = jnp.max(cur, axis=0, keepdims=True)
        idx = jnp.min(jnp.where(cur == m, ei, N_EXPERTS), axis=0, keepdims=True)
        pick = ei == idx
        idxs.append(idx)
        vals.append(jnp.sum(jnp.where(pick, scores, 0.0), axis=0, keepdims=True))
        sel = jnp.where(pick, 1.0, sel)
        cur = jnp.where(pick, -jnp.inf, cur)
    w = jnp.concatenate(vals, axis=0)
    eidx_ref[...] = jnp.concatenate(idxs, axis=0)
    ew_ref[...] = w / jnp.sum(w, axis=0, keepdims=True) * ROUTED_SCALE
    sel_ref[...] = _bf(sel)

    @pl.when(pl.program_id(0) == 0)
    def _():
        cnt_ref[...] = jnp.zeros_like(cnt_ref)

    cnt_ref[...] += jnp.sum(sel, axis=1, keepdims=True)


def _router(hffn, router, bias):
    n = hffn.shape[0]
    r_t = router.T
    r_hi = _bf(r_t)
    r_lo = _bf(r_t - r_hi.astype(F32))
    const = lambda a: pl.BlockSpec(a.shape, lambda i: (0,) * a.ndim)
    b_col = bias.reshape(N_EXPERTS, 1)
    return pl.pallas_call(
        _router_kernel,
        grid=(n // TM,),
        in_specs=[pl.BlockSpec((TM, D_MODEL), lambda i: (i, 0)), const(r_hi), const(r_lo), const(b_col)],
        out_specs=[pl.BlockSpec((N_EXPERTS, TM), lambda i: (0, i)),
                   pl.BlockSpec((TOP_K, TM), lambda i: (0, i)),
                   pl.BlockSpec((TOP_K, TM), lambda i: (0, i)),
                   pl.BlockSpec((N_EXPERTS, LANES), lambda i: (0, 0))],
        out_shape=[jax.ShapeDtypeStruct((N_EXPERTS, n), BF16),
                   jax.ShapeDtypeStruct((TOP_K, n), jnp.int32),
                   jax.ShapeDtypeStruct((TOP_K, n), F32),
                   jax.ShapeDtypeStruct((N_EXPERTS, LANES), F32)],
        compiler_params=_cparams(1),
        name="router",
    )(hffn, r_hi, r_lo, b_col)


def _positions_kernel(sel_ref, eidx_ref, base_ref, upper_ref, pos_ref, carry_ref):
    @pl.when(pl.program_id(0) == 0)
    def _():
        carry_ref[...] = jnp.zeros_like(carry_ref)

    sel = sel_ref[...]
    rank = jnp.dot(sel, upper_ref[...], preferred_element_type=F32)
    pos_e = base_ref[:, 0:1] + carry_ref[:, 0:1] + rank
    ei = lax.broadcasted_iota(jnp.int32, pos_e.shape, 0)
    eidx = eidx_ref[...]
    rows = [jnp.sum(jnp.where(ei == eidx[k:k + 1], pos_e, 0.0), axis=0, keepdims=True) for k in range(TOP_K)]
    pos_ref[...] = jnp.concatenate(rows, axis=0).astype(jnp.int32)
    carry_ref[...] += jnp.sum(sel.astype(F32), axis=1, keepdims=True)


def _positions(sel, eidx, base):
    n = sel.shape[1]
    pb = POS_TB
    upper = jnp.asarray(np.triu(np.ones((pb, pb), np.float32), 1), dtype=BF16)
    return pl.pallas_call(
        _positions_kernel,
        grid=(n // pb,),
        in_specs=[pl.BlockSpec((N_EXPERTS, pb), lambda i: (0, i)),
                  pl.BlockSpec((TOP_K, pb), lambda i: (0, i)),
                  pl.BlockSpec((N_EXPERTS, LANES), lambda i: (0, 0)),
                  pl.BlockSpec((pb, pb), lambda i: (0, 0))],
        out_specs=pl.BlockSpec((TOP_K, pb), lambda i: (0, i)),
        out_shape=jax.ShapeDtypeStruct((TOP_K, n), jnp.int32),
        scratch_shapes=[pltpu.VMEM((N_EXPERTS, LANES), F32)],
        compiler_params=_cparams(1),
        name="positions",
    )(sel, eidx, base, upper)


def _pack_pairs(x):
    half = x.shape[1] // 2
    bits = lax.bitcast_convert_type(_bf(x).astype(F32), jnp.uint32)
    return (bits[:, :half] >> 16) | (bits[:, half:] & jnp.uint32(0xFFFF0000))


def _unpack_pairs(w):
    lo = lax.bitcast_convert_type(w << 16, F32)
    hi = lax.bitcast_convert_type(w & jnp.uint32(0xFFFF0000), F32)
    return jnp.concatenate([_bf(lo), _bf(hi)], axis=1)


def _sc_gather(table, idx):
    b, w = idx.shape[0], table.shape[1]
    n_workers = SC_CORES * SC_SUBCORES
    per_w = b // n_workers
    assert b % (n_workers * SC_CHUNK) == 0
    mesh = plsc.VectorSubcoreMesh(core_axis_name="c", subcore_axis_name="s")

    n_chunk = per_w // SC_CHUNK
    assert n_chunk % 2 == 0

    @functools.partial(
        pl.kernel, mesh=mesh, out_type=jax.ShapeDtypeStruct((b, w), table.dtype),
        scratch_types=[pltpu.VMEM((2, SC_CHUNK), jnp.int32), pltpu.VMEM((2, SC_CHUNK, w), table.dtype),
                       pltpu.SemaphoreType.DMA((2,))])
    def gather(table_hbm, idx_hbm, out_hbm, idx_v, rows_v, sems):
        wid = lax.axis_index("s") * SC_CORES + lax.axis_index("c")
        base = wid * per_w

        def start(c, slot):
            off = pl.multiple_of(base + c * SC_CHUNK, SC_CHUNK)
            pltpu.sync_copy(idx_hbm.at[pl.ds(off, SC_CHUNK)], idx_v.at[slot])
            pltpu.async_copy(table_hbm.at[idx_v.at[slot]], rows_v.at[slot], sems.at[slot])

        def finish(c, slot):
            off = pl.multiple_of(base + c * SC_CHUNK, SC_CHUNK)
            pltpu.make_async_copy(table_hbm.at[idx_v.at[slot]], rows_v.at[slot], sems.at[slot]).wait()
            pltpu.sync_copy(rows_v.at[slot], out_hbm.at[pl.ds(off, SC_CHUNK)])

        start(0, 0)

        @pl.loop(0, n_chunk, step=2)
        def _(c):
            start(c + 1, 1)
            finish(c, 0)

            @pl.when(c + 2 < n_chunk)
            def _():
                start(c + 2, 0)

            finish(c + 1, 1)

    return gather(table, idx)


def _sc_scatter(src, pos3, n_rows):
    n, w = src.shape
    n_workers = SC_CORES * SC_SUBCORES
    per_w = n // n_workers
    assert n % (n_workers * SC_CHUNK) == 0
    mesh = plsc.VectorSubcoreMesh(core_axis_name="c", subcore_axis_name="s")

    n_chunk = per_w // SC_CHUNK
    assert n_chunk % 2 == 0

    @functools.partial(
        pl.kernel, mesh=mesh, out_type=jax.ShapeDtypeStruct((n_rows, w), src.dtype),
        scratch_types=[pltpu.VMEM((2, TOP_K, SC_CHUNK), jnp.int32), pltpu.VMEM((2, SC_CHUNK, w), src.dtype),
                       pltpu.SemaphoreType.DMA((2,)), pltpu.SemaphoreType.DMA((2,))])
    def scatter(src_hbm, pos_hbm, out_hbm, idx_v, rows_v, ld_sems, sc_sems):
        wid = lax.axis_index("s") * SC_CORES + lax.axis_index("c")
        base = wid * per_w

        def loads(c, slot):
            off = pl.multiple_of(base + c * SC_CHUNK, SC_CHUNK)
            return (pltpu.make_async_copy(src_hbm.at[pl.ds(off, SC_CHUNK)], rows_v.at[slot], ld_sems.at[slot]),
                    pltpu.make_async_copy(pos_hbm.at[off // SC_CHUNK], idx_v.at[slot], ld_sems.at[slot]))

        def scatter_chunk(slot):
            copies = [pltpu.async_copy(rows_v.at[slot], out_hbm.at[idx_v.at[slot, k]], sc_sems.at[slot])
                      for k in range(TOP_K)]
            for cp in copies:
                cp.wait()

        def half_step(c, slot):
            for cp in loads(c, slot):
                cp.wait()

            @pl.when(c + 1 < n_chunk)
            def _():
                for cp in loads(c + 1, 1 - slot):
                    cp.start()

            scatter_chunk(slot)

        for cp in loads(0, 0):
            cp.start()

        @pl.loop(0, n_chunk, step=2)
        def _(c):
            half_step(c, 0)
            half_step(c + 1, 1)

    return scatter(src, pos3)


def _experts_kernel(te_ref, nu_ref, xs_ref, wg_ref, wu_ref, wd_ref, ys_ref, wg_bf, wu_bf, wd_bf):
    i = pl.program_id(0)
    active = i < nu_ref[0]

    @pl.when(jnp.logical_and(active, jnp.logical_or(i == 0, te_ref[i] != te_ref[jnp.maximum(i - 1, 0)])))
    def _():
        wg_bf[...] = _bf(wg_ref[0])
        wu_bf[...] = _bf(wu_ref[0])
        wd_bf[...] = _bf(wd_ref[0])

    @pl.when(active)
    def _():
        x = _unpack_pairs(xs_ref[...])
        act = _glu(x, wg_bf[...], wu_bf[...])
        ys_ref[...] = _pack_pairs(jnp.dot(_bf(act), wd_bf[...], preferred_element_type=F32))


def _experts(xs, tile_expert, n_used, mp, layer):
    n_tiles = xs.shape[0] // MOE_TILE
    half = D_MODEL // 2
    wspec = lambda shape: pl.BlockSpec((None, 1) + shape, lambda i, te, nu: (layer, te[i], 0, 0))
    return pl.pallas_call(
        _experts_kernel,
        grid_spec=pltpu.PrefetchScalarGridSpec(
            num_scalar_prefetch=2, grid=(n_tiles,),
            in_specs=[pl.BlockSpec((MOE_TILE, half), lambda i, te, nu: (jnp.minimum(i, nu[0] - 1), 0)),
                      wspec((D_MODEL, D_EXPERT)), wspec((D_MODEL, D_EXPERT)), wspec((D_EXPERT, D_MODEL))],
            out_specs=pl.BlockSpec((MOE_TILE, half), lambda i, te, nu: (jnp.minimum(i, nu[0] - 1), 0)),
            scratch_shapes=[pltpu.VMEM((D_MODEL, D_EXPERT), BF16), pltpu.VMEM((D_MODEL, D_EXPERT), BF16),
                            pltpu.VMEM((D_EXPERT, D_MODEL), BF16)]),
        out_shape=jax.ShapeDtypeStruct(xs.shape, jnp.uint32),
        compiler_params=_cparams(1),
        name="experts",
    )(tile_expert, n_used, xs, mp['wg'], mp['wu'], mp['wd'])


def _combine_kernel(h_ref, *refs):
    yg_refs = refs[:TOP_K]
    ew_ref, eye_ref, sg_ref, su_ref, sd_ref, x1_ref, g2_ref, o_ref = refs[TOP_K:]
    act = _glu(h_ref[...], sg_ref[...], su_ref[...])
    acc = jnp.dot(_bf(act), sd_ref[...], preferred_element_type=F32)
    ew = ew_ref[...]
    hi = _bf(ew)
    lo = _bf(ew - hi.astype(F32))
    ew_t = _dot_tn(hi, eye_ref[...]) + _dot_tn(lo, eye_ref[...])
    for k in range(TOP_K):
        acc = acc + ew_t[:, k:k + 1] * _unpack_pairs(yg_refs[k][...]).astype(F32)
    o_ref[...] = x1_ref[...] + g2_ref[...] * acc


def _combine(rows, hffn, yg, ew, mp, x1, mod4, layer, row0, n_out):
    half = D_MODEL // 2
    n_blk = rows.n // TM
    blk0 = row0 // TM
    const = lambda a: pl.BlockSpec(a.shape, lambda i: (0,) * a.ndim)
    tok = lambda w: pl.BlockSpec((TM, w), lambda i: (blk0 + i, 0))
    slot = lambda k: pl.BlockSpec((TM, half), lambda i: (k * n_blk + blk0 + i, 0))
    eye = jnp.eye(TOP_K, dtype=BF16)
    return pl.pallas_call(
        _combine_kernel,
        grid=(n_out // TM,),
        in_specs=[tok(D_MODEL)] + [slot(k) for k in range(TOP_K)]
                 + [pl.BlockSpec((TOP_K, TM), lambda i: (0, blk0 + i)), const(eye),
                    const(mp['sg']), const(mp['su']), const(mp['sd']), tok(D_MODEL),
                    rows.mod_spec(layer, 5, TM, blk0)],
        out_specs=pl.BlockSpec((TM, D_MODEL), lambda i: (i, 0)),
        out_shape=jax.ShapeDtypeStruct((n_out, D_MODEL), F32),
        compiler_params=_cparams(1),
        name=f"combine{layer}",
    )(hffn, *([yg] * TOP_K), ew, eye, mp['sg'], mp['su'], mp['sd'], x1, mod4)


def _moe(rows, hffn, hpack, x1, router, bias, mp, mod4, layer, out_ranges):
    n = rows.n
    sel, eidx, ew, cnt = _router(hffn, router, bias)
    counts = cnt[:, 0].astype(jnp.int32)
    padded = (counts + MOE_TILE - 1) // MOE_TILE * MOE_TILE
    ends = jnp.cumsum(padded)
    n_rows = n * TOP_K + N_EXPERTS * MOE_TILE
    n_tiles = n_rows // MOE_TILE
    base = jnp.broadcast_to((ends - padded).astype(F32)[:, None], (N_EXPERTS, LANES))
    tile_start = jnp.arange(n_tiles, dtype=jnp.int32) * MOE_TILE
    tile_expert = jnp.minimum(jnp.sum((ends[None, :] <= tile_start[:, None]).astype(jnp.int32), axis=1),
                              N_EXPERTS - 1)
    n_used = (ends[-1:] // MOE_TILE).astype(jnp.int32)
    pos = _positions(sel, eidx, base)
    pos3 = pos.reshape(TOP_K, n // SC_CHUNK, SC_CHUNK).transpose(1, 0, 2)
    xs = _sc_scatter(hpack, pos3, n_rows)
    ys = _experts(xs, tile_expert, n_used, mp, layer)
    yg = _sc_gather(ys, pos.reshape(-1))
    return [_combine(rows, hffn, yg, ew, mp, x1, mod4, layer, row0, n_out) for row0, n_out in out_ranges]


def _glu(x, wg, wu):
    hg = jnp.dot(x, wg, preferred_element_type=F32)
    hu = jnp.dot(x, wu, preferred_element_type=F32)
    return _silu(hg) * hu


def kernel(x_prompt, x_sample, c, c_ctx, cache_attn_k, cache_attn_v, state_rwkv_fwd, state_rwkv_bwd,
           state_hgrn_fwd, state_hgrn_bwd, norm1_g, norm2_g, mod_w, mod_b, ab_w_in, ab_w_out, attn_q_norm,
           attn_k_norm, attn_sink, rwkv_mu, rwkv_w0, rwkv_w2, rwkv_a0, rwkv_a2, rwkv_g2, rwkv_k_k, rwkv_k_a,
           rwkv_r_k, rwkv_ln_w, rwkv_ln_b, hgrn_w_in, hgrn_w_out, hgrn_lower_bounds, hgrn_norm_g, moe_router,
           moe_bias, moe_w_gate, moe_w_up, moe_w_down, moe_shared_gate, moe_shared_up, moe_shared_down):
    n_cseq, cseq, _ = x_prompt.shape
    n_lseq, lseq, _ = x_sample.shape
    depth = mod_w.shape[0]
    assert depth == 2 and n_lseq + 1 <= SUBLANES
    assert cseq == PREP_TM and lseq % TM == 0 and lseq % HG_TB == 0 and cseq % HG_TB == 0
    assert (n_cseq * cseq) % TM == 0
    assert n_cseq % RW_BB == 0 and n_lseq % RW_BB == 0 and (n_cseq * cseq) % (lseq * RW_BB) == 0
    rows = _Rows(n_cseq * cseq, n_lseq * lseq, lseq)
    assert rows.n % MOE_TILE == 0 and lseq % MOE_TILE == 0 and rows.n_ctx % MOE_TILE == 0
    kv_w = KV_A * HD_A

    xs = (x_prompt.reshape(rows.n_ctx, D_MODEL), x_sample.reshape(rows.n_lat, D_MODEL))
    cvecs = jnp.concatenate([c_ctx[None, :], c, jnp.zeros((SUBLANES - 1 - n_lseq, D_MODEL), F32)], axis=0)
    mod4 = _modulation(cvecs, mod_w, mod_b).reshape(depth, SUBLANES, 1, 6 * D_MODEL)

    ones_q = _block_ones(W_A, HD_A)
    ones_k = _block_ones(kv_w, HD_A)
    ones_b = _block_ones(W_B, HD_B)
    ones_pair = _block_ones(LANES, HD_B)[:LANES]
    ones_pair = jnp.kron(jnp.eye(2, dtype=BF16), ones_pair)
    cos_t, sin_t = _rope_tables(lseq)

    def moe(l, hffn, hpack, x1, out_ranges):
        mp = {'wg': moe_w_gate, 'wu': moe_w_up, 'wd': moe_w_down,
              'sg': _bf(moe_shared_gate[l]), 'su': _bf(moe_shared_up[l]), 'sd': _bf(moe_shared_down[l])}
        return _moe(rows, hffn, hpack, x1, moe_router[l], moe_bias[l], mp, mod4, l, out_ranges)

    assert W_A == W_B
    all_rows = jnp.zeros((rows.n, W_B), F32)

    pr = {'mu': rwkv_mu[0], 'w0': rwkv_w0[0], 'w2': rwkv_w2[0], 'a0': rwkv_a0[0], 'a2': rwkv_a2[0],
          'g2': rwkv_g2[0], 'k_k': rwkv_k_k[0], 'k_a': rwkv_k_a[0], 'r_k': rwkv_r_k[0].reshape(-1),
          'ln_w': rwkv_ln_w[0], 'ln_b': rwkv_ln_b[0]}
    p_att, p_rw = _inproj(rows, xs, norm1_g[0], mod4, 0, _bf(ab_w_in[0]), (ATT_IN, RWKV_IN), TM)
    qg_t = jnp.tile(attn_q_norm[0], H_A).reshape(1, W_A)
    kg_t = jnp.tile(attn_k_norm[0], KV_A).reshape(1, kv_w)
    o_att, new_k, new_v = _ctx_attention(p_att, n_cseq, cseq, qg_t, kg_t, attn_sink[0], ones_q, ones_k, all_rows)
    past = cache_attn_k.shape[2]
    o_att = _lat_attention(p_att, rows.n_ctx // lseq, n_lseq, lseq, qg_t, kg_t, attn_sink[0], ones_q, ones_k,
                           cos_t, sin_t, cache_attn_k[:, 0].reshape(n_lseq, past, kv_w),
                           cache_attn_v[:, 0].reshape(n_lseq, past, kv_w), o_att)

    pp = _rwkv_prep(rows, p_rw, pr, ones_b)
    zero_st = jnp.zeros((n_cseq, H_B // 2, HD_B, LANES), F32)
    o_f, o_b, sf_c, sb_c = _rwkv_scan(pp, 0, n_cseq, cseq, zero_st, zero_st, ones_pair, all_rows, all_rows)
    o_f, o_b, _, _ = _rwkv_scan(pp, rows.n_ctx, n_lseq, lseq, _state_to_pairs(state_rwkv_fwd[:, 0]),
                                _state_to_pairs(state_rwkv_bwd[:, 0]), ones_pair, o_f, o_b)
    x1, hffn, hpack = _outproj0(rows, xs, o_att, o_f, o_b, pp, pr, ones_b, _bf(ab_w_out[0]), norm2_g[0], mod4, 0)
    (x,) = moe(0, hffn, hpack, x1, [(0, rows.n)])

    (p1,) = _inproj(rows, (x, x), norm1_g[1], mod4, 1, _bf(hgrn_w_in[0]), (IN_C,), IN1_TM)
    zero_h = jnp.zeros((n_cseq, H_C, DK_C, DV_C), F32)
    all_rows_c = jnp.zeros((rows.n, D_C), F32)
    h_f, h_b, hsf_c, hsb_c = _hgrn_scan(p1, 0, n_cseq, cseq, hgrn_lower_bounds, zero_h, zero_h,
                                        all_rows_c, all_rows_c)
    h_f, h_b, _, _ = _hgrn_scan(p1, rows.n_ctx, n_lseq, lseq, hgrn_lower_bounds,
                                state_hgrn_fwd[:, 0], state_hgrn_bwd[:, 0], h_f, h_b)
    x1, hffn, hpack = _outproj1(rows, (x, x), h_f, h_b, p1, hgrn_norm_g[0], _bf(hgrn_w_out[0]), norm2_g[1],
                                mod4, 1)
    y_c, y_l = moe(1, hffn, hpack, x1, [(0, rows.n_ctx), (rows.n_ctx, rows.n_lat)])

    y_prompt = y_c.reshape(n_cseq, cseq, D_MODEL)
    y_sample = y_l.reshape(n_lseq, lseq, D_MODEL)
    return (y_prompt, y_sample,
            new_k.reshape(n_cseq, 1, cseq, KV_A, HD_A), new_v.reshape(n_cseq, 1, cseq, KV_A, HD_A),
            _pairs_to_state(sf_c)[:, None], _pairs_to_state(sb_c)[:, None],
            hsf_c[:, None], hsb_c[:, None])
```

```python
import functools

import numpy as np
import jax
import jax.numpy as jnp
from jax import lax
from jax.experimental import pallas as pl
from jax.experimental.pallas import tpu as pltpu
from jax.experimental.pallas import tpu_sc as plsc

F32 = jnp.float32
BF16 = jnp.bfloat16

D_MODEL = 1024
GRID_W = 64
H_A = 8
KV_A = 2
G_A = H_A // KV_A
HD_A = 64
W_A = H_A * HD_A
WINDOW = 128
QBLK = 128
ROPE_BASE = 10000.0
ROPE_PAIR = HD_A // 4
ATTN_SCALE = HD_A ** -0.5
NEG_INF = -1e30
H_B = 8
HD_B = 64
W_B = H_B * HD_B
LORA_W = 64
LORA_A = 64
LORA_G = 128
GN_EPS = 64e-5
ATT_IN = W_A + 2 * KV_A * HD_A
RWKV_IN = 3 * W_B + LORA_W + LORA_A + LORA_G
IN_AB = ATT_IN + RWKV_IN
H_C = 8
DK_C = 128
DV_C = 128
D_C = H_C * DV_C
CHUNK = 64
IN_C = 5 * D_C
N_EXPERTS = 64
TOP_K = 8
N_GROUPS = 8
TOPK_GROUPS = 4
D_EXPERT = 256
ROUTED_SCALE = 2.5
EPS = 1e-6

LANES = 128
SUBLANES = 8
VMEM_LIMIT = 52 * 1024 * 1024

TM = 512
PREP_TM = 256
IN1_TM = 256
RW_TB = 128
RW_BB = 4
RW_GROUP_BB = 4
HG_TB = 256
MOE_TILE = 1024
POS_TB = 512
SC_CORES = 2
SC_SUBCORES = 16
SC_CHUNK = 64


def _cparams(n_axes):
    return pltpu.CompilerParams(dimension_semantics=("arbitrary",) * n_axes,
                                vmem_limit_bytes=VMEM_LIMIT)


def _bf(x):
    return x.astype(BF16)


def _split2(x):
    hi = lax.bitcast_convert_type(
        lax.bitcast_convert_type(x, jnp.uint32) & jnp.uint32(0xFFFF0000), F32)
    return hi, x - hi


def _seg_sum(x, ones2):
    hi, lo = _split2(x)
    return jnp.dot(jnp.concatenate([_bf(hi), _bf(lo)], axis=1), ones2,
                   preferred_element_type=F32)


def _dot_nt(a, b):
    return lax.dot_general(a, b, (((1,), (1,)), ((), ())), preferred_element_type=F32)


def _dot_tn(a, b):
    return lax.dot_general(a, b, (((0,), (0,)), ((), ())), preferred_element_type=F32)


def _sigmoid(x):
    return 1.0 / (1.0 + jnp.exp(-x))


def _silu(x):
    return x * _sigmoid(x)


def _chunks(seq, n):
    seq = list(seq)
    return [seq[i:i + n] for i in range(0, len(seq), n)]


def _block_ones(width, seg):
    idx = np.arange(width) // seg
    bd = (idx[:, None] == idx[None, :]).astype(np.float32)
    return jnp.asarray(np.concatenate([bd, bd], axis=0), dtype=BF16)


def _mod_kernel(c_ref, w_ref, b_ref, o_ref):
    s = _silu(c_ref[...])
    o_ref[0] = jnp.dot(_bf(s), _bf(w_ref[0]), preferred_element_type=F32) + b_ref[0]


def _modulation(cvecs, mod_w, mod_b):
    depth = mod_w.shape[0]
    n_col = 6 * D_MODEL // D_MODEL
    return pl.pallas_call(
        _mod_kernel,
        grid=(depth, n_col),
        in_specs=[pl.BlockSpec((SUBLANES, D_MODEL), lambda l, j: (0, 0)),
                  pl.BlockSpec((1, D_MODEL, D_MODEL), lambda l, j: (l, 0, j)),
                  pl.BlockSpec((1, 1, D_MODEL), lambda l, j: (l, 0, j))],
        out_specs=pl.BlockSpec((1, SUBLANES, D_MODEL), lambda l, j: (l, 0, j)),
        out_shape=jax.ShapeDtypeStruct((depth, SUBLANES, 6 * D_MODEL), F32),
        compiler_params=_cparams(2),
        name="modulation",
    )(cvecs, mod_w, mod_b.reshape(depth, 1, 6 * D_MODEL))


class _Rows:
    def __init__(self, n_ctx, n_lat, lat_seq):
        self.n_ctx, self.n_lat, self.lat_seq = n_ctx, n_lat, lat_seq
        self.n = n_ctx + n_lat

    def mod_row(self, i, tm):
        nctx_blk = self.n_ctx // tm
        per_seq = self.lat_seq // tm
        return jnp.where(i < nctx_blk, 0, 1 + (i - nctx_blk) // per_seq)

    def mod_spec(self, layer, chunk, tm, blk0=0):
        return pl.BlockSpec((None, None, 1, D_MODEL),
                            lambda i, *_: (layer, self.mod_row(i + blk0, tm), 0, chunk))


def _rms_mod(x, g, sc, sh):
    ms = jnp.mean(x * x, axis=-1, keepdims=True)
    return x * lax.rsqrt(ms + EPS) * g * (1.0 + sc) + sh


def _x_specs(rows, xs, tm=TM):
    xa, xb = xs
    nctx_blk = rows.n_ctx // tm
    lat0 = nctx_blk if xb.shape[0] == rows.n else 0
    return [pl.BlockSpec((tm, D_MODEL), lambda i: (jnp.minimum(i, nctx_blk - 1), 0)),
            pl.BlockSpec((tm, D_MODEL), lambda i: (jnp.maximum(i - nctx_blk, 0) + lat0, 0))]


def _pick_x(rows, xa_ref, xb_ref):
    return jnp.where(pl.program_id(0) < rows.n_ctx // xa_ref.shape[0], xa_ref[...], xb_ref[...])


def _inproj_kernel(rows, splits, xa_ref, xb_ref, g_ref, sh_ref, sc_ref, w_ref, *o_refs):
    h = _rms_mod(_pick_x(rows, xa_ref, xb_ref), g_ref[...], sc_ref[...], sh_ref[...])
    p = jnp.dot(_bf(h), w_ref[...], preferred_element_type=F32)
    lo = 0
    for o_ref, width in zip(o_refs, splits):
        o_ref[...] = p[:, lo:lo + width]
        lo += width


def _inproj(rows, xs, g, mod4, layer, w_bf, splits, tm):
    n_out = w_bf.shape[1]
    return pl.pallas_call(
        functools.partial(_inproj_kernel, rows, splits),
        grid=(rows.n // tm,),
        in_specs=_x_specs(rows, xs, tm) + [
            pl.BlockSpec((1, D_MODEL), lambda i: (0, 0)),
            rows.mod_spec(layer, 0, tm),
            rows.mod_spec(layer, 1, tm),
            pl.BlockSpec((D_MODEL, n_out), lambda i: (0, 0))],
        out_specs=[pl.BlockSpec((tm, wd), lambda i: (i, 0)) for wd in splits],
        out_shape=[jax.ShapeDtypeStruct((rows.n, wd), F32) for wd in splits],
        compiler_params=_cparams(1),
        name=f"inproj{layer}",
    )(*xs, g.reshape(1, D_MODEL), mod4, mod4, w_bf)


def _head_rms(x, gain_t, ones2):
    ms = _seg_sum(x * x, ones2) * (1.0 / HD_A)
    return x * lax.rsqrt(ms + EPS) * gain_t


def _sink_softmax_pv(parts, sink):
    m = jnp.maximum(functools.reduce(jnp.maximum, [jnp.max(s, axis=-1, keepdims=True) for s, _ in parts]), sink)
    den = jnp.exp(sink - m)
    acc = None
    for s, v in parts:
        p = jnp.exp(s - m)
        den = den + jnp.sum(p, axis=-1, keepdims=True)
        pv = jnp.dot(_bf(p), v, preferred_element_type=F32)
        acc = pv if acc is None else acc + pv
    return acc / den


def _ctx_attn_kernel(p_ref, qg_ref, kg_ref, sink_ref, ones_q_ref, ones_k_ref, prev_ref, o_ref, k_ref, v_ref):
    del prev_ref
    p = p_ref[...]
    q = _head_rms(p[:, :W_A], qg_ref[...], ones_q_ref[...]) * ATTN_SCALE
    k = _head_rms(p[:, W_A:W_A + KV_A * HD_A], kg_ref[...], ones_k_ref[...])
    v = p[:, W_A + KV_A * HD_A:ATT_IN]
    k_ref[0] = k
    v_ref[0] = v
    qb, kb, vb = _bf(q), _bf(k), _bf(v)
    outs = []
    for h in range(H_A):
        j = h // G_A
        s = _dot_nt(qb[:, h * HD_A:(h + 1) * HD_A], kb[:, j * HD_A:(j + 1) * HD_A])
        outs.append(_sink_softmax_pv([(s, vb[:, j * HD_A:(j + 1) * HD_A])], sink_ref[h]))
    o_ref[...] = jnp.concatenate(outs, axis=1)


def _ctx_attention(p_att, n_seq, seq, qg_t, kg_t, sink, ones_q, ones_k, prev):
    kv_w = KV_A * HD_A
    return pl.pallas_call(
        _ctx_attn_kernel,
        grid=(n_seq,),
        in_specs=[pl.BlockSpec((seq, ATT_IN), lambda b: (b, 0)),
                  pl.BlockSpec((1, W_A), lambda b: (0, 0)),
                  pl.BlockSpec((1, kv_w), lambda b: (0, 0)),
                  pl.BlockSpec(memory_space=pltpu.SMEM),
                  pl.BlockSpec(ones_q.shape, lambda b: (0, 0)),
                  pl.BlockSpec(ones_k.shape, lambda b: (0, 0)),
                  pl.BlockSpec(memory_space=pl.ANY)],
        out_specs=[pl.BlockSpec((seq, W_A), lambda b: (b, 0)),
                   pl.BlockSpec((1, seq, kv_w), lambda b: (b, 0, 0)),
                   pl.BlockSpec((1, seq, kv_w), lambda b: (b, 0, 0))],
        input_output_aliases={6: 0},
        out_shape=[jax.ShapeDtypeStruct(prev.shape, F32),
                   jax.ShapeDtypeStruct((n_seq, seq, kv_w), F32),
                   jax.ShapeDtypeStruct((n_seq, seq, kv_w), F32)],
        compiler_params=_cparams(1),
        name="ctx_attention",
    )(p_att, qg_t, kg_t, sink, ones_q, ones_k, prev)


def _rope(x, cos_t, sin_t):
    lane = lax.broadcasted_iota(jnp.int32, cos_t.shape, 1)
    low = (lane % (2 * ROPE_PAIR)) < ROPE_PAIR
    outs = []
    for s in range(x.shape[1] // LANES):
        xs = x[:, s * LANES:(s + 1) * LANES]
        partner = jnp.where(low, pltpu.roll(xs, LANES - ROPE_PAIR, 1), pltpu.roll(xs, ROPE_PAIR, 1))
        outs.append(xs * cos_t + partner * sin_t)
    return outs[0] if len(outs) == 1 else jnp.concatenate(outs, axis=1)


def _lat_attn_kernel(seq, p_ref, qg_ref, kg_ref, sink_ref, ones_q_ref, ones_k_ref, cos_ref, sin_ref,
                     kc_ref, vc_ref, prev_ref, o_ref, q_scr, k_scr, v_scr):
    del prev_ref
    kv_w = KV_A * HD_A
    p = p_ref[...]
    q = _head_rms(p[:, :W_A], qg_ref[...], ones_q_ref[...])
    k = _head_rms(p[:, W_A:W_A + kv_w], kg_ref[...], ones_k_ref[...])
    qr = _bf(_rope(q, cos_ref[...], sin_ref[...]) * ATTN_SCALE)
    kr = _bf(_rope(k, cos_ref[...], sin_ref[...]))
    vb = _bf(p[:, W_A + kv_w:ATT_IN])
    for h in range(H_A):
        q_scr[h] = qr[:, h * HD_A:(h + 1) * HD_A]
    for j in range(KV_A):
        k_scr[j] = kr[:, j * HD_A:(j + 1) * HD_A]
        v_scr[j] = vb[:, j * HD_A:(j + 1) * HD_A]
    kc = _bf(kc_ref[0])
    vc = _bf(vc_ref[0])
    n_local = 3 * QBLK
    grp = lax.broadcasted_iota(jnp.int32, (G_A * QBLK, 1), 0) // QBLK

    def block(i, carry):
        q0 = pl.multiple_of(i * QBLK, QBLK)
        start = pl.multiple_of(jnp.clip((i - 1) * QBLK, 0, seq - n_local), QBLK)
        ipos = q0 + lax.broadcasted_iota(jnp.int32, (G_A * QBLK, n_local), 0) % QBLK
        jpos = start + lax.broadcasted_iota(jnp.int32, (G_A * QBLK, n_local), 1)
        band = jnp.abs(jpos - ipos) <= WINDOW
        outs = []
        for j in range(KV_A):
            qs = jnp.concatenate([q_scr[j * G_A + g, pl.ds(q0, QBLK), :] for g in range(G_A)], axis=0)
            kl = k_scr[j, pl.ds(start, n_local), :]
            vl = v_scr[j, pl.ds(start, n_local), :]
            sink = jnp.zeros((G_A * QBLK, 1), F32)
            for g in range(G_A):
                sink = jnp.where(grp == g, sink_ref[j * G_A + g], sink)
            s_loc = jnp.where(band, _dot_nt(qs, kl), NEG_INF)
            s_ctx = _dot_nt(qs, kc[:, j * HD_A:(j + 1) * HD_A])
            o = _sink_softmax_pv([(s_loc, vl), (s_ctx, vc[:, j * HD_A:(j + 1) * HD_A])], sink)
            outs.extend(o[g * QBLK:(g + 1) * QBLK] for g in range(G_A))
        o_ref[pl.ds(q0, QBLK), :] = jnp.concatenate(outs, axis=1)
        return carry

    lax.fori_loop(0, seq // QBLK, block, 0)


def _lat_attention(p_att, row_blk0, n_seq, seq, qg_t, kg_t, sink, ones_q, ones_k, cos_t, sin_t, kc, vc, prev):
    kv_w = KV_A * HD_A
    past = kc.shape[1]
    return pl.pallas_call(
        functools.partial(_lat_attn_kernel, seq),
        grid=(n_seq,),
        in_specs=[pl.BlockSpec((seq, ATT_IN), lambda b: (row_blk0 + b, 0)),
                  pl.BlockSpec((1, W_A), lambda b: (0, 0)),
                  pl.BlockSpec((1, kv_w), lambda b: (0, 0)),
                  pl.BlockSpec(memory_space=pltpu.SMEM),
                  pl.BlockSpec(ones_q.shape, lambda b: (0, 0)),
                  pl.BlockSpec(ones_k.shape, lambda b: (0, 0)),
                  pl.BlockSpec((seq, LANES), lambda b: (0, 0)),
                  pl.BlockSpec((seq, LANES), lambda b: (0, 0)),
                  pl.BlockSpec((1, past, kv_w), lambda b: (b, 0, 0)),
                  pl.BlockSpec((1, past, kv_w), lambda b: (b, 0, 0)),
                  pl.BlockSpec(memory_space=pl.ANY)],
        out_specs=pl.BlockSpec((seq, W_A), lambda b: (row_blk0 + b, 0)),
        out_shape=jax.ShapeDtypeStruct(prev.shape, F32),
        input_output_aliases={10: 0},
        scratch_shapes=[pltpu.VMEM((H_A, seq, HD_A), BF16), pltpu.VMEM((KV_A, seq, HD_A), BF16),
                        pltpu.VMEM((KV_A, seq, HD_A), BF16)],
        compiler_params=_cparams(1),
        name="lat_attention",
    )(p_att, qg_t, kg_t, sink, ones_q, ones_k, cos_t, sin_t, kc, vc, prev)


def _rope_tables(seq):
    pos = np.arange(seq)
    row = (pos // GRID_W).astype(np.float32)
    col = (pos % GRID_W).astype(np.float32)
    d_axis = HD_A // 2
    inv = (ROPE_BASE ** (-np.arange(0, d_axis, 2, dtype=np.float32) / d_axis)).astype(np.float32)
    cos_h = np.zeros((seq, HD_A), np.float32)
    sin_h = np.zeros((seq, HD_A), np.float32)
    for seg, p_ in enumerate((row, col)):
        ang = (p_[:, None] * inv[None, :]).astype(np.float32)
        c, s = np.cos(ang), np.sin(ang)
        base = seg * d_axis
        cos_h[:, base:base + d_axis // 2] = c
        cos_h[:, base + d_axis // 2:base + d_axis] = c
        sin_h[:, base:base + d_axis // 2] = -s
        sin_h[:, base + d_axis // 2:base + d_axis] = s
    rep = LANES // HD_A
    return jnp.asarray(np.tile(cos_h, (1, rep))), jnp.asarray(np.tile(sin_h, (1, rep)))


def _rwkv_prep_kernel(rows, x_ref, prev_ref, next_ref, mu_ref, kk_ref, ka_ref, rk_ref, w0_ref, w2_ref,
                      a0_ref, a2_ref, g2_ref, ones_ref,
                      nkk_ref, r_ref, v_ref, g_ref, bonus_ref,
                      wf_ref, kaf_ref, kdf_ref, wb_ref, kab_ref, kdb_ref):
    i = pl.program_id(0)
    nctx_blk = rows.n_ctx // PREP_TM
    per_seq = rows.lat_seq // PREP_TM
    is_ctx = i < nctx_blk
    first = jnp.logical_or(is_ctx, (i - nctx_blk) % per_seq == 0)
    last = jnp.logical_or(is_ctx, (i - nctx_blk) % per_seq == per_seq - 1)
    x = x_ref[...]
    ridx = lax.broadcasted_iota(jnp.int32, x.shape, 0)
    prev_row = jnp.where(first, 0.0, prev_ref[SUBLANES - 1:SUBLANES, :])
    next_row = jnp.where(last, 0.0, next_ref[0:1, :])
    xm1 = jnp.where(ridx == 0, prev_row, pltpu.roll(x, 1, 0))
    xp1 = jnp.where(ridx == PREP_TM - 1, next_row, pltpu.roll(x, PREP_TM - 1, 0))
    pw = x + (0.5 * (xm1 + xp1) - x) * mu_ref[...]

    r = pw[:, 0:W_B]
    k = pw[:, W_B:2 * W_B]
    v = pw[:, 2 * W_B:3 * W_B]
    wd = pw[:, 3 * W_B:3 * W_B + LORA_W]
    ad = pw[:, 3 * W_B + LORA_W:3 * W_B + LORA_W + LORA_A]
    gd = pw[:, 3 * W_B + LORA_W + LORA_A:]
    ones2 = ones_ref[...]

    kk = k * kk_ref[...]
    kk = kk / jnp.maximum(jnp.sqrt(_seg_sum(kk * kk, ones2)), 1e-12)
    nkk_ref[...] = -kk
    r_ref[...] = r
    v_ref[...] = v
    g_ref[...] = jnp.dot(_bf(_sigmoid(gd)), g2_ref[...], preferred_element_type=F32)
    tw = _bf(jnp.tanh(wd))
    adb = _bf(ad)
    bonus = jnp.zeros_like(r)
    for d, (w_o, ka_o, kd_o) in enumerate(((wf_ref, kaf_ref, kdf_ref), (wb_ref, kab_ref, kdb_ref))):
        z = -(w0_ref[d:d + 1, :] + jnp.dot(tw, w2_ref[d], preferred_element_type=F32))
        softplus = jnp.maximum(z, 0.0) + jnp.log(1.0 + jnp.exp(-jnp.abs(z)))
        w_o[...] = jnp.exp(-jnp.exp(-softplus - 0.5))
        a = _sigmoid(a0_ref[d:d + 1, :] + jnp.dot(adb, a2_ref[d], preferred_element_type=F32))
        kd = k * (1.0 + (a - 1.0) * ka_ref[...])
        ka_o[...] = kk * a
        kd_o[...] = kd
        bonus = bonus + _seg_sum(r * kd * rk_ref[...], ones2) * v
    bonus_ref[...] = bonus


def _rwkv_prep(rows, p_rw, pr, ones_b):
    n = rows.n
    n_halo = n // SUBLANES
    blk_halo = PREP_TM // SUBLANES
    row = lambda a: a.reshape(1, -1)
    full = lambda a: pl.BlockSpec(a.shape, lambda i: (0,) * a.ndim)
    consts = [row(pr['mu']), row(pr['k_k']), row(pr['k_a']), row(pr['r_k']), pr['w0'], _bf(pr['w2']),
              pr['a0'], _bf(pr['a2']), _bf(pr['g2']), ones_b]
    outs = pl.pallas_call(
        functools.partial(_rwkv_prep_kernel, rows),
        grid=(n // PREP_TM,),
        in_specs=[pl.BlockSpec((PREP_TM, RWKV_IN), lambda i: (i, 0)),
                  pl.BlockSpec((SUBLANES, RWKV_IN), lambda i: (jnp.maximum(i * blk_halo - 1, 0), 0)),
                  pl.BlockSpec((SUBLANES, RWKV_IN), lambda i: (jnp.minimum((i + 1) * blk_halo, n_halo - 1), 0))]
                 + [full(a) for a in consts],
        out_specs=[pl.BlockSpec((PREP_TM, W_B), lambda i: (i, 0))] * 11,
        out_shape=[jax.ShapeDtypeStruct((n, W_B), F32)] * 11,
        compiler_params=_cparams(1),
        name="rwkv_prep",
    )(p_rw, p_rw, p_rw, *consts)
    names = ('nkk', 'r', 'v', 'g', 'bonus', 'w_f', 'ka_f', 'kd_f', 'w_b', 'ka_b', 'kd_b')
    return dict(zip(names, outs))


def _rwkv_scan_kernel(n_tb, nkkf_ref, rf_ref, vf_ref, wf_ref, kaf_ref, kdf_ref,
                      nkkb_ref, rb_ref, vb_ref, wb_ref, kab_ref, kdb_ref,
                      s0f_ref, s0b_ref, ones_ref, prevf_ref, prevb_ref,
                      of_ref, ob_ref, sff_ref, sfb_ref, sf_scr, sb_scr, vt_scr):
    del prevf_ref, prevb_ref
    s_scr = (sf_scr, sb_scr)
    tb = pl.program_id(1)
    n_pair = H_B // 2
    half = RW_TB // 2
    dirs = ((nkkf_ref, rf_ref, vf_ref, wf_ref, kaf_ref, kdf_ref, of_ref, False),
            (nkkb_ref, rb_ref, vb_ref, wb_ref, kab_ref, kdb_ref, ob_ref, True))

    @pl.when(tb == 0)
    def _():
        sf_scr[...] = s0f_ref[...]
        sb_scr[...] = s0b_ref[...]

    lane = lax.broadcasted_iota(jnp.int32, (HD_B, LANES), 1)
    for d, refs in enumerate(dirs):
        v_ref = refs[2]
        for bb in range(RW_BB):
            for p in range(n_pair):
                vt = v_ref[bb, :, p * LANES:(p + 1) * LANES].T
                top, bot = vt[:HD_B], vt[HD_B:]
                for s in range(2):
                    if s == 0:
                        t2 = jnp.where(lane < HD_B, top, pltpu.roll(bot, HD_B, 1))
                    else:
                        t2 = jnp.where(lane < HD_B, pltpu.roll(top, HD_B, 1), bot)
                    vt_scr[d, bb, p, s] = t2

    ones2 = ones_ref[...]
    row8 = lax.broadcasted_iota(jnp.int32, (SUBLANES, LANES), 0)
    lane8 = lax.broadcasted_iota(jnp.int32, (SUBLANES, LANES), 1)
    sel_r = jnp.logical_or(jnp.logical_and(row8 % 2 == 0, lane8 < HD_B),
                           jnp.logical_and(row8 % 2 == 1, lane8 >= HD_B))

    def row_of(rev, tt):
        return RW_TB - 1 - tt if rev else tt

    def emit_output(d, bb, tau):
        r_ref, o_ref = dirs[d][1], dirs[d][6]
        r = r_ref[bb, pl.ds(tau, 1), :]
        r8 = jnp.zeros((SUBLANES, LANES), F32)
        for p in range(n_pair):
            rp = jnp.broadcast_to(r[:, p * LANES:(p + 1) * LANES], (SUBLANES, LANES))
            r8 = jnp.where(jnp.logical_and(sel_r, row8 // 2 == p), rp, r8)
        s_all = jnp.concatenate([_bf(s_scr[d][bb, p]) for p in range(n_pair)], axis=0)
        o8 = _dot_nt(_bf(r8), s_all)
        o_parts = []
        for p in range(n_pair):
            for h in range(2):
                o_parts.append(o8[2 * p + h:2 * p + h + 1, p * HD_B:(p + 1) * HD_B])
        o_ref[bb, pl.ds(tau, 1), :] = jnp.concatenate(o_parts, axis=1)

    groups = [(d, bbs) for d in range(2) for bbs in _chunks(range(RW_BB), RW_GROUP_BB)]

    def reduce_phase(grp, tt):
        d, bbs = grp
        rev = dirs[d][7]
        tau = row_of(rev, tt)
        sub = tau // half
        lt = tau % half
        mask = jnp.logical_or(lane == lt, lane == lt + HD_B)
        lhs = []
        for bb in bbs:
            emit_output(d, bb, row_of(rev, jnp.maximum(tt - 1, 0)))
            nkk = dirs[d][0][bb, pl.ds(tau, 1), :]
            for p in range(n_pair):
                prod = s_scr[d][bb, p] * nkk[:, p * LANES:(p + 1) * LANES]
                lhs.append(jnp.concatenate([_bf(prod), _bf(jnp.where(mask, vt_scr[d, bb, p, sub], 0.0))],
                                           axis=1))
        return jnp.dot(jnp.concatenate(lhs, axis=0), ones2, preferred_element_type=F32)

    def update_phase(grp, tt, red):
        d, bbs = grp
        _, _, _, w_ref, ka_ref, kd_ref, _, rev = dirs[d]
        tau = row_of(rev, tt)
        for k, bb in enumerate(bbs):
            w = w_ref[bb, pl.ds(tau, 1), :]
            ka = ka_ref[bb, pl.ds(tau, 1), :]
            kd = kd_ref[bb, pl.ds(tau, 1), :]
            for p in range(n_pair):
                sl = slice(p * LANES, (p + 1) * LANES)
                r0 = (k * n_pair + p) * HD_B
                sa = red[r0:r0 + HD_B, :LANES]
                vcol = red[r0:r0 + HD_B, LANES:]
                s_scr[d][bb, p] = s_scr[d][bb, p] * w[:, sl] + sa * ka[:, sl] + vcol * kd[:, sl]

    def step(tt, carry):
        reds = [reduce_phase(g, tt) for g in groups]
        for g, red in zip(groups, reds):
            update_phase(g, tt, red)
        return carry

    lax.fori_loop(0, RW_TB, step, 0)
    for d in range(2):
        for bb in range(RW_BB):
            emit_output(d, bb, row_of(dirs[d][7], RW_TB - 1))

    @pl.when(tb == n_tb - 1)
    def _():
        sff_ref[...] = sf_scr[...]
        sfb_ref[...] = sb_scr[...]


def _rwkv_scan(pp, row0, n_seq, seq, s0_f, s0_b, ones_pair, prev_f, prev_b):
    n_tb = seq // RW_TB
    n_pair = H_B // 2
    blk0 = row0 // seq
    view = lambda a: a.reshape(a.shape[0] // seq, seq, W_B)
    fwd = pl.BlockSpec((RW_BB, RW_TB, W_B), lambda b, t: (blk0 // RW_BB + b, t, 0))
    bwd = pl.BlockSpec((RW_BB, RW_TB, W_B), lambda b, t: (blk0 // RW_BB + b, n_tb - 1 - t, 0))
    st = pl.BlockSpec((RW_BB, n_pair, HD_B, LANES), lambda b, t: (b, 0, 0, 0))
    ins_f = [view(pp[k]) for k in ('nkk', 'r', 'v', 'w_f', 'ka_f', 'kd_f')]
    ins_b = [view(pp[k]) for k in ('nkk', 'r', 'v', 'w_b', 'ka_b', 'kd_b')]
    st_shape = jax.ShapeDtypeStruct((n_seq, n_pair, HD_B, LANES), F32)
    o_shape = jax.ShapeDtypeStruct(view(prev_f).shape, F32)
    any_spec = pl.BlockSpec(memory_space=pl.ANY)
    o_f, o_b, sf, sb = pl.pallas_call(
        functools.partial(_rwkv_scan_kernel, n_tb),
        grid=(n_seq // RW_BB, n_tb),
        in_specs=[fwd] * 6 + [bwd] * 6 + [st, st, pl.BlockSpec(ones_pair.shape, lambda b, t: (0, 0)),
                                           any_spec, any_spec],
        out_specs=[fwd, bwd, st, st],
        out_shape=[o_shape, o_shape, st_shape, st_shape],
        input_output_aliases={15: 0, 16: 1},
        scratch_shapes=[pltpu.VMEM((RW_BB, n_pair, HD_B, LANES), F32),
                        pltpu.VMEM((RW_BB, n_pair, HD_B, LANES), F32),
                        pltpu.VMEM((2, RW_BB, n_pair, 2, HD_B, LANES), F32)],
        compiler_params=_cparams(2),
        name="rwkv_scan",
    )(*ins_f, *ins_b, s0_f, s0_b, ones_pair, view(prev_f), view(prev_b))
    return o_f.reshape(prev_f.shape), o_b.reshape(prev_b.shape), sf, sb


def _state_to_pairs(s):
    b = s.shape[0]
    return s.reshape(b, H_B // 2, 2, HD_B, HD_B).transpose(0, 1, 3, 2, 4).reshape(b, H_B // 2, HD_B, 2 * HD_B)


def _pairs_to_state(s):
    b = s.shape[0]
    return s.reshape(b, H_B // 2, HD_B, 2, HD_B).transpose(0, 1, 3, 2, 4).reshape(b, H_B, HD_B, HD_B)


def _tail(x, y, g1, n2g, sc2, sh2, x1_ref, h_ref, hp_ref):
    x1 = x + g1 * y
    x1_ref[...] = x1
    h = _rms_mod(x1, n2g, sc2, sh2)
    h_ref[...] = _bf(h)
    hp_ref[...] = _pack_pairs(h)


def _outproj0_kernel(rows, xa_ref, xb_ref, oa_ref, of_ref, ob_ref, bonus_ref, g_ref, lnw_ref, lnb_ref, ones_ref,
                     w_ref, g1_ref, n2g_ref, sc2_ref, sh2_ref, x1_ref, h_ref, hp_ref):
    o_sum = of_ref[...] + ob_ref[...]
    ones2 = ones_ref[...]
    mean = _seg_sum(o_sum, ones2) * (1.0 / HD_B)
    cen = o_sum - mean
    var = _seg_sum(cen * cen, ones2) * (1.0 / HD_B)
    gn = cen * lax.rsqrt(var + GN_EPS) * lnw_ref[...] + lnb_ref[...]
    o_rw = (gn + bonus_ref[...]) * g_ref[...]
    mix = jnp.concatenate([_bf(oa_ref[...]), _bf(o_rw)], axis=1)
    y = jnp.dot(mix, w_ref[...], preferred_element_type=F32)
    _tail(_pick_x(rows, xa_ref, xb_ref), y, g1_ref[...], n2g_ref[...], sc2_ref[...], sh2_ref[...], x1_ref, h_ref, hp_ref)


def _outproj0(rows, xs, o_att, o_f, o_b, pp, pr, ones_b, w_out_bf, n2g, mod4, layer):
    n = rows.n
    tok = lambda w: pl.BlockSpec((TM, w), lambda i: (i, 0))
    const = lambda a: pl.BlockSpec(a.shape, lambda i: (0,) * a.ndim)
    lnw, lnb, n2 = pr['ln_w'].reshape(1, -1), pr['ln_b'].reshape(1, -1), n2g.reshape(1, -1)
    return pl.pallas_call(
        functools.partial(_outproj0_kernel, rows),
        grid=(n // TM,),
        in_specs=_x_specs(rows, xs) + [tok(W_A), tok(W_B), tok(W_B), tok(W_B), tok(W_B),
                  const(lnw), const(lnb), const(ones_b), const(w_out_bf),
                  rows.mod_spec(layer, 2, TM), const(n2), rows.mod_spec(layer, 4, TM), rows.mod_spec(layer, 3, TM)],
        out_specs=[tok(D_MODEL), tok(D_MODEL), tok(D_MODEL // 2)],
        out_shape=[jax.ShapeDtypeStruct((n, D_MODEL), F32), jax.ShapeDtypeStruct((n, D_MODEL), BF16),
                   jax.ShapeDtypeStruct((n, D_MODEL // 2), jnp.uint32)],
        compiler_params=_cparams(1),
        name="outproj0",
    )(*xs, o_att, o_f, o_b, pp['bonus'], pp['g'], lnw, lnb, ones_b, w_out_bf, mod4, n2, mod4, mod4)


def _outproj1_kernel(rows, xa_ref, xb_ref, of_ref, ob_ref, gate_ref, ng_ref, w_ref, g1_ref, n2g_ref, sc2_ref, sh2_ref,
                     x1_ref, h_ref, hp_ref):
    o_sum = of_ref[...] + ob_ref[...]
    parts = []
    for h in range(H_C):
        oh = o_sum[:, h * DV_C:(h + 1) * DV_C]
        parts.append(oh * lax.rsqrt(jnp.mean(oh * oh, axis=-1, keepdims=True) + EPS))
    o = jnp.concatenate(parts, axis=1) * ng_ref[...] * _silu(gate_ref[...])
    y = jnp.dot(_bf(o), w_ref[...], preferred_element_type=F32)
    _tail(_pick_x(rows, xa_ref, xb_ref), y, g1_ref[...], n2g_ref[...], sc2_ref[...], sh2_ref[...], x1_ref, h_ref, hp_ref)


def _outproj1(rows, xs, o_f, o_b, p1, norm_g, w_out_bf, n2g, mod4, layer):
    n = rows.n
    tok = lambda w: pl.BlockSpec((TM, w), lambda i: (i, 0))
    const = lambda a: pl.BlockSpec(a.shape, lambda i: (0,) * a.ndim)
    ng, n2 = norm_g.reshape(1, -1), n2g.reshape(1, -1)
    return pl.pallas_call(
        functools.partial(_outproj1_kernel, rows),
        grid=(n // TM,),
        in_specs=_x_specs(rows, xs) + [tok(D_C), tok(D_C), pl.BlockSpec((TM, D_C), lambda i: (i, 4)),
                  const(ng), const(w_out_bf),
                  rows.mod_spec(layer, 2, TM), const(n2), rows.mod_spec(layer, 4, TM), rows.mod_spec(layer, 3, TM)],
        out_specs=[tok(D_MODEL), tok(D_MODEL), tok(D_MODEL // 2)],
        out_shape=[jax.ShapeDtypeStruct((n, D_MODEL), F32), jax.ShapeDtypeStruct((n, D_MODEL), BF16),
                   jax.ShapeDtypeStruct((n, D_MODEL // 2), jnp.uint32)],
        compiler_params=_cparams(1),
        name="outproj1",
    )(*xs, o_f, o_b, p1, ng, w_out_bf, mod4, n2, mod4, mod4)


def _hgrn_kernel(n_tb, qf_ref, ff_ref, if_ref, qb_ref, fb_ref, ib_ref, lbp_ref, s0f_ref, s0b_ref,
                 trif_ref, trib_ref, prevf_ref, prevb_ref, of_ref, ob_ref, sff_ref, sfb_ref, s_scr):
    del prevf_ref, prevb_ref
    tb = pl.program_id(1)

    @pl.when(tb == 0)
    def _():
        for h in range(H_C):
            s_scr[0, h] = s0f_ref[0, h].T
            s_scr[1, h] = s0b_ref[0, h].T

    lbp = lbp_ref[...]
    e = jnp.exp(lbp - jnp.max(lbp, axis=0, keepdims=True))
    sm = e / jnp.sum(e, axis=0, keepdims=True)
    lb = (sm[0:1] + sm[1:2]) - sm[0:1]

    n_chunk = HG_TB // CHUNK
    ti = lax.broadcasted_iota(jnp.int32, (HG_TB, HG_TB), 0)
    si = lax.broadcasted_iota(jnp.int32, (HG_TB, HG_TB), 1)
    same = (ti // CHUNK) == (si // CHUNK)
    dirs = ((qf_ref, ff_ref, if_ref, of_ref, trif_ref, jnp.logical_and(same, ti >= si), CHUNK - 1, False),
            (qb_ref, fb_ref, ib_ref, ob_ref, trib_ref, jnp.logical_and(same, ti <= si), 0, True))

    staged = []
    for d, (q_ref, f_ref, i_ref, o_ref, tri_ref, causal, last_row, rev) in enumerate(dirs):
        q = _silu(q_ref[...])
        f = lb + (1.0 - lb) * _sigmoid(f_ref[...])
        k = 1.0 - f
        v = _bf(i_ref[...])
        g = jnp.log(f)
        g1 = _bf(g)
        g2 = _bf(g - g1.astype(F32))
        tri2 = tri_ref[...]
        b_parts, last_parts, dec = [], [], []
        for c in range(n_chunk):
            rc = slice(c * CHUNK, (c + 1) * CHUNK)
            bc = jnp.dot(tri2, jnp.concatenate([g1[rc], g2[rc]], axis=0), preferred_element_type=F32)
            b_parts.append(bc)
            last = bc[last_row:last_row + 1]
            last_parts.append(jnp.broadcast_to(last, bc.shape))
            dec.append(jnp.exp(last))
        b = jnp.concatenate(b_parts, axis=0)
        b_last = jnp.concatenate(last_parts, axis=0)
        staged.append((_bf(q * jnp.exp(b)), _bf(k * jnp.exp(-b)), _bf(k * jnp.exp(b_last - b)), v, dec))

    for h in range(H_C):
        sl = slice(h * DK_C, (h + 1) * DK_C)
        for d, (q_ref, f_ref, i_ref, o_ref, tri_ref, causal, last_row, rev) in enumerate(dirs):
            q_in, k_in, k_out, v, dec = staged[d]
            qh, vh = q_in[:, sl], v[:, sl]
            att = jnp.where(causal, _dot_nt(qh, k_in[:, sl]), 0.0)
            o_intra = jnp.dot(_bf(att), vh, preferred_element_type=F32)
            s_t = s_scr[d, h]
            for c in (range(n_chunk - 1, -1, -1) if rev else range(n_chunk)):
                rc = slice(c * CHUNK, (c + 1) * CHUNK)
                o_ref[rc, sl] = o_intra[rc] + _dot_nt(qh[rc], _bf(s_t))
                s_t = dec[c][:, sl] * s_t + _dot_tn(vh[rc], k_out[rc, sl])
            s_scr[d, h] = s_t

    @pl.when(tb == n_tb - 1)
    def _():
        for h in range(H_C):
            sff_ref[0, h] = s_scr[0, h].T
            sfb_ref[0, h] = s_scr[1, h].T


def _hgrn_scan(p1, row0, n_seq, seq, lb_params, s0_f, s0_b, prev_f, prev_b):
    n_tb = seq // HG_TB
    blk0 = row0 // HG_TB
    tri = np.tril(np.ones((CHUNK, CHUNK), np.float32))
    tri_f = jnp.asarray(np.concatenate([tri] * 2, axis=1), dtype=BF16)
    tri_b = jnp.asarray(np.concatenate([tri.T] * 2, axis=1), dtype=BF16)
    fwd = lambda col: pl.BlockSpec((HG_TB, D_C), lambda b, t: (blk0 + b * n_tb + t, col))
    bwd = lambda col: pl.BlockSpec((HG_TB, D_C), lambda b, t: (blk0 + b * n_tb + n_tb - 1 - t, col))
    st = pl.BlockSpec((1, H_C, DK_C, DV_C), lambda b, t: (b, 0, 0, 0))
    const = lambda a: pl.BlockSpec(a.shape, lambda b, t: (0,) * a.ndim)
    o_shape = jax.ShapeDtypeStruct(prev_f.shape, F32)
    st_shape = jax.ShapeDtypeStruct((n_seq, H_C, DK_C, DV_C), F32)
    any_spec = pl.BlockSpec(memory_space=pl.ANY)
    return pl.pallas_call(
        functools.partial(_hgrn_kernel, n_tb),
        grid=(n_seq, n_tb),
        in_specs=[fwd(0), fwd(1), fwd(3), bwd(0), bwd(2), bwd(3), const(lb_params), st, st,
                  const(tri_f), const(tri_b), any_spec, any_spec],
        out_specs=[fwd(0), bwd(0), st, st],
        out_shape=[o_shape, o_shape, st_shape, st_shape],
        input_output_aliases={11: 0, 12: 1},
        scratch_shapes=[pltpu.VMEM((2, H_C, DV_C, DK_C), F32)],
        compiler_params=_cparams(2),
        name="hgrn_scan",
    )(p1, p1, p1, p1, p1, p1, lb_params, s0_f, s0_b, tri_f, tri_b, prev_f, prev_b)


def _router_kernel(h_ref, rhi_ref, rlo_ref, bias_ref, sel_ref, eidx_ref, ew_ref, cnt_ref):
    x = h_ref[...]
    tm = x.shape[0]
    logits = _dot_nt(rhi_ref[...], x) + _dot_nt(rlo_ref[...], x)
    scores = _sigmoid(logits)
    biased = scores + bias_ref[...]
    per = N_EXPERTS // N_GROUPS
    sub = lax.broadcasted_iota(jnp.int32, (per, tm), 0)
    gs_rows = []
    for g in range(N_GROUPS):
        blk = biased[g * per:(g + 1) * per]
        m1 = jnp.max(blk, axis=0, keepdims=True)
        first = jnp.min(jnp.where(blk == m1, sub, per), axis=0, keepdims=True)
        m2 = jnp.max(jnp.where(sub == first, -jnp.inf, blk), axis=0, keepdims=True)
        gs_rows.append(m1 + m2)
    gs = jnp.concatenate(gs_rows, axis=0)
    gi = lax.broadcasted_iota(jnp.int32, gs.shape, 0)
    rank = jnp.zeros(gs.shape, jnp.int32)
    for s in range(1, N_GROUPS):
        other = pltpu.roll(gs, s, 0)
        oi = pltpu.roll(gi, s, 0)
        beats = jnp.logical_or(other > gs, jnp.logical_and(other == gs, oi < gi))
        rank = rank + jnp.where(beats, 1, 0)
    keep = jnp.where(rank < TOPK_GROUPS, 1.0, 0.0)
    emask = jnp.concatenate([jnp.broadcast_to(keep[g:g + 1], (per, tm)) for g in range(N_GROUPS)], axis=0)
    cur = jnp.where(emask > 0.0, biased, -jnp.inf)
    ei = lax.broadcasted_iota(jnp.int32, cur.shape, 0)
    sel = jnp.zeros(cur.shape, F32)
    idxs, vals = [], []
    for _ in range(TOP_K):
        m = jnp.max(cur, axis=0, keepdims=True)
        idx = jnp.min(jnp.where(cur == m, ei, N_EXPERTS), axis=0, keepdims=True)
        pick = ei == idx
        idxs.append(idx)
        vals.append(jnp.sum(jnp.where(pick, scores, 0.0), axis=0, keepdims=True))
        sel = jnp.where(pick, 1.0, sel)
        cur = jnp.where(pick, -jnp.inf, cur)
    w = jnp.concatenate(vals, axis=0)
    eidx_ref[...] = jnp.concatenate(idxs, axis=0)
    ew_ref[...] = w / jnp.sum(w, axis=0, keepdims=True) * ROUTED_SCALE
    sel_ref[...] = _bf(sel)

    @pl.when(pl.program_id(0) == 0)
    def _():
        cnt_ref[...] = jnp.zeros_like(cnt_ref)

    cnt_ref[...] += jnp.sum(sel, axis=1, keepdims=True)


def _router(hffn, router, bias):
    n = hffn.shape[0]
    r_t = router.T
    r_hi = _bf(r_t)
    r_lo = _bf(r_t - r_hi.astype(F32))
    const = lambda a: pl.BlockSpec(a.shape, lambda i: (0,) * a.ndim)
    b_col = bias.reshape(N_EXPERTS, 1)
    return pl.pallas_call(
        _router_kernel,
        grid=(n // TM,),
        in_specs=[pl.BlockSpec((TM, D_MODEL), lambda i: (i, 0)), const(r_hi), const(r_lo), const(b_col)],
        out_specs=[pl.BlockSpec((N_EXPERTS, TM), lambda i: (0, i)),
                   pl.BlockSpec((TOP_K, TM), lambda i: (0, i)),
                   pl.BlockSpec((TOP_K, TM), lambda i: (0, i)),
                   pl.BlockSpec((N_EXPERTS, LANES), lambda i: (0, 0))],
        out_shape=[jax.ShapeDtypeStruct((N_EXPERTS, n), BF16),
                   jax.ShapeDtypeStruct((TOP_K, n), jnp.int32),
                   jax.ShapeDtypeStruct((TOP_K, n), F32),
                   jax.ShapeDtypeStruct((N_EXPERTS, LANES), F32)],
        compiler_params=_cparams(1),
        name="router",
    )(hffn, r_hi, r_lo, b_col)


def _positions_kernel(sel_ref, eidx_ref, base_ref, upper_ref, pos_ref, carry_ref):
    @pl.when(pl.program_id(0) == 0)
    def _():
        carry_ref[...] = jnp.zeros_like(carry_ref)

    sel = sel_ref[...]
    rank = jnp.dot(sel, upper_ref[...], preferred_element_type=F32)
    pos_e = base_ref[:, 0:1] + carry_ref[:, 0:1] + rank
    ei = lax.broadcasted_iota(jnp.int32, pos_e.shape, 0)
    eidx = eidx_ref[...]
    rows = [jnp.sum(jnp.where(ei == eidx[k:k + 1], pos_e, 0.0), axis=0, keepdims=True) for k in range(TOP_K)]
    pos_ref[...] = jnp.concatenate(rows, axis=0).astype(jnp.int32)
    carry_ref[...] += jnp.sum(sel.astype(F32), axis=1, keepdims=True)


def _positions(sel, eidx, base):
    n = sel.shape[1]
    pb = POS_TB
    upper = jnp.asarray(np.triu(np.ones((pb, pb), np.float32), 1), dtype=BF16)
    return pl.pallas_call(
        _positions_kernel,
        grid=(n // pb,),
        in_specs=[pl.BlockSpec((N_EXPERTS, pb), lambda i: (0, i)),
                  pl.BlockSpec((TOP_K, pb), lambda i: (0, i)),
                  pl.BlockSpec((N_EXPERTS, LANES), lambda i: (0, 0)),
                  pl.BlockSpec((pb, pb), lambda i: (0, 0))],
        out_specs=pl.BlockSpec((TOP_K, pb), lambda i: (0, i)),
        out_shape=jax.ShapeDtypeStruct((TOP_K, n), jnp.int32),
        scratch_shapes=[pltpu.VMEM((N_EXPERTS, LANES), F32)],
        compiler_params=_cparams(1),
        name="positions",
    )(sel, eidx, base, upper)


def _pack_pairs(x):
    half = x.shape[1] // 2
    bits = lax.bitcast_convert_type(_bf(x).astype(F32), jnp.uint32)
    return (bits[:, :half] >> 16) | (bits[:, half:] & jnp.uint32(0xFFFF0000))


def _unpack_pairs(w):
    lo = lax.bitcast_convert_type(w << 16, F32)
    hi = lax.bitcast_convert_type(w & jnp.uint32(0xFFFF0000), F32)
    return jnp.concatenate([_bf(lo), _bf(hi)], axis=1)


def _sc_gather(table, idx):
    b, w = idx.shape[0], table.shape[1]
    n_workers = SC_CORES * SC_SUBCORES
    per_w = b // n_workers
    assert b % (n_workers * SC_CHUNK) == 0
    mesh = plsc.VectorSubcoreMesh(core_axis_name="c", subcore_axis_name="s")

    n_chunk = per_w // SC_CHUNK
    assert n_chunk % 2 == 0

    @functools.partial(
        pl.kernel, mesh=mesh, out_type=jax.ShapeDtypeStruct((b, w), table.dtype),
        scratch_types=[pltpu.VMEM((2, SC_CHUNK), jnp.int32), pltpu.VMEM((2, SC_CHUNK, w), table.dtype),
                       pltpu.SemaphoreType.DMA((2,))])
    def gather(table_hbm, idx_hbm, out_hbm, idx_v, rows_v, sems):
        wid = lax.axis_index("s") * SC_CORES + lax.axis_index("c")
        base = wid * per_w

        def start(c, slot):
            off = pl.multiple_of(base + c * SC_CHUNK, SC_CHUNK)
            pltpu.sync_copy(idx_hbm.at[pl.ds(off, SC_CHUNK)], idx_v.at[slot])
            pltpu.async_copy(table_hbm.at[idx_v.at[slot]], rows_v.at[slot], sems.at[slot])

        def finish(c, slot):
            off = pl.multiple_of(base + c * SC_CHUNK, SC_CHUNK)
            pltpu.make_async_copy(table_hbm.at[idx_v.at[slot]], rows_v.at[slot], sems.at[slot]).wait()
            pltpu.sync_copy(rows_v.at[slot], out_hbm.at[pl.ds(off, SC_CHUNK)])

        start(0, 0)

        @pl.loop(0, n_chunk, step=2)
        def _(c):
            start(c + 1, 1)
            finish(c, 0)

            @pl.when(c + 2 < n_chunk)
            def _():
                start(c + 2, 0)

            finish(c + 1, 1)

    return gather(table, idx)


def _sc_scatter(src, pos3, n_rows):
    n, w = src.shape
    n_workers = SC_CORES * SC_SUBCORES
    per_w = n // n_workers
    assert n % (n_workers * SC_CHUNK) == 0
    mesh = plsc.VectorSubcoreMesh(core_axis_name="c", subcore_axis_name="s")

    n_chunk = per_w // SC_CHUNK
    assert n_chunk % 2 == 0

    @functools.partial(
        pl.kernel, mesh=mesh, out_type=jax.ShapeDtypeStruct((n_rows, w), src.dtype),
        scratch_types=[pltpu.VMEM((2, TOP_K, SC_CHUNK), jnp.int32), pltpu.VMEM((2, SC_CHUNK, w), src.dtype),
                       pltpu.SemaphoreType.DMA((2,)), pltpu.SemaphoreType.DMA((2,))])
    def scatter(src_hbm, pos_hbm, out_hbm, idx_v, rows_v, ld_sems, sc_sems):
        wid = lax.axis_index("s") * SC_CORES + lax.axis_index("c")
        base = wid * per_w

        def loads(c, slot):
            off = pl.multiple_of(base + c * SC_CHUNK, SC_CHUNK)
            return (pltpu.make_async_copy(src_hbm.at[pl.ds(off, SC_CHUNK)], rows_v.at[slot], ld_sems.at[slot]),
                    pltpu.make_async_copy(pos_hbm.at[off // SC_CHUNK], idx_v.at[slot], ld_sems.at[slot]))

        def scatter_chunk(slot):
            copies = [pltpu.async_copy(rows_v.at[slot], out_hbm.at[idx_v.at[slot, k]], sc_sems.at[slot])
                      for k in range(TOP_K)]
            for cp in copies:
                cp.wait()

        def half_step(c, slot):
            for cp in loads(c, slot):
                cp.wait()

            @pl.when(c + 1 < n_chunk)
            def _():
                for cp in loads(c + 1, 1 - slot):
                    cp.start()

            scatter_chunk(slot)

        for cp in loads(0, 0):
            cp.start()

        @pl.loop(0, n_chunk, step=2)
        def _(c):
            half_step(c, 0)
            half_step(c + 1, 1)

    return scatter(src, pos3)


def _experts_kernel(te_ref, nu_ref, xs_ref, wg_ref, wu_ref, wd_ref, ys_ref, wg_bf, wu_bf, wd_bf):
    i = pl.program_id(0)
    active = i < nu_ref[0]

    @pl.when(jnp.logical_and(active, jnp.logical_or(i == 0, te_ref[i] != te_ref[jnp.maximum(i - 1, 0)])))
    def _():
        wg_bf[...] = _bf(wg_ref[0])
        wu_bf[...] = _bf(wu_ref[0])
        wd_bf[...] = _bf(wd_ref[0])

    @pl.when(active)
    def _():
        x = _unpack_pairs(xs_ref[...])
        act = _glu(x, wg_bf[...], wu_bf[...])
        ys_ref[...] = _pack_pairs(jnp.dot(_bf(act), wd_bf[...], preferred_element_type=F32))


def _experts(xs, tile_expert, n_used, mp, layer):
    n_tiles = xs.shape[0] // MOE_TILE
    half = D_MODEL // 2
    wspec = lambda shape: pl.BlockSpec((None, 1) + shape, lambda i, te, nu: (layer, te[i], 0, 0))
    return pl.pallas_call(
        _experts_kernel,
        grid_spec=pltpu.PrefetchScalarGridSpec(
            num_scalar_prefetch=2, grid=(n_tiles,),
            in_specs=[pl.BlockSpec((MOE_TILE, half), lambda i, te, nu: (jnp.minimum(i, nu[0] - 1), 0)),
                      wspec((D_MODEL, D_EXPERT)), wspec((D_MODEL, D_EXPERT)), wspec((D_EXPERT, D_MODEL))],
            out_specs=pl.BlockSpec((MOE_TILE, half), lambda i, te, nu: (jnp.minimum(i, nu[0] - 1), 0)),
            scratch_shapes=[pltpu.VMEM((D_MODEL, D_EXPERT), BF16), pltpu.VMEM((D_MODEL, D_EXPERT), BF16),
                            pltpu.VMEM((D_EXPERT, D_MODEL), BF16)]),
        out_shape=jax.ShapeDtypeStruct(xs.shape, jnp.uint32),
        compiler_params=_cparams(1),
        name="experts",
    )(tile_expert, n_used, xs, mp['wg'], mp['wu'], mp['wd'])


def _combine_kernel(h_ref, *refs):
    yg_refs = refs[:TOP_K]
    ew_ref, eye_ref, sg_ref, su_ref, sd_ref, x1_ref, g2_ref, o_ref = refs[TOP_K:]
    act = _glu(h_ref[...], sg_ref[...], su_ref[...])
    acc = jnp.dot(_bf(act), sd_ref[...], preferred_element_type=F32)
    ew = ew_ref[...]
    hi = _bf(ew)
    lo = _bf(ew - hi.astype(F32))
    ew_t = _dot_tn(hi, eye_ref[...]) + _dot_tn(lo, eye_ref[...])
    for k in range(TOP_K):
        acc = acc + ew_t[:, k:k + 1] * _unpack_pairs(yg_refs[k][...]).astype(F32)
    o_ref[...] = x1_ref[...] + g2_ref[...] * acc


def _combine(rows, hffn, yg, ew, mp, x1, mod4, layer, row0, n_out):
    half = D_MODEL // 2
    n_blk = rows.n // TM
    blk0 = row0 // TM
    const = lambda a: pl.BlockSpec(a.shape, lambda i: (0,) * a.ndim)
    tok = lambda w: pl.BlockSpec((TM, w), lambda i: (blk0 + i, 0))
    slot = lambda k: pl.BlockSpec((TM, half), lambda i: (k * n_blk + blk0 + i, 0))
    eye = jnp.eye(TOP_K, dtype=BF16)
    return pl.pallas_call(
        _combine_kernel,
        grid=(n_out // TM,),
        in_specs=[tok(D_MODEL)] + [slot(k) for k in range(TOP_K)]
                 + [pl.BlockSpec((TOP_K, TM), lambda i: (0, blk0 + i)), const(eye),
                    const(mp['sg']), const(mp['su']), const(mp['sd']), tok(D_MODEL),
                    rows.mod_spec(layer, 5, TM, blk0)],
        out_specs=pl.BlockSpec((TM, D_MODEL), lambda i: (i, 0)),
        out_shape=jax.ShapeDtypeStruct((n_out, D_MODEL), F32),
        compiler_params=_cparams(1),
        name=f"combine{layer}",
    )(hffn, *([yg] * TOP_K), ew, eye, mp['sg'], mp['su'], mp['sd'], x1, mod4)


def _moe(rows, hffn, hpack, x1, router, bias, mp, mod4, layer, out_ranges):
    n = rows.n
    sel, eidx, ew, cnt = _router(hffn, router, bias)
    counts = cnt[:, 0].astype(jnp.int32)
    padded = (counts + MOE_TILE - 1) // MOE_TILE * MOE_TILE
    ends = jnp.cumsum(padded)
    n_rows = n * TOP_K + N_EXPERTS * MOE_TILE
    n_tiles = n_rows // MOE_TILE
    base = jnp.broadcast_to((ends - padded).astype(F32)[:, None], (N_EXPERTS, LANES))
    tile_start = jnp.arange(n_tiles, dtype=jnp.int32) * MOE_TILE
    tile_expert = jnp.minimum(jnp.sum((ends[None, :] <= tile_start[:, None]).astype(jnp.int32), axis=1),
                              N_EXPERTS - 1)
    n_used = (ends[-1:] // MOE_TILE).astype(jnp.int32)
    pos = _positions(sel, eidx, base)
    pos3 = pos.reshape(TOP_K, n // SC_CHUNK, SC_CHUNK).transpose(1, 0, 2)
    xs = _sc_scatter(hpack, pos3, n_rows)
    ys = _experts(xs, tile_expert, n_used, mp, layer)
    yg = _sc_gather(ys, pos.reshape(-1))
    return [_combine(rows, hffn, yg, ew, mp, x1, mod4, layer, row0, n_out) for row0, n_out in out_ranges]


def _glu(x, wg, wu):
    hg = jnp.dot(x, wg, preferred_element_type=F32)
    hu = jnp.dot(x, wu, preferred_element_type=F32)
    return _silu(hg) * hu


def kernel(x_prompt, x_sample, c, c_ctx, cache_attn_k, cache_attn_v, state_rwkv_fwd, state_rwkv_bwd,
           state_hgrn_fwd, state_hgrn_bwd, norm1_g, norm2_g, mod_w, mod_b, ab_w_in, ab_w_out, attn_q_norm,
           attn_k_norm, attn_sink, rwkv_mu, rwkv_w0, rwkv_w2, rwkv_a0, rwkv_a2, rwkv_g2, rwkv_k_k, rwkv_k_a,
           rwkv_r_k, rwkv_ln_w, rwkv_ln_b, hgrn_w_in, hgrn_w_out, hgrn_lower_bounds, hgrn_norm_g, moe_router,
           moe_bias, moe_w_gate, moe_w_up, moe_w_down, moe_shared_gate, moe_shared_up, moe_shared_down):
    n_cseq, cseq, _ = x_prompt.shape
    n_lseq, lseq, _ = x_sample.shape
    depth = mod_w.shape[0]
    assert depth == 2 and n_lseq + 1 <= SUBLANES
    assert cseq == PREP_TM and lseq % TM == 0 and lseq % HG_TB == 0 and cseq % HG_TB == 0
    assert (n_cseq * cseq) % TM == 0
    assert n_cseq % RW_BB == 0 and n_lseq % RW_BB == 0 and (n_cseq * cseq) % (lseq * RW_BB) == 0
    rows = _Rows(n_cseq * cseq, n_lseq * lseq, lseq)
    assert rows.n % MOE_TILE == 0 and lseq % MOE_TILE == 0 and rows.n_ctx % MOE_TILE == 0
    kv_w = KV_A * HD_A

    xs = (x_prompt.reshape(rows.n_ctx, D_MODEL), x_sample.reshape(rows.n_lat, D_MODEL))
    cvecs = jnp.concatenate([c_ctx[None, :], c, jnp.zeros((SUBLANES - 1 - n_lseq, D_MODEL), F32)], axis=0)
    mod4 = _modulation(cvecs, mod_w, mod_b).reshape(depth, SUBLANES, 1, 6 * D_MODEL)

    ones_q = _block_ones(W_A, HD_A)
    ones_k = _block_ones(kv_w, HD_A)
    ones_b = _block_ones(W_B, HD_B)
    ones_pair = _block_ones(LANES, HD_B)[:LANES]
    ones_pair = jnp.kron(jnp.eye(2, dtype=BF16), ones_pair)
    cos_t, sin_t = _rope_tables(lseq)

    def moe(l, hffn, hpack, x1, out_ranges):
        mp = {'wg': moe_w_gate, 'wu': moe_w_up, 'wd': moe_w_down,
              'sg': _bf(moe_shared_gate[l]), 'su': _bf(moe_shared_up[l]), 'sd': _bf(moe_shared_down[l])}
        return _moe(rows, hffn, hpack, x1, moe_router[l], moe_bias[l], mp, mod4, l, out_ranges)

    assert W_A == W_B
    all_rows = jnp.zeros((rows.n, W_B), F32)

    pr = {'mu': rwkv_mu[0], 'w0': rwkv_w0[0], 'w2': rwkv_w2[0], 'a0': rwkv_a0[0], 'a2': rwkv_a2[0],
          'g2': rwkv_g2[0], 'k_k': rwkv_k_k[0], 'k_a': rwkv_k_a[0], 'r_k': rwkv_r_k[0].reshape(-1),
          'ln_w': rwkv_ln_w[0], 'ln_b': rwkv_ln_b[0]}
    p_att, p_rw = _inproj(rows, xs, norm1_g[0], mod4, 0, _bf(ab_w_in[0]), (ATT_IN, RWKV_IN), TM)
    qg_t = jnp.tile(attn_q_norm[0], H_A).reshape(1, W_A)
    kg_t = jnp.tile(attn_k_norm[0], KV_A).reshape(1, kv_w)
    o_att, new_k, new_v = _ctx_attention(p_att, n_cseq, cseq, qg_t, kg_t, attn_sink[0], ones_q, ones_k, all_rows)
    past = cache_attn_k.shape[2]
    o_att = _lat_attention(p_att, rows.n_ctx // lseq, n_lseq, lseq, qg_t, kg_t, attn_sink[0], ones_q, ones_k,
                           cos_t, sin_t, cache_attn_k[:, 0].reshape(n_lseq, past, kv_w),
                           cache_attn_v[:, 0].reshape(n_lseq, past, kv_w), o_att)

    pp = _rwkv_prep(rows, p_rw, pr, ones_b)
    zero_st = jnp.zeros((n_cseq, H_B // 2, HD_B, LANES), F32)
    o_f, o_b, sf_c, sb_c = _rwkv_scan(pp, 0, n_cseq, cseq, zero_st, zero_st, ones_pair, all_rows, all_rows)
    o_f, o_b, _, _ = _rwkv_scan(pp, rows.n_ctx, n_lseq, lseq, _state_to_pairs(state_rwkv_fwd[:, 0]),
                                _state_to_pairs(state_rwkv_bwd[:, 0]), ones_pair, o_f, o_b)
    x1, hffn, hpack = _outproj0(rows, xs, o_att, o_f, o_b, pp, pr, ones_b, _bf(ab_w_out[0]), norm2_g[0], mod4, 0)
    (x,) = moe(0, hffn, hpack, x1, [(0, rows.n)])

    (p1,) = _inproj(rows, (x, x), norm1_g[1], mod4, 1, _bf(hgrn_w_in[0]), (IN_C,), IN1_TM)
    zero_h = jnp.zeros((n_cseq, H_C, DK_C, DV_C), F32)
    all_rows_c = jnp.zeros((rows.n, D_C), F32)
    h_f, h_b, hsf_c, hsb_c = _hgrn_scan(p1, 0, n_cseq, cseq, hgrn_lower_bounds, zero_h, zero_h,
                                        all_rows_c, all_rows_c)
    h_f, h_b, _, _ = _hgrn_scan(p1, rows.n_ctx, n_lseq, lseq, hgrn_lower_bounds,
                                state_hgrn_fwd[:, 0], state_hgrn_bwd[:, 0], h_f, h_b)
    x1, hffn, hpack = _outproj1(rows, (x, x), h_f, h_b, p1, hgrn_norm_g[0], _bf(hgrn_w_out[0]), norm2_g[1],
                                mod4, 1)
    y_c, y_l = moe(1, hffn, hpack, x1, [(0, rows.n_ctx), (rows.n_ctx, rows.n_lat)])

    y_prompt = y_c.reshape(n_cseq, cseq, D_MODEL)
    y_sample = y_l.reshape(n_lseq, lseq, D_MODEL)
    return (y_prompt, y_sample,
            new_k.reshape(n_cseq, 1, cseq, KV_A, HD_A), new_v.reshape(n_cseq, 1, cseq, KV_A, HD_A),
            _pairs_to_state(sf_c)[:, None], _pairs_to_state(sb_c)[:, None],
            hsf_c[:, None], hsb_c[:, None])
```

```python
import functools

import numpy as np
import jax
import jax.numpy as jnp
from jax import lax
from jax.experimental import pallas as pl
from jax.experimental.pallas import tpu as pltpu
from jax.experimental.pallas import tpu_sc as plsc

F32 = jnp.float32
BF16 = jnp.bfloat16

D_MODEL = 1024
GRID_W = 64
H_A = 8
KV_A = 2
G_A = H_A // KV_A
HD_A = 64
W_A = H_A * HD_A
WINDOW = 128
QBLK = 128
ROPE_BASE = 10000.0
ROPE_PAIR = HD_A // 4
ATTN_SCALE = HD_A ** -0.5
NEG_INF = -1e30
H_B = 8
HD_B = 64
W_B = H_B * HD_B
LORA_W = 64
LORA_A = 64
LORA_G = 128
GN_EPS = 64e-5
ATT_IN = W_A + 2 * KV_A * HD_A
RWKV_IN = 3 * W_B + LORA_W + LORA_A + LORA_G
IN_AB = ATT_IN + RWKV_IN
H_C = 8
DK_C = 128
DV_C = 128
D_C = H_C * DV_C
CHUNK = 64
IN_C = 5 * D_C
N_EXPERTS = 64
TOP_K = 8
N_GROUPS = 8
TOPK_GROUPS = 4
D_EXPERT = 256
ROUTED_SCALE = 2.5
EPS = 1e-6

LANES = 128
SUBLANES = 8
VMEM_LIMIT = 52 * 1024 * 1024

TM = 512
PREP_TM = 256
IN1_TM = 256
RW_TB = 128
RW_BB = 4
RW_GROUP_BB = 4
HG_TB = 256
MOE_TILE = 1024
POS_TB = 512
SC_CORES = 2
SC_SUBCORES = 16
SC_CHUNK = 64


def _cparams(n_axes):
    return pltpu.CompilerParams(dimension_semantics=("arbitrary",) * n_axes,
                                vmem_limit_bytes=VMEM_LIMIT)


def _bf(x):
    return x.astype(BF16)


def _split2(x):
    hi = lax.bitcast_convert_type(
        lax.bitcast_convert_type(x, jnp.uint32) & jnp.uint32(0xFFFF0000), F32)
    return hi, x - hi


def _seg_sum(x, ones2):
    hi, lo = _split2(x)
    return jnp.dot(jnp.concatenate([_bf(hi), _bf(lo)], axis=1), ones2,
                   preferred_element_type=F32)


def _dot_nt(a, b):
    return lax.dot_general(a, b, (((1,), (1,)), ((), ())), preferred_element_type=F32)


def _dot_tn(a, b):
    return lax.dot_general(a, b, (((0,), (0,)), ((), ())), preferred_element_type=F32)


def _sigmoid(x):
    return 1.0 / (1.0 + jnp.exp(-x))


def _silu(x):
    return x * _sigmoid(x)


def _chunks(seq, n):
    seq = list(seq)
    return [seq[i:i + n] for i in range(0, len(seq), n)]


def _block_ones(width, seg):
    idx = np.arange(width) // seg
    bd = (idx[:, None] == idx[None, :]).astype(np.float32)
    return jnp.asarray(np.concatenate([bd, bd], axis=0), dtype=BF16)


def _mod_kernel(c_ref, w_ref, b_ref, o_ref):
    s = _silu(c_ref[...])
    o_ref[0] = jnp.dot(_bf(s), _bf(w_ref[0]), preferred_element_type=F32) + b_ref[0]


def _modulation(cvecs, mod_w, mod_b):
    depth = mod_w.shape[0]
    n_col = 6 * D_MODEL // D_MODEL
    return pl.pallas_call(
        _mod_kernel,
        grid=(depth, n_col),
        in_specs=[pl.BlockSpec((SUBLANES, D_MODEL), lambda l, j: (0, 0)),
                  pl.BlockSpec((1, D_MODEL, D_MODEL), lambda l, j: (l, 0, j)),
                  pl.BlockSpec((1, 1, D_MODEL), lambda l, j: (l, 0, j))],
        out_specs=pl.BlockSpec((1, SUBLANES, D_MODEL), lambda l, j: (l, 0, j)),
        out_shape=jax.ShapeDtypeStruct((depth, SUBLANES, 6 * D_MODEL), F32),
        compiler_params=_cparams(2),
        name="modulation",
    )(cvecs, mod_w, mod_b.reshape(depth, 1, 6 * D_MODEL))


class _Rows:
    def __init__(self, n_ctx, n_lat, lat_seq):
        self.n_ctx, self.n_lat, self.lat_seq = n_ctx, n_lat, lat_seq
        self.n = n_ctx + n_lat

    def mod_row(self, i, tm):
        nctx_blk = self.n_ctx // tm
        per_seq = self.lat_seq // tm
        return jnp.where(i < nctx_blk, 0, 1 + (i - nctx_blk) // per_seq)

    def mod_spec(self, layer, chunk, tm, blk0=0):
        return pl.BlockSpec((None, None, 1, D_MODEL),
                            lambda i, *_: (layer, self.mod_row(i + blk0, tm), 0, chunk))


def _rms_mod(x, g, sc, sh):
    ms = jnp.mean(x * x, axis=-1, keepdims=True)
    return x * lax.rsqrt(ms + EPS) * g * (1.0 + sc) + sh


def _x_specs(rows, xs, tm=TM):
    xa, xb = xs
    nctx_blk = rows.n_ctx // tm
    lat0 = nctx_blk if xb.shape[0] == rows.n else 0
    return [pl.BlockSpec((tm, D_MODEL), lambda i: (jnp.minimum(i, nctx_blk - 1), 0)),
            pl.BlockSpec((tm, D_MODEL), lambda i: (jnp.maximum(i - nctx_blk, 0) + lat0, 0))]


def _pick_x(rows, xa_ref, xb_ref):
    return jnp.where(pl.program_id(0) < rows.n_ctx // xa_ref.shape[0], xa_ref[...], xb_ref[...])


def _inproj_kernel(rows, splits, xa_ref, xb_ref, g_ref, sh_ref, sc_ref, w_ref, *o_refs):
    h = _rms_mod(_pick_x(rows, xa_ref, xb_ref), g_ref[...], sc_ref[...], sh_ref[...])
    p = jnp.dot(_bf(h), w_ref[...], preferred_element_type=F32)
    lo = 0
    for o_ref, width in zip(o_refs, splits):
        o_ref[...] = p[:, lo:lo + width]
        lo += width


def _inproj(rows, xs, g, mod4, layer, w_bf, splits, tm):
    n_out = w_bf.shape[1]
    return pl.pallas_call(
        functools.partial(_inproj_kernel, rows, splits),
        grid=(rows.n // tm,),
        in_specs=_x_specs(rows, xs, tm) + [
            pl.BlockSpec((1, D_MODEL), lambda i: (0, 0)),
            rows.mod_spec(layer, 0, tm),
            rows.mod_spec(layer, 1, tm),
            pl.BlockSpec((D_MODEL, n_out), lambda i: (0, 0))],
        out_specs=[pl.BlockSpec((tm, wd), lambda i: (i, 0)) for wd in splits],
        out_shape=[jax.ShapeDtypeStruct((rows.n, wd), F32) for wd in splits],
        compiler_params=_cparams(1),
        name=f"inproj{layer}",
    )(*xs, g.reshape(1, D_MODEL), mod4, mod4, w_bf)


def _head_rms(x, gain_t, ones2):
    ms = _seg_sum(x * x, ones2) * (1.0 / HD_A)
    return x * lax.rsqrt(ms + EPS) * gain_t


def _sink_softmax_pv(parts, sink):
    m = jnp.maximum(functools.reduce(jnp.maximum, [jnp.max(s, axis=-1, keepdims=True) for s, _ in parts]), sink)
    den = jnp.exp(sink - m)
    acc = None
    for s, v in parts:
        p = jnp.exp(s - m)
        den = den + jnp.sum(p, axis=-1, keepdims=True)
        pv = jnp.dot(_bf(p), v, preferred_element_type=F32)
        acc = pv if acc is None else acc + pv
    return acc / den


def _ctx_attn_kernel(p_ref, qg_ref, kg_ref, sink_ref, ones_q_ref, ones_k_ref, prev_ref, o_ref, k_ref, v_ref):
    del prev_ref
    p = p_ref[...]
    q = _head_rms(p[:, :W_A], qg_ref[...], ones_q_ref[...]) * ATTN_SCALE
    k = _head_rms(p[:, W_A:W_A + KV_A * HD_A], kg_ref[...], ones_k_ref[...])
    v = p[:, W_A + KV_A * HD_A:ATT_IN]
    k_ref[0] = k
    v_ref[0] = v
    qb, kb, vb = _bf(q), _bf(k), _bf(v)
    outs = []
    for h in range(H_A):
        j = h // G_A
        s = _dot_nt(qb[:, h * HD_A:(h + 1) * HD_A], kb[:, j * HD_A:(j + 1) * HD_A])
        outs.append(_sink_softmax_pv([(s, vb[:, j * HD_A:(j + 1) * HD_A])], sink_ref[h]))
    o_ref[...] = jnp.concatenate(outs, axis=1)


def _ctx_attention(p_att, n_seq, seq, qg_t, kg_t, sink, ones_q, ones_k, prev):
    kv_w = KV_A * HD_A
    return pl.pallas_call(
        _ctx_attn_kernel,
        grid=(n_seq,),
        in_specs=[pl.BlockSpec((seq, ATT_IN), lambda b: (b, 0)),
                  pl.BlockSpec((1, W_A), lambda b: (0, 0)),
                  pl.BlockSpec((1, kv_w), lambda b: (0, 0)),
                  pl.BlockSpec(memory_space=pltpu.SMEM),
                  pl.BlockSpec(ones_q.shape, lambda b: (0, 0)),
                  pl.BlockSpec(ones_k.shape, lambda b: (0, 0)),
                  pl.BlockSpec(memory_space=pl.ANY)],
        out_specs=[pl.BlockSpec((seq, W_A), lambda b: (b, 0)),
                   pl.BlockSpec((1, seq, kv_w), lambda b: (b, 0, 0)),
                   pl.BlockSpec((1, seq, kv_w), lambda b: (b, 0, 0))],
        input_output_aliases={6: 0},
        out_shape=[jax.ShapeDtypeStruct(prev.shape, F32),
                   jax.ShapeDtypeStruct((n_seq, seq, kv_w), F32),
                   jax.ShapeDtypeStruct((n_seq, seq, kv_w), F32)],
        compiler_params=_cparams(1),
        name="ctx_attention",
    )(p_att, qg_t, kg_t, sink, ones_q, ones_k, prev)


def _rope(x, cos_t, sin_t):
    lane = lax.broadcasted_iota(jnp.int32, cos_t.shape, 1)
    low = (lane % (2 * ROPE_PAIR)) < ROPE_PAIR
    outs = []
    for s in range(x.shape[1] // LANES):
        xs = x[:, s * LANES:(s + 1) * LANES]
        partner = jnp.where(low, pltpu.roll(xs, LANES - ROPE_PAIR, 1), pltpu.roll(xs, ROPE_PAIR, 1))
        outs.append(xs * cos_t + partner * sin_t)
    return outs[0] if len(outs) == 1 else jnp.concatenate(outs, axis=1)


def _lat_attn_kernel(seq, p_ref, qg_ref, kg_ref, sink_ref, ones_q_ref, ones_k_ref, cos_ref, sin_ref,
                     kc_ref, vc_ref, prev_ref, o_ref, q_scr, k_scr, v_scr):
    del prev_ref
    kv_w = KV_A * HD_A
    p = p_ref[...]
    q = _head_rms(p[:, :W_A], qg_ref[...], ones_q_ref[...])
    k = _head_rms(p[:, W_A:W_A + kv_w], kg_ref[...], ones_k_ref[...])
    qr = _bf(_rope(q, cos_ref[...], sin_ref[...]) * ATTN_SCALE)
    kr = _bf(_rope(k, cos_ref[...], sin_ref[...]))
    vb = _bf(p[:, W_A + kv_w:ATT_IN])
    for h in range(H_A):
        q_scr[h] = qr[:, h * HD_A:(h + 1) * HD_A]
    for j in range(KV_A):
        k_scr[j] = kr[:, j * HD_A:(j + 1) * HD_A]
        v_scr[j] = vb[:, j * HD_A:(j + 1) * HD_A]
    kc = _bf(kc_ref[0])
    vc = _bf(vc_ref[0])
    n_local = 3 * QBLK
    grp = lax.broadcasted_iota(jnp.int32, (G_A * QBLK, 1), 0) // QBLK

    def block(i, carry):
        q0 = pl.multiple_of(i * QBLK, QBLK)
        start = pl.multiple_of(jnp.clip((i - 1) * QBLK, 0, seq - n_local), QBLK)
        ipos = q0 + lax.broadcasted_iota(jnp.int32, (G_A * QBLK, n_local), 0) % QBLK
        jpos = start + lax.broadcasted_iota(jnp.int32, (G_A * QBLK, n_local), 1)
        band = jnp.abs(jpos - ipos) <= WINDOW
        outs = []
        for j in range(KV_A):
            qs = jnp.concatenate([q_scr[j * G_A + g, pl.ds(q0, QBLK), :] for g in range(G_A)], axis=0)
            kl = k_scr[j, pl.ds(start, n_local), :]
            vl = v_scr[j, pl.ds(start, n_local), :]
            sink = jnp.zeros((G_A * QBLK, 1), F32)
            for g in range(G_A):
                sink = jnp.where(grp == g, sink_ref[j * G_A + g], sink)
            s_loc = jnp.where(band, _dot_nt(qs, kl), NEG_INF)
            s_ctx = _dot_nt(qs, kc[:, j * HD_A:(j + 1) * HD_A])
            o = _sink_softmax_pv([(s_loc, vl), (s_ctx, vc[:, j * HD_A:(j + 1) * HD_A])], sink)
            outs.extend(o[g * QBLK:(g + 1) * QBLK] for g in range(G_A))
        o_ref[pl.ds(q0, QBLK), :] = jnp.concatenate(outs, axis=1)
        return carry

    lax.fori_loop(0, seq // QBLK, block, 0)


def _lat_attention(p_att, row_blk0, n_seq, seq, qg_t, kg_t, sink, ones_q, ones_k, cos_t, sin_t, kc, vc, prev):
    kv_w = KV_A * HD_A
    past = kc.shape[1]
    return pl.pallas_call(
        functools.partial(_lat_attn_kernel, seq),
        grid=(n_seq,),
        in_specs=[pl.BlockSpec((seq, ATT_IN), lambda b: (row_blk0 + b, 0)),
                  pl.BlockSpec((1, W_A), lambda b: (0, 0)),
                  pl.BlockSpec((1, kv_w), lambda b: (0, 0)),
                  pl.BlockSpec(memory_space=pltpu.SMEM),
                  pl.BlockSpec(ones_q.shape, lambda b: (0, 0)),
                  pl.BlockSpec(ones_k.shape, lambda b: (0, 0)),
                  pl.BlockSpec((seq, LANES), lambda b: (0, 0)),
                  pl.BlockSpec((seq, LANES), lambda b: (0, 0)),
                  pl.BlockSpec((1, past, kv_w), lambda b: (b, 0, 0)),
                  pl.BlockSpec((1, past, kv_w), lambda b: (b, 0, 0)),
                  pl.BlockSpec(memory_space=pl.ANY)],
        out_specs=pl.BlockSpec((seq, W_A), lambda b: (row_blk0 + b, 0)),
        out_shape=jax.ShapeDtypeStruct(prev.shape, F32),
        input_output_aliases={10: 0},
        scratch_shapes=[pltpu.VMEM((H_A, seq, HD_A), BF16), pltpu.VMEM((KV_A, seq, HD_A), BF16),
                        pltpu.VMEM((KV_A, seq, HD_A), BF16)],
        compiler_params=_cparams(1),
        name="lat_attention",
    )(p_att, qg_t, kg_t, sink, ones_q, ones_k, cos_t, sin_t, kc, vc, prev)


def _rope_tables(seq):
    pos = np.arange(seq)
    row = (pos // GRID_W).astype(np.float32)
    col = (pos % GRID_W).astype(np.float32)
    d_axis = HD_A // 2
    inv = (ROPE_BASE ** (-np.arange(0, d_axis, 2, dtype=np.float32) / d_axis)).astype(np.float32)
    cos_h = np.zeros((seq, HD_A), np.float32)
    sin_h = np.zeros((seq, HD_A), np.float32)
    for seg, p_ in enumerate((row, col)):
        ang = (p_[:, None] * inv[None, :]).astype(np.float32)
        c, s = np.cos(ang), np.sin(ang)
        base = seg * d_axis
        cos_h[:, base:base + d_axis // 2] = c
        cos_h[:, base + d_axis // 2:base + d_axis] = c
        sin_h[:, base:base + d_axis // 2] = -s
        sin_h[:, base + d_axis // 2:base + d_axis] = s
    rep = LANES // HD_A
    return jnp.asarray(np.tile(cos_h, (1, rep))), jnp.asarray(np.tile(sin_h, (1, rep)))


def _rwkv_prep_kernel(rows, x_ref, prev_ref, next_ref, mu_ref, kk_ref, ka_ref, rk_ref, w0_ref, w2_ref,
                      a0_ref, a2_ref, g2_ref, ones_ref,
                      nkk_ref, r_ref, v_ref, g_ref, bonus_ref,
                      wf_ref, kaf_ref, kdf_ref, wb_ref, kab_ref, kdb_ref):
    i = pl.program_id(0)
    nctx_blk = rows.n_ctx // PREP_TM
    per_seq = rows.lat_seq // PREP_TM
    is_ctx = i < nctx_blk
    first = jnp.logical_or(is_ctx, (i - nctx_blk) % per_seq == 0)
    last = jnp.logical_or(is_ctx, (i - nctx_blk) % per_seq == per_seq - 1)
    x = x_ref[...]
    ridx = lax.broadcasted_iota(jnp.int32, x.shape, 0)
    prev_row = jnp.where(first, 0.0, prev_ref[SUBLANES - 1:SUBLANES, :])
    next_row = jnp.where(last, 0.0, next_ref[0:1, :])
    xm1 = jnp.where(ridx == 0, prev_row, pltpu.roll(x, 1, 0))
    xp1 = jnp.where(ridx == PREP_TM - 1, next_row, pltpu.roll(x, PREP_TM - 1, 0))
    pw = x + (0.5 * (xm1 + xp1) - x) * mu_ref[...]

    r = pw[:, 0:W_B]
    k = pw[:, W_B:2 * W_B]
    v = pw[:, 2 * W_B:3 * W_B]
    wd = pw[:, 3 * W_B:3 * W_B + LORA_W]
    ad = pw[:, 3 * W_B + LORA_W:3 * W_B + LORA_W + LORA_A]
    gd = pw[:, 3 * W_B + LORA_W + LORA_A:]
    ones2 = ones_ref[...]

    kk = k * kk_ref[...]
    kk = kk / jnp.maximum(jnp.sqrt(_seg_sum(kk * kk, ones2)), 1e-12)
    nkk_ref[...] = -kk
    r_ref[...] = r
    v_ref[...] = v
    g_ref[...] = jnp.dot(_bf(_sigmoid(gd)), g2_ref[...], preferred_element_type=F32)
    tw = _bf(jnp.tanh(wd))
    adb = _bf(ad)
    bonus = jnp.zeros_like(r)
    for d, (w_o, ka_o, kd_o) in enumerate(((wf_ref, kaf_ref, kdf_ref), (wb_ref, kab_ref, kdb_ref))):
        z = -(w0_ref[d:d + 1, :] + jnp.dot(tw, w2_ref[d], preferred_element_type=F32))
        softplus = jnp.maximum(z, 0.0) + jnp.log(1.0 + jnp.exp(-jnp.abs(z)))
        w_o[...] = jnp.exp(-jnp.exp(-softplus - 0.5))
        a = _sigmoid(a0_ref[d:d + 1, :] + jnp.dot(adb, a2_ref[d], preferred_element_type=F32))
        kd = k * (1.0 + (a - 1.0) * ka_ref[...])
        ka_o[...] = kk * a
        kd_o[...] = kd
        bonus = bonus + _seg_sum(r * kd * rk_ref[...], ones2) * v
    bonus_ref[...] = bonus


def _rwkv_prep(rows, p_rw, pr, ones_b):
    n = rows.n
    n_halo = n // SUBLANES
    blk_halo = PREP_TM // SUBLANES
    row = lambda a: a.reshape(1, -1)
    full = lambda a: pl.BlockSpec(a.shape, lambda i: (0,) * a.ndim)
    consts = [row(pr['mu']), row(pr['k_k']), row(pr['k_a']), row(pr['r_k']), pr['w0'], _bf(pr['w2']),
              pr['a0'], _bf(pr['a2']), _bf(pr['g2']), ones_b]
    outs = pl.pallas_call(
        functools.partial(_rwkv_prep_kernel, rows),
        grid=(n // PREP_TM,),
        in_specs=[pl.BlockSpec((PREP_TM, RWKV_IN), lambda i: (i, 0)),
                  pl.BlockSpec((SUBLANES, RWKV_IN), lambda i: (jnp.maximum(i * blk_halo - 1, 0), 0)),
                  pl.BlockSpec((SUBLANES, RWKV_IN), lambda i: (jnp.minimum((i + 1) * blk_halo, n_halo - 1), 0))]
                 + [full(a) for a in consts],
        out_specs=[pl.BlockSpec((PREP_TM, W_B), lambda i: (i, 0))] * 11,
        out_shape=[jax.ShapeDtypeStruct((n, W_B), F32)] * 11,
        compiler_params=_cparams(1),
        name="rwkv_prep",
    )(p_rw, p_rw, p_rw, *consts)
    names = ('nkk', 'r', 'v', 'g', 'bonus', 'w_f', 'ka_f', 'kd_f', 'w_b', 'ka_b', 'kd_b')
    return dict(zip(names, outs))


def _rwkv_scan_kernel(n_tb, nkkf_ref, rf_ref, vf_ref, wf_ref, kaf_ref, kdf_ref,
                      nkkb_ref, rb_ref, vb_ref, wb_ref, kab_ref, kdb_ref,
                      s0f_ref, s0b_ref, ones_ref, prevf_ref, prevb_ref,
                      of_ref, ob_ref, sff_ref, sfb_ref, sf_scr, sb_scr, vt_scr):
    del prevf_ref, prevb_ref
    s_scr = (sf_scr, sb_scr)
    tb = pl.program_id(1)
    n_pair = H_B // 2
    half = RW_TB // 2
    dirs = ((nkkf_ref, rf_ref, vf_ref, wf_ref, kaf_ref, kdf_ref, of_ref, False),
            (nkkb_ref, rb_ref, vb_ref, wb_ref, kab_ref, kdb_ref, ob_ref, True))

    @pl.when(tb == 0)
    def _():
        sf_scr[...] = s0f_ref[...]
        sb_scr[...] = s0b_ref[...]

    lane = lax.broadcasted_iota(jnp.int32, (HD_B, LANES), 1)
    for d, refs in enumerate(dirs):
        v_ref = refs[2]
        for bb in range(RW_BB):
            for p in range(n_pair):
                vt = v_ref[bb, :, p * LANES:(p + 1) * LANES].T
                top, bot = vt[:HD_B], vt[HD_B:]
                for s in range(2):
                    if s == 0:
                        t2 = jnp.where(lane < HD_B, top, pltpu.roll(bot, HD_B, 1))
                    else:
                        t2 = jnp.where(lane < HD_B, pltpu.roll(top, HD_B, 1), bot)
                    vt_scr[d, bb, p, s] = t2

    ones2 = ones_ref[...]
    row8 = lax.broadcasted_iota(jnp.int32, (SUBLANES, LANES), 0)
    lane8 = lax.broadcasted_iota(jnp.int32, (SUBLANES, LANES), 1)
    sel_r = jnp.logical_or(jnp.logical_and(row8 % 2 == 0, lane8 < HD_B),
                           jnp.logical_and(row8 % 2 == 1, lane8 >= HD_B))

    def row_of(rev, tt):
        return RW_TB - 1 - tt if rev else tt

    def emit_output(d, bb, tau):
        r_ref, o_ref = dirs[d][1], dirs[d][6]
        r = r_ref[bb, pl.ds(tau, 1), :]
        r8 = jnp.zeros((SUBLANES, LANES), F32)
        for p in range(n_pair):
            rp = jnp.broadcast_to(r[:, p * LANES:(p + 1) * LANES], (SUBLANES, LANES))
            r8 = jnp.where(jnp.logical_and(sel_r, row8 // 2 == p), rp, r8)
        s_all = jnp.concatenate([_bf(s_scr[d][bb, p]) for p in range(n_pair)], axis=0)
        o8 = _dot_nt(_bf(r8), s_all)
        o_parts = []
        for p in range(n_pair):
            for h in range(2):
                o_parts.append(o8[2 * p + h:2 * p + h + 1, p * HD_B:(p + 1) * HD_B])
        o_ref[bb, pl.ds(tau, 1), :] = jnp.concatenate(o_parts, axis=1)

    groups = [(d, bbs) for d in range(2) for bbs in _chunks(range(RW_BB), RW_GROUP_BB)]

    def reduce_phase(grp, tt):
        d, bbs = grp
        rev = dirs[d][7]
        tau = row_of(rev, tt)
        sub = tau // half
        lt = tau % half
        mask = jnp.logical_or(lane == lt, lane == lt + HD_B)
        lhs = []
        for bb in bbs:
            emit_output(d, bb, row_of(rev, jnp.maximum(tt - 1, 0)))
            nkk = dirs[d][0][bb, pl.ds(tau, 1), :]
            for p in range(n_pair):
                prod = s_scr[d][bb, p] * nkk[:, p * LANES:(p + 1) * LANES]
                lhs.append(jnp.concatenate([_bf(prod), _bf(jnp.where(mask, vt_scr[d, bb, p, sub], 0.0))],
                                           axis=1))
        return jnp.dot(jnp.concatenate(lhs, axis=0), ones2, preferred_element_type=F32)

    def update_phase(grp, tt, red):
        d, bbs = grp
        _, _, _, w_ref, ka_ref, kd_ref, _, rev = dirs[d]
        tau = row_of(rev, tt)
        for k, bb in enumerate(bbs):
            w = w_ref[bb, pl.ds(tau, 1), :]
            ka = ka_ref[bb, pl.ds(tau, 1), :]
            kd = kd_ref[bb, pl.ds(tau, 1), :]
            for p in range(n_pair):
                sl = slice(p * LANES, (p + 1) * LANES)
                r0 = (k * n_pair + p) * HD_B
                sa = red[r0:r0 + HD_B, :LANES]
                vcol = red[r0:r0 + HD_B, LANES:]
                s_scr[d][bb, p] = s_scr[d][bb, p] * w[:, sl] + sa * ka[:, sl] + vcol * kd[:, sl]

    def step(tt, carry):
        reds = [reduce_phase(g, tt) for g in groups]
        for g, red in zip(groups, reds):
            update_phase(g, tt, red)
        return carry

    lax.fori_loop(0, RW_TB, step, 0)
    for d in range(2):
        for bb in range(RW_BB):
            emit_output(d, bb, row_of(dirs[d][7], RW_TB - 1))

    @pl.when(tb == n_tb - 1)
    def _():
        sff_ref[...] = sf_scr[...]
        sfb_ref[...] = sb_scr[...]


def _rwkv_scan(pp, row0, n_seq, seq, s0_f, s0_b, ones_pair, prev_f, prev_b):
    n_tb = seq // RW_TB
    n_pair = H_B // 2
    blk0 = row0 // seq
    view = lambda a: a.reshape(a.shape[0] // seq, seq, W_B)
    fwd = pl.BlockSpec((RW_BB, RW_TB, W_B), lambda b, t: (blk0 // RW_BB + b, t, 0))
    bwd = pl.BlockSpec((RW_BB, RW_TB, W_B), lambda b, t: (blk0 // RW_BB + b, n_tb - 1 - t, 0))
    st = pl.BlockSpec((RW_BB, n_pair, HD_B, LANES), lambda b, t: (b, 0, 0, 0))
    ins_f = [view(pp[k]) for k in ('nkk', 'r', 'v', 'w_f', 'ka_f', 'kd_f')]
    ins_b = [view(pp[k]) for k in ('nkk', 'r', 'v', 'w_b', 'ka_b', 'kd_b')]
    st_shape = jax.ShapeDtypeStruct((n_seq, n_pair, HD_B, LANES), F32)
    o_shape = jax.ShapeDtypeStruct(view(prev_f).shape, F32)
    any_spec = pl.BlockSpec(memory_space=pl.ANY)
    o_f, o_b, sf, sb = pl.pallas_call(
        functools.partial(_rwkv_scan_kernel, n_tb),
        grid=(n_seq // RW_BB, n_tb),
        in_specs=[fwd] * 6 + [bwd] * 6 + [st, st, pl.BlockSpec(ones_pair.shape, lambda b, t: (0, 0)),
                                           any_spec, any_spec],
        out_specs=[fwd, bwd, st, st],
        out_shape=[o_shape, o_shape, st_shape, st_shape],
        input_output_aliases={15: 0, 16: 1},
        scratch_shapes=[pltpu.VMEM((RW_BB, n_pair, HD_B, LANES), F32),
                        pltpu.VMEM((RW_BB, n_pair, HD_B, LANES), F32),
                        pltpu.VMEM((2, RW_BB, n_pair, 2, HD_B, LANES), F32)],
        compiler_params=_cparams(2),
        name="rwkv_scan",
    )(*ins_f, *ins_b, s0_f, s0_b, ones_pair, view(prev_f), view(prev_b))
    return o_f.reshape(prev_f.shape), o_b.reshape(prev_b.shape), sf, sb


def _state_to_pairs(s):
    b = s.shape[0]
    return s.reshape(b, H_B // 2, 2, HD_B, HD_B).transpose(0, 1, 3, 2, 4).reshape(b, H_B // 2, HD_B, 2 * HD_B)


def _pairs_to_state(s):
    b = s.shape[0]
    return s.reshape(b, H_B // 2, HD_B, 2, HD_B).transpose(0, 1, 3, 2, 4).reshape(b, H_B, HD_B, HD_B)


def _tail(x, y, g1, n2g, sc2, sh2, x1_ref, h_ref, hp_ref):
    x1 = x + g1 * y
    x1_ref[...] = x1
    h = _rms_mod(x1, n2g, sc2, sh2)
    h_ref[...] = _bf(h)
    hp_ref[...] = _pack_pairs(h)


def _outproj0_kernel(rows, xa_ref, xb_ref, oa_ref, of_ref, ob_ref, bonus_ref, g_ref, lnw_ref, lnb_ref, ones_ref,
                     w_ref, g1_ref, n2g_ref, sc2_ref, sh2_ref, x1_ref, h_ref, hp_ref):
    o_sum = of_ref[...] + ob_ref[...]
    ones2 = ones_ref[...]
    mean = _seg_sum(o_sum, ones2) * (1.0 / HD_B)
    cen = o_sum - mean
    var = _seg_sum(cen * cen, ones2) * (1.0 / HD_B)
    gn = cen * lax.rsqrt(var + GN_EPS) * lnw_ref[...] + lnb_ref[...]
    o_rw = (gn + bonus_ref[...]) * g_ref[...]
    mix = jnp.concatenate([_bf(oa_ref[...]), _bf(o_rw)], axis=1)
    y = jnp.dot(mix, w_ref[...], preferred_element_type=F32)
    _tail(_pick_x(rows, xa_ref, xb_ref), y, g1_ref[...], n2g_ref[...], sc2_ref[...], sh2_ref[...], x1_ref, h_ref, hp_ref)


def _outproj0(rows, xs, o_att, o_f, o_b, pp, pr, ones_b, w_out_bf, n2g, mod4, layer):
    n = rows.n
    tok = lambda w: pl.BlockSpec((TM, w), lambda i: (i, 0))
    const = lambda a: pl.BlockSpec(a.shape, lambda i: (0,) * a.ndim)
    lnw, lnb, n2 = pr['ln_w'].reshape(1, -1), pr['ln_b'].reshape(1, -1), n2g.reshape(1, -1)
    return pl.pallas_call(
        functools.partial(_outproj0_kernel, rows),
        grid=(n // TM,),
        in_specs=_x_specs(rows, xs) + [tok(W_A), tok(W_B), tok(W_B), tok(W_B), tok(W_B),
                  const(lnw), const(lnb), const(ones_b), const(w_out_bf),
                  rows.mod_spec(layer, 2, TM), const(n2), rows.mod_spec(layer, 4, TM), rows.mod_spec(layer, 3, TM)],
        out_specs=[tok(D_MODEL), tok(D_MODEL), tok(D_MODEL // 2)],
        out_shape=[jax.ShapeDtypeStruct((n, D_MODEL), F32), jax.ShapeDtypeStruct((n, D_MODEL), BF16),
                   jax.ShapeDtypeStruct((n, D_MODEL // 2), jnp.uint32)],
        compiler_params=_cparams(1),
        name="outproj0",
    )(*xs, o_att, o_f, o_b, pp['bonus'], pp['g'], lnw, lnb, ones_b, w_out_bf, mod4, n2, mod4, mod4)


def _outproj1_kernel(rows, xa_ref, xb_ref, of_ref, ob_ref, gate_ref, ng_ref, w_ref, g1_ref, n2g_ref, sc2_ref, sh2_ref,
                     x1_ref, h_ref, hp_ref):
    o_sum = of_ref[...] + ob_ref[...]
    parts = []
    for h in range(H_C):
        oh = o_sum[:, h * DV_C:(h + 1) * DV_C]
        parts.append(oh * lax.rsqrt(jnp.mean(oh * oh, axis=-1, keepdims=True) + EPS))
    o = jnp.concatenate(parts, axis=1) * ng_ref[...] * _silu(gate_ref[...])
    y = jnp.dot(_bf(o), w_ref[...], preferred_element_type=F32)
    _tail(_pick_x(rows, xa_ref, xb_ref), y, g1_ref[...], n2g_ref[...], sc2_ref[...], sh2_ref[...], x1_ref, h_ref, hp_ref)


def _outproj1(rows, xs, o_f, o_b, p1, norm_g, w_out_bf, n2g, mod4, layer):
    n = rows.n
    tok = lambda w: pl.BlockSpec((TM, w), lambda i: (i, 0))
    const = lambda a: pl.BlockSpec(a.shape, lambda i: (0,) * a.ndim)
    ng, n2 = norm_g.reshape(1, -1), n2g.reshape(1, -1)
    return pl.pallas_call(
        functools.partial(_outproj1_kernel, rows),
        grid=(n // TM,),
        in_specs=_x_specs(rows, xs) + [tok(D_C), tok(D_C), pl.BlockSpec((TM, D_C), lambda i: (i, 4)),
                  const(ng), const(w_out_bf),
                  rows.mod_spec(layer, 2, TM), const(n2), rows.mod_spec(layer, 4, TM), rows.mod_spec(layer, 3, TM)],
        out_specs=[tok(D_MODEL), tok(D_MODEL), tok(D_MODEL // 2)],
        out_shape=[jax.ShapeDtypeStruct((n, D_MODEL), F32), jax.ShapeDtypeStruct((n, D_MODEL), BF16),
                   jax.ShapeDtypeStruct((n, D_MODEL // 2), jnp.uint32)],
        compiler_params=_cparams(1),
        name="outproj1",
    )(*xs, o_f, o_b, p1, ng, w_out_bf, mod4, n2, mod4, mod4)


def _hgrn_kernel(n_tb, qf_ref, ff_ref, if_ref, qb_ref, fb_ref, ib_ref, lbp_ref, s0f_ref, s0b_ref,
                 trif_ref, trib_ref, prevf_ref, prevb_ref, of_ref, ob_ref, sff_ref, sfb_ref, s_scr):
    del prevf_ref, prevb_ref
    tb = pl.program_id(1)

    @pl.when(tb == 0)
    def _():
        for h in range(H_C):
            s_scr[0, h] = s0f_ref[0, h].T
            s_scr[1, h] = s0b_ref[0, h].T

    lbp = lbp_ref[...]
    e = jnp.exp(lbp - jnp.max(lbp, axis=0, keepdims=True))
    sm = e / jnp.sum(e, axis=0, keepdims=True)
    lb = (sm[0:1] + sm[1:2]) - sm[0:1]

    n_chunk = HG_TB // CHUNK
    ti = lax.broadcasted_iota(jnp.int32, (HG_TB, HG_TB), 0)
    si = lax.broadcasted_iota(jnp.int32, (HG_TB, HG_TB), 1)
    same = (ti // CHUNK) == (si // CHUNK)
    dirs = ((qf_ref, ff_ref, if_ref, of_ref, trif_ref, jnp.logical_and(same, ti >= si), CHUNK - 1, False),
            (qb_ref, fb_ref, ib_ref, ob_ref, trib_ref, jnp.logical_and(same, ti <= si), 0, True))

    staged = []
    for d, (q_ref, f_ref, i_ref, o_ref, tri_ref, causal, last_row, rev) in enumerate(dirs):
        q = _silu(q_ref[...])
        f = lb + (1.0 - lb) * _sigmoid(f_ref[...])
        k = 1.0 - f
        v = _bf(i_ref[...])
        g = jnp.log(f)
        g1 = _bf(g)
        g2 = _bf(g - g1.astype(F32))
        tri2 = tri_ref[...]
        b_parts, last_parts, dec = [], [], []
        for c in range(n_chunk):
            rc = slice(c * CHUNK, (c + 1) * CHUNK)
            bc = jnp.dot(tri2, jnp.concatenate([g1[rc], g2[rc]], axis=0), preferred_element_type=F32)
            b_parts.append(bc)
            last = bc[last_row:last_row + 1]
            last_parts.append(jnp.broadcast_to(last, bc.shape))
            dec.append(jnp.exp(last))
        b = jnp.concatenate(b_parts, axis=0)
        b_last = jnp.concatenate(last_parts, axis=0)
        staged.append((_bf(q * jnp.exp(b)), _bf(k * jnp.exp(-b)), _bf(k * jnp.exp(b_last - b)), v, dec))

    for h in range(H_C):
        sl = slice(h * DK_C, (h + 1) * DK_C)
        for d, (q_ref, f_ref, i_ref, o_ref, tri_ref, causal, last_row, rev) in enumerate(dirs):
            q_in, k_in, k_out, v, dec = staged[d]
            qh, vh = q_in[:, sl], v[:, sl]
            att = jnp.where(causal, _dot_nt(qh, k_in[:, sl]), 0.0)
            o_intra = jnp.dot(_bf(att), vh, preferred_element_type=F32)
            s_t = s_scr[d, h]
            for c in (range(n_chunk - 1, -1, -1) if rev else range(n_chunk)):
                rc = slice(c * CHUNK, (c + 1) * CHUNK)
                o_ref[rc, sl] = o_intra[rc] + _dot_nt(qh[rc], _bf(s_t))
                s_t = dec[c][:, sl] * s_t + _dot_tn(vh[rc], k_out[rc, sl])
            s_scr[d, h] = s_t

    @pl.when(tb == n_tb - 1)
    def _():
        for h in range(H_C):
            sff_ref[0, h] = s_scr[0, h].T
            sfb_ref[0, h] = s_scr[1, h].T


def _hgrn_scan(p1, row0, n_seq, seq, lb_params, s0_f, s0_b, prev_f, prev_b):
    n_tb = seq // HG_TB
    blk0 = row0 // HG_TB
    tri = np.tril(np.ones((CHUNK, CHUNK), np.float32))
    tri_f = jnp.asarray(np.concatenate([tri] * 2, axis=1), dtype=BF16)
    tri_b = jnp.asarray(np.concatenate([tri.T] * 2, axis=1), dtype=BF16)
    fwd = lambda col: pl.BlockSpec((HG_TB, D_C), lambda b, t: (blk0 + b * n_tb + t, col))
    bwd = lambda col: pl.BlockSpec((HG_TB, D_C), lambda b, t: (blk0 + b * n_tb + n_tb - 1 - t, col))
    st = pl.BlockSpec((1, H_C, DK_C, DV_C), lambda b, t: (b, 0, 0, 0))
    const = lambda a: pl.BlockSpec(a.shape, lambda b, t: (0,) * a.ndim)
    o_shape = jax.ShapeDtypeStruct(prev_f.shape, F32)
    st_shape = jax.ShapeDtypeStruct((n_seq, H_C, DK_C, DV_C), F32)
    any_spec = pl.BlockSpec(memory_space=pl.ANY)
    return pl.pallas_call(
        functools.partial(_hgrn_kernel, n_tb),
        grid=(n_seq, n_tb),
        in_specs=[fwd(0), fwd(1), fwd(3), bwd(0), bwd(2), bwd(3), const(lb_params), st, st,
                  const(tri_f), const(tri_b), any_spec, any_spec],
        out_specs=[fwd(0), bwd(0), st, st],
        out_shape=[o_shape, o_shape, st_shape, st_shape],
        input_output_aliases={11: 0, 12: 1},
        scratch_shapes=[pltpu.VMEM((2, H_C, DV_C, DK_C), F32)],
        compiler_params=_cparams(2),
        name="hgrn_scan",
    )(p1, p1, p1, p1, p1, p1, lb_params, s0_f, s0_b, tri_f, tri_b, prev_f, prev_b)


def _router_kernel(h_ref, rhi_ref, rlo_ref, bias_ref, sel_ref, eidx_ref, ew_ref, cnt_ref):
    x = h_ref[...]
    tm = x.shape[0]
    logits = _dot_nt(rhi_ref[...], x) + _dot_nt(rlo_ref[...], x)
    scores = _sigmoid(logits)
    biased = scores + bias_ref[...]
    per = N_EXPERTS // N_GROUPS
    sub = lax.broadcasted_iota(jnp.int32, (per, tm), 0)
    gs_rows = []
    for g in range(N_GROUPS):
        blk = biased[g * per:(g + 1) * per]
        m1 = jnp.max(blk, axis=0, keepdims=True)
        first = jnp.min(jnp.where(blk == m1, sub, per), axis=0, keepdims=True)
        m2 = jnp.max(jnp.where(sub == first, -jnp.inf, blk), axis=0, keepdims=True)
        gs_rows.append(m1 + m2)
    gs = jnp.concatenate(gs_rows, axis=0)
    gi = lax.broadcasted_iota(jnp.int32, gs.shape, 0)
    rank = jnp.zeros(gs.shape, jnp.int32)
    for s in range(1, N_GROUPS):
        other = pltpu.roll(gs, s, 0)
        oi = pltpu.roll(gi, s, 0)
        beats = jnp.logical_or(other > gs, jnp.logical_and(other == gs, oi < gi))
        rank = rank + jnp.where(beats, 1, 0)
    keep = jnp.where(rank < TOPK_GROUPS, 1.0, 0.0)
    emask = jnp.concatenate([jnp.broadcast_to(keep[g:g + 1], (per, tm)) for g in range(N_GROUPS)], axis=0)
    cur = jnp.where(emask > 0.0, biased, -jnp.inf)
    ei = lax.broadcasted_iota(jnp.int32, cur.shape, 0)
    sel = jnp.zeros(cur.shape, F32)
    idxs, vals = [], []
    for _ in range(TOP_K):
        m = jnp.max(cur, axis=0, keepdims=True)
        idx = jnp.min(jnp.where(cur == m, ei, N_EXPERTS), axis=0, keepdims=True)
        pick = ei == idx
        idxs.append(idx)
        vals.append(jnp.sum(jnp.where(pick, scores, 0.0), axis=0, keepdims=True))
        sel = jnp.where(pick, 1.0, sel)
        cur = jnp.where(pick, -jnp.inf, cur)
    w = jnp.concatenate(vals, axis=0)
    eidx_ref[...] = jnp.concatenate(idxs, axis=0)
    ew_ref[...] = w / jnp.sum(w, axis=0, keepdims=True) * ROUTED_SCALE
    sel_ref[...] = _bf(sel)

    @pl.when(pl.program_id(0) == 0)
    def _():
        cnt_ref[...] = jnp.zeros_like(cnt_ref)

    cnt_ref[...] += jnp.sum(sel, axis=1, keepdims=True)


def _router(hffn, router, bias):
    n = hffn.shape[0]
    r_t = router.T
    r_hi = _bf(r_t)
    r_lo = _bf(r_t - r_hi.astype(F32))
    const = lambda a: pl.BlockSpec(a.shape, lambda i: (0,) * a.ndim)
    b_col = bias.reshape(N_EXPERTS, 1)
    return pl.pallas_call(
        _router_kernel,
        grid=(n // TM,),
        in_specs=[pl.BlockSpec((TM, D_MODEL), lambda i: (i, 0)), const(r_hi), const(r_lo), const(b_col)],
        out_specs=[pl.BlockSpec((N_EXPERTS, TM), lambda i: (0, i)),
                   pl.BlockSpec((TOP_K, TM), lambda i: (0, i)),
                   pl.BlockSpec((TOP_K, TM), lambda i: (0, i)),
                   pl.BlockSpec((N_EXPERTS, LANES), lambda i: (0, 0))],
        out_shape=[jax.ShapeDtypeStruct((N_EXPERTS, n), BF16),
                   jax.ShapeDtypeStruct((TOP_K, n), jnp.int32),
                   jax.ShapeDtypeStruct((TOP_K, n), F32),
                   jax.ShapeDtypeStruct((N_EXPERTS, LANES), F32)],
        compiler_params=_cparams(1),
        name="router",
    )(hffn, r_hi, r_lo, b_col)


def _positions_kernel(sel_ref, eidx_ref, base_ref, upper_ref, pos_ref, carry_ref):
    @pl.when(pl.program_id(0) == 0)
    def _():
        carry_ref[...] = jnp.zeros_like(carry_ref)

    sel = sel_ref[...]
    rank = jnp.dot(sel, upper_ref[...], preferred_element_type=F32)
    pos_e = base_ref[:, 0:1] + carry_ref[:, 0:1] + rank
    ei = lax.broadcasted_iota(jnp.int32, pos_e.shape, 0)
    eidx = eidx_ref[...]
    rows = [jnp.sum(jnp.where(ei == eidx[k:k + 1], pos_e, 0.0), axis=0, keepdims=True) for k in range(TOP_K)]
    pos_ref[...] = jnp.concatenate(rows, axis=0).astype(jnp.int32)
    carry_ref[...] += jnp.sum(sel.astype(F32), axis=1, keepdims=True)


def _positions(sel, eidx, base):
    n = sel.shape[1]
    pb = POS_TB
    upper = jnp.asarray(np.triu(np.ones((pb, pb), np.float32), 1), dtype=BF16)
    return pl.pallas_call(
        _positions_kernel,
        grid=(n // pb,),
        in_specs=[pl.BlockSpec((N_EXPERTS, pb), lambda i: (0, i)),
                  pl.BlockSpec((TOP_K, pb), lambda i: (0, i)),
                  pl.BlockSpec((N_EXPERTS, LANES), lambda i: (0, 0)),
                  pl.BlockSpec((pb, pb), lambda i: (0, 0))],
        out_specs=pl.BlockSpec((TOP_K, pb), lambda i: (0, i)),
        out_shape=jax.ShapeDtypeStruct((TOP_K, n), jnp.int32),
        scratch_shapes=[pltpu.VMEM((N_EXPERTS, LANES), F32)],
        compiler_params=_cparams(1),
        name="positions",
    )(sel, eidx, base, upper)


def _pack_pairs(x):
    half = x.shape[1] // 2
    bits = lax.bitcast_convert_type(_bf(x).astype(F32), jnp.uint32)
    return (bits[:, :half] >> 16) | (bits[:, half:] & jnp.uint32(0xFFFF0000))


def _unpack_pairs(w):
    lo = lax.bitcast_convert_type(w << 16, F32)
    hi = lax.bitcast_convert_type(w & jnp.uint32(0xFFFF0000), F32)
    return jnp.concatenate([_bf(lo), _bf(hi)], axis=1)


def _sc_gather(table, idx):
    b, w = idx.shape[0], table.shape[1]
    n_workers = SC_CORES * SC_SUBCORES
    per_w = b // n_workers
    assert b % (n_workers * SC_CHUNK) == 0
    mesh = plsc.VectorSubcoreMesh(core_axis_name="c", subcore_axis_name="s")

    n_chunk = per_w // SC_CHUNK
    assert n_chunk % 2 == 0

    @functools.partial(
        pl.kernel, mesh=mesh, out_type=jax.ShapeDtypeStruct((b, w), table.dtype),
        scratch_types=[pltpu.VMEM((n_chunk, SC_CHUNK), jnp.int32), pltpu.VMEM((2, SC_CHUNK, w), table.dtype),
                       pltpu.SemaphoreType.DMA((2,))])
    def gather(table_hbm, idx_hbm, out_hbm, idx_v, rows_v, sems):
        wid = lax.axis_index("s") * SC_CORES + lax.axis_index("c")
        base = wid * per_w
        pltpu.sync_copy(idx_hbm.at[wid], idx_v)

        def start(c, slot):
            pltpu.async_copy(table_hbm.at[idx_v.at[c]], rows_v.at[slot], sems.at[slot])

        def finish(c, slot):
            off = pl.multiple_of(base + c * SC_CHUNK, SC_CHUNK)
            pltpu.make_async_copy(table_hbm.at[idx_v.at[c]], rows_v.at[slot], sems.at[slot]).wait()
            pltpu.sync_copy(rows_v.at[slot], out_hbm.at[pl.ds(off, SC_CHUNK)])

        start(0, 0)

        @pl.loop(0, n_chunk, step=2)
        def _(c):
            start(c + 1, 1)
            finish(c, 0)

            @pl.when(c + 2 < n_chunk)
            def _():
                start(c + 2, 0)

            finish(c + 1, 1)

    return gather(table, idx.reshape(n_workers, n_chunk, SC_CHUNK))


def _sc_scatter(src, pos3, n_rows):
    n, w = src.shape
    n_workers = SC_CORES * SC_SUBCORES
    per_w = n // n_workers
    assert n % (n_workers * SC_CHUNK) == 0
    mesh = plsc.VectorSubcoreMesh(core_axis_name="c", subcore_axis_name="s")

    n_chunk = per_w // SC_CHUNK
    assert n_chunk % 2 == 0

    @functools.partial(
        pl.kernel, mesh=mesh, out_type=jax.ShapeDtypeStruct((n_rows, w), src.dtype),
        scratch_types=[pltpu.VMEM((2, TOP_K, SC_CHUNK), jnp.int32), pltpu.VMEM((2, SC_CHUNK, w), src.dtype),
                       pltpu.SemaphoreType.DMA((2,)), pltpu.SemaphoreType.DMA((2,))])
    def scatter(src_hbm, pos_hbm, out_hbm, idx_v, rows_v, ld_sems, sc_sems):
        wid = lax.axis_index("s") * SC_CORES + lax.axis_index("c")
        base = wid * per_w

        def loads(c, slot):
            off = pl.multiple_of(base + c * SC_CHUNK, SC_CHUNK)
            return (pltpu.make_async_copy(src_hbm.at[pl.ds(off, SC_CHUNK)], rows_v.at[slot], ld_sems.at[slot]),
                    pltpu.make_async_copy(pos_hbm.at[off // SC_CHUNK], idx_v.at[slot], ld_sems.at[slot]))

        def scatter_chunk(slot):
            copies = [pltpu.async_copy(rows_v.at[slot], out_hbm.at[idx_v.at[slot, k]], sc_sems.at[slot])
                      for k in range(TOP_K)]
            for cp in copies:
                cp.wait()

        def half_step(c, slot):
            for cp in loads(c, slot):
                cp.wait()

            @pl.when(c + 1 < n_chunk)
            def _():
                for cp in loads(c + 1, 1 - slot):
                    cp.start()

            scatter_chunk(slot)

        for cp in loads(0, 0):
            cp.start()

        @pl.loop(0, n_chunk, step=2)
        def _(c):
            half_step(c, 0)
            half_step(c + 1, 1)

    return scatter(src, pos3)


def _experts_kernel(te_ref, nu_ref, xs_ref, wg_ref, wu_ref, wd_ref, ys_ref, wg_bf, wu_bf, wd_bf):
    i = pl.program_id(0)
    active = i < nu_ref[0]

    @pl.when(jnp.logical_and(active, jnp.logical_or(i == 0, te_ref[i] != te_ref[jnp.maximum(i - 1, 0)])))
    def _():
        wg_bf[...] = _bf(wg_ref[0])
        wu_bf[...] = _bf(wu_ref[0])
        wd_bf[...] = _bf(wd_ref[0])

    @pl.when(active)
    def _():
        x = _unpack_pairs(xs_ref[...])
        act = _glu(x, wg_bf[...], wu_bf[...])
        ys_ref[...] = _pack_pairs(jnp.dot(_bf(act), wd_bf[...], preferred_element_type=F32))


def _experts(xs, tile_expert, n_used, mp, layer):
    n_tiles = xs.shape[0] // MOE_TILE
    half = D_MODEL // 2
    wspec = lambda shape: pl.BlockSpec((None, 1) + shape, lambda i, te, nu: (layer, te[i], 0, 0))
    return pl.pallas_call(
        _experts_kernel,
        grid_spec=pltpu.PrefetchScalarGridSpec(
            num_scalar_prefetch=2, grid=(n_tiles,),
            in_specs=[pl.BlockSpec((MOE_TILE, half), lambda i, te, nu: (jnp.minimum(i, nu[0] - 1), 0)),
                      wspec((D_MODEL, D_EXPERT)), wspec((D_MODEL, D_EXPERT)), wspec((D_EXPERT, D_MODEL))],
            out_specs=pl.BlockSpec((MOE_TILE, half), lambda i, te, nu: (jnp.minimum(i, nu[0] - 1), 0)),
            scratch_shapes=[pltpu.VMEM((D_MODEL, D_EXPERT), BF16), pltpu.VMEM((D_MODEL, D_EXPERT), BF16),
                            pltpu.VMEM((D_EXPERT, D_MODEL), BF16)]),
        out_shape=jax.ShapeDtypeStruct(xs.shape, jnp.uint32),
        compiler_params=_cparams(1),
        name="experts",
    )(tile_expert, n_used, xs, mp['wg'], mp['wu'], mp['wd'])


def _combine_kernel(h_ref, *refs):
    yg_refs = refs[:TOP_K]
    ew_ref, eye_ref, sg_ref, su_ref, sd_ref, x1_ref, g2_ref, o_ref = refs[TOP_K:]
    act = _glu(h_ref[...], sg_ref[...], su_ref[...])
    acc = jnp.dot(_bf(act), sd_ref[...], preferred_element_type=F32)
    ew = ew_ref[...]
    hi = _bf(ew)
    lo = _bf(ew - hi.astype(F32))
    ew_t = _dot_tn(hi, eye_ref[...]) + _dot_tn(lo, eye_ref[...])
    for k in range(TOP_K):
        acc = acc + ew_t[:, k:k + 1] * _unpack_pairs(yg_refs[k][...]).astype(F32)
    o_ref[...] = x1_ref[...] + g2_ref[...] * acc


def _combine(rows, hffn, yg, ew, mp, x1, mod4, layer, row0, n_out):
    half = D_MODEL // 2
    n_blk = rows.n // TM
    blk0 = row0 // TM
    const = lambda a: pl.BlockSpec(a.shape, lambda i: (0,) * a.ndim)
    tok = lambda w: pl.BlockSpec((TM, w), lambda i: (blk0 + i, 0))
    slot = lambda k: pl.BlockSpec((TM, half), lambda i: (k * n_blk + blk0 + i, 0))
    eye = jnp.eye(TOP_K, dtype=BF16)
    return pl.pallas_call(
        _combine_kernel,
        grid=(n_out // TM,),
        in_specs=[tok(D_MODEL)] + [slot(k) for k in range(TOP_K)]
                 + [pl.BlockSpec((TOP_K, TM), lambda i: (0, blk0 + i)), const(eye),
                    const(mp['sg']), const(mp['su']), const(mp['sd']), tok(D_MODEL),
                    rows.mod_spec(layer, 5, TM, blk0)],
        out_specs=pl.BlockSpec((TM, D_MODEL), lambda i: (i, 0)),
        out_shape=jax.ShapeDtypeStruct((n_out, D_MODEL), F32),
        compiler_params=_cparams(1),
        name=f"combine{layer}",
    )(hffn, *([yg] * TOP_K), ew, eye, mp['sg'], mp['su'], mp['sd'], x1, mod4)


def _moe(rows, hffn, hpack, x1, router, bias, mp, mod4, layer, out_ranges):
    n = rows.n
    sel, eidx, ew, cnt = _router(hffn, router, bias)
    counts = cnt[:, 0].astype(jnp.int32)
    padded = (counts + MOE_TILE - 1) // MOE_TILE * MOE_TILE
    ends = jnp.cumsum(padded)
    n_rows = n * TOP_K + N_EXPERTS * MOE_TILE
    n_tiles = n_rows // MOE_TILE
    base = jnp.broadcast_to((ends - padded).astype(F32)[:, None], (N_EXPERTS, LANES))
    tile_start = jnp.arange(n_tiles, dtype=jnp.int32) * MOE_TILE
    tile_expert = jnp.minimum(jnp.sum((ends[None, :] <= tile_start[:, None]).astype(jnp.int32), axis=1),
                              N_EXPERTS - 1)
    n_used = (ends[-1:] // MOE_TILE).astype(jnp.int32)
    pos = _positions(sel, eidx, base)
    pos3 = pos.reshape(TOP_K, n // SC_CHUNK, SC_CHUNK).transpose(1, 0, 2)
    xs = _sc_scatter(hpack, pos3, n_rows)
    ys = _experts(xs, tile_expert, n_used, mp, layer)
    yg = _sc_gather(ys, pos.reshape(-1))
    return [_combine(rows, hffn, yg, ew, mp, x1, mod4, layer, row0, n_out) for row0, n_out in out_ranges]


def _glu(x, wg, wu):
    hg = jnp.dot(x, wg, preferred_element_type=F32)
    hu = jnp.dot(x, wu, preferred_element_type=F32)
    return _silu(hg) * hu


def kernel(x_prompt, x_sample, c, c_ctx, cache_attn_k, cache_attn_v, state_rwkv_fwd, state_rwkv_bwd,
           state_hgrn_fwd, state_hgrn_bwd, norm1_g, norm2_g, mod_w, mod_b, ab_w_in, ab_w_out, attn_q_norm,
           attn_k_norm, attn_sink, rwkv_mu, rwkv_w0, rwkv_w2, rwkv_a0, rwkv_a2, rwkv_g2, rwkv_k_k, rwkv_k_a,
           rwkv_r_k, rwkv_ln_w, rwkv_ln_b, hgrn_w_in, hgrn_w_out, hgrn_lower_bounds, hgrn_norm_g, moe_router,
           moe_bias, moe_w_gate, moe_w_up, moe_w_down, moe_shared_gate, moe_shared_up, moe_shared_down):
    n_cseq, cseq, _ = x_prompt.shape
    n_lseq, lseq, _ = x_sample.shape
    depth = mod_w.shape[0]
    assert depth == 2 and n_lseq + 1 <= SUBLANES
    assert cseq == PREP_TM and lseq % TM == 0 and lseq % HG_TB == 0 and cseq % HG_TB == 0
    assert (n_cseq * cseq) % TM == 0
    assert n_cseq % RW_BB == 0 and n_lseq % RW_BB == 0 and (n_cseq * cseq) % (lseq * RW_BB) == 0
    rows = _Rows(n_cseq * cseq, n_lseq * lseq, lseq)
    assert rows.n % MOE_TILE == 0 and lseq % MOE_TILE == 0 and rows.n_ctx % MOE_TILE == 0
    kv_w = KV_A * HD_A

    xs = (x_prompt.reshape(rows.n_ctx, D_MODEL), x_sample.reshape(rows.n_lat, D_MODEL))
    cvecs = jnp.concatenate([c_ctx[None, :], c, jnp.zeros((SUBLANES - 1 - n_lseq, D_MODEL), F32)], axis=0)
    mod4 = _modulation(cvecs, mod_w, mod_b).reshape(depth, SUBLANES, 1, 6 * D_MODEL)

    ones_q = _block_ones(W_A, HD_A)
    ones_k = _block_ones(kv_w, HD_A)
    ones_b = _block_ones(W_B, HD_B)
    ones_pair = _block_ones(LANES, HD_B)[:LANES]
    ones_pair = jnp.kron(jnp.eye(2, dtype=BF16), ones_pair)
    cos_t, sin_t = _rope_tables(lseq)

    def moe(l, hffn, hpack, x1, out_ranges):
        mp = {'wg': moe_w_gate, 'wu': moe_w_up, 'wd': moe_w_down,
              'sg': _bf(moe_shared_gate[l]), 'su': _bf(moe_shared_up[l]), 'sd': _bf(moe_shared_down[l])}
        return _moe(rows, hffn, hpack, x1, moe_router[l], moe_bias[l], mp, mod4, l, out_ranges)

    assert W_A == W_B
    all_rows = jnp.zeros((rows.n, W_B), F32)

    pr = {'mu': rwkv_mu[0], 'w0': rwkv_w0[0], 'w2': rwkv_w2[0], 'a0': rwkv_a0[0], 'a2': rwkv_a2[0],
          'g2': rwkv_g2[0], 'k_k': rwkv_k_k[0], 'k_a': rwkv_k_a[0], 'r_k': rwkv_r_k[0].reshape(-1),
          'ln_w': rwkv_ln_w[0], 'ln_b': rwkv_ln_b[0]}
    p_att, p_rw = _inproj(rows, xs, norm1_g[0], mod4, 0, _bf(ab_w_in[0]), (ATT_IN, RWKV_IN), TM)
    qg_t = jnp.tile(attn_q_norm[0], H_A).reshape(1, W_A)
    kg_t = jnp.tile(attn_k_norm[0], KV_A).reshape(1, kv_w)
    o_att, new_k, new_v = _ctx_attention(p_att, n_cseq, cseq, qg_t, kg_t, attn_sink[0], ones_q, ones_k, all_rows)
    past = cache_attn_k.shape[2]
    o_att = _lat_attention(p_att, rows.n_ctx // lseq, n_lseq, lseq, qg_t, kg_t, attn_sink[0], ones_q, ones_k,
                           cos_t, sin_t, cache_attn_k[:, 0].reshape(n_lseq, past, kv_w),
                           cache_attn_v[:, 0].reshape(n_lseq, past, kv_w), o_att)

    pp = _rwkv_prep(rows, p_rw, pr, ones_b)
    zero_st = jnp.zeros((n_cseq, H_B // 2, HD_B, LANES), F32)
    o_f, o_b, sf_c, sb_c = _rwkv_scan(pp, 0, n_cseq, cseq, zero_st, zero_st, ones_pair, all_rows, all_rows)
    o_f, o_b, _, _ = _rwkv_scan(pp, rows.n_ctx, n_lseq, lseq, _state_to_pairs(state_rwkv_fwd[:, 0]),
                                _state_to_pairs(state_rwkv_bwd[:, 0]), ones_pair, o_f, o_b)
    x1, hffn, hpack = _outproj0(rows, xs, o_att, o_f, o_b, pp, pr, ones_b, _bf(ab_w_out[0]), norm2_g[0], mod4, 0)
    (x,) = moe(0, hffn, hpack, x1, [(0, rows.n)])

    (p1,) = _inproj(rows, (x, x), norm1_g[1], mod4, 1, _bf(hgrn_w_in[0]), (IN_C,), IN1_TM)
    zero_h = jnp.zeros((n_cseq, H_C, DK_C, DV_C), F32)
    all_rows_c = jnp.zeros((rows.n, D_C), F32)
    h_f, h_b, hsf_c, hsb_c = _hgrn_scan(p1, 0, n_cseq, cseq, hgrn_lower_bounds, zero_h, zero_h,
                                        all_rows_c, all_rows_c)
    h_f, h_b, _, _ = _hgrn_scan(p1, rows.n_ctx, n_lseq, lseq, hgrn_lower_bounds,
                                state_hgrn_fwd[:, 0], state_hgrn_bwd[:, 0], h_f, h_b)
    x1, hffn, hpack = _outproj1(rows, (x, x), h_f, h_b, p1, hgrn_norm_g[0], _bf(hgrn_w_out[0]), norm2_g[1],
                                mod4, 1)
    y_c, y_l = moe(1, hffn, hpack, x1, [(0, rows.n_ctx), (rows.n_ctx, rows.n_lat)])

    y_prompt = y_c.reshape(n_cseq, cseq, D_MODEL)
    y_sample = y_l.reshape(n_lseq, lseq, D_MODEL)
    return (y_prompt, y_sample,
            new_k.reshape(n_cseq, 1, cseq, KV_A, HD_A), new_v.reshape(n_cseq, 1, cseq, KV_A, HD_A),
            _pairs_to_state(sf_c)[:, None], _pairs_to_state(sb_c)[:, None],
            hsf_c[:, None], hsb_c[:, None])
```

```python
import functools

import numpy as np
import jax
import jax.numpy as jnp
from jax import lax
from jax.experimental import pallas as pl
from jax.experimental.pallas import tpu as pltpu
from jax.experimental.pallas import tpu_sc as plsc

F32 = jnp.float32
BF16 = jnp.bfloat16

D_MODEL = 1024
GRID_W = 64
H_A = 8
KV_A = 2
G_A = H_A // KV_A
HD_A = 64
W_A = H_A * HD_A
WINDOW = 128
QBLK = 128
ROPE_BASE = 10000.0
ROPE_PAIR = HD_A // 4
ATTN_SCALE = HD_A ** -0.5
NEG_INF = -1e30
H_B = 8
HD_B = 64
W_B = H_B * HD_B
LORA_W = 64
LORA_A = 64
LORA_G = 128
GN_EPS = 64e-5
ATT_IN = W_A + 2 * KV_A * HD_A
RWKV_IN = 3 * W_B + LORA_W + LORA_A + LORA_G
IN_AB = ATT_IN + RWKV_IN
H_C = 8
DK_C = 128
DV_C = 128
D_C = H_C * DV_C
CHUNK = 64
IN_C = 5 * D_C
N_EXPERTS = 64
TOP_K = 8
N_GROUPS = 8
TOPK_GROUPS = 4
D_EXPERT = 256
ROUTED_SCALE = 2.5
EPS = 1e-6

LANES = 128
SUBLANES = 8
VMEM_LIMIT = 52 * 1024 * 1024

TM = 512
PREP_TM = 256
IN1_TM = 512
RW_TB = 128
RW_BB = 4
RW_GROUP_BB = 4
HG_TB = 256
MOE_TILE = 1024
POS_TB = 512
SC_CORES = 2
SC_SUBCORES = 16
SC_CHUNK = 64


def _cparams(n_axes):
    return pltpu.CompilerParams(dimension_semantics=("arbitrary",) * n_axes,
                                vmem_limit_bytes=VMEM_LIMIT)


def _bf(x):
    return x.astype(BF16)


def _split2(x):
    hi = lax.bitcast_convert_type(
        lax.bitcast_convert_type(x, jnp.uint32) & jnp.uint32(0xFFFF0000), F32)
    return hi, x - hi


def _seg_sum(x, ones2):
    hi, lo = _split2(x)
    return jnp.dot(jnp.concatenate([_bf(hi), _bf(lo)], axis=1), ones2,
                   preferred_element_type=F32)


def _dot_nt(a, b):
    return lax.dot_general(a, b, (((1,), (1,)), ((), ())), preferred_element_type=F32)


def _dot_tn(a, b):
    return lax.dot_general(a, b, (((0,), (0,)), ((), ())), preferred_element_type=F32)


def _sigmoid(x):
    return 1.0 / (1.0 + jnp.exp(-x))


def _silu(x):
    return x * _sigmoid(x)


def _chunks(seq, n):
    seq = list(seq)
    return [seq[i:i + n] for i in range(0, len(seq), n)]


def _block_ones(width, seg):
    idx = np.arange(width) // seg
    bd = (idx[:, None] == idx[None, :]).astype(np.float32)
    return jnp.asarray(np.concatenate([bd, bd], axis=0), dtype=BF16)


def _mod_kernel(c_ref, w_ref, b_ref, o_ref):
    s = _silu(c_ref[...])
    o_ref[0] = jnp.dot(_bf(s), _bf(w_ref[0]), preferred_element_type=F32) + b_ref[0]


def _modulation(cvecs, mod_w, mod_b):
    depth = mod_w.shape[0]
    n_col = 6 * D_MODEL // D_MODEL
    return pl.pallas_call(
        _mod_kernel,
        grid=(depth, n_col),
        in_specs=[pl.BlockSpec((SUBLANES, D_MODEL), lambda l, j: (0, 0)),
                  pl.BlockSpec((1, D_MODEL, D_MODEL), lambda l, j: (l, 0, j)),
                  pl.BlockSpec((1, 1, D_MODEL), lambda l, j: (l, 0, j))],
        out_specs=pl.BlockSpec((1, SUBLANES, D_MODEL), lambda l, j: (l, 0, j)),
        out_shape=jax.ShapeDtypeStruct((depth, SUBLANES, 6 * D_MODEL), F32),
        compiler_params=_cparams(2),
        name="modulation",
    )(cvecs, mod_w, mod_b.reshape(depth, 1, 6 * D_MODEL))


class _Rows:
    def __init__(self, n_ctx, n_lat, lat_seq):
        self.n_ctx, self.n_lat, self.lat_seq = n_ctx, n_lat, lat_seq
        self.n = n_ctx + n_lat

    def mod_row(self, i, tm):
        nctx_blk = self.n_ctx // tm
        per_seq = self.lat_seq // tm
        return jnp.where(i < nctx_blk, 0, 1 + (i - nctx_blk) // per_seq)

    def mod_spec(self, layer, chunk, tm, blk0=0):
        return pl.BlockSpec((None, None, 1, D_MODEL),
                            lambda i, *_: (layer, self.mod_row(i + blk0, tm), 0, chunk))


def _rms_mod(x, g, sc, sh):
    ms = jnp.mean(x * x, axis=-1, keepdims=True)
    return x * lax.rsqrt(ms + EPS) * g * (1.0 + sc) + sh


def _x_specs(rows, xs, tm=TM):
    xa, xb = xs
    nctx_blk = rows.n_ctx // tm
    lat0 = nctx_blk if xb.shape[0] == rows.n else 0
    return [pl.BlockSpec((tm, D_MODEL), lambda i: (jnp.minimum(i, nctx_blk - 1), 0)),
            pl.BlockSpec((tm, D_MODEL), lambda i: (jnp.maximum(i - nctx_blk, 0) + lat0, 0))]


def _pick_x(rows, xa_ref, xb_ref):
    return jnp.where(pl.program_id(0) < rows.n_ctx // xa_ref.shape[0], xa_ref[...], xb_ref[...])


def _inproj_kernel(rows, splits, xa_ref, xb_ref, g_ref, sh_ref, sc_ref, w_ref, *o_refs):
    h = _rms_mod(_pick_x(rows, xa_ref, xb_ref), g_ref[...], sc_ref[...], sh_ref[...])
    p = jnp.dot(_bf(h), w_ref[...], preferred_element_type=F32)
    lo = 0
    for o_ref, width in zip(o_refs, splits):
        o_ref[...] = p[:, lo:lo + width]
        lo += width


def _inproj(rows, xs, g, mod4, layer, w_bf, splits, tm):
    n_out = w_bf.shape[1]
    return pl.pallas_call(
        functools.partial(_inproj_kernel, rows, splits),
        grid=(rows.n // tm,),
        in_specs=_x_specs(rows, xs, tm) + [
            pl.BlockSpec((1, D_MODEL), lambda i: (0, 0)),
            rows.mod_spec(layer, 0, tm),
            rows.mod_spec(layer, 1, tm),
            pl.BlockSpec((D_MODEL, n_out), lambda i: (0, 0), pipeline_mode=pl.Buffered(1))],
        out_specs=[pl.BlockSpec((tm, wd), lambda i: (i, 0)) for wd in splits],
        out_shape=[jax.ShapeDtypeStruct((rows.n, wd), F32) for wd in splits],
        compiler_params=_cparams(1),
        name=f"inproj{layer}",
    )(*xs, g.reshape(1, D_MODEL), mod4, mod4, w_bf)


def _head_rms(x, gain_t, ones2):
    ms = _seg_sum(x * x, ones2) * (1.0 / HD_A)
    return x * lax.rsqrt(ms + EPS) * gain_t


def _sink_softmax_pv(parts, sink):
    m = jnp.maximum(functools.reduce(jnp.maximum, [jnp.max(s, axis=-1, keepdims=True) for s, _ in parts]), sink)
    den = jnp.exp(sink - m)
    acc = None
    for s, v in parts:
        p = jnp.exp(s - m)
        den = den + jnp.sum(p, axis=-1, keepdims=True)
        pv = jnp.dot(_bf(p), v, preferred_element_type=F32)
        acc = pv if acc is None else acc + pv
    return acc / den


def _ctx_attn_kernel(p_ref, qg_ref, kg_ref, sink_ref, ones_q_ref, ones_k_ref, prev_ref, o_ref, k_ref, v_ref):
    del prev_ref
    p = p_ref[...]
    q = _head_rms(p[:, :W_A], qg_ref[...], ones_q_ref[...]) * ATTN_SCALE
    k = _head_rms(p[:, W_A:W_A + KV_A * HD_A], kg_ref[...], ones_k_ref[...])
    v = p[:, W_A + KV_A * HD_A:ATT_IN]
    k_ref[0] = k
    v_ref[0] = v
    qb, kb, vb = _bf(q), _bf(k), _bf(v)
    outs = []
    for h in range(H_A):
        j = h // G_A
        s = _dot_nt(qb[:, h * HD_A:(h + 1) * HD_A], kb[:, j * HD_A:(j + 1) * HD_A])
        outs.append(_sink_softmax_pv([(s, vb[:, j * HD_A:(j + 1) * HD_A])], sink_ref[h]))
    o_ref[...] = jnp.concatenate(outs, axis=1)


def _ctx_attention(p_att, n_seq, seq, qg_t, kg_t, sink, ones_q, ones_k, prev):
    kv_w = KV_A * HD_A
    return pl.pallas_call(
        _ctx_attn_kernel,
        grid=(n_seq,),
        in_specs=[pl.BlockSpec((seq, ATT_IN), lambda b: (b, 0)),
                  pl.BlockSpec((1, W_A), lambda b: (0, 0)),
                  pl.BlockSpec((1, kv_w), lambda b: (0, 0)),
                  pl.BlockSpec(memory_space=pltpu.SMEM),
                  pl.BlockSpec(ones_q.shape, lambda b: (0, 0)),
                  pl.BlockSpec(ones_k.shape, lambda b: (0, 0)),
                  pl.BlockSpec(memory_space=pl.ANY)],
        out_specs=[pl.BlockSpec((seq, W_A), lambda b: (b, 0)),
                   pl.BlockSpec((1, seq, kv_w), lambda b: (b, 0, 0)),
                   pl.BlockSpec((1, seq, kv_w), lambda b: (b, 0, 0))],
        input_output_aliases={6: 0},
        out_shape=[jax.ShapeDtypeStruct(prev.shape, F32),
                   jax.ShapeDtypeStruct((n_seq, seq, kv_w), F32),
                   jax.ShapeDtypeStruct((n_seq, seq, kv_w), F32)],
        compiler_params=_cparams(1),
        name="ctx_attention",
    )(p_att, qg_t, kg_t, sink, ones_q, ones_k, prev)


def _rope(x, cos_t, sin_t):
    lane = lax.broadcasted_iota(jnp.int32, cos_t.shape, 1)
    low = (lane % (2 * ROPE_PAIR)) < ROPE_PAIR
    outs = []
    for s in range(x.shape[1] // LANES):
        xs = x[:, s * LANES:(s + 1) * LANES]
        partner = jnp.where(low, pltpu.roll(xs, LANES - ROPE_PAIR, 1), pltpu.roll(xs, ROPE_PAIR, 1))
        outs.append(xs * cos_t + partner * sin_t)
    return outs[0] if len(outs) == 1 else jnp.concatenate(outs, axis=1)


def _lat_attn_kernel(seq, p_ref, qg_ref, kg_ref, sink_ref, ones_q_ref, ones_k_ref, cos_ref, sin_ref,
                     kc_ref, vc_ref, prev_ref, o_ref, q_scr, k_scr, v_scr):
    del prev_ref
    kv_w = KV_A * HD_A
    p = p_ref[...]
    q = _head_rms(p[:, :W_A], qg_ref[...], ones_q_ref[...])
    k = _head_rms(p[:, W_A:W_A + kv_w], kg_ref[...], ones_k_ref[...])
    qr = _bf(_rope(q, cos_ref[...], sin_ref[...]) * ATTN_SCALE)
    kr = _bf(_rope(k, cos_ref[...], sin_ref[...]))
    vb = _bf(p[:, W_A + kv_w:ATT_IN])
    for h in range(H_A):
        q_scr[h] = qr[:, h * HD_A:(h + 1) * HD_A]
    for j in range(KV_A):
        k_scr[j] = kr[:, j * HD_A:(j + 1) * HD_A]
        v_scr[j] = vb[:, j * HD_A:(j + 1) * HD_A]
    kc = _bf(kc_ref[0])
    vc = _bf(vc_ref[0])
    n_local = 3 * QBLK
    grp = lax.broadcasted_iota(jnp.int32, (G_A * QBLK, 1), 0) // QBLK

    def block(i, carry):
        q0 = pl.multiple_of(i * QBLK, QBLK)
        start = pl.multiple_of(jnp.clip((i - 1) * QBLK, 0, seq - n_local), QBLK)
        ipos = q0 + lax.broadcasted_iota(jnp.int32, (G_A * QBLK, n_local), 0) % QBLK
        jpos = start + lax.broadcasted_iota(jnp.int32, (G_A * QBLK, n_local), 1)
        band = jnp.abs(jpos - ipos) <= WINDOW
        outs = []
        for j in range(KV_A):
            qs = jnp.concatenate([q_scr[j * G_A + g, pl.ds(q0, QBLK), :] for g in range(G_A)], axis=0)
            kl = k_scr[j, pl.ds(start, n_local), :]
            vl = v_scr[j, pl.ds(start, n_local), :]
            sink = jnp.zeros((G_A * QBLK, 1), F32)
            for g in range(G_A):
                sink = jnp.where(grp == g, sink_ref[j * G_A + g], sink)
            s_loc = jnp.where(band, _dot_nt(qs, kl), NEG_INF)
            s_ctx = _dot_nt(qs, kc[:, j * HD_A:(j + 1) * HD_A])
            o = _sink_softmax_pv([(s_loc, vl), (s_ctx, vc[:, j * HD_A:(j + 1) * HD_A])], sink)
            outs.extend(o[g * QBLK:(g + 1) * QBLK] for g in range(G_A))
        o_ref[pl.ds(q0, QBLK), :] = jnp.concatenate(outs, axis=1)
        return carry

    lax.fori_loop(0, seq // QBLK, block, 0)


def _lat_attention(p_att, row_blk0, n_seq, seq, qg_t, kg_t, sink, ones_q, ones_k, cos_t, sin_t, kc, vc, prev):
    kv_w = KV_A * HD_A
    past = kc.shape[1]
    return pl.pallas_call(
        functools.partial(_lat_attn_kernel, seq),
        grid=(n_seq,),
        in_specs=[pl.BlockSpec((seq, ATT_IN), lambda b: (row_blk0 + b, 0)),
                  pl.BlockSpec((1, W_A), lambda b: (0, 0)),
                  pl.BlockSpec((1, kv_w), lambda b: (0, 0)),
                  pl.BlockSpec(memory_space=pltpu.SMEM),
                  pl.BlockSpec(ones_q.shape, lambda b: (0, 0)),
                  pl.BlockSpec(ones_k.shape, lambda b: (0, 0)),
                  pl.BlockSpec((seq, LANES), lambda b: (0, 0)),
                  pl.BlockSpec((seq, LANES), lambda b: (0, 0)),
                  pl.BlockSpec((1, past, kv_w), lambda b: (b, 0, 0)),
                  pl.BlockSpec((1, past, kv_w), lambda b: (b, 0, 0)),
                  pl.BlockSpec(memory_space=pl.ANY)],
        out_specs=pl.BlockSpec((seq, W_A), lambda b: (row_blk0 + b, 0)),
        out_shape=jax.ShapeDtypeStruct(prev.shape, F32),
        input_output_aliases={10: 0},
        scratch_shapes=[pltpu.VMEM((H_A, seq, HD_A), BF16), pltpu.VMEM((KV_A, seq, HD_A), BF16),
                        pltpu.VMEM((KV_A, seq, HD_A), BF16)],
        compiler_params=_cparams(1),
        name="lat_attention",
    )(p_att, qg_t, kg_t, sink, ones_q, ones_k, cos_t, sin_t, kc, vc, prev)


def _rope_tables(seq):
    pos = np.arange(seq)
    row = (pos // GRID_W).astype(np.float32)
    col = (pos % GRID_W).astype(np.float32)
    d_axis = HD_A // 2
    inv = (ROPE_BASE ** (-np.arange(0, d_axis, 2, dtype=np.float32) / d_axis)).astype(np.float32)
    cos_h = np.zeros((seq, HD_A), np.float32)
    sin_h = np.zeros((seq, HD_A), np.float32)
    for seg, p_ in enumerate((row, col)):
        ang = (p_[:, None] * inv[None, :]).astype(np.float32)
        c, s = np.cos(ang), np.sin(ang)
        base = seg * d_axis
        cos_h[:, base:base + d_axis // 2] = c
        cos_h[:, base + d_axis // 2:base + d_axis] = c
        sin_h[:, base:base + d_axis // 2] = -s
        sin_h[:, base + d_axis // 2:base + d_axis] = s
    rep = LANES // HD_A
    return jnp.asarray(np.tile(cos_h, (1, rep))), jnp.asarray(np.tile(sin_h, (1, rep)))


def _rwkv_prep_kernel(rows, x_ref, prev_ref, next_ref, mu_ref, kk_ref, ka_ref, rk_ref, w0_ref, w2_ref,
                      a0_ref, a2_ref, g2_ref, ones_ref,
                      nkk_ref, r_ref, v_ref, g_ref, bonus_ref,
                      wf_ref, kaf_ref, kdf_ref, wb_ref, kab_ref, kdb_ref):
    i = pl.program_id(0)
    nctx_blk = rows.n_ctx // PREP_TM
    per_seq = rows.lat_seq // PREP_TM
    is_ctx = i < nctx_blk
    first = jnp.logical_or(is_ctx, (i - nctx_blk) % per_seq == 0)
    last = jnp.logical_or(is_ctx, (i - nctx_blk) % per_seq == per_seq - 1)
    x = x_ref[...]
    ridx = lax.broadcasted_iota(jnp.int32, x.shape, 0)
    prev_row = jnp.where(first, 0.0, prev_ref[SUBLANES - 1:SUBLANES, :])
    next_row = jnp.where(last, 0.0, next_ref[0:1, :])
    xm1 = jnp.where(ridx == 0, prev_row, pltpu.roll(x, 1, 0))
    xp1 = jnp.where(ridx == PREP_TM - 1, next_row, pltpu.roll(x, PREP_TM - 1, 0))
    pw = x + (0.5 * (xm1 + xp1) - x) * mu_ref[...]

    r = pw[:, 0:W_B]
    k = pw[:, W_B:2 * W_B]
    v = pw[:, 2 * W_B:3 * W_B]
    wd = pw[:, 3 * W_B:3 * W_B + LORA_W]
    ad = pw[:, 3 * W_B + LORA_W:3 * W_B + LORA_W + LORA_A]
    gd = pw[:, 3 * W_B + LORA_W + LORA_A:]
    ones2 = ones_ref[...]

    kk = k * kk_ref[...]
    kk = kk / jnp.maximum(jnp.sqrt(_seg_sum(kk * kk, ones2)), 1e-12)
    nkk_ref[...] = -kk
    r_ref[...] = r
    v_ref[...] = v
    g_ref[...] = jnp.dot(_bf(_sigmoid(gd)), g2_ref[...], preferred_element_type=F32)
    tw = _bf(jnp.tanh(wd))
    adb = _bf(ad)
    bonus = jnp.zeros_like(r)
    for d, (w_o, ka_o, kd_o) in enumerate(((wf_ref, kaf_ref, kdf_ref), (wb_ref, kab_ref, kdb_ref))):
        z = -(w0_ref[d:d + 1, :] + jnp.dot(tw, w2_ref[d], preferred_element_type=F32))
        softplus = jnp.maximum(z, 0.0) + jnp.log(1.0 + jnp.exp(-jnp.abs(z)))
        w_o[...] = jnp.exp(-jnp.exp(-softplus - 0.5))
        a = _sigmoid(a0_ref[d:d + 1, :] + jnp.dot(adb, a2_ref[d], preferred_element_type=F32))
        kd = k * (1.0 + (a - 1.0) * ka_ref[...])
        ka_o[...] = kk * a
        kd_o[...] = kd
        bonus = bonus + _seg_sum(r * kd * rk_ref[...], ones2) * v
    bonus_ref[...] = bonus


def _rwkv_prep(rows, p_rw, pr, ones_b):
    n = rows.n
    n_halo = n // SUBLANES
    blk_halo = PREP_TM // SUBLANES
    row = lambda a: a.reshape(1, -1)
    full = lambda a: pl.BlockSpec(a.shape, lambda i: (0,) * a.ndim)
    consts = [row(pr['mu']), row(pr['k_k']), row(pr['k_a']), row(pr['r_k']), pr['w0'], _bf(pr['w2']),
              pr['a0'], _bf(pr['a2']), _bf(pr['g2']), ones_b]
    outs = pl.pallas_call(
        functools.partial(_rwkv_prep_kernel, rows),
        grid=(n // PREP_TM,),
        in_specs=[pl.BlockSpec((PREP_TM, RWKV_IN), lambda i: (i, 0)),
                  pl.BlockSpec((SUBLANES, RWKV_IN), lambda i: (jnp.maximum(i * blk_halo - 1, 0), 0)),
                  pl.BlockSpec((SUBLANES, RWKV_IN), lambda i: (jnp.minimum((i + 1) * blk_halo, n_halo - 1), 0))]
                 + [full(a) for a in consts],
        out_specs=[pl.BlockSpec((PREP_TM, W_B), lambda i: (i, 0))] * 11,
        out_shape=[jax.ShapeDtypeStruct((n, W_B), F32)] * 11,
        compiler_params=_cparams(1),
        name="rwkv_prep",
    )(p_rw, p_rw, p_rw, *consts)
    names = ('nkk', 'r', 'v', 'g', 'bonus', 'w_f', 'ka_f', 'kd_f', 'w_b', 'ka_b', 'kd_b')
    return dict(zip(names, outs))


def _rwkv_scan_kernel(n_tb, nkkf_ref, rf_ref, vf_ref, wf_ref, kaf_ref, kdf_ref,
                      nkkb_ref, rb_ref, vb_ref, wb_ref, kab_ref, kdb_ref,
                      s0f_ref, s0b_ref, ones_ref, prevf_ref, prevb_ref,
                      of_ref, ob_ref, sff_ref, sfb_ref, sf_scr, sb_scr, vt_scr):
    del prevf_ref, prevb_ref
    s_scr = (sf_scr, sb_scr)
    tb = pl.program_id(1)
    n_pair = H_B // 2
    half = RW_TB // 2
    dirs = ((nkkf_ref, rf_ref, vf_ref, wf_ref, kaf_ref, kdf_ref, of_ref, False),
            (nkkb_ref, rb_ref, vb_ref, wb_ref, kab_ref, kdb_ref, ob_ref, True))

    @pl.when(tb == 0)
    def _():
        sf_scr[...] = s0f_ref[...]
        sb_scr[...] = s0b_ref[...]

    lane = lax.broadcasted_iota(jnp.int32, (HD_B, LANES), 1)
    for d, refs in enumerate(dirs):
        v_ref = refs[2]
        for bb in range(RW_BB):
            for p in range(n_pair):
                vt = v_ref[bb, :, p * LANES:(p + 1) * LANES].T
                top, bot = vt[:HD_B], vt[HD_B:]
                for s in range(2):
                    if s == 0:
                        t2 = jnp.where(lane < HD_B, top, pltpu.roll(bot, HD_B, 1))
                    else:
                        t2 = jnp.where(lane < HD_B, pltpu.roll(top, HD_B, 1), bot)
                    vt_scr[d, bb, p, s] = t2

    ones2 = ones_ref[...]
    row8 = lax.broadcasted_iota(jnp.int32, (SUBLANES, LANES), 0)
    lane8 = lax.broadcasted_iota(jnp.int32, (SUBLANES, LANES), 1)
    sel_r = jnp.logical_or(jnp.logical_and(row8 % 2 == 0, lane8 < HD_B),
                           jnp.logical_and(row8 % 2 == 1, lane8 >= HD_B))

    def row_of(rev, tt):
        return RW_TB - 1 - tt if rev else tt

    def emit_output(d, bb, tau):
        r_ref, o_ref = dirs[d][1], dirs[d][6]
        r = r_ref[bb, pl.ds(tau, 1), :]
        r8 = jnp.zeros((SUBLANES, LANES), F32)
        for p in range(n_pair):
            rp = jnp.broadcast_to(r[:, p * LANES:(p + 1) * LANES], (SUBLANES, LANES))
            r8 = jnp.where(jnp.logical_and(sel_r, row8 // 2 == p), rp, r8)
        s_all = jnp.concatenate([_bf(s_scr[d][bb, p]) for p in range(n_pair)], axis=0)
        o8 = _dot_nt(_bf(r8), s_all)
        o_parts = []
        for p in range(n_pair):
            for h in range(2):
                o_parts.append(o8[2 * p + h:2 * p + h + 1, p * HD_B:(p + 1) * HD_B])
        o_ref[bb, pl.ds(tau, 1), :] = jnp.concatenate(o_parts, axis=1)

    groups = [(d, bbs) for d in range(2) for bbs in _chunks(range(RW_BB), RW_GROUP_BB)]

    def reduce_phase(grp, tt):
        d, bbs = grp
        rev = dirs[d][7]
        tau = row_of(rev, tt)
        sub = tau // half
        lt = tau % half
        mask = jnp.logical_or(lane == lt, lane == lt + HD_B)
        lhs = []
        for bb in bbs:
            emit_output(d, bb, row_of(rev, jnp.maximum(tt - 1, 0)))
            nkk = dirs[d][0][bb, pl.ds(tau, 1), :]
            for p in range(n_pair):
                prod = s_scr[d][bb, p] * nkk[:, p * LANES:(p + 1) * LANES]
                lhs.append(jnp.concatenate([_bf(prod), _bf(jnp.where(mask, vt_scr[d, bb, p, sub], 0.0))],
                                           axis=1))
        return jnp.dot(jnp.concatenate(lhs, axis=0), ones2, preferred_element_type=F32)

    def update_phase(grp, tt, red):
        d, bbs = grp
        _, _, _, w_ref, ka_ref, kd_ref, _, rev = dirs[d]
        tau = row_of(rev, tt)
        for k, bb in enumerate(bbs):
            w = w_ref[bb, pl.ds(tau, 1), :]
            ka = ka_ref[bb, pl.ds(tau, 1), :]
            kd = kd_ref[bb, pl.ds(tau, 1), :]
            for p in range(n_pair):
                sl = slice(p * LANES, (p + 1) * LANES)
                r0 = (k * n_pair + p) * HD_B
                sa = red[r0:r0 + HD_B, :LANES]
                vcol = red[r0:r0 + HD_B, LANES:]
                s_scr[d][bb, p] = s_scr[d][bb, p] * w[:, sl] + sa * ka[:, sl] + vcol * kd[:, sl]

    def step(tt, carry):
        reds = [reduce_phase(g, tt) for g in groups]
        for g, red in zip(groups, reds):
            update_phase(g, tt, red)
        return carry

    lax.fori_loop(0, RW_TB, step, 0)
    for d in range(2):
        for bb in range(RW_BB):
            emit_output(d, bb, row_of(dirs[d][7], RW_TB - 1))

    @pl.when(tb == n_tb - 1)
    def _():
        sff_ref[...] = sf_scr[...]
        sfb_ref[...] = sb_scr[...]


def _rwkv_scan(pp, row0, n_seq, seq, s0_f, s0_b, ones_pair, prev_f, prev_b):
    n_tb = seq // RW_TB
    n_pair = H_B // 2
    blk0 = row0 // seq
    view = lambda a: a.reshape(a.shape[0] // seq, seq, W_B)
    fwd = pl.BlockSpec((RW_BB, RW_TB, W_B), lambda b, t: (blk0 // RW_BB + b, t, 0))
    bwd = pl.BlockSpec((RW_BB, RW_TB, W_B), lambda b, t: (blk0 // RW_BB + b, n_tb - 1 - t, 0))
    st = pl.BlockSpec((RW_BB, n_pair, HD_B, LANES), lambda b, t: (b, 0, 0, 0))
    ins_f = [view(pp[k]) for k in ('nkk', 'r', 'v', 'w_f', 'ka_f', 'kd_f')]
    ins_b = [view(pp[k]) for k in ('nkk', 'r', 'v', 'w_b', 'ka_b', 'kd_b')]
    st_shape = jax.ShapeDtypeStruct((n_seq, n_pair, HD_B, LANES), F32)
    o_shape = jax.ShapeDtypeStruct(view(prev_f).shape, F32)
    any_spec = pl.BlockSpec(memory_space=pl.ANY)
    o_f, o_b, sf, sb = pl.pallas_call(
        functools.partial(_rwkv_scan_kernel, n_tb),
        grid=(n_seq // RW_BB, n_tb),
        in_specs=[fwd] * 6 + [bwd] * 6 + [st, st, pl.BlockSpec(ones_pair.shape, lambda b, t: (0, 0)),
                                           any_spec, any_spec],
        out_specs=[fwd, bwd, st, st],
        out_shape=[o_shape, o_shape, st_shape, st_shape],
        input_output_aliases={15: 0, 16: 1},
        scratch_shapes=[pltpu.VMEM((RW_BB, n_pair, HD_B, LANES), F32),
                        pltpu.VMEM((RW_BB, n_pair, HD_B, LANES), F32),
                        pltpu.VMEM((2, RW_BB, n_pair, 2, HD_B, LANES), F32)],
        compiler_params=_cparams(2),
        name="rwkv_scan",
    )(*ins_f, *ins_b, s0_f, s0_b, ones_pair, view(prev_f), view(prev_b))
    return o_f.reshape(prev_f.shape), o_b.reshape(prev_b.shape), sf, sb


def _state_to_pairs(s):
    b = s.shape[0]
    return s.reshape(b, H_B // 2, 2, HD_B, HD_B).transpose(0, 1, 3, 2, 4).reshape(b, H_B // 2, HD_B, 2 * HD_B)


def _pairs_to_state(s):
    b = s.shape[0]
    return s.reshape(b, H_B // 2, HD_B, 2, HD_B).transpose(0, 1, 3, 2, 4).reshape(b, H_B, HD_B, HD_B)


def _tail(x, y, g1, n2g, sc2, sh2, x1_ref, h_ref, hp_ref):
    x1 = x + g1 * y
    x1_ref[...] = x1
    h = _rms_mod(x1, n2g, sc2, sh2)
    h_ref[...] = _bf(h)
    hp_ref[...] = _pack_pairs(h)


def _outproj0_kernel(rows, xa_ref, xb_ref, oa_ref, of_ref, ob_ref, bonus_ref, g_ref, lnw_ref, lnb_ref, ones_ref,
                     w_ref, g1_ref, n2g_ref, sc2_ref, sh2_ref, x1_ref, h_ref, hp_ref):
    o_sum = of_ref[...] + ob_ref[...]
    ones2 = ones_ref[...]
    mean = _seg_sum(o_sum, ones2) * (1.0 / HD_B)
    cen = o_sum - mean
    var = _seg_sum(cen * cen, ones2) * (1.0 / HD_B)
    gn = cen * lax.rsqrt(var + GN_EPS) * lnw_ref[...] + lnb_ref[...]
    o_rw = (gn + bonus_ref[...]) * g_ref[...]
    mix = jnp.concatenate([_bf(oa_ref[...]), _bf(o_rw)], axis=1)
    y = jnp.dot(mix, w_ref[...], preferred_element_type=F32)
    _tail(_pick_x(rows, xa_ref, xb_ref), y, g1_ref[...], n2g_ref[...], sc2_ref[...], sh2_ref[...], x1_ref, h_ref, hp_ref)


def _outproj0(rows, xs, o_att, o_f, o_b, pp, pr, ones_b, w_out_bf, n2g, mod4, layer):
    n = rows.n
    tok = lambda w: pl.BlockSpec((TM, w), lambda i: (i, 0))
    const = lambda a: pl.BlockSpec(a.shape, lambda i: (0,) * a.ndim)
    lnw, lnb, n2 = pr['ln_w'].reshape(1, -1), pr['ln_b'].reshape(1, -1), n2g.reshape(1, -1)
    return pl.pallas_call(
        functools.partial(_outproj0_kernel, rows),
        grid=(n // TM,),
        in_specs=_x_specs(rows, xs) + [tok(W_A), tok(W_B), tok(W_B), tok(W_B), tok(W_B),
                  const(lnw), const(lnb), const(ones_b), const(w_out_bf),
                  rows.mod_spec(layer, 2, TM), const(n2), rows.mod_spec(layer, 4, TM), rows.mod_spec(layer, 3, TM)],
        out_specs=[tok(D_MODEL), tok(D_MODEL), tok(D_MODEL // 2)],
        out_shape=[jax.ShapeDtypeStruct((n, D_MODEL), F32), jax.ShapeDtypeStruct((n, D_MODEL), BF16),
                   jax.ShapeDtypeStruct((n, D_MODEL // 2), jnp.uint32)],
        compiler_params=_cparams(1),
        name="outproj0",
    )(*xs, o_att, o_f, o_b, pp['bonus'], pp['g'], lnw, lnb, ones_b, w_out_bf, mod4, n2, mod4, mod4)


def _outproj1_kernel(rows, xa_ref, xb_ref, of_ref, ob_ref, gate_ref, ng_ref, w_ref, g1_ref, n2g_ref, sc2_ref, sh2_ref,
                     x1_ref, h_ref, hp_ref):
    o_sum = of_ref[...] + ob_ref[...]
    parts = []
    for h in range(H_C):
        oh = o_sum[:, h * DV_C:(h + 1) * DV_C]
        parts.append(oh * lax.rsqrt(jnp.mean(oh * oh, axis=-1, keepdims=True) + EPS))
    o = jnp.concatenate(parts, axis=1) * ng_ref[...] * _silu(gate_ref[...])
    y = jnp.dot(_bf(o), w_ref[...], preferred_element_type=F32)
    _tail(_pick_x(rows, xa_ref, xb_ref), y, g1_ref[...], n2g_ref[...], sc2_ref[...], sh2_ref[...], x1_ref, h_ref, hp_ref)


def _outproj1(rows, xs, o_f, o_b, p1, norm_g, w_out_bf, n2g, mod4, layer):
    n = rows.n
    tok = lambda w: pl.BlockSpec((TM, w), lambda i: (i, 0))
    const = lambda a: pl.BlockSpec(a.shape, lambda i: (0,) * a.ndim)
    ng, n2 = norm_g.reshape(1, -1), n2g.reshape(1, -1)
    return pl.pallas_call(
        functools.partial(_outproj1_kernel, rows),
        grid=(n // TM,),
        in_specs=_x_specs(rows, xs) + [tok(D_C), tok(D_C), pl.BlockSpec((TM, D_C), lambda i: (i, 4)),
                  const(ng), const(w_out_bf),
                  rows.mod_spec(layer, 2, TM), const(n2), rows.mod_spec(layer, 4, TM), rows.mod_spec(layer, 3, TM)],
        out_specs=[tok(D_MODEL), tok(D_MODEL), tok(D_MODEL // 2)],
        out_shape=[jax.ShapeDtypeStruct((n, D_MODEL), F32), jax.ShapeDtypeStruct((n, D_MODEL), BF16),
                   jax.ShapeDtypeStruct((n, D_MODEL // 2), jnp.uint32)],
        compiler_params=_cparams(1),
        name="outproj1",
    )(*xs, o_f, o_b, p1, ng, w_out_bf, mod4, n2, mod4, mod4)


def _hgrn_kernel(n_tb, qf_ref, ff_ref, if_ref, qb_ref, fb_ref, ib_ref, lbp_ref, s0f_ref, s0b_ref,
                 trif_ref, trib_ref, prevf_ref, prevb_ref, of_ref, ob_ref, sff_ref, sfb_ref, s_scr):
    del prevf_ref, prevb_ref
    tb = pl.program_id(1)

    @pl.when(tb == 0)
    def _():
        for h in range(H_C):
            s_scr[0, h] = s0f_ref[0, h].T
            s_scr[1, h] = s0b_ref[0, h].T

    lbp = lbp_ref[...]
    e = jnp.exp(lbp - jnp.max(lbp, axis=0, keepdims=True))
    sm = e / jnp.sum(e, axis=0, keepdims=True)
    lb = (sm[0:1] + sm[1:2]) - sm[0:1]

    n_chunk = HG_TB // CHUNK
    ti = lax.broadcasted_iota(jnp.int32, (HG_TB, HG_TB), 0)
    si = lax.broadcasted_iota(jnp.int32, (HG_TB, HG_TB), 1)
    same = (ti // CHUNK) == (si // CHUNK)
    dirs = ((qf_ref, ff_ref, if_ref, of_ref, trif_ref, jnp.logical_and(same, ti >= si), CHUNK - 1, False),
            (qb_ref, fb_ref, ib_ref, ob_ref, trib_ref, jnp.logical_and(same, ti <= si), 0, True))

    staged = []
    for d, (q_ref, f_ref, i_ref, o_ref, tri_ref, causal, last_row, rev) in enumerate(dirs):
        q = _silu(q_ref[...])
        f = lb + (1.0 - lb) * _sigmoid(f_ref[...])
        k = 1.0 - f
        v = _bf(i_ref[...])
        g = jnp.log(f)
        g1 = _bf(g)
        g2 = _bf(g - g1.astype(F32))
        tri2 = tri_ref[...]
        b_parts, last_parts, dec = [], [], []
        for c in range(n_chunk):
            rc = slice(c * CHUNK, (c + 1) * CHUNK)
            bc = jnp.dot(tri2, jnp.concatenate([g1[rc], g2[rc]], axis=0), preferred_element_type=F32)
            b_parts.append(bc)
            last = bc[last_row:last_row + 1]
            last_parts.append(jnp.broadcast_to(last, bc.shape))
            dec.append(jnp.exp(last))
        b = jnp.concatenate(b_parts, axis=0)
        b_last = jnp.concatenate(last_parts, axis=0)
        staged.append((_bf(q * jnp.exp(b)), _bf(k * jnp.exp(-b)), _bf(k * jnp.exp(b_last - b)), v, dec))

    for h in range(H_C):
        sl = slice(h * DK_C, (h + 1) * DK_C)
        for d, (q_ref, f_ref, i_ref, o_ref, tri_ref, causal, last_row, rev) in enumerate(dirs):
            q_in, k_in, k_out, v, dec = staged[d]
            qh, vh = q_in[:, sl], v[:, sl]
            att = jnp.where(causal, _dot_nt(qh, k_in[:, sl]), 0.0)
            o_intra = jnp.dot(_bf(att), vh, preferred_element_type=F32)
            s_t = s_scr[d, h]
            for c in (range(n_chunk - 1, -1, -1) if rev else range(n_chunk)):
                rc = slice(c * CHUNK, (c + 1) * CHUNK)
                o_ref[rc, sl] = o_intra[rc] + _dot_nt(qh[rc], _bf(s_t))
                s_t = dec[c][:, sl] * s_t + _dot_tn(vh[rc], k_out[rc, sl])
            s_scr[d, h] = s_t

    @pl.when(tb == n_tb - 1)
    def _():
        for h in range(H_C):
            sff_ref[0, h] = s_scr[0, h].T
            sfb_ref[0, h] = s_scr[1, h].T


def _hgrn_scan(p1, row0, n_seq, seq, lb_params, s0_f, s0_b, prev_f, prev_b):
    n_tb = seq // HG_TB
    blk0 = row0 // HG_TB
    tri = np.tril(np.ones((CHUNK, CHUNK), np.float32))
    tri_f = jnp.asarray(np.concatenate([tri] * 2, axis=1), dtype=BF16)
    tri_b = jnp.asarray(np.concatenate([tri.T] * 2, axis=1), dtype=BF16)
    fwd = lambda col: pl.BlockSpec((HG_TB, D_C), lambda b, t: (blk0 + b * n_tb + t, col))
    bwd = lambda col: pl.BlockSpec((HG_TB, D_C), lambda b, t: (blk0 + b * n_tb + n_tb - 1 - t, col))
    st = pl.BlockSpec((1, H_C, DK_C, DV_C), lambda b, t: (b, 0, 0, 0))
    const = lambda a: pl.BlockSpec(a.shape, lambda b, t: (0,) * a.ndim)
    o_shape = jax.ShapeDtypeStruct(prev_f.shape, F32)
    st_shape = jax.ShapeDtypeStruct((n_seq, H_C, DK_C, DV_C), F32)
    any_spec = pl.BlockSpec(memory_space=pl.ANY)
    return pl.pallas_call(
        functools.partial(_hgrn_kernel, n_tb),
        grid=(n_seq, n_tb),
        in_specs=[fwd(0), fwd(1), fwd(3), bwd(0), bwd(2), bwd(3), const(lb_params), st, st,
                  const(tri_f), const(tri_b), any_spec, any_spec],
        out_specs=[fwd(0), bwd(0), st, st],
        out_shape=[o_shape, o_shape, st_shape, st_shape],
        input_output_aliases={11: 0, 12: 1},
        scratch_shapes=[pltpu.VMEM((2, H_C, DV_C, DK_C), F32)],
        compiler_params=_cparams(2),
        name="hgrn_scan",
    )(p1, p1, p1, p1, p1, p1, lb_params, s0_f, s0_b, tri_f, tri_b, prev_f, prev_b)


def _router_kernel(h_ref, rhi_ref, rlo_ref, bias_ref, sel_ref, eidx_ref, ew_ref, cnt_ref):
    x = h_ref[...]
    tm = x.shape[0]
    logits = _dot_nt(rhi_ref[...], x) + _dot_nt(rlo_ref[...], x)
    scores = _sigmoid(logits)
    biased = scores + bias_ref[...]
    per = N_EXPERTS // N_GROUPS
    sub = lax.broadcasted_iota(jnp.int32, (per, tm), 0)
    gs_rows = []
    for g in range(N_GROUPS):
        blk = biased[g * per:(g + 1) * per]
        m1 = jnp.max(blk, axis=0, keepdims=True)
        first = jnp.min(jnp.where(blk == m1, sub, per), axis=0, keepdims=True)
        m2 = jnp.max(jnp.where(sub == first, -jnp.inf, blk), axis=0, keepdims=True)
        gs_rows.append(m1 + m2)
    gs = jnp.concatenate(gs_rows, axis=0)
    gi = lax.broadcasted_iota(jnp.int32, gs.shape, 0)
    rank = jnp.zeros(gs.shape, jnp.int32)
    for s in range(1, N_GROUPS):
        other = pltpu.roll(gs, s, 0)
        oi = pltpu.roll(gi, s, 0)
        beats = jnp.logical_or(other > gs, jnp.logical_and(other == gs, oi < gi))
        rank = rank + jnp.where(beats, 1, 0)
    keep = jnp.where(rank < TOPK_GROUPS, 1.0, 0.0)
    emask = jnp.concatenate([jnp.broadcast_to(keep[g:g + 1], (per, tm)) for g in range(N_GROUPS)], axis=0)
    cur = jnp.where(emask > 0.0, biased, -jnp.inf)
    ei = lax.broadcasted_iota(jnp.int32, cur.shape, 0)
    sel = jnp.zeros(cur.shape, F32)
    idxs, vals = [], []
    for _ in range(TOP_K):
        m = jnp.max(cur, axis=0, keepdims=True)
        idx = jnp.min(jnp.where(cur == m, ei, N_EXPERTS), axis=0, keepdims=True)
        pick = ei == idx
        idxs.append(idx)
        vals.append(jnp.sum(jnp.where(pick, scores, 0.0), axis=0, keepdims=True))
        sel = jnp.where(pick, 1.0, sel)
        cur = jnp.where(pick, -jnp.inf, cur)
    w = jnp.concatenate(vals, axis=0)
    eidx_ref[...] = jnp.concatenate(idxs, axis=0)
    ew_ref[...] = w / jnp.sum(w, axis=0, keepdims=True) * ROUTED_SCALE
    sel_ref[...] = _bf(sel)

    @pl.when(pl.program_id(0) == 0)
    def _():
        cnt_ref[...] = jnp.zeros_like(cnt_ref)

    cnt_ref[...] += jnp.sum(sel, axis=1, keepdims=True)


def _router(hffn, router, bias):
    n = hffn.shape[0]
    r_t = router.T
    r_hi = _bf(r_t)
    r_lo = _bf(r_t - r_hi.astype(F32))
    const = lambda a: pl.BlockSpec(a.shape, lambda i: (0,) * a.ndim)
    b_col = bias.reshape(N_EXPERTS, 1)
    return pl.pallas_call(
        _router_kernel,
        grid=(n // TM,),
        in_specs=[pl.BlockSpec((TM, D_MODEL), lambda i: (i, 0)), const(r_hi), const(r_lo), const(b_col)],
        out_specs=[pl.BlockSpec((N_EXPERTS, TM), lambda i: (0, i)),
                   pl.BlockSpec((TOP_K, TM), lambda i: (0, i)),
                   pl.BlockSpec((TOP_K, TM), lambda i: (0, i)),
                   pl.BlockSpec((N_EXPERTS, LANES), lambda i: (0, 0))],
        out_shape=[jax.ShapeDtypeStruct((N_EXPERTS, n), BF16),
                   jax.ShapeDtypeStruct((TOP_K, n), jnp.int32),
                   jax.ShapeDtypeStruct((TOP_K, n), F32),
                   jax.ShapeDtypeStruct((N_EXPERTS, LANES), F32)],
        compiler_params=_cparams(1),
        name="router",
    )(hffn, r_hi, r_lo, b_col)


def _positions_kernel(sel_ref, eidx_ref, base_ref, upper_ref, pos_ref, carry_ref):
    @pl.when(pl.program_id(0) == 0)
    def _():
        carry_ref[...] = jnp.zeros_like(carry_ref)

    sel = sel_ref[...]
    rank = jnp.dot(sel, upper_ref[...], preferred_element_type=F32)
    pos_e = base_ref[:, 0:1] + carry_ref[:, 0:1] + rank
    ei = lax.broadcasted_iota(jnp.int32, pos_e.shape, 0)
    eidx = eidx_ref[...]
    rows = [jnp.sum(jnp.where(ei == eidx[k:k + 1], pos_e, 0.0), axis=0, keepdims=True) for k in range(TOP_K)]
    pos_ref[...] = jnp.concatenate(rows, axis=0).astype(jnp.int32)
    carry_ref[...] += jnp.sum(sel.astype(F32), axis=1, keepdims=True)


def _positions(sel, eidx, base):
    n = sel.shape[1]
    pb = POS_TB
    upper = jnp.asarray(np.triu(np.ones((pb, pb), np.float32), 1), dtype=BF16)
    return pl.pallas_call(
        _positions_kernel,
        grid=(n // pb,),
        in_specs=[pl.BlockSpec((N_EXPERTS, pb), lambda i: (0, i)),
                  pl.BlockSpec((TOP_K, pb), lambda i: (0, i)),
                  pl.BlockSpec((N_EXPERTS, LANES), lambda i: (0, 0)),
                  pl.BlockSpec((pb, pb), lambda i: (0, 0))],
        out_specs=pl.BlockSpec((TOP_K, pb), lambda i: (0, i)),
        out_shape=jax.ShapeDtypeStruct((TOP_K, n), jnp.int32),
        scratch_shapes=[pltpu.VMEM((N_EXPERTS, LANES), F32)],
        compiler_params=_cparams(1),
        name="positions",
    )(sel, eidx, base, upper)


def _pack_pairs(x):
    half = x.shape[1] // 2
    bits = lax.bitcast_convert_type(_bf(x).astype(F32), jnp.uint32)
    return (bits[:, :half] >> 16) | (bits[:, half:] & jnp.uint32(0xFFFF0000))


def _unpack_pairs(w):
    lo = lax.bitcast_convert_type(w << 16, F32)
    hi = lax.bitcast_convert_type(w & jnp.uint32(0xFFFF0000), F32)
    return jnp.concatenate([_bf(lo), _bf(hi)], axis=1)


def _sc_gather(table, idx):
    b, w = idx.shape[0], table.shape[1]
    n_workers = SC_CORES * SC_SUBCORES
    per_w = b // n_workers
    assert b % (n_workers * SC_CHUNK) == 0
    mesh = plsc.VectorSubcoreMesh(core_axis_name="c", subcore_axis_name="s")

    n_chunk = per_w // SC_CHUNK
    assert n_chunk % 2 == 0

    @functools.partial(
        pl.kernel, mesh=mesh, out_type=jax.ShapeDtypeStruct((b, w), table.dtype),
        scratch_types=[pltpu.VMEM((2, SC_CHUNK), jnp.int32), pltpu.VMEM((2, SC_CHUNK, w), table.dtype),
                       pltpu.SemaphoreType.DMA((2,))])
    def gather(table_hbm, idx_hbm, out_hbm, idx_v, rows_v, sems):
        wid = lax.axis_index("s") * SC_CORES + lax.axis_index("c")
        base = wid * per_w

        def start(c, slot):
            off = pl.multiple_of(base + c * SC_CHUNK, SC_CHUNK)
            pltpu.sync_copy(idx_hbm.at[pl.ds(off, SC_CHUNK)], idx_v.at[slot])
            pltpu.async_copy(table_hbm.at[idx_v.at[slot]], rows_v.at[slot], sems.at[slot])

        def finish(c, slot):
            off = pl.multiple_of(base + c * SC_CHUNK, SC_CHUNK)
            pltpu.make_async_copy(table_hbm.at[idx_v.at[slot]], rows_v.at[slot], sems.at[slot]).wait()
            pltpu.sync_copy(rows_v.at[slot], out_hbm.at[pl.ds(off, SC_CHUNK)])

        start(0, 0)

        @pl.loop(0, n_chunk, step=2)
        def _(c):
            start(c + 1, 1)
            finish(c, 0)

            @pl.when(c + 2 < n_chunk)
            def _():
                start(c + 2, 0)

            finish(c + 1, 1)

    return gather(table, idx)


def _sc_scatter(src, pos3, n_rows):
    n, w = src.shape
    n_workers = SC_CORES * SC_SUBCORES
    per_w = n // n_workers
    assert n % (n_workers * SC_CHUNK) == 0
    mesh = plsc.VectorSubcoreMesh(core_axis_name="c", subcore_axis_name="s")

    n_chunk = per_w // SC_CHUNK
    assert n_chunk % 2 == 0

    @functools.partial(
        pl.kernel, mesh=mesh, out_type=jax.ShapeDtypeStruct((n_rows, w), src.dtype),
        scratch_types=[pltpu.VMEM((2, TOP_K, SC_CHUNK), jnp.int32), pltpu.VMEM((2, SC_CHUNK, w), src.dtype),
                       pltpu.SemaphoreType.DMA((2,)), pltpu.SemaphoreType.DMA((2,))])
    def scatter(src_hbm, pos_hbm, out_hbm, idx_v, rows_v, ld_sems, sc_sems):
        wid = lax.axis_index("s") * SC_CORES + lax.axis_index("c")
        base = wid * per_w

        def loads(c, slot):
            off = pl.multiple_of(base + c * SC_CHUNK, SC_CHUNK)
            return (pltpu.make_async_copy(src_hbm.at[pl.ds(off, SC_CHUNK)], rows_v.at[slot], ld_sems.at[slot]),
                    pltpu.make_async_copy(pos_hbm.at[off // SC_CHUNK], idx_v.at[slot], ld_sems.at[slot]))

        def scatter_chunk(slot):
            copies = [pltpu.async_copy(rows_v.at[slot], out_hbm.at[idx_v.at[slot, k]], sc_sems.at[slot])
                      for k in range(TOP_K)]
            for cp in copies:
                cp.wait()

        def half_step(c, slot):
            for cp in loads(c, slot):
                cp.wait()

            @pl.when(c + 1 < n_chunk)
            def _():
                for cp in loads(c + 1, 1 - slot):
                    cp.start()

            scatter_chunk(slot)

        for cp in loads(0, 0):
            cp.start()

        @pl.loop(0, n_chunk, step=2)
        def _(c):
            half_step(c, 0)
            half_step(c + 1, 1)

    return scatter(src, pos3)


def _experts_kernel(te_ref, nu_ref, xs_ref, wg_ref, wu_ref, wd_ref, ys_ref, wg_bf, wu_bf, wd_bf):
    i = pl.program_id(0)
    active = i < nu_ref[0]

    @pl.when(jnp.logical_and(active, jnp.logical_or(i == 0, te_ref[i] != te_ref[jnp.maximum(i - 1, 0)])))
    def _():
        wg_bf[...] = _bf(wg_ref[0])
        wu_bf[...] = _bf(wu_ref[0])
        wd_bf[...] = _bf(wd_ref[0])

    @pl.when(active)
    def _():
        x = _unpack_pairs(xs_ref[...])
        act = _glu(x, wg_bf[...], wu_bf[...])
        ys_ref[...] = _pack_pairs(jnp.dot(_bf(act), wd_bf[...], preferred_element_type=F32))


def _experts(xs, tile_expert, n_used, mp, layer):
    n_tiles = xs.shape[0] // MOE_TILE
    half = D_MODEL // 2
    wspec = lambda shape: pl.BlockSpec((None, 1) + shape, lambda i, te, nu: (layer, te[i], 0, 0))
    return pl.pallas_call(
        _experts_kernel,
        grid_spec=pltpu.PrefetchScalarGridSpec(
            num_scalar_prefetch=2, grid=(n_tiles,),
            in_specs=[pl.BlockSpec((MOE_TILE, half), lambda i, te, nu: (jnp.minimum(i, nu[0] - 1), 0)),
                      wspec((D_MODEL, D_EXPERT)), wspec((D_MODEL, D_EXPERT)), wspec((D_EXPERT, D_MODEL))],
            out_specs=pl.BlockSpec((MOE_TILE, half), lambda i, te, nu: (jnp.minimum(i, nu[0] - 1), 0)),
            scratch_shapes=[pltpu.VMEM((D_MODEL, D_EXPERT), BF16), pltpu.VMEM((D_MODEL, D_EXPERT), BF16),
                            pltpu.VMEM((D_EXPERT, D_MODEL), BF16)]),
        out_shape=jax.ShapeDtypeStruct(xs.shape, jnp.uint32),
        compiler_params=_cparams(1),
        name="experts",
    )(tile_expert, n_used, xs, mp['wg'], mp['wu'], mp['wd'])


def _combine_kernel(h_ref, *refs):
    yg_refs = refs[:TOP_K]
    ew_ref, eye_ref, sg_ref, su_ref, sd_ref, x1_ref, g2_ref, o_ref = refs[TOP_K:]
    act = _glu(h_ref[...], sg_ref[...], su_ref[...])
    acc = jnp.dot(_bf(act), sd_ref[...], preferred_element_type=F32)
    ew = ew_ref[...]
    hi = _bf(ew)
    lo = _bf(ew - hi.astype(F32))
    ew_t = _dot_tn(hi, eye_ref[...]) + _dot_tn(lo, eye_ref[...])
    for k in range(TOP_K):
        acc = acc + ew_t[:, k:k + 1] * _unpack_pairs(yg_refs[k][...]).astype(F32)
    o_ref[...] = x1_ref[...] + g2_ref[...] * acc


def _combine(rows, hffn, yg, ew, mp, x1, mod4, layer, row0, n_out):
    half = D_MODEL // 2
    n_blk = rows.n // TM
    blk0 = row0 // TM
    const = lambda a: pl.BlockSpec(a.shape, lambda i: (0,) * a.ndim)
    tok = lambda w: pl.BlockSpec((TM, w), lambda i: (blk0 + i, 0))
    slot = lambda k: pl.BlockSpec((TM, half), lambda i: (k * n_blk + blk0 + i, 0))
    eye = jnp.eye(TOP_K, dtype=BF16)
    return pl.pallas_call(
        _combine_kernel,
        grid=(n_out // TM,),
        in_specs=[tok(D_MODEL)] + [slot(k) for k in range(TOP_K)]
                 + [pl.BlockSpec((TOP_K, TM), lambda i: (0, blk0 + i)), const(eye),
                    const(mp['sg']), const(mp['su']), const(mp['sd']), tok(D_MODEL),
                    rows.mod_spec(layer, 5, TM, blk0)],
        out_specs=pl.BlockSpec((TM, D_MODEL), lambda i: (i, 0)),
        out_shape=jax.ShapeDtypeStruct((n_out, D_MODEL), F32),
        compiler_params=_cparams(1),
        name=f"combine{layer}",
    )(hffn, *([yg] * TOP_K), ew, eye, mp['sg'], mp['su'], mp['sd'], x1, mod4)


def _moe(rows, hffn, hpack, x1, router, bias, mp, mod4, layer, out_ranges):
    n = rows.n
    sel, eidx, ew, cnt = _router(hffn, router, bias)
    counts = cnt[:, 0].astype(jnp.int32)
    padded = (counts + MOE_TILE - 1) // MOE_TILE * MOE_TILE
    ends = jnp.cumsum(padded)
    n_rows = n * TOP_K + N_EXPERTS * MOE_TILE
    n_tiles = n_rows // MOE_TILE
    base = jnp.broadcast_to((ends - padded).astype(F32)[:, None], (N_EXPERTS, LANES))
    tile_start = jnp.arange(n_tiles, dtype=jnp.int32) * MOE_TILE
    tile_expert = jnp.minimum(jnp.sum((ends[None, :] <= tile_start[:, None]).astype(jnp.int32), axis=1),
                              N_EXPERTS - 1)
    n_used = (ends[-1:] // MOE_TILE).astype(jnp.int32)
    pos = _positions(sel, eidx, base)
    pos3 = pos.reshape(TOP_K, n // SC_CHUNK, SC_CHUNK).transpose(1, 0, 2)
    xs = _sc_scatter(hpack, pos3, n_rows)
    ys = _experts(xs, tile_expert, n_used, mp, layer)
    yg = _sc_gather(ys, pos.reshape(-1))
    return [_combine(rows, hffn, yg, ew, mp, x1, mod4, layer, row0, n_out) for row0, n_out in out_ranges]


def _glu(x, wg, wu):
    hg = jnp.dot(x, wg, preferred_element_type=F32)
    hu = jnp.dot(x, wu, preferred_element_type=F32)
    return _silu(hg) * hu


def kernel(x_prompt, x_sample, c, c_ctx, cache_attn_k, cache_attn_v, state_rwkv_fwd, state_rwkv_bwd,
           state_hgrn_fwd, state_hgrn_bwd, norm1_g, norm2_g, mod_w, mod_b, ab_w_in, ab_w_out, attn_q_norm,
           attn_k_norm, attn_sink, rwkv_mu, rwkv_w0, rwkv_w2, rwkv_a0, rwkv_a2, rwkv_g2, rwkv_k_k, rwkv_k_a,
           rwkv_r_k, rwkv_ln_w, rwkv_ln_b, hgrn_w_in, hgrn_w_out, hgrn_lower_bounds, hgrn_norm_g, moe_router,
           moe_bias, moe_w_gate, moe_w_up, moe_w_down, moe_shared_gate, moe_shared_up, moe_shared_down):
    n_cseq, cseq, _ = x_prompt.shape
    n_lseq, lseq, _ = x_sample.shape
    depth = mod_w.shape[0]
    assert depth == 2 and n_lseq + 1 <= SUBLANES
    assert cseq == PREP_TM and lseq % TM == 0 and lseq % HG_TB == 0 and cseq % HG_TB == 0
    assert (n_cseq * cseq) % TM == 0
    assert n_cseq % RW_BB == 0 and n_lseq % RW_BB == 0 and (n_cseq * cseq) % (lseq * RW_BB) == 0
    rows = _Rows(n_cseq * cseq, n_lseq * lseq, lseq)
    assert rows.n % MOE_TILE == 0 and lseq % MOE_TILE == 0 and rows.n_ctx % MOE_TILE == 0
    kv_w = KV_A * HD_A

    xs = (x_prompt.reshape(rows.n_ctx, D_MODEL), x_sample.reshape(rows.n_lat, D_MODEL))
    cvecs = jnp.concatenate([c_ctx[None, :], c, jnp.zeros((SUBLANES - 1 - n_lseq, D_MODEL), F32)], axis=0)
    mod4 = _modulation(cvecs, mod_w, mod_b).reshape(depth, SUBLANES, 1, 6 * D_MODEL)

    ones_q = _block_ones(W_A, HD_A)
    ones_k = _block_ones(kv_w, HD_A)
    ones_b = _block_ones(W_B, HD_B)
    ones_pair = _block_ones(LANES, HD_B)[:LANES]
    ones_pair = jnp.kron(jnp.eye(2, dtype=BF16), ones_pair)
    cos_t, sin_t = _rope_tables(lseq)

    def moe(l, hffn, hpack, x1, out_ranges):
        mp = {'wg': moe_w_gate, 'wu': moe_w_up, 'wd': moe_w_down,
              'sg': _bf(moe_shared_gate[l]), 'su': _bf(moe_shared_up[l]), 'sd': _bf(moe_shared_down[l])}
        return _moe(rows, hffn, hpack, x1, moe_router[l], moe_bias[l], mp, mod4, l, out_ranges)

    assert W_A == W_B
    all_rows = jnp.zeros((rows.n, W_B), F32)

    pr = {'mu': rwkv_mu[0], 'w0': rwkv_w0[0], 'w2': rwkv_w2[0], 'a0': rwkv_a0[0], 'a2': rwkv_a2[0],
          'g2': rwkv_g2[0], 'k_k': rwkv_k_k[0], 'k_a': rwkv_k_a[0], 'r_k': rwkv_r_k[0].reshape(-1),
          'ln_w': rwkv_ln_w[0], 'ln_b': rwkv_ln_b[0]}
    p_att, p_rw = _inproj(rows, xs, norm1_g[0], mod4, 0, _bf(ab_w_in[0]), (ATT_IN, RWKV_IN), TM)
    qg_t = jnp.tile(attn_q_norm[0], H_A).reshape(1, W_A)
    kg_t = jnp.tile(attn_k_norm[0], KV_A).reshape(1, kv_w)
    o_att, new_k, new_v = _ctx_attention(p_att, n_cseq, cseq, qg_t, kg_t, attn_sink[0], ones_q, ones_k, all_rows)
    past = cache_attn_k.shape[2]
    o_att = _lat_attention(p_att, rows.n_ctx // lseq, n_lseq, lseq, qg_t, kg_t, attn_sink[0], ones_q, ones_k,
                           cos_t, sin_t, cache_attn_k[:, 0].reshape(n_lseq, past, kv_w),
                           cache_attn_v[:, 0].reshape(n_lseq, past, kv_w), o_att)

    pp = _rwkv_prep(rows, p_rw, pr, ones_b)
    zero_st = jnp.zeros((n_cseq, H_B // 2, HD_B, LANES), F32)
    o_f, o_b, sf_c, sb_c = _rwkv_scan(pp, 0, n_cseq, cseq, zero_st, zero_st, ones_pair, all_rows, all_rows)
    o_f, o_b, _, _ = _rwkv_scan(pp, rows.n_ctx, n_lseq, lseq, _state_to_pairs(state_rwkv_fwd[:, 0]),
                                _state_to_pairs(state_rwkv_bwd[:, 0]), ones_pair, o_f, o_b)
    x1, hffn, hpack = _outproj0(rows, xs, o_att, o_f, o_b, pp, pr, ones_b, _bf(ab_w_out[0]), norm2_g[0], mod4, 0)
    (x,) = moe(0, hffn, hpack, x1, [(0, rows.n)])

    (p1,) = _inproj(rows, (x, x), norm1_g[1], mod4, 1, _bf(hgrn_w_in[0]), (IN_C,), IN1_TM)
    zero_h = jnp.zeros((n_cseq, H_C, DK_C, DV_C), F32)
    all_rows_c = jnp.zeros((rows.n, D_C), F32)
    h_f, h_b, hsf_c, hsb_c = _hgrn_scan(p1, 0, n_cseq, cseq, hgrn_lower_bounds, zero_h, zero_h,
                                        all_rows_c, all_rows_c)
    h_f, h_b, _, _ = _hgrn_scan(p1, rows.n_ctx, n_lseq, lseq, hgrn_lower_bounds,
                                state_hgrn_fwd[:, 0], state_hgrn_bwd[:, 0], h_f, h_b)
    x1, hffn, hpack = _outproj1(rows, (x, x), h_f, h_b, p1, hgrn_norm_g[0], _bf(hgrn_w_out[0]), norm2_g[1],
                                mod4, 1)
    y_c, y_l = moe(1, hffn, hpack, x1, [(0, rows.n_ctx), (rows.n_ctx, rows.n_lat)])

    y_prompt = y_c.reshape(n_cseq, cseq, D_MODEL)
    y_sample = y_l.reshape(n_lseq, lseq, D_MODEL)
    return (y_prompt, y_sample,
            new_k.reshape(n_cseq, 1, cseq, KV_A, HD_A), new_v.reshape(n_cseq, 1, cseq, KV_A, HD_A),
            _pairs_to_state(sf_c)[:, None], _pairs_to_state(sb_c)[:, None],
            hsf_c[:, None], hsb_c[:, None])
```

```python
import functools

import numpy as np
import jax
import jax.numpy as jnp
from jax import lax
from jax.experimental import pallas as pl
from jax.experimental.pallas import tpu as pltpu
from jax.experimental.pallas import tpu_sc as plsc

F32 = jnp.float32
BF16 = jnp.bfloat16

D_MODEL = 1024
GRID_W = 64
H_A = 8
KV_A = 2
G_A = H_A // KV_A
HD_A = 64
W_A = H_A * HD_A
WINDOW = 128
QBLK = 128
ROPE_BASE = 10000.0
ROPE_PAIR = HD_A // 4
ATTN_SCALE = HD_A ** -0.5
NEG_INF = -1e30
H_B = 8
HD_B = 64
W_B = H_B * HD_B
LORA_W = 64
LORA_A = 64
LORA_G = 128
GN_EPS = 64e-5
ATT_IN = W_A + 2 * KV_A * HD_A
RWKV_IN = 3 * W_B + LORA_W + LORA_A + LORA_G
IN_AB = ATT_IN + RWKV_IN
H_C = 8
DK_C = 128
DV_C = 128
D_C = H_C * DV_C
CHUNK = 64
IN_C = 5 * D_C
N_EXPERTS = 64
TOP_K = 8
N_GROUPS = 8
TOPK_GROUPS = 4
D_EXPERT = 256
ROUTED_SCALE = 2.5
EPS = 1e-6

LANES = 128
SUBLANES = 8
VMEM_LIMIT = 52 * 1024 * 1024

TM = 512
PREP_TM = 256
IN1_TM = 512
RW_TB = 128
RW_BB = 4
RW_GROUP_BB = 4
HG_TB = 256
MOE_TILE = 1024
POS_TB = 1024
SC_CORES = 2
SC_SUBCORES = 16
SC_CHUNK = 64


def _cparams(n_axes):
    return pltpu.CompilerParams(dimension_semantics=("arbitrary",) * n_axes,
                                vmem_limit_bytes=VMEM_LIMIT)


def _bf(x):
    return x.astype(BF16)


def _split2(x):
    hi = lax.bitcast_convert_type(
        lax.bitcast_convert_type(x, jnp.uint32) & jnp.uint32(0xFFFF0000), F32)
    return hi, x - hi


def _seg_sum(x, ones2):
    hi, lo = _split2(x)
    return jnp.dot(jnp.concatenate([_bf(hi), _bf(lo)], axis=1), ones2,
                   preferred_element_type=F32)


def _dot_nt(a, b):
    return lax.dot_general(a, b, (((1,), (1,)), ((), ())), preferred_element_type=F32)


def _dot_tn(a, b):
    return lax.dot_general(a, b, (((0,), (0,)), ((), ())), preferred_element_type=F32)


def _sigmoid(x):
    return 1.0 / (1.0 + jnp.exp(-x))


def _silu(x):
    return x * _sigmoid(x)


def _chunks(seq, n):
    seq = list(seq)
    return [seq[i:i + n] for i in range(0, len(seq), n)]


def _block_ones(width, seg):
    idx = np.arange(width) // seg
    bd = (idx[:, None] == idx[None, :]).astype(np.float32)
    return jnp.asarray(np.concatenate([bd, bd], axis=0), dtype=BF16)


def _mod_kernel(c_ref, w_ref, b_ref, o_ref):
    s = _silu(c_ref[...])
    o_ref[0] = jnp.dot(_bf(s), _bf(w_ref[0]), preferred_element_type=F32) + b_ref[0]


def _modulation(cvecs, mod_w, mod_b):
    depth = mod_w.shape[0]
    n_col = 6 * D_MODEL // D_MODEL
    return pl.pallas_call(
        _mod_kernel,
        grid=(depth, n_col),
        in_specs=[pl.BlockSpec((SUBLANES, D_MODEL), lambda l, j: (0, 0)),
                  pl.BlockSpec((1, D_MODEL, D_MODEL), lambda l, j: (l, 0, j)),
                  pl.BlockSpec((1, 1, D_MODEL), lambda l, j: (l, 0, j))],
        out_specs=pl.BlockSpec((1, SUBLANES, D_MODEL), lambda l, j: (l, 0, j)),
        out_shape=jax.ShapeDtypeStruct((depth, SUBLANES, 6 * D_MODEL), F32),
        compiler_params=_cparams(2),
        name="modulation",
    )(cvecs, mod_w, mod_b.reshape(depth, 1, 6 * D_MODEL))


class _Rows:
    def __init__(self, n_ctx, n_lat, lat_seq):
        self.n_ctx, self.n_lat, self.lat_seq = n_ctx, n_lat, lat_seq
        self.n = n_ctx + n_lat

    def mod_row(self, i, tm):
        nctx_blk = self.n_ctx // tm
        per_seq = self.lat_seq // tm
        return jnp.where(i < nctx_blk, 0, 1 + (i - nctx_blk) // per_seq)

    def mod_spec(self, layer, chunk, tm, blk0=0):
        return pl.BlockSpec((None, None, 1, D_MODEL),
                            lambda i, *_: (layer, self.mod_row(i + blk0, tm), 0, chunk))


def _rms_mod(x, g, sc, sh):
    ms = jnp.mean(x * x, axis=-1, keepdims=True)
    return x * lax.rsqrt(ms + EPS) * g * (1.0 + sc) + sh


def _x_specs(rows, xs, tm=TM):
    xa, xb = xs
    nctx_blk = rows.n_ctx // tm
    lat0 = nctx_blk if xb.shape[0] == rows.n else 0
    return [pl.BlockSpec((tm, D_MODEL), lambda i: (jnp.minimum(i, nctx_blk - 1), 0)),
            pl.BlockSpec((tm, D_MODEL), lambda i: (jnp.maximum(i - nctx_blk, 0) + lat0, 0))]


def _pick_x(rows, xa_ref, xb_ref):
    return jnp.where(pl.program_id(0) < rows.n_ctx // xa_ref.shape[0], xa_ref[...], xb_ref[...])


def _inproj_kernel(rows, splits, xa_ref, xb_ref, g_ref, sh_ref, sc_ref, w_ref, *o_refs):
    h = _rms_mod(_pick_x(rows, xa_ref, xb_ref), g_ref[...], sc_ref[...], sh_ref[...])
    p = jnp.dot(_bf(h), w_ref[...], preferred_element_type=F32)
    lo = 0
    for o_ref, width in zip(o_refs, splits):
        o_ref[...] = p[:, lo:lo + width]
        lo += width


def _inproj(rows, xs, g, mod4, layer, w_bf, splits, tm):
    n_out = w_bf.shape[1]
    return pl.pallas_call(
        functools.partial(_inproj_kernel, rows, splits),
        grid=(rows.n // tm,),
        in_specs=_x_specs(rows, xs, tm) + [
            pl.BlockSpec((1, D_MODEL), lambda i: (0, 0)),
            rows.mod_spec(layer, 0, tm),
            rows.mod_spec(layer, 1, tm),
            pl.BlockSpec((D_MODEL, n_out), lambda i: (0, 0), pipeline_mode=pl.Buffered(1))],
        out_specs=[pl.BlockSpec((tm, wd), lambda i: (i, 0)) for wd in splits],
        out_shape=[jax.ShapeDtypeStruct((rows.n, wd), F32) for wd in splits],
        compiler_params=_cparams(1),
        name=f"inproj{layer}",
    )(*xs, g.reshape(1, D_MODEL), mod4, mod4, w_bf)


def _head_rms(x, gain_t, ones2):
    ms = _seg_sum(x * x, ones2) * (1.0 / HD_A)
    return x * lax.rsqrt(ms + EPS) * gain_t


def _sink_softmax_pv(parts, sink):
    m = jnp.maximum(functools.reduce(jnp.maximum, [jnp.max(s, axis=-1, keepdims=True) for s, _ in parts]), sink)
    den = jnp.exp(sink - m)
    acc = None
    for s, v in parts:
        p = jnp.exp(s - m)
        den = den + jnp.sum(p, axis=-1, keepdims=True)
        pv = jnp.dot(_bf(p), v, preferred_element_type=F32)
        acc = pv if acc is None else acc + pv
    return acc / den


def _ctx_attn_kernel(p_ref, qg_ref, kg_ref, sink_ref, ones_q_ref, ones_k_ref, prev_ref, o_ref, k_ref, v_ref):
    del prev_ref
    p = p_ref[...]
    q = _head_rms(p[:, :W_A], qg_ref[...], ones_q_ref[...]) * ATTN_SCALE
    k = _head_rms(p[:, W_A:W_A + KV_A * HD_A], kg_ref[...], ones_k_ref[...])
    v = p[:, W_A + KV_A * HD_A:ATT_IN]
    k_ref[0] = k
    v_ref[0] = v
    qb, kb, vb = _bf(q), _bf(k), _bf(v)
    outs = []
    for h in range(H_A):
        j = h // G_A
        s = _dot_nt(qb[:, h * HD_A:(h + 1) * HD_A], kb[:, j * HD_A:(j + 1) * HD_A])
        outs.append(_sink_softmax_pv([(s, vb[:, j * HD_A:(j + 1) * HD_A])], sink_ref[h]))
    o_ref[...] = jnp.concatenate(outs, axis=1)


def _ctx_attention(p_att, n_seq, seq, qg_t, kg_t, sink, ones_q, ones_k, prev):
    kv_w = KV_A * HD_A
    return pl.pallas_call(
        _ctx_attn_kernel,
        grid=(n_seq,),
        in_specs=[pl.BlockSpec((seq, ATT_IN), lambda b: (b, 0)),
                  pl.BlockSpec((1, W_A), lambda b: (0, 0)),
                  pl.BlockSpec((1, kv_w), lambda b: (0, 0)),
                  pl.BlockSpec(memory_space=pltpu.SMEM),
                  pl.BlockSpec(ones_q.shape, lambda b: (0, 0)),
                  pl.BlockSpec(ones_k.shape, lambda b: (0, 0)),
                  pl.BlockSpec(memory_space=pl.ANY)],
        out_specs=[pl.BlockSpec((seq, W_A), lambda b: (b, 0)),
                   pl.BlockSpec((1, seq, kv_w), lambda b: (b, 0, 0)),
                   pl.BlockSpec((1, seq, kv_w), lambda b: (b, 0, 0))],
        input_output_aliases={6: 0},
        out_shape=[jax.ShapeDtypeStruct(prev.shape, F32),
                   jax.ShapeDtypeStruct((n_seq, seq, kv_w), F32),
                   jax.ShapeDtypeStruct((n_seq, seq, kv_w), F32)],
        compiler_params=_cparams(1),
        name="ctx_attention",
    )(p_att, qg_t, kg_t, sink, ones_q, ones_k, prev)


def _rope(x, cos_t, sin_t):
    lane = lax.broadcasted_iota(jnp.int32, cos_t.shape, 1)
    low = (lane % (2 * ROPE_PAIR)) < ROPE_PAIR
    outs = []
    for s in range(x.shape[1] // LANES):
        xs = x[:, s * LANES:(s + 1) * LANES]
        partner = jnp.where(low, pltpu.roll(xs, LANES - ROPE_PAIR, 1), pltpu.roll(xs, ROPE_PAIR, 1))
        outs.append(xs * cos_t + partner * sin_t)
    return outs[0] if len(outs) == 1 else jnp.concatenate(outs, axis=1)


def _lat_attn_kernel(seq, p_ref, qg_ref, kg_ref, sink_ref, ones_q_ref, ones_k_ref, cos_ref, sin_ref,
                     kc_ref, vc_ref, prev_ref, o_ref, q_scr, k_scr, v_scr):
    del prev_ref
    kv_w = KV_A * HD_A
    p = p_ref[...]
    q = _head_rms(p[:, :W_A], qg_ref[...], ones_q_ref[...])
    k = _head_rms(p[:, W_A:W_A + kv_w], kg_ref[...], ones_k_ref[...])
    qr = _bf(_rope(q, cos_ref[...], sin_ref[...]) * ATTN_SCALE)
    kr = _bf(_rope(k, cos_ref[...], sin_ref[...]))
    vb = _bf(p[:, W_A + kv_w:ATT_IN])
    for h in range(H_A):
        q_scr[h] = qr[:, h * HD_A:(h + 1) * HD_A]
    for j in range(KV_A):
        k_scr[j] = kr[:, j * HD_A:(j + 1) * HD_A]
        v_scr[j] = vb[:, j * HD_A:(j + 1) * HD_A]
    kc = _bf(kc_ref[0])
    vc = _bf(vc_ref[0])
    n_local = 3 * QBLK
    grp = lax.broadcasted_iota(jnp.int32, (G_A * QBLK, 1), 0) // QBLK

    def block(i, carry):
        q0 = pl.multiple_of(i * QBLK, QBLK)
        start = pl.multiple_of(jnp.clip((i - 1) * QBLK, 0, seq - n_local), QBLK)
        ipos = q0 + lax.broadcasted_iota(jnp.int32, (G_A * QBLK, n_local), 0) % QBLK
        jpos = start + lax.broadcasted_iota(jnp.int32, (G_A * QBLK, n_local), 1)
        band = jnp.abs(jpos - ipos) <= WINDOW
        outs = []
        for j in range(KV_A):
            qs = jnp.concatenate([q_scr[j * G_A + g, pl.ds(q0, QBLK), :] for g in range(G_A)], axis=0)
            kl = k_scr[j, pl.ds(start, n_local), :]
            vl = v_scr[j, pl.ds(start, n_local), :]
            sink = jnp.zeros((G_A * QBLK, 1), F32)
            for g in range(G_A):
                sink = jnp.where(grp == g, sink_ref[j * G_A + g], sink)
            s_loc = jnp.where(band, _dot_nt(qs, kl), NEG_INF)
            s_ctx = _dot_nt(qs, kc[:, j * HD_A:(j + 1) * HD_A])
            o = _sink_softmax_pv([(s_loc, vl), (s_ctx, vc[:, j * HD_A:(j + 1) * HD_A])], sink)
            outs.extend(o[g * QBLK:(g + 1) * QBLK] for g in range(G_A))
        o_ref[pl.ds(q0, QBLK), :] = jnp.concatenate(outs, axis=1)
        return carry

    lax.fori_loop(0, seq // QBLK, block, 0)


def _lat_attention(p_att, row_blk0, n_seq, seq, qg_t, kg_t, sink, ones_q, ones_k, cos_t, sin_t, kc, vc, prev):
    kv_w = KV_A * HD_A
    past = kc.shape[1]
    return pl.pallas_call(
        functools.partial(_lat_attn_kernel, seq),
        grid=(n_seq,),
        in_specs=[pl.BlockSpec((seq, ATT_IN), lambda b: (row_blk0 + b, 0)),
                  pl.BlockSpec((1, W_A), lambda b: (0, 0)),
                  pl.BlockSpec((1, kv_w), lambda b: (0, 0)),
                  pl.BlockSpec(memory_space=pltpu.SMEM),
                  pl.BlockSpec(ones_q.shape, lambda b: (0, 0)),
                  pl.BlockSpec(ones_k.shape, lambda b: (0, 0)),
                  pl.BlockSpec((seq, LANES), lambda b: (0, 0)),
                  pl.BlockSpec((seq, LANES), lambda b: (0, 0)),
                  pl.BlockSpec((1, past, kv_w), lambda b: (b, 0, 0)),
                  pl.BlockSpec((1, past, kv_w), lambda b: (b, 0, 0)),
                  pl.BlockSpec(memory_space=pl.ANY)],
        out_specs=pl.BlockSpec((seq, W_A), lambda b: (row_blk0 + b, 0)),
        out_shape=jax.ShapeDtypeStruct(prev.shape, F32),
        input_output_aliases={10: 0},
        scratch_shapes=[pltpu.VMEM((H_A, seq, HD_A), BF16), pltpu.VMEM((KV_A, seq, HD_A), BF16),
                        pltpu.VMEM((KV_A, seq, HD_A), BF16)],
        compiler_params=_cparams(1),
        name="lat_attention",
    )(p_att, qg_t, kg_t, sink, ones_q, ones_k, cos_t, sin_t, kc, vc, prev)


def _rope_tables(seq):
    pos = np.arange(seq)
    row = (pos // GRID_W).astype(np.float32)
    col = (pos % GRID_W).astype(np.float32)
    d_axis = HD_A // 2
    inv = (ROPE_BASE ** (-np.arange(0, d_axis, 2, dtype=np.float32) / d_axis)).astype(np.float32)
    cos_h = np.zeros((seq, HD_A), np.float32)
    sin_h = np.zeros((seq, HD_A), np.float32)
    for seg, p_ in enumerate((row, col)):
        ang = (p_[:, None] * inv[None, :]).astype(np.float32)
        c, s = np.cos(ang), np.sin(ang)
        base = seg * d_axis
        cos_h[:, base:base + d_axis // 2] = c
        cos_h[:, base + d_axis // 2:base + d_axis] = c
        sin_h[:, base:base + d_axis // 2] = -s
        sin_h[:, base + d_axis // 2:base + d_axis] = s
    rep = LANES // HD_A
    return jnp.asarray(np.tile(cos_h, (1, rep))), jnp.asarray(np.tile(sin_h, (1, rep)))


def _rwkv_prep_kernel(rows, x_ref, prev_ref, next_ref, mu_ref, kk_ref, ka_ref, rk_ref, w0_ref, w2_ref,
                      a0_ref, a2_ref, g2_ref, ones_ref,
                      nkk_ref, r_ref, v_ref, g_ref, bonus_ref,
                      wf_ref, kaf_ref, kdf_ref, wb_ref, kab_ref, kdb_ref):
    i = pl.program_id(0)
    nctx_blk = rows.n_ctx // PREP_TM
    per_seq = rows.lat_seq // PREP_TM
    is_ctx = i < nctx_blk
    first = jnp.logical_or(is_ctx, (i - nctx_blk) % per_seq == 0)
    last = jnp.logical_or(is_ctx, (i - nctx_blk) % per_seq == per_seq - 1)
    x = x_ref[...]
    ridx = lax.broadcasted_iota(jnp.int32, x.shape, 0)
    prev_row = jnp.where(first, 0.0, prev_ref[SUBLANES - 1:SUBLANES, :])
    next_row = jnp.where(last, 0.0, next_ref[0:1, :])
    xm1 = jnp.where(ridx == 0, prev_row, pltpu.roll(x, 1, 0))
    xp1 = jnp.where(ridx == PREP_TM - 1, next_row, pltpu.roll(x, PREP_TM - 1, 0))
    pw = x + (0.5 * (xm1 + xp1) - x) * mu_ref[...]

    r = pw[:, 0:W_B]
    k = pw[:, W_B:2 * W_B]
    v = pw[:, 2 * W_B:3 * W_B]
    wd = pw[:, 3 * W_B:3 * W_B + LORA_W]
    ad = pw[:, 3 * W_B + LORA_W:3 * W_B + LORA_W + LORA_A]
    gd = pw[:, 3 * W_B + LORA_W + LORA_A:]
    ones2 = ones_ref[...]

    kk = k * kk_ref[...]
    kk = kk / jnp.maximum(jnp.sqrt(_seg_sum(kk * kk, ones2)), 1e-12)
    nkk_ref[...] = -kk
    r_ref[...] = r
    v_ref[...] = v
    g_ref[...] = jnp.dot(_bf(_sigmoid(gd)), g2_ref[...], preferred_element_type=F32)
    tw = _bf(jnp.tanh(wd))
    adb = _bf(ad)
    bonus = jnp.zeros_like(r)
    for d, (w_o, ka_o, kd_o) in enumerate(((wf_ref, kaf_ref, kdf_ref), (wb_ref, kab_ref, kdb_ref))):
        z = -(w0_ref[d:d + 1, :] + jnp.dot(tw, w2_ref[d], preferred_element_type=F32))
        softplus = jnp.maximum(z, 0.0) + jnp.log(1.0 + jnp.exp(-jnp.abs(z)))
        w_o[...] = jnp.exp(-jnp.exp(-softplus - 0.5))
        a = _sigmoid(a0_ref[d:d + 1, :] + jnp.dot(adb, a2_ref[d], preferred_element_type=F32))
        kd = k * (1.0 + (a - 1.0) * ka_ref[...])
        ka_o[...] = kk * a
        kd_o[...] = kd
        bonus = bonus + _seg_sum(r * kd * rk_ref[...], ones2) * v
    bonus_ref[...] = bonus


def _rwkv_prep(rows, p_rw, pr, ones_b):
    n = rows.n
    n_halo = n // SUBLANES
    blk_halo = PREP_TM // SUBLANES
    row = lambda a: a.reshape(1, -1)
    full = lambda a: pl.BlockSpec(a.shape, lambda i: (0,) * a.ndim)
    consts = [row(pr['mu']), row(pr['k_k']), row(pr['k_a']), row(pr['r_k']), pr['w0'], _bf(pr['w2']),
              pr['a0'], _bf(pr['a2']), _bf(pr['g2']), ones_b]
    outs = pl.pallas_call(
        functools.partial(_rwkv_prep_kernel, rows),
        grid=(n // PREP_TM,),
        in_specs=[pl.BlockSpec((PREP_TM, RWKV_IN), lambda i: (i, 0)),
                  pl.BlockSpec((SUBLANES, RWKV_IN), lambda i: (jnp.maximum(i * blk_halo - 1, 0), 0)),
                  pl.BlockSpec((SUBLANES, RWKV_IN), lambda i: (jnp.minimum((i + 1) * blk_halo, n_halo - 1), 0))]
                 + [full(a) for a in consts],
        out_specs=[pl.BlockSpec((PREP_TM, W_B), lambda i: (i, 0))] * 11,
        out_shape=[jax.ShapeDtypeStruct((n, W_B), F32)] * 11,
        compiler_params=_cparams(1),
        name="rwkv_prep",
    )(p_rw, p_rw, p_rw, *consts)
    names = ('nkk', 'r', 'v', 'g', 'bonus', 'w_f', 'ka_f', 'kd_f', 'w_b', 'ka_b', 'kd_b')
    return dict(zip(names, outs))


def _rwkv_scan_kernel(n_tb, nkkf_ref, rf_ref, vf_ref, wf_ref, kaf_ref, kdf_ref,
                      nkkb_ref, rb_ref, vb_ref, wb_ref, kab_ref, kdb_ref,
                      s0f_ref, s0b_ref, ones_ref, prevf_ref, prevb_ref,
                      of_ref, ob_ref, sff_ref, sfb_ref, sf_scr, sb_scr, vt_scr):
    del prevf_ref, prevb_ref
    s_scr = (sf_scr, sb_scr)
    tb = pl.program_id(1)
    n_pair = H_B // 2
    half = RW_TB // 2
    dirs = ((nkkf_ref, rf_ref, vf_ref, wf_ref, kaf_ref, kdf_ref, of_ref, False),
            (nkkb_ref, rb_ref, vb_ref, wb_ref, kab_ref, kdb_ref, ob_ref, True))

    @pl.when(tb == 0)
    def _():
        sf_scr[...] = s0f_ref[...]
        sb_scr[...] = s0b_ref[...]

    lane = lax.broadcasted_iota(jnp.int32, (HD_B, LANES), 1)
    for d, refs in enumerate(dirs):
        v_ref = refs[2]
        for bb in range(RW_BB):
            for p in range(n_pair):
                vt = v_ref[bb, :, p * LANES:(p + 1) * LANES].T
                top, bot = vt[:HD_B], vt[HD_B:]
                for s in range(2):
                    if s == 0:
                        t2 = jnp.where(lane < HD_B, top, pltpu.roll(bot, HD_B, 1))
                    else:
                        t2 = jnp.where(lane < HD_B, pltpu.roll(top, HD_B, 1), bot)
                    vt_scr[d, bb, p, s] = t2

    ones2 = ones_ref[...]
    row8 = lax.broadcasted_iota(jnp.int32, (SUBLANES, LANES), 0)
    lane8 = lax.broadcasted_iota(jnp.int32, (SUBLANES, LANES), 1)
    sel_r = jnp.logical_or(jnp.logical_and(row8 % 2 == 0, lane8 < HD_B),
                           jnp.logical_and(row8 % 2 == 1, lane8 >= HD_B))

    def row_of(rev, tt):
        return RW_TB - 1 - tt if rev else tt

    def emit_output(d, bb, tau):
        r_ref, o_ref = dirs[d][1], dirs[d][6]
        r = r_ref[bb, pl.ds(tau, 1), :]
        r8 = jnp.zeros((SUBLANES, LANES), F32)
        for p in range(n_pair):
            rp = jnp.broadcast_to(r[:, p * LANES:(p + 1) * LANES], (SUBLANES, LANES))
            r8 = jnp.where(jnp.logical_and(sel_r, row8 // 2 == p), rp, r8)
        s_all = jnp.concatenate([_bf(s_scr[d][bb, p]) for p in range(n_pair)], axis=0)
        o8 = _dot_nt(_bf(r8), s_all)
        o_parts = []
        for p in range(n_pair):
            for h in range(2):
                o_parts.append(o8[2 * p + h:2 * p + h + 1, p * HD_B:(p + 1) * HD_B])
        o_ref[bb, pl.ds(tau, 1), :] = jnp.concatenate(o_parts, axis=1)

    groups = [(d, bbs) for d in range(2) for bbs in _chunks(range(RW_BB), RW_GROUP_BB)]

    def reduce_phase(grp, tt):
        d, bbs = grp
        rev = dirs[d][7]
        tau = row_of(rev, tt)
        sub = tau // half
        lt = tau % half
        mask = jnp.logical_or(lane == lt, lane == lt + HD_B)
        lhs = []
        for bb in bbs:
            emit_output(d, bb, row_of(rev, jnp.maximum(tt - 1, 0)))
            nkk = dirs[d][0][bb, pl.ds(tau, 1), :]
            for p in range(n_pair):
                prod = s_scr[d][bb, p] * nkk[:, p * LANES:(p + 1) * LANES]
                lhs.append(jnp.concatenate([_bf(prod), _bf(jnp.where(mask, vt_scr[d, bb, p, sub], 0.0))],
                                           axis=1))
        return jnp.dot(jnp.concatenate(lhs, axis=0), ones2, preferred_element_type=F32)

    def update_phase(grp, tt, red):
        d, bbs = grp
        _, _, _, w_ref, ka_ref, kd_ref, _, rev = dirs[d]
        tau = row_of(rev, tt)
        for k, bb in enumerate(bbs):
            w = w_ref[bb, pl.ds(tau, 1), :]
            ka = ka_ref[bb, pl.ds(tau, 1), :]
            kd = kd_ref[bb, pl.ds(tau, 1), :]
            for p in range(n_pair):
                sl = slice(p * LANES, (p + 1) * LANES)
                r0 = (k * n_pair + p) * HD_B
                sa = red[r0:r0 + HD_B, :LANES]
                vcol = red[r0:r0 + HD_B, LANES:]
                s_scr[d][bb, p] = s_scr[d][bb, p] * w[:, sl] + sa * ka[:, sl] + vcol * kd[:, sl]

    def step(tt, carry):
        reds = [reduce_phase(g, tt) for g in groups]
        for g, red in zip(groups, reds):
            update_phase(g, tt, red)
        return carry

    lax.fori_loop(0, RW_TB, step, 0)
    for d in range(2):
        for bb in range(RW_BB):
            emit_output(d, bb, row_of(dirs[d][7], RW_TB - 1))

    @pl.when(tb == n_tb - 1)
    def _():
        sff_ref[...] = sf_scr[...]
        sfb_ref[...] = sb_scr[...]


def _rwkv_scan(pp, row0, n_seq, seq, s0_f, s0_b, ones_pair, prev_f, prev_b):
    n_tb = seq // RW_TB
    n_pair = H_B // 2
    blk0 = row0 // seq
    view = lambda a: a.reshape(a.shape[0] // seq, seq, W_B)
    fwd = pl.BlockSpec((RW_BB, RW_TB, W_B), lambda b, t: (blk0 // RW_BB + b, t, 0))
    bwd = pl.BlockSpec((RW_BB, RW_TB, W_B), lambda b, t: (blk0 // RW_BB + b, n_tb - 1 - t, 0))
    st = pl.BlockSpec((RW_BB, n_pair, HD_B, LANES), lambda b, t: (b, 0, 0, 0))
    ins_f = [view(pp[k]) for k in ('nkk', 'r', 'v', 'w_f', 'ka_f', 'kd_f')]
    ins_b = [view(pp[k]) for k in ('nkk', 'r', 'v', 'w_b', 'ka_b', 'kd_b')]
    st_shape = jax.ShapeDtypeStruct((n_seq, n_pair, HD_B, LANES), F32)
    o_shape = jax.ShapeDtypeStruct(view(prev_f).shape, F32)
    any_spec = pl.BlockSpec(memory_space=pl.ANY)
    o_f, o_b, sf, sb = pl.pallas_call(
        functools.partial(_rwkv_scan_kernel, n_tb),
        grid=(n_seq // RW_BB, n_tb),
        in_specs=[fwd] * 6 + [bwd] * 6 + [st, st, pl.BlockSpec(ones_pair.shape, lambda b, t: (0, 0)),
                                           any_spec, any_spec],
        out_specs=[fwd, bwd, st, st],
        out_shape=[o_shape, o_shape, st_shape, st_shape],
        input_output_aliases={15: 0, 16: 1},
        scratch_shapes=[pltpu.VMEM((RW_BB, n_pair, HD_B, LANES), F32),
                        pltpu.VMEM((RW_BB, n_pair, HD_B, LANES), F32),
                        pltpu.VMEM((2, RW_BB, n_pair, 2, HD_B, LANES), F32)],
        compiler_params=_cparams(2),
        name="rwkv_scan",
    )(*ins_f, *ins_b, s0_f, s0_b, ones_pair, view(prev_f), view(prev_b))
    return o_f.reshape(prev_f.shape), o_b.reshape(prev_b.shape), sf, sb


def _state_to_pairs(s):
    b = s.shape[0]
    return s.reshape(b, H_B // 2, 2, HD_B, HD_B).transpose(0, 1, 3, 2, 4).reshape(b, H_B // 2, HD_B, 2 * HD_B)


def _pairs_to_state(s):
    b = s.shape[0]
    return s.reshape(b, H_B // 2, HD_B, 2, HD_B).transpose(0, 1, 3, 2, 4).reshape(b, H_B, HD_B, HD_B)


def _tail(x, y, g1, n2g, sc2, sh2, x1_ref, h_ref, hp_ref):
    x1 = x + g1 * y
    x1_ref[...] = x1
    h = _rms_mod(x1, n2g, sc2, sh2)
    h_ref[...] = _bf(h)
    hp_ref[...] = _pack_pairs(h)


def _outproj0_kernel(rows, xa_ref, xb_ref, oa_ref, of_ref, ob_ref, bonus_ref, g_ref, lnw_ref, lnb_ref, ones_ref,
                     w_ref, g1_ref, n2g_ref, sc2_ref, sh2_ref, x1_ref, h_ref, hp_ref):
    o_sum = of_ref[...] + ob_ref[...]
    ones2 = ones_ref[...]
    mean = _seg_sum(o_sum, ones2) * (1.0 / HD_B)
    cen = o_sum - mean
    var = _seg_sum(cen * cen, ones2) * (1.0 / HD_B)
    gn = cen * lax.rsqrt(var + GN_EPS) * lnw_ref[...] + lnb_ref[...]
    o_rw = (gn + bonus_ref[...]) * g_ref[...]
    mix = jnp.concatenate([_bf(oa_ref[...]), _bf(o_rw)], axis=1)
    y = jnp.dot(mix, w_ref[...], preferred_element_type=F32)
    _tail(_pick_x(rows, xa_ref, xb_ref), y, g1_ref[...], n2g_ref[...], sc2_ref[...], sh2_ref[...], x1_ref, h_ref, hp_ref)


def _outproj0(rows, xs, o_att, o_f, o_b, pp, pr, ones_b, w_out_bf, n2g, mod4, layer):
    n = rows.n
    tok = lambda w: pl.BlockSpec((TM, w), lambda i: (i, 0))
    const = lambda a: pl.BlockSpec(a.shape, lambda i: (0,) * a.ndim)
    lnw, lnb, n2 = pr['ln_w'].reshape(1, -1), pr['ln_b'].reshape(1, -1), n2g.reshape(1, -1)
    return pl.pallas_call(
        functools.partial(_outproj0_kernel, rows),
        grid=(n // TM,),
        in_specs=_x_specs(rows, xs) + [tok(W_A), tok(W_B), tok(W_B), tok(W_B), tok(W_B),
                  const(lnw), const(lnb), const(ones_b), const(w_out_bf),
                  rows.mod_spec(layer, 2, TM), const(n2), rows.mod_spec(layer, 4, TM), rows.mod_spec(layer, 3, TM)],
        out_specs=[tok(D_MODEL), tok(D_MODEL), tok(D_MODEL // 2)],
        out_shape=[jax.ShapeDtypeStruct((n, D_MODEL), F32), jax.ShapeDtypeStruct((n, D_MODEL), BF16),
                   jax.ShapeDtypeStruct((n, D_MODEL // 2), jnp.uint32)],
        compiler_params=_cparams(1),
        name="outproj0",
    )(*xs, o_att, o_f, o_b, pp['bonus'], pp['g'], lnw, lnb, ones_b, w_out_bf, mod4, n2, mod4, mod4)


def _outproj1_kernel(rows, xa_ref, xb_ref, of_ref, ob_ref, gate_ref, ng_ref, w_ref, g1_ref, n2g_ref, sc2_ref, sh2_ref,
                     x1_ref, h_ref, hp_ref):
    o_sum = of_ref[...] + ob_ref[...]
    parts = []
    for h in range(H_C):
        oh = o_sum[:, h * DV_C:(h + 1) * DV_C]
        parts.append(oh * lax.rsqrt(jnp.mean(oh * oh, axis=-1, keepdims=True) + EPS))
    o = jnp.concatenate(parts, axis=1) * ng_ref[...] * _silu(gate_ref[...])
    y = jnp.dot(_bf(o), w_ref[...], preferred_element_type=F32)
    _tail(_pick_x(rows, xa_ref, xb_ref), y, g1_ref[...], n2g_ref[...], sc2_ref[...], sh2_ref[...], x1_ref, h_ref, hp_ref)


def _outproj1(rows, xs, o_f, o_b, p1, norm_g, w_out_bf, n2g, mod4, layer):
    n = rows.n
    tok = lambda w: pl.BlockSpec((TM, w), lambda i: (i, 0))
    const = lambda a: pl.BlockSpec(a.shape, lambda i: (0,) * a.ndim)
    ng, n2 = norm_g.reshape(1, -1), n2g.reshape(1, -1)
    return pl.pallas_call(
        functools.partial(_outproj1_kernel, rows),
        grid=(n // TM,),
        in_specs=_x_specs(rows, xs) + [tok(D_C), tok(D_C), pl.BlockSpec((TM, D_C), lambda i: (i, 4)),
                  const(ng), const(w_out_bf),
                  rows.mod_spec(layer, 2, TM), const(n2), rows.mod_spec(layer, 4, TM), rows.mod_spec(layer, 3, TM)],
        out_specs=[tok(D_MODEL), tok(D_MODEL), tok(D_MODEL // 2)],
        out_shape=[jax.ShapeDtypeStruct((n, D_MODEL), F32), jax.ShapeDtypeStruct((n, D_MODEL), BF16),
                   jax.ShapeDtypeStruct((n, D_MODEL // 2), jnp.uint32)],
        compiler_params=_cparams(1),
        name="outproj1",
    )(*xs, o_f, o_b, p1, ng, w_out_bf, mod4, n2, mod4, mod4)


def _hgrn_kernel(n_tb, qf_ref, ff_ref, if_ref, qb_ref, fb_ref, ib_ref, lbp_ref, s0f_ref, s0b_ref,
                 trif_ref, trib_ref, prevf_ref, prevb_ref, of_ref, ob_ref, sff_ref, sfb_ref, s_scr):
    del prevf_ref, prevb_ref
    tb = pl.program_id(1)

    @pl.when(tb == 0)
    def _():
        for h in range(H_C):
            s_scr[0, h] = s0f_ref[0, h].T
            s_scr[1, h] = s0b_ref[0, h].T

    lbp = lbp_ref[...]
    e = jnp.exp(lbp - jnp.max(lbp, axis=0, keepdims=True))
    sm = e / jnp.sum(e, axis=0, keepdims=True)
    lb = (sm[0:1] + sm[1:2]) - sm[0:1]

    n_chunk = HG_TB // CHUNK
    ti = lax.broadcasted_iota(jnp.int32, (HG_TB, HG_TB), 0)
    si = lax.broadcasted_iota(jnp.int32, (HG_TB, HG_TB), 1)
    same = (ti // CHUNK) == (si // CHUNK)
    dirs = ((qf_ref, ff_ref, if_ref, of_ref, trif_ref, jnp.logical_and(same, ti >= si), CHUNK - 1, False),
            (qb_ref, fb_ref, ib_ref, ob_ref, trib_ref, jnp.logical_and(same, ti <= si), 0, True))

    staged = []
    for d, (q_ref, f_ref, i_ref, o_ref, tri_ref, causal, last_row, rev) in enumerate(dirs):
        q = _silu(q_ref[...])
        f = lb + (1.0 - lb) * _sigmoid(f_ref[...])
        k = 1.0 - f
        v = _bf(i_ref[...])
        g = jnp.log(f)
        g1 = _bf(g)
        g2 = _bf(g - g1.astype(F32))
        tri2 = tri_ref[...]
        b_parts, last_parts, dec = [], [], []
        for c in range(n_chunk):
            rc = slice(c * CHUNK, (c + 1) * CHUNK)
            bc = jnp.dot(tri2, jnp.concatenate([g1[rc], g2[rc]], axis=0), preferred_element_type=F32)
            b_parts.append(bc)
            last = bc[last_row:last_row + 1]
            last_parts.append(jnp.broadcast_to(last, bc.shape))
            dec.append(jnp.exp(last))
        b = jnp.concatenate(b_parts, axis=0)
        b_last = jnp.concatenate(last_parts, axis=0)
        staged.append((_bf(q * jnp.exp(b)), _bf(k * jnp.exp(-b)), _bf(k * jnp.exp(b_last - b)), v, dec))

    for h in range(H_C):
        sl = slice(h * DK_C, (h + 1) * DK_C)
        for d, (q_ref, f_ref, i_ref, o_ref, tri_ref, causal, last_row, rev) in enumerate(dirs):
            q_in, k_in, k_out, v, dec = staged[d]
            qh, vh = q_in[:, sl], v[:, sl]
            att = jnp.where(causal, _dot_nt(qh, k_in[:, sl]), 0.0)
            o_intra = jnp.dot(_bf(att), vh, preferred_element_type=F32)
            s_t = s_scr[d, h]
            for c in (range(n_chunk - 1, -1, -1) if rev else range(n_chunk)):
                rc = slice(c * CHUNK, (c + 1) * CHUNK)
                o_ref[rc, sl] = o_intra[rc] + _dot_nt(qh[rc], _bf(s_t))
                s_t = dec[c][:, sl] * s_t + _dot_tn(vh[rc], k_out[rc, sl])
            s_scr[d, h] = s_t

    @pl.when(tb == n_tb - 1)
    def _():
        for h in range(H_C):
            sff_ref[0, h] = s_scr[0, h].T
            sfb_ref[0, h] = s_scr[1, h].T


def _hgrn_scan(p1, row0, n_seq, seq, lb_params, s0_f, s0_b, prev_f, prev_b):
    n_tb = seq // HG_TB
    blk0 = row0 // HG_TB
    tri = np.tril(np.ones((CHUNK, CHUNK), np.float32))
    tri_f = jnp.asarray(np.concatenate([tri] * 2, axis=1), dtype=BF16)
    tri_b = jnp.asarray(np.concatenate([tri.T] * 2, axis=1), dtype=BF16)
    fwd = lambda col: pl.BlockSpec((HG_TB, D_C), lambda b, t: (blk0 + b * n_tb + t, col))
    bwd = lambda col: pl.BlockSpec((HG_TB, D_C), lambda b, t: (blk0 + b * n_tb + n_tb - 1 - t, col))
    st = pl.BlockSpec((1, H_C, DK_C, DV_C), lambda b, t: (b, 0, 0, 0))
    const = lambda a: pl.BlockSpec(a.shape, lambda b, t: (0,) * a.ndim)
    o_shape = jax.ShapeDtypeStruct(prev_f.shape, F32)
    st_shape = jax.ShapeDtypeStruct((n_seq, H_C, DK_C, DV_C), F32)
    any_spec = pl.BlockSpec(memory_space=pl.ANY)
    return pl.pallas_call(
        functools.partial(_hgrn_kernel, n_tb),
        grid=(n_seq, n_tb),
        in_specs=[fwd(0), fwd(1), fwd(3), bwd(0), bwd(2), bwd(3), const(lb_params), st, st,
                  const(tri_f), const(tri_b), any_spec, any_spec],
        out_specs=[fwd(0), bwd(0), st, st],
        out_shape=[o_shape, o_shape, st_shape, st_shape],
        input_output_aliases={11: 0, 12: 1},
        scratch_shapes=[pltpu.VMEM((2, H_C, DV_C, DK_C), F32)],
        compiler_params=_cparams(2),
        name="hgrn_scan",
    )(p1, p1, p1, p1, p1, p1, lb_params, s0_f, s0_b, tri_f, tri_b, prev_f, prev_b)


def _router_kernel(h_ref, rhi_ref, rlo_ref, bias_ref, sel_ref, eidx_ref, ew_ref, cnt_ref):
    x = h_ref[...]
    tm = x.shape[0]
    logits = _dot_nt(rhi_ref[...], x) + _dot_nt(rlo_ref[...], x)
    scores = _sigmoid(logits)
    biased = scores + bias_ref[...]
    per = N_EXPERTS // N_GROUPS
    sub = lax.broadcasted_iota(jnp.int32, (per, tm), 0)
    gs_rows = []
    for g in range(N_GROUPS):
        blk = biased[g * per:(g + 1) * per]
        m1 = jnp.max(blk, axis=0, keepdims=True)
        first = jnp.min(jnp.where(blk == m1, sub, per), axis=0, keepdims=True)
        m2 = jnp.max(jnp.where(sub == first, -jnp.inf, blk), axis=0, keepdims=True)
        gs_rows.append(m1 + m2)
    gs = jnp.concatenate(gs_rows, axis=0)
    gi = lax.broadcasted_iota(jnp.int32, gs.shape, 0)
    rank = jnp.zeros(gs.shape, jnp.int32)
    for s in range(1, N_GROUPS):
        other = pltpu.roll(gs, s, 0)
        oi = pltpu.roll(gi, s, 0)
        beats = jnp.logical_or(other > gs, jnp.logical_and(other == gs, oi < gi))
        rank = rank + jnp.where(beats, 1, 0)
    keep = jnp.where(rank < TOPK_GROUPS, 1.0, 0.0)
    emask = jnp.concatenate([jnp.broadcast_to(keep[g:g + 1], (per, tm)) for g in range(N_GROUPS)], axis=0)
    cur = jnp.where(emask > 0.0, biased, -jnp.inf)
    ei = lax.broadcasted_iota(jnp.int32, cur.shape, 0)
    sel = jnp.zeros(cur.shape, F32)
    idxs, vals = [], []
    for _ in range(TOP_K):
        m = jnp.max(cur, axis=0, keepdims=True)
        idx = jnp.min(jnp.where(cur == m, ei, N_EXPERTS), axis=0, keepdims=True)
        pick = ei == idx
        idxs.append(idx)
        vals.append(jnp.sum(jnp.where(pick, scores, 0.0), axis=0, keepdims=True))
        sel = jnp.where(pick, 1.0, sel)
        cur = jnp.where(pick, -jnp.inf, cur)
    w = jnp.concatenate(vals, axis=0)
    eidx_ref[...] = jnp.concatenate(idxs, axis=0)
    ew_ref[...] = w / jnp.sum(w, axis=0, keepdims=True) * ROUTED_SCALE
    sel_ref[...] = _bf(sel)

    @pl.when(pl.program_id(0) == 0)
    def _():
        cnt_ref[...] = jnp.zeros_like(cnt_ref)

    cnt_ref[...] += jnp.sum(sel, axis=1, keepdims=True)


def _router(hffn, router, bias):
    n = hffn.shape[0]
    r_t = router.T
    r_hi = _bf(r_t)
    r_lo = _bf(r_t - r_hi.astype(F32))
    const = lambda a: pl.BlockSpec(a.shape, lambda i: (0,) * a.ndim)
    b_col = bias.reshape(N_EXPERTS, 1)
    return pl.pallas_call(
        _router_kernel,
        grid=(n // TM,),
        in_specs=[pl.BlockSpec((TM, D_MODEL), lambda i: (i, 0)), const(r_hi), const(r_lo), const(b_col)],
        out_specs=[pl.BlockSpec((N_EXPERTS, TM), lambda i: (0, i)),
                   pl.BlockSpec((TOP_K, TM), lambda i: (0, i)),
                   pl.BlockSpec((TOP_K, TM), lambda i: (0, i)),
                   pl.BlockSpec((N_EXPERTS, LANES), lambda i: (0, 0))],
        out_shape=[jax.ShapeDtypeStruct((N_EXPERTS, n), BF16),
                   jax.ShapeDtypeStruct((TOP_K, n), jnp.int32),
                   jax.ShapeDtypeStruct((TOP_K, n), F32),
                   jax.ShapeDtypeStruct((N_EXPERTS, LANES), F32)],
        compiler_params=_cparams(1),
        name="router",
    )(hffn, r_hi, r_lo, b_col)


def _positions_kernel(sel_ref, eidx_ref, base_ref, upper_ref, pos_ref, carry_ref):
    @pl.when(pl.program_id(0) == 0)
    def _():
        carry_ref[...] = jnp.zeros_like(carry_ref)

    sel = sel_ref[...]
    rank = jnp.dot(sel, upper_ref[...], preferred_element_type=F32)
    pos_e = base_ref[:, 0:1] + carry_ref[:, 0:1] + rank
    ei = lax.broadcasted_iota(jnp.int32, pos_e.shape, 0)
    eidx = eidx_ref[...]
    rows = [jnp.sum(jnp.where(ei == eidx[k:k + 1], pos_e, 0.0), axis=0, keepdims=True) for k in range(TOP_K)]
    pos_ref[...] = jnp.concatenate(rows, axis=0).astype(jnp.int32)
    carry_ref[...] += jnp.sum(sel.astype(F32), axis=1, keepdims=True)


def _positions(sel, eidx, base):
    n = sel.shape[1]
    pb = POS_TB
    upper = jnp.asarray(np.triu(np.ones((pb, pb), np.float32), 1), dtype=BF16)
    return pl.pallas_call(
        _positions_kernel,
        grid=(n // pb,),
        in_specs=[pl.BlockSpec((N_EXPERTS, pb), lambda i: (0, i)),
                  pl.BlockSpec((TOP_K, pb), lambda i: (0, i)),
                  pl.BlockSpec((N_EXPERTS, LANES), lambda i: (0, 0)),
                  pl.BlockSpec((pb, pb), lambda i: (0, 0))],
        out_specs=pl.BlockSpec((TOP_K, pb), lambda i: (0, i)),
        out_shape=jax.ShapeDtypeStruct((TOP_K, n), jnp.int32),
        scratch_shapes=[pltpu.VMEM((N_EXPERTS, LANES), F32)],
        compiler_params=_cparams(1),
        name="positions",
    )(sel, eidx, base, upper)


def _pack_pairs(x):
    half = x.shape[1] // 2
    bits = lax.bitcast_convert_type(_bf(x).astype(F32), jnp.uint32)
    return (bits[:, :half] >> 16) | (bits[:, half:] & jnp.uint32(0xFFFF0000))


def _unpack_pairs(w):
    lo = lax.bitcast_convert_type(w << 16, F32)
    hi = lax.bitcast_convert_type(w & jnp.uint32(0xFFFF0000), F32)
    return jnp.concatenate([_bf(lo), _bf(hi)], axis=1)


def _sc_gather(table, idx):
    b, w = idx.shape[0], table.shape[1]
    n_workers = SC_CORES * SC_SUBCORES
    per_w = b // n_workers
    assert b % (n_workers * SC_CHUNK) == 0
    mesh = plsc.VectorSubcoreMesh(core_axis_name="c", subcore_axis_name="s")

    n_chunk = per_w // SC_CHUNK
    assert n_chunk % 2 == 0

    @functools.partial(
        pl.kernel, mesh=mesh, out_type=jax.ShapeDtypeStruct((b, w), table.dtype),
        scratch_types=[pltpu.VMEM((2, SC_CHUNK), jnp.int32), pltpu.VMEM((2, SC_CHUNK, w), table.dtype),
                       pltpu.SemaphoreType.DMA((2,))])
    def gather(table_hbm, idx_hbm, out_hbm, idx_v, rows_v, sems):
        wid = lax.axis_index("s") * SC_CORES + lax.axis_index("c")
        base = wid * per_w

        def start(c, slot):
            off = pl.multiple_of(base + c * SC_CHUNK, SC_CHUNK)
            pltpu.sync_copy(idx_hbm.at[pl.ds(off, SC_CHUNK)], idx_v.at[slot])
            pltpu.async_copy(table_hbm.at[idx_v.at[slot]], rows_v.at[slot], sems.at[slot])

        def finish(c, slot):
            off = pl.multiple_of(base + c * SC_CHUNK, SC_CHUNK)
            pltpu.make_async_copy(table_hbm.at[idx_v.at[slot]], rows_v.at[slot], sems.at[slot]).wait()
            pltpu.sync_copy(rows_v.at[slot], out_hbm.at[pl.ds(off, SC_CHUNK)])

        start(0, 0)

        @pl.loop(0, n_chunk, step=2)
        def _(c):
            start(c + 1, 1)
            finish(c, 0)

            @pl.when(c + 2 < n_chunk)
            def _():
                start(c + 2, 0)

            finish(c + 1, 1)

    return gather(table, idx)


def _sc_scatter(src, pos3, n_rows):
    n, w = src.shape
    n_workers = SC_CORES * SC_SUBCORES
    per_w = n // n_workers
    assert n % (n_workers * SC_CHUNK) == 0
    mesh = plsc.VectorSubcoreMesh(core_axis_name="c", subcore_axis_name="s")

    n_chunk = per_w // SC_CHUNK
    assert n_chunk % 2 == 0

    @functools.partial(
        pl.kernel, mesh=mesh, out_type=jax.ShapeDtypeStruct((n_rows, w), src.dtype),
        scratch_types=[pltpu.VMEM((2, TOP_K, SC_CHUNK), jnp.int32), pltpu.VMEM((2, SC_CHUNK, w), src.dtype),
                       pltpu.SemaphoreType.DMA((2,)), pltpu.SemaphoreType.DMA((2,))])
    def scatter(src_hbm, pos_hbm, out_hbm, idx_v, rows_v, ld_sems, sc_sems):
        wid = lax.axis_index("s") * SC_CORES + lax.axis_index("c")
        base = wid * per_w

        def loads(c, slot):
            off = pl.multiple_of(base + c * SC_CHUNK, SC_CHUNK)
            return (pltpu.make_async_copy(src_hbm.at[pl.ds(off, SC_CHUNK)], rows_v.at[slot], ld_sems.at[slot]),
                    pltpu.make_async_copy(pos_hbm.at[off // SC_CHUNK], idx_v.at[slot], ld_sems.at[slot]))

        def scatter_chunk(slot):
            copies = [pltpu.async_copy(rows_v.at[slot], out_hbm.at[idx_v.at[slot, k]], sc_sems.at[slot])
                      for k in range(TOP_K)]
            for cp in copies:
                cp.wait()

        def half_step(c, slot):
            for cp in loads(c, slot):
                cp.wait()

            @pl.when(c + 1 < n_chunk)
            def _():
                for cp in loads(c + 1, 1 - slot):
                    cp.start()

            scatter_chunk(slot)

        for cp in loads(0, 0):
            cp.start()

        @pl.loop(0, n_chunk, step=2)
        def _(c):
            half_step(c, 0)
            half_step(c + 1, 1)

    return scatter(src, pos3)


def _experts_kernel(te_ref, nu_ref, xs_ref, wg_ref, wu_ref, wd_ref, ys_ref, wg_bf, wu_bf, wd_bf):
    i = pl.program_id(0)
    active = i < nu_ref[0]

    @pl.when(jnp.logical_and(active, jnp.logical_or(i == 0, te_ref[i] != te_ref[jnp.maximum(i - 1, 0)])))
    def _():
        wg_bf[...] = _bf(wg_ref[0])
        wu_bf[...] = _bf(wu_ref[0])
        wd_bf[...] = _bf(wd_ref[0])

    @pl.when(active)
    def _():
        x = _unpack_pairs(xs_ref[...])
        act = _glu(x, wg_bf[...], wu_bf[...])
        ys_ref[...] = _pack_pairs(jnp.dot(_bf(act), wd_bf[...], preferred_element_type=F32))


def _experts(xs, tile_expert, n_used, mp, layer):
    n_tiles = xs.shape[0] // MOE_TILE
    half = D_MODEL // 2
    wspec = lambda shape: pl.BlockSpec((None, 1) + shape, lambda i, te, nu: (layer, te[i], 0, 0))
    return pl.pallas_call(
        _experts_kernel,
        grid_spec=pltpu.PrefetchScalarGridSpec(
            num_scalar_prefetch=2, grid=(n_tiles,),
            in_specs=[pl.BlockSpec((MOE_TILE, half), lambda i, te, nu: (jnp.minimum(i, nu[0] - 1), 0)),
                      wspec((D_MODEL, D_EXPERT)), wspec((D_MODEL, D_EXPERT)), wspec((D_EXPERT, D_MODEL))],
            out_specs=pl.BlockSpec((MOE_TILE, half), lambda i, te, nu: (jnp.minimum(i, nu[0] - 1), 0)),
            scratch_shapes=[pltpu.VMEM((D_MODEL, D_EXPERT), BF16), pltpu.VMEM((D_MODEL, D_EXPERT), BF16),
                            pltpu.VMEM((D_EXPERT, D_MODEL), BF16)]),
        out_shape=jax.ShapeDtypeStruct(xs.shape, jnp.uint32),
        compiler_params=_cparams(1),
        name="experts",
    )(tile_expert, n_used, xs, mp['wg'], mp['wu'], mp['wd'])


def _combine_kernel(h_ref, *refs):
    yg_refs = refs[:TOP_K]
    ew_ref, eye_ref, sg_ref, su_ref, sd_ref, x1_ref, g2_ref, o_ref = refs[TOP_K:]
    act = _glu(h_ref[...], sg_ref[...], su_ref[...])
    acc = jnp.dot(_bf(act), sd_ref[...], preferred_element_type=F32)
    ew = ew_ref[...]
    hi = _bf(ew)
    lo = _bf(ew - hi.astype(F32))
    ew_t = _dot_tn(hi, eye_ref[...]) + _dot_tn(lo, eye_ref[...])
    for k in range(TOP_K):
        acc = acc + ew_t[:, k:k + 1] * _unpack_pairs(yg_refs[k][...]).astype(F32)
    o_ref[...] = x1_ref[...] + g2_ref[...] * acc


def _combine(rows, hffn, yg, ew, mp, x1, mod4, layer, row0, n_out):
    half = D_MODEL // 2
    n_blk = rows.n // TM
    blk0 = row0 // TM
    const = lambda a: pl.BlockSpec(a.shape, lambda i: (0,) * a.ndim)
    tok = lambda w: pl.BlockSpec((TM, w), lambda i: (blk0 + i, 0))
    slot = lambda k: pl.BlockSpec((TM, half), lambda i: (k * n_blk + blk0 + i, 0))
    eye = jnp.eye(TOP_K, dtype=BF16)
    return pl.pallas_call(
        _combine_kernel,
        grid=(n_out // TM,),
        in_specs=[tok(D_MODEL)] + [slot(k) for k in range(TOP_K)]
                 + [pl.BlockSpec((TOP_K, TM), lambda i: (0, blk0 + i)), const(eye),
                    const(mp['sg']), const(mp['su']), const(mp['sd']), tok(D_MODEL),
                    rows.mod_spec(layer, 5, TM, blk0)],
        out_specs=pl.BlockSpec((TM, D_MODEL), lambda i: (i, 0)),
        out_shape=jax.ShapeDtypeStruct((n_out, D_MODEL), F32),
        compiler_params=_cparams(1),
        name=f"combine{layer}",
    )(hffn, *([yg] * TOP_K), ew, eye, mp['sg'], mp['su'], mp['sd'], x1, mod4)


def _moe(rows, hffn, hpack, x1, router, bias, mp, mod4, layer, out_ranges):
    n = rows.n
    sel, eidx, ew, cnt = _router(hffn, router, bias)
    counts = cnt[:, 0].astype(jnp.int32)
    padded = (counts + MOE_TILE - 1) // MOE_TILE * MOE_TILE
    ends = jnp.cumsum(padded)
    n_rows = n * TOP_K + N_EXPERTS * MOE_TILE
    n_tiles = n_rows // MOE_TILE
    base = jnp.broadcast_to((ends - padded).astype(F32)[:, None], (N_EXPERTS, LANES))
    tile_start = jnp.arange(n_tiles, dtype=jnp.int32) * MOE_TILE
    tile_expert = jnp.minimum(jnp.sum((ends[None, :] <= tile_start[:, None]).astype(jnp.int32), axis=1),
                              N_EXPERTS - 1)
    n_used = (ends[-1:] // MOE_TILE).astype(jnp.int32)
    pos = _positions(sel, eidx, base)
    pos3 = pos.reshape(TOP_K, n // SC_CHUNK, SC_CHUNK).transpose(1, 0, 2)
    xs = _sc_scatter(hpack, pos3, n_rows)
    ys = _experts(xs, tile_expert, n_used, mp, layer)
    yg = _sc_gather(ys, pos.reshape(-1))
    return [_combine(rows, hffn, yg, ew, mp, x1, mod4, layer, row0, n_out) for row0, n_out in out_ranges]


def _glu(x, wg, wu):
    hg = jnp.dot(x, wg, preferred_element_type=F32)
    hu = jnp.dot(x, wu, preferred_element_type=F32)
    return _silu(hg) * hu


def kernel(x_prompt, x_sample, c, c_ctx, cache_attn_k, cache_attn_v, state_rwkv_fwd, state_rwkv_bwd,
           state_hgrn_fwd, state_hgrn_bwd, norm1_g, norm2_g, mod_w, mod_b, ab_w_in, ab_w_out, attn_q_norm,
           attn_k_norm, attn_sink, rwkv_mu, rwkv_w0, rwkv_w2, rwkv_a0, rwkv_a2, rwkv_g2, rwkv_k_k, rwkv_k_a,
           rwkv_r_k, rwkv_ln_w, rwkv_ln_b, hgrn_w_in, hgrn_w_out, hgrn_lower_bounds, hgrn_norm_g, moe_router,
           moe_bias, moe_w_gate, moe_w_up, moe_w_down, moe_shared_gate, moe_shared_up, moe_shared_down):
    n_cseq, cseq, _ = x_prompt.shape
    n_lseq, lseq, _ = x_sample.shape
    depth = mod_w.shape[0]
    assert depth == 2 and n_lseq + 1 <= SUBLANES
    assert cseq == PREP_TM and lseq % TM == 0 and lseq % HG_TB == 0 and cseq % HG_TB == 0
    assert (n_cseq * cseq) % TM == 0
    assert n_cseq % RW_BB == 0 and n_lseq % RW_BB == 0 and (n_cseq * cseq) % (lseq * RW_BB) == 0
    rows = _Rows(n_cseq * cseq, n_lseq * lseq, lseq)
    assert rows.n % MOE_TILE == 0 and lseq % MOE_TILE == 0 and rows.n_ctx % MOE_TILE == 0
    kv_w = KV_A * HD_A

    xs = (x_prompt.reshape(rows.n_ctx, D_MODEL), x_sample.reshape(rows.n_lat, D_MODEL))
    cvecs = jnp.concatenate([c_ctx[None, :], c, jnp.zeros((SUBLANES - 1 - n_lseq, D_MODEL), F32)], axis=0)
    mod4 = _modulation(cvecs, mod_w, mod_b).reshape(depth, SUBLANES, 1, 6 * D_MODEL)

    ones_q = _block_ones(W_A, HD_A)
    ones_k = _block_ones(kv_w, HD_A)
    ones_b = _block_ones(W_B, HD_B)
    ones_pair = _block_ones(LANES, HD_B)[:LANES]
    ones_pair = jnp.kron(jnp.eye(2, dtype=BF16), ones_pair)
    cos_t, sin_t = _rope_tables(lseq)

    def moe(l, hffn, hpack, x1, out_ranges):
        mp = {'wg': moe_w_gate, 'wu': moe_w_up, 'wd': moe_w_down,
              'sg': _bf(moe_shared_gate[l]), 'su': _bf(moe_shared_up[l]), 'sd': _bf(moe_shared_down[l])}
        return _moe(rows, hffn, hpack, x1, moe_router[l], moe_bias[l], mp, mod4, l, out_ranges)

    assert W_A == W_B
    all_rows = jnp.zeros((rows.n, W_B), F32)

    pr = {'mu': rwkv_mu[0], 'w0': rwkv_w0[0], 'w2': rwkv_w2[0], 'a0': rwkv_a0[0], 'a2': rwkv_a2[0],
          'g2': rwkv_g2[0], 'k_k': rwkv_k_k[0], 'k_a': rwkv_k_a[0], 'r_k': rwkv_r_k[0].reshape(-1),
          'ln_w': rwkv_ln_w[0], 'ln_b': rwkv_ln_b[0]}
    p_att, p_rw = _inproj(rows, xs, norm1_g[0], mod4, 0, _bf(ab_w_in[0]), (ATT_IN, RWKV_IN), TM)
    qg_t = jnp.tile(attn_q_norm[0], H_A).reshape(1, W_A)
    kg_t = jnp.tile(attn_k_norm[0], KV_A).reshape(1, kv_w)
    o_att, new_k, new_v = _ctx_attention(p_att, n_cseq, cseq, qg_t, kg_t, attn_sink[0], ones_q, ones_k, all_rows)
    past = cache_attn_k.shape[2]
    o_att = _lat_attention(p_att, rows.n_ctx // lseq, n_lseq, lseq, qg_t, kg_t, attn_sink[0], ones_q, ones_k,
                           cos_t, sin_t, cache_attn_k[:, 0].reshape(n_lseq, past, kv_w),
                           cache_attn_v[:, 0].reshape(n_lseq, past, kv_w), o_att)

    pp = _rwkv_prep(rows, p_rw, pr, ones_b)
    zero_st = jnp.zeros((n_cseq, H_B // 2, HD_B, LANES), F32)
    o_f, o_b, sf_c, sb_c = _rwkv_scan(pp, 0, n_cseq, cseq, zero_st, zero_st, ones_pair, all_rows, all_rows)
    o_f, o_b, _, _ = _rwkv_scan(pp, rows.n_ctx, n_lseq, lseq, _state_to_pairs(state_rwkv_fwd[:, 0]),
                                _state_to_pairs(state_rwkv_bwd[:, 0]), ones_pair, o_f, o_b)
    x1, hffn, hpack = _outproj0(rows, xs, o_att, o_f, o_b, pp, pr, ones_b, _bf(ab_w_out[0]), norm2_g[0], mod4, 0)
    (x,) = moe(0, hffn, hpack, x1, [(0, rows.n)])

    (p1,) = _inproj(rows, (x, x), norm1_g[1], mod4, 1, _bf(hgrn_w_in[0]), (IN_C,), IN1_TM)
    zero_h = jnp.zeros((n_cseq, H_C, DK_C, DV_C), F32)
    all_rows_c = jnp.zeros((rows.n, D_C), F32)
    h_f, h_b, hsf_c, hsb_c = _hgrn_scan(p1, 0, n_cseq, cseq, hgrn_lower_bounds, zero_h, zero_h,
                                        all_rows_c, all_rows_c)
    h_f, h_b, _, _ = _hgrn_scan(p1, rows.n_ctx, n_lseq, lseq, hgrn_lower_bounds,
                                state_hgrn_fwd[:, 0], state_hgrn_bwd[:, 0], h_f, h_b)
    x1, hffn, hpack = _outproj1(rows, (x, x), h_f, h_b, p1, hgrn_norm_g[0], _bf(hgrn_w_out[0]), norm2_g[1],
                                mod4, 1)
    y_c, y_l = moe(1, hffn, hpack, x1, [(0, rows.n_ctx), (rows.n_ctx, rows.n_lat)])

    y_prompt = y_c.reshape(n_cseq, cseq, D_MODEL)
    y_sample = y_l.reshape(n_lseq, lseq, D_MODEL)
    return (y_prompt, y_sample,
            new_k.reshape(n_cseq, 1, cseq, KV_A, HD_A), new_v.reshape(n_cseq, 1, cseq, KV_A, HD_A),
            _pairs_to_state(sf_c)[:, None], _pairs_to_state(sb_c)[:, None],
            hsf_c[:, None], hsb_c[:, None])
```

```python
import functools

import numpy as np
import jax
import jax.numpy as jnp
from jax import lax
from jax.experimental import pallas as pl
from jax.experimental.pallas import tpu as pltpu
from jax.experimental.pallas import tpu_sc as plsc

F32 = jnp.float32
BF16 = jnp.bfloat16

D_MODEL = 1024
GRID_W = 64
H_A = 8
KV_A = 2
G_A = H_A // KV_A
HD_A = 64
W_A = H_A * HD_A
WINDOW = 128
QBLK = 128
ROPE_BASE = 10000.0
ROPE_PAIR = HD_A // 4
ATTN_SCALE = HD_A ** -0.5
NEG_INF = -1e30
H_B = 8
HD_B = 64
W_B = H_B * HD_B
LORA_W = 64
LORA_A = 64
LORA_G = 128
GN_EPS = 64e-5
ATT_IN = W_A + 2 * KV_A * HD_A
RWKV_IN = 3 * W_B + LORA_W + LORA_A + LORA_G
IN_AB = ATT_IN + RWKV_IN
H_C = 8
DK_C = 128
DV_C = 128
D_C = H_C * DV_C
CHUNK = 64
IN_C = 5 * D_C
N_EXPERTS = 64
TOP_K = 8
N_GROUPS = 8
TOPK_GROUPS = 4
D_EXPERT = 256
ROUTED_SCALE = 2.5
EPS = 1e-6

LANES = 128
SUBLANES = 8
VMEM_LIMIT = 52 * 1024 * 1024

TM = 512
PREP_TM = 256
IN1_TM = 512
RW_TB = 128
RW_BB = 4
RW_GROUP_BB = 4
HG_TB = 256
MOE_TILE = 1024
POS_TB = 512
SC_CORES = 2
SC_SUBCORES = 16
SC_CHUNK = 64


def _cparams(n_axes):
    return pltpu.CompilerParams(dimension_semantics=("arbitrary",) * n_axes,
                                vmem_limit_bytes=VMEM_LIMIT)


def _bf(x):
    return x.astype(BF16)


def _split2(x):
    hi = lax.bitcast_convert_type(
        lax.bitcast_convert_type(x, jnp.uint32) & jnp.uint32(0xFFFF0000), F32)
    return hi, x - hi


def _seg_sum(x, ones2):
    hi, lo = _split2(x)
    return jnp.dot(jnp.concatenate([_bf(hi), _bf(lo)], axis=1), ones2,
                   preferred_element_type=F32)


def _dot_nt(a, b):
    return lax.dot_general(a, b, (((1,), (1,)), ((), ())), preferred_element_type=F32)


def _dot_tn(a, b):
    return lax.dot_general(a, b, (((0,), (0,)), ((), ())), preferred_element_type=F32)


def _sigmoid(x):
    return 1.0 / (1.0 + jnp.exp(-x))


def _silu(x):
    return x * _sigmoid(x)


def _chunks(seq, n):
    seq = list(seq)
    return [seq[i:i + n] for i in range(0, len(seq), n)]


def _block_ones(width, seg):
    idx = np.arange(width) // seg
    bd = (idx[:, None] == idx[None, :]).astype(np.float32)
    return jnp.asarray(np.concatenate([bd, bd], axis=0), dtype=BF16)


def _mod_kernel(c_ref, w_ref, b_ref, o_ref):
    s = _silu(c_ref[...])
    o_ref[0] = jnp.dot(_bf(s), _bf(w_ref[0]), preferred_element_type=F32) + b_ref[0]


def _modulation(cvecs, mod_w, mod_b):
    depth = mod_w.shape[0]
    n_col = 6 * D_MODEL // D_MODEL
    return pl.pallas_call(
        _mod_kernel,
        grid=(depth, n_col),
        in_specs=[pl.BlockSpec((SUBLANES, D_MODEL), lambda l, j: (0, 0)),
                  pl.BlockSpec((1, D_MODEL, D_MODEL), lambda l, j: (l, 0, j)),
                  pl.BlockSpec((1, 1, D_MODEL), lambda l, j: (l, 0, j))],
        out_specs=pl.BlockSpec((1, SUBLANES, D_MODEL), lambda l, j: (l, 0, j)),
        out_shape=jax.ShapeDtypeStruct((depth, SUBLANES, 6 * D_MODEL), F32),
        compiler_params=_cparams(2),
        name="modulation",
    )(cvecs, mod_w, mod_b.reshape(depth, 1, 6 * D_MODEL))


class _Rows:
    def __init__(self, n_ctx, n_lat, lat_seq):
        self.n_ctx, self.n_lat, self.lat_seq = n_ctx, n_lat, lat_seq
        self.n = n_ctx + n_lat

    def mod_row(self, i, tm):
        nctx_blk = self.n_ctx // tm
        per_seq = self.lat_seq // tm
        return jnp.where(i < nctx_blk, 0, 1 + (i - nctx_blk) // per_seq)

    def mod_spec(self, layer, chunk, tm, blk0=0):
        return pl.BlockSpec((None, None, 1, D_MODEL),
                            lambda i, *_: (layer, self.mod_row(i + blk0, tm), 0, chunk))


def _rms_mod(x, g, sc, sh):
    ms = jnp.mean(x * x, axis=-1, keepdims=True)
    return x * lax.rsqrt(ms + EPS) * g * (1.0 + sc) + sh


def _x_specs(rows, xs, tm=TM):
    xa, xb = xs
    nctx_blk = rows.n_ctx // tm
    lat0 = nctx_blk if xb.shape[0] == rows.n else 0
    return [pl.BlockSpec((tm, D_MODEL), lambda i: (jnp.minimum(i, nctx_blk - 1), 0)),
            pl.BlockSpec((tm, D_MODEL), lambda i: (jnp.maximum(i - nctx_blk, 0) + lat0, 0))]


def _pick_x(rows, xa_ref, xb_ref):
    return jnp.where(pl.program_id(0) < rows.n_ctx // xa_ref.shape[0], xa_ref[...], xb_ref[...])


def _inproj_kernel(rows, splits, xa_ref, xb_ref, g_ref, sh_ref, sc_ref, w_ref, *o_refs):
    h = _rms_mod(_pick_x(rows, xa_ref, xb_ref), g_ref[...], sc_ref[...], sh_ref[...])
    p = jnp.dot(_bf(h), w_ref[...], preferred_element_type=F32)
    lo = 0
    for o_ref, width in zip(o_refs, splits):
        o_ref[...] = p[:, lo:lo + width]
        lo += width


def _inproj(rows, xs, g, mod4, layer, w_bf, splits, tm):
    n_out = w_bf.shape[1]
    return pl.pallas_call(
        functools.partial(_inproj_kernel, rows, splits),
        grid=(rows.n // tm,),
        in_specs=_x_specs(rows, xs, tm) + [
            pl.BlockSpec((1, D_MODEL), lambda i: (0, 0)),
            rows.mod_spec(layer, 0, tm),
            rows.mod_spec(layer, 1, tm),
            pl.BlockSpec((D_MODEL, n_out), lambda i: (0, 0), pipeline_mode=pl.Buffered(1))],
        out_specs=[pl.BlockSpec((tm, wd), lambda i: (i, 0)) for wd in splits],
        out_shape=[jax.ShapeDtypeStruct((rows.n, wd), F32) for wd in splits],
        compiler_params=_cparams(1),
        name=f"inproj{layer}",
    )(*xs, g.reshape(1, D_MODEL), mod4, mod4, w_bf)


def _head_rms(x, gain_t, ones2):
    ms = _seg_sum(x * x, ones2) * (1.0 / HD_A)
    return x * lax.rsqrt(ms + EPS) * gain_t


def _sink_softmax_pv(parts, sink):
    m = jnp.maximum(functools.reduce(jnp.maximum, [jnp.max(s, axis=-1, keepdims=True) for s, _ in parts]), sink)
    den = jnp.exp(sink - m)
    acc = None
    for s, v in parts:
        p = jnp.exp(s - m)
        den = den + jnp.sum(p, axis=-1, keepdims=True)
        pv = jnp.dot(_bf(p), v, preferred_element_type=F32)
        acc = pv if acc is None else acc + pv
    return acc / den


def _ctx_attn_kernel(p_ref, qg_ref, kg_ref, sink_ref, ones_q_ref, ones_k_ref, prev_ref, o_ref, k_ref, v_ref):
    del prev_ref
    p = p_ref[...]
    q = _head_rms(p[:, :W_A], qg_ref[...], ones_q_ref[...]) * ATTN_SCALE
    k = _head_rms(p[:, W_A:W_A + KV_A * HD_A], kg_ref[...], ones_k_ref[...])
    v = p[:, W_A + KV_A * HD_A:ATT_IN]
    k_ref[0] = k
    v_ref[0] = v
    qb, kb, vb = _bf(q), _bf(k), _bf(v)
    outs = []
    for h in range(H_A):
        j = h // G_A
        s = _dot_nt(qb[:, h * HD_A:(h + 1) * HD_A], kb[:, j * HD_A:(j + 1) * HD_A])
        outs.append(_sink_softmax_pv([(s, vb[:, j * HD_A:(j + 1) * HD_A])], sink_ref[h]))
    o_ref[...] = jnp.concatenate(outs, axis=1)


def _ctx_attention(p_att, n_seq, seq, qg_t, kg_t, sink, ones_q, ones_k, prev):
    kv_w = KV_A * HD_A
    return pl.pallas_call(
        _ctx_attn_kernel,
        grid=(n_seq,),
        in_specs=[pl.BlockSpec((seq, ATT_IN), lambda b: (b, 0)),
                  pl.BlockSpec((1, W_A), lambda b: (0, 0)),
                  pl.BlockSpec((1, kv_w), lambda b: (0, 0)),
                  pl.BlockSpec(memory_space=pltpu.SMEM),
                  pl.BlockSpec(ones_q.shape, lambda b: (0, 0)),
                  pl.BlockSpec(ones_k.shape, lambda b: (0, 0)),
                  pl.BlockSpec(memory_space=pl.ANY)],
        out_specs=[pl.BlockSpec((seq, W_A), lambda b: (b, 0)),
                   pl.BlockSpec((1, seq, kv_w), lambda b: (b, 0, 0)),
                   pl.BlockSpec((1, seq, kv_w), lambda b: (b, 0, 0))],
        input_output_aliases={6: 0},
        out_shape=[jax.ShapeDtypeStruct(prev.shape, F32),
                   jax.ShapeDtypeStruct((n_seq, seq, kv_w), F32),
                   jax.ShapeDtypeStruct((n_seq, seq, kv_w), F32)],
        compiler_params=_cparams(1),
        name="ctx_attention",
    )(p_att, qg_t, kg_t, sink, ones_q, ones_k, prev)


def _rope(x, cos_t, sin_t):
    lane = lax.broadcasted_iota(jnp.int32, cos_t.shape, 1)
    low = (lane % (2 * ROPE_PAIR)) < ROPE_PAIR
    outs = []
    for s in range(x.shape[1] // LANES):
        xs = x[:, s * LANES:(s + 1) * LANES]
        partner = jnp.where(low, pltpu.roll(xs, LANES - ROPE_PAIR, 1), pltpu.roll(xs, ROPE_PAIR, 1))
        outs.append(xs * cos_t + partner * sin_t)
    return outs[0] if len(outs) == 1 else jnp.concatenate(outs, axis=1)


def _lat_attn_kernel(seq, p_ref, qg_ref, kg_ref, sink_ref, ones_q_ref, ones_k_ref, cos_ref, sin_ref,
                     kc_ref, vc_ref, prev_ref, o_ref, q_scr, k_scr, v_scr):
    del prev_ref
    kv_w = KV_A * HD_A
    p = p_ref[...]
    q = _head_rms(p[:, :W_A], qg_ref[...], ones_q_ref[...])
    k = _head_rms(p[:, W_A:W_A + kv_w], kg_ref[...], ones_k_ref[...])
    qr = _bf(_rope(q, cos_ref[...], sin_ref[...]) * ATTN_SCALE)
    kr = _bf(_rope(k, cos_ref[...], sin_ref[...]))
    vb = _bf(p[:, W_A + kv_w:ATT_IN])
    for h in range(H_A):
        q_scr[h] = qr[:, h * HD_A:(h + 1) * HD_A]
    for j in range(KV_A):
        k_scr[j] = kr[:, j * HD_A:(j + 1) * HD_A]
        v_scr[j] = vb[:, j * HD_A:(j + 1) * HD_A]
    kc = _bf(kc_ref[0])
    vc = _bf(vc_ref[0])
    n_local = 3 * QBLK
    grp = lax.broadcasted_iota(jnp.int32, (G_A * QBLK, 1), 0) // QBLK

    def block(i, carry):
        q0 = pl.multiple_of(i * QBLK, QBLK)
        start = pl.multiple_of(jnp.clip((i - 1) * QBLK, 0, seq - n_local), QBLK)
        ipos = q0 + lax.broadcasted_iota(jnp.int32, (G_A * QBLK, n_local), 0) % QBLK
        jpos = start + lax.broadcasted_iota(jnp.int32, (G_A * QBLK, n_local), 1)
        band = jnp.abs(jpos - ipos) <= WINDOW
        outs = []
        for j in range(KV_A):
            qs = jnp.concatenate([q_scr[j * G_A + g, pl.ds(q0, QBLK), :] for g in range(G_A)], axis=0)
            kl = k_scr[j, pl.ds(start, n_local), :]
            vl = v_scr[j, pl.ds(start, n_local), :]
            sink = jnp.zeros((G_A * QBLK, 1), F32)
            for g in range(G_A):
                sink = jnp.where(grp == g, sink_ref[j * G_A + g], sink)
            s_loc = jnp.where(band, _dot_nt(qs, kl), NEG_INF)
            s_ctx = _dot_nt(qs, kc[:, j * HD_A:(j + 1) * HD_A])
            o = _sink_softmax_pv([(s_loc, vl), (s_ctx, vc[:, j * HD_A:(j + 1) * HD_A])], sink)
            outs.extend(o[g * QBLK:(g + 1) * QBLK] for g in range(G_A))
        o_ref[pl.ds(q0, QBLK), :] = jnp.concatenate(outs, axis=1)
        return carry

    lax.fori_loop(0, seq // QBLK, block, 0)


def _lat_attention(p_att, row_blk0, n_seq, seq, qg_t, kg_t, sink, ones_q, ones_k, cos_t, sin_t, kc, vc, prev):
    kv_w = KV_A * HD_A
    past = kc.shape[1]
    return pl.pallas_call(
        functools.partial(_lat_attn_kernel, seq),
        grid=(n_seq,),
        in_specs=[pl.BlockSpec((seq, ATT_IN), lambda b: (row_blk0 + b, 0)),
                  pl.BlockSpec((1, W_A), lambda b: (0, 0)),
                  pl.BlockSpec((1, kv_w), lambda b: (0, 0)),
                  pl.BlockSpec(memory_space=pltpu.SMEM),
                  pl.BlockSpec(ones_q.shape, lambda b: (0, 0)),
                  pl.BlockSpec(ones_k.shape, lambda b: (0, 0)),
                  pl.BlockSpec((seq, LANES), lambda b: (0, 0)),
                  pl.BlockSpec((seq, LANES), lambda b: (0, 0)),
                  pl.BlockSpec((1, past, kv_w), lambda b: (b, 0, 0)),
                  pl.BlockSpec((1, past, kv_w), lambda b: (b, 0, 0)),
                  pl.BlockSpec(memory_space=pl.ANY)],
        out_specs=pl.BlockSpec((seq, W_A), lambda b: (row_blk0 + b, 0)),
        out_shape=jax.ShapeDtypeStruct(prev.shape, F32),
        input_output_aliases={10: 0},
        scratch_shapes=[pltpu.VMEM((H_A, seq, HD_A), BF16), pltpu.VMEM((KV_A, seq, HD_A), BF16),
                        pltpu.VMEM((KV_A, seq, HD_A), BF16)],
        compiler_params=_cparams(1),
        name="lat_attention",
    )(p_att, qg_t, kg_t, sink, ones_q, ones_k, cos_t, sin_t, kc, vc, prev)


def _rope_tables(seq):
    pos = np.arange(seq)
    row = (pos // GRID_W).astype(np.float32)
    col = (pos % GRID_W).astype(np.float32)
    d_axis = HD_A // 2
    inv = (ROPE_BASE ** (-np.arange(0, d_axis, 2, dtype=np.float32) / d_axis)).astype(np.float32)
    cos_h = np.zeros((seq, HD_A), np.float32)
    sin_h = np.zeros((seq, HD_A), np.float32)
    for seg, p_ in enumerate((row, col)):
        ang = (p_[:, None] * inv[None, :]).astype(np.float32)
        c, s = np.cos(ang), np.sin(ang)
        base = seg * d_axis
        cos_h[:, base:base + d_axis // 2] = c
        cos_h[:, base + d_axis // 2:base + d_axis] = c
        sin_h[:, base:base + d_axis // 2] = -s
        sin_h[:, base + d_axis // 2:base + d_axis] = s
    rep = LANES // HD_A
    return jnp.asarray(np.tile(cos_h, (1, rep))), jnp.asarray(np.tile(sin_h, (1, rep)))


def _rwkv_prep_kernel(rows, x_ref, prev_ref, next_ref, mu_ref, kk_ref, ka_ref, rk_ref, w0_ref, w2_ref,
                      a0_ref, a2_ref, g2_ref, ones_ref,
                      nkk_ref, r_ref, v_ref, g_ref, bonus_ref,
                      wf_ref, kaf_ref, kdf_ref, wb_ref, kab_ref, kdb_ref):
    i = pl.program_id(0)
    nctx_blk = rows.n_ctx // PREP_TM
    per_seq = rows.lat_seq // PREP_TM
    is_ctx = i < nctx_blk
    first = jnp.logical_or(is_ctx, (i - nctx_blk) % per_seq == 0)
    last = jnp.logical_or(is_ctx, (i - nctx_blk) % per_seq == per_seq - 1)
    x = x_ref[...]
    ridx = lax.broadcasted_iota(jnp.int32, x.shape, 0)
    prev_row = jnp.where(first, 0.0, prev_ref[SUBLANES - 1:SUBLANES, :])
    next_row = jnp.where(last, 0.0, next_ref[0:1, :])
    xm1 = jnp.where(ridx == 0, prev_row, pltpu.roll(x, 1, 0))
    xp1 = jnp.where(ridx == PREP_TM - 1, next_row, pltpu.roll(x, PREP_TM - 1, 0))
    pw = x + (0.5 * (xm1 + xp1) - x) * mu_ref[...]

    r = pw[:, 0:W_B]
    k = pw[:, W_B:2 * W_B]
    v = pw[:, 2 * W_B:3 * W_B]
    wd = pw[:, 3 * W_B:3 * W_B + LORA_W]
    ad = pw[:, 3 * W_B + LORA_W:3 * W_B + LORA_W + LORA_A]
    gd = pw[:, 3 * W_B + LORA_W + LORA_A:]
    ones2 = ones_ref[...]

    kk = k * kk_ref[...]
    kk = kk / jnp.maximum(jnp.sqrt(_seg_sum(kk * kk, ones2)), 1e-12)
    nkk_ref[...] = -kk
    r_ref[...] = r
    v_ref[...] = v
    g_ref[...] = jnp.dot(_bf(_sigmoid(gd)), g2_ref[...], preferred_element_type=F32)
    tw = _bf(jnp.tanh(wd))
    adb = _bf(ad)
    bonus = jnp.zeros_like(r)
    for d, (w_o, ka_o, kd_o) in enumerate(((wf_ref, kaf_ref, kdf_ref), (wb_ref, kab_ref, kdb_ref))):
        z = -(w0_ref[d:d + 1, :] + jnp.dot(tw, w2_ref[d], preferred_element_type=F32))
        softplus = jnp.maximum(z, 0.0) + jnp.log(1.0 + jnp.exp(-jnp.abs(z)))
        w_o[...] = jnp.exp(-jnp.exp(-softplus - 0.5))
        a = _sigmoid(a0_ref[d:d + 1, :] + jnp.dot(adb, a2_ref[d], preferred_element_type=F32))
        kd = k * (1.0 + (a - 1.0) * ka_ref[...])
        ka_o[...] = kk * a
        kd_o[...] = kd
        bonus = bonus + _seg_sum(r * kd * rk_ref[...], ones2) * v
    bonus_ref[...] = bonus


def _rwkv_prep(rows, p_rw, pr, ones_b):
    n = rows.n
    n_halo = n // SUBLANES
    blk_halo = PREP_TM // SUBLANES
    row = lambda a: a.reshape(1, -1)
    full = lambda a: pl.BlockSpec(a.shape, lambda i: (0,) * a.ndim)
    consts = [row(pr['mu']), row(pr['k_k']), row(pr['k_a']), row(pr['r_k']), pr['w0'], _bf(pr['w2']),
              pr['a0'], _bf(pr['a2']), _bf(pr['g2']), ones_b]
    outs = pl.pallas_call(
        functools.partial(_rwkv_prep_kernel, rows),
        grid=(n // PREP_TM,),
        in_specs=[pl.BlockSpec((PREP_TM, RWKV_IN), lambda i: (i, 0)),
                  pl.BlockSpec((SUBLANES, RWKV_IN), lambda i: (jnp.maximum(i * blk_halo - 1, 0), 0)),
                  pl.BlockSpec((SUBLANES, RWKV_IN), lambda i: (jnp.minimum((i + 1) * blk_halo, n_halo - 1), 0))]
                 + [full(a) for a in consts],
        out_specs=[pl.BlockSpec((PREP_TM, W_B), lambda i: (i, 0))] * 11,
        out_shape=[jax.ShapeDtypeStruct((n, W_B), F32)] * 11,
        compiler_params=_cparams(1),
        name="rwkv_prep",
    )(p_rw, p_rw, p_rw, *consts)
    names = ('nkk', 'r', 'v', 'g', 'bonus', 'w_f', 'ka_f', 'kd_f', 'w_b', 'ka_b', 'kd_b')
    return dict(zip(names, outs))


def _rwkv_scan_kernel(n_tb, nkkf_ref, rf_ref, vf_ref, wf_ref, kaf_ref, kdf_ref,
                      nkkb_ref, rb_ref, vb_ref, wb_ref, kab_ref, kdb_ref,
                      s0f_ref, s0b_ref, ones_ref, prevf_ref, prevb_ref,
                      of_ref, ob_ref, sff_ref, sfb_ref, sf_scr, sb_scr, vt_scr):
    del prevf_ref, prevb_ref
    s_scr = (sf_scr, sb_scr)
    tb = pl.program_id(1)
    n_pair = H_B // 2
    half = RW_TB // 2
    dirs = ((nkkf_ref, rf_ref, vf_ref, wf_ref, kaf_ref, kdf_ref, of_ref, False),
            (nkkb_ref, rb_ref, vb_ref, wb_ref, kab_ref, kdb_ref, ob_ref, True))

    @pl.when(tb == 0)
    def _():
        sf_scr[...] = s0f_ref[...]
        sb_scr[...] = s0b_ref[...]

    lane = lax.broadcasted_iota(jnp.int32, (HD_B, LANES), 1)
    for d, refs in enumerate(dirs):
        v_ref = refs[2]
        for bb in range(RW_BB):
            for p in range(n_pair):
                vt = v_ref[bb, :, p * LANES:(p + 1) * LANES].T
                top, bot = vt[:HD_B], vt[HD_B:]
                for s in range(2):
                    if s == 0:
                        t2 = jnp.where(lane < HD_B, top, pltpu.roll(bot, HD_B, 1))
                    else:
                        t2 = jnp.where(lane < HD_B, pltpu.roll(top, HD_B, 1), bot)
                    vt_scr[d, bb, p, s] = t2

    ones2 = ones_ref[...]
    row8 = lax.broadcasted_iota(jnp.int32, (SUBLANES, LANES), 0)
    lane8 = lax.broadcasted_iota(jnp.int32, (SUBLANES, LANES), 1)
    sel_r = jnp.logical_or(jnp.logical_and(row8 % 2 == 0, lane8 < HD_B),
                           jnp.logical_and(row8 % 2 == 1, lane8 >= HD_B))

    def row_of(rev, tt):
        return RW_TB - 1 - tt if rev else tt

    def emit_output(d, bb, tau):
        r_ref, o_ref = dirs[d][1], dirs[d][6]
        r = r_ref[bb, pl.ds(tau, 1), :]
        r8 = jnp.zeros((SUBLANES, LANES), F32)
        for p in range(n_pair):
            rp = jnp.broadcast_to(r[:, p * LANES:(p + 1) * LANES], (SUBLANES, LANES))
            r8 = jnp.where(jnp.logical_and(sel_r, row8 // 2 == p), rp, r8)
        s_all = jnp.concatenate([_bf(s_scr[d][bb, p]) for p in range(n_pair)], axis=0)
        o8 = _dot_nt(_bf(r8), s_all)
        o_parts = []
        for p in range(n_pair):
            for h in range(2):
                o_parts.append(o8[2 * p + h:2 * p + h + 1, p * HD_B:(p + 1) * HD_B])
        o_ref[bb, pl.ds(tau, 1), :] = jnp.concatenate(o_parts, axis=1)

    groups = [(d, bbs) for d in range(2) for bbs in _chunks(range(RW_BB), RW_GROUP_BB)]

    def reduce_phase(grp, tt):
        d, bbs = grp
        rev = dirs[d][7]
        tau = row_of(rev, tt)
        sub = tau // half
        lt = tau % half
        mask = jnp.logical_or(lane == lt, lane == lt + HD_B)
        lhs = []
        for bb in bbs:
            emit_output(d, bb, row_of(rev, jnp.maximum(tt - 1, 0)))
            nkk = dirs[d][0][bb, pl.ds(tau, 1), :]
            for p in range(n_pair):
                prod = s_scr[d][bb, p] * nkk[:, p * LANES:(p + 1) * LANES]
                lhs.append(jnp.concatenate([_bf(prod), _bf(jnp.where(mask, vt_scr[d, bb, p, sub], 0.0))],
                                           axis=1))
        return jnp.dot(jnp.concatenate(lhs, axis=0), ones2, preferred_element_type=F32)

    def update_phase(grp, tt, red):
        d, bbs = grp
        _, _, _, w_ref, ka_ref, kd_ref, _, rev = dirs[d]
        tau = row_of(rev, tt)
        for k, bb in enumerate(bbs):
            w = w_ref[bb, pl.ds(tau, 1), :]
            ka = ka_ref[bb, pl.ds(tau, 1), :]
            kd = kd_ref[bb, pl.ds(tau, 1), :]
            for p in range(n_pair):
                sl = slice(p * LANES, (p + 1) * LANES)
                r0 = (k * n_pair + p) * HD_B
                sa = red[r0:r0 + HD_B, :LANES]
                vcol = red[r0:r0 + HD_B, LANES:]
                s_scr[d][bb, p] = s_scr[d][bb, p] * w[:, sl] + sa * ka[:, sl] + vcol * kd[:, sl]

    def step(tt, carry):
        reds = [reduce_phase(g, tt) for g in groups]
        for g, red in zip(groups, reds):
            update_phase(g, tt, red)
        return carry

    lax.fori_loop(0, RW_TB, step, 0)
    for d in range(2):
        for bb in range(RW_BB):
            emit_output(d, bb, row_of(dirs[d][7], RW_TB - 1))

    @pl.when(tb == n_tb - 1)
    def _():
        sff_ref[...] = sf_scr[...]
        sfb_ref[...] = sb_scr[...]


def _rwkv_scan(pp, row0, n_seq, seq, s0_f, s0_b, ones_pair, prev_f, prev_b):
    n_tb = seq // RW_TB
    n_pair = H_B // 2
    blk0 = row0 // seq
    view = lambda a: a.reshape(a.shape[0] // seq, seq, W_B)
    fwd = pl.BlockSpec((RW_BB, RW_TB, W_B), lambda b, t: (blk0 // RW_BB + b, t, 0))
    bwd = pl.BlockSpec((RW_BB, RW_TB, W_B), lambda b, t: (blk0 // RW_BB + b, n_tb - 1 - t, 0))
    st = pl.BlockSpec((RW_BB, n_pair, HD_B, LANES), lambda b, t: (b, 0, 0, 0))
    ins_f = [view(pp[k]) for k in ('nkk', 'r', 'v', 'w_f', 'ka_f', 'kd_f')]
    ins_b = [view(pp[k]) for k in ('nkk', 'r', 'v', 'w_b', 'ka_b', 'kd_b')]
    st_shape = jax.ShapeDtypeStruct((n_seq, n_pair, HD_B, LANES), F32)
    o_shape = jax.ShapeDtypeStruct(view(prev_f).shape, F32)
    any_spec = pl.BlockSpec(memory_space=pl.ANY)
    o_f, o_b, sf, sb = pl.pallas_call(
        functools.partial(_rwkv_scan_kernel, n_tb),
        grid=(n_seq // RW_BB, n_tb),
        in_specs=[fwd] * 6 + [bwd] * 6 + [st, st, pl.BlockSpec(ones_pair.shape, lambda b, t: (0, 0)),
                                           any_spec, any_spec],
        out_specs=[fwd, bwd, st, st],
        out_shape=[o_shape, o_shape, st_shape, st_shape],
        input_output_aliases={15: 0, 16: 1},
        scratch_shapes=[pltpu.VMEM((RW_BB, n_pair, HD_B, LANES), F32),
                        pltpu.VMEM((RW_BB, n_pair, HD_B, LANES), F32),
                        pltpu.VMEM((2, RW_BB, n_pair, 2, HD_B, LANES), F32)],
        compiler_params=_cparams(2),
        name="rwkv_scan",
    )(*ins_f, *ins_b, s0_f, s0_b, ones_pair, view(prev_f), view(prev_b))
    return o_f.reshape(prev_f.shape), o_b.reshape(prev_b.shape), sf, sb


def _state_to_pairs(s):
    b = s.shape[0]
    return s.reshape(b, H_B // 2, 2, HD_B, HD_B).transpose(0, 1, 3, 2, 4).reshape(b, H_B // 2, HD_B, 2 * HD_B)


def _pairs_to_state(s):
    b = s.shape[0]
    return s.reshape(b, H_B // 2, HD_B, 2, HD_B).transpose(0, 1, 3, 2, 4).reshape(b, H_B, HD_B, HD_B)


def _tail(x, y, g1, n2g, sc2, sh2, x1_ref, h_ref, hp_ref):
    x1 = x + g1 * y
    x1_ref[...] = x1
    h = _rms_mod(x1, n2g, sc2, sh2)
    h_ref[...] = _bf(h)
    hp_ref[...] = _pack_pairs(h)


def _outproj0_kernel(rows, xa_ref, xb_ref, oa_ref, of_ref, ob_ref, bonus_ref, g_ref, lnw_ref, lnb_ref, ones_ref,
                     w_ref, g1_ref, n2g_ref, sc2_ref, sh2_ref, x1_ref, h_ref, hp_ref):
    o_sum = of_ref[...] + ob_ref[...]
    ones2 = ones_ref[...]
    mean = _seg_sum(o_sum, ones2) * (1.0 / HD_B)
    cen = o_sum - mean
    var = _seg_sum(cen * cen, ones2) * (1.0 / HD_B)
    gn = cen * lax.rsqrt(var + GN_EPS) * lnw_ref[...] + lnb_ref[...]
    o_rw = (gn + bonus_ref[...]) * g_ref[...]
    mix = jnp.concatenate([_bf(oa_ref[...]), _bf(o_rw)], axis=1)
    y = jnp.dot(mix, w_ref[...], preferred_element_type=F32)
    _tail(_pick_x(rows, xa_ref, xb_ref), y, g1_ref[...], n2g_ref[...], sc2_ref[...], sh2_ref[...], x1_ref, h_ref, hp_ref)


def _outproj0(rows, xs, o_att, o_f, o_b, pp, pr, ones_b, w_out_bf, n2g, mod4, layer):
    n = rows.n
    tok = lambda w: pl.BlockSpec((TM, w), lambda i: (i, 0))
    const = lambda a: pl.BlockSpec(a.shape, lambda i: (0,) * a.ndim)
    lnw, lnb, n2 = pr['ln_w'].reshape(1, -1), pr['ln_b'].reshape(1, -1), n2g.reshape(1, -1)
    return pl.pallas_call(
        functools.partial(_outproj0_kernel, rows),
        grid=(n // TM,),
        in_specs=_x_specs(rows, xs) + [tok(W_A), tok(W_B), tok(W_B), tok(W_B), tok(W_B),
                  const(lnw), const(lnb), const(ones_b), const(w_out_bf),
                  rows.mod_spec(layer, 2, TM), const(n2), rows.mod_spec(layer, 4, TM), rows.mod_spec(layer, 3, TM)],
        out_specs=[tok(D_MODEL), tok(D_MODEL), tok(D_MODEL // 2)],
        out_shape=[jax.ShapeDtypeStruct((n, D_MODEL), F32), jax.ShapeDtypeStruct((n, D_MODEL), BF16),
                   jax.ShapeDtypeStruct((n, D_MODEL // 2), jnp.uint32)],
        compiler_params=_cparams(1),
        name="outproj0",
    )(*xs, o_att, o_f, o_b, pp['bonus'], pp['g'], lnw, lnb, ones_b, w_out_bf, mod4, n2, mod4, mod4)


def _outproj1_kernel(rows, xa_ref, xb_ref, of_ref, ob_ref, gate_ref, ng_ref, w_ref, g1_ref, n2g_ref, sc2_ref, sh2_ref,
                     x1_ref, h_ref, hp_ref):
    o_sum = of_ref[...] + ob_ref[...]
    parts = []
    for h in range(H_C):
        oh = o_sum[:, h * DV_C:(h + 1) * DV_C]
        parts.append(oh * lax.rsqrt(jnp.mean(oh * oh, axis=-1, keepdims=True) + EPS))
    o = jnp.concatenate(parts, axis=1) * ng_ref[...] * _silu(gate_ref[...])
    y = jnp.dot(_bf(o), w_ref[...], preferred_element_type=F32)
    _tail(_pick_x(rows, xa_ref, xb_ref), y, g1_ref[...], n2g_ref[...], sc2_ref[...], sh2_ref[...], x1_ref, h_ref, hp_ref)


def _outproj1(rows, xs, o_f, o_b, p1, norm_g, w_out_bf, n2g, mod4, layer):
    n = rows.n
    tok = lambda w: pl.BlockSpec((TM, w), lambda i: (i, 0))
    const = lambda a: pl.BlockSpec(a.shape, lambda i: (0,) * a.ndim)
    ng, n2 = norm_g.reshape(1, -1), n2g.reshape(1, -1)
    return pl.pallas_call(
        functools.partial(_outproj1_kernel, rows),
        grid=(n // TM,),
        in_specs=_x_specs(rows, xs) + [tok(D_C), tok(D_C), pl.BlockSpec((TM, D_C), lambda i: (i, 4)),
                  const(ng), const(w_out_bf),
                  rows.mod_spec(layer, 2, TM), const(n2), rows.mod_spec(layer, 4, TM), rows.mod_spec(layer, 3, TM)],
        out_specs=[tok(D_MODEL), tok(D_MODEL), tok(D_MODEL // 2)],
        out_shape=[jax.ShapeDtypeStruct((n, D_MODEL), F32), jax.ShapeDtypeStruct((n, D_MODEL), BF16),
                   jax.ShapeDtypeStruct((n, D_MODEL // 2), jnp.uint32)],
        compiler_params=_cparams(1),
        name="outproj1",
    )(*xs, o_f, o_b, p1, ng, w_out_bf, mod4, n2, mod4, mod4)


def _hgrn_kernel(n_tb, qf_ref, ff_ref, if_ref, qb_ref, fb_ref, ib_ref, lbp_ref, s0f_ref, s0b_ref,
                 trif_ref, trib_ref, prevf_ref, prevb_ref, of_ref, ob_ref, sff_ref, sfb_ref, s_scr):
    del prevf_ref, prevb_ref
    tb = pl.program_id(1)

    @pl.when(tb == 0)
    def _():
        for h in range(H_C):
            s_scr[0, h] = s0f_ref[0, h].T
            s_scr[1, h] = s0b_ref[0, h].T

    lbp = lbp_ref[...]
    e = jnp.exp(lbp - jnp.max(lbp, axis=0, keepdims=True))
    sm = e / jnp.sum(e, axis=0, keepdims=True)
    lb = (sm[0:1] + sm[1:2]) - sm[0:1]

    n_chunk = HG_TB // CHUNK
    ti = lax.broadcasted_iota(jnp.int32, (HG_TB, HG_TB), 0)
    si = lax.broadcasted_iota(jnp.int32, (HG_TB, HG_TB), 1)
    same = (ti // CHUNK) == (si // CHUNK)
    dirs = ((qf_ref, ff_ref, if_ref, of_ref, trif_ref, jnp.logical_and(same, ti >= si), CHUNK - 1, False),
            (qb_ref, fb_ref, ib_ref, ob_ref, trib_ref, jnp.logical_and(same, ti <= si), 0, True))

    staged = []
    for d, (q_ref, f_ref, i_ref, o_ref, tri_ref, causal, last_row, rev) in enumerate(dirs):
        q = _silu(q_ref[...])
        f = lb + (1.0 - lb) * _sigmoid(f_ref[...])
        k = 1.0 - f
        v = _bf(i_ref[...])
        g = jnp.log(f)
        g1 = _bf(g)
        g2 = _bf(g - g1.astype(F32))
        tri2 = tri_ref[...]
        b_parts, last_parts, dec = [], [], []
        for c in range(n_chunk):
            rc = slice(c * CHUNK, (c + 1) * CHUNK)
            bc = jnp.dot(tri2, jnp.concatenate([g1[rc], g2[rc]], axis=0), preferred_element_type=F32)
            b_parts.append(bc)
            last = bc[last_row:last_row + 1]
            last_parts.append(jnp.broadcast_to(last, bc.shape))
            dec.append(jnp.exp(last))
        b = jnp.concatenate(b_parts, axis=0)
        b_last = jnp.concatenate(last_parts, axis=0)
        staged.append((_bf(q * jnp.exp(b)), _bf(k * jnp.exp(-b)), _bf(k * jnp.exp(b_last - b)), v, dec))

    for h in range(H_C):
        sl = slice(h * DK_C, (h + 1) * DK_C)
        for d, (q_ref, f_ref, i_ref, o_ref, tri_ref, causal, last_row, rev) in enumerate(dirs):
            q_in, k_in, k_out, v, dec = staged[d]
            qh, vh = q_in[:, sl], v[:, sl]
            att = jnp.where(causal, _dot_nt(qh, k_in[:, sl]), 0.0)
            o_intra = jnp.dot(_bf(att), vh, preferred_element_type=F32)
            s_t = s_scr[d, h]
            for c in (range(n_chunk - 1, -1, -1) if rev else range(n_chunk)):
                rc = slice(c * CHUNK, (c + 1) * CHUNK)
                o_ref[rc, sl] = o_intra[rc] + _dot_nt(qh[rc], _bf(s_t))
                s_t = dec[c][:, sl] * s_t + _dot_tn(vh[rc], k_out[rc, sl])
            s_scr[d, h] = s_t

    @pl.when(tb == n_tb - 1)
    def _():
        for h in range(H_C):
            sff_ref[0, h] = s_scr[0, h].T
            sfb_ref[0, h] = s_scr[1, h].T


def _hgrn_scan(p1, row0, n_seq, seq, lb_params, s0_f, s0_b, prev_f, prev_b):
    n_tb = seq // HG_TB
    blk0 = row0 // HG_TB
    tri = np.tril(np.ones((CHUNK, CHUNK), np.float32))
    tri_f = jnp.asarray(np.concatenate([tri] * 2, axis=1), dtype=BF16)
    tri_b = jnp.asarray(np.concatenate([tri.T] * 2, axis=1), dtype=BF16)
    fwd = lambda col: pl.BlockSpec((HG_TB, D_C), lambda b, t: (blk0 + b * n_tb + t, col))
    bwd = lambda col: pl.BlockSpec((HG_TB, D_C), lambda b, t: (blk0 + b * n_tb + n_tb - 1 - t, col))
    st = pl.BlockSpec((1, H_C, DK_C, DV_C), lambda b, t: (b, 0, 0, 0))
    const = lambda a: pl.BlockSpec(a.shape, lambda b, t: (0,) * a.ndim)
    o_shape = jax.ShapeDtypeStruct(prev_f.shape, F32)
    st_shape = jax.ShapeDtypeStruct((n_seq, H_C, DK_C, DV_C), F32)
    any_spec = pl.BlockSpec(memory_space=pl.ANY)
    return pl.pallas_call(
        functools.partial(_hgrn_kernel, n_tb),
        grid=(n_seq, n_tb),
        in_specs=[fwd(0), fwd(1), fwd(3), bwd(0), bwd(2), bwd(3), const(lb_params), st, st,
                  const(tri_f), const(tri_b), any_spec, any_spec],
        out_specs=[fwd(0), bwd(0), st, st],
        out_shape=[o_shape, o_shape, st_shape, st_shape],
        input_output_aliases={11: 0, 12: 1},
        scratch_shapes=[pltpu.VMEM((2, H_C, DV_C, DK_C), F32)],
        compiler_params=_cparams(2),
        name="hgrn_scan",
    )(p1, p1, p1, p1, p1, p1, lb_params, s0_f, s0_b, tri_f, tri_b, prev_f, prev_b)


def _router_kernel(h_ref, rhi_ref, rlo_ref, bias_ref, sel_ref, eidx_ref, ew_ref, cnt_ref):
    x = h_ref[...]
    tm = x.shape[0]
    logits = _dot_nt(rhi_ref[...], x) + _dot_nt(rlo_ref[...], x)
    scores = _sigmoid(logits)
    biased = scores + bias_ref[...]
    per = N_EXPERTS // N_GROUPS
    sub = lax.broadcasted_iota(jnp.int32, (per, tm), 0)
    gs_rows = []
    for g in range(N_GROUPS):
        blk = biased[g * per:(g + 1) * per]
        m1 = jnp.max(blk, axis=0, keepdims=True)
        first = jnp.min(jnp.where(blk == m1, sub, per), axis=0, keepdims=True)
        m2 = jnp.max(jnp.where(sub == first, -jnp.inf, blk), axis=0, keepdims=True)
        gs_rows.append(m1 + m2)
    gs = jnp.concatenate(gs_rows, axis=0)
    gi = lax.broadcasted_iota(jnp.int32, gs.shape, 0)
    rank = jnp.zeros(gs.shape, jnp.int32)
    for s in range(1, N_GROUPS):
        other = pltpu.roll(gs, s, 0)
        oi = pltpu.roll(gi, s, 0)
        beats = jnp.logical_or(other > gs, jnp.logical_and(other == gs, oi < gi))
        rank = rank + jnp.where(beats, 1, 0)
    keep = jnp.where(rank < TOPK_GROUPS, 1.0, 0.0)
    emask = jnp.concatenate([jnp.broadcast_to(keep[g:g + 1], (per, tm)) for g in range(N_GROUPS)], axis=0)
    cur = jnp.where(emask > 0.0, biased, -jnp.inf)
    ei = lax.broadcasted_iota(jnp.int32, cur.shape, 0)
    sel = jnp.zeros(cur.shape, F32)
    idxs, vals = [], []
    for _ in range(TOP_K):
        m = jnp.max(cur, axis=0, keepdims=True)
        idx = jnp.min(jnp.where(cur == m, ei, N_EXPERTS), axis=0, keepdims=True)
        pick = ei == idx
        idxs.append(idx)
        vals.append(jnp.sum(jnp.where(pick, scores, 0.0), axis=0, keepdims=True))
        sel = jnp.where(pick, 1.0, sel)
        cur = jnp.where(pick, -jnp.inf, cur)
    w = jnp.concatenate(vals, axis=0)
    eidx_ref[...] = jnp.concatenate(idxs, axis=0)
    ew_ref[...] = w / jnp.sum(w, axis=0, keepdims=True) * ROUTED_SCALE
    sel_ref[...] = _bf(sel)

    @pl.when(pl.program_id(0) == 0)
    def _():
        cnt_ref[...] = jnp.zeros_like(cnt_ref)

    cnt_ref[...] += jnp.sum(sel, axis=1, keepdims=True)


def _router(hffn, router, bias):
    n = hffn.shape[0]
    r_t = router.T
    r_hi = _bf(r_t)
    r_lo = _bf(r_t - r_hi.astype(F32))
    const = lambda a: pl.BlockSpec(a.shape, lambda i: (0,) * a.ndim)
    b_col = bias.reshape(N_EXPERTS, 1)
    return pl.pallas_call(
        _router_kernel,
        grid=(n // TM,),
        in_specs=[pl.BlockSpec((TM, D_MODEL), lambda i: (i, 0)), const(r_hi), const(r_lo), const(b_col)],
        out_specs=[pl.BlockSpec((N_EXPERTS, TM), lambda i: (0, i)),
                   pl.BlockSpec((TOP_K, TM), lambda i: (0, i)),
                   pl.BlockSpec((TOP_K, TM), lambda i: (0, i)),
                   pl.BlockSpec((N_EXPERTS, LANES), lambda i: (0, 0))],
        out_shape=[jax.ShapeDtypeStruct((N_EXPERTS, n), BF16),
                   jax.ShapeDtypeStruct((TOP_K, n), jnp.int32),
                   jax.ShapeDtypeStruct((TOP_K, n), F32),
                   jax.ShapeDtypeStruct((N_EXPERTS, LANES), F32)],
        compiler_params=_cparams(1),
        name="router",
    )(hffn, r_hi, r_lo, b_col)


def _positions_kernel(sel_ref, eidx_ref, base_ref, upper_ref, pos_ref, carry_ref):
    @pl.when(pl.program_id(0) == 0)
    def _():
        carry_ref[...] = jnp.zeros_like(carry_ref)

    sel = sel_ref[...]
    rank = jnp.dot(sel, upper_ref[...], preferred_element_type=F32)
    pos_e = base_ref[:, 0:1] + carry_ref[:, 0:1] + rank
    ei = lax.broadcasted_iota(jnp.int32, pos_e.shape, 0)
    eidx = eidx_ref[...]
    rows = [jnp.sum(jnp.where(ei == eidx[k:k + 1], pos_e, 0.0), axis=0, keepdims=True) for k in range(TOP_K)]
    pos_ref[...] = jnp.concatenate(rows, axis=0).astype(jnp.int32)
    carry_ref[...] += jnp.sum(sel.astype(F32), axis=1, keepdims=True)


def _positions(sel, eidx, base):
    n = sel.shape[1]
    pb = POS_TB
    upper = jnp.asarray(np.triu(np.ones((pb, pb), np.float32), 1), dtype=BF16)
    return pl.pallas_call(
        _positions_kernel,
        grid=(n // pb,),
        in_specs=[pl.BlockSpec((N_EXPERTS, pb), lambda i: (0, i)),
                  pl.BlockSpec((TOP_K, pb), lambda i: (0, i)),
                  pl.BlockSpec((N_EXPERTS, LANES), lambda i: (0, 0)),
                  pl.BlockSpec((pb, pb), lambda i: (0, 0))],
        out_specs=pl.BlockSpec((TOP_K, pb), lambda i: (0, i)),
        out_shape=jax.ShapeDtypeStruct((TOP_K, n), jnp.int32),
        scratch_shapes=[pltpu.VMEM((N_EXPERTS, LANES), F32)],
        compiler_params=_cparams(1),
        name="positions",
    )(sel, eidx, base, upper)


def _pack_pairs(x):
    half = x.shape[1] // 2
    bits = lax.bitcast_convert_type(_bf(x).astype(F32), jnp.uint32)
    return (bits[:, :half] >> 16) | (bits[:, half:] & jnp.uint32(0xFFFF0000))


def _unpack_pairs(w):
    lo = lax.bitcast_convert_type(w << 16, F32)
    hi = lax.bitcast_convert_type(w & jnp.uint32(0xFFFF0000), F32)
    return jnp.concatenate([_bf(lo), _bf(hi)], axis=1)


def _sc_gather(table, idx):
    b, w = idx.shape[0], table.shape[1]
    n_workers = SC_CORES * SC_SUBCORES
    per_w = b // n_workers
    assert b % (n_workers * SC_CHUNK) == 0
    mesh = plsc.VectorSubcoreMesh(core_axis_name="c", subcore_axis_name="s")

    n_chunk = per_w // SC_CHUNK
    assert n_chunk % 2 == 0

    @functools.partial(
        pl.kernel, mesh=mesh, out_type=jax.ShapeDtypeStruct((b, w), table.dtype),
        scratch_types=[pltpu.VMEM((2, SC_CHUNK), jnp.int32), pltpu.VMEM((2, SC_CHUNK, w), table.dtype),
                       pltpu.SemaphoreType.DMA((2,))])
    def gather(table_hbm, idx_hbm, out_hbm, idx_v, rows_v, sems):
        wid = lax.axis_index("s") * SC_CORES + lax.axis_index("c")
        base = wid * per_w

        def start(c, slot):
            off = pl.multiple_of(base + c * SC_CHUNK, SC_CHUNK)
            pltpu.sync_copy(idx_hbm.at[pl.ds(off, SC_CHUNK)], idx_v.at[slot])
            pltpu.async_copy(table_hbm.at[idx_v.at[slot]], rows_v.at[slot], sems.at[slot])

        def finish(c, slot):
            off = pl.multiple_of(base + c * SC_CHUNK, SC_CHUNK)
            pltpu.make_async_copy(table_hbm.at[idx_v.at[slot]], rows_v.at[slot], sems.at[slot]).wait()
            pltpu.sync_copy(rows_v.at[slot], out_hbm.at[pl.ds(off, SC_CHUNK)])

        start(0, 0)

        @pl.loop(0, n_chunk, step=2)
        def _(c):
            start(c + 1, 1)
            finish(c, 0)

            @pl.when(c + 2 < n_chunk)
            def _():
                start(c + 2, 0)

            finish(c + 1, 1)

    return gather(table, idx)


def _sc_scatter(src, pos3, n_rows):
    n, w = src.shape
    n_workers = SC_CORES * SC_SUBCORES
    per_w = n // n_workers
    assert n % (n_workers * SC_CHUNK) == 0
    mesh = plsc.VectorSubcoreMesh(core_axis_name="c", subcore_axis_name="s")

    n_chunk = per_w // SC_CHUNK
    assert n_chunk % 2 == 0

    @functools.partial(
        pl.kernel, mesh=mesh, out_type=jax.ShapeDtypeStruct((n_rows, w), src.dtype),
        scratch_types=[pltpu.VMEM((2, TOP_K, SC_CHUNK), jnp.int32), pltpu.VMEM((2, SC_CHUNK, w), src.dtype),
                       pltpu.SemaphoreType.DMA((2,)), pltpu.SemaphoreType.DMA((2,))])
    def scatter(src_hbm, pos_hbm, out_hbm, idx_v, rows_v, ld_sems, sc_sems):
        wid = lax.axis_index("s") * SC_CORES + lax.axis_index("c")
        base = wid * per_w

        def loads(c, slot):
            off = pl.multiple_of(base + c * SC_CHUNK, SC_CHUNK)
            return (pltpu.make_async_copy(src_hbm.at[pl.ds(off, SC_CHUNK)], rows_v.at[slot], ld_sems.at[slot]),
                    pltpu.make_async_copy(pos_hbm.at[off // SC_CHUNK], idx_v.at[slot], ld_sems.at[slot]))

        def scatter_chunk(slot):
            copies = [pltpu.async_copy(rows_v.at[slot], out_hbm.at[idx_v.at[slot, k]], sc_sems.at[slot])
                      for k in range(TOP_K)]
            for cp in copies:
                cp.wait()

        def half_step(c, slot):
            for cp in loads(c, slot):
                cp.wait()

            @pl.when(c + 1 < n_chunk)
            def _():
                for cp in loads(c + 1, 1 - slot):
                    cp.start()

            scatter_chunk(slot)

        for cp in loads(0, 0):
            cp.start()

        @pl.loop(0, n_chunk, step=2)
        def _(c):
            half_step(c, 0)
            half_step(c + 1, 1)

    return scatter(src, pos3)


def _experts_kernel(te_ref, nu_ref, xs_ref, wg_ref, wu_ref, wd_ref, ys_ref, wg_bf, wu_bf, wd_bf):
    i = pl.program_id(0)
    active = i < nu_ref[0]

    @pl.when(jnp.logical_and(active, jnp.logical_or(i == 0, te_ref[i] != te_ref[jnp.maximum(i - 1, 0)])))
    def _():
        wg_bf[...] = _bf(wg_ref[0])
        wu_bf[...] = _bf(wu_ref[0])
        wd_bf[...] = _bf(wd_ref[0])

    @pl.when(active)
    def _():
        x = _unpack_pairs(xs_ref[...])
        act = _glu(x, wg_bf[...], wu_bf[...])
        ys_ref[...] = _pack_pairs(jnp.dot(_bf(act), wd_bf[...], preferred_element_type=F32))


def _experts(xs, tile_expert, n_used, mp, layer):
    n_tiles = xs.shape[0] // MOE_TILE
    half = D_MODEL // 2
    wspec = lambda shape: pl.BlockSpec((None, 1) + shape, lambda i, te, nu: (layer, te[i], 0, 0))
    return pl.pallas_call(
        _experts_kernel,
        grid_spec=pltpu.PrefetchScalarGridSpec(
            num_scalar_prefetch=2, grid=(n_tiles,),
            in_specs=[pl.BlockSpec((MOE_TILE, half), lambda i, te, nu: (jnp.minimum(i, nu[0] - 1), 0)),
                      wspec((D_MODEL, D_EXPERT)), wspec((D_MODEL, D_EXPERT)), wspec((D_EXPERT, D_MODEL))],
            out_specs=pl.BlockSpec((MOE_TILE, half), lambda i, te, nu: (jnp.minimum(i, nu[0] - 1), 0)),
            scratch_shapes=[pltpu.VMEM((D_MODEL, D_EXPERT), BF16), pltpu.VMEM((D_MODEL, D_EXPERT), BF16),
                            pltpu.VMEM((D_EXPERT, D_MODEL), BF16)]),
        out_shape=jax.ShapeDtypeStruct(xs.shape, jnp.uint32),
        compiler_params=_cparams(1),
        name="experts",
    )(tile_expert, n_used, xs, mp['wg'], mp['wu'], mp['wd'])


def _shared_kernel(h_ref, sg_ref, su_ref, sd_ref, x1_ref, g2_ref, o_ref):
    act = _glu(h_ref[...], sg_ref[...], su_ref[...])
    o_ref[...] = x1_ref[...] + g2_ref[...] * jnp.dot(_bf(act), sd_ref[...], preferred_element_type=F32)


def _shared_base(rows, hffn, mp, x1, mod4, layer):
    const = lambda a: pl.BlockSpec(a.shape, lambda i: (0,) * a.ndim)
    tok = pl.BlockSpec((TM, D_MODEL), lambda i: (i, 0))
    return pl.pallas_call(
        _shared_kernel,
        grid=(rows.n // TM,),
        in_specs=[tok, const(mp['sg']), const(mp['su']), const(mp['sd']), tok, rows.mod_spec(layer, 5, TM)],
        out_specs=tok,
        out_shape=jax.ShapeDtypeStruct((rows.n, D_MODEL), F32),
        compiler_params=_cparams(1),
        name=f"shared{layer}",
    )(hffn, mp['sg'], mp['su'], mp['sd'], x1, mod4)


def _combine_kernel(base_ref, *refs):
    yg_refs = refs[:TOP_K]
    ew_ref, eye_ref, g2_ref, o_ref = refs[TOP_K:]
    ew = ew_ref[...]
    hi = _bf(ew)
    lo = _bf(ew - hi.astype(F32))
    ew_t = _dot_tn(hi, eye_ref[...]) + _dot_tn(lo, eye_ref[...])
    acc = ew_t[:, 0:1] * _unpack_pairs(yg_refs[0][...]).astype(F32)
    for k in range(1, TOP_K):
        acc = acc + ew_t[:, k:k + 1] * _unpack_pairs(yg_refs[k][...]).astype(F32)
    o_ref[...] = base_ref[...] + g2_ref[...] * acc


def _combine(rows, base, yg, ew, mod4, layer, row0, n_out):
    half = D_MODEL // 2
    n_blk = rows.n // TM
    blk0 = row0 // TM
    const = lambda a: pl.BlockSpec(a.shape, lambda i: (0,) * a.ndim)
    tok = lambda w: pl.BlockSpec((TM, w), lambda i: (blk0 + i, 0))
    slot = lambda k: pl.BlockSpec((TM, half), lambda i: (k * n_blk + blk0 + i, 0))
    eye = jnp.eye(TOP_K, dtype=BF16)
    return pl.pallas_call(
        _combine_kernel,
        grid=(n_out // TM,),
        in_specs=[tok(D_MODEL)] + [slot(k) for k in range(TOP_K)]
                 + [pl.BlockSpec((TOP_K, TM), lambda i: (0, blk0 + i)), const(eye),
                    rows.mod_spec(layer, 5, TM, blk0)],
        out_specs=pl.BlockSpec((TM, D_MODEL), lambda i: (i, 0)),
        out_shape=jax.ShapeDtypeStruct((n_out, D_MODEL), F32),
        compiler_params=_cparams(1),
        name=f"combine{layer}",
    )(base, *([yg] * TOP_K), ew, eye, mod4)


def _moe(rows, hffn, hpack, x1, router, bias, mp, mod4, layer, out_ranges):
    n = rows.n
    sel, eidx, ew, cnt = _router(hffn, router, bias)
    counts = cnt[:, 0].astype(jnp.int32)
    padded = (counts + MOE_TILE - 1) // MOE_TILE * MOE_TILE
    ends = jnp.cumsum(padded)
    n_rows = n * TOP_K + N_EXPERTS * MOE_TILE
    n_tiles = n_rows // MOE_TILE
    base = jnp.broadcast_to((ends - padded).astype(F32)[:, None], (N_EXPERTS, LANES))
    tile_start = jnp.arange(n_tiles, dtype=jnp.int32) * MOE_TILE
    tile_expert = jnp.minimum(jnp.sum((ends[None, :] <= tile_start[:, None]).astype(jnp.int32), axis=1),
                              N_EXPERTS - 1)
    n_used = (ends[-1:] // MOE_TILE).astype(jnp.int32)
    pos = _positions(sel, eidx, base)
    pos3 = pos.reshape(TOP_K, n // SC_CHUNK, SC_CHUNK).transpose(1, 0, 2)
    xs = _sc_scatter(hpack, pos3, n_rows)
    shared = _shared_base(rows, hffn, mp, x1, mod4, layer)
    ys = _experts(xs, tile_expert, n_used, mp, layer)
    yg = _sc_gather(ys, pos.reshape(-1))
    return [_combine(rows, shared, yg, ew, mod4, layer, row0, n_out) for row0, n_out in out_ranges]


def _glu(x, wg, wu):
    hg = jnp.dot(x, wg, preferred_element_type=F32)
    hu = jnp.dot(x, wu, preferred_element_type=F32)
    return _silu(hg) * hu


def kernel(x_prompt, x_sample, c, c_ctx, cache_attn_k, cache_attn_v, state_rwkv_fwd, state_rwkv_bwd,
           state_hgrn_fwd, state_hgrn_bwd, norm1_g, norm2_g, mod_w, mod_b, ab_w_in, ab_w_out, attn_q_norm,
           attn_k_norm, attn_sink, rwkv_mu, rwkv_w0, rwkv_w2, rwkv_a0, rwkv_a2, rwkv_g2, rwkv_k_k, rwkv_k_a,
           rwkv_r_k, rwkv_ln_w, rwkv_ln_b, hgrn_w_in, hgrn_w_out, hgrn_lower_bounds, hgrn_norm_g, moe_router,
           moe_bias, moe_w_gate, moe_w_up, moe_w_down, moe_shared_gate, moe_shared_up, moe_shared_down):
    n_cseq, cseq, _ = x_prompt.shape
    n_lseq, lseq, _ = x_sample.shape
    depth = mod_w.shape[0]
    assert depth == 2 and n_lseq + 1 <= SUBLANES
    assert cseq == PREP_TM and lseq % TM == 0 and lseq % HG_TB == 0 and cseq % HG_TB == 0
    assert (n_cseq * cseq) % TM == 0
    assert n_cseq % RW_BB == 0 and n_lseq % RW_BB == 0 and (n_cseq * cseq) % (lseq * RW_BB) == 0
    rows = _Rows(n_cseq * cseq, n_lseq * lseq, lseq)
    assert rows.n % MOE_TILE == 0 and lseq % MOE_TILE == 0 and rows.n_ctx % MOE_TILE == 0
    kv_w = KV_A * HD_A

    xs = (x_prompt.reshape(rows.n_ctx, D_MODEL), x_sample.reshape(rows.n_lat, D_MODEL))
    cvecs = jnp.concatenate([c_ctx[None, :], c, jnp.zeros((SUBLANES - 1 - n_lseq, D_MODEL), F32)], axis=0)
    mod4 = _modulation(cvecs, mod_w, mod_b).reshape(depth, SUBLANES, 1, 6 * D_MODEL)

    ones_q = _block_ones(W_A, HD_A)
    ones_k = _block_ones(kv_w, HD_A)
    ones_b = _block_ones(W_B, HD_B)
    ones_pair = _block_ones(LANES, HD_B)[:LANES]
    ones_pair = jnp.kron(jnp.eye(2, dtype=BF16), ones_pair)
    cos_t, sin_t = _rope_tables(lseq)

    def moe(l, hffn, hpack, x1, out_ranges):
        mp = {'wg': moe_w_gate, 'wu': moe_w_up, 'wd': moe_w_down,
              'sg': _bf(moe_shared_gate[l]), 'su': _bf(moe_shared_up[l]), 'sd': _bf(moe_shared_down[l])}
        return _moe(rows, hffn, hpack, x1, moe_router[l], moe_bias[l], mp, mod4, l, out_ranges)

    assert W_A == W_B
    all_rows = jnp.zeros((rows.n, W_B), F32)

    pr = {'mu': rwkv_mu[0], 'w0': rwkv_w0[0], 'w2': rwkv_w2[0], 'a0': rwkv_a0[0], 'a2': rwkv_a2[0],
          'g2': rwkv_g2[0], 'k_k': rwkv_k_k[0], 'k_a': rwkv_k_a[0], 'r_k': rwkv_r_k[0].reshape(-1),
          'ln_w': rwkv_ln_w[0], 'ln_b': rwkv_ln_b[0]}
    p_att, p_rw = _inproj(rows, xs, norm1_g[0], mod4, 0, _bf(ab_w_in[0]), (ATT_IN, RWKV_IN), TM)
    qg_t = jnp.tile(attn_q_norm[0], H_A).reshape(1, W_A)
    kg_t = jnp.tile(attn_k_norm[0], KV_A).reshape(1, kv_w)
    o_att, new_k, new_v = _ctx_attention(p_att, n_cseq, cseq, qg_t, kg_t, attn_sink[0], ones_q, ones_k, all_rows)
    past = cache_attn_k.shape[2]
    o_att = _lat_attention(p_att, rows.n_ctx // lseq, n_lseq, lseq, qg_t, kg_t, attn_sink[0], ones_q, ones_k,
                           cos_t, sin_t, cache_attn_k[:, 0].reshape(n_lseq, past, kv_w),
                           cache_attn_v[:, 0].reshape(n_lseq, past, kv_w), o_att)

    pp = _rwkv_prep(rows, p_rw, pr, ones_b)
    zero_st = jnp.zeros((n_cseq, H_B // 2, HD_B, LANES), F32)
    o_f, o_b, sf_c, sb_c = _rwkv_scan(pp, 0, n_cseq, cseq, zero_st, zero_st, ones_pair, all_rows, all_rows)
    o_f, o_b, _, _ = _rwkv_scan(pp, rows.n_ctx, n_lseq, lseq, _state_to_pairs(state_rwkv_fwd[:, 0]),
                                _state_to_pairs(state_rwkv_bwd[:, 0]), ones_pair, o_f, o_b)
    x1, hffn, hpack = _outproj0(rows, xs, o_att, o_f, o_b, pp, pr, ones_b, _bf(ab_w_out[0]), norm2_g[0], mod4, 0)
    (x,) = moe(0, hffn, hpack, x1, [(0, rows.n)])

    (p1,) = _inproj(rows, (x, x), norm1_g[1], mod4, 1, _bf(hgrn_w_in[0]), (IN_C,), IN1_TM)
    zero_h = jnp.zeros((n_cseq, H_C, DK_C, DV_C), F32)
    all_rows_c = jnp.zeros((rows.n, D_C), F32)
    h_f, h_b, hsf_c, hsb_c = _hgrn_scan(p1, 0, n_cseq, cseq, hgrn_lower_bounds, zero_h, zero_h,
                                        all_rows_c, all_rows_c)
    h_f, h_b, _, _ = _hgrn_scan(p1, rows.n_ctx, n_lseq, lseq, hgrn_lower_bounds,
                                state_hgrn_fwd[:, 0], state_hgrn_bwd[:, 0], h_f, h_b)
    x1, hffn, hpack = _outproj1(rows, (x, x), h_f, h_b, p1, hgrn_norm_g[0], _bf(hgrn_w_out[0]), norm2_g[1],
                                mod4, 1)
    y_c, y_l = moe(1, hffn, hpack, x1, [(0, rows.n_ctx), (rows.n_ctx, rows.n_lat)])

    y_prompt = y_c.reshape(n_cseq, cseq, D_MODEL)
    y_sample = y_l.reshape(n_lseq, lseq, D_MODEL)
    return (y_prompt, y_sample,
            new_k.reshape(n_cseq, 1, cseq, KV_A, HD_A), new_v.reshape(n_cseq, 1, cseq, KV_A, HD_A),
            _pairs_to_state(sf_c)[:, None], _pairs_to_state(sb_c)[:, None],
            hsf_c[:, None], hsb_c[:, None])
```
